```python
import math
import jax, jax.numpy as jnp
from jax import lax
import numpy as np

D_MODEL = 1024
BATCH = 4
SEQ = 4096
DEPTH = 4

GRID_W = 64
CTX_LEN = 256
N_BRANCH = 4
BRANCH_W = 512
RET_HEADS = 4
RET_DK = 128
RET_DV = 128
RET_CHUNK = 128
HY_ORDER = 2
HY_EMB = 33
HY_HID = 64
HY_FAST_PCT = 0.3
HY_SLOW_PCT = 1.5
HY_TARGET = 1e-2
ATT_HEADS = 4
ATT_KV_HEADS = 2
HEAD_DIM = 128
Q_BLOCK = 128
ROPE_THETA = 10000.0
FF_DENSE = 2816
N_EXPERTS = 8
TOP_K = 2
FF_EXPERT = 3584
MOE_BLOCK = 256
N_MOD = 6
ALPHA = (2 * DEPTH) ** 0.25
BETA = (8 * DEPTH) ** -0.25

A_W = 3 * BRANCH_W
R_W = 2 * RET_HEADS * RET_DK + 2 * RET_HEADS * RET_DV
H_W = (HY_ORDER + 1) * BRANCH_W
D_W = (ATT_HEADS + 2 * ATT_KV_HEADS) * HEAD_DIM
IN_W = A_W + R_W + H_W + D_W
IN_SPLITS = (A_W, A_W + R_W, A_W + R_W + H_W)
HY_FILT_OUT = HY_ORDER * 2 * BRANCH_W

kernel_name = "hybrid_dit_gated_branch_block"


def layer_norm(x, g, b, eps=1e-6):
    xf = x.astype(jnp.float32)
    mu = jnp.mean(xf, -1, keepdims=True)
    var = jnp.mean(jnp.square(xf - mu), -1, keepdims=True)
    return ((xf - mu) * lax.rsqrt(var + eps) * g + b).astype(x.dtype)


def rms_norm(x, g, eps=1e-6):
    xf = x.astype(jnp.float32)
    return (xf * lax.rsqrt(jnp.mean(xf * xf, -1, keepdims=True) + eps) * g).astype(x.dtype)


def head_norm(o, eps=1e-6):
    mu = jnp.mean(o, -1, keepdims=True)
    var = jnp.mean(jnp.square(o - mu), -1, keepdims=True)
    return (o - mu) * lax.rsqrt(var + eps)


def heads(z, n):
    b, l, _ = z.shape
    return z.reshape(b, l, n, -1).transpose(0, 2, 1, 3)


def merge_heads(z):
    b, n, l, d = z.shape
    return z.transpose(0, 2, 1, 3).reshape(b, l, n * d)


def flip(a):
    return a[:, :, ::-1]


def dwconv3(x, w, b):
    xp = jnp.pad(x, ((0, 0), (1, 1), (0, 0)))
    return xp[:, :-2] * w[0] + xp[:, 1:-1] * w[1] + xp[:, 2:] * w[2] + b


def axial_rope_tables(rows):
    row = jnp.repeat(jnp.arange(rows, dtype=jnp.float32), GRID_W)
    col = jnp.tile(jnp.arange(GRID_W, dtype=jnp.float32), rows)
    n_freq = HEAD_DIM // 4
    inv = ROPE_THETA ** (-jnp.arange(n_freq, dtype=jnp.float32) / n_freq)
    ang = jnp.stack([row[:, None] * inv, col[:, None] * inv], axis=1)
    return jnp.cos(ang), jnp.sin(ang)


def apply_rope(x, cos, sin):
    b, h, l, _ = x.shape
    xr = x.astype(jnp.float32).reshape(b, h, l, 2, 2, HEAD_DIM // 4)
    x1, x2 = xr[..., 0, :], xr[..., 1, :]
    o1 = x1 * cos - x2 * sin
    o2 = x2 * cos + x1 * sin
    return jnp.stack([o1, o2], axis=-2).reshape(x.shape).astype(x.dtype)


def short_conv_mixer(z, w, b):
    bg, cg, xv = jnp.split(z, 3, axis=-1)
    return bg * dwconv3(cg * xv, w, b)


def retention_scan(q, k, v, log_gamma, state0):
    b, h, l, _ = q.shape
    n = l // RET_CHUNK
    i = jnp.arange(RET_CHUNK, dtype=jnp.float32)
    lg = log_gamma.astype(jnp.float32)[:, None]
    rel = i[:, None] - i[None, :]
    intra = jnp.where(rel >= 0, jnp.exp(lg[..., None] * jnp.maximum(rel, 0.0)), 0.0)
    q_dec = jnp.exp(lg * (i + 1.0))[:, :, None]
    k_dec = jnp.exp(lg * (RET_CHUNK - 1.0 - i))[:, :, None]
    c_dec = jnp.exp(lg[:, 0] * RET_CHUNK)[:, None, None]

    def to_chunks(a):
        return a.reshape(b, h, n, RET_CHUNK, a.shape[-1]).transpose(2, 0, 1, 3, 4)

    def step(state, inp):
        qc, kc, vc = inp
        s = jnp.einsum('bhid,bhjd->bhij', qc, kc) * intra
        o = jnp.einsum('bhij,bhjv->bhiv', s, vc) + jnp.einsum('bhid,bhdv->bhiv', qc * q_dec, state)
        state = c_dec * state + jnp.einsum('bhjd,bhjv->bhdv', kc * k_dec, vc)
        return state, o

    state, o = lax.scan(step, state0, (to_chunks(q), to_chunks(k), to_chunks(v)))
    return o.transpose(1, 2, 0, 3, 4).reshape(b, h, l, v.shape[-1]), state


def retention_out(o, g):
    return jax.nn.silu(g) * merge_heads(head_norm(o)).astype(g.dtype)


def retention_mixer(z, zc, cos, sin, log_gamma, need_ctx):
    hk, hv = RET_HEADS * RET_DK, RET_HEADS * RET_DV

    def prep(zz, rotate):
        q, k, v, g = jnp.split(zz, (hk, 2 * hk, 2 * hk + hv), axis=-1)
        q = heads(q, RET_HEADS).astype(jnp.float32)
        k = heads(k, RET_HEADS).astype(jnp.float32) * RET_DK ** -0.5
        v = heads(v, RET_HEADS).astype(jnp.float32)
        if rotate:
            q, k = apply_rope(q, cos, sin), apply_rope(k, cos, sin)
        return q, k, v, g

    q, k, v, g = prep(z, True)
    qc, kc, vc, gc = prep(zc, False)
    zero = jnp.zeros((z.shape[0], RET_HEADS, RET_DK, RET_DV), jnp.float32)
    oc_f, st_f = retention_scan(qc, kc, vc, log_gamma[0], zero)
    oc_b, st_b = retention_scan(flip(qc), flip(kc), flip(vc), log_gamma[1], zero)
    o_f, _ = retention_scan(q, k, v, log_gamma[0], st_f)
    o_b, _ = retention_scan(flip(q), flip(k), flip(v), log_gamma[1], st_b)
    y = retention_out(o_f + flip(o_b), g)
    yc = retention_out(oc_f + flip(oc_b), gc) if need_ctx else None
    return y, yc


def hyena_filters(l, w1, b1, w2, b2, w3, freq):
    t = jnp.linspace(0.0, 1.0, l, dtype=jnp.float32)[:, None]
    bands = (HY_EMB - 1) // 2
    w = 2.0 * math.pi * jnp.arange(l, dtype=jnp.float32) / l
    f = jnp.linspace(1e-4, bands - 1, bands, dtype=jnp.float32)
    ang = w[:, None] * f[None, :]
    feats = jnp.concatenate([t, jnp.cos(ang), -jnp.sin(ang)], axis=-1)
    hdn = jnp.sin(freq[0] * (feats @ w1 + b1))
    hdn = jnp.sin(freq[1] * (hdn @ w2 + b2))
    filt = (hdn @ w3).astype(jnp.float32)
    max_decay = math.log(HY_TARGET) / HY_FAST_PCT
    min_decay = math.log(HY_TARGET) / HY_SLOW_PCT
    deltas = jnp.linspace(min_decay, max_decay, HY_FILT_OUT, dtype=jnp.float32)
    filt = filt * jnp.exp(-t * jnp.abs(deltas))
    return filt.reshape(l, HY_ORDER, 2, BRANCH_W).transpose(1, 2, 0, 3)


def bidir_fftconv(u, hf, hb):
    l = u.shape[1]
    filt = jnp.concatenate([hf, jnp.zeros_like(hf[:1]), hb[:0:-1]], axis=0)
    ff = jnp.fft.rfft(filt, axis=0)
    uf = jnp.fft.rfft(u, n=2 * l, axis=1)
    return jnp.fft.irfft(uf * ff, n=2 * l, axis=1)[:, :l]


def hyena_mixer(z, conv_w, conv_b, filt, skip):
    z = dwconv3(z, conv_w, conv_b)
    v, x1, x2 = jnp.split(z, 3, axis=-1)
    y = v.astype(jnp.float32)
    for o, gate in enumerate((x1, x2)):
        y = gate.astype(jnp.float32) * (bidir_fftconv(y, filt[o, 0], filt[o, 1]) + skip[o] * y)
    return y.astype(z.dtype)


def attention_mixer(z, zc, cos, sin, gq, gk, need_ctx):
    qw, kw = ATT_HEADS * HEAD_DIM, ATT_KV_HEADS * HEAD_DIM
    grp = ATT_HEADS // ATT_KV_HEADS
    scale = HEAD_DIM ** -0.5

    def prep(zz):
        q, k, v = jnp.split(zz, (qw, qw + kw), axis=-1)
        q = rms_norm(heads(q, ATT_HEADS), gq)
        k = rms_norm(heads(k, ATT_KV_HEADS), gk)
        return q, k, heads(v, ATT_KV_HEADS)

    def group(a):
        b, _, l, d = a.shape
        return a.reshape(b, ATT_KV_HEADS, grp, l, d)

    def attend(qg, keys, vals):
        s = jnp.einsum('bkgqd,bksd->bkgqs', qg, keys).astype(jnp.float32) * scale
        p = jax.nn.softmax(s, axis=-1).astype(vals.dtype)
        return jnp.einsum('bkgqs,bksd->bkgqd', p, vals)

    q, k, v = prep(z)
    qc, kc, vc = prep(zc)
    q, k = apply_rope(q, cos, sin), apply_rope(k, cos, sin)
    b, _, l, _ = q.shape
    nb = l // Q_BLOCK
    k_all = jnp.concatenate([k, kc], axis=2)
    v_all = jnp.concatenate([v, vc], axis=2)
    qb = group(q).reshape(b, ATT_KV_HEADS, grp, nb, Q_BLOCK, HEAD_DIM).transpose(3, 0, 1, 2, 4, 5)
    ob = lax.map(lambda qi: attend(qi, k_all, v_all), qb)
    y = merge_heads(ob.transpose(1, 2, 3, 0, 4, 5).reshape(b, ATT_HEADS, l, HEAD_DIM))
    if not need_ctx:
        return y, None
    oc = attend(group(qc), kc, vc)
    yc = merge_heads(oc.reshape(b, ATT_HEADS, zc.shape[1], HEAD_DIM))
    return y, yc


def merge_branches(u, ys, w_gate, b_gate, w_branch, w_o):
    m = 0.0
    for i, y in enumerate(ys):
        m = m + jax.nn.sigmoid(u @ w_gate[i] + b_gate[i]) * (y @ w_branch[i])
    return m @ w_o


def swiglu(u, w1, w3, w2):
    return (jax.nn.silu(u @ w1) * (u @ w3)) @ w2


def moe_ffn(t, w_r, w1, w3, w2):
    n_tok = t.shape[0]
    n_pair = n_tok * TOP_K
    logits = (t @ w_r).astype(jnp.float32)
    top_v, top_i = lax.top_k(logits, TOP_K)
    wts = jax.nn.softmax(top_v, axis=-1)
    e_flat = top_i.reshape(-1)
    tok = jnp.repeat(jnp.arange(n_tok, dtype=jnp.int32), TOP_K)
    order = jnp.argsort(e_flat)
    e_s, tok_s, w_s = e_flat[order], tok[order], wts.reshape(-1)[order]
    counts = jnp.bincount(e_flat, length=N_EXPERTS)
    padded = (counts + MOE_BLOCK - 1) // MOE_BLOCK * MOE_BLOCK
    start = jnp.cumsum(counts) - counts
    pad_end = jnp.cumsum(padded)
    pad_start = pad_end - padded
    dest = pad_start[e_s] + jnp.arange(n_pair, dtype=jnp.int32) - start[e_s]
    n_blk = -(-n_pair // MOE_BLOCK) + N_EXPERTS
    row_tok = jnp.zeros((n_blk * MOE_BLOCK,), jnp.int32).at[dest].set(tok_s)
    blk_e = jnp.minimum(jnp.searchsorted(pad_end, jnp.arange(n_blk) * MOE_BLOCK, side='right'), N_EXPERTS - 1)
    xb = t[row_tok].reshape(n_blk, MOE_BLOCK, t.shape[-1])
    yb = lax.map(lambda a: swiglu(a[0], w1[a[1]], w3[a[1]], w2[a[1]]), (xb, blk_e))
    yb = yb.reshape(n_blk * MOE_BLOCK, t.shape[-1])
    return jnp.zeros_like(t).at[tok_s].add(yb[dest] * w_s[:, None].astype(t.dtype))


def setup_inputs(seed: int = 0) -> dict:
    key = jax.random.key(seed)
    ks = iter(jax.random.split(key, 48))
    D = D_MODEL
    n_dense, n_moe = (DEPTH + 1) // 2, DEPTH // 2

    def nrm(shape, s):
        return jax.random.normal(next(ks), shape, jnp.float32) * s

    base = np.log(-np.log(1.0 - 2.0 ** (-5.0 - np.arange(RET_HEADS))))
    return {
        "x": nrm((BATCH, SEQ, D), 1.0),
        "c": nrm((BATCH, D), 1.0),
        "ctx": nrm((BATCH, CTX_LEN, D), 1.0),
        "c_ctx": nrm((D,), 1.0),
        "w_mod": nrm((DEPTH, D, N_MOD * D), D ** -0.5),
        "b_mod": nrm((DEPTH, N_MOD * D), 0.02),
        "w_in": nrm((DEPTH, D, IN_W), D ** -0.5),
        "conv_a_w": nrm((DEPTH, 3, BRANCH_W), 3 ** -0.5),
        "conv_a_b": nrm((DEPTH, BRANCH_W), 0.02),
        "ret_decay": jnp.asarray(base, jnp.float32) + nrm((DEPTH, 2, RET_HEADS), 0.1),
        "hy_conv_w": nrm((DEPTH, 3, H_W), 3 ** -0.5),
        "hy_conv_b": nrm((DEPTH, H_W), 0.02),
        "hy_w1": nrm((DEPTH, HY_EMB, HY_HID), HY_EMB ** -0.5),
        "hy_b1": nrm((DEPTH, HY_HID), 0.02),
        "hy_w2": nrm((DEPTH, HY_HID, HY_HID), HY_HID ** -0.5),
        "hy_b2": nrm((DEPTH, HY_HID), 0.02),
        "hy_w3": nrm((DEPTH, HY_HID, HY_FILT_OUT), 0.03 * HY_HID ** -0.5),
        "hy_freq": 1.0 + nrm((DEPTH, 2, HY_HID), 0.05),
        "hy_skip": nrm((DEPTH, HY_ORDER, BRANCH_W), 0.5),
        "q_norm": 1.0 + nrm((DEPTH, HEAD_DIM), 0.02),
        "k_norm": 1.0 + nrm((DEPTH, HEAD_DIM), 0.02),
        "w_gate": nrm((DEPTH, N_BRANCH, D, D), D ** -0.5),
        "b_gate": nrm((DEPTH, N_BRANCH, D), 0.02),
        "w_branch": nrm((DEPTH, N_BRANCH, BRANCH_W, D), BRANCH_W ** -0.5 * BETA),
        "w_o": nrm((DEPTH, D, D), D ** -0.5 * BETA),
        "ln_g": 1.0 + nrm((DEPTH, 2, D), 0.02),
        "ln_b": nrm((DEPTH, 2, D), 0.02),
        "ffn_w1": nrm((n_dense, D, FF_DENSE), D ** -0.5),
        "ffn_w3": nrm((n_dense, D, FF_DENSE), D ** -0.5),
        "ffn_w2": nrm((n_dense, FF_DENSE, D), FF_DENSE ** -0.5 * BETA),
        "router": nrm((n_moe, D, N_EXPERTS), D ** -0.5),
        "moe_w1": nrm((n_moe, N_EXPERTS, D, FF_EXPERT), D ** -0.5),
        "moe_w3": nrm((n_moe, N_EXPERTS, D, FF_EXPERT), D ** -0.5),
        "moe_w2": nrm((n_moe, N_EXPERTS, FF_EXPERT, D), FF_EXPERT ** -0.5 * BETA),
    }


def reference(x, c, ctx, c_ctx, w_mod, b_mod, w_in, conv_a_w, conv_a_b, ret_decay,
              hy_conv_w, hy_conv_b, hy_w1, hy_b1, hy_w2, hy_b2, hy_w3, hy_freq, hy_skip,
              q_norm, k_norm, w_gate, b_gate, w_branch, w_o, ln_g, ln_b,
              ffn_w1, ffn_w3, ffn_w2, router, moe_w1, moe_w3, moe_w2):
    n_lat, n_ctx = x.shape[1], ctx.shape[1]
    rows = n_lat // GRID_W
    cos, sin = axial_rope_tables(rows)
    c_act = jax.nn.silu(c)[:, None, :]
    cc_act = jax.nn.silu(c_ctx)
    h = ctx
    for l in range(DEPTH):
        need_ctx = l < DEPTH - 1
        mod = jnp.split(c_act @ w_mod[l] + b_mod[l], N_MOD, axis=-1)
        mod_c = jnp.split(cc_act @ w_mod[l] + b_mod[l], N_MOD, axis=-1)

        u = x * (1.0 + mod[1]) + mod[0]
        uc = h * (1.0 + mod_c[1]) + mod_c[0]
        zA, zR, zH, zD = jnp.split(u @ w_in[l], IN_SPLITS, axis=-1)
        cA, cR, cH, cD = jnp.split(uc @ w_in[l], IN_SPLITS, axis=-1)
        log_gamma = -jnp.exp(ret_decay[l].astype(jnp.float32))
        yR, yRc = retention_mixer(zR, cR, cos, sin, log_gamma, need_ctx)
        yD, yDc = attention_mixer(zD, cD, cos, sin, q_norm[l], k_norm[l], need_ctx)
        yA = short_conv_mixer(zA, conv_a_w[l], conv_a_b[l])
        filt = hyena_filters(n_lat, hy_w1[l], hy_b1[l], hy_w2[l], hy_b2[l], hy_w3[l], hy_freq[l])
        yH = hyena_mixer(zH, hy_conv_w[l], hy_conv_b[l], filt, hy_skip[l])
        mix = merge_branches(u, (yA, yR, yH, yD), w_gate[l], b_gate[l], w_branch[l], w_o[l])
        x = layer_norm(ALPHA * x + mod[2] * mix, ln_g[l, 0], ln_b[l, 0])
        if need_ctx:
            yAc = short_conv_mixer(cA, conv_a_w[l], conv_a_b[l])
            filt_c = hyena_filters(n_ctx, hy_w1[l], hy_b1[l], hy_w2[l], hy_b2[l], hy_w3[l], hy_freq[l])
            yHc = hyena_mixer(cH, hy_conv_w[l], hy_conv_b[l], filt_c, hy_skip[l])
            mix_c = merge_branches(uc, (yAc, yRc, yHc, yDc), w_gate[l], b_gate[l], w_branch[l], w_o[l])
            h = layer_norm(ALPHA * h + mod_c[2] * mix_c, ln_g[l, 0], ln_b[l, 0])

        u = x * (1.0 + mod[4]) + mod[3]
        i = l // 2
        if l % 2 == 0:
            f = swiglu(u, ffn_w1[i], ffn_w3[i], ffn_w2[i])
            if need_ctx:
                uc = h * (1.0 + mod_c[4]) + mod_c[3]
                fc = swiglu(uc, ffn_w1[i], ffn_w3[i], ffn_w2[i])
        else:
            flat = u.reshape(-1, D_MODEL)
            if need_ctx:
                uc = h * (1.0 + mod_c[4]) + mod_c[3]
                flat = jnp.concatenate([flat, uc.reshape(-1, D_MODEL)], axis=0)
            fo = moe_ffn(flat, router[i], moe_w1[i], moe_w3[i], moe_w2[i])
            n_u = u.shape[0] * n_lat
            f = fo[:n_u].reshape(u.shape)
            if need_ctx:
                fc = fo[n_u:].reshape(h.shape)
        x = layer_norm(ALPHA * x + mod[5] * f, ln_g[l, 1], ln_b[l, 1])
        if need_ctx:
            h = layer_norm(ALPHA * h + mod_c[5] * fc, ln_g[l, 1], ln_b[l, 1])
    return x
```

```python
import functools
import math

import numpy as np
import jax
import jax.numpy as jnp
from jax import lax
from jax.experimental import pallas as pl
from jax.experimental.pallas import tpu as pltpu

F32 = jnp.float32
BF16 = jnp.bfloat16

GRID_W = 64
BRANCH_W = 512
RET_HEADS = 4
RET_D = 128
RET_CHUNK = 128
HY_EMB = 33
HY_FAST_PCT = 0.3
HY_SLOW_PCT = 1.5
HY_TARGET = 1e-2
ATT_HEADS = 4
ATT_KV_HEADS = 2
HEAD_DIM = 128
ROPE_THETA = 10000.0
N_EXPERTS = 8
TOP_K = 2
N_MOD = 6
A_W = 3 * BRANCH_W
R_W = 4 * RET_HEADS * RET_D
H_W = 3 * BRANCH_W
D_W = (ATT_HEADS + 2 * ATT_KV_HEADS) * HEAD_DIM
IN_W = A_W + R_W + H_W + D_W
LN_EPS = 1e-6

SUB = 256
LANE = 128
FFT_N1 = 128
VMEM_LIMIT = 56 * 1024 * 1024


def _cp(sem, vmem=VMEM_LIMIT):
    return pltpu.CompilerParams(dimension_semantics=sem, vmem_limit_bytes=vmem)


def _dot(a, b):
    return jnp.dot(a, b, preferred_element_type=F32)


def _dot_nt(a, b):
    return lax.dot_general(a, b, (((1,), (1,)), ((), ())), preferred_element_type=F32)


def _dot_tn(a, b):
    return lax.dot_general(a, b, (((0,), (0,)), ((), ())), preferred_element_type=F32)


def _split_dot(f_cat, x):
    m = f_cat.shape[0] // 2
    xh = x.astype(BF16)
    xl = (x - xh.astype(F32)).astype(BF16)
    r = _dot(f_cat, xh)
    return r[:m] + r[m:] + _dot(f_cat[:m], xl)


def _modulate(x, mod_ref, shift_row, scale_row):
    parts = []
    for s in range(x.shape[0] // SUB):
        sh = mod_ref[s, shift_row:shift_row + 1, :]
        sc = mod_ref[s, scale_row:scale_row + 1, :]
        parts.append(x[s * SUB:(s + 1) * SUB] * (1.0 + sc) + sh)
    return parts[0] if len(parts) == 1 else jnp.concatenate(parts, axis=0)


def _gated_residual_ln(x, f, mod_ref, gate_row, alpha, g, b):
    parts = []
    for s in range(x.shape[0] // SUB):
        gt = mod_ref[s, gate_row:gate_row + 1, :]
        parts.append(alpha * x[s * SUB:(s + 1) * SUB] + gt * f[s * SUB:(s + 1) * SUB])
    r = parts[0] if len(parts) == 1 else jnp.concatenate(parts, axis=0)
    mu = jnp.mean(r, axis=-1, keepdims=True)
    rc = r - mu
    var = jnp.mean(rc * rc, axis=-1, keepdims=True)
    return rc * lax.rsqrt(var + LN_EPS) * g + b


def _mod_kernel(a_ref, w_ref, b_ref, o_ref):
    o_ref[0] = _dot(a_ref[...].astype(BF16), w_ref[0].astype(BF16)) + b_ref[0]


def _mod_all(act, w_mod, b_mod):
    depth, d, n = w_mod.shape
    tn = 512
    return pl.pallas_call(
        _mod_kernel,
        grid=(depth, n // tn),
        in_specs=[pl.BlockSpec((8, d), lambda l, j: (0, 0)),
                  pl.BlockSpec((1, d, tn), lambda l, j: (l, 0, j)),
                  pl.BlockSpec((1, 1, tn), lambda l, j: (l, 0, j))],
        out_specs=pl.BlockSpec((1, 8, tn), lambda l, j: (l, 0, j)),
        out_shape=jax.ShapeDtypeStruct((depth, 8, n), F32),
        compiler_params=_cp(("parallel", "parallel")),
        name="mod_vectors",
    )(act, w_mod, b_mod.reshape(depth, 1, n))


def _proj_kernel(x_ref, mod_ref, w_ref, o_ref, u_ref):
    @pl.when(pl.program_id(1) == 0)
    def _():
        u_ref[...] = _modulate(x_ref[...], mod_ref, 0, 1).astype(BF16)
    o_ref[...] = _dot(u_ref[...], w_ref[...])


def _proj(x, modx, w, tm, tn):
    m, d = x.shape
    n = w.shape[1]
    return pl.pallas_call(
        _proj_kernel,
        grid=(m // tm, n // tn),
        in_specs=[pl.BlockSpec((tm, d), lambda i, j: (i, 0)),
                  pl.BlockSpec((tm // SUB, 8, d), lambda i, j: (i, 0, 0)),
                  pl.BlockSpec((d, tn), lambda i, j: (0, j))],
        out_specs=pl.BlockSpec((tm, tn), lambda i, j: (i, j)),
        out_shape=jax.ShapeDtypeStruct((m, n), F32),
        scratch_shapes=[pltpu.VMEM((tm, d), BF16)],
        compiler_params=_cp(("parallel", "arbitrary")),
        name="in_proj",
    )(x, modx, w)


def _prep_kernel(rq_ref, rk_ref, aq_ref, ak_ref, av_ref, cos_ref, sin_ref, gq_ref, gk_ref,
                 orq_ref, ork_ref, oaq_ref, oak_ref, oav_ref):
    c = cos_ref[...]
    s = sin_ref[...]
    lane = lax.broadcasted_iota(jnp.int32, c.shape, 1)
    first = (lane % 64) < 32

    def rope(x):
        partner = jnp.where(first, pltpu.roll(x, 96, 1), pltpu.roll(x, 32, 1))
        return x * c + partner * s

    def rms(x, g):
        return x * lax.rsqrt(jnp.mean(x * x, axis=-1, keepdims=True) + LN_EPS) * g

    for h in range(RET_HEADS):
        sl = slice(h * LANE, (h + 1) * LANE)
        orq_ref[:, sl] = rope(rq_ref[:, sl])
        ork_ref[:, sl] = rope(rk_ref[:, sl] * (RET_D ** -0.5))
    for h in range(ATT_HEADS):
        sl = slice(h * LANE, (h + 1) * LANE)
        oaq_ref[:, sl] = rope(rms(aq_ref[:, sl], gq_ref[...])).astype(BF16)
    for h in range(ATT_KV_HEADS):
        sl = slice(h * LANE, (h + 1) * LANE)
        oak_ref[:, sl] = rope(rms(ak_ref[:, sl], gk_ref[...])).astype(BF16)
    oav_ref[...] = av_ref[...].astype(BF16)


def _prep(z, cos_t, sin_t, gq, gk, tm):
    m = z.shape[0]
    qw, kw = ATT_HEADS * HEAD_DIM, ATT_KV_HEADS * HEAD_DIM
    rw = RET_HEADS * RET_D
    d_off = A_W + R_W + H_W
    row = lambda i: (i, 0)
    return pl.pallas_call(
        _prep_kernel,
        grid=(m // tm,),
        in_specs=[pl.BlockSpec((tm, rw), lambda i: (i, A_W // rw)),
                  pl.BlockSpec((tm, rw), lambda i: (i, A_W // rw + 1)),
                  pl.BlockSpec((tm, qw), lambda i: (i, d_off // qw)),
                  pl.BlockSpec((tm, kw), lambda i: (i, (d_off + qw) // kw)),
                  pl.BlockSpec((tm, kw), lambda i: (i, (d_off + qw) // kw + 1)),
                  pl.BlockSpec((tm, LANE), row),
                  pl.BlockSpec((tm, LANE), row),
                  pl.BlockSpec((1, LANE), lambda i: (0, 0)),
                  pl.BlockSpec((1, LANE), lambda i: (0, 0))],
        out_specs=[pl.BlockSpec((tm, rw), row), pl.BlockSpec((tm, rw), row),
                   pl.BlockSpec((tm, qw), row), pl.BlockSpec((tm, kw), row), pl.BlockSpec((tm, kw), row)],
        out_shape=[jax.ShapeDtypeStruct((m, rw), F32), jax.ShapeDtypeStruct((m, rw), F32),
                   jax.ShapeDtypeStruct((m, qw), BF16), jax.ShapeDtypeStruct((m, kw), BF16),
                   jax.ShapeDtypeStruct((m, kw), BF16)],
        compiler_params=_cp(("parallel",)),
        name="qk_prep",
    )(z, z, z, z, z, cos_t, sin_t, gq.reshape(1, LANE), gk.reshape(1, LANE))


def _attn_kernel(q_ref, k_ref, v_ref, o_ref, *, n_lat_tiles, lat_len, scale):
    i = pl.program_id(2)
    grp = ATT_HEADS // ATT_KV_HEADS

    def attend(k, v):
        for g in range(grp):
            q = q_ref[0, :, g * HEAD_DIM:(g + 1) * HEAD_DIM]
            s = _dot_nt(q, k) * scale
            p = jnp.exp(s - jnp.max(s, axis=-1, keepdims=True))
            den = jnp.sum(p, axis=-1, keepdims=True)
            o_ref[0, :, g * HEAD_DIM:(g + 1) * HEAD_DIM] = _dot(p.astype(BF16), v) / den

    @pl.when(i < n_lat_tiles)
    def _():
        attend(k_ref[0], v_ref[0])

    @pl.when(i >= n_lat_tiles)
    def _():
        attend(k_ref[0, lat_len:, :], v_ref[0, lat_len:, :])


def _attention(aq, ak, av, nb, lat_len, tq):
    m = aq.shape[0]
    t = m // nb
    grp_w = (ATT_HEADS // ATT_KV_HEADS) * HEAD_DIM
    kern = functools.partial(_attn_kernel, n_lat_tiles=lat_len // tq, lat_len=lat_len, scale=HEAD_DIM ** -0.5)
    out = pl.pallas_call(
        kern,
        grid=(nb, ATT_KV_HEADS, t // tq),
        in_specs=[pl.BlockSpec((1, tq, grp_w), lambda b, kv, i: (b, i, kv)),
                  pl.BlockSpec((1, t, HEAD_DIM), lambda b, kv, i: (b, 0, kv)),
                  pl.BlockSpec((1, t, HEAD_DIM), lambda b, kv, i: (b, 0, kv))],
        out_specs=pl.BlockSpec((1, tq, grp_w), lambda b, kv, i: (b, i, kv)),
        out_shape=jax.ShapeDtypeStruct((nb, t, ATT_HEADS * HEAD_DIM), F32),
        compiler_params=_cp(("parallel", "parallel", "arbitrary")),
        name="gqa_attention",
    )(aq.reshape(nb, t, -1), ak.reshape(nb, t, -1), av.reshape(nb, t, -1))
    return out.reshape(m, -1)


def _ret_kernel(q_ref, k_ref, v_ref, g_ref, intra_ref, qd_ref, kd_ref, cd_ref, o_ref, of_ref, ob_ref,
                *, n_lat, n_ctx):
    c_len = RET_CHUNK
    n = n_lat + n_ctx

    def chunk(ref, c):
        return ref[0, pl.ds(pl.multiple_of(c * c_len, c_len), c_len), :]

    def step(d, c, state):
        qc, kc, vc = chunk(q_ref, c), chunk(k_ref, c), chunk(v_ref, c).astype(BF16)
        s = _dot_nt(qc.astype(BF16), kc.astype(BF16)) * intra_ref[d, 0]
        o = _dot(s.astype(BF16), vc) + _dot((qc * qd_ref[d, 0]).astype(BF16), state.astype(BF16))
        state = cd_ref[d, 0, 0:1, :] * state + _dot_tn((kc * kd_ref[d, 0]).astype(BF16), vc)
        return o, state

    def body(t, carry):
        sf, sb = carry
        cf = jnp.where(t < n_ctx, n_lat + t, t - n_ctx)
        cb = jnp.where(t < n_ctx, n - 1 - t, n_lat - 1 - (t - n_ctx))
        o_f, sf = step(0, cf, sf)
        of_ref[pl.ds(pl.multiple_of(cf * c_len, c_len), c_len), :] = o_f
        o_b, sb = step(1, cb, sb)
        ob_ref[pl.ds(pl.multiple_of(cb * c_len, c_len), c_len), :] = o_b
        return sf, sb

    zero = jnp.zeros((RET_D, RET_D), F32)
    lax.fori_loop(0, n, body, (zero, zero))
    o = of_ref[...] + ob_ref[...]
    mu = jnp.mean(o, axis=-1, keepdims=True)
    oc = o - mu
    var = jnp.mean(oc * oc, axis=-1, keepdims=True)
    g = g_ref[0]
    o_ref[0] = (g * jax.nn.sigmoid(g)) * (oc * lax.rsqrt(var + LN_EPS))


def _retention(rq, rk, z, tabs, nb, lat_len):
    m = rq.shape[0]
    t = m // nb
    intra, qd, kd, cd = tabs
    v_blk = (A_W + 2 * RET_HEADS * RET_D) // LANE
    g_blk = v_blk + RET_HEADS
    kern = functools.partial(_ret_kernel, n_lat=lat_len // RET_CHUNK, n_ctx=(t - lat_len) // RET_CHUNK)
    seq = lambda off: pl.BlockSpec((1, t, LANE), lambda b, h: (b, 0, off + h))
    tab = lambda r: pl.BlockSpec((2, 1, r, LANE), lambda b, h: (0, h, 0, 0))
    out = pl.pallas_call(
        kern,
        grid=(nb, RET_HEADS),
        in_specs=[seq(0), seq(0), seq(v_blk), seq(g_blk), tab(RET_CHUNK), tab(RET_CHUNK), tab(RET_CHUNK), tab(8)],
        out_specs=seq(0),
        out_shape=jax.ShapeDtypeStruct((nb, t, RET_HEADS * RET_D), F32),
        scratch_shapes=[pltpu.VMEM((t, LANE), F32), pltpu.VMEM((t, LANE), F32)],
        compiler_params=_cp(("parallel", "parallel")),
        name="retention",
    )(rq.reshape(nb, t, -1), rk.reshape(nb, t, -1), z.reshape(nb, t, -1), z.reshape(nb, t, -1), intra, qd, kd, cd)
    return out.reshape(m, -1)


def _retention_tables(log_gamma):
    c = RET_CHUNK
    i = jnp.arange(c, dtype=F32)
    lg = log_gamma.astype(F32)[:, :, None]
    rel = i[:, None] - i[None, :]
    intra_f = jnp.where(rel >= 0, jnp.exp(lg[0][..., None] * jnp.maximum(rel, 0.0)), 0.0)
    intra_b = jnp.where(rel <= 0, jnp.exp(lg[1][..., None] * jnp.maximum(-rel, 0.0)), 0.0)
    qd = jnp.stack([jnp.exp(lg[0] * (i + 1.0)), jnp.exp(lg[1] * (c - i))])
    kd = jnp.stack([jnp.exp(lg[0] * (c - 1.0 - i)), jnp.exp(lg[1] * i)])
    cd = jnp.exp(lg * c)
    bc = lambda a: jnp.broadcast_to(a[..., None], a.shape + (LANE,))
    return (jnp.stack([intra_f, intra_b]), bc(qd), bc(kd),
            jnp.broadcast_to(cd[..., None], (2, RET_HEADS, 8, LANE)))


def _shift_rows(x, lat_len):
    t = x.shape[0]
    row = lax.broadcasted_iota(jnp.int32, x.shape, 0)
    xm = jnp.where((row == 0) | (row == lat_len), 0.0, pltpu.roll(x, 1, 0))
    xp = jnp.where((row == lat_len - 1) | (row == t - 1), 0.0, pltpu.roll(x, t - 1, 0))
    return xm, xp


def _shortconv_kernel(bg_ref, cg_ref, xv_ref, w_ref, b_ref, o_ref, *, lat_len):
    p = cg_ref[0] * xv_ref[0]
    pm, pp = _shift_rows(p, lat_len)
    o_ref[0] = bg_ref[0] * (pm * w_ref[0:1, :] + p * w_ref[1:2, :] + pp * w_ref[2:3, :] + b_ref[...])


def _shortconv(z, w, b, nb, lat_len):
    m = z.shape[0]
    t = m // nb
    nct = BRANCH_W // LANE
    kern = functools.partial(_shortconv_kernel, lat_len=lat_len)
    seq = lambda off: pl.BlockSpec((1, t, LANE), lambda bb, j: (bb, 0, off + j))
    out = pl.pallas_call(
        kern,
        grid=(nb, nct),
        in_specs=[seq(0), seq(nct), seq(2 * nct),
                  pl.BlockSpec((3, LANE), lambda bb, j: (0, j)),
                  pl.BlockSpec((1, LANE), lambda bb, j: (0, j))],
        out_specs=seq(0),
        out_shape=jax.ShapeDtypeStruct((nb, t, BRANCH_W), F32),
        compiler_params=_cp(("parallel", "parallel")),
        name="short_conv",
    )(z.reshape(nb, t, -1), z.reshape(nb, t, -1), z.reshape(nb, t, -1), w, b.reshape(1, -1))
    return out.reshape(m, -1)


def _hyconv_kernel(z_ref, w_ref, b_ref, ol_ref, oc_ref, *, lat_len):
    x = z_ref[0]
    xm, xp = _shift_rows(x, lat_len)
    y = xm * w_ref[0:1, :] + x * w_ref[1:2, :] + xp * w_ref[2:3, :] + b_ref[...]
    ol_ref[0, 0] = y[:lat_len]
    oc_ref[0, 0] = y[lat_len:]


def _hyconv(z, w, b, nb, lat_len):
    m = z.shape[0]
    t = m // nb
    nct = BRANCH_W // LANE
    off = (A_W + R_W) // LANE
    kern = functools.partial(_hyconv_kernel, lat_len=lat_len)
    return pl.pallas_call(
        kern,
        grid=(nb, 3 * nct),
        in_specs=[pl.BlockSpec((1, t, LANE), lambda bb, j: (bb, 0, off + j)),
                  pl.BlockSpec((3, LANE), lambda bb, j: (0, j)),
                  pl.BlockSpec((1, LANE), lambda bb, j: (0, j))],
        out_specs=[pl.BlockSpec((1, 1, lat_len, LANE), lambda bb, j: (j // nct, bb, 0, j % nct)),
                   pl.BlockSpec((1, 1, t - lat_len, LANE), lambda bb, j: (j // nct, bb, 0, j % nct))],
        out_shape=[jax.ShapeDtypeStruct((3, nb, lat_len, BRANCH_W), F32),
                   jax.ShapeDtypeStruct((3, nb, t - lat_len, BRANCH_W), F32)],
        compiler_params=_cp(("parallel", "parallel")),
        name="hyena_dwconv",
    )(z.reshape(nb, t, -1), w, b.reshape(1, -1))


def _stack_c(m):
    return np.block([[m.real, -m.imag], [m.imag, m.real]])


def _split_np(m):
    m32 = np.asarray(m, np.float32)
    hi = m32.astype(BF16)
    lo = (m32 - hi.astype(np.float32)).astype(BF16)
    return np.concatenate([hi, lo], axis=-2)


@functools.lru_cache(maxsize=None)
def _dft_tables(lat_len):
    n = 2 * lat_len
    n1, n2 = FFT_N1, 2 * lat_len // FFT_N1
    a1 = np.arange(n1)
    a2 = np.arange(n2)
    w1 = np.exp(-2j * np.pi * np.outer(a1, a1[:n1 // 2]) / n1)
    f1 = _stack_c(w1)
    w2 = np.exp(-2j * np.pi * np.outer(a2, a2) / n2)
    tw = np.exp(-2j * np.pi * np.outer(a1, a2) / n)
    cf = w2[None, :, :] * tw[:, None, :]
    f2 = np.stack([_stack_c(cf[k]) for k in range(n1)])
    g2 = np.stack([_stack_c(cf[k].conj().T) for k in range(n1)])
    w1i = np.exp(2j * np.pi * np.outer(a1[:n1 // 2], a1) / n1) / n
    f3 = _stack_c(w1i)
    return _split_np(f1), _split_np(f2), _split_np(g2), _split_np(f3)


@functools.lru_cache(maxsize=None)
def _dft_tables_small(ctx_len):
    n = 2 * ctx_len
    a = np.arange(n)
    ff = _stack_c(np.exp(-2j * np.pi * np.outer(a, a[:ctx_len]) / n))
    fi = _stack_c(np.exp(2j * np.pi * np.outer(a[:ctx_len], a) / n) / n)
    return _split_np(ff), _split_np(fi)


def _fft1_kernel(x_ref, f_ref, o_ref):
    o_ref[0] = _split_dot(f_ref[...], x_ref[0])


def _fft1(x, f1, tc):
    p, r, cols = x.shape
    return pl.pallas_call(
        _fft1_kernel,
        grid=(p, cols // tc),
        in_specs=[pl.BlockSpec((1, r, tc), lambda pp, j: (pp, 0, j)),
                  pl.BlockSpec(f1.shape, lambda pp, j: (0, 0))],
        out_specs=pl.BlockSpec((1, 2 * r, tc), lambda pp, j: (pp, 0, j)),
        out_shape=jax.ShapeDtypeStruct((p, 2 * r, cols), F32),
        compiler_params=_cp(("parallel", "parallel")),
        name="hyena_fft_stage1",
    )(x, f1)


def _fft2_kernel(a_ref, h_ref, f_ref, g_ref, o_ref, *, kb, n2):
    w = a_ref.shape[-1]
    for t in range(kb):
        a = a_ref[0, :, t].reshape(2 * n2, w)
        x = _split_dot(f_ref[t], a)
        xr, xi = x[:n2], x[n2:]
        hr, hi = h_ref[0, t], h_ref[1, t]
        y = jnp.concatenate([xr * hr - xi * hi, xr * hi + xi * hr], axis=0)
        o_ref[0, :, t] = _split_dot(g_ref[t], y).reshape(2, n2, w)


def _fft2(a, hspec, f2, g2, kb):
    p, _, n1, n2, w = a.shape
    kern = functools.partial(_fft2_kernel, kb=kb, n2=n2)
    return pl.pallas_call(
        kern,
        grid=(p, n1 // kb),
        in_specs=[pl.BlockSpec((1, 2, kb, n2, w), lambda pp, k: (pp, 0, k, 0, 0)),
                  pl.BlockSpec((2, kb, n2, w), lambda pp, k: (0, k, 0, 0)),
                  pl.BlockSpec((kb, 4 * n2, 2 * n2), lambda pp, k: (k, 0, 0)),
                  pl.BlockSpec((kb, 4 * n2, 2 * n2), lambda pp, k: (k, 0, 0))],
        out_specs=pl.BlockSpec((1, 2, kb, n2, w), lambda pp, k: (pp, 0, k, 0, 0)),
        out_shape=jax.ShapeDtypeStruct(a.shape, F32),
        compiler_params=_cp(("parallel", "parallel")),
        name="hyena_fft_stage2",
    )(a, hspec, f2, g2)


def _fft3_kernel(b_ref, f_ref, gate_ref, prev_ref, skip_ref, o_ref):
    conv = _split_dot(f_ref[...], b_ref[0])
    r = conv.shape[0] // 2
    tc = conv.shape[1]
    prev = prev_ref[0].reshape(2 * r, tc)
    gate = gate_ref[0].reshape(2 * r, tc)
    o_ref[0] = (gate * (conv + skip_ref[...] * prev)).reshape(2, r, tc)


def _fft3(bm, f3, gate, prev, skip_t, out_rows, tc):
    p, r2, cols = bm.shape
    r = r2 // 4
    return pl.pallas_call(
        _fft3_kernel,
        grid=(p, cols // tc),
        in_specs=[pl.BlockSpec((1, r2, tc), lambda pp, j: (pp, 0, j)),
                  pl.BlockSpec(f3.shape, lambda pp, j: (0, 0)),
                  pl.BlockSpec((1, 2, r, tc), lambda pp, j: (pp, 0, 0, j)),
                  pl.BlockSpec((1, 2, r, tc), lambda pp, j: (pp, 0, 0, j)),
                  pl.BlockSpec((1, tc), lambda pp, j: (0, j))],
        out_specs=pl.BlockSpec((1, 2, r, tc), lambda pp, j: (pp, 0, 0, j)),
        out_shape=jax.ShapeDtypeStruct((p, 2, out_rows, cols), F32),
        compiler_params=_cp(("parallel", "parallel")),
        name="hyena_fft_stage3",
    )(bm, f3, gate, prev, skip_t)


def _hyctx_kernel(v_ref, x1_ref, x2_ref, h_ref, ff_ref, fi_ref, skip_ref, yh_hbm_ref, o_ref):
    del yh_hbm_ref
    lc, w = v_ref.shape[2], v_ref.shape[3]
    y = v_ref[0].reshape(2 * lc, w)
    for o, gate_ref in enumerate((x1_ref, x2_ref)):
        x = _split_dot(ff_ref[...], y)
        xr, xi = x[:2 * lc], x[2 * lc:]
        hr, hi = h_ref[o, 0], h_ref[o, 1]
        spec = jnp.concatenate([xr * hr - xi * hi, xr * hi + xi * hr], axis=0)
        conv = _split_dot(fi_ref[...], spec)
        y = gate_ref[0].reshape(2 * lc, w) * (conv + skip_ref[o:o + 1, :] * y)
    o_ref[0] = y.reshape(2, lc, w)


def _hyena_ctx(hc, hspec_c, ff, fi, skip, yh):
    _, nb, lc, w = hc.shape
    p = nb // 2
    t = yh.shape[2]
    part = lambda s: pl.BlockSpec((1, 2, lc, w), lambda pp: (pp, 0, 0, 0))
    hc4 = hc.reshape(3, p, 2, lc, w)
    return pl.pallas_call(
        _hyctx_kernel,
        grid=(p,),
        in_specs=[part(0), part(1), part(2),
                  pl.BlockSpec(hspec_c.shape, lambda pp: (0, 0, 0, 0)),
                  pl.BlockSpec(ff.shape, lambda pp: (0, 0)),
                  pl.BlockSpec(fi.shape, lambda pp: (0, 0)),
                  pl.BlockSpec(skip.shape, lambda pp: (0, 0)),
                  pl.BlockSpec(memory_space=pl.ANY)],
        out_specs=pl.BlockSpec((1, 2, lc, w), lambda pp: (pp, 0, t // lc - 1, 0)),
        out_shape=jax.ShapeDtypeStruct(yh.shape, F32),
        input_output_aliases={7: 0},
        compiler_params=_cp(("parallel",)),
        name="hyena_ctx",
    )(hc4[0], hc4[1], hc4[2], hspec_c, ff, fi, skip, yh)


def _hyena_filter_spectra(l, w1, b1, w2, b2, w3, freq):
    t = jnp.linspace(0.0, 1.0, l, dtype=F32)[:, None]
    bands = (HY_EMB - 1) // 2
    w = 2.0 * math.pi * jnp.arange(l, dtype=F32) / l
    f = jnp.linspace(1e-4, bands - 1, bands, dtype=F32)
    ang = w[:, None] * f[None, :]
    feats = jnp.concatenate([t, jnp.cos(ang), -jnp.sin(ang)], axis=-1)
    hdn = jnp.sin(freq[0] * (feats @ w1 + b1))
    hdn = jnp.sin(freq[1] * (hdn @ w2 + b2))
    filt = (hdn @ w3).astype(F32)
    n_out = filt.shape[-1]
    deltas = jnp.linspace(math.log(HY_TARGET) / HY_SLOW_PCT, math.log(HY_TARGET) / HY_FAST_PCT, n_out, dtype=F32)
    filt = filt * jnp.exp(-t * jnp.abs(deltas))
    filt = filt.reshape(l, 2, 2, BRANCH_W).transpose(1, 2, 0, 3)
    hf, hb = filt[:, 0], filt[:, 1]
    full = jnp.concatenate([hf, jnp.zeros_like(hf[:, :1]), hb[:, :0:-1]], axis=1)
    return jnp.fft.fft(full, axis=1)


def _hyena(z, conv_w, conv_b, filt_w, skip, nb, lat_len):
    m = z.shape[0]
    t = m // nb
    lc = t - lat_len
    w = BRANCH_W
    p = nb // 2
    n1, n2 = FFT_N1, 2 * lat_len // FFT_N1
    cols = n2 * w
    hl, hc = _hyconv(z, conv_w, conv_b, nb, lat_len)
    spec_l = _hyena_filter_spectra(lat_len, *filt_w)
    spec_c = _hyena_filter_spectra(lc, *filt_w)
    sl = spec_l.reshape(2, n2, n1, w).transpose(0, 2, 1, 3)
    hs_l = jnp.stack([sl.real, sl.imag], axis=1).astype(F32)
    hs_c = jnp.stack([spec_c.real, spec_c.imag], axis=1).astype(F32)
    f1, f2, g2, f3 = _dft_tables(lat_len)
    ff, fi = _dft_tables_small(lc)
    tc = min(cols, 2048)
    kb = 8
    view = lambda a: a.reshape(p, 2, n1 // 2, cols)
    skip_t = jnp.tile(skip, (1, n2))
    y = hl[0]
    for o in range(2):
        a = _fft1(y.reshape(p, n1, cols), f1, tc)
        bm = _fft2(a.reshape(p, 2, n1, n2, w), hs_l[o], f2, g2, kb)
        rows = n1 // 2 if o == 0 else t * w // cols
        y = _fft3(bm.reshape(p, 2 * n1, cols), f3, view(hl[1 + o]), view(y), skip_t[o:o + 1], rows, tc)
    yh = y.reshape(p, 2, t, w)
    yh = _hyena_ctx(hc, hs_c, ff, fi, skip, yh)
    return yh.reshape(m, w)


def _merge_kernel(x_ref, mod_ref, ya_ref, yr_ref, yh_ref, yd_ref, wg_ref, bg_ref, wb_ref, wo_ref, g_ref, b_ref,
                  o_ref, *, alpha):
    x = x_ref[...]
    u = _modulate(x, mod_ref, 0, 1).astype(BF16)
    acc = None
    for i, y_ref in enumerate((ya_ref, yr_ref, yh_ref, yd_ref)):
        gate = jax.nn.sigmoid(_dot(u, wg_ref[i]) + bg_ref[i:i + 1, :])
        term = gate * _dot(y_ref[...].astype(BF16), wb_ref[i])
        acc = term if acc is None else acc + term
    mix = _dot(acc.astype(BF16), wo_ref[...])
    o_ref[...] = _gated_residual_ln(x, mix, mod_ref, 2, alpha, g_ref[...], b_ref[...])


def _merge(x, modx, ys, wg, bg, wb, wo, g, b, alpha, tm):
    m, d = x.shape
    w = BRANCH_W
    row = lambda i: (i, 0)
    const2 = lambda i: (0, 0)
    const3 = lambda i: (0, 0, 0)
    return pl.pallas_call(
        functools.partial(_merge_kernel, alpha=alpha),
        grid=(m // tm,),
        in_specs=[pl.BlockSpec((tm, d), row),
                  pl.BlockSpec((tm // SUB, 8, d), lambda i: (i, 0, 0)),
                  pl.BlockSpec((tm, w), row), pl.BlockSpec((tm, w), row),
                  pl.BlockSpec((tm, w), row), pl.BlockSpec((tm, w), row),
                  pl.BlockSpec(wg.shape, const3), pl.BlockSpec(bg.shape, const2),
                  pl.BlockSpec(wb.shape, const3), pl.BlockSpec(wo.shape, const2),
                  pl.BlockSpec((1, d), const2), pl.BlockSpec((1, d), const2)],
        out_specs=pl.BlockSpec((tm, d), row),
        out_shape=jax.ShapeDtypeStruct((m, d), F32),
        compiler_params=_cp(("parallel",)),
        name="branch_merge",
    )(x, modx, *ys, wg, bg, wb, wo, g.reshape(1, d), b.reshape(1, d))


def _ffn_kernel(x_ref, mod_ref, w1_ref, w3_ref, w2_ref, g_ref, b_ref, o_ref, u_ref, acc_ref, *, alpha):
    j = pl.program_id(1)

    @pl.when(j == 0)
    def _():
        u_ref[...] = _modulate(x_ref[...], mod_ref, 3, 4).astype(BF16)
        acc_ref[...] = jnp.zeros_like(acc_ref)

    u = u_ref[...]
    h1 = _dot(u, w1_ref[...])
    h3 = _dot(u, w3_ref[...])
    acc_ref[...] += _dot((h1 * jax.nn.sigmoid(h1) * h3).astype(BF16), w2_ref[...])

    @pl.when(j == pl.num_programs(1) - 1)
    def _():
        o_ref[...] = _gated_residual_ln(x_ref[...], acc_ref[...], mod_ref, 5, alpha, g_ref[...], b_ref[...])


def _ffn(x, modx, w1, w3, w2, g, b, alpha, tm, tf):
    m, d = x.shape
    ff = w1.shape[1]
    return pl.pallas_call(
        functools.partial(_ffn_kernel, alpha=alpha),
        grid=(m // tm, ff // tf),
        in_specs=[pl.BlockSpec((tm, d), lambda i, j: (i, 0)),
                  pl.BlockSpec((tm // SUB, 8, d), lambda i, j: (i, 0, 0)),
                  pl.BlockSpec((d, tf), lambda i, j: (0, j)),
                  pl.BlockSpec((d, tf), lambda i, j: (0, j)),
                  pl.BlockSpec((tf, d), lambda i, j: (j, 0)),
                  pl.BlockSpec((1, d), lambda i, j: (0, 0)),
                  pl.BlockSpec((1, d), lambda i, j: (0, 0))],
        out_specs=pl.BlockSpec((tm, d), lambda i, j: (i, 0)),
        out_shape=jax.ShapeDtypeStruct((m, d), F32),
        scratch_shapes=[pltpu.VMEM((tm, d), BF16), pltpu.VMEM((tm, d), F32)],
        compiler_params=_cp(("parallel", "arbitrary")),
        name="dense_swiglu",
    )(x, modx, w1, w3, w2, g.reshape(1, d), b.reshape(1, d))


def _route_kernel(x_ref, mod_ref, wr_ref, u_ref, lg_ref):
    u = _modulate(x_ref[...], mod_ref, 3, 4)
    u_ref[...] = u.astype(BF16)
    lg_ref[...] = jnp.dot(u, wr_ref[...], preferred_element_type=F32, precision=lax.Precision.HIGHEST)


def _route(x, modx, wr_pad, tm):
    m, d = x.shape
    return pl.pallas_call(
        _route_kernel,
        grid=(m // tm,),
        in_specs=[pl.BlockSpec((tm, d), lambda i: (i, 0)),
                  pl.BlockSpec((tm // SUB, 8, d), lambda i: (i, 0, 0)),
                  pl.BlockSpec((d, LANE), lambda i: (0, 0))],
        out_specs=[pl.BlockSpec((tm, d), lambda i: (i, 0)), pl.BlockSpec((tm, LANE), lambda i: (i, 0))],
        out_shape=[jax.ShapeDtypeStruct((m, d), BF16), jax.ShapeDtypeStruct((m, LANE), F32)],
        compiler_params=_cp(("parallel",)),
        name="moe_router",
    )(x, modx, wr_pad)


def _moe_kernel(be_ref, nu_ref, xb_ref, rw_ref, w1_ref, w3_ref, w2_ref, o_ref, acc_ref):
    i, j = pl.program_id(0), pl.program_id(1)
    live = i < nu_ref[0]

    @pl.when(j == 0)
    def _():
        acc_ref[...] = jnp.zeros_like(acc_ref)

    @pl.when(live)
    def _():
        u = xb_ref[...]
        h1 = _dot(u, w1_ref[0])
        h3 = _dot(u, w3_ref[0])
        acc_ref[...] += _dot((h1 * jax.nn.sigmoid(h1) * h3).astype(BF16), w2_ref[0])

    @pl.when(j == pl.num_programs(1) - 1)
    def _():
        o_ref[...] = acc_ref[...] * rw_ref[...]


def _moe_experts(xb, row_w, blk_e, n_used, w1, w3, w2, tm, tf):
    r, d = xb.shape
    ff = w1.shape[2]
    nj = ff // tf

    def wmap(i, j, be, nu):
        return (be[i], 0, jnp.where(i < nu[0], j, nj - 1))

    def w2map(i, j, be, nu):
        return (be[i], jnp.where(i < nu[0], j, nj - 1), 0)

    grid_spec = pltpu.PrefetchScalarGridSpec(
        num_scalar_prefetch=2,
        grid=(r // tm, nj),
        in_specs=[pl.BlockSpec((tm, d), lambda i, j, be, nu: (i, 0)),
                  pl.BlockSpec((tm, 1), lambda i, j, be, nu: (i, 0)),
                  pl.BlockSpec((1, d, tf), wmap),
                  pl.BlockSpec((1, d, tf), wmap),
                  pl.BlockSpec((1, tf, d), w2map)],
        out_specs=pl.BlockSpec((tm, d), lambda i, j, be, nu: (i, 0)),
        scratch_shapes=[pltpu.VMEM((tm, d), F32)],
    )
    return pl.pallas_call(
        _moe_kernel,
        grid_spec=grid_spec,
        out_shape=jax.ShapeDtypeStruct((r, d), F32),
        compiler_params=_cp(("arbitrary", "arbitrary")),
        name="moe_experts",
    )(blk_e, n_used, xb, row_w, w1, w3, w2)


def _ln_res_kernel(x_ref, f_ref, mod_ref, g_ref, b_ref, o_ref, *, alpha):
    o_ref[...] = _gated_residual_ln(x_ref[...], f_ref[...], mod_ref, 5, alpha, g_ref[...], b_ref[...])


def _ln_res(x, f, modx, g, b, alpha, tm):
    m, d = x.shape
    return pl.pallas_call(
        functools.partial(_ln_res_kernel, alpha=alpha),
        grid=(m // tm,),
        in_specs=[pl.BlockSpec((tm, d), lambda i: (i, 0)), pl.BlockSpec((tm, d), lambda i: (i, 0)),
                  pl.BlockSpec((tm // SUB, 8, d), lambda i: (i, 0, 0)),
                  pl.BlockSpec((1, d), lambda i: (0, 0)), pl.BlockSpec((1, d), lambda i: (0, 0))],
        out_specs=pl.BlockSpec((tm, d), lambda i: (i, 0)),
        out_shape=jax.ShapeDtypeStruct((m, d), F32),
        compiler_params=_cp(("parallel",)),
        name="moe_residual_ln",
    )(x, f, modx, g.reshape(1, d), b.reshape(1, d))


def _moe(x, modx, router, w1, w3, w2, g, b, alpha, tm_rows, tm, tf):
    m, d = x.shape
    wr_pad = jnp.zeros((d, LANE), F32).at[:, :N_EXPERTS].set(router)
    u, logits = _route(x, modx, wr_pad, tm_rows)
    top_v, top_i = lax.top_k(logits[:, :N_EXPERTS], TOP_K)
    wts = jax.nn.softmax(top_v, axis=-1).reshape(-1)
    e_flat = top_i.reshape(-1).astype(jnp.int32)
    n_pair = m * TOP_K
    onehot = (e_flat[:, None] == jnp.arange(N_EXPERTS, dtype=jnp.int32)[None, :]).astype(jnp.int32)
    csum = jnp.cumsum(onehot, axis=0)
    counts = csum[-1]
    rank = jnp.take_along_axis(csum, e_flat[:, None], axis=1)[:, 0] - 1
    padded = (counts + tm - 1) // tm * tm
    pad_end = jnp.cumsum(padded)
    pad_start = pad_end - padded
    dest = pad_start[e_flat] + rank
    n_blk = -(-n_pair // tm) + N_EXPERTS
    r = n_blk * tm
    tok = jnp.arange(n_pair, dtype=jnp.int32) // TOP_K
    row_tok = jnp.zeros((r,), jnp.int32).at[dest].set(tok)
    row_w = jnp.zeros((r,), F32).at[dest].set(wts)
    n_used = (pad_end[-1] // tm).astype(jnp.int32)
    blk = jnp.minimum(jnp.arange(n_blk, dtype=jnp.int32), n_used - 1) * tm
    blk_e = jnp.minimum(jnp.searchsorted(pad_end, blk, side='right'), N_EXPERTS - 1).astype(jnp.int32)
    xb = jnp.take(u, row_tok, axis=0)
    yb = _moe_experts(xb, row_w[:, None], blk_e, n_used.reshape(1), w1, w3, w2, tm, tf)
    pair = dest.reshape(m, TOP_K)
    f = jnp.take(yb, pair[:, 0], axis=0) + jnp.take(yb, pair[:, 1], axis=0)
    return _ln_res(x, f, modx, g, b, alpha, tm_rows)


def _rope_tables(nb, lat_len, ctx_len):
    rows = lat_len // GRID_W
    row = jnp.repeat(jnp.arange(rows, dtype=F32), GRID_W)
    col = jnp.tile(jnp.arange(GRID_W, dtype=F32), rows)
    n_freq = HEAD_DIM // 4
    inv = ROPE_THETA ** (-jnp.arange(n_freq, dtype=F32) / n_freq)
    ar, ac = row[:, None] * inv, col[:, None] * inv
    cos = jnp.concatenate([jnp.cos(ar), jnp.cos(ar), jnp.cos(ac), jnp.cos(ac)], axis=1)
    sin = jnp.concatenate([-jnp.sin(ar), jnp.sin(ar), -jnp.sin(ac), jnp.sin(ac)], axis=1)
    cos = jnp.concatenate([cos, jnp.ones((ctx_len, HEAD_DIM), F32)], axis=0)
    sin = jnp.concatenate([sin, jnp.zeros((ctx_len, HEAD_DIM), F32)], axis=0)
    return jnp.tile(cos, (nb, 1)), jnp.tile(sin, (nb, 1))


def _row_tile(m, pref):
    tm = pref
    while m % tm:
        tm //= 2
    return tm


def kernel(x, c, ctx, c_ctx, w_mod, b_mod, w_in, conv_a_w, conv_a_b, ret_decay, hy_conv_w, hy_conv_b, hy_w1, hy_b1, hy_w2, hy_b2, hy_w3, hy_freq, hy_skip, q_norm, k_norm, w_gate, b_gate, w_branch, w_o, ln_g, ln_b, ffn_w1, ffn_w3, ffn_w2, router, moe_w1, moe_w3, moe_w2):
    nb, lat_len, d = x.shape
    ctx_len = ctx.shape[1]
    depth = w_mod.shape[0]
    t = lat_len + ctx_len
    m = nb * t
    alpha = float((2 * depth) ** 0.25)
    assert lat_len % SUB == 0 and ctx_len % SUB == 0 and nb % 2 == 0

    act = jnp.concatenate([jax.nn.silu(c), jax.nn.silu(c_ctx)[None, :],
                           jnp.zeros((8 - nb - 1, d), F32)], axis=0)
    mods = _mod_all(act, w_mod, b_mod).reshape(depth, 8, N_MOD, d)
    mods = jnp.pad(mods, ((0, 0), (0, 0), (0, 8 - N_MOD), (0, 0)))
    tiles_per_batch = t // SUB
    tile_pos = np.arange(m // SUB) % tiles_per_batch
    tile_group = np.where(tile_pos < lat_len // SUB, np.arange(m // SUB) // tiles_per_batch, nb)
    cos_t, sin_t = _rope_tables(nb, lat_len, ctx_len)

    rows = jnp.concatenate([x, ctx], axis=1).reshape(m, d)
    tm = _row_tile(m, 512)
    for l in range(depth):
        modx = mods[l][tile_group]
        z = _proj(rows, modx, w_in[l].astype(BF16), tm, 512)
        rq, rk, aq, ak, av = _prep(z, cos_t, sin_t, q_norm[l], k_norm[l], tm)
        y_d = _attention(aq, ak, av, nb, lat_len, SUB)
        log_gamma = -jnp.exp(ret_decay[l].astype(F32))
        y_r = _retention(rq, rk, z, _retention_tables(log_gamma), nb, lat_len)
        y_a = _shortconv(z, conv_a_w[l], conv_a_b[l], nb, lat_len)
        y_h = _hyena(z, hy_conv_w[l], hy_conv_b[l],
                     (hy_w1[l], hy_b1[l], hy_w2[l], hy_b2[l], hy_w3[l], hy_freq[l]), hy_skip[l], nb, lat_len)
        rows = _merge(rows, modx, (y_a, y_r, y_h, y_d), w_gate[l].astype(BF16), b_gate[l],
                      w_branch[l].astype(BF16), w_o[l].astype(BF16), ln_g[l, 0], ln_b[l, 0], alpha, SUB)
        i = l // 2
        if l % 2 == 0:
            rows = _ffn(rows, modx, ffn_w1[i].astype(BF16), ffn_w3[i].astype(BF16), ffn_w2[i].astype(BF16),
                        ln_g[l, 1], ln_b[l, 1], alpha, tm, 256)
        else:
            rows = _moe(rows, modx, router[i], moe_w1[i].astype(BF16), moe_w3[i].astype(BF16),
                        moe_w2[i].astype(BF16), ln_g[l, 1], ln_b[l, 1], alpha, tm, 512, 512)
    return rows.reshape(nb, t, d)[:, :lat_len]
```

```python
import functools
import math

import numpy as np
import jax
import jax.numpy as jnp
from jax import lax
from jax.experimental import pallas as pl
from jax.experimental.pallas import tpu as pltpu

F32 = jnp.float32
BF16 = jnp.bfloat16

GRID_W = 64
BRANCH_W = 512
RET_HEADS = 4
RET_D = 128
RET_CHUNK = 128
HY_EMB = 33
HY_FAST_PCT = 0.3
HY_SLOW_PCT = 1.5
HY_TARGET = 1e-2
ATT_HEADS = 4
ATT_KV_HEADS = 2
HEAD_DIM = 128
ROPE_THETA = 10000.0
N_EXPERTS = 8
TOP_K = 2
N_MOD = 6
A_W = 3 * BRANCH_W
R_W = 4 * RET_HEADS * RET_D
H_W = 3 * BRANCH_W
D_W = (ATT_HEADS + 2 * ATT_KV_HEADS) * HEAD_DIM
IN_W = A_W + R_W + H_W + D_W
LN_EPS = 1e-6

SUB = 256
LANE = 128
FFT_N1 = 128
VMEM_LIMIT = 56 * 1024 * 1024


def _cp(sem, vmem=VMEM_LIMIT):
    return pltpu.CompilerParams(dimension_semantics=sem, vmem_limit_bytes=vmem)


def _dot(a, b):
    return jnp.dot(a, b, preferred_element_type=F32)


def _dot_nt(a, b):
    return lax.dot_general(a, b, (((1,), (1,)), ((), ())), preferred_element_type=F32)


def _dot_tn(a, b):
    return lax.dot_general(a, b, (((0,), (0,)), ((), ())), preferred_element_type=F32)


def _split_dot(f_cat, x):
    m = f_cat.shape[0] // 2
    xh = x.astype(BF16)
    xl = (x - xh.astype(F32)).astype(BF16)
    r = _dot(f_cat, xh)
    return r[:m] + r[m:] + _dot(f_cat[:m], xl)


def _modulate(x, mod_ref, shift_row, scale_row):
    parts = []
    for s in range(x.shape[0] // SUB):
        sh = mod_ref[s, shift_row:shift_row + 1, :]
        sc = mod_ref[s, scale_row:scale_row + 1, :]
        parts.append(x[s * SUB:(s + 1) * SUB] * (1.0 + sc) + sh)
    return parts[0] if len(parts) == 1 else jnp.concatenate(parts, axis=0)


def _gated_residual_ln(x, f, mod_ref, gate_row, alpha, g, b):
    parts = []
    for s in range(x.shape[0] // SUB):
        gt = mod_ref[s, gate_row:gate_row + 1, :]
        parts.append(alpha * x[s * SUB:(s + 1) * SUB] + gt * f[s * SUB:(s + 1) * SUB])
    r = parts[0] if len(parts) == 1 else jnp.concatenate(parts, axis=0)
    mu = jnp.mean(r, axis=-1, keepdims=True)
    rc = r - mu
    var = jnp.mean(rc * rc, axis=-1, keepdims=True)
    return rc * lax.rsqrt(var + LN_EPS) * g + b


def _mod_kernel(a_ref, w_ref, b_ref, o_ref):
    o_ref[0] = _dot(a_ref[...].astype(BF16), w_ref[0].astype(BF16)) + b_ref[0]


def _mod_all(act, w_mod, b_mod):
    depth, d, n = w_mod.shape
    tn = 512
    return pl.pallas_call(
        _mod_kernel,
        grid=(depth, n // tn),
        in_specs=[pl.BlockSpec((8, d), lambda l, j: (0, 0)),
                  pl.BlockSpec((1, d, tn), lambda l, j: (l, 0, j)),
                  pl.BlockSpec((1, 1, tn), lambda l, j: (l, 0, j))],
        out_specs=pl.BlockSpec((1, 8, tn), lambda l, j: (l, 0, j)),
        out_shape=jax.ShapeDtypeStruct((depth, 8, n), F32),
        compiler_params=_cp(("parallel", "parallel")),
        name="mod_vectors",
    )(act, w_mod, b_mod.reshape(depth, 1, n))


def _proj_kernel(x_ref, mod_ref, w_ref, o_ref, u_ref, *, tn):
    j = pl.program_id(1)

    @pl.when(j == 0)
    def _():
        u_ref[...] = _modulate(x_ref[...], mod_ref, 0, 1).astype(BF16)
    col = pl.multiple_of(j * tn, tn)
    o_ref[...] = _dot(u_ref[...], w_ref[:, pl.ds(col, tn)])


def _proj(x, modx, w, tm, tn):
    m, d = x.shape
    n = w.shape[1]
    return pl.pallas_call(
        functools.partial(_proj_kernel, tn=tn),
        grid=(m // tm, n // tn),
        in_specs=[pl.BlockSpec((tm, d), lambda i, j: (i, 0)),
                  pl.BlockSpec((tm // SUB, 8, d), lambda i, j: (i, 0, 0)),
                  pl.BlockSpec((d, n), lambda i, j: (0, 0), pipeline_mode=pl.Buffered(1))],
        out_specs=pl.BlockSpec((tm, tn), lambda i, j: (i, j)),
        out_shape=jax.ShapeDtypeStruct((m, n), F32),
        scratch_shapes=[pltpu.VMEM((tm, d), BF16)],
        compiler_params=_cp(("parallel", "arbitrary")),
        name="in_proj",
    )(x, modx, w)


def _prep_kernel(rq_ref, rk_ref, aq_ref, ak_ref, av_ref, cos_ref, sin_ref, gq_ref, gk_ref,
                 orq_ref, ork_ref, oaq_ref, oak_ref, oav_ref):
    c = cos_ref[...]
    s = sin_ref[...]
    lane = lax.broadcasted_iota(jnp.int32, c.shape, 1)
    first = (lane % 64) < 32

    def rope(x):
        partner = jnp.where(first, pltpu.roll(x, 96, 1), pltpu.roll(x, 32, 1))
        return x * c + partner * s

    def rms(x, g):
        return x * lax.rsqrt(jnp.mean(x * x, axis=-1, keepdims=True) + LN_EPS) * g

    for h in range(RET_HEADS):
        sl = slice(h * LANE, (h + 1) * LANE)
        orq_ref[:, sl] = rope(rq_ref[:, sl])
        ork_ref[:, sl] = rope(rk_ref[:, sl] * (RET_D ** -0.5))
    for h in range(ATT_HEADS):
        sl = slice(h * LANE, (h + 1) * LANE)
        oaq_ref[:, sl] = rope(rms(aq_ref[:, sl], gq_ref[...])).astype(BF16)
    for h in range(ATT_KV_HEADS):
        sl = slice(h * LANE, (h + 1) * LANE)
        oak_ref[:, sl] = rope(rms(ak_ref[:, sl], gk_ref[...])).astype(BF16)
    oav_ref[...] = av_ref[...].astype(BF16)


def _prep(z, cos_t, sin_t, gq, gk, tm):
    m = z.shape[0]
    qw, kw = ATT_HEADS * HEAD_DIM, ATT_KV_HEADS * HEAD_DIM
    rw = RET_HEADS * RET_D
    d_off = A_W + R_W + H_W
    row = lambda i: (i, 0)
    return pl.pallas_call(
        _prep_kernel,
        grid=(m // tm,),
        in_specs=[pl.BlockSpec((tm, rw), lambda i: (i, A_W // rw)),
                  pl.BlockSpec((tm, rw), lambda i: (i, A_W // rw + 1)),
                  pl.BlockSpec((tm, qw), lambda i: (i, d_off // qw)),
                  pl.BlockSpec((tm, kw), lambda i: (i, (d_off + qw) // kw)),
                  pl.BlockSpec((tm, kw), lambda i: (i, (d_off + qw) // kw + 1)),
                  pl.BlockSpec((tm, LANE), row),
                  pl.BlockSpec((tm, LANE), row),
                  pl.BlockSpec((1, LANE), lambda i: (0, 0)),
                  pl.BlockSpec((1, LANE), lambda i: (0, 0))],
        out_specs=[pl.BlockSpec((tm, rw), row), pl.BlockSpec((tm, rw), row),
                   pl.BlockSpec((tm, qw), row), pl.BlockSpec((tm, kw), row), pl.BlockSpec((tm, kw), row)],
        out_shape=[jax.ShapeDtypeStruct((m, rw), F32), jax.ShapeDtypeStruct((m, rw), F32),
                   jax.ShapeDtypeStruct((m, qw), BF16), jax.ShapeDtypeStruct((m, kw), BF16),
                   jax.ShapeDtypeStruct((m, kw), BF16)],
        compiler_params=_cp(("parallel",)),
        name="qk_prep",
    )(z, z, z, z, z, cos_t, sin_t, gq.reshape(1, LANE), gk.reshape(1, LANE))


def _attn_kernel(q_ref, k_ref, v_ref, o_ref, *, n_lat_tiles, lat_len, scale):
    i = pl.program_id(2)
    grp = ATT_HEADS // ATT_KV_HEADS

    def attend(k, v):
        for g in range(grp):
            q = q_ref[0, :, g * HEAD_DIM:(g + 1) * HEAD_DIM]
            s = _dot_nt(q, k) * scale
            p = jnp.exp(s - jnp.max(s, axis=-1, keepdims=True))
            den = jnp.sum(p, axis=-1, keepdims=True)
            o_ref[0, :, g * HEAD_DIM:(g + 1) * HEAD_DIM] = _dot(p.astype(BF16), v) / den

    @pl.when(i < n_lat_tiles)
    def _():
        attend(k_ref[0], v_ref[0])

    @pl.when(i >= n_lat_tiles)
    def _():
        attend(k_ref[0, lat_len:, :], v_ref[0, lat_len:, :])


def _attention(aq, ak, av, nb, lat_len, tq):
    m = aq.shape[0]
    t = m // nb
    grp_w = (ATT_HEADS // ATT_KV_HEADS) * HEAD_DIM
    kern = functools.partial(_attn_kernel, n_lat_tiles=lat_len // tq, lat_len=lat_len, scale=HEAD_DIM ** -0.5)
    out = pl.pallas_call(
        kern,
        grid=(nb, ATT_KV_HEADS, t // tq),
        in_specs=[pl.BlockSpec((1, tq, grp_w), lambda b, kv, i: (b, i, kv)),
                  pl.BlockSpec((1, t, HEAD_DIM), lambda b, kv, i: (b, 0, kv)),
                  pl.BlockSpec((1, t, HEAD_DIM), lambda b, kv, i: (b, 0, kv))],
        out_specs=pl.BlockSpec((1, tq, grp_w), lambda b, kv, i: (b, i, kv)),
        out_shape=jax.ShapeDtypeStruct((nb, t, ATT_HEADS * HEAD_DIM), F32),
        compiler_params=_cp(("parallel", "parallel", "arbitrary")),
        name="gqa_attention",
    )(aq.reshape(nb, t, -1), ak.reshape(nb, t, -1), av.reshape(nb, t, -1))
    return out.reshape(m, -1)


def _ret_kernel(q_ref, k_ref, v_ref, g_ref, intra_ref, qd_ref, kd_ref, cd_ref, o_ref, of_ref, ob_ref,
                *, n_lat, n_ctx):
    c_len = RET_CHUNK
    n = n_lat + n_ctx

    def chunk(ref, c):
        return ref[0, pl.ds(pl.multiple_of(c * c_len, c_len), c_len), :]

    def step(d, c, state):
        qc, kc, vc = chunk(q_ref, c), chunk(k_ref, c), chunk(v_ref, c).astype(BF16)
        s = _dot_nt(qc.astype(BF16), kc.astype(BF16)) * intra_ref[d, 0]
        o = _dot(s.astype(BF16), vc) + _dot((qc * qd_ref[d, 0]).astype(BF16), state.astype(BF16))
        state = cd_ref[d, 0, 0:1, :] * state + _dot_tn((kc * kd_ref[d, 0]).astype(BF16), vc)
        return o, state

    def body(t, carry):
        sf, sb = carry
        cf = jnp.where(t < n_ctx, n_lat + t, t - n_ctx)
        cb = jnp.where(t < n_ctx, n - 1 - t, n_lat - 1 - (t - n_ctx))
        o_f, sf = step(0, cf, sf)
        of_ref[pl.ds(pl.multiple_of(cf * c_len, c_len), c_len), :] = o_f
        o_b, sb = step(1, cb, sb)
        ob_ref[pl.ds(pl.multiple_of(cb * c_len, c_len), c_len), :] = o_b
        return sf, sb

    zero = jnp.zeros((RET_D, RET_D), F32)
    lax.fori_loop(0, n, body, (zero, zero))
    o = of_ref[...] + ob_ref[...]
    mu = jnp.mean(o, axis=-1, keepdims=True)
    oc = o - mu
    var = jnp.mean(oc * oc, axis=-1, keepdims=True)
    g = g_ref[0]
    o_ref[0] = (g * jax.nn.sigmoid(g)) * (oc * lax.rsqrt(var + LN_EPS))


def _retention(rq, rk, z, tabs, nb, lat_len):
    m = rq.shape[0]
    t = m // nb
    intra, qd, kd, cd = tabs
    v_blk = (A_W + 2 * RET_HEADS * RET_D) // LANE
    g_blk = v_blk + RET_HEADS
    kern = functools.partial(_ret_kernel, n_lat=lat_len // RET_CHUNK, n_ctx=(t - lat_len) // RET_CHUNK)
    seq = lambda off: pl.BlockSpec((1, t, LANE), lambda b, h: (b, 0, off + h))
    tab = lambda r: pl.BlockSpec((2, 1, r, LANE), lambda b, h: (0, h, 0, 0))
    out = pl.pallas_call(
        kern,
        grid=(nb, RET_HEADS),
        in_specs=[seq(0), seq(0), seq(v_blk), seq(g_blk), tab(RET_CHUNK), tab(RET_CHUNK), tab(RET_CHUNK), tab(8)],
        out_specs=seq(0),
        out_shape=jax.ShapeDtypeStruct((nb, t, RET_HEADS * RET_D), F32),
        scratch_shapes=[pltpu.VMEM((t, LANE), F32), pltpu.VMEM((t, LANE), F32)],
        compiler_params=_cp(("parallel", "parallel")),
        name="retention",
    )(rq.reshape(nb, t, -1), rk.reshape(nb, t, -1), z.reshape(nb, t, -1), z.reshape(nb, t, -1), intra, qd, kd, cd)
    return out.reshape(m, -1)


def _retention_tables(log_gamma):
    c = RET_CHUNK
    i = jnp.arange(c, dtype=F32)
    lg = log_gamma.astype(F32)[:, :, None]
    rel = i[:, None] - i[None, :]
    intra_f = jnp.where(rel >= 0, jnp.exp(lg[0][..., None] * jnp.maximum(rel, 0.0)), 0.0)
    intra_b = jnp.where(rel <= 0, jnp.exp(lg[1][..., None] * jnp.maximum(-rel, 0.0)), 0.0)
    qd = jnp.stack([jnp.exp(lg[0] * (i + 1.0)), jnp.exp(lg[1] * (c - i))])
    kd = jnp.stack([jnp.exp(lg[0] * (c - 1.0 - i)), jnp.exp(lg[1] * i)])
    cd = jnp.exp(lg * c)
    bc = lambda a: jnp.broadcast_to(a[..., None], a.shape + (LANE,))
    return (jnp.stack([intra_f, intra_b]), bc(qd), bc(kd),
            jnp.broadcast_to(cd[..., None], (2, RET_HEADS, 8, LANE)))


def _shift_rows(x, lat_len):
    t = x.shape[0]
    row = lax.broadcasted_iota(jnp.int32, x.shape, 0)
    xm = jnp.where((row == 0) | (row == lat_len), 0.0, pltpu.roll(x, 1, 0))
    xp = jnp.where((row == lat_len - 1) | (row == t - 1), 0.0, pltpu.roll(x, t - 1, 0))
    return xm, xp


def _shortconv_kernel(bg_ref, cg_ref, xv_ref, w_ref, b_ref, o_ref, *, lat_len):
    p = cg_ref[0] * xv_ref[0]
    pm, pp = _shift_rows(p, lat_len)
    o_ref[0] = bg_ref[0] * (pm * w_ref[0:1, :] + p * w_ref[1:2, :] + pp * w_ref[2:3, :] + b_ref[...])


def _shortconv(z, w, b, nb, lat_len):
    m = z.shape[0]
    t = m // nb
    nct = BRANCH_W // LANE
    kern = functools.partial(_shortconv_kernel, lat_len=lat_len)
    seq = lambda off: pl.BlockSpec((1, t, LANE), lambda bb, j: (bb, 0, off + j))
    out = pl.pallas_call(
        kern,
        grid=(nb, nct),
        in_specs=[seq(0), seq(nct), seq(2 * nct),
                  pl.BlockSpec((3, LANE), lambda bb, j: (0, j)),
                  pl.BlockSpec((1, LANE), lambda bb, j: (0, j))],
        out_specs=seq(0),
        out_shape=jax.ShapeDtypeStruct((nb, t, BRANCH_W), F32),
        compiler_params=_cp(("parallel", "parallel")),
        name="short_conv",
    )(z.reshape(nb, t, -1), z.reshape(nb, t, -1), z.reshape(nb, t, -1), w, b.reshape(1, -1))
    return out.reshape(m, -1)


def _hyconv_kernel(z_ref, w_ref, b_ref, ol_ref, oc_ref, *, lat_len):
    x = z_ref[0]
    xm, xp = _shift_rows(x, lat_len)
    y = xm * w_ref[0:1, :] + x * w_ref[1:2, :] + xp * w_ref[2:3, :] + b_ref[...]
    ol_ref[0, 0] = y[:lat_len]
    oc_ref[0, 0] = y[lat_len:]


def _hyconv(z, w, b, nb, lat_len):
    m = z.shape[0]
    t = m // nb
    nct = BRANCH_W // LANE
    off = (A_W + R_W) // LANE
    kern = functools.partial(_hyconv_kernel, lat_len=lat_len)
    return pl.pallas_call(
        kern,
        grid=(nb, 3 * nct),
        in_specs=[pl.BlockSpec((1, t, LANE), lambda bb, j: (bb, 0, off + j)),
                  pl.BlockSpec((3, LANE), lambda bb, j: (0, j)),
                  pl.BlockSpec((1, LANE), lambda bb, j: (0, j))],
        out_specs=[pl.BlockSpec((1, 1, lat_len, LANE), lambda bb, j: (j // nct, bb, 0, j % nct)),
                   pl.BlockSpec((1, 1, t - lat_len, LANE), lambda bb, j: (j // nct, bb, 0, j % nct))],
        out_shape=[jax.ShapeDtypeStruct((3, nb, lat_len, BRANCH_W), F32),
                   jax.ShapeDtypeStruct((3, nb, t - lat_len, BRANCH_W), F32)],
        compiler_params=_cp(("parallel", "parallel")),
        name="hyena_dwconv",
    )(z.reshape(nb, t, -1), w, b.reshape(1, -1))


def _stack_c(m):
    return np.block([[m.real, -m.imag], [m.imag, m.real]])


def _split_np(m):
    m32 = np.asarray(m, np.float32)
    hi = m32.astype(BF16)
    lo = (m32 - hi.astype(np.float32)).astype(BF16)
    return np.concatenate([hi, lo], axis=-2)


@functools.lru_cache(maxsize=None)
def _dft_tables(lat_len):
    n = 2 * lat_len
    n1, n2 = FFT_N1, 2 * lat_len // FFT_N1
    a1 = np.arange(n1)
    a2 = np.arange(n2)
    w1 = np.exp(-2j * np.pi * np.outer(a1, a1[:n1 // 2]) / n1)
    f1 = _stack_c(w1)
    w2 = np.exp(-2j * np.pi * np.outer(a2, a2) / n2)
    tw = np.exp(-2j * np.pi * np.outer(a1, a2) / n)
    cf = w2[None, :, :] * tw[:, None, :]
    f2 = np.stack([_stack_c(cf[k]) for k in range(n1)])
    g2 = np.stack([_stack_c(cf[k].conj().T) for k in range(n1)])
    w1i = np.exp(2j * np.pi * np.outer(a1[:n1 // 2], a1) / n1) / n
    f3 = _stack_c(w1i)
    f1_real = f1[:, :n1 // 2]
    return _split_np(f1), _split_np(f1_real), _split_np(f2), _split_np(g2), _split_np(f3)


@functools.lru_cache(maxsize=None)
def _dft_tables_small(ctx_len):
    n = 2 * ctx_len
    a = np.arange(n)
    ff = _stack_c(np.exp(-2j * np.pi * np.outer(a, a[:ctx_len]) / n))
    fi = _stack_c(np.exp(2j * np.pi * np.outer(a[:ctx_len], a) / n) / n)
    return _split_np(ff), _split_np(ff[:, :ctx_len]), _split_np(fi)


def _fft1_kernel(x_ref, f_ref, o_ref):
    f = f_ref[...]
    for j in range(x_ref.shape[3]):
        parts = [x_ref[0, b, :, j, :] for b in range(x_ref.shape[1])]
        r = _split_dot(f, parts[0] if len(parts) == 1 else jnp.concatenate(parts, axis=0))
        h = r.shape[0] // 2
        o_ref[0, 0, :, j, :] = r[:h]
        o_ref[0, 1, :, j, :] = r[h:]


def _fft1(x, f1, jb):
    p, mem, r, n2, w = x.shape
    n1 = f1.shape[0] // 4
    return pl.pallas_call(
        _fft1_kernel,
        grid=(p, n2 // jb),
        in_specs=[pl.BlockSpec((1, mem, r, jb, w), lambda pp, j: (pp, 0, 0, j, 0)),
                  pl.BlockSpec(f1.shape, lambda pp, j: (0, 0))],
        out_specs=pl.BlockSpec((1, 2, n1, jb, w), lambda pp, j: (pp, 0, 0, j, 0)),
        out_shape=jax.ShapeDtypeStruct((p, 2, n1, n2, w), F32),
        compiler_params=_cp(("parallel", "parallel")),
        name="hyena_fft_stage1",
    )(x, f1)


def _fft2f_kernel(a_ref, f_ref, o_ref, *, kb, n2):
    w = a_ref.shape[-1]
    for t in range(kb):
        xf = _split_dot(f_ref[t], a_ref[0, :, t].reshape(2 * n2, w))
        xb = _split_dot(f_ref[t], a_ref[1, :, t].reshape(2 * n2, w))
        o_ref[0, 0, t] = xf[:n2] + xb[:n2]
        o_ref[0, 1, t] = xf[n2:] - xb[n2:]


def _fft2_filters(a, f2, kb):
    q, _, n1, n2, w = a.shape
    return pl.pallas_call(
        functools.partial(_fft2f_kernel, kb=kb, n2=n2),
        grid=(q // 2, n1 // kb),
        in_specs=[pl.BlockSpec((2, 2, kb, n2, w), lambda o, k: (o, 0, k, 0, 0)),
                  pl.BlockSpec((kb, 4 * n2, 2 * n2), lambda o, k: (k, 0, 0))],
        out_specs=pl.BlockSpec((1, 2, kb, n2, w), lambda o, k: (o, 0, k, 0, 0)),
        out_shape=jax.ShapeDtypeStruct((q // 2, 2, n1, n2, w), F32),
        compiler_params=_cp(("parallel", "parallel")),
        name="hyena_filter_spectrum",
    )(a, f2)


def _fft2_kernel(a_ref, h_ref, f_ref, g_ref, o_ref, *, kb, n2):
    w = a_ref.shape[-1]
    for t in range(kb):
        a = a_ref[0, :, t].reshape(2 * n2, w)
        x = _split_dot(f_ref[t], a)
        xr, xi = x[:n2], x[n2:]
        hr, hi = h_ref[0, t], h_ref[1, t]
        y = jnp.concatenate([xr * hr - xi * hi, xr * hi + xi * hr], axis=0)
        o_ref[0, :, t] = _split_dot(g_ref[t], y).reshape(2, n2, w)


def _fft2(a, hspec, f2, g2, kb):
    p, _, n1, n2, w = a.shape
    kern = functools.partial(_fft2_kernel, kb=kb, n2=n2)
    return pl.pallas_call(
        kern,
        grid=(p, n1 // kb),
        in_specs=[pl.BlockSpec((1, 2, kb, n2, w), lambda pp, k: (pp, 0, k, 0, 0)),
                  pl.BlockSpec((2, kb, n2, w), lambda pp, k: (0, k, 0, 0)),
                  pl.BlockSpec((kb, 4 * n2, 2 * n2), lambda pp, k: (k, 0, 0)),
                  pl.BlockSpec((kb, 4 * n2, 2 * n2), lambda pp, k: (k, 0, 0))],
        out_specs=pl.BlockSpec((1, 2, kb, n2, w), lambda pp, k: (pp, 0, k, 0, 0)),
        out_shape=jax.ShapeDtypeStruct(a.shape, F32),
        compiler_params=_cp(("parallel", "parallel")),
        name="hyena_fft_stage2",
    )(a, hspec, f2, g2)


def _fft3_kernel(b_ref, f_ref, gate_ref, prev_ref, skip_ref, o_ref):
    f = f_ref[...]
    r = gate_ref.shape[2]
    for j in range(b_ref.shape[3]):
        bj = jnp.concatenate([b_ref[0, 0, :, j, :], b_ref[0, 1, :, j, :]], axis=0)
        conv = _split_dot(f, bj)
        for mem in range(2):
            cm = conv[mem * r:(mem + 1) * r]
            o_ref[0, mem, :, j, :] = gate_ref[0, mem, :, j, :] * (cm + skip_ref[...] * prev_ref[0, mem, :, j, :])


def _fft3(bm, f3, gate, prev, skip, out_rows, jb):
    p, _, n1, n2, w = bm.shape
    r = n1 // 2
    seq = pl.BlockSpec((1, 2, r, jb, w), lambda pp, j: (pp, 0, 0, j, 0))
    return pl.pallas_call(
        _fft3_kernel,
        grid=(p, n2 // jb),
        in_specs=[pl.BlockSpec((1, 2, n1, jb, w), lambda pp, j: (pp, 0, 0, j, 0)),
                  pl.BlockSpec(f3.shape, lambda pp, j: (0, 0)),
                  seq, seq,
                  pl.BlockSpec((1, w), lambda pp, j: (0, 0))],
        out_specs=seq,
        out_shape=jax.ShapeDtypeStruct((p, 2, out_rows, n2, w), F32),
        compiler_params=_cp(("parallel", "parallel")),
        name="hyena_fft_stage3",
    )(bm, f3, gate, prev, skip)


def _hyctx_kernel(v_ref, x1_ref, x2_ref, filt_ref, ff_ref, ffr_ref, fi_ref, skip_ref, yh_hbm_ref, o_ref):
    del yh_hbm_ref
    lc, w = v_ref.shape[2], v_ref.shape[3]
    n = 2 * lc
    y = v_ref[0].reshape(2 * lc, w)
    for o, gate_ref in enumerate((x1_ref, x2_ref)):
        sf = _split_dot(ffr_ref[...], filt_ref[o, 0])
        sb = _split_dot(ffr_ref[...], filt_ref[o, 1])
        hr, hi = sf[:n] + sb[:n], sf[n:] - sb[n:]
        x = _split_dot(ff_ref[...], y)
        xr, xi = x[:n], x[n:]
        spec = jnp.concatenate([xr * hr - xi * hi, xr * hi + xi * hr], axis=0)
        conv = _split_dot(fi_ref[...], spec)
        y = gate_ref[0].reshape(2 * lc, w) * (conv + skip_ref[o:o + 1, :] * y)
    o_ref[0] = y.reshape(2, lc, w)


def _hyena_ctx(hc, filt_c, ff, ffr, fi, skip, yh):
    _, nb, lc, w = hc.shape
    p = nb // 2
    t = yh.shape[2]
    part = pl.BlockSpec((1, 2, lc, w), lambda pp: (pp, 0, 0, 0))
    hc4 = hc.reshape(3, p, 2, lc, w)
    return pl.pallas_call(
        _hyctx_kernel,
        grid=(p,),
        in_specs=[part, part, part,
                  pl.BlockSpec(filt_c.shape, lambda pp: (0, 0, 0, 0)),
                  pl.BlockSpec(ff.shape, lambda pp: (0, 0)),
                  pl.BlockSpec(ffr.shape, lambda pp: (0, 0)),
                  pl.BlockSpec(fi.shape, lambda pp: (0, 0)),
                  pl.BlockSpec(skip.shape, lambda pp: (0, 0)),
                  pl.BlockSpec(memory_space=pl.ANY)],
        out_specs=pl.BlockSpec((1, 2, lc, w), lambda pp: (pp, 0, t // lc - 1, 0)),
        out_shape=jax.ShapeDtypeStruct(yh.shape, F32),
        input_output_aliases={8: 0},
        compiler_params=_cp(("parallel",)),
        name="hyena_ctx",
    )(hc4[0], hc4[1], hc4[2], filt_c, ff, ffr, fi, skip, yh)


def _hyena_filters(l, w1, b1, w2, b2, w3, freq):
    t = jnp.linspace(0.0, 1.0, l, dtype=F32)[:, None]
    bands = (HY_EMB - 1) // 2
    w = 2.0 * math.pi * jnp.arange(l, dtype=F32) / l
    f = jnp.linspace(1e-4, bands - 1, bands, dtype=F32)
    ang = w[:, None] * f[None, :]
    feats = jnp.concatenate([t, jnp.cos(ang), -jnp.sin(ang)], axis=-1)
    hdn = jnp.sin(freq[0] * (feats @ w1 + b1))
    hdn = jnp.sin(freq[1] * (hdn @ w2 + b2))
    filt = (hdn @ w3).astype(F32)
    n_out = filt.shape[-1]
    deltas = jnp.linspace(math.log(HY_TARGET) / HY_SLOW_PCT, math.log(HY_TARGET) / HY_FAST_PCT, n_out, dtype=F32)
    filt = filt * jnp.exp(-t * jnp.abs(deltas))
    filt = filt.reshape(l, 2, 2, BRANCH_W).transpose(1, 2, 0, 3)
    keep = jnp.ones((1, 2, l, 1), F32).at[0, 1, 0, 0].set(0.0)
    return filt * keep


def _hyena(z, conv_w, conv_b, filt_w, skip, nb, lat_len):
    m = z.shape[0]
    t = m // nb
    lc = t - lat_len
    w = BRANCH_W
    p = nb // 2
    n1, n2 = FFT_N1, 2 * lat_len // FFT_N1
    r = n1 // 2
    hl, hc = _hyconv(z, conv_w, conv_b, nb, lat_len)
    f1, f1r, f2, g2, f3 = _dft_tables(lat_len)
    ff, ffr, fi = _dft_tables_small(lc)
    jb, kb = 8, 8
    filt_l = _hyena_filters(lat_len, *filt_w)
    fa = _fft1(filt_l.reshape(4, 1, r, n2, w), f1r, jb)
    hs_l = _fft2_filters(fa, f2, kb)
    seq = lambda a: a.reshape(p, 2, r, n2, w)
    y = seq(hl[0])
    for o in range(2):
        a = _fft1(y, f1, jb)
        bm = _fft2(a, hs_l[o], f2, g2, kb)
        rows = r if o == 0 else t // n2
        y = _fft3(bm, f3, seq(hl[1 + o]), y, skip[o:o + 1], rows, jb)
    yh = y.reshape(p, 2, t, w)
    yh = _hyena_ctx(hc, _hyena_filters(lc, *filt_w), ff, ffr, fi, skip, yh)
    return yh.reshape(m, w)


def _merge_kernel(x_ref, mod_ref, ya_ref, yr_ref, yh_ref, yd_ref, wg_ref, bg_ref, wb_ref, wo_ref, g_ref, b_ref,
                  o_ref, *, alpha):
    x = x_ref[...]
    u = _modulate(x, mod_ref, 0, 1).astype(BF16)
    acc = None
    for i, y_ref in enumerate((ya_ref, yr_ref, yh_ref, yd_ref)):
        gate = jax.nn.sigmoid(_dot(u, wg_ref[i]) + bg_ref[i:i + 1, :])
        term = gate * _dot(y_ref[...].astype(BF16), wb_ref[i])
        acc = term if acc is None else acc + term
    mix = _dot(acc.astype(BF16), wo_ref[...])
    o_ref[...] = _gated_residual_ln(x, mix, mod_ref, 2, alpha, g_ref[...], b_ref[...])


def _merge(x, modx, ys, wg, bg, wb, wo, g, b, alpha, tm):
    m, d = x.shape
    w = BRANCH_W
    row = lambda i: (i, 0)
    const2 = lambda i: (0, 0)
    const3 = lambda i: (0, 0, 0)
    one = pl.Buffered(1)
    return pl.pallas_call(
        functools.partial(_merge_kernel, alpha=alpha),
        grid=(m // tm,),
        in_specs=[pl.BlockSpec((tm, d), row),
                  pl.BlockSpec((tm // SUB, 8, d), lambda i: (i, 0, 0)),
                  pl.BlockSpec((tm, w), row), pl.BlockSpec((tm, w), row),
                  pl.BlockSpec((tm, w), row), pl.BlockSpec((tm, w), row),
                  pl.BlockSpec(wg.shape, const3, pipeline_mode=one), pl.BlockSpec(bg.shape, const2),
                  pl.BlockSpec(wb.shape, const3, pipeline_mode=one),
                  pl.BlockSpec(wo.shape, const2, pipeline_mode=one),
                  pl.BlockSpec((1, d), const2), pl.BlockSpec((1, d), const2)],
        out_specs=pl.BlockSpec((tm, d), row),
        out_shape=jax.ShapeDtypeStruct((m, d), F32),
        compiler_params=_cp(("parallel",)),
        name="branch_merge",
    )(x, modx, *ys, wg, bg, wb, wo, g.reshape(1, d), b.reshape(1, d))


def _ffn_kernel(x_ref, mod_ref, w1_ref, w3_ref, w2_ref, g_ref, b_ref, o_ref, *, alpha, tf):
    x = x_ref[...]
    u = _modulate(x, mod_ref, 3, 4).astype(BF16)
    acc = None
    for j in range(w1_ref.shape[1] // tf):
        sl = slice(j * tf, (j + 1) * tf)
        h1 = _dot(u, w1_ref[:, sl])
        h3 = _dot(u, w3_ref[:, sl])
        part = _dot((h1 * jax.nn.sigmoid(h1) * h3).astype(BF16), w2_ref[sl, :])
        acc = part if acc is None else acc + part
    o_ref[...] = _gated_residual_ln(x, acc, mod_ref, 5, alpha, g_ref[...], b_ref[...])


def _ffn(x, modx, w1, w3, w2, g, b, alpha, tm, tf):
    m, d = x.shape
    resident = lambda a: pl.BlockSpec(a.shape, lambda i: (0, 0), pipeline_mode=pl.Buffered(1))
    return pl.pallas_call(
        functools.partial(_ffn_kernel, alpha=alpha, tf=tf),
        grid=(m // tm,),
        in_specs=[pl.BlockSpec((tm, d), lambda i: (i, 0)),
                  pl.BlockSpec((tm // SUB, 8, d), lambda i: (i, 0, 0)),
                  resident(w1), resident(w3), resident(w2),
                  pl.BlockSpec((1, d), lambda i: (0, 0)),
                  pl.BlockSpec((1, d), lambda i: (0, 0))],
        out_specs=pl.BlockSpec((tm, d), lambda i: (i, 0)),
        out_shape=jax.ShapeDtypeStruct((m, d), F32),
        compiler_params=_cp(("parallel",)),
        name="dense_swiglu",
    )(x, modx, w1, w3, w2, g.reshape(1, d), b.reshape(1, d))


def _route_kernel(x_ref, mod_ref, wr_ref, u_ref, lg_ref):
    u = _modulate(x_ref[...], mod_ref, 3, 4)
    u_ref[...] = u.astype(BF16)
    lg_ref[...] = jnp.dot(u, wr_ref[...], preferred_element_type=F32, precision=lax.Precision.HIGHEST)


def _route(x, modx, wr_pad, tm):
    m, d = x.shape
    return pl.pallas_call(
        _route_kernel,
        grid=(m // tm,),
        in_specs=[pl.BlockSpec((tm, d), lambda i: (i, 0)),
                  pl.BlockSpec((tm // SUB, 8, d), lambda i: (i, 0, 0)),
                  pl.BlockSpec((d, LANE), lambda i: (0, 0))],
        out_specs=[pl.BlockSpec((tm, d), lambda i: (i, 0)), pl.BlockSpec((tm, LANE), lambda i: (i, 0))],
        out_shape=[jax.ShapeDtypeStruct((m, d), BF16), jax.ShapeDtypeStruct((m, LANE), F32)],
        compiler_params=_cp(("parallel",)),
        name="moe_router",
    )(x, modx, wr_pad)


def _moe_kernel(be_ref, nu_ref, xb_ref, w1_ref, w3_ref, w2_ref, o_ref, acc_ref):
    i, j = pl.program_id(0), pl.program_id(1)
    live = i < nu_ref[0]

    @pl.when(j == 0)
    def _():
        acc_ref[...] = jnp.zeros_like(acc_ref)

    @pl.when(live)
    def _():
        u = xb_ref[...]
        h1 = _dot(u, w1_ref[0])
        h3 = _dot(u, w3_ref[0])
        acc_ref[...] += _dot((h1 * jax.nn.sigmoid(h1) * h3).astype(BF16), w2_ref[0])

    @pl.when(j == pl.num_programs(1) - 1)
    def _():
        o_ref[...] = acc_ref[...]


def _moe_experts(xb, blk_e, n_used, w1, w3, w2, tm, tf):
    r, d = xb.shape
    ff = w1.shape[2]
    nj = ff // tf

    def wmap(i, j, be, nu):
        return (be[i], 0, jnp.where(i < nu[0], j, nj - 1))

    def w2map(i, j, be, nu):
        return (be[i], jnp.where(i < nu[0], j, nj - 1), 0)

    grid_spec = pltpu.PrefetchScalarGridSpec(
        num_scalar_prefetch=2,
        grid=(r // tm, nj),
        in_specs=[pl.BlockSpec((tm, d), lambda i, j, be, nu: (i, 0)),
                  pl.BlockSpec((1, d, tf), wmap),
                  pl.BlockSpec((1, d, tf), wmap),
                  pl.BlockSpec((1, tf, d), w2map)],
        out_specs=pl.BlockSpec((tm, d), lambda i, j, be, nu: (i, 0)),
        scratch_shapes=[pltpu.VMEM((tm, d), F32)],
    )
    return pl.pallas_call(
        _moe_kernel,
        grid_spec=grid_spec,
        out_shape=jax.ShapeDtypeStruct((r, d), F32),
        compiler_params=_cp(("arbitrary", "arbitrary")),
        name="moe_experts",
    )(blk_e, n_used, xb, w1, w3, w2)


def _ln_res_kernel(x_ref, f0_ref, f1_ref, wt_ref, mod_ref, g_ref, b_ref, o_ref, *, alpha):
    f = f0_ref[...] * wt_ref[:, 0:1] + f1_ref[...] * wt_ref[:, 1:2]
    o_ref[...] = _gated_residual_ln(x_ref[...], f, mod_ref, 5, alpha, g_ref[...], b_ref[...])


def _ln_res(x, f01, wts, modx, g, b, alpha, tm):
    m, d = x.shape
    nt = m // tm
    return pl.pallas_call(
        functools.partial(_ln_res_kernel, alpha=alpha),
        grid=(nt,),
        in_specs=[pl.BlockSpec((tm, d), lambda i: (i, 0)),
                  pl.BlockSpec((tm, d), lambda i: (i, 0)),
                  pl.BlockSpec((tm, d), lambda i: (i + nt, 0)),
                  pl.BlockSpec((tm, TOP_K), lambda i: (i, 0)),
                  pl.BlockSpec((tm // SUB, 8, d), lambda i: (i, 0, 0)),
                  pl.BlockSpec((1, d), lambda i: (0, 0)), pl.BlockSpec((1, d), lambda i: (0, 0))],
        out_specs=pl.BlockSpec((tm, d), lambda i: (i, 0)),
        out_shape=jax.ShapeDtypeStruct((m, d), F32),
        compiler_params=_cp(("parallel",)),
        name="moe_residual_ln",
    )(x, f01, f01, wts, modx, g.reshape(1, d), b.reshape(1, d))


def _moe(x, modx, router, w1, w3, w2, g, b, alpha, tm_rows, tm, tf):
    m, d = x.shape
    wr_pad = jnp.zeros((d, LANE), F32).at[:, :N_EXPERTS].set(router)
    u, logits = _route(x, modx, wr_pad, tm_rows)
    top_v, top_i = lax.top_k(logits[:, :N_EXPERTS], TOP_K)
    wts = jax.nn.softmax(top_v, axis=-1)
    e_flat = top_i.reshape(-1).astype(jnp.int32)
    n_pair = m * TOP_K
    onehot = (e_flat[:, None] == jnp.arange(N_EXPERTS, dtype=jnp.int32)[None, :]).astype(jnp.int32)
    csum = jnp.cumsum(onehot, axis=0)
    counts = csum[-1]
    rank = jnp.take_along_axis(csum, e_flat[:, None], axis=1)[:, 0] - 1
    padded = (counts + tm - 1) // tm * tm
    pad_end = jnp.cumsum(padded)
    pad_start = pad_end - padded
    dest = pad_start[e_flat] + rank
    n_blk = -(-n_pair // tm) + N_EXPERTS
    r = n_blk * tm
    tok = jnp.arange(n_pair, dtype=jnp.int32) // TOP_K
    row_tok = jnp.zeros((r,), jnp.int32).at[dest].set(tok, unique_indices=True)
    n_used = (pad_end[-1] // tm).astype(jnp.int32)
    blk = jnp.minimum(jnp.arange(n_blk, dtype=jnp.int32), n_used - 1) * tm
    blk_e = jnp.minimum(jnp.searchsorted(pad_end, blk, side='right'), N_EXPERTS - 1).astype(jnp.int32)
    xb = jnp.take(u, row_tok, axis=0)
    yb = _moe_experts(xb, blk_e, n_used.reshape(1), w1, w3, w2, tm, tf)
    f01 = jnp.take(yb, dest.reshape(m, TOP_K).T.reshape(-1), axis=0)
    return _ln_res(x, f01, wts, modx, g, b, alpha, tm_rows)


def _rope_tables(nb, lat_len, ctx_len):
    rows = lat_len // GRID_W
    row = jnp.repeat(jnp.arange(rows, dtype=F32), GRID_W)
    col = jnp.tile(jnp.arange(GRID_W, dtype=F32), rows)
    n_freq = HEAD_DIM // 4
    inv = ROPE_THETA ** (-jnp.arange(n_freq, dtype=F32) / n_freq)
    ar, ac = row[:, None] * inv, col[:, None] * inv
    cos = jnp.concatenate([jnp.cos(ar), jnp.cos(ar), jnp.cos(ac), jnp.cos(ac)], axis=1)
    sin = jnp.concatenate([-jnp.sin(ar), jnp.sin(ar), -jnp.sin(ac), jnp.sin(ac)], axis=1)
    cos = jnp.concatenate([cos, jnp.ones((ctx_len, HEAD_DIM), F32)], axis=0)
    sin = jnp.concatenate([sin, jnp.zeros((ctx_len, HEAD_DIM), F32)], axis=0)
    return jnp.tile(cos, (nb, 1)), jnp.tile(sin, (nb, 1))


def _row_tile(m, pref):
    tm = pref
    while m % tm:
        tm //= 2
    return tm


def kernel(x, c, ctx, c_ctx, w_mod, b_mod, w_in, conv_a_w, conv_a_b, ret_decay, hy_conv_w, hy_conv_b, hy_w1, hy_b1, hy_w2, hy_b2, hy_w3, hy_freq, hy_skip, q_norm, k_norm, w_gate, b_gate, w_branch, w_o, ln_g, ln_b, ffn_w1, ffn_w3, ffn_w2, router, moe_w1, moe_w3, moe_w2):
    nb, lat_len, d = x.shape
    ctx_len = ctx.shape[1]
    depth = w_mod.shape[0]
    t = lat_len + ctx_len
    m = nb * t
    alpha = float((2 * depth) ** 0.25)
    assert lat_len % SUB == 0 and ctx_len % SUB == 0 and nb % 2 == 0

    act = jnp.concatenate([jax.nn.silu(c), jax.nn.silu(c_ctx)[None, :],
                           jnp.zeros((8 - nb - 1, d), F32)], axis=0)
    mods = _mod_all(act, w_mod, b_mod).reshape(depth, 8, N_MOD, d)
    mods = jnp.pad(mods, ((0, 0), (0, 0), (0, 8 - N_MOD), (0, 0)))
    tiles_per_batch = t // SUB
    tile_pos = np.arange(m // SUB) % tiles_per_batch
    tile_group = np.where(tile_pos < lat_len // SUB, np.arange(m // SUB) // tiles_per_batch, nb)
    cos_t, sin_t = _rope_tables(nb, lat_len, ctx_len)

    rows = jnp.concatenate([x, ctx], axis=1).reshape(m, d)
    tm = _row_tile(m, 512)
    for l in range(depth):
        modx = mods[l][tile_group]
        z = _proj(rows, modx, w_in[l].astype(BF16), tm, 1024)
        rq, rk, aq, ak, av = _prep(z, cos_t, sin_t, q_norm[l], k_norm[l], tm)
        y_d = _attention(aq, ak, av, nb, lat_len, SUB)
        log_gamma = -jnp.exp(ret_decay[l].astype(F32))
        y_r = _retention(rq, rk, z, _retention_tables(log_gamma), nb, lat_len)
        y_a = _shortconv(z, conv_a_w[l], conv_a_b[l], nb, lat_len)
        y_h = _hyena(z, hy_conv_w[l], hy_conv_b[l],
                     (hy_w1[l], hy_b1[l], hy_w2[l], hy_b2[l], hy_w3[l], hy_freq[l]), hy_skip[l], nb, lat_len)
        rows = _merge(rows, modx, (y_a, y_r, y_h, y_d), w_gate[l].astype(BF16), b_gate[l],
                      w_branch[l].astype(BF16), w_o[l].astype(BF16), ln_g[l, 0], ln_b[l, 0], alpha, tm)
        i = l // 2
        if l % 2 == 0:
            rows = _ffn(rows, modx, ffn_w1[i].astype(BF16), ffn_w3[i].astype(BF16), ffn_w2[i].astype(BF16),
                        ln_g[l, 1], ln_b[l, 1], alpha, tm, 256)
        else:
            rows = _moe(rows, modx, router[i], moe_w1[i].astype(BF16), moe_w3[i].astype(BF16),
                        moe_w2[i].astype(BF16), ln_g[l, 1], ln_b[l, 1], alpha, tm, 512, 512)
    return rows.reshape(nb, t, d)[:, :lat_len]
```

```python
import functools
import math

import numpy as np
import jax
import jax.numpy as jnp
from jax import lax
from jax.experimental import pallas as pl
from jax.experimental.pallas import tpu as pltpu

F32 = jnp.float32
BF16 = jnp.bfloat16

GRID_W = 64
BRANCH_W = 512
RET_HEADS = 4
RET_D = 128
RET_CHUNK = 128
HY_EMB = 33
HY_FAST_PCT = 0.3
HY_SLOW_PCT = 1.5
HY_TARGET = 1e-2
ATT_HEADS = 4
ATT_KV_HEADS = 2
HEAD_DIM = 128
ROPE_THETA = 10000.0
N_EXPERTS = 8
TOP_K = 2
N_MOD = 6
A_W = 3 * BRANCH_W
R_W = 4 * RET_HEADS * RET_D
H_W = 3 * BRANCH_W
D_W = (ATT_HEADS + 2 * ATT_KV_HEADS) * HEAD_DIM
IN_W = A_W + R_W + H_W + D_W
LN_EPS = 1e-6

SUB = 256
LANE = 128
FFT_N1 = 128
RET_HEADS_PER_STEP = 2
DFT_PASSES = 1
DFT_FILTER_PASSES = 2
VMEM_LIMIT = 56 * 1024 * 1024


def _cp(sem, vmem=VMEM_LIMIT):
    return pltpu.CompilerParams(dimension_semantics=sem, vmem_limit_bytes=vmem)


def _dot(a, b):
    return jnp.dot(a, b, preferred_element_type=F32)


def _dot_nt(a, b):
    return lax.dot_general(a, b, (((1,), (1,)), ((), ())), preferred_element_type=F32)


def _dot_tn(a, b):
    return lax.dot_general(a, b, (((0,), (0,)), ((), ())), preferred_element_type=F32)


def _split_dot(f_cat, x, passes):
    xh = x.astype(BF16)
    if passes == 1:
        return _dot(f_cat, xh)
    m = f_cat.shape[0] // 2
    r = _dot(f_cat, xh)
    r = r[:m] + r[m:]
    if passes == 3:
        r = r + _dot(f_cat[:m], (x - xh.astype(F32)).astype(BF16))
    return r


def _dft_operand(f_cat, passes):
    return f_cat[..., :f_cat.shape[-2] // 2, :] if passes == 1 else f_cat


def _modulate(x, mod_ref, shift_row, scale_row):
    parts = []
    for s in range(x.shape[0] // SUB):
        sh = mod_ref[s, shift_row:shift_row + 1, :]
        sc = mod_ref[s, scale_row:scale_row + 1, :]
        parts.append(x[s * SUB:(s + 1) * SUB] * (1.0 + sc) + sh)
    return parts[0] if len(parts) == 1 else jnp.concatenate(parts, axis=0)


def _gated_residual_ln(x, f, mod_ref, gate_row, alpha, g, b):
    parts = []
    for s in range(x.shape[0] // SUB):
        gt = mod_ref[s, gate_row:gate_row + 1, :]
        parts.append(alpha * x[s * SUB:(s + 1) * SUB] + gt * f[s * SUB:(s + 1) * SUB])
    r = parts[0] if len(parts) == 1 else jnp.concatenate(parts, axis=0)
    mu = jnp.mean(r, axis=-1, keepdims=True)
    rc = r - mu
    var = jnp.mean(rc * rc, axis=-1, keepdims=True)
    return rc * lax.rsqrt(var + LN_EPS) * g + b


def _mod_kernel(a_ref, w_ref, b_ref, o_ref):
    o_ref[0] = _dot(a_ref[...].astype(BF16), w_ref[0].astype(BF16)) + b_ref[0]


def _mod_all(act, w_mod, b_mod):
    depth, d, n = w_mod.shape
    tn = 512
    return pl.pallas_call(
        _mod_kernel,
        grid=(depth, n // tn),
        in_specs=[pl.BlockSpec((8, d), lambda l, j: (0, 0)),
                  pl.BlockSpec((1, d, tn), lambda l, j: (l, 0, j)),
                  pl.BlockSpec((1, 1, tn), lambda l, j: (l, 0, j))],
        out_specs=pl.BlockSpec((1, 8, tn), lambda l, j: (l, 0, j)),
        out_shape=jax.ShapeDtypeStruct((depth, 8, n), F32),
        compiler_params=_cp(("parallel", "parallel")),
        name="mod_vectors",
    )(act, w_mod, b_mod.reshape(depth, 1, n))


def _proj_kernel(x_ref, mod_ref, w_ref, o_ref, u_ref, *, tn):
    j = pl.program_id(1)

    @pl.when(j == 0)
    def _():
        u_ref[...] = _modulate(x_ref[...], mod_ref, 0, 1).astype(BF16)
    col = pl.multiple_of(j * tn, tn)
    o_ref[...] = _dot(u_ref[...], w_ref[:, pl.ds(col, tn)])


def _proj(x, modx, w, tm, tn):
    m, d = x.shape
    n = w.shape[1]
    return pl.pallas_call(
        functools.partial(_proj_kernel, tn=tn),
        grid=(m // tm, n // tn),
        in_specs=[pl.BlockSpec((tm, d), lambda i, j: (i, 0)),
                  pl.BlockSpec((tm // SUB, 8, d), lambda i, j: (i, 0, 0)),
                  pl.BlockSpec((d, n), lambda i, j: (0, 0), pipeline_mode=pl.Buffered(1))],
        out_specs=pl.BlockSpec((tm, tn), lambda i, j: (i, j)),
        out_shape=jax.ShapeDtypeStruct((m, n), F32),
        scratch_shapes=[pltpu.VMEM((tm, d), BF16)],
        compiler_params=_cp(("parallel", "arbitrary")),
        name="in_proj",
    )(x, modx, w)


def _prep_kernel(rq_ref, rk_ref, aq_ref, ak_ref, av_ref, cos_ref, sin_ref, gq_ref, gk_ref,
                 orq_ref, ork_ref, oaq_ref, oak_ref, oav_ref):
    c = cos_ref[...]
    s = sin_ref[...]
    lane = lax.broadcasted_iota(jnp.int32, c.shape, 1)
    first = (lane % 64) < 32

    def rope(x):
        partner = jnp.where(first, pltpu.roll(x, 96, 1), pltpu.roll(x, 32, 1))
        return x * c + partner * s

    def rms(x, g):
        return x * lax.rsqrt(jnp.mean(x * x, axis=-1, keepdims=True) + LN_EPS) * g

    for h in range(RET_HEADS):
        sl = slice(h * LANE, (h + 1) * LANE)
        orq_ref[:, sl] = rope(rq_ref[:, sl])
        ork_ref[:, sl] = rope(rk_ref[:, sl] * (RET_D ** -0.5))
    q_scale = (HEAD_DIM ** -0.5) * math.log2(math.e)
    for h in range(ATT_HEADS):
        sl = slice(h * LANE, (h + 1) * LANE)
        oaq_ref[:, sl] = (rope(rms(aq_ref[:, sl], gq_ref[...])) * q_scale).astype(BF16)
    ones_col = jnp.where(lane == 0, 1.0, 0.0).astype(BF16)
    for h in range(ATT_KV_HEADS):
        sl = slice(h * LANE, (h + 1) * LANE)
        oak_ref[:, sl] = rope(rms(ak_ref[:, sl], gk_ref[...])).astype(BF16)
        oav_ref[:, 2 * h * LANE:(2 * h + 1) * LANE] = av_ref[:, sl].astype(BF16)
        oav_ref[:, (2 * h + 1) * LANE:(2 * h + 2) * LANE] = ones_col


def _prep(z, cos_t, sin_t, gq, gk, tm):
    m = z.shape[0]
    qw, kw = ATT_HEADS * HEAD_DIM, ATT_KV_HEADS * HEAD_DIM
    rw = RET_HEADS * RET_D
    d_off = A_W + R_W + H_W
    row = lambda i: (i, 0)
    return pl.pallas_call(
        _prep_kernel,
        grid=(m // tm,),
        in_specs=[pl.BlockSpec((tm, rw), lambda i: (i, A_W // rw)),
                  pl.BlockSpec((tm, rw), lambda i: (i, A_W // rw + 1)),
                  pl.BlockSpec((tm, qw), lambda i: (i, d_off // qw)),
                  pl.BlockSpec((tm, kw), lambda i: (i, (d_off + qw) // kw)),
                  pl.BlockSpec((tm, kw), lambda i: (i, (d_off + qw) // kw + 1)),
                  pl.BlockSpec((tm, LANE), row),
                  pl.BlockSpec((tm, LANE), row),
                  pl.BlockSpec((1, LANE), lambda i: (0, 0)),
                  pl.BlockSpec((1, LANE), lambda i: (0, 0))],
        out_specs=[pl.BlockSpec((tm, rw), row), pl.BlockSpec((tm, rw), row),
                   pl.BlockSpec((tm, qw), row), pl.BlockSpec((tm, kw), row), pl.BlockSpec((tm, 2 * kw), row)],
        out_shape=[jax.ShapeDtypeStruct((m, rw), F32), jax.ShapeDtypeStruct((m, rw), F32),
                   jax.ShapeDtypeStruct((m, qw), BF16), jax.ShapeDtypeStruct((m, kw), BF16),
                   jax.ShapeDtypeStruct((m, 2 * kw), BF16)],
        compiler_params=_cp(("parallel",)),
        name="qk_prep",
    )(z, z, z, z, z, cos_t, sin_t, gq.reshape(1, LANE), gk.reshape(1, LANE))


def _attn_kernel(q_ref, k_ref, v_ref, o_ref, *, n_lat_tiles, lat_len):
    i = pl.program_id(2)
    grp = ATT_HEADS // ATT_KV_HEADS

    def attend(k, v):
        for g in range(grp):
            q = q_ref[0, :, g * HEAD_DIM:(g + 1) * HEAD_DIM]
            s = _dot_nt(q, k)
            p = jnp.exp2(s - jnp.max(s, axis=-1, keepdims=True)).astype(BF16)
            acc = _dot(p, v)
            o_ref[0, :, g * HEAD_DIM:(g + 1) * HEAD_DIM] = acc[:, :HEAD_DIM] / acc[:, HEAD_DIM:HEAD_DIM + 1]

    @pl.when(i < n_lat_tiles)
    def _():
        attend(k_ref[0], v_ref[0])

    @pl.when(i >= n_lat_tiles)
    def _():
        attend(k_ref[0, lat_len:, :], v_ref[0, lat_len:, :])


def _attention(aq, ak, av, nb, lat_len, tq):
    m = aq.shape[0]
    t = m // nb
    grp_w = (ATT_HEADS // ATT_KV_HEADS) * HEAD_DIM
    kern = functools.partial(_attn_kernel, n_lat_tiles=lat_len // tq, lat_len=lat_len)
    out = pl.pallas_call(
        kern,
        grid=(nb, ATT_KV_HEADS, t // tq),
        in_specs=[pl.BlockSpec((1, tq, grp_w), lambda b, kv, i: (b, i, kv)),
                  pl.BlockSpec((1, t, HEAD_DIM), lambda b, kv, i: (b, 0, kv)),
                  pl.BlockSpec((1, t, 2 * HEAD_DIM), lambda b, kv, i: (b, 0, kv))],
        out_specs=pl.BlockSpec((1, tq, grp_w), lambda b, kv, i: (b, i, kv)),
        out_shape=jax.ShapeDtypeStruct((nb, t, ATT_HEADS * HEAD_DIM), F32),
        compiler_params=_cp(("parallel", "parallel", "arbitrary")),
        name="gqa_attention",
    )(aq.reshape(nb, t, -1), ak.reshape(nb, t, -1), av.reshape(nb, t, -1))
    return out.reshape(m, -1)


def _ret_kernel(q_ref, k_ref, v_ref, g_ref, intra_ref, qd_ref, kd_ref, cd_ref, o_ref, ob_ref,
                *, n_lat, n_ctx):
    c_len = RET_CHUNK
    n = n_lat + n_ctx

    hps = q_ref.shape[2] // LANE

    def chunk(ref, c, h):
        return ref[0, pl.ds(pl.multiple_of(c * c_len, c_len), c_len), h * LANE:(h + 1) * LANE]

    def step(d, c, h, state):
        qc, kc, vc = chunk(q_ref, c, h), chunk(k_ref, c, h), chunk(v_ref, c, h).astype(BF16)
        s = _dot_nt(qc.astype(BF16), kc.astype(BF16)) * intra_ref[d, h]
        o = _dot(s.astype(BF16), vc) + _dot((qc * qd_ref[d, h]).astype(BF16), state.astype(BF16))
        state = cd_ref[d, h, 0:1, :] * state + _dot_tn((kc * kd_ref[d, h]).astype(BF16), vc)
        return o, state

    def body(t, carry):
        cf = jnp.where(t < n_ctx, n_lat + t, t - n_ctx)
        cb = jnp.where(t < n_ctx, n - 1 - t, n_lat - 1 - (t - n_ctx))
        new = []
        for h in range(hps):
            o_f, sf = step(0, cf, h, carry[2 * h])
            o_ref[0, pl.ds(pl.multiple_of(cf * c_len, c_len), c_len), h * LANE:(h + 1) * LANE] = o_f
            o_b, sb = step(1, cb, h, carry[2 * h + 1])
            ob_ref[pl.ds(pl.multiple_of(cb * c_len, c_len), c_len), h * LANE:(h + 1) * LANE] = o_b
            new += [sf, sb]
        return tuple(new)

    zero = jnp.zeros((RET_D, RET_D), F32)
    lax.fori_loop(0, n, body, (zero,) * (2 * hps))
    for h in range(hps):
        sl = slice(h * LANE, (h + 1) * LANE)
        o = o_ref[0, :, sl] + ob_ref[:, sl]
        mu = jnp.mean(o, axis=-1, keepdims=True)
        oc = o - mu
        var = jnp.mean(oc * oc, axis=-1, keepdims=True)
        g = g_ref[0, :, sl]
        o_ref[0, :, sl] = (g * jax.nn.sigmoid(g)) * (oc * lax.rsqrt(var + LN_EPS))


def _retention(rq, rk, z, tabs, nb, lat_len):
    m = rq.shape[0]
    t = m // nb
    intra, qd, kd, cd = tabs
    v_blk = (A_W + 2 * RET_HEADS * RET_D) // LANE
    g_blk = v_blk + RET_HEADS
    kern = functools.partial(_ret_kernel, n_lat=lat_len // RET_CHUNK, n_ctx=(t - lat_len) // RET_CHUNK)
    hps = RET_HEADS_PER_STEP
    hw = hps * LANE
    seq = lambda off: pl.BlockSpec((1, t, hw), lambda b, h: (b, 0, off // hps + h))
    tab = lambda r: pl.BlockSpec((2, hps, r, LANE), lambda b, h: (0, h, 0, 0))
    out = pl.pallas_call(
        kern,
        grid=(nb, RET_HEADS // hps),
        in_specs=[seq(0), seq(0), seq(v_blk),
                  pl.BlockSpec((1, t, hw), lambda b, h: (b, 0, g_blk // hps + h), pipeline_mode=pl.Buffered(1)),
                  tab(RET_CHUNK), tab(RET_CHUNK), tab(RET_CHUNK), tab(8)],
        out_specs=seq(0),
        out_shape=jax.ShapeDtypeStruct((nb, t, RET_HEADS * RET_D), F32),
        scratch_shapes=[pltpu.VMEM((t, hw), F32)],
        compiler_params=_cp(("parallel", "parallel")),
        name="retention",
    )(rq.reshape(nb, t, -1), rk.reshape(nb, t, -1), z.reshape(nb, t, -1), z.reshape(nb, t, -1), intra, qd, kd, cd)
    return out.reshape(m, -1)


def _retention_tables(log_gamma):
    c = RET_CHUNK
    i = jnp.arange(c, dtype=F32)
    lg = log_gamma.astype(F32)[:, :, None]
    rel = i[:, None] - i[None, :]
    intra_f = jnp.where(rel >= 0, jnp.exp(lg[0][..., None] * jnp.maximum(rel, 0.0)), 0.0)
    intra_b = jnp.where(rel <= 0, jnp.exp(lg[1][..., None] * jnp.maximum(-rel, 0.0)), 0.0)
    qd = jnp.stack([jnp.exp(lg[0] * (i + 1.0)), jnp.exp(lg[1] * (c - i))])
    kd = jnp.stack([jnp.exp(lg[0] * (c - 1.0 - i)), jnp.exp(lg[1] * i)])
    cd = jnp.exp(lg * c)
    bc = lambda a: jnp.broadcast_to(a[..., None], a.shape + (LANE,))
    return (jnp.stack([intra_f, intra_b]), bc(qd), bc(kd),
            jnp.broadcast_to(cd[..., None], (2, RET_HEADS, 8, LANE)))


def _shift_rows(x, lat_len):
    t = x.shape[0]
    row = lax.broadcasted_iota(jnp.int32, x.shape, 0)
    xm = jnp.where((row == 0) | (row == lat_len), 0.0, pltpu.roll(x, 1, 0))
    xp = jnp.where((row == lat_len - 1) | (row == t - 1), 0.0, pltpu.roll(x, t - 1, 0))
    return xm, xp


def _shortconv_kernel(bg_ref, cg_ref, xv_ref, w_ref, b_ref, o_ref, *, lat_len):
    p = cg_ref[0] * xv_ref[0]
    pm, pp = _shift_rows(p, lat_len)
    o_ref[0] = bg_ref[0] * (pm * w_ref[0:1, :] + p * w_ref[1:2, :] + pp * w_ref[2:3, :] + b_ref[...])


def _shortconv(z, w, b, nb, lat_len):
    m = z.shape[0]
    t = m // nb
    nct = BRANCH_W // LANE
    kern = functools.partial(_shortconv_kernel, lat_len=lat_len)
    seq = lambda off: pl.BlockSpec((1, t, LANE), lambda bb, j: (bb, 0, off + j))
    out = pl.pallas_call(
        kern,
        grid=(nb, nct),
        in_specs=[seq(0), seq(nct), seq(2 * nct),
                  pl.BlockSpec((3, LANE), lambda bb, j: (0, j)),
                  pl.BlockSpec((1, LANE), lambda bb, j: (0, j))],
        out_specs=seq(0),
        out_shape=jax.ShapeDtypeStruct((nb, t, BRANCH_W), F32),
        compiler_params=_cp(("parallel", "parallel")),
        name="short_conv",
    )(z.reshape(nb, t, -1), z.reshape(nb, t, -1), z.reshape(nb, t, -1), w, b.reshape(1, -1))
    return out.reshape(m, -1)


def _hyconv_kernel(z_ref, w_ref, b_ref, ol_ref, oc_ref, *, lat_len):
    x = z_ref[0]
    xm, xp = _shift_rows(x, lat_len)
    y = xm * w_ref[0:1, :] + x * w_ref[1:2, :] + xp * w_ref[2:3, :] + b_ref[...]
    ol_ref[0, 0] = y[:lat_len]
    oc_ref[0, 0] = y[lat_len:]


def _hyconv(z, w, b, nb, lat_len):
    m = z.shape[0]
    t = m // nb
    nct = BRANCH_W // LANE
    off = (A_W + R_W) // LANE
    kern = functools.partial(_hyconv_kernel, lat_len=lat_len)
    return pl.pallas_call(
        kern,
        grid=(nb, 3 * nct),
        in_specs=[pl.BlockSpec((1, t, LANE), lambda bb, j: (bb, 0, off + j)),
                  pl.BlockSpec((3, LANE), lambda bb, j: (0, j)),
                  pl.BlockSpec((1, LANE), lambda bb, j: (0, j))],
        out_specs=[pl.BlockSpec((1, 1, lat_len, LANE), lambda bb, j: (j // nct, bb, 0, j % nct)),
                   pl.BlockSpec((1, 1, t - lat_len, LANE), lambda bb, j: (j // nct, bb, 0, j % nct))],
        out_shape=[jax.ShapeDtypeStruct((3, nb, lat_len, BRANCH_W), F32),
                   jax.ShapeDtypeStruct((3, nb, t - lat_len, BRANCH_W), F32)],
        compiler_params=_cp(("parallel", "parallel")),
        name="hyena_dwconv",
    )(z.reshape(nb, t, -1), w, b.reshape(1, -1))


def _stack_c(m):
    return np.block([[m.real, -m.imag], [m.imag, m.real]])


def _split_np(m):
    m32 = np.asarray(m, np.float32)
    hi = m32.astype(BF16)
    lo = (m32 - hi.astype(np.float32)).astype(BF16)
    return np.concatenate([hi, lo], axis=-2)


@functools.lru_cache(maxsize=None)
def _dft_tables(lat_len):
    n = 2 * lat_len
    n1, n2 = FFT_N1, 2 * lat_len // FFT_N1
    a1 = np.arange(n1)
    a2 = np.arange(n2)
    w1 = np.exp(-2j * np.pi * np.outer(a1, a1[:n1 // 2]) / n1)
    f1 = _stack_c(w1)
    w2 = np.exp(-2j * np.pi * np.outer(a2, a2) / n2)
    tw = np.exp(-2j * np.pi * np.outer(a1, a2) / n)
    cf = w2[None, :, :] * tw[:, None, :]
    f2 = np.stack([_stack_c(cf[k]) for k in range(n1)])
    g2 = np.stack([_stack_c(cf[k].conj().T) for k in range(n1)])
    w1i = np.exp(2j * np.pi * np.outer(a1[:n1 // 2], a1) / n1) / n
    f3 = _stack_c(w1i)
    f1_real = f1[:, :n1 // 2]
    return _split_np(f1), _split_np(f1_real), _split_np(f2), _split_np(g2), _split_np(f3)


@functools.lru_cache(maxsize=None)
def _dft_tables_small(ctx_len):
    n = 2 * ctx_len
    a = np.arange(n)
    ff = _stack_c(np.exp(-2j * np.pi * np.outer(a, a[:ctx_len]) / n))
    fi = _stack_c(np.exp(2j * np.pi * np.outer(a[:ctx_len], a) / n) / n)
    return _split_np(ff), _split_np(ff[:, :ctx_len]), _split_np(fi)


def _fft1_kernel(x_ref, f_ref, o_ref, *, passes):
    f = f_ref[...]
    for j in range(x_ref.shape[3]):
        parts = [x_ref[0, b, :, j, :] for b in range(x_ref.shape[1])]
        r = _split_dot(f, parts[0] if len(parts) == 1 else jnp.concatenate(parts, axis=0), passes)
        h = r.shape[0] // 2
        o_ref[0, 0, :, j, :] = r[:h]
        o_ref[0, 1, :, j, :] = r[h:]


def _fft1(x, sel, f1, jb, passes):
    _, p, mem, r, n2, w = x.shape
    f1 = _dft_operand(f1, passes)
    n1 = f1.shape[0] // (2 if passes == 1 else 4)
    return pl.pallas_call(
        functools.partial(_fft1_kernel, passes=passes),
        grid=(p, n2 // jb),
        in_specs=[pl.BlockSpec((None, 1, mem, r, jb, w), lambda pp, j: (sel, pp, 0, 0, j, 0)),
                  pl.BlockSpec(f1.shape, lambda pp, j: (0, 0))],
        out_specs=pl.BlockSpec((1, 2, n1, jb, w), lambda pp, j: (pp, 0, 0, j, 0)),
        out_shape=jax.ShapeDtypeStruct((p, 2, n1, n2, w), F32),
        compiler_params=_cp(("parallel", "parallel")),
        name="hyena_fft_stage1",
    )(x, f1)


def _fft2f_kernel(a_ref, f_ref, o_ref, *, kb, n2, passes):
    w = a_ref.shape[-1]
    for t in range(kb):
        xf = _split_dot(f_ref[t], a_ref[0, :, t].reshape(2 * n2, w), passes)
        xb = _split_dot(f_ref[t], a_ref[1, :, t].reshape(2 * n2, w), passes)
        o_ref[0, 0, t] = xf[:n2] + xb[:n2]
        o_ref[0, 1, t] = xf[n2:] - xb[n2:]


def _fft2_filters(a, f2, kb, passes):
    q, _, n1, n2, w = a.shape
    f2 = _dft_operand(f2, passes)
    return pl.pallas_call(
        functools.partial(_fft2f_kernel, kb=kb, n2=n2, passes=passes),
        grid=(q // 2, n1 // kb),
        in_specs=[pl.BlockSpec((2, 2, kb, n2, w), lambda o, k: (o, 0, k, 0, 0)),
                  pl.BlockSpec((kb,) + f2.shape[1:], lambda o, k: (k, 0, 0))],
        out_specs=pl.BlockSpec((1, 2, kb, n2, w), lambda o, k: (o, 0, k, 0, 0)),
        out_shape=jax.ShapeDtypeStruct((q // 2, 2, n1, n2, w), F32),
        compiler_params=_cp(("parallel", "parallel")),
        name="hyena_filter_spectrum",
    )(a, f2)


def _fft2_kernel(a_ref, h_ref, f_ref, g_ref, o_ref, *, kb, n2, passes):
    w = a_ref.shape[-1]
    for t in range(kb):
        a = a_ref[0, :, t].reshape(2 * n2, w)
        x = _split_dot(f_ref[t], a, passes)
        xr, xi = x[:n2], x[n2:]
        hr, hi = h_ref[0, t], h_ref[1, t]
        y = jnp.concatenate([xr * hr - xi * hi, xr * hi + xi * hr], axis=0)
        o_ref[0, :, t] = _split_dot(g_ref[t], y, passes).reshape(2, n2, w)


def _fft2(a, hspec, order, f2, g2, kb, passes):
    p, _, n1, n2, w = a.shape
    f2, g2 = _dft_operand(f2, passes), _dft_operand(g2, passes)
    kern = functools.partial(_fft2_kernel, kb=kb, n2=n2, passes=passes)
    return pl.pallas_call(
        kern,
        grid=(p, n1 // kb),
        in_specs=[pl.BlockSpec((1, 2, kb, n2, w), lambda pp, k: (pp, 0, k, 0, 0)),
                  pl.BlockSpec((None, 2, kb, n2, w), lambda pp, k: (order, 0, k, 0, 0)),
                  pl.BlockSpec((kb,) + f2.shape[1:], lambda pp, k: (k, 0, 0)),
                  pl.BlockSpec((kb,) + g2.shape[1:], lambda pp, k: (k, 0, 0))],
        out_specs=pl.BlockSpec((1, 2, kb, n2, w), lambda pp, k: (pp, 0, k, 0, 0)),
        out_shape=jax.ShapeDtypeStruct(a.shape, F32),
        compiler_params=_cp(("parallel", "parallel")),
        name="hyena_fft_stage2",
    )(a, hspec, f2, g2)


def _fft3_kernel(b_ref, f_ref, gate_ref, prev_ref, skip_ref, o_ref, *, passes):
    f = f_ref[...]
    r = gate_ref.shape[2]
    for j in range(b_ref.shape[3]):
        bj = jnp.concatenate([b_ref[0, 0, :, j, :], b_ref[0, 1, :, j, :]], axis=0)
        conv = _split_dot(f, bj, passes)
        for mem in range(2):
            cm = conv[mem * r:(mem + 1) * r]
            o_ref[0, mem, :, j, :] = gate_ref[0, mem, :, j, :] * (cm + skip_ref[...] * prev_ref[0, mem, :, j, :])


def _fft3(bm, f3, gate, gate_sel, prev, prev_sel, skip, out_rows, jb, passes):
    p, _, n1, n2, w = bm.shape
    r = n1 // 2
    f3 = _dft_operand(f3, passes)
    seq = lambda s: pl.BlockSpec((None, 1, 2, r, jb, w), lambda pp, j: (s, pp, 0, 0, j, 0))
    return pl.pallas_call(
        functools.partial(_fft3_kernel, passes=passes),
        grid=(p, n2 // jb),
        in_specs=[pl.BlockSpec((1, 2, n1, jb, w), lambda pp, j: (pp, 0, 0, j, 0)),
                  pl.BlockSpec(f3.shape, lambda pp, j: (0, 0)),
                  seq(gate_sel), seq(prev_sel),
                  pl.BlockSpec((1, w), lambda pp, j: (0, 0))],
        out_specs=pl.BlockSpec((1, 2, r, jb, w), lambda pp, j: (pp, 0, 0, j, 0)),
        out_shape=jax.ShapeDtypeStruct((p, 2, out_rows, n2, w), F32),
        compiler_params=_cp(("parallel", "parallel")),
        name="hyena_fft_stage3",
    )(bm, f3, gate, prev, skip)


def _hyctx_kernel(v_ref, x1_ref, x2_ref, filt_ref, ff_ref, ffr_ref, fi_ref, skip_ref, yh_hbm_ref, o_ref,
                  *, passes, filter_passes):
    del yh_hbm_ref
    lc, w = v_ref.shape[2], v_ref.shape[3]
    n = 2 * lc
    y = v_ref[0].reshape(2 * lc, w)
    for o, gate_ref in enumerate((x1_ref, x2_ref)):
        sf = _split_dot(ffr_ref[...], filt_ref[o, 0], filter_passes)
        sb = _split_dot(ffr_ref[...], filt_ref[o, 1], filter_passes)
        hr, hi = sf[:n] + sb[:n], sf[n:] - sb[n:]
        x = _split_dot(ff_ref[...], y, passes)
        xr, xi = x[:n], x[n:]
        spec = jnp.concatenate([xr * hr - xi * hi, xr * hi + xi * hr], axis=0)
        conv = _split_dot(fi_ref[...], spec, passes)
        y = gate_ref[0].reshape(2 * lc, w) * (conv + skip_ref[o:o + 1, :] * y)
    o_ref[0] = y.reshape(2, lc, w)


def _hyena_ctx(hc, filt_c, ff, ffr, fi, skip, yh, passes, filter_passes):
    _, nb, lc, w = hc.shape
    p = nb // 2
    t = yh.shape[2]
    ff, fi, ffr = _dft_operand(ff, passes), _dft_operand(fi, passes), _dft_operand(ffr, filter_passes)
    part = pl.BlockSpec((1, 2, lc, w), lambda pp: (pp, 0, 0, 0))
    hc4 = hc.reshape(3, p, 2, lc, w)
    return pl.pallas_call(
        functools.partial(_hyctx_kernel, passes=passes, filter_passes=filter_passes),
        grid=(p,),
        in_specs=[part, part, part,
                  pl.BlockSpec(filt_c.shape, lambda pp: (0, 0, 0, 0)),
                  pl.BlockSpec(ff.shape, lambda pp: (0, 0)),
                  pl.BlockSpec(ffr.shape, lambda pp: (0, 0)),
                  pl.BlockSpec(fi.shape, lambda pp: (0, 0)),
                  pl.BlockSpec(skip.shape, lambda pp: (0, 0)),
                  pl.BlockSpec(memory_space=pl.ANY)],
        out_specs=pl.BlockSpec((1, 2, lc, w), lambda pp: (pp, 0, t // lc - 1, 0)),
        out_shape=jax.ShapeDtypeStruct(yh.shape, F32),
        input_output_aliases={8: 0},
        compiler_params=_cp(("parallel",)),
        name="hyena_ctx",
    )(hc4[0], hc4[1], hc4[2], filt_c, ff, ffr, fi, skip, yh)


def _hyena_filters(l, w1, b1, w2, b2, w3, freq):
    t = jnp.linspace(0.0, 1.0, l, dtype=F32)[:, None]
    bands = (HY_EMB - 1) // 2
    w = 2.0 * math.pi * jnp.arange(l, dtype=F32) / l
    f = jnp.linspace(1e-4, bands - 1, bands, dtype=F32)
    ang = w[:, None] * f[None, :]
    feats = jnp.concatenate([t, jnp.cos(ang), -jnp.sin(ang)], axis=-1)
    hdn = jnp.sin(freq[0] * (feats @ w1 + b1))
    hdn = jnp.sin(freq[1] * (hdn @ w2 + b2))
    filt = (hdn @ w3).astype(F32)
    n_out = filt.shape[-1]
    deltas = jnp.linspace(math.log(HY_TARGET) / HY_SLOW_PCT, math.log(HY_TARGET) / HY_FAST_PCT, n_out, dtype=F32)
    filt = filt * jnp.exp(-t * jnp.abs(deltas))
    filt = filt.reshape(l, 2, 2, BRANCH_W).transpose(1, 2, 0, 3)
    keep = jnp.ones((1, 2, l, 1), F32).at[0, 1, 0, 0].set(0.0)
    return filt * keep


def _hyena(z, conv_w, conv_b, filt_w, skip, nb, lat_len):
    m = z.shape[0]
    t = m // nb
    lc = t - lat_len
    w = BRANCH_W
    p = nb // 2
    n1, n2 = FFT_N1, 2 * lat_len // FFT_N1
    r = n1 // 2
    hl, hc = _hyconv(z, conv_w, conv_b, nb, lat_len)
    f1, f1r, f2, g2, f3 = _dft_tables(lat_len)
    ff, ffr, fi = _dft_tables_small(lc)
    jb, kb = 8, 8
    dp, fp = DFT_PASSES, DFT_FILTER_PASSES
    filt_l = _hyena_filters(lat_len, *filt_w)
    fa = _fft1(filt_l.reshape(1, 4, 1, r, n2, w), 0, f1r, jb, fp)
    hs_l = _fft2_filters(fa, f2, kb, fp)
    hl6 = hl.reshape(3, p, 2, r, n2, w)
    y, y_sel = hl6, 0
    for o in range(2):
        a = _fft1(y, y_sel, f1, jb, dp)
        bm = _fft2(a, hs_l, o, f2, g2, kb, dp)
        rows = r if o == 0 else t // n2
        y = _fft3(bm, f3, hl6, 1 + o, y, y_sel, skip[o:o + 1], rows, jb, dp)[None]
        y_sel = 0
    yh = y.reshape(p, 2, t, w)
    yh = _hyena_ctx(hc, _hyena_filters(lc, *filt_w), ff, ffr, fi, skip, yh, dp, fp)
    return yh.reshape(m, w)


def _merge_kernel(x_ref, mod_ref, ya_ref, yr_ref, yh_ref, yd_ref, wg_ref, bg_ref, wb_ref, wo_ref, g_ref, b_ref,
                  o_ref, *, alpha):
    x = x_ref[...]
    u = _modulate(x, mod_ref, 0, 1).astype(BF16)
    acc = None
    for i, y_ref in enumerate((ya_ref, yr_ref, yh_ref, yd_ref)):
        gate = jax.nn.sigmoid(_dot(u, wg_ref[i]) + bg_ref[i:i + 1, :])
        term = gate * _dot(y_ref[...].astype(BF16), wb_ref[i])
        acc = term if acc is None else acc + term
    mix = _dot(acc.astype(BF16), wo_ref[...])
    o_ref[...] = _gated_residual_ln(x, mix, mod_ref, 2, alpha, g_ref[...], b_ref[...])


def _merge(x, modx, ys, wg, bg, wb, wo, g, b, alpha, tm):
    m, d = x.shape
    w = BRANCH_W
    row = lambda i: (i, 0)
    const2 = lambda i: (0, 0)
    const3 = lambda i: (0, 0, 0)
    one = pl.Buffered(1)
    return pl.pallas_call(
        functools.partial(_merge_kernel, alpha=alpha),
        grid=(m // tm,),
        in_specs=[pl.BlockSpec((tm, d), row),
                  pl.BlockSpec((tm // SUB, 8, d), lambda i: (i, 0, 0)),
                  pl.BlockSpec((tm, w), row), pl.BlockSpec((tm, w), row),
                  pl.BlockSpec((tm, w), row), pl.BlockSpec((tm, w), row),
                  pl.BlockSpec(wg.shape, const3, pipeline_mode=one), pl.BlockSpec(bg.shape, const2),
                  pl.BlockSpec(wb.shape, const3, pipeline_mode=one),
                  pl.BlockSpec(wo.shape, const2, pipeline_mode=one),
                  pl.BlockSpec((1, d), const2), pl.BlockSpec((1, d), const2)],
        out_specs=pl.BlockSpec((tm, d), row),
        out_shape=jax.ShapeDtypeStruct((m, d), F32),
        compiler_params=_cp(("parallel",)),
        name="branch_merge",
    )(x, modx, *ys, wg, bg, wb, wo, g.reshape(1, d), b.reshape(1, d))


def _ffn_kernel(x_ref, mod_ref, w1_ref, w3_ref, w2_ref, g_ref, b_ref, o_ref, *, alpha, tf):
    x = x_ref[...]
    u = _modulate(x, mod_ref, 3, 4).astype(BF16)
    acc = None
    for j in range(w1_ref.shape[1] // tf):
        sl = slice(j * tf, (j + 1) * tf)
        h1 = _dot(u, w1_ref[:, sl])
        h3 = _dot(u, w3_ref[:, sl])
        part = _dot((h1 * jax.nn.sigmoid(h1) * h3).astype(BF16), w2_ref[sl, :])
        acc = part if acc is None else acc + part
    o_ref[...] = _gated_residual_ln(x, acc, mod_ref, 5, alpha, g_ref[...], b_ref[...])


def _ffn(x, modx, w1, w3, w2, g, b, alpha, tm, tf):
    m, d = x.shape
    resident = lambda a: pl.BlockSpec(a.shape, lambda i: (0, 0), pipeline_mode=pl.Buffered(1))
    return pl.pallas_call(
        functools.partial(_ffn_kernel, alpha=alpha, tf=tf),
        grid=(m // tm,),
        in_specs=[pl.BlockSpec((tm, d), lambda i: (i, 0)),
                  pl.BlockSpec((tm // SUB, 8, d), lambda i: (i, 0, 0)),
                  resident(w1), resident(w3), resident(w2),
                  pl.BlockSpec((1, d), lambda i: (0, 0)),
                  pl.BlockSpec((1, d), lambda i: (0, 0))],
        out_specs=pl.BlockSpec((tm, d), lambda i: (i, 0)),
        out_shape=jax.ShapeDtypeStruct((m, d), F32),
        compiler_params=_cp(("parallel",)),
        name="dense_swiglu",
    )(x, modx, w1, w3, w2, g.reshape(1, d), b.reshape(1, d))


def _route_kernel(x_ref, mod_ref, wr_ref, u_ref, lg_ref):
    u = _modulate(x_ref[...], mod_ref, 3, 4)
    u_ref[...] = u
    lg_ref[...] = jnp.dot(u, wr_ref[...], preferred_element_type=F32, precision=lax.Precision.HIGHEST)


def _route(x, modx, wr_pad, tm):
    m, d = x.shape
    return pl.pallas_call(
        _route_kernel,
        grid=(m // tm,),
        in_specs=[pl.BlockSpec((tm, d), lambda i: (i, 0)),
                  pl.BlockSpec((tm // SUB, 8, d), lambda i: (i, 0, 0)),
                  pl.BlockSpec((d, LANE), lambda i: (0, 0))],
        out_specs=[pl.BlockSpec((tm, d), lambda i: (i, 0)), pl.BlockSpec((tm, LANE), lambda i: (i, 0))],
        out_shape=[jax.ShapeDtypeStruct((m, d), F32), jax.ShapeDtypeStruct((m, LANE), F32)],
        compiler_params=_cp(("parallel",)),
        name="moe_router",
    )(x, modx, wr_pad)


def _moe_kernel(be_ref, nu_ref, xb_ref, w1_ref, w3_ref, w2_ref, o_ref, acc_ref, u_ref):
    i, j = pl.program_id(0), pl.program_id(1)
    live = i < nu_ref[0]

    @pl.when(j == 0)
    def _():
        acc_ref[...] = jnp.zeros_like(acc_ref)
        u_ref[...] = xb_ref[...].astype(BF16)

    @pl.when(live)
    def _():
        u = u_ref[...]
        h1 = _dot(u, w1_ref[0])
        h3 = _dot(u, w3_ref[0])
        acc_ref[...] += _dot((h1 * jax.nn.sigmoid(h1) * h3).astype(BF16), w2_ref[0])

    @pl.when(j == pl.num_programs(1) - 1)
    def _():
        o_ref[...] = acc_ref[...]


def _moe_experts(xb, blk_e, n_used, w1, w3, w2, tm, tf):
    r, d = xb.shape
    ff = w1.shape[2]
    nj = ff // tf

    def wmap(i, j, be, nu):
        return (be[i], 0, jnp.where(i < nu[0], j, nj - 1))

    def w2map(i, j, be, nu):
        return (be[i], jnp.where(i < nu[0], j, nj - 1), 0)

    grid_spec = pltpu.PrefetchScalarGridSpec(
        num_scalar_prefetch=2,
        grid=(r // tm, nj),
        in_specs=[pl.BlockSpec((tm, d), lambda i, j, be, nu: (i, 0)),
                  pl.BlockSpec((1, d, tf), wmap),
                  pl.BlockSpec((1, d, tf), wmap),
                  pl.BlockSpec((1, tf, d), w2map)],
        out_specs=pl.BlockSpec((tm, d), lambda i, j, be, nu: (i, 0)),
        scratch_shapes=[pltpu.VMEM((tm, d), F32), pltpu.VMEM((tm, d), BF16)],
    )
    return pl.pallas_call(
        _moe_kernel,
        grid_spec=grid_spec,
        out_shape=jax.ShapeDtypeStruct((r, d), F32),
        compiler_params=_cp(("arbitrary", "arbitrary")),
        name="moe_experts",
    )(blk_e, n_used, xb, w1, w3, w2)


def _ln_res_kernel(x_ref, f0_ref, f1_ref, wt_ref, mod_ref, g_ref, b_ref, o_ref, *, alpha):
    f = f0_ref[...] * wt_ref[:, 0:1] + f1_ref[...] * wt_ref[:, 1:2]
    o_ref[...] = _gated_residual_ln(x_ref[...], f, mod_ref, 5, alpha, g_ref[...], b_ref[...])


def _ln_res(x, f01, wts, modx, g, b, alpha, tm):
    m, d = x.shape
    nt = m // tm
    return pl.pallas_call(
        functools.partial(_ln_res_kernel, alpha=alpha),
        grid=(nt,),
        in_specs=[pl.BlockSpec((tm, d), lambda i: (i, 0)),
                  pl.BlockSpec((tm, d), lambda i: (i, 0)),
                  pl.BlockSpec((tm, d), lambda i: (i + nt, 0)),
                  pl.BlockSpec((tm, TOP_K), lambda i: (i, 0)),
                  pl.BlockSpec((tm // SUB, 8, d), lambda i: (i, 0, 0)),
                  pl.BlockSpec((1, d), lambda i: (0, 0)), pl.BlockSpec((1, d), lambda i: (0, 0))],
        out_specs=pl.BlockSpec((tm, d), lambda i: (i, 0)),
        out_shape=jax.ShapeDtypeStruct((m, d), F32),
        compiler_params=_cp(("parallel",)),
        name="moe_residual_ln",
    )(x, f01, f01, wts, modx, g.reshape(1, d), b.reshape(1, d))


def _moe(x, modx, router, w1, w3, w2, g, b, alpha, tm_rows, tm, tf):
    m, d = x.shape
    wr_pad = jnp.zeros((d, LANE), F32).at[:, :N_EXPERTS].set(router)
    u, logits = _route(x, modx, wr_pad, tm_rows)
    top_v, top_i = lax.top_k(logits[:, :N_EXPERTS], TOP_K)
    wts = jax.nn.softmax(top_v, axis=-1)
    e_flat = top_i.reshape(-1).astype(jnp.int32)
    n_pair = m * TOP_K
    onehot = (e_flat[:, None] == jnp.arange(N_EXPERTS, dtype=jnp.int32)[None, :]).astype(jnp.int32)
    csum = jnp.cumsum(onehot, axis=0)
    counts = csum[-1]
    rank = jnp.take_along_axis(csum, e_flat[:, None], axis=1)[:, 0] - 1
    padded = (counts + tm - 1) // tm * tm
    pad_end = jnp.cumsum(padded)
    pad_start = pad_end - padded
    dest = pad_start[e_flat] + rank
    n_blk = -(-n_pair // tm) + N_EXPERTS
    r = n_blk * tm
    tok = jnp.arange(n_pair, dtype=jnp.int32) // TOP_K
    row_tok = jnp.zeros((r,), jnp.int32).at[dest].set(tok, unique_indices=True)
    n_used = (pad_end[-1] // tm).astype(jnp.int32)
    blk = jnp.minimum(jnp.arange(n_blk, dtype=jnp.int32), n_used - 1) * tm
    blk_e = jnp.minimum(jnp.searchsorted(pad_end, blk, side='right'), N_EXPERTS - 1).astype(jnp.int32)
    xb = jnp.take(u, row_tok, axis=0)
    yb = _moe_experts(xb, blk_e, n_used.reshape(1), w1, w3, w2, tm, tf)
    f01 = jnp.take(yb, dest.reshape(m, TOP_K).T.reshape(-1), axis=0)
    return _ln_res(x, f01, wts, modx, g, b, alpha, tm_rows)


def _rope_tables(nb, lat_len, ctx_len):
    rows = lat_len // GRID_W
    row = jnp.repeat(jnp.arange(rows, dtype=F32), GRID_W)
    col = jnp.tile(jnp.arange(GRID_W, dtype=F32), rows)
    n_freq = HEAD_DIM // 4
    inv = ROPE_THETA ** (-jnp.arange(n_freq, dtype=F32) / n_freq)
    ar, ac = row[:, None] * inv, col[:, None] * inv
    cos = jnp.concatenate([jnp.cos(ar), jnp.cos(ar), jnp.cos(ac), jnp.cos(ac)], axis=1)
    sin = jnp.concatenate([-jnp.sin(ar), jnp.sin(ar), -jnp.sin(ac), jnp.sin(ac)], axis=1)
    cos = jnp.concatenate([cos, jnp.ones((ctx_len, HEAD_DIM), F32)], axis=0)
    sin = jnp.concatenate([sin, jnp.zeros((ctx_len, HEAD_DIM), F32)], axis=0)
    return jnp.tile(cos, (nb, 1)), jnp.tile(sin, (nb, 1))


def _row_tile(m, pref):
    tm = pref
    while m % tm:
        tm //= 2
    return tm


def kernel(x, c, ctx, c_ctx, w_mod, b_mod, w_in, conv_a_w, conv_a_b, ret_decay, hy_conv_w, hy_conv_b, hy_w1, hy_b1, hy_w2, hy_b2, hy_w3, hy_freq, hy_skip, q_norm, k_norm, w_gate, b_gate, w_branch, w_o, ln_g, ln_b, ffn_w1, ffn_w3, ffn_w2, router, moe_w1, moe_w3, moe_w2):
    nb, lat_len, d = x.shape
    ctx_len = ctx.shape[1]
    depth = w_mod.shape[0]
    t = lat_len + ctx_len
    m = nb * t
    alpha = float((2 * depth) ** 0.25)
    assert lat_len % SUB == 0 and ctx_len % SUB == 0 and nb % 2 == 0

    act = jnp.concatenate([jax.nn.silu(c), jax.nn.silu(c_ctx)[None, :],
                           jnp.zeros((8 - nb - 1, d), F32)], axis=0)
    mods = _mod_all(act, w_mod, b_mod).reshape(depth, 8, N_MOD, d)
    mods = jnp.pad(mods, ((0, 0), (0, 0), (0, 8 - N_MOD), (0, 0)))
    tiles_per_batch = t // SUB
    tile_pos = np.arange(m // SUB) % tiles_per_batch
    tile_group = np.where(tile_pos < lat_len // SUB, np.arange(m // SUB) // tiles_per_batch, nb)
    cos_t, sin_t = _rope_tables(nb, lat_len, ctx_len)

    rows = jnp.concatenate([x, ctx], axis=1).reshape(m, d)
    tm = _row_tile(m, 512)
    for l in range(depth):
        modx = mods[l][tile_group]
        z = _proj(rows, modx, w_in[l].astype(BF16), _row_tile(m, 1024), 1024)
        rq, rk, aq, ak, av = _prep(z, cos_t, sin_t, q_norm[l], k_norm[l], tm)
        y_d = _attention(aq, ak, av, nb, lat_len, SUB)
        log_gamma = -jnp.exp(ret_decay[l].astype(F32))
        y_r = _retention(rq, rk, z, _retention_tables(log_gamma), nb, lat_len)
        y_a = _shortconv(z, conv_a_w[l], conv_a_b[l], nb, lat_len)
        y_h = _hyena(z, hy_conv_w[l], hy_conv_b[l],
                     (hy_w1[l], hy_b1[l], hy_w2[l], hy_b2[l], hy_w3[l], hy_freq[l]), hy_skip[l], nb, lat_len)
        rows = _merge(rows, modx, (y_a, y_r, y_h, y_d), w_gate[l].astype(BF16), b_gate[l],
                      w_branch[l].astype(BF16), w_o[l].astype(BF16), ln_g[l, 0], ln_b[l, 0], alpha, tm)
        i = l // 2
        if l % 2 == 0:
            rows = _ffn(rows, modx, ffn_w1[i].astype(BF16), ffn_w3[i].astype(BF16), ffn_w2[i].astype(BF16),
                        ln_g[l, 1], ln_b[l, 1], alpha, tm, 256)
        else:
            rows = _moe(rows, modx, router[i], moe_w1[i].astype(BF16), moe_w3[i].astype(BF16),
                        moe_w2[i].astype(BF16), ln_g[l, 1], ln_b[l, 1], alpha, tm, 512, 1792)
    return rows.reshape(nb, t, d)[:, :lat_len]
```

```python
import functools
import math

import numpy as np
import jax
import jax.numpy as jnp
from jax import lax
from jax.experimental import pallas as pl
from jax.experimental.pallas import tpu as pltpu

F32 = jnp.float32
BF16 = jnp.bfloat16

GRID_W = 64
BRANCH_W = 512
RET_HEADS = 4
RET_D = 128
RET_CHUNK = 128
HY_EMB = 33
HY_FAST_PCT = 0.3
HY_SLOW_PCT = 1.5
HY_TARGET = 1e-2
ATT_HEADS = 4
ATT_KV_HEADS = 2
HEAD_DIM = 128
ROPE_THETA = 10000.0
N_EXPERTS = 8
TOP_K = 2
N_MOD = 6
A_W = 3 * BRANCH_W
R_W = 4 * RET_HEADS * RET_D
H_W = 3 * BRANCH_W
D_W = (ATT_HEADS + 2 * ATT_KV_HEADS) * HEAD_DIM
IN_W = A_W + R_W + H_W + D_W
LN_EPS = 1e-6

SUB = 256
LANE = 128
FFT_N1 = 128
RET_HEADS_PER_STEP = 2
RET_UNROLL = 2
ATT_TQ = 512
ATT_KC = 1024
DFT_PASSES = 1
DFT_FILTER_PASSES = 2
VMEM_LIMIT = 56 * 1024 * 1024


def _cp(sem, vmem=VMEM_LIMIT):
    return pltpu.CompilerParams(dimension_semantics=sem, vmem_limit_bytes=vmem)


def _dot(a, b):
    return jnp.dot(a, b, preferred_element_type=F32)


def _dot_nt(a, b):
    return lax.dot_general(a, b, (((1,), (1,)), ((), ())), preferred_element_type=F32)


def _dot_tn(a, b):
    return lax.dot_general(a, b, (((0,), (0,)), ((), ())), preferred_element_type=F32)


def _split_dot(f_cat, x, passes):
    xh = x.astype(BF16)
    if passes == 1:
        return _dot(f_cat, xh)
    m = f_cat.shape[0] // 2
    r = _dot(f_cat, xh)
    r = r[:m] + r[m:]
    if passes == 3:
        r = r + _dot(f_cat[:m], (x - xh.astype(F32)).astype(BF16))
    return r


def _dft_operand(f_cat, passes):
    return f_cat[..., :f_cat.shape[-2] // 2, :] if passes == 1 else f_cat


def _modulate(x, mod_ref, shift_row, scale_row):
    parts = []
    for s in range(x.shape[0] // SUB):
        sh = mod_ref[s, shift_row:shift_row + 1, :]
        sc = mod_ref[s, scale_row:scale_row + 1, :]
        parts.append(x[s * SUB:(s + 1) * SUB] * (1.0 + sc) + sh)
    return parts[0] if len(parts) == 1 else jnp.concatenate(parts, axis=0)


def _gated_residual_ln(x, f, mod_ref, gate_row, alpha, g, b):
    parts = []
    for s in range(x.shape[0] // SUB):
        gt = mod_ref[s, gate_row:gate_row + 1, :]
        parts.append(alpha * x[s * SUB:(s + 1) * SUB] + gt * f[s * SUB:(s + 1) * SUB])
    r = parts[0] if len(parts) == 1 else jnp.concatenate(parts, axis=0)
    mu = jnp.mean(r, axis=-1, keepdims=True)
    rc = r - mu
    var = jnp.mean(rc * rc, axis=-1, keepdims=True)
    return rc * lax.rsqrt(var + LN_EPS) * g + b


def _mod_kernel(a_ref, w_ref, b_ref, o_ref):
    o_ref[0] = _dot(a_ref[...].astype(BF16), w_ref[0].astype(BF16)) + b_ref[0]


def _mod_all(act, w_mod, b_mod):
    depth, d, n = w_mod.shape
    tn = 512
    return pl.pallas_call(
        _mod_kernel,
        grid=(depth, n // tn),
        in_specs=[pl.BlockSpec((8, d), lambda l, j: (0, 0)),
                  pl.BlockSpec((1, d, tn), lambda l, j: (l, 0, j)),
                  pl.BlockSpec((1, 1, tn), lambda l, j: (l, 0, j))],
        out_specs=pl.BlockSpec((1, 8, tn), lambda l, j: (l, 0, j)),
        out_shape=jax.ShapeDtypeStruct((depth, 8, n), F32),
        compiler_params=_cp(("parallel", "parallel")),
        name="mod_vectors",
    )(act, w_mod, b_mod.reshape(depth, 1, n))


def _proj_kernel(x_ref, mod_ref, w_ref, o_ref, u_ref, *, tn):
    j = pl.program_id(1)

    @pl.when(j == 0)
    def _():
        u_ref[...] = _modulate(x_ref[...], mod_ref, 0, 1).astype(BF16)
    col = pl.multiple_of(j * tn, tn)
    o_ref[...] = _dot(u_ref[...], w_ref[:, pl.ds(col, tn)])


def _proj(x, modx, w, layer, tm, tn):
    m, d = x.shape
    n = w.shape[2]
    return pl.pallas_call(
        functools.partial(_proj_kernel, tn=tn),
        grid=(m // tm, n // tn),
        in_specs=[pl.BlockSpec((tm, d), lambda i, j: (i, 0)),
                  pl.BlockSpec((tm // SUB, 8, d), lambda i, j: (i, 0, 0)),
                  pl.BlockSpec((None, d, n), lambda i, j: (layer, 0, 0), pipeline_mode=pl.Buffered(1))],
        out_specs=pl.BlockSpec((tm, tn), lambda i, j: (i, j)),
        out_shape=jax.ShapeDtypeStruct((m, n), F32),
        scratch_shapes=[pltpu.VMEM((tm, d), BF16)],
        compiler_params=_cp(("parallel", "arbitrary")),
        name="in_proj",
    )(x, modx, w)


def _prep_kernel(rq_ref, rk_ref, aq_ref, ak_ref, av_ref, cos_ref, sin_ref, gq_ref, gk_ref,
                 orq_ref, ork_ref, oaq_ref, oak_ref, oav_ref):
    c = cos_ref[...]
    s = sin_ref[...]
    lane = lax.broadcasted_iota(jnp.int32, c.shape, 1)
    first = (lane % 64) < 32

    def rope(x):
        partner = jnp.where(first, pltpu.roll(x, 96, 1), pltpu.roll(x, 32, 1))
        return x * c + partner * s

    def rms(x, g):
        return x * lax.rsqrt(jnp.mean(x * x, axis=-1, keepdims=True) + LN_EPS) * g

    for h in range(RET_HEADS):
        sl = slice(h * LANE, (h + 1) * LANE)
        orq_ref[:, sl] = rope(rq_ref[:, sl])
        ork_ref[:, sl] = rope(rk_ref[:, sl] * (RET_D ** -0.5))
    q_scale = (HEAD_DIM ** -0.5) * math.log2(math.e)
    for h in range(ATT_HEADS):
        sl = slice(h * LANE, (h + 1) * LANE)
        oaq_ref[:, sl] = (rope(rms(aq_ref[:, sl], gq_ref[...])) * q_scale).astype(BF16)
    ones_col = jnp.where(lane == 0, 1.0, 0.0).astype(BF16)
    for h in range(ATT_KV_HEADS):
        sl = slice(h * LANE, (h + 1) * LANE)
        oak_ref[:, sl] = rope(rms(ak_ref[:, sl], gk_ref[...])).astype(BF16)
        oav_ref[:, 2 * h * LANE:(2 * h + 1) * LANE] = av_ref[:, sl].astype(BF16)
        oav_ref[:, (2 * h + 1) * LANE:(2 * h + 2) * LANE] = ones_col


def _prep(z, cos_t, sin_t, gq, gk, tm):
    m = z.shape[0]
    qw, kw = ATT_HEADS * HEAD_DIM, ATT_KV_HEADS * HEAD_DIM
    rw = RET_HEADS * RET_D
    d_off = A_W + R_W + H_W
    row = lambda i: (i, 0)
    return pl.pallas_call(
        _prep_kernel,
        grid=(m // tm,),
        in_specs=[pl.BlockSpec((tm, rw), lambda i: (i, A_W // rw)),
                  pl.BlockSpec((tm, rw), lambda i: (i, A_W // rw + 1)),
                  pl.BlockSpec((tm, qw), lambda i: (i, d_off // qw)),
                  pl.BlockSpec((tm, kw), lambda i: (i, (d_off + qw) // kw)),
                  pl.BlockSpec((tm, kw), lambda i: (i, (d_off + qw) // kw + 1)),
                  pl.BlockSpec((tm, LANE), row),
                  pl.BlockSpec((tm, LANE), row),
                  pl.BlockSpec((1, LANE), lambda i: (0, 0)),
                  pl.BlockSpec((1, LANE), lambda i: (0, 0))],
        out_specs=[pl.BlockSpec((tm, rw), row), pl.BlockSpec((tm, rw), row),
                   pl.BlockSpec((tm, qw), row), pl.BlockSpec((tm, kw), row), pl.BlockSpec((tm, 2 * kw), row)],
        out_shape=[jax.ShapeDtypeStruct((m, rw), F32), jax.ShapeDtypeStruct((m, rw), F32),
                   jax.ShapeDtypeStruct((m, qw), BF16), jax.ShapeDtypeStruct((m, kw), BF16),
                   jax.ShapeDtypeStruct((m, 2 * kw), BF16)],
        compiler_params=_cp(("parallel",)),
        name="qk_prep",
    )(z, z, z, z, z, cos_t, sin_t, gq.reshape(1, LANE), gk.reshape(1, LANE))


def _attn_kernel(q_ref, k_ref, v_ref, *rest, key_lo, key_hi, kc):
    o_ref = rest[-1]
    tq = q_ref.shape[1]
    grp = ATT_HEADS // ATT_KV_HEADS
    q2 = jnp.concatenate([q_ref[0, :, g * HEAD_DIM:(g + 1) * HEAD_DIM] for g in range(grp)], axis=0)
    m_run, acc = None, None
    for lo in range(key_lo, key_hi, kc):
        hi = min(lo + kc, key_hi)
        s = _dot_nt(q2, k_ref[0, lo:hi, :])
        m_chunk = jnp.max(s, axis=-1, keepdims=True)
        m_new = m_chunk if m_run is None else jnp.maximum(m_run, m_chunk)
        pv = _dot(jnp.exp2(s - m_new).astype(BF16), v_ref[0, lo:hi, :])
        acc = pv if acc is None else jnp.exp2(m_run - m_new) * acc + pv
        m_run = m_new
    for g in range(grp):
        a = acc[g * tq:(g + 1) * tq]
        o_ref[0, :, g * HEAD_DIM:(g + 1) * HEAD_DIM] = a[:, :HEAD_DIM] / a[:, HEAD_DIM:HEAD_DIM + 1]


def _attention(aq, ak, av, nb, lat_len):
    m = aq.shape[0]
    t = m // nb
    lc = t - lat_len
    grp_w = (ATT_HEADS // ATT_KV_HEADS) * HEAD_DIM
    args = (aq.reshape(nb, t, -1), ak.reshape(nb, t, -1), av.reshape(nb, t, -1))

    def call(tq, n_tiles, first_tile, key_lo, key_hi, prev):
        qmap = lambda b, kv, i: (b, first_tile + i, kv)
        specs = [pl.BlockSpec((1, tq, grp_w), qmap),
                 pl.BlockSpec((1, t, HEAD_DIM), lambda b, kv, i: (b, 0, kv)),
                 pl.BlockSpec((1, t, 2 * HEAD_DIM), lambda b, kv, i: (b, 0, kv))]
        extra, alias = (), {}
        if prev is not None:
            specs.append(pl.BlockSpec(memory_space=pl.ANY))
            extra, alias = (prev,), {3: 0}
        return pl.pallas_call(
            functools.partial(_attn_kernel, key_lo=key_lo, key_hi=key_hi, kc=ATT_KC),
            grid=(nb, ATT_KV_HEADS, n_tiles),
            in_specs=specs,
            out_specs=pl.BlockSpec((1, tq, grp_w), qmap),
            out_shape=jax.ShapeDtypeStruct((nb, t, ATT_HEADS * HEAD_DIM), F32),
            input_output_aliases=alias,
            compiler_params=_cp(("parallel", "parallel", "arbitrary")),
            name="gqa_attention",
        )(*args, *extra)

    tq = math.gcd(ATT_TQ, lat_len)
    out = call(tq, lat_len // tq, 0, 0, t, None)
    out = call(lc, 1, t // lc - 1, lat_len, t, out)
    return out.reshape(m, -1)


def _ret_kernel(q_ref, k_ref, v_ref, g_ref, intra_ref, qd_ref, kd_ref, cd_ref, o_ref, ob_ref,
                *, n_lat, n_ctx):
    c_len = RET_CHUNK
    n = n_lat + n_ctx

    hps = q_ref.shape[2] // LANE

    def chunk(ref, c, h):
        return ref[0, pl.ds(pl.multiple_of(c * c_len, c_len), c_len), h * LANE:(h + 1) * LANE]

    def step(d, c, h, state):
        qc, kc, vc = chunk(q_ref, c, h), chunk(k_ref, c, h), chunk(v_ref, c, h).astype(BF16)
        s = _dot_nt(qc.astype(BF16), kc.astype(BF16)) * intra_ref[d, h]
        o = _dot(s.astype(BF16), vc) + _dot((qc * qd_ref[d, h]).astype(BF16), state.astype(BF16))
        state = cd_ref[d, h, 0:1, :] * state + _dot_tn((kc * kd_ref[d, h]).astype(BF16), vc)
        return o, state

    def body(t, carry):
        cf = jnp.where(t < n_ctx, n_lat + t, t - n_ctx)
        cb = jnp.where(t < n_ctx, n - 1 - t, n_lat - 1 - (t - n_ctx))
        new = []
        for h in range(hps):
            o_f, sf = step(0, cf, h, carry[2 * h])
            o_ref[0, pl.ds(pl.multiple_of(cf * c_len, c_len), c_len), h * LANE:(h + 1) * LANE] = o_f
            o_b, sb = step(1, cb, h, carry[2 * h + 1])
            ob_ref[pl.ds(pl.multiple_of(cb * c_len, c_len), c_len), h * LANE:(h + 1) * LANE] = o_b
            new += [sf, sb]
        return tuple(new)

    zero = jnp.zeros((RET_D, RET_D), F32)
    lax.fori_loop(0, n, body, (zero,) * (2 * hps), unroll=RET_UNROLL)
    for h in range(hps):
        sl = slice(h * LANE, (h + 1) * LANE)
        o = o_ref[0, :, sl] + ob_ref[:, sl]
        mu = jnp.mean(o, axis=-1, keepdims=True)
        oc = o - mu
        var = jnp.mean(oc * oc, axis=-1, keepdims=True)
        g = g_ref[0, :, sl]
        o_ref[0, :, sl] = (g * jax.nn.sigmoid(g)) * (oc * lax.rsqrt(var + LN_EPS))


def _retention(rq, rk, z, tabs, nb, lat_len):
    m = rq.shape[0]
    t = m // nb
    intra, qd, kd, cd = tabs
    v_blk = (A_W + 2 * RET_HEADS * RET_D) // LANE
    g_blk = v_blk + RET_HEADS
    kern = functools.partial(_ret_kernel, n_lat=lat_len // RET_CHUNK, n_ctx=(t - lat_len) // RET_CHUNK)
    hps = RET_HEADS_PER_STEP
    hw = hps * LANE
    seq = lambda off: pl.BlockSpec((1, t, hw), lambda b, h: (b, 0, off // hps + h))
    tab = lambda r: pl.BlockSpec((2, hps, r, LANE), lambda b, h: (0, h, 0, 0))
    out = pl.pallas_call(
        kern,
        grid=(nb, RET_HEADS // hps),
        in_specs=[seq(0), seq(0), seq(v_blk),
                  pl.BlockSpec((1, t, hw), lambda b, h: (b, 0, g_blk // hps + h), pipeline_mode=pl.Buffered(1)),
                  tab(RET_CHUNK), tab(RET_CHUNK), tab(RET_CHUNK), tab(8)],
        out_specs=seq(0),
        out_shape=jax.ShapeDtypeStruct((nb, t, RET_HEADS * RET_D), F32),
        scratch_shapes=[pltpu.VMEM((t, hw), F32)],
        compiler_params=_cp(("parallel", "parallel")),
        name="retention",
    )(rq.reshape(nb, t, -1), rk.reshape(nb, t, -1), z.reshape(nb, t, -1), z.reshape(nb, t, -1), intra, qd, kd, cd)
    return out.reshape(m, -1)


def _retention_tables(log_gamma):
    c = RET_CHUNK
    i = jnp.arange(c, dtype=F32)
    lg = log_gamma.astype(F32)[:, :, None]
    rel = i[:, None] - i[None, :]
    intra_f = jnp.where(rel >= 0, jnp.exp(lg[0][..., None] * jnp.maximum(rel, 0.0)), 0.0)
    intra_b = jnp.where(rel <= 0, jnp.exp(lg[1][..., None] * jnp.maximum(-rel, 0.0)), 0.0)
    qd = jnp.stack([jnp.exp(lg[0] * (i + 1.0)), jnp.exp(lg[1] * (c - i))])
    kd = jnp.stack([jnp.exp(lg[0] * (c - 1.0 - i)), jnp.exp(lg[1] * i)])
    cd = jnp.exp(lg * c)
    bc = lambda a: jnp.broadcast_to(a[..., None], a.shape + (LANE,))
    return (jnp.stack([intra_f, intra_b]), bc(qd), bc(kd),
            jnp.broadcast_to(cd[..., None], (2, RET_HEADS, 8, LANE)))


def _shift_rows(x, lat_len):
    t = x.shape[0]
    row = lax.broadcasted_iota(jnp.int32, x.shape, 0)
    xm = jnp.where((row == 0) | (row == lat_len), 0.0, pltpu.roll(x, 1, 0))
    xp = jnp.where((row == lat_len - 1) | (row == t - 1), 0.0, pltpu.roll(x, t - 1, 0))
    return xm, xp


def _shortconv_kernel(bg_ref, cg_ref, xv_ref, w_ref, b_ref, o_ref, *, lat_len):
    p = cg_ref[0] * xv_ref[0]
    pm, pp = _shift_rows(p, lat_len)
    o_ref[0] = bg_ref[0] * (pm * w_ref[0:1, :] + p * w_ref[1:2, :] + pp * w_ref[2:3, :] + b_ref[...])


def _shortconv(z, w, b, nb, lat_len):
    m = z.shape[0]
    t = m // nb
    nct = BRANCH_W // LANE
    kern = functools.partial(_shortconv_kernel, lat_len=lat_len)
    seq = lambda off: pl.BlockSpec((1, t, LANE), lambda bb, j: (bb, 0, off + j))
    out = pl.pallas_call(
        kern,
        grid=(nb, nct),
        in_specs=[seq(0), seq(nct), seq(2 * nct),
                  pl.BlockSpec((3, LANE), lambda bb, j: (0, j)),
                  pl.BlockSpec((1, LANE), lambda bb, j: (0, j))],
        out_specs=seq(0),
        out_shape=jax.ShapeDtypeStruct((nb, t, BRANCH_W), F32),
        compiler_params=_cp(("parallel", "parallel")),
        name="short_conv",
    )(z.reshape(nb, t, -1), z.reshape(nb, t, -1), z.reshape(nb, t, -1), w, b.reshape(1, -1))
    return out.reshape(m, -1)


def _hyconv_kernel(z_ref, w_ref, b_ref, ol_ref, oc_ref, *, lat_len):
    x = z_ref[0]
    xm, xp = _shift_rows(x, lat_len)
    y = xm * w_ref[0:1, :] + x * w_ref[1:2, :] + xp * w_ref[2:3, :] + b_ref[...]
    ol_ref[0, 0] = y[:lat_len]
    oc_ref[0, 0] = y[lat_len:]


def _hyconv(z, w, b, nb, lat_len):
    m = z.shape[0]
    t = m // nb
    nct = BRANCH_W // LANE
    off = (A_W + R_W) // LANE
    kern = functools.partial(_hyconv_kernel, lat_len=lat_len)
    return pl.pallas_call(
        kern,
        grid=(nb, 3 * nct),
        in_specs=[pl.BlockSpec((1, t, LANE), lambda bb, j: (bb, 0, off + j)),
                  pl.BlockSpec((3, LANE), lambda bb, j: (0, j)),
                  pl.BlockSpec((1, LANE), lambda bb, j: (0, j))],
        out_specs=[pl.BlockSpec((1, 1, lat_len, LANE), lambda bb, j: (j // nct, bb, 0, j % nct)),
                   pl.BlockSpec((1, 1, t - lat_len, LANE), lambda bb, j: (j // nct, bb, 0, j % nct))],
        out_shape=[jax.ShapeDtypeStruct((3, nb, lat_len, BRANCH_W), F32),
                   jax.ShapeDtypeStruct((3, nb, t - lat_len, BRANCH_W), F32)],
        compiler_params=_cp(("parallel", "parallel")),
        name="hyena_dwconv",
    )(z.reshape(nb, t, -1), w, b.reshape(1, -1))


def _stack_c(m):
    return np.block([[m.real, -m.imag], [m.imag, m.real]])


def _split_np(m):
    m32 = np.asarray(m, np.float32)
    hi = m32.astype(BF16)
    lo = (m32 - hi.astype(np.float32)).astype(BF16)
    return np.concatenate([hi, lo], axis=-2)


@functools.lru_cache(maxsize=None)
def _dft_tables(lat_len):
    n = 2 * lat_len
    n1, n2 = FFT_N1, 2 * lat_len // FFT_N1
    a1 = np.arange(n1)
    a2 = np.arange(n2)
    w1 = np.exp(-2j * np.pi * np.outer(a1, a1[:n1 // 2]) / n1)
    f1 = _stack_c(w1)
    w2 = np.exp(-2j * np.pi * np.outer(a2, a2) / n2)
    tw = np.exp(-2j * np.pi * np.outer(a1, a2) / n)
    cf = w2[None, :, :] * tw[:, None, :]
    f2 = np.stack([_stack_c(cf[k]) for k in range(n1)])
    g2 = np.stack([_stack_c(cf[k].conj().T) for k in range(n1)])
    w1i = np.exp(2j * np.pi * np.outer(a1[:n1 // 2], a1) / n1) / n
    f3 = _stack_c(w1i)
    f1_real = f1[:, :n1 // 2]
    return _split_np(f1), _split_np(f1_real), _split_np(f2), _split_np(g2), _split_np(f3)


@functools.lru_cache(maxsize=None)
def _dft_tables_small(ctx_len):
    n = 2 * ctx_len
    a = np.arange(n)
    ff = _stack_c(np.exp(-2j * np.pi * np.outer(a, a[:ctx_len]) / n))
    fi = _stack_c(np.exp(2j * np.pi * np.outer(a[:ctx_len], a) / n) / n)
    return _split_np(ff), _split_np(ff[:, :ctx_len]), _split_np(fi)


def _fft1_kernel(x_ref, f_ref, o_ref, *, passes):
    f = f_ref[...]
    for j in range(x_ref.shape[3]):
        parts = [x_ref[0, b, :, j, :] for b in range(x_ref.shape[1])]
        r = _split_dot(f, parts[0] if len(parts) == 1 else jnp.concatenate(parts, axis=0), passes)
        h = r.shape[0] // 2
        o_ref[0, 0, :, j, :] = r[:h]
        o_ref[0, 1, :, j, :] = r[h:]


def _fft1(x, sel, f1, jb, passes):
    _, p, mem, r, n2, w = x.shape
    f1 = _dft_operand(f1, passes)
    n1 = f1.shape[0] // (2 if passes == 1 else 4)
    return pl.pallas_call(
        functools.partial(_fft1_kernel, passes=passes),
        grid=(p, n2 // jb),
        in_specs=[pl.BlockSpec((None, 1, mem, r, jb, w), lambda pp, j: (sel, pp, 0, 0, j, 0)),
                  pl.BlockSpec(f1.shape, lambda pp, j: (0, 0))],
        out_specs=pl.BlockSpec((1, 2, n1, jb, w), lambda pp, j: (pp, 0, 0, j, 0)),
        out_shape=jax.ShapeDtypeStruct((p, 2, n1, n2, w), F32),
        compiler_params=_cp(("parallel", "parallel")),
        name="hyena_fft_stage1",
    )(x, f1)


def _fft2f_kernel(a_ref, f_ref, o_ref, *, kb, n2, passes):
    w = a_ref.shape[-1]
    for t in range(kb):
        xf = _split_dot(f_ref[t], a_ref[0, :, t].reshape(2 * n2, w), passes)
        xb = _split_dot(f_ref[t], a_ref[1, :, t].reshape(2 * n2, w), passes)
        o_ref[0, 0, t] = xf[:n2] + xb[:n2]
        o_ref[0, 1, t] = xf[n2:] - xb[n2:]


def _fft2_filters(a, f2, kb, passes):
    q, _, n1, n2, w = a.shape
    f2 = _dft_operand(f2, passes)
    return pl.pallas_call(
        functools.partial(_fft2f_kernel, kb=kb, n2=n2, passes=passes),
        grid=(q // 2, n1 // kb),
        in_specs=[pl.BlockSpec((2, 2, kb, n2, w), lambda o, k: (o, 0, k, 0, 0)),
                  pl.BlockSpec((kb,) + f2.shape[1:], lambda o, k: (k, 0, 0))],
        out_specs=pl.BlockSpec((1, 2, kb, n2, w), lambda o, k: (o, 0, k, 0, 0)),
        out_shape=jax.ShapeDtypeStruct((q // 2, 2, n1, n2, w), F32),
        compiler_params=_cp(("parallel", "parallel")),
        name="hyena_filter_spectrum",
    )(a, f2)


def _fft2_kernel(a_ref, h_ref, f_ref, g_ref, o_ref, *, kb, n2, passes):
    w = a_ref.shape[-1]
    for t in range(kb):
        a = a_ref[0, :, t].reshape(2 * n2, w)
        x = _split_dot(f_ref[t], a, passes)
        xr, xi = x[:n2], x[n2:]
        hr, hi = h_ref[0, t], h_ref[1, t]
        y = jnp.concatenate([xr * hr - xi * hi, xr * hi + xi * hr], axis=0)
        o_ref[0, :, t] = _split_dot(g_ref[t], y, passes).reshape(2, n2, w)


def _fft2(a, hspec, order, f2, g2, kb, passes):
    p, _, n1, n2, w = a.shape
    f2, g2 = _dft_operand(f2, passes), _dft_operand(g2, passes)
    kern = functools.partial(_fft2_kernel, kb=kb, n2=n2, passes=passes)
    return pl.pallas_call(
        kern,
        grid=(p, n1 // kb),
        in_specs=[pl.BlockSpec((1, 2, kb, n2, w), lambda pp, k: (pp, 0, k, 0, 0)),
                  pl.BlockSpec((None, 2, kb, n2, w), lambda pp, k: (order, 0, k, 0, 0)),
                  pl.BlockSpec((kb,) + f2.shape[1:], lambda pp, k: (k, 0, 0)),
                  pl.BlockSpec((kb,) + g2.shape[1:], lambda pp, k: (k, 0, 0))],
        out_specs=pl.BlockSpec((1, 2, kb, n2, w), lambda pp, k: (pp, 0, k, 0, 0)),
        out_shape=jax.ShapeDtypeStruct(a.shape, F32),
        compiler_params=_cp(("parallel", "parallel")),
        name="hyena_fft_stage2",
    )(a, hspec, f2, g2)


def _fft3_kernel(b_ref, f_ref, gate_ref, prev_ref, skip_ref, o_ref, *, passes):
    f = f_ref[...]
    r = gate_ref.shape[2]
    for j in range(b_ref.shape[3]):
        bj = jnp.concatenate([b_ref[0, 0, :, j, :], b_ref[0, 1, :, j, :]], axis=0)
        conv = _split_dot(f, bj, passes)
        for mem in range(2):
            cm = conv[mem * r:(mem + 1) * r]
            o_ref[0, mem, :, j, :] = gate_ref[0, mem, :, j, :] * (cm + skip_ref[...] * prev_ref[0, mem, :, j, :])


def _fft3(bm, f3, gate, gate_sel, prev, prev_sel, skip, out_rows, jb, passes):
    p, _, n1, n2, w = bm.shape
    r = n1 // 2
    f3 = _dft_operand(f3, passes)
    seq = lambda s: pl.BlockSpec((None, 1, 2, r, jb, w), lambda pp, j: (s, pp, 0, 0, j, 0))
    return pl.pallas_call(
        functools.partial(_fft3_kernel, passes=passes),
        grid=(p, n2 // jb),
        in_specs=[pl.BlockSpec((1, 2, n1, jb, w), lambda pp, j: (pp, 0, 0, j, 0)),
                  pl.BlockSpec(f3.shape, lambda pp, j: (0, 0)),
                  seq(gate_sel), seq(prev_sel),
                  pl.BlockSpec((1, w), lambda pp, j: (0, 0))],
        out_specs=pl.BlockSpec((1, 2, r, jb, w), lambda pp, j: (pp, 0, 0, j, 0)),
        out_shape=jax.ShapeDtypeStruct((p, 2, out_rows, n2, w), F32),
        compiler_params=_cp(("parallel", "parallel")),
        name="hyena_fft_stage3",
    )(bm, f3, gate, prev, skip)


def _hyctx_kernel(v_ref, x1_ref, x2_ref, filt_ref, ff_ref, ffr_ref, fi_ref, skip_ref, yh_hbm_ref, o_ref,
                  *, passes, filter_passes):
    del yh_hbm_ref
    lc, w = v_ref.shape[2], v_ref.shape[3]
    n = 2 * lc
    y = v_ref[0].reshape(2 * lc, w)
    for o, gate_ref in enumerate((x1_ref, x2_ref)):
        sf = _split_dot(ffr_ref[...], filt_ref[o, 0], filter_passes)
        sb = _split_dot(ffr_ref[...], filt_ref[o, 1], filter_passes)
        hr, hi = sf[:n] + sb[:n], sf[n:] - sb[n:]
        x = _split_dot(ff_ref[...], y, passes)
        xr, xi = x[:n], x[n:]
        spec = jnp.concatenate([xr * hr - xi * hi, xr * hi + xi * hr], axis=0)
        conv = _split_dot(fi_ref[...], spec, passes)
        y = gate_ref[0].reshape(2 * lc, w) * (conv + skip_ref[o:o + 1, :] * y)
    o_ref[0] = y.reshape(2, lc, w)


def _hyena_ctx(hc, filt_c, ff, ffr, fi, skip, yh, passes, filter_passes):
    _, nb, lc, w = hc.shape
    p = nb // 2
    t = yh.shape[2]
    ff, fi, ffr = _dft_operand(ff, passes), _dft_operand(fi, passes), _dft_operand(ffr, filter_passes)
    part = pl.BlockSpec((1, 2, lc, w), lambda pp: (pp, 0, 0, 0))
    hc4 = hc.reshape(3, p, 2, lc, w)
    return pl.pallas_call(
        functools.partial(_hyctx_kernel, passes=passes, filter_passes=filter_passes),
        grid=(p,),
        in_specs=[part, part, part,
                  pl.BlockSpec(filt_c.shape, lambda pp: (0, 0, 0, 0)),
                  pl.BlockSpec(ff.shape, lambda pp: (0, 0)),
                  pl.BlockSpec(ffr.shape, lambda pp: (0, 0)),
                  pl.BlockSpec(fi.shape, lambda pp: (0, 0)),
                  pl.BlockSpec(skip.shape, lambda pp: (0, 0)),
                  pl.BlockSpec(memory_space=pl.ANY)],
        out_specs=pl.BlockSpec((1, 2, lc, w), lambda pp: (pp, 0, t // lc - 1, 0)),
        out_shape=jax.ShapeDtypeStruct(yh.shape, F32),
        input_output_aliases={8: 0},
        compiler_params=_cp(("parallel",)),
        name="hyena_ctx",
    )(hc4[0], hc4[1], hc4[2], filt_c, ff, ffr, fi, skip, yh)


def _hyena_filters(l, w1, b1, w2, b2, w3, freq):
    t = jnp.linspace(0.0, 1.0, l, dtype=F32)[:, None]
    bands = (HY_EMB - 1) // 2
    w = 2.0 * math.pi * jnp.arange(l, dtype=F32) / l
    f = jnp.linspace(1e-4, bands - 1, bands, dtype=F32)
    ang = w[:, None] * f[None, :]
    feats = jnp.concatenate([t, jnp.cos(ang), -jnp.sin(ang)], axis=-1)
    hdn = jnp.sin(freq[0] * (feats @ w1 + b1))
    hdn = jnp.sin(freq[1] * (hdn @ w2 + b2))
    n_out = w3.shape[-1]
    deltas = jnp.linspace(math.log(HY_TARGET) / HY_SLOW_PCT, math.log(HY_TARGET) / HY_FAST_PCT, n_out, dtype=F32)
    decay = jnp.exp(-t * jnp.abs(deltas))
    pos = jnp.arange(l)[:, None]
    parts = []
    for q in range(n_out // BRANCH_W):
        sl = slice(q * BRANCH_W, (q + 1) * BRANCH_W)
        f = (hdn @ w3[:, sl]).astype(F32) * decay[:, sl]
        parts.append(jnp.where(pos == 0, 0.0, f) if q % 2 else f)
    return jnp.stack(parts).reshape(n_out // (2 * BRANCH_W), 2, l, BRANCH_W)


def _hyena(z, conv_w, conv_b, filt_w, skip, nb, lat_len):
    m = z.shape[0]
    t = m // nb
    lc = t - lat_len
    w = BRANCH_W
    p = nb // 2
    n1, n2 = FFT_N1, 2 * lat_len // FFT_N1
    r = n1 // 2
    hl, hc = _hyconv(z, conv_w, conv_b, nb, lat_len)
    f1, f1r, f2, g2, f3 = _dft_tables(lat_len)
    ff, ffr, fi = _dft_tables_small(lc)
    jb, kb = 8, 8
    dp, fp = DFT_PASSES, DFT_FILTER_PASSES
    filt_l = _hyena_filters(lat_len, *filt_w)
    fa = _fft1(filt_l.reshape(1, 4, 1, r, n2, w), 0, f1r, jb, fp)
    hs_l = _fft2_filters(fa, f2, kb, fp)
    hl6 = hl.reshape(3, p, 2, r, n2, w)
    y, y_sel = hl6, 0
    for o in range(2):
        a = _fft1(y, y_sel, f1, jb, dp)
        bm = _fft2(a, hs_l, o, f2, g2, kb, dp)
        rows = r if o == 0 else t // n2
        y = _fft3(bm, f3, hl6, 1 + o, y, y_sel, skip[o:o + 1], rows, jb, dp)[None]
        y_sel = 0
    yh = y.reshape(p, 2, t, w)
    yh = _hyena_ctx(hc, _hyena_filters(lc, *filt_w), ff, ffr, fi, skip, yh, dp, fp)
    return yh.reshape(m, w)


def _merge_kernel(x_ref, mod_ref, ya_ref, yr_ref, yh_ref, yd_ref, wg_ref, bg_ref, wb_ref, wo_ref, g_ref, b_ref,
                  o_ref, *, alpha):
    x = x_ref[...]
    u = _modulate(x, mod_ref, 0, 1).astype(BF16)
    acc = None
    for i, y_ref in enumerate((ya_ref, yr_ref, yh_ref, yd_ref)):
        gate = jax.nn.sigmoid(_dot(u, wg_ref[i]) + bg_ref[i:i + 1, :])
        term = gate * _dot(y_ref[...].astype(BF16), wb_ref[i])
        acc = term if acc is None else acc + term
    mix = _dot(acc.astype(BF16), wo_ref[...])
    o_ref[...] = _gated_residual_ln(x, mix, mod_ref, 2, alpha, g_ref[...], b_ref[...])


def _merge(x, modx, ys, wg, bg, wb, wo, layer, g, b, alpha, tm):
    m, d = x.shape
    w = BRANCH_W
    row = lambda i: (i, 0)
    const2 = lambda i: (0, 0)
    one = pl.Buffered(1)
    stacked = lambda a: pl.BlockSpec((None,) + a.shape[1:], lambda i: (layer,) + (0,) * (a.ndim - 1),
                                     pipeline_mode=one)
    return pl.pallas_call(
        functools.partial(_merge_kernel, alpha=alpha),
        grid=(m // tm,),
        in_specs=[pl.BlockSpec((tm, d), row),
                  pl.BlockSpec((tm // SUB, 8, d), lambda i: (i, 0, 0)),
                  pl.BlockSpec((tm, w), row), pl.BlockSpec((tm, w), row),
                  pl.BlockSpec((tm, w), row), pl.BlockSpec((tm, w), row),
                  stacked(wg), pl.BlockSpec(bg.shape, const2), stacked(wb), stacked(wo),
                  pl.BlockSpec((1, d), const2), pl.BlockSpec((1, d), const2)],
        out_specs=pl.BlockSpec((tm, d), row),
        out_shape=jax.ShapeDtypeStruct((m, d), F32),
        compiler_params=_cp(("parallel",)),
        name="branch_merge",
    )(x, modx, *ys, wg, bg, wb, wo, g.reshape(1, d), b.reshape(1, d))


def _ffn_kernel(x_ref, mod_ref, w1_ref, w3_ref, w2_ref, g_ref, b_ref, o_ref, *, alpha, tf):
    x = x_ref[...]
    u = _modulate(x, mod_ref, 3, 4).astype(BF16)
    acc = None
    for j in range(w1_ref.shape[1] // tf):
        sl = slice(j * tf, (j + 1) * tf)
        h1 = _dot(u, w1_ref[:, sl])
        h3 = _dot(u, w3_ref[:, sl])
        part = _dot((h1 * jax.nn.sigmoid(h1) * h3).astype(BF16), w2_ref[sl, :])
        acc = part if acc is None else acc + part
    o_ref[...] = _gated_residual_ln(x, acc, mod_ref, 5, alpha, g_ref[...], b_ref[...])


def _ffn(x, modx, w1, w3, w2, layer, g, b, alpha, tm, tf):
    m, d = x.shape
    resident = lambda a: pl.BlockSpec((None,) + a.shape[1:], lambda i: (layer, 0, 0), pipeline_mode=pl.Buffered(1))
    return pl.pallas_call(
        functools.partial(_ffn_kernel, alpha=alpha, tf=tf),
        grid=(m // tm,),
        in_specs=[pl.BlockSpec((tm, d), lambda i: (i, 0)),
                  pl.BlockSpec((tm // SUB, 8, d), lambda i: (i, 0, 0)),
                  resident(w1), resident(w3), resident(w2),
                  pl.BlockSpec((1, d), lambda i: (0, 0)),
                  pl.BlockSpec((1, d), lambda i: (0, 0))],
        out_specs=pl.BlockSpec((tm, d), lambda i: (i, 0)),
        out_shape=jax.ShapeDtypeStruct((m, d), F32),
        compiler_params=_cp(("parallel",)),
        name="dense_swiglu",
    )(x, modx, w1, w3, w2, g.reshape(1, d), b.reshape(1, d))


def _route_kernel(x_ref, mod_ref, wr_ref, u_ref, lg_ref):
    u = _modulate(x_ref[...], mod_ref, 3, 4)
    u_ref[...] = u
    lg_ref[...] = jnp.dot(u, wr_ref[...], preferred_element_type=F32, precision=lax.Precision.HIGHEST)


def _route(x, modx, wr_pad, tm):
    m, d = x.shape
    return pl.pallas_call(
        _route_kernel,
        grid=(m // tm,),
        in_specs=[pl.BlockSpec((tm, d), lambda i: (i, 0)),
                  pl.BlockSpec((tm // SUB, 8, d), lambda i: (i, 0, 0)),
                  pl.BlockSpec((d, LANE), lambda i: (0, 0))],
        out_specs=[pl.BlockSpec((tm, d), lambda i: (i, 0)), pl.BlockSpec((tm, LANE), lambda i: (i, 0))],
        out_shape=[jax.ShapeDtypeStruct((m, d), F32), jax.ShapeDtypeStruct((m, LANE), F32)],
        compiler_params=_cp(("parallel",)),
        name="moe_router",
    )(x, modx, wr_pad)


def _moe_kernel(be_ref, nu_ref, xb_ref, w1_ref, w3_ref, w2_ref, o_ref, acc_ref, u_ref):
    i, j = pl.program_id(0), pl.program_id(1)
    live = i < nu_ref[0]

    @pl.when(j == 0)
    def _():
        acc_ref[...] = jnp.zeros_like(acc_ref)
        u_ref[...] = xb_ref[...].astype(BF16)

    @pl.when(live)
    def _():
        u = u_ref[...]
        h1 = _dot(u, w1_ref[0])
        h3 = _dot(u, w3_ref[0])
        acc_ref[...] += _dot((h1 * jax.nn.sigmoid(h1) * h3).astype(BF16), w2_ref[0])

    @pl.when(j == pl.num_programs(1) - 1)
    def _():
        o_ref[...] = acc_ref[...]


def _moe_experts(xb, blk_e, n_used, w1, w3, w2, layer, tm, tf):
    r, d = xb.shape
    ff = w1.shape[3]
    nj = ff // tf

    def wmap(i, j, be, nu):
        return (layer, be[i], 0, jnp.where(i < nu[0], j, nj - 1))

    def w2map(i, j, be, nu):
        return (layer, be[i], jnp.where(i < nu[0], j, nj - 1), 0)

    grid_spec = pltpu.PrefetchScalarGridSpec(
        num_scalar_prefetch=2,
        grid=(r // tm, nj),
        in_specs=[pl.BlockSpec((tm, d), lambda i, j, be, nu: (i, 0)),
                  pl.BlockSpec((None, 1, d, tf), wmap),
                  pl.BlockSpec((None, 1, d, tf), wmap),
                  pl.BlockSpec((None, 1, tf, d), w2map)],
        out_specs=pl.BlockSpec((tm, d), lambda i, j, be, nu: (i, 0)),
        scratch_shapes=[pltpu.VMEM((tm, d), F32), pltpu.VMEM((tm, d), BF16)],
    )
    return pl.pallas_call(
        _moe_kernel,
        grid_spec=grid_spec,
        out_shape=jax.ShapeDtypeStruct((r, d), F32),
        compiler_params=_cp(("arbitrary", "arbitrary")),
        name="moe_experts",
    )(blk_e, n_used, xb, w1, w3, w2)


def _ln_res_kernel(x_ref, f0_ref, f1_ref, wt_ref, mod_ref, g_ref, b_ref, o_ref, *, alpha):
    f = f0_ref[...] * wt_ref[:, 0:1] + f1_ref[...] * wt_ref[:, 1:2]
    o_ref[...] = _gated_residual_ln(x_ref[...], f, mod_ref, 5, alpha, g_ref[...], b_ref[...])


def _ln_res(x, f01, wts, modx, g, b, alpha, tm):
    m, d = x.shape
    nt = m // tm
    return pl.pallas_call(
        functools.partial(_ln_res_kernel, alpha=alpha),
        grid=(nt,),
        in_specs=[pl.BlockSpec((tm, d), lambda i: (i, 0)),
                  pl.BlockSpec((tm, d), lambda i: (i, 0)),
                  pl.BlockSpec((tm, d), lambda i: (i + nt, 0)),
                  pl.BlockSpec((tm, TOP_K), lambda i: (i, 0)),
                  pl.BlockSpec((tm // SUB, 8, d), lambda i: (i, 0, 0)),
                  pl.BlockSpec((1, d), lambda i: (0, 0)), pl.BlockSpec((1, d), lambda i: (0, 0))],
        out_specs=pl.BlockSpec((tm, d), lambda i: (i, 0)),
        out_shape=jax.ShapeDtypeStruct((m, d), F32),
        compiler_params=_cp(("parallel",)),
        name="moe_residual_ln",
    )(x, f01, f01, wts, modx, g.reshape(1, d), b.reshape(1, d))


def _moe(x, modx, router, w1, w3, w2, layer, g, b, alpha, tm_rows, tm, tf):
    m, d = x.shape
    wr_pad = jnp.zeros((d, LANE), F32).at[:, :N_EXPERTS].set(router)
    u, logits = _route(x, modx, wr_pad, tm_rows)
    top_v, top_i = lax.top_k(logits[:, :N_EXPERTS], TOP_K)
    wts = jax.nn.softmax(top_v, axis=-1)
    e_flat = top_i.reshape(-1).astype(jnp.int32)
    n_pair = m * TOP_K
    onehot = (e_flat[:, None] == jnp.arange(N_EXPERTS, dtype=jnp.int32)[None, :]).astype(jnp.int32)
    csum = jnp.cumsum(onehot, axis=0)
    counts = csum[-1]
    rank = jnp.take_along_axis(csum, e_flat[:, None], axis=1)[:, 0] - 1
    padded = (counts + tm - 1) // tm * tm
    pad_end = jnp.cumsum(padded)
    pad_start = pad_end - padded
    dest = pad_start[e_flat] + rank
    n_blk = -(-n_pair // tm) + N_EXPERTS
    r = n_blk * tm
    tok = jnp.arange(n_pair, dtype=jnp.int32) // TOP_K
    n_used = (pad_end[-1] // tm).astype(jnp.int32)
    blk = jnp.minimum(jnp.arange(n_blk, dtype=jnp.int32), n_used - 1) * tm
    blk_e = jnp.minimum(jnp.searchsorted(pad_end, blk, side='right'), N_EXPERTS - 1).astype(jnp.int32)
    _, tok_sorted = lax.sort_key_val(dest, tok)
    shift = pad_start - (jnp.cumsum(counts) - counts)
    src = jnp.arange(r, dtype=jnp.int32) - jnp.repeat(shift[blk_e], tm).astype(jnp.int32)
    row_tok = jnp.take(tok_sorted, jnp.clip(src, 0, n_pair - 1), mode="clip")
    xb = jnp.take(u, row_tok, axis=0, mode="clip")
    yb = _moe_experts(xb, blk_e, n_used.reshape(1), w1, w3, w2, layer, tm, tf)
    f01 = jnp.take(yb, dest.reshape(m, TOP_K).T.reshape(-1), axis=0, mode="clip")
    return _ln_res(x, f01, wts, modx, g, b, alpha, tm_rows)


def _rope_tables(nb, lat_len, ctx_len):
    rows = lat_len // GRID_W
    row = jnp.repeat(jnp.arange(rows, dtype=F32), GRID_W)
    col = jnp.tile(jnp.arange(GRID_W, dtype=F32), rows)
    n_freq = HEAD_DIM // 4
    inv = ROPE_THETA ** (-jnp.arange(n_freq, dtype=F32) / n_freq)
    ar, ac = row[:, None] * inv, col[:, None] * inv
    cos = jnp.concatenate([jnp.cos(ar), jnp.cos(ar), jnp.cos(ac), jnp.cos(ac)], axis=1)
    sin = jnp.concatenate([-jnp.sin(ar), jnp.sin(ar), -jnp.sin(ac), jnp.sin(ac)], axis=1)
    cos = jnp.concatenate([cos, jnp.ones((ctx_len, HEAD_DIM), F32)], axis=0)
    sin = jnp.concatenate([sin, jnp.zeros((ctx_len, HEAD_DIM), F32)], axis=0)
    return jnp.tile(cos, (nb, 1)), jnp.tile(sin, (nb, 1))


def _row_tile(m, pref):
    tm = pref
    while m % tm:
        tm //= 2
    return tm


def kernel(x, c, ctx, c_ctx, w_mod, b_mod, w_in, conv_a_w, conv_a_b, ret_decay, hy_conv_w, hy_conv_b, hy_w1, hy_b1, hy_w2, hy_b2, hy_w3, hy_freq, hy_skip, q_norm, k_norm, w_gate, b_gate, w_branch, w_o, ln_g, ln_b, ffn_w1, ffn_w3, ffn_w2, router, moe_w1, moe_w3, moe_w2):
    nb, lat_len, d = x.shape
    ctx_len = ctx.shape[1]
    depth = w_mod.shape[0]
    t = lat_len + ctx_len
    m = nb * t
    alpha = float((2 * depth) ** 0.25)
    assert lat_len % SUB == 0 and ctx_len % SUB == 0 and nb % 2 == 0

    act = jnp.concatenate([jax.nn.silu(c), jax.nn.silu(c_ctx)[None, :],
                           jnp.zeros((8 - nb - 1, d), F32)], axis=0)
    mods = _mod_all(act, w_mod, b_mod).reshape(depth, 8, N_MOD, d)
    mods = jnp.pad(mods, ((0, 0), (0, 0), (0, 8 - N_MOD), (0, 0)))
    tiles_per_batch = t // SUB
    tile_pos = np.arange(m // SUB) % tiles_per_batch
    tile_group = np.where(tile_pos < lat_len // SUB, np.arange(m // SUB) // tiles_per_batch, nb)
    cos_t, sin_t = _rope_tables(nb, lat_len, ctx_len)

    cast = lambda a: a.astype(BF16)
    w_in_b, w_gate_b, w_branch_b, w_o_b = cast(w_in), cast(w_gate), cast(w_branch), cast(w_o)
    ffn_b = (cast(ffn_w1), cast(ffn_w3), cast(ffn_w2))
    moe_b = (cast(moe_w1), cast(moe_w3), cast(moe_w2))

    rows = jnp.concatenate([x, ctx], axis=1).reshape(m, d)
    tm = _row_tile(m, 512)
    for l in range(depth):
        modx = mods[l][tile_group]
        z = _proj(rows, modx, w_in_b, l, _row_tile(m, 1024), 1024)
        rq, rk, aq, ak, av = _prep(z, cos_t, sin_t, q_norm[l], k_norm[l], tm)
        y_d = _attention(aq, ak, av, nb, lat_len)
        log_gamma = -jnp.exp(ret_decay[l].astype(F32))
        y_r = _retention(rq, rk, z, _retention_tables(log_gamma), nb, lat_len)
        y_a = _shortconv(z, conv_a_w[l], conv_a_b[l], nb, lat_len)
        y_h = _hyena(z, hy_conv_w[l], hy_conv_b[l],
                     (hy_w1[l], hy_b1[l], hy_w2[l], hy_b2[l], hy_w3[l], hy_freq[l]), hy_skip[l], nb, lat_len)
        rows = _merge(rows, modx, (y_a, y_r, y_h, y_d), w_gate_b, b_gate[l], w_branch_b, w_o_b, l,
                      ln_g[l, 0], ln_b[l, 0], alpha, tm)
        i = l // 2
        if l % 2 == 0:
            rows = _ffn(rows, modx, *ffn_b, i, ln_g[l, 1], ln_b[l, 1], alpha, tm, 256)
        else:
            rows = _moe(rows, modx, router[i], *moe_b, i, ln_g[l, 1], ln_b[l, 1], alpha, tm, 512, 1792)
    return rows.reshape(nb, t, d)[:, :lat_len]
```

```python
import functools
import math

import numpy as np
import jax
import jax.numpy as jnp
from jax import lax
from jax.experimental import pallas as pl
from jax.experimental.pallas import tpu as pltpu

F32 = jnp.float32
BF16 = jnp.bfloat16

GRID_W = 64
BRANCH_W = 512
RET_HEADS = 4
RET_D = 128
RET_CHUNK = 128
HY_EMB = 33
HY_FAST_PCT = 0.3
HY_SLOW_PCT = 1.5
HY_TARGET = 1e-2
ATT_HEADS = 4
ATT_KV_HEADS = 2
HEAD_DIM = 128
ROPE_THETA = 10000.0
N_EXPERTS = 8
TOP_K = 2
N_MOD = 6
A_W = 3 * BRANCH_W
R_W = 4 * RET_HEADS * RET_D
H_W = 3 * BRANCH_W
D_W = (ATT_HEADS + 2 * ATT_KV_HEADS) * HEAD_DIM
IN_W = A_W + R_W + H_W + D_W
LN_EPS = 1e-6

SUB = 256
LANE = 128
FFT_N1 = 128
RET_UNROLL = 2
ATT_TQ = 512
ATT_KC = 1024
DFT_PASSES = 1
DFT_FILTER_PASSES = 2
VMEM_LIMIT = 56 * 1024 * 1024


def _cp(sem, vmem=VMEM_LIMIT):
    return pltpu.CompilerParams(dimension_semantics=sem, vmem_limit_bytes=vmem)


def _dot(a, b):
    return jnp.dot(a, b, preferred_element_type=F32)


def _dot_nt(a, b):
    return lax.dot_general(a, b, (((1,), (1,)), ((), ())), preferred_element_type=F32)


def _dot_tn(a, b):
    return lax.dot_general(a, b, (((0,), (0,)), ((), ())), preferred_element_type=F32)


def _split_dot(f_cat, x, passes):
    xh = x.astype(BF16)
    if passes == 1:
        return _dot(f_cat, xh)
    m = f_cat.shape[0] // 2
    r = _dot(f_cat, xh)
    r = r[:m] + r[m:]
    if passes == 3:
        r = r + _dot(f_cat[:m], (x - xh.astype(F32)).astype(BF16))
    return r


def _dft_operand(f_cat, passes):
    return f_cat[..., :f_cat.shape[-2] // 2, :] if passes == 1 else f_cat


def _modulate(x, mod_ref, shift_row, scale_row):
    parts = []
    for s in range(x.shape[0] // SUB):
        sh = mod_ref[s, shift_row:shift_row + 1, :]
        sc = mod_ref[s, scale_row:scale_row + 1, :]
        parts.append(x[s * SUB:(s + 1) * SUB] * (1.0 + sc) + sh)
    return parts[0] if len(parts) == 1 else jnp.concatenate(parts, axis=0)


def _gated_residual_ln(x, f, mod_ref, gate_row, alpha, g, b):
    parts = []
    for s in range(x.shape[0] // SUB):
        gt = mod_ref[s, gate_row:gate_row + 1, :]
        parts.append(alpha * x[s * SUB:(s + 1) * SUB] + gt * f[s * SUB:(s + 1) * SUB])
    r = parts[0] if len(parts) == 1 else jnp.concatenate(parts, axis=0)
    mu = jnp.mean(r, axis=-1, keepdims=True)
    rc = r - mu
    var = jnp.mean(rc * rc, axis=-1, keepdims=True)
    return rc * lax.rsqrt(var + LN_EPS) * g + b


def _mod_kernel(a_ref, w_ref, b_ref, o_ref):
    o_ref[0] = _dot(a_ref[...].astype(BF16), w_ref[0].astype(BF16)) + b_ref[0]


def _mod_all(act, w_mod, b_mod):
    depth, d, n = w_mod.shape
    tn = 512
    return pl.pallas_call(
        _mod_kernel,
        grid=(depth, n // tn),
        in_specs=[pl.BlockSpec((8, d), lambda l, j: (0, 0)),
                  pl.BlockSpec((1, d, tn), lambda l, j: (l, 0, j)),
                  pl.BlockSpec((1, 1, tn), lambda l, j: (l, 0, j))],
        out_specs=pl.BlockSpec((1, 8, tn), lambda l, j: (l, 0, j)),
        out_shape=jax.ShapeDtypeStruct((depth, 8, n), F32),
        compiler_params=_cp(("parallel", "parallel")),
        name="mod_vectors",
    )(act, w_mod, b_mod.reshape(depth, 1, n))


def _proj_kernel(x_ref, mod_ref, w_ref, o_ref, u_ref, *, tn):
    j = pl.program_id(1)

    @pl.when(j == 0)
    def _():
        u_ref[...] = _modulate(x_ref[...], mod_ref, 0, 1).astype(BF16)
    col = pl.multiple_of(j * tn, tn)
    o_ref[...] = _dot(u_ref[...], w_ref[:, pl.ds(col, tn)])


def _proj(x, modx, w, layer, tm, tn):
    m, d = x.shape
    n = w.shape[2]
    return pl.pallas_call(
        functools.partial(_proj_kernel, tn=tn),
        grid=(m // tm, n // tn),
        in_specs=[pl.BlockSpec((tm, d), lambda i, j: (i, 0)),
                  pl.BlockSpec((tm // SUB, 8, d), lambda i, j: (i, 0, 0)),
                  pl.BlockSpec((None, d, n), lambda i, j: (layer, 0, 0), pipeline_mode=pl.Buffered(1))],
        out_specs=pl.BlockSpec((tm, tn), lambda i, j: (i, j)),
        out_shape=jax.ShapeDtypeStruct((m, n), F32),
        scratch_shapes=[pltpu.VMEM((tm, d), BF16)],
        compiler_params=_cp(("parallel", "arbitrary")),
        name="in_proj",
    )(x, modx, w)


def _prep_kernel(rq_ref, rk_ref, aq_ref, ak_ref, av_ref, cos_ref, sin_ref, gq_ref, gk_ref,
                 orq_ref, ork_ref, oaq_ref, oak_ref, oav_ref):
    c = cos_ref[...]
    s = sin_ref[...]
    lane = lax.broadcasted_iota(jnp.int32, c.shape, 1)
    first = (lane % 64) < 32

    def rope(x):
        partner = jnp.where(first, pltpu.roll(x, 96, 1), pltpu.roll(x, 32, 1))
        return x * c + partner * s

    def rms(x, g):
        return x * lax.rsqrt(jnp.mean(x * x, axis=-1, keepdims=True) + LN_EPS) * g

    for h in range(RET_HEADS):
        sl = slice(h * LANE, (h + 1) * LANE)
        orq_ref[:, sl] = rope(rq_ref[:, sl])
        ork_ref[:, sl] = rope(rk_ref[:, sl] * (RET_D ** -0.5))
    q_scale = (HEAD_DIM ** -0.5) * math.log2(math.e)
    for h in range(ATT_HEADS):
        sl = slice(h * LANE, (h + 1) * LANE)
        oaq_ref[:, sl] = (rope(rms(aq_ref[:, sl], gq_ref[...])) * q_scale).astype(BF16)
    ones_col = jnp.where(lane == 0, 1.0, 0.0).astype(BF16)
    for h in range(ATT_KV_HEADS):
        sl = slice(h * LANE, (h + 1) * LANE)
        oak_ref[:, sl] = rope(rms(ak_ref[:, sl], gk_ref[...])).astype(BF16)
        oav_ref[:, 2 * h * LANE:(2 * h + 1) * LANE] = av_ref[:, sl].astype(BF16)
        oav_ref[:, (2 * h + 1) * LANE:(2 * h + 2) * LANE] = ones_col


def _prep(z, cos_t, sin_t, gq, gk, tm):
    m = z.shape[0]
    qw, kw = ATT_HEADS * HEAD_DIM, ATT_KV_HEADS * HEAD_DIM
    rw = RET_HEADS * RET_D
    d_off = A_W + R_W + H_W
    row = lambda i: (i, 0)
    return pl.pallas_call(
        _prep_kernel,
        grid=(m // tm,),
        in_specs=[pl.BlockSpec((tm, rw), lambda i: (i, A_W // rw)),
                  pl.BlockSpec((tm, rw), lambda i: (i, A_W // rw + 1)),
                  pl.BlockSpec((tm, qw), lambda i: (i, d_off // qw)),
                  pl.BlockSpec((tm, kw), lambda i: (i, (d_off + qw) // kw)),
                  pl.BlockSpec((tm, kw), lambda i: (i, (d_off + qw) // kw + 1)),
                  pl.BlockSpec((tm, LANE), row),
                  pl.BlockSpec((tm, LANE), row),
                  pl.BlockSpec((1, LANE), lambda i: (0, 0)),
                  pl.BlockSpec((1, LANE), lambda i: (0, 0))],
        out_specs=[pl.BlockSpec((tm, rw), row), pl.BlockSpec((tm, rw), row),
                   pl.BlockSpec((tm, qw), row), pl.BlockSpec((tm, kw), row), pl.BlockSpec((tm, 2 * kw), row)],
        out_shape=[jax.ShapeDtypeStruct((m, rw), F32), jax.ShapeDtypeStruct((m, rw), F32),
                   jax.ShapeDtypeStruct((m, qw), BF16), jax.ShapeDtypeStruct((m, kw), BF16),
                   jax.ShapeDtypeStruct((m, 2 * kw), BF16)],
        compiler_params=_cp(("parallel",)),
        name="qk_prep",
    )(z, z, z, z, z, cos_t, sin_t, gq.reshape(1, LANE), gk.reshape(1, LANE))


def _attn_kernel(q_ref, k_ref, v_ref, *rest, key_lo, key_hi, kc):
    o_ref = rest[-1]
    tq = q_ref.shape[1]
    grp = ATT_HEADS // ATT_KV_HEADS
    q2 = jnp.concatenate([q_ref[0, :, g * HEAD_DIM:(g + 1) * HEAD_DIM] for g in range(grp)], axis=0)
    m_run, acc = None, None
    for lo in range(key_lo, key_hi, kc):
        hi = min(lo + kc, key_hi)
        s = _dot_nt(q2, k_ref[0, lo:hi, :])
        m_chunk = jnp.max(s, axis=-1, keepdims=True)
        m_new = m_chunk if m_run is None else jnp.maximum(m_run, m_chunk)
        pv = _dot(jnp.exp2(s - m_new).astype(BF16), v_ref[0, lo:hi, :])
        acc = pv if acc is None else jnp.exp2(m_run - m_new) * acc + pv
        m_run = m_new
    for g in range(grp):
        a = acc[g * tq:(g + 1) * tq]
        o_ref[0, :, g * HEAD_DIM:(g + 1) * HEAD_DIM] = a[:, :HEAD_DIM] / a[:, HEAD_DIM:HEAD_DIM + 1]


def _attention(aq, ak, av, nb, lat_len):
    m = aq.shape[0]
    t = m // nb
    lc = t - lat_len
    grp_w = (ATT_HEADS // ATT_KV_HEADS) * HEAD_DIM
    args = (aq.reshape(nb, t, -1), ak.reshape(nb, t, -1), av.reshape(nb, t, -1))

    def call(tq, n_tiles, first_tile, key_lo, key_hi, prev):
        qmap = lambda b, kv, i: (b, first_tile + i, kv)
        specs = [pl.BlockSpec((1, tq, grp_w), qmap),
                 pl.BlockSpec((1, t, HEAD_DIM), lambda b, kv, i: (b, 0, kv)),
                 pl.BlockSpec((1, t, 2 * HEAD_DIM), lambda b, kv, i: (b, 0, kv))]
        extra, alias = (), {}
        if prev is not None:
            specs.append(pl.BlockSpec(memory_space=pl.ANY))
            extra, alias = (prev,), {3: 0}
        return pl.pallas_call(
            functools.partial(_attn_kernel, key_lo=key_lo, key_hi=key_hi, kc=ATT_KC),
            grid=(nb, ATT_KV_HEADS, n_tiles),
            in_specs=specs,
            out_specs=pl.BlockSpec((1, tq, grp_w), qmap),
            out_shape=jax.ShapeDtypeStruct((nb, t, ATT_HEADS * HEAD_DIM), F32),
            input_output_aliases=alias,
            compiler_params=_cp(("parallel", "parallel", "arbitrary")),
            name="gqa_attention",
        )(*args, *extra)

    tq = math.gcd(ATT_TQ, lat_len)
    out = call(tq, lat_len // tq, 0, 0, t, None)
    out = call(lc, 1, t // lc - 1, lat_len, t, out)
    return out.reshape(m, -1)


def _ret_kernel(q_ref, k_ref, v_ref, g_ref, intra_ref, qd_ref, kd_ref, cd_ref, o_ref, st_ref,
                *, n_lat, n_ctx):
    c_len = RET_CHUNK
    n = n_lat + n_ctx

    def rows(c):
        return pl.ds(pl.multiple_of(c * c_len, c_len), c_len)

    def kv(d, c):
        return _dot_tn((k_ref[0, rows(c), :] * kd_ref[d, 0]).astype(BF16), v_ref[0, rows(c), :].astype(BF16))

    def scan_step(t, carry):
        sf, sb = carry
        cf = jnp.where(t < n_ctx, n_lat + t, t - n_ctx)
        cb = jnp.where(t < n_ctx, n - 1 - t, n_lat - 1 - (t - n_ctx))
        st_ref[cf, :RET_D, :] = sf.astype(BF16)
        st_ref[cb, RET_D:, :] = sb.astype(BF16)
        return (cd_ref[0, 0, 0:1, :] * sf + kv(0, cf), cd_ref[1, 0, 0:1, :] * sb + kv(1, cb))

    zero = jnp.zeros((RET_D, RET_D), F32)
    lax.fori_loop(0, n, scan_step, (zero, zero), unroll=RET_UNROLL)

    mask = intra_ref[0, 0] + intra_ref[1, 0]

    def out_step(c, carry):
        qc = q_ref[0, rows(c), :]
        vc = v_ref[0, rows(c), :].astype(BF16)
        s = _dot_nt(qc.astype(BF16), k_ref[0, rows(c), :].astype(BF16)) * mask
        qdec = jnp.concatenate([qc * qd_ref[0, 0], qc * qd_ref[1, 0]], axis=1).astype(BF16)
        o = _dot(s.astype(BF16), vc) + _dot(qdec, st_ref[c])
        mu = jnp.mean(o, axis=-1, keepdims=True)
        oc = o - mu
        var = jnp.mean(oc * oc, axis=-1, keepdims=True)
        g = g_ref[0, rows(c), :]
        o_ref[0, rows(c), :] = (g * jax.nn.sigmoid(g)) * (oc * lax.rsqrt(var + LN_EPS))
        return carry

    lax.fori_loop(0, n, out_step, 0, unroll=RET_UNROLL)


def _retention(rq, rk, z, tabs, nb, lat_len):
    m = rq.shape[0]
    t = m // nb
    intra, qd, kd, cd = tabs
    v_blk = (A_W + 2 * RET_HEADS * RET_D) // LANE
    g_blk = v_blk + RET_HEADS
    kern = functools.partial(_ret_kernel, n_lat=lat_len // RET_CHUNK, n_ctx=(t - lat_len) // RET_CHUNK)
    seq = lambda off: pl.BlockSpec((1, t, LANE), lambda b, h: (b, 0, off + h))
    tab = lambda r: pl.BlockSpec((2, 1, r, LANE), lambda b, h: (0, h, 0, 0))
    out = pl.pallas_call(
        kern,
        grid=(nb, RET_HEADS),
        in_specs=[seq(0), seq(0), seq(v_blk), seq(g_blk),
                  tab(RET_CHUNK), tab(RET_CHUNK), tab(RET_CHUNK), tab(8)],
        out_specs=seq(0),
        out_shape=jax.ShapeDtypeStruct((nb, t, RET_HEADS * RET_D), F32),
        scratch_shapes=[pltpu.VMEM((t // RET_CHUNK, 2 * RET_D, RET_D), BF16)],
        compiler_params=_cp(("parallel", "parallel")),
        name="retention",
    )(rq.reshape(nb, t, -1), rk.reshape(nb, t, -1), z.reshape(nb, t, -1), z.reshape(nb, t, -1), intra, qd, kd, cd)
    return out.reshape(m, -1)


def _retention_tables(log_gamma):
    c = RET_CHUNK
    i = jnp.arange(c, dtype=F32)
    lg = log_gamma.astype(F32)[:, :, None]
    rel = i[:, None] - i[None, :]
    intra_f = jnp.where(rel >= 0, jnp.exp(lg[0][..., None] * jnp.maximum(rel, 0.0)), 0.0)
    intra_b = jnp.where(rel <= 0, jnp.exp(lg[1][..., None] * jnp.maximum(-rel, 0.0)), 0.0)
    qd = jnp.stack([jnp.exp(lg[0] * (i + 1.0)), jnp.exp(lg[1] * (c - i))])
    kd = jnp.stack([jnp.exp(lg[0] * (c - 1.0 - i)), jnp.exp(lg[1] * i)])
    cd = jnp.exp(lg * c)
    bc = lambda a: jnp.broadcast_to(a[..., None], a.shape + (LANE,))
    return (jnp.stack([intra_f, intra_b]), bc(qd), bc(kd),
            jnp.broadcast_to(cd[..., None], (2, RET_HEADS, 8, LANE)))


def _shift_rows(x, lat_len):
    t = x.shape[0]
    row = lax.broadcasted_iota(jnp.int32, x.shape, 0)
    xm = jnp.where((row == 0) | (row == lat_len), 0.0, pltpu.roll(x, 1, 0))
    xp = jnp.where((row == lat_len - 1) | (row == t - 1), 0.0, pltpu.roll(x, t - 1, 0))
    return xm, xp


def _shortconv_kernel(bg_ref, cg_ref, xv_ref, w_ref, b_ref, o_ref, *, lat_len):
    p = cg_ref[0] * xv_ref[0]
    pm, pp = _shift_rows(p, lat_len)
    o_ref[0] = bg_ref[0] * (pm * w_ref[0:1, :] + p * w_ref[1:2, :] + pp * w_ref[2:3, :] + b_ref[...])


def _shortconv(z, w, b, nb, lat_len):
    m = z.shape[0]
    t = m // nb
    nct = BRANCH_W // LANE
    kern = functools.partial(_shortconv_kernel, lat_len=lat_len)
    seq = lambda off: pl.BlockSpec((1, t, LANE), lambda bb, j: (bb, 0, off + j))
    out = pl.pallas_call(
        kern,
        grid=(nb, nct),
        in_specs=[seq(0), seq(nct), seq(2 * nct),
                  pl.BlockSpec((3, LANE), lambda bb, j: (0, j)),
                  pl.BlockSpec((1, LANE), lambda bb, j: (0, j))],
        out_specs=seq(0),
        out_shape=jax.ShapeDtypeStruct((nb, t, BRANCH_W), F32),
        compiler_params=_cp(("parallel", "parallel")),
        name="short_conv",
    )(z.reshape(nb, t, -1), z.reshape(nb, t, -1), z.reshape(nb, t, -1), w, b.reshape(1, -1))
    return out.reshape(m, -1)


def _hyconv_kernel(z_ref, w_ref, b_ref, ol_ref, oc_ref, *, lat_len):
    x = z_ref[0]
    xm, xp = _shift_rows(x, lat_len)
    y = xm * w_ref[0:1, :] + x * w_ref[1:2, :] + xp * w_ref[2:3, :] + b_ref[...]
    ol_ref[0, 0] = y[:lat_len]
    oc_ref[0, 0] = y[lat_len:]


def _hyconv(z, w, b, nb, lat_len):
    m = z.shape[0]
    t = m // nb
    nct = BRANCH_W // LANE
    off = (A_W + R_W) // LANE
    kern = functools.partial(_hyconv_kernel, lat_len=lat_len)
    return pl.pallas_call(
        kern,
        grid=(nb, 3 * nct),
        in_specs=[pl.BlockSpec((1, t, LANE), lambda bb, j: (bb, 0, off + j)),
                  pl.BlockSpec((3, LANE), lambda bb, j: (0, j)),
                  pl.BlockSpec((1, LANE), lambda bb, j: (0, j))],
        out_specs=[pl.BlockSpec((1, 1, lat_len, LANE), lambda bb, j: (j // nct, bb, 0, j % nct)),
                   pl.BlockSpec((1, 1, t - lat_len, LANE), lambda bb, j: (j // nct, bb, 0, j % nct))],
        out_shape=[jax.ShapeDtypeStruct((3, nb, lat_len, BRANCH_W), F32),
                   jax.ShapeDtypeStruct((3, nb, t - lat_len, BRANCH_W), F32)],
        compiler_params=_cp(("parallel", "parallel")),
        name="hyena_dwconv",
    )(z.reshape(nb, t, -1), w, b.reshape(1, -1))


def _stack_c(m):
    return np.block([[m.real, -m.imag], [m.imag, m.real]])


def _split_np(m):
    m32 = np.asarray(m, np.float32)
    hi = m32.astype(BF16)
    lo = (m32 - hi.astype(np.float32)).astype(BF16)
    return np.concatenate([hi, lo], axis=-2)


@functools.lru_cache(maxsize=None)
def _dft_tables(lat_len):
    n = 2 * lat_len
    n1, n2 = FFT_N1, 2 * lat_len // FFT_N1
    a1 = np.arange(n1)
    a2 = np.arange(n2)
    w1 = np.exp(-2j * np.pi * np.outer(a1, a1[:n1 // 2]) / n1)
    f1 = _stack_c(w1)
    w2 = np.exp(-2j * np.pi * np.outer(a2, a2) / n2)
    tw = np.exp(-2j * np.pi * np.outer(a1, a2) / n)
    cf = w2[None, :, :] * tw[:, None, :]
    f2 = np.stack([_stack_c(cf[k]) for k in range(n1)])
    g2 = np.stack([_stack_c(cf[k].conj().T) for k in range(n1)])
    w1i = np.exp(2j * np.pi * np.outer(a1[:n1 // 2], a1) / n1) / n
    f3 = _stack_c(w1i)
    f1_real = f1[:, :n1 // 2]
    return _split_np(f1), _split_np(f1_real), _split_np(f2), _split_np(g2), _split_np(f3)


@functools.lru_cache(maxsize=None)
def _dft_tables_small(ctx_len):
    n = 2 * ctx_len
    a = np.arange(n)
    ff = _stack_c(np.exp(-2j * np.pi * np.outer(a, a[:ctx_len]) / n))
    fi = _stack_c(np.exp(2j * np.pi * np.outer(a[:ctx_len], a) / n) / n)
    return _split_np(ff), _split_np(ff[:, :ctx_len]), _split_np(fi)


def _fft1_kernel(x_ref, f_ref, o_ref, *, passes):
    f = f_ref[...]
    for j in range(x_ref.shape[3]):
        parts = [x_ref[0, b, :, j, :] for b in range(x_ref.shape[1])]
        r = _split_dot(f, parts[0] if len(parts) == 1 else jnp.concatenate(parts, axis=0), passes)
        h = r.shape[0] // 2
        o_ref[0, 0, :, j, :] = r[:h]
        o_ref[0, 1, :, j, :] = r[h:]


def _fft1(x, sel, f1, jb, passes):
    _, p, mem, r, n2, w = x.shape
    f1 = _dft_operand(f1, passes)
    n1 = f1.shape[0] // (2 if passes == 1 else 4)
    return pl.pallas_call(
        functools.partial(_fft1_kernel, passes=passes),
        grid=(p, n2 // jb),
        in_specs=[pl.BlockSpec((None, 1, mem, r, jb, w), lambda pp, j: (sel, pp, 0, 0, j, 0)),
                  pl.BlockSpec(f1.shape, lambda pp, j: (0, 0))],
        out_specs=pl.BlockSpec((1, 2, n1, jb, w), lambda pp, j: (pp, 0, 0, j, 0)),
        out_shape=jax.ShapeDtypeStruct((p, 2, n1, n2, w), F32),
        compiler_params=_cp(("parallel", "parallel")),
        name="hyena_fft_stage1",
    )(x, f1)


def _fft2f_kernel(a_ref, f_ref, o_ref, *, kb, n2, passes):
    w = a_ref.shape[-1]
    for t in range(kb):
        xf = _split_dot(f_ref[t], a_ref[0, :, t].reshape(2 * n2, w), passes)
        xb = _split_dot(f_ref[t], a_ref[1, :, t].reshape(2 * n2, w), passes)
        o_ref[0, 0, t] = xf[:n2] + xb[:n2]
        o_ref[0, 1, t] = xf[n2:] - xb[n2:]


def _fft2_filters(a, f2, kb, passes):
    q, _, n1, n2, w = a.shape
    f2 = _dft_operand(f2, passes)
    return pl.pallas_call(
        functools.partial(_fft2f_kernel, kb=kb, n2=n2, passes=passes),
        grid=(q // 2, n1 // kb),
        in_specs=[pl.BlockSpec((2, 2, kb, n2, w), lambda o, k: (o, 0, k, 0, 0)),
                  pl.BlockSpec((kb,) + f2.shape[1:], lambda o, k: (k, 0, 0))],
        out_specs=pl.BlockSpec((1, 2, kb, n2, w), lambda o, k: (o, 0, k, 0, 0)),
        out_shape=jax.ShapeDtypeStruct((q // 2, 2, n1, n2, w), F32),
        compiler_params=_cp(("parallel", "parallel")),
        name="hyena_filter_spectrum",
    )(a, f2)


def _fft2_kernel(a_ref, h_ref, f_ref, g_ref, o_ref, *, kb, n2, passes):
    w = a_ref.shape[-1]
    for t in range(kb):
        a = a_ref[0, :, t].reshape(2 * n2, w)
        x = _split_dot(f_ref[t], a, passes)
        xr, xi = x[:n2], x[n2:]
        hr, hi = h_ref[0, t], h_ref[1, t]
        y = jnp.concatenate([xr * hr - xi * hi, xr * hi + xi * hr], axis=0)
        o_ref[0, :, t] = _split_dot(g_ref[t], y, passes).reshape(2, n2, w)


def _fft2(a, hspec, order, f2, g2, kb, passes):
    p, _, n1, n2, w = a.shape
    f2, g2 = _dft_operand(f2, passes), _dft_operand(g2, passes)
    kern = functools.partial(_fft2_kernel, kb=kb, n2=n2, passes=passes)
    return pl.pallas_call(
        kern,
        grid=(p, n1 // kb),
        in_specs=[pl.BlockSpec((1, 2, kb, n2, w), lambda pp, k: (pp, 0, k, 0, 0)),
                  pl.BlockSpec((None, 2, kb, n2, w), lambda pp, k: (order, 0, k, 0, 0)),
                  pl.BlockSpec((kb,) + f2.shape[1:], lambda pp, k: (k, 0, 0)),
                  pl.BlockSpec((kb,) + g2.shape[1:], lambda pp, k: (k, 0, 0))],
        out_specs=pl.BlockSpec((1, 2, kb, n2, w), lambda pp, k: (pp, 0, k, 0, 0)),
        out_shape=jax.ShapeDtypeStruct(a.shape, F32),
        compiler_params=_cp(("parallel", "parallel")),
        name="hyena_fft_stage2",
    )(a, hspec, f2, g2)


def _fft3_kernel(b_ref, f_ref, gate_ref, prev_ref, skip_ref, o_ref, *, passes):
    f = f_ref[...]
    r = gate_ref.shape[2]
    for j in range(b_ref.shape[3]):
        bj = jnp.concatenate([b_ref[0, 0, :, j, :], b_ref[0, 1, :, j, :]], axis=0)
        conv = _split_dot(f, bj, passes)
        for mem in range(2):
            cm = conv[mem * r:(mem + 1) * r]
            o_ref[0, mem, :, j, :] = gate_ref[0, mem, :, j, :] * (cm + skip_ref[...] * prev_ref[0, mem, :, j, :])


def _fft3(bm, f3, gate, gate_sel, prev, prev_sel, skip, out_rows, jb, passes):
    p, _, n1, n2, w = bm.shape
    r = n1 // 2
    f3 = _dft_operand(f3, passes)
    seq = lambda s: pl.BlockSpec((None, 1, 2, r, jb, w), lambda pp, j: (s, pp, 0, 0, j, 0))
    return pl.pallas_call(
        functools.partial(_fft3_kernel, passes=passes),
        grid=(p, n2 // jb),
        in_specs=[pl.BlockSpec((1, 2, n1, jb, w), lambda pp, j: (pp, 0, 0, j, 0)),
                  pl.BlockSpec(f3.shape, lambda pp, j: (0, 0)),
                  seq(gate_sel), seq(prev_sel),
                  pl.BlockSpec((1, w), lambda pp, j: (0, 0))],
        out_specs=pl.BlockSpec((1, 2, r, jb, w), lambda pp, j: (pp, 0, 0, j, 0)),
        out_shape=jax.ShapeDtypeStruct((p, 2, out_rows, n2, w), F32),
        compiler_params=_cp(("parallel", "parallel")),
        name="hyena_fft_stage3",
    )(bm, f3, gate, prev, skip)


def _hyctx_kernel(v_ref, x1_ref, x2_ref, filt_ref, ff_ref, ffr_ref, fi_ref, skip_ref, yh_hbm_ref, o_ref,
                  *, passes, filter_passes):
    del yh_hbm_ref
    lc, w = v_ref.shape[2], v_ref.shape[3]
    n = 2 * lc
    y = v_ref[0].reshape(2 * lc, w)
    for o, gate_ref in enumerate((x1_ref, x2_ref)):
        sf = _split_dot(ffr_ref[...], filt_ref[o, 0], filter_passes)
        sb = _split_dot(ffr_ref[...], filt_ref[o, 1], filter_passes)
        hr, hi = sf[:n] + sb[:n], sf[n:] - sb[n:]
        x = _split_dot(ff_ref[...], y, passes)
        xr, xi = x[:n], x[n:]
        spec = jnp.concatenate([xr * hr - xi * hi, xr * hi + xi * hr], axis=0)
        conv = _split_dot(fi_ref[...], spec, passes)
        y = gate_ref[0].reshape(2 * lc, w) * (conv + skip_ref[o:o + 1, :] * y)
    o_ref[0] = y.reshape(2, lc, w)


def _hyena_ctx(hc, filt_c, ff, ffr, fi, skip, yh, passes, filter_passes):
    _, nb, lc, w = hc.shape
    p = nb // 2
    t = yh.shape[2]
    ff, fi, ffr = _dft_operand(ff, passes), _dft_operand(fi, passes), _dft_operand(ffr, filter_passes)
    part = pl.BlockSpec((1, 2, lc, w), lambda pp: (pp, 0, 0, 0))
    hc4 = hc.reshape(3, p, 2, lc, w)
    return pl.pallas_call(
        functools.partial(_hyctx_kernel, passes=passes, filter_passes=filter_passes),
        grid=(p,),
        in_specs=[part, part, part,
                  pl.BlockSpec(filt_c.shape, lambda pp: (0, 0, 0, 0)),
                  pl.BlockSpec(ff.shape, lambda pp: (0, 0)),
                  pl.BlockSpec(ffr.shape, lambda pp: (0, 0)),
                  pl.BlockSpec(fi.shape, lambda pp: (0, 0)),
                  pl.BlockSpec(skip.shape, lambda pp: (0, 0)),
                  pl.BlockSpec(memory_space=pl.ANY)],
        out_specs=pl.BlockSpec((1, 2, lc, w), lambda pp: (pp, 0, t // lc - 1, 0)),
        out_shape=jax.ShapeDtypeStruct(yh.shape, F32),
        input_output_aliases={8: 0},
        compiler_params=_cp(("parallel",)),
        name="hyena_ctx",
    )(hc4[0], hc4[1], hc4[2], filt_c, ff, ffr, fi, skip, yh)


def _hyena_filters(l, w1, b1, w2, b2, w3, freq):
    t = jnp.linspace(0.0, 1.0, l, dtype=F32)[:, None]
    bands = (HY_EMB - 1) // 2
    w = 2.0 * math.pi * jnp.arange(l, dtype=F32) / l
    f = jnp.linspace(1e-4, bands - 1, bands, dtype=F32)
    ang = w[:, None] * f[None, :]
    feats = jnp.concatenate([t, jnp.cos(ang), -jnp.sin(ang)], axis=-1)
    hdn = jnp.sin(freq[0] * (feats @ w1 + b1))
    hdn = jnp.sin(freq[1] * (hdn @ w2 + b2))
    n_out = w3.shape[-1]
    deltas = jnp.linspace(math.log(HY_TARGET) / HY_SLOW_PCT, math.log(HY_TARGET) / HY_FAST_PCT, n_out, dtype=F32)
    decay = jnp.exp(-t * jnp.abs(deltas))
    pos = jnp.arange(l)[:, None]
    parts = []
    for q in range(n_out // BRANCH_W):
        sl = slice(q * BRANCH_W, (q + 1) * BRANCH_W)
        f = (hdn @ w3[:, sl]).astype(F32) * decay[:, sl]
        parts.append(jnp.where(pos == 0, 0.0, f) if q % 2 else f)
    return jnp.stack(parts).reshape(n_out // (2 * BRANCH_W), 2, l, BRANCH_W)


def _hyena(z, conv_w, conv_b, filt_w, skip, nb, lat_len):
    m = z.shape[0]
    t = m // nb
    lc = t - lat_len
    w = BRANCH_W
    p = nb // 2
    n1, n2 = FFT_N1, 2 * lat_len // FFT_N1
    r = n1 // 2
    hl, hc = _hyconv(z, conv_w, conv_b, nb, lat_len)
    f1, f1r, f2, g2, f3 = _dft_tables(lat_len)
    ff, ffr, fi = _dft_tables_small(lc)
    jb, kb = 8, 8
    dp, fp = DFT_PASSES, DFT_FILTER_PASSES
    filt_l = _hyena_filters(lat_len, *filt_w)
    fa = _fft1(filt_l.reshape(1, 4, 1, r, n2, w), 0, f1r, jb, fp)
    hs_l = _fft2_filters(fa, f2, kb, fp)
    hl6 = hl.reshape(3, p, 2, r, n2, w)
    y, y_sel = hl6, 0
    for o in range(2):
        a = _fft1(y, y_sel, f1, jb, dp)
        bm = _fft2(a, hs_l, o, f2, g2, kb, dp)
        rows = r if o == 0 else t // n2
        y = _fft3(bm, f3, hl6, 1 + o, y, y_sel, skip[o:o + 1], rows, jb, dp)[None]
        y_sel = 0
    yh = y.reshape(p, 2, t, w)
    yh = _hyena_ctx(hc, _hyena_filters(lc, *filt_w), ff, ffr, fi, skip, yh, dp, fp)
    return yh.reshape(m, w)


def _merge_kernel(x_ref, mod_ref, ya_ref, yr_ref, yh_ref, yd_ref, wg_ref, bg_ref, wb_ref, wo_ref, g_ref, b_ref,
                  o_ref, *, alpha):
    x = x_ref[...]
    u = _modulate(x, mod_ref, 0, 1).astype(BF16)
    acc = None
    for i, y_ref in enumerate((ya_ref, yr_ref, yh_ref, yd_ref)):
        gate = jax.nn.sigmoid(_dot(u, wg_ref[i]) + bg_ref[i:i + 1, :])
        term = gate * _dot(y_ref[...].astype(BF16), wb_ref[i])
        acc = term if acc is None else acc + term
    mix = _dot(acc.astype(BF16), wo_ref[...])
    o_ref[...] = _gated_residual_ln(x, mix, mod_ref, 2, alpha, g_ref[...], b_ref[...])


def _merge(x, modx, ys, wg, bg, wb, wo, layer, g, b, alpha, tm):
    m, d = x.shape
    w = BRANCH_W
    row = lambda i: (i, 0)
    const2 = lambda i: (0, 0)
    one = pl.Buffered(1)
    stacked = lambda a: pl.BlockSpec((None,) + a.shape[1:], lambda i: (layer,) + (0,) * (a.ndim - 1),
                                     pipeline_mode=one)
    return pl.pallas_call(
        functools.partial(_merge_kernel, alpha=alpha),
        grid=(m // tm,),
        in_specs=[pl.BlockSpec((tm, d), row),
                  pl.BlockSpec((tm // SUB, 8, d), lambda i: (i, 0, 0)),
                  pl.BlockSpec((tm, w), row), pl.BlockSpec((tm, w), row),
                  pl.BlockSpec((tm, w), row), pl.BlockSpec((tm, w), row),
                  stacked(wg), pl.BlockSpec(bg.shape, const2), stacked(wb), stacked(wo),
                  pl.BlockSpec((1, d), const2), pl.BlockSpec((1, d), const2)],
        out_specs=pl.BlockSpec((tm, d), row),
        out_shape=jax.ShapeDtypeStruct((m, d), F32),
        compiler_params=_cp(("parallel",)),
        name="branch_merge",
    )(x, modx, *ys, wg, bg, wb, wo, g.reshape(1, d), b.reshape(1, d))


def _ffn_kernel(x_ref, mod_ref, w1_ref, w3_ref, w2_ref, g_ref, b_ref, o_ref, *, alpha, tf):
    x = x_ref[...]
    u = _modulate(x, mod_ref, 3, 4).astype(BF16)
    acc = None
    for j in range(w1_ref.shape[1] // tf):
        sl = slice(j * tf, (j + 1) * tf)
        h1 = _dot(u, w1_ref[:, sl])
        h3 = _dot(u, w3_ref[:, sl])
        part = _dot((h1 * jax.nn.sigmoid(h1) * h3).astype(BF16), w2_ref[sl, :])
        acc = part if acc is None else acc + part
    o_ref[...] = _gated_residual_ln(x, acc, mod_ref, 5, alpha, g_ref[...], b_ref[...])


def _ffn(x, modx, w1, w3, w2, layer, g, b, alpha, tm, tf):
    m, d = x.shape
    resident = lambda a: pl.BlockSpec((None,) + a.shape[1:], lambda i: (layer, 0, 0), pipeline_mode=pl.Buffered(1))
    return pl.pallas_call(
        functools.partial(_ffn_kernel, alpha=alpha, tf=tf),
        grid=(m // tm,),
        in_specs=[pl.BlockSpec((tm, d), lambda i: (i, 0)),
                  pl.BlockSpec((tm // SUB, 8, d), lambda i: (i, 0, 0)),
                  resident(w1), resident(w3), resident(w2),
                  pl.BlockSpec((1, d), lambda i: (0, 0)),
                  pl.BlockSpec((1, d), lambda i: (0, 0))],
        out_specs=pl.BlockSpec((tm, d), lambda i: (i, 0)),
        out_shape=jax.ShapeDtypeStruct((m, d), F32),
        compiler_params=_cp(("parallel",)),
        name="dense_swiglu",
    )(x, modx, w1, w3, w2, g.reshape(1, d), b.reshape(1, d))


def _route_kernel(x_ref, mod_ref, wr_ref, u_ref, lg_ref):
    u = _modulate(x_ref[...], mod_ref, 3, 4)
    u_ref[...] = u
    lg_ref[...] = jnp.dot(u, wr_ref[...], preferred_element_type=F32, precision=lax.Precision.HIGHEST)


def _route(x, modx, wr_pad, tm):
    m, d = x.shape
    return pl.pallas_call(
        _route_kernel,
        grid=(m // tm,),
        in_specs=[pl.BlockSpec((tm, d), lambda i: (i, 0)),
                  pl.BlockSpec((tm // SUB, 8, d), lambda i: (i, 0, 0)),
                  pl.BlockSpec((d, LANE), lambda i: (0, 0))],
        out_specs=[pl.BlockSpec((tm, d), lambda i: (i, 0)), pl.BlockSpec((tm, LANE), lambda i: (i, 0))],
        out_shape=[jax.ShapeDtypeStruct((m, d), F32), jax.ShapeDtypeStruct((m, LANE), F32)],
        compiler_params=_cp(("parallel",)),
        name="moe_router",
    )(x, modx, wr_pad)


def _moe_kernel(be_ref, nu_ref, xb_ref, w1_ref, w3_ref, w2_ref, o_ref, acc_ref, u_ref):
    i, j = pl.program_id(0), pl.program_id(1)
    live = i < nu_ref[0]

    @pl.when(j == 0)
    def _():
        acc_ref[...] = jnp.zeros_like(acc_ref)
        u_ref[...] = xb_ref[...].astype(BF16)

    @pl.when(live)
    def _():
        u = u_ref[...]
        h1 = _dot(u, w1_ref[0].astype(BF16))
        h3 = _dot(u, w3_ref[0].astype(BF16))
        acc_ref[...] += _dot((h1 * jax.nn.sigmoid(h1) * h3).astype(BF16), w2_ref[0].astype(BF16))

    @pl.when(j == pl.num_programs(1) - 1)
    def _():
        o_ref[...] = acc_ref[...]


def _moe_experts(xb, blk_e, n_used, w1, w3, w2, layer, tm, tf):
    r, d = xb.shape
    ff = w1.shape[3]
    nj = ff // tf

    def wmap(i, j, be, nu):
        return (layer, be[i], 0, jnp.where(i < nu[0], j, nj - 1))

    def w2map(i, j, be, nu):
        return (layer, be[i], jnp.where(i < nu[0], j, nj - 1), 0)

    grid_spec = pltpu.PrefetchScalarGridSpec(
        num_scalar_prefetch=2,
        grid=(r // tm, nj),
        in_specs=[pl.BlockSpec((tm, d), lambda i, j, be, nu: (i, 0)),
                  pl.BlockSpec((None, 1, d, tf), wmap),
                  pl.BlockSpec((None, 1, d, tf), wmap),
                  pl.BlockSpec((None, 1, tf, d), w2map)],
        out_specs=pl.BlockSpec((tm, d), lambda i, j, be, nu: (i, 0)),
        scratch_shapes=[pltpu.VMEM((tm, d), F32), pltpu.VMEM((tm, d), BF16)],
    )
    return pl.pallas_call(
        _moe_kernel,
        grid_spec=grid_spec,
        out_shape=jax.ShapeDtypeStruct((r, d), F32),
        compiler_params=_cp(("arbitrary", "arbitrary")),
        name="moe_experts",
    )(blk_e, n_used, xb, w1, w3, w2)


def _ln_res_kernel(x_ref, f0_ref, f1_ref, wt_ref, mod_ref, g_ref, b_ref, o_ref, *, alpha):
    f = f0_ref[...] * wt_ref[:, 0:1] + f1_ref[...] * wt_ref[:, 1:2]
    o_ref[...] = _gated_residual_ln(x_ref[...], f, mod_ref, 5, alpha, g_ref[...], b_ref[...])


def _ln_res(x, f01, wts, modx, g, b, alpha, tm):
    m, d = x.shape
    nt = m // tm
    return pl.pallas_call(
        functools.partial(_ln_res_kernel, alpha=alpha),
        grid=(nt,),
        in_specs=[pl.BlockSpec((tm, d), lambda i: (i, 0)),
                  pl.BlockSpec((tm, d), lambda i: (i, 0)),
                  pl.BlockSpec((tm, d), lambda i: (i + nt, 0)),
                  pl.BlockSpec((tm, TOP_K), lambda i: (i, 0)),
                  pl.BlockSpec((tm // SUB, 8, d), lambda i: (i, 0, 0)),
                  pl.BlockSpec((1, d), lambda i: (0, 0)), pl.BlockSpec((1, d), lambda i: (0, 0))],
        out_specs=pl.BlockSpec((tm, d), lambda i: (i, 0)),
        out_shape=jax.ShapeDtypeStruct((m, d), F32),
        compiler_params=_cp(("parallel",)),
        name="moe_residual_ln",
    )(x, f01, f01, wts, modx, g.reshape(1, d), b.reshape(1, d))


def _moe(x, modx, router, w1, w3, w2, layer, g, b, alpha, tm_rows, tm, tf):
    m, d = x.shape
    wr_pad = jnp.zeros((d, LANE), F32).at[:, :N_EXPERTS].set(router)
    u, logits = _route(x, modx, wr_pad, tm_rows)
    top_v, top_i = lax.top_k(logits[:, :N_EXPERTS], TOP_K)
    wts = jax.nn.softmax(top_v, axis=-1)
    e_flat = top_i.reshape(-1).astype(jnp.int32)
    n_pair = m * TOP_K
    onehot = (e_flat[:, None] == jnp.arange(N_EXPERTS, dtype=jnp.int32)[None, :]).astype(jnp.int32)
    csum = jnp.cumsum(onehot, axis=0)
    counts = csum[-1]
    rank = jnp.take_along_axis(csum, e_flat[:, None], axis=1)[:, 0] - 1
    padded = (counts + tm - 1) // tm * tm
    pad_end = jnp.cumsum(padded)
    pad_start = pad_end - padded
    dest = pad_start[e_flat] + rank
    n_blk = -(-n_pair // tm) + N_EXPERTS
    r = n_blk * tm
    tok = jnp.arange(n_pair, dtype=jnp.int32) // TOP_K
    n_used = (pad_end[-1] // tm).astype(jnp.int32)
    blk = jnp.minimum(jnp.arange(n_blk, dtype=jnp.int32), n_used - 1) * tm
    blk_e = jnp.minimum(jnp.searchsorted(pad_end, blk, side='right'), N_EXPERTS - 1).astype(jnp.int32)
    _, tok_sorted = lax.sort_key_val(dest, tok)
    shift = pad_start - (jnp.cumsum(counts) - counts)
    src = jnp.arange(r, dtype=jnp.int32) - jnp.repeat(shift[blk_e], tm).astype(jnp.int32)
    row_tok = jnp.take(tok_sorted, jnp.clip(src, 0, n_pair - 1), mode="clip")
    xb = jnp.take(u, row_tok, axis=0, mode="clip")
    yb = _moe_experts(xb, blk_e, n_used.reshape(1), w1, w3, w2, layer, tm, tf)
    f01 = jnp.take(yb, dest.reshape(m, TOP_K).T.reshape(-1), axis=0, mode="clip")
    return _ln_res(x, f01, wts, modx, g, b, alpha, tm_rows)


def _rope_tables(nb, lat_len, ctx_len):
    rows = lat_len // GRID_W
    row = jnp.repeat(jnp.arange(rows, dtype=F32), GRID_W)
    col = jnp.tile(jnp.arange(GRID_W, dtype=F32), rows)
    n_freq = HEAD_DIM // 4
    inv = ROPE_THETA ** (-jnp.arange(n_freq, dtype=F32) / n_freq)
    ar, ac = row[:, None] * inv, col[:, None] * inv
    cos = jnp.concatenate([jnp.cos(ar), jnp.cos(ar), jnp.cos(ac), jnp.cos(ac)], axis=1)
    sin = jnp.concatenate([-jnp.sin(ar), jnp.sin(ar), -jnp.sin(ac), jnp.sin(ac)], axis=1)
    cos = jnp.concatenate([cos, jnp.ones((ctx_len, HEAD_DIM), F32)], axis=0)
    sin = jnp.concatenate([sin, jnp.zeros((ctx_len, HEAD_DIM), F32)], axis=0)
    return jnp.tile(cos, (nb, 1)), jnp.tile(sin, (nb, 1))


def _row_tile(m, pref):
    tm = pref
    while m % tm:
        tm //= 2
    return tm


def kernel(x, c, ctx, c_ctx, w_mod, b_mod, w_in, conv_a_w, conv_a_b, ret_decay, hy_conv_w, hy_conv_b, hy_w1, hy_b1, hy_w2, hy_b2, hy_w3, hy_freq, hy_skip, q_norm, k_norm, w_gate, b_gate, w_branch, w_o, ln_g, ln_b, ffn_w1, ffn_w3, ffn_w2, router, moe_w1, moe_w3, moe_w2):
    nb, lat_len, d = x.shape
    ctx_len = ctx.shape[1]
    depth = w_mod.shape[0]
    t = lat_len + ctx_len
    m = nb * t
    alpha = float((2 * depth) ** 0.25)
    assert lat_len % SUB == 0 and ctx_len % SUB == 0 and nb % 2 == 0

    act = jnp.concatenate([jax.nn.silu(c), jax.nn.silu(c_ctx)[None, :],
                           jnp.zeros((8 - nb - 1, d), F32)], axis=0)
    mods = _mod_all(act, w_mod, b_mod).reshape(depth, 8, N_MOD, d)
    mods = jnp.pad(mods, ((0, 0), (0, 0), (0, 8 - N_MOD), (0, 0)))
    tiles_per_batch = t // SUB
    tile_pos = np.arange(m // SUB) % tiles_per_batch
    tile_group = np.where(tile_pos < lat_len // SUB, np.arange(m // SUB) // tiles_per_batch, nb)
    cos_t, sin_t = _rope_tables(nb, lat_len, ctx_len)

    cast = lambda a: a.astype(BF16)
    w_in_b, w_gate_b, w_branch_b, w_o_b = cast(w_in), cast(w_gate), cast(w_branch), cast(w_o)
    ffn_b = (cast(ffn_w1), cast(ffn_w3), cast(ffn_w2))
    moe_w = (moe_w1, moe_w3, moe_w2)

    rows = jnp.concatenate([x, ctx], axis=1).reshape(m, d)
    tm = _row_tile(m, 512)
    for l in range(depth):
        modx = mods[l][tile_group]
        z = _proj(rows, modx, w_in_b, l, _row_tile(m, 1024), 1024)
        rq, rk, aq, ak, av = _prep(z, cos_t, sin_t, q_norm[l], k_norm[l], tm)
        y_d = _attention(aq, ak, av, nb, lat_len)
        log_gamma = -jnp.exp(ret_decay[l].astype(F32))
        y_r = _retention(rq, rk, z, _retention_tables(log_gamma), nb, lat_len)
        y_a = _shortconv(z, conv_a_w[l], conv_a_b[l], nb, lat_len)
        y_h = _hyena(z, hy_conv_w[l], hy_conv_b[l],
                     (hy_w1[l], hy_b1[l], hy_w2[l], hy_b2[l], hy_w3[l], hy_freq[l]), hy_skip[l], nb, lat_len)
        rows = _merge(rows, modx, (y_a, y_r, y_h, y_d), w_gate_b, b_gate[l], w_branch_b, w_o_b, l,
                      ln_g[l, 0], ln_b[l, 0], alpha, tm)
        i = l // 2
        if l % 2 == 0:
            rows = _ffn(rows, modx, *ffn_b, i, ln_g[l, 1], ln_b[l, 1], alpha, tm, 256)
        else:
            rows = _moe(rows, modx, router[i], *moe_w, i, ln_g[l, 1], ln_b[l, 1], alpha, tm, 512, 896)
    return rows.reshape(nb, t, d)[:, :lat_len]
```

```python
import functools
import math

import numpy as np
import jax
import jax.numpy as jnp
from jax import lax
from jax.experimental import pallas as pl
from jax.experimental.pallas import tpu as pltpu

F32 = jnp.float32
BF16 = jnp.bfloat16

GRID_W = 64
BRANCH_W = 512
RET_HEADS = 4
RET_D = 128
RET_CHUNK = 128
HY_EMB = 33
HY_FAST_PCT = 0.3
HY_SLOW_PCT = 1.5
HY_TARGET = 1e-2
ATT_HEADS = 4
ATT_KV_HEADS = 2
HEAD_DIM = 128
ROPE_THETA = 10000.0
N_EXPERTS = 8
TOP_K = 2
N_MOD = 6
A_W = 3 * BRANCH_W
R_W = 4 * RET_HEADS * RET_D
H_W = 3 * BRANCH_W
D_W = (ATT_HEADS + 2 * ATT_KV_HEADS) * HEAD_DIM
IN_W = A_W + R_W + H_W + D_W
LN_EPS = 1e-6

SUB = 256
LANE = 128
FFT_N1 = 128
RET_UNROLL = 8
ATT_TQ = 512
ATT_KC = 1024
DFT_PASSES = 1
DFT_FILTER_PASSES = 2
VMEM_LIMIT = 56 * 1024 * 1024


def _cp(sem, vmem=VMEM_LIMIT):
    return pltpu.CompilerParams(dimension_semantics=sem, vmem_limit_bytes=vmem)


def _dot(a, b):
    return jnp.dot(a, b, preferred_element_type=F32)


def _dot_nt(a, b):
    return lax.dot_general(a, b, (((1,), (1,)), ((), ())), preferred_element_type=F32)


def _dot_tn(a, b):
    return lax.dot_general(a, b, (((0,), (0,)), ((), ())), preferred_element_type=F32)


def _split_dot(f_cat, x, passes):
    xh = x.astype(BF16)
    if passes == 1:
        return _dot(f_cat, xh)
    m = f_cat.shape[0] // 2
    r = _dot(f_cat, xh)
    r = r[:m] + r[m:]
    if passes == 3:
        r = r + _dot(f_cat[:m], (x - xh.astype(F32)).astype(BF16))
    return r


def _dft_operand(f_cat, passes):
    return f_cat[..., :f_cat.shape[-2] // 2, :] if passes == 1 else f_cat


def _modulate(x, mod_ref, shift_row, scale_row):
    parts = []
    for s in range(x.shape[0] // SUB):
        sh = mod_ref[s, shift_row:shift_row + 1, :]
        sc = mod_ref[s, scale_row:scale_row + 1, :]
        parts.append(x[s * SUB:(s + 1) * SUB] * (1.0 + sc) + sh)
    return parts[0] if len(parts) == 1 else jnp.concatenate(parts, axis=0)


def _gated_residual_ln(x, f, mod_ref, gate_row, alpha, g, b):
    parts = []
    for s in range(x.shape[0] // SUB):
        gt = mod_ref[s, gate_row:gate_row + 1, :]
        parts.append(alpha * x[s * SUB:(s + 1) * SUB] + gt * f[s * SUB:(s + 1) * SUB])
    r = parts[0] if len(parts) == 1 else jnp.concatenate(parts, axis=0)
    mu = jnp.mean(r, axis=-1, keepdims=True)
    rc = r - mu
    var = jnp.mean(rc * rc, axis=-1, keepdims=True)
    return rc * lax.rsqrt(var + LN_EPS) * g + b


def _mod_kernel(a_ref, w_ref, b_ref, o_ref):
    o_ref[0] = _dot(a_ref[...].astype(BF16), w_ref[0].astype(BF16)) + b_ref[0]


def _mod_all(act, w_mod, b_mod):
    depth, d, n = w_mod.shape
    tn = 512
    return pl.pallas_call(
        _mod_kernel,
        grid=(depth, n // tn),
        in_specs=[pl.BlockSpec((8, d), lambda l, j: (0, 0)),
                  pl.BlockSpec((1, d, tn), lambda l, j: (l, 0, j)),
                  pl.BlockSpec((1, 1, tn), lambda l, j: (l, 0, j))],
        out_specs=pl.BlockSpec((1, 8, tn), lambda l, j: (l, 0, j)),
        out_shape=jax.ShapeDtypeStruct((depth, 8, n), F32),
        compiler_params=_cp(("parallel", "parallel")),
        name="mod_vectors",
    )(act, w_mod, b_mod.reshape(depth, 1, n))


PROJ_TN = 512
assert (A_W, R_W, H_W, D_W) == (3 * PROJ_TN, 4 * PROJ_TN, 3 * PROJ_TN, 2 * PROJ_TN)


def _proj_kernel(x_ref, mod_ref, w_ref, cos_ref, sin_ref, gq_ref, gk_ref,
                 za_ref, zvg_ref, zh_ref, rq_ref, rk_ref, aq_ref, ak_ref, av_ref, u_ref):
    j = pl.program_id(1)

    @pl.when(j == 0)
    def _():
        u_ref[...] = _modulate(x_ref[...], mod_ref, 0, 1).astype(BF16)

    col = pl.multiple_of(j * PROJ_TN, PROJ_TN)
    r = _dot(u_ref[...], w_ref[:, pl.ds(col, PROJ_TN)])

    def rope(x):
        lane = lax.broadcasted_iota(jnp.int32, x.shape, 1)
        partner = jnp.where((lane % 64) < 32, pltpu.roll(x, 96, 1), pltpu.roll(x, 32, 1))
        return x * cos_ref[...] + partner * sin_ref[...]

    def rms(x, g):
        return x * lax.rsqrt(jnp.mean(x * x, axis=-1, keepdims=True) + LN_EPS) * g

    head = lambda h: slice(h * LANE, (h + 1) * LANE)

    @pl.when(j < 3)
    def _():
        za_ref[...] = r

    @pl.when(j == 3)
    def _():
        for h in range(RET_HEADS):
            rq_ref[:, head(h)] = rope(r[:, head(h)])

    @pl.when(j == 4)
    def _():
        for h in range(RET_HEADS):
            rk_ref[:, head(h)] = rope(r[:, head(h)] * (RET_D ** -0.5))

    @pl.when((j == 5) | (j == 6))
    def _():
        zvg_ref[...] = r

    @pl.when((j >= 7) & (j < 10))
    def _():
        zh_ref[...] = r

    @pl.when(j == 10)
    def _():
        q_scale = (HEAD_DIM ** -0.5) * math.log2(math.e)
        for h in range(ATT_HEADS):
            aq_ref[:, head(h)] = (rope(rms(r[:, head(h)], gq_ref[...])) * q_scale).astype(BF16)

    @pl.when(j == 11)
    def _():
        lane = lax.broadcasted_iota(jnp.int32, (r.shape[0], LANE), 1)
        ones_col = jnp.where(lane == 0, 1.0, 0.0).astype(BF16)
        for h in range(ATT_KV_HEADS):
            ak_ref[:, head(h)] = rope(rms(r[:, head(h)], gk_ref[...])).astype(BF16)
            av_ref[:, head(2 * h)] = r[:, head(ATT_KV_HEADS + h)].astype(BF16)
            av_ref[:, head(2 * h + 1)] = ones_col


def _proj(x, modx, w, layer, cos_t, sin_t, gq, gk, tm):
    m, d = x.shape
    n = w.shape[2]
    tn = PROJ_TN
    qw, kw = ATT_HEADS * HEAD_DIM, ATT_KV_HEADS * HEAD_DIM
    row = lambda i, j: (i, 0)
    span = lambda first, count: (lambda i, j: (i, jnp.clip(j - first, 0, count - 1)))
    return pl.pallas_call(
        _proj_kernel,
        grid=(m // tm, n // tn),
        in_specs=[pl.BlockSpec((tm, d), row),
                  pl.BlockSpec((tm // SUB, 8, d), lambda i, j: (i, 0, 0)),
                  pl.BlockSpec((None, d, n), lambda i, j: (layer, 0, 0), pipeline_mode=pl.Buffered(1)),
                  pl.BlockSpec((tm, LANE), row), pl.BlockSpec((tm, LANE), row),
                  pl.BlockSpec((1, LANE), lambda i, j: (0, 0)), pl.BlockSpec((1, LANE), lambda i, j: (0, 0))],
        out_specs=[pl.BlockSpec((tm, tn), span(0, 3)), pl.BlockSpec((tm, tn), span(5, 2)),
                   pl.BlockSpec((tm, tn), span(7, 3)),
                   pl.BlockSpec((tm, tn), row), pl.BlockSpec((tm, tn), row),
                   pl.BlockSpec((tm, qw), row), pl.BlockSpec((tm, kw), row), pl.BlockSpec((tm, 2 * kw), row)],
        out_shape=[jax.ShapeDtypeStruct((m, A_W), F32), jax.ShapeDtypeStruct((m, 2 * tn), F32),
                   jax.ShapeDtypeStruct((m, H_W), F32),
                   jax.ShapeDtypeStruct((m, tn), F32), jax.ShapeDtypeStruct((m, tn), F32),
                   jax.ShapeDtypeStruct((m, qw), BF16), jax.ShapeDtypeStruct((m, kw), BF16),
                   jax.ShapeDtypeStruct((m, 2 * kw), BF16)],
        scratch_shapes=[pltpu.VMEM((tm, d), BF16)],
        compiler_params=_cp(("parallel", "arbitrary")),
        name="in_proj",
    )(x, modx, w, cos_t, sin_t, gq.reshape(1, LANE), gk.reshape(1, LANE))


def _attn_kernel(q_ref, k_ref, v_ref, *rest, key_lo, key_hi, kc):
    o_ref = rest[-1]
    tq = q_ref.shape[1]
    grp = ATT_HEADS // ATT_KV_HEADS
    q2 = jnp.concatenate([q_ref[0, :, g * HEAD_DIM:(g + 1) * HEAD_DIM] for g in range(grp)], axis=0)
    m_run, acc = None, None
    for lo in range(key_lo, key_hi, kc):
        hi = min(lo + kc, key_hi)
        s = _dot_nt(q2, k_ref[0, lo:hi, :])
        m_chunk = jnp.max(s, axis=-1, keepdims=True)
        m_new = m_chunk if m_run is None else jnp.maximum(m_run, m_chunk)
        pv = _dot(jnp.exp2(s - m_new).astype(BF16), v_ref[0, lo:hi, :])
        acc = pv if acc is None else jnp.exp2(m_run - m_new) * acc + pv
        m_run = m_new
    for g in range(grp):
        a = acc[g * tq:(g + 1) * tq]
        o_ref[0, :, g * HEAD_DIM:(g + 1) * HEAD_DIM] = a[:, :HEAD_DIM] / a[:, HEAD_DIM:HEAD_DIM + 1]


def _attention(aq, ak, av, nb, lat_len):
    m = aq.shape[0]
    t = m // nb
    lc = t - lat_len
    grp_w = (ATT_HEADS // ATT_KV_HEADS) * HEAD_DIM
    args = (aq.reshape(nb, t, -1), ak.reshape(nb, t, -1), av.reshape(nb, t, -1))

    def call(tq, n_tiles, first_tile, key_lo, key_hi, prev):
        qmap = lambda b, kv, i: (b, first_tile + i, kv)
        specs = [pl.BlockSpec((1, tq, grp_w), qmap),
                 pl.BlockSpec((1, t, HEAD_DIM), lambda b, kv, i: (b, 0, kv)),
                 pl.BlockSpec((1, t, 2 * HEAD_DIM), lambda b, kv, i: (b, 0, kv))]
        extra, alias = (), {}
        if prev is not None:
            specs.append(pl.BlockSpec(memory_space=pl.ANY))
            extra, alias = (prev,), {3: 0}
        return pl.pallas_call(
            functools.partial(_attn_kernel, key_lo=key_lo, key_hi=key_hi, kc=ATT_KC),
            grid=(nb, ATT_KV_HEADS, n_tiles),
            in_specs=specs,
            out_specs=pl.BlockSpec((1, tq, grp_w), qmap),
            out_shape=jax.ShapeDtypeStruct((nb, t, ATT_HEADS * HEAD_DIM), F32),
            input_output_aliases=alias,
            compiler_params=_cp(("parallel", "parallel", "arbitrary")),
            name="gqa_attention",
        )(*args, *extra)

    tq = math.gcd(ATT_TQ, lat_len)
    out = call(tq, lat_len // tq, 0, 0, t, None)
    out = call(lc, 1, t // lc - 1, lat_len, t, out)
    return out.reshape(m, -1)


def _ret_kernel(q_ref, k_ref, v_ref, g_ref, intra_ref, qd_ref, kd_ref, cd_ref, o_ref, st_ref,
                *, n_lat, n_ctx):
    c_len = RET_CHUNK
    n = n_lat + n_ctx

    def rows(c):
        return pl.ds(pl.multiple_of(c * c_len, c_len), c_len)

    def kv(d, c):
        return _dot_tn((k_ref[0, rows(c), :] * kd_ref[d, 0]).astype(BF16), v_ref[0, rows(c), :].astype(BF16))

    def scan_step(t, carry):
        sf, sb = carry
        cf = jnp.where(t < n_ctx, n_lat + t, t - n_ctx)
        cb = jnp.where(t < n_ctx, n - 1 - t, n_lat - 1 - (t - n_ctx))
        st_ref[cf, :RET_D, :] = sf.astype(BF16)
        st_ref[cb, RET_D:, :] = sb.astype(BF16)
        return (cd_ref[0, 0, 0:1, :] * sf + kv(0, cf), cd_ref[1, 0, 0:1, :] * sb + kv(1, cb))

    zero = jnp.zeros((RET_D, RET_D), F32)
    lax.fori_loop(0, n, scan_step, (zero, zero), unroll=RET_UNROLL)

    mask = intra_ref[0, 0] + intra_ref[1, 0]

    def out_step(c, carry):
        qc = q_ref[0, rows(c), :]
        vc = v_ref[0, rows(c), :].astype(BF16)
        s = _dot_nt(qc.astype(BF16), k_ref[0, rows(c), :].astype(BF16)) * mask
        qdec = jnp.concatenate([qc * qd_ref[0, 0], qc * qd_ref[1, 0]], axis=1).astype(BF16)
        o = _dot(s.astype(BF16), vc) + _dot(qdec, st_ref[c])
        mu = jnp.mean(o, axis=-1, keepdims=True)
        oc = o - mu
        var = jnp.mean(oc * oc, axis=-1, keepdims=True)
        g = g_ref[0, rows(c), :]
        o_ref[0, rows(c), :] = (g * jax.nn.sigmoid(g)) * (oc * lax.rsqrt(var + LN_EPS))
        return carry

    lax.fori_loop(0, n, out_step, 0, unroll=RET_UNROLL)


def _retention(rq, rk, z, tabs, nb, lat_len):
    m = rq.shape[0]
    t = m // nb
    intra, qd, kd, cd = tabs
    v_blk, g_blk = 0, RET_HEADS
    kern = functools.partial(_ret_kernel, n_lat=lat_len // RET_CHUNK, n_ctx=(t - lat_len) // RET_CHUNK)
    seq = lambda off: pl.BlockSpec((1, t, LANE), lambda b, h: (b, 0, off + h))
    tab = lambda r: pl.BlockSpec((2, 1, r, LANE), lambda b, h: (0, h, 0, 0))
    out = pl.pallas_call(
        kern,
        grid=(nb, RET_HEADS),
        in_specs=[seq(0), seq(0), seq(v_blk), seq(g_blk),
                  tab(RET_CHUNK), tab(RET_CHUNK), tab(RET_CHUNK), tab(8)],
        out_specs=seq(0),
        out_shape=jax.ShapeDtypeStruct((nb, t, RET_HEADS * RET_D), F32),
        scratch_shapes=[pltpu.VMEM((t // RET_CHUNK, 2 * RET_D, RET_D), BF16)],
        compiler_params=_cp(("parallel", "parallel")),
        name="retention",
    )(rq.reshape(nb, t, -1), rk.reshape(nb, t, -1), z.reshape(nb, t, -1), z.reshape(nb, t, -1), intra, qd, kd, cd)
    return out.reshape(m, -1)


def _retention_tables(log_gamma):
    c = RET_CHUNK
    i = jnp.arange(c, dtype=F32)
    lg = log_gamma.astype(F32)[:, :, None]
    rel = i[:, None] - i[None, :]
    intra_f = jnp.where(rel >= 0, jnp.exp(lg[0][..., None] * jnp.maximum(rel, 0.0)), 0.0)
    intra_b = jnp.where(rel <= 0, jnp.exp(lg[1][..., None] * jnp.maximum(-rel, 0.0)), 0.0)
    qd = jnp.stack([jnp.exp(lg[0] * (i + 1.0)), jnp.exp(lg[1] * (c - i))])
    kd = jnp.stack([jnp.exp(lg[0] * (c - 1.0 - i)), jnp.exp(lg[1] * i)])
    cd = jnp.exp(lg * c)
    bc = lambda a: jnp.broadcast_to(a[..., None], a.shape + (LANE,))
    return (jnp.stack([intra_f, intra_b]), bc(qd), bc(kd),
            jnp.broadcast_to(cd[..., None], (2, RET_HEADS, 8, LANE)))


def _shift_rows(x, lat_len):
    t = x.shape[0]
    row = lax.broadcasted_iota(jnp.int32, x.shape, 0)
    xm = jnp.where((row == 0) | (row == lat_len), 0.0, pltpu.roll(x, 1, 0))
    xp = jnp.where((row == lat_len - 1) | (row == t - 1), 0.0, pltpu.roll(x, t - 1, 0))
    return xm, xp


def _shortconv_kernel(bg_ref, cg_ref, xv_ref, w_ref, b_ref, o_ref, *, lat_len):
    p = cg_ref[0] * xv_ref[0]
    pm, pp = _shift_rows(p, lat_len)
    o_ref[0] = bg_ref[0] * (pm * w_ref[0:1, :] + p * w_ref[1:2, :] + pp * w_ref[2:3, :] + b_ref[...])


def _shortconv(z, w, b, nb, lat_len):
    m = z.shape[0]
    t = m // nb
    nct = BRANCH_W // LANE
    kern = functools.partial(_shortconv_kernel, lat_len=lat_len)
    seq = lambda off: pl.BlockSpec((1, t, LANE), lambda bb, j: (bb, 0, off + j))
    out = pl.pallas_call(
        kern,
        grid=(nb, nct),
        in_specs=[seq(0), seq(nct), seq(2 * nct),
                  pl.BlockSpec((3, LANE), lambda bb, j: (0, j)),
                  pl.BlockSpec((1, LANE), lambda bb, j: (0, j))],
        out_specs=seq(0),
        out_shape=jax.ShapeDtypeStruct((nb, t, BRANCH_W), F32),
        compiler_params=_cp(("parallel", "parallel")),
        name="short_conv",
    )(z.reshape(nb, t, -1), z.reshape(nb, t, -1), z.reshape(nb, t, -1), w, b.reshape(1, -1))
    return out.reshape(m, -1)


def _hyconv_kernel(z_ref, w_ref, b_ref, ol_ref, oc_ref, *, lat_len):
    x = z_ref[0]
    xm, xp = _shift_rows(x, lat_len)
    y = xm * w_ref[0:1, :] + x * w_ref[1:2, :] + xp * w_ref[2:3, :] + b_ref[...]
    ol_ref[0, 0] = y[:lat_len]
    oc_ref[0, 0] = y[lat_len:]


def _hyconv(z, w, b, nb, lat_len):
    m = z.shape[0]
    t = m // nb
    nct = BRANCH_W // LANE
    off = 0
    kern = functools.partial(_hyconv_kernel, lat_len=lat_len)
    return pl.pallas_call(
        kern,
        grid=(nb, 3 * nct),
        in_specs=[pl.BlockSpec((1, t, LANE), lambda bb, j: (bb, 0, off + j)),
                  pl.BlockSpec((3, LANE), lambda bb, j: (0, j)),
                  pl.BlockSpec((1, LANE), lambda bb, j: (0, j))],
        out_specs=[pl.BlockSpec((1, 1, lat_len, LANE), lambda bb, j: (j // nct, bb, 0, j % nct)),
                   pl.BlockSpec((1, 1, t - lat_len, LANE), lambda bb, j: (j // nct, bb, 0, j % nct))],
        out_shape=[jax.ShapeDtypeStruct((3, nb, lat_len, BRANCH_W), F32),
                   jax.ShapeDtypeStruct((3, nb, t - lat_len, BRANCH_W), F32)],
        compiler_params=_cp(("parallel", "parallel")),
        name="hyena_dwconv",
    )(z.reshape(nb, t, -1), w, b.reshape(1, -1))


def _stack_c(m):
    return np.block([[m.real, -m.imag], [m.imag, m.real]])


def _split_np(m):
    m32 = np.asarray(m, np.float32)
    hi = m32.astype(BF16)
    lo = (m32 - hi.astype(np.float32)).astype(BF16)
    return np.concatenate([hi, lo], axis=-2)


@functools.lru_cache(maxsize=None)
def _dft_tables(lat_len):
    n = 2 * lat_len
    n1, n2 = FFT_N1, 2 * lat_len // FFT_N1
    a1 = np.arange(n1)
    a2 = np.arange(n2)
    w1 = np.exp(-2j * np.pi * np.outer(a1, a1[:n1 // 2]) / n1)
    f1 = _stack_c(w1)
    w2 = np.exp(-2j * np.pi * np.outer(a2, a2) / n2)
    tw = np.exp(-2j * np.pi * np.outer(a1, a2) / n)
    cf = w2[None, :, :] * tw[:, None, :]
    f2 = np.stack([_stack_c(cf[k]) for k in range(n1)])
    g2 = np.stack([_stack_c(cf[k].conj().T) for k in range(n1)])
    w1i = np.exp(2j * np.pi * np.outer(a1[:n1 // 2], a1) / n1) / n
    f3 = _stack_c(w1i)
    f1_real = f1[:, :n1 // 2]
    return _split_np(f1), _split_np(f1_real), _split_np(f2), _split_np(g2), _split_np(f3)


@functools.lru_cache(maxsize=None)
def _dft_tables_small(ctx_len):
    n = 2 * ctx_len
    a = np.arange(n)
    ff = _stack_c(np.exp(-2j * np.pi * np.outer(a, a[:ctx_len]) / n))
    fi = _stack_c(np.exp(2j * np.pi * np.outer(a[:ctx_len], a) / n) / n)
    return _split_np(ff), _split_np(ff[:, :ctx_len]), _split_np(fi)


def _fft1_kernel(x_ref, f_ref, o_ref, *, passes):
    f = f_ref[...]
    for j in range(x_ref.shape[3]):
        parts = [x_ref[0, b, :, j, :] for b in range(x_ref.shape[1])]
        r = _split_dot(f, parts[0] if len(parts) == 1 else jnp.concatenate(parts, axis=0), passes)
        h = r.shape[0] // 2
        o_ref[0, 0, :, j, :] = r[:h]
        o_ref[0, 1, :, j, :] = r[h:]


def _fft1(x, sel, f1, jb, passes):
    _, p, mem, r, n2, w = x.shape
    f1 = _dft_operand(f1, passes)
    n1 = f1.shape[0] // (2 if passes == 1 else 4)
    return pl.pallas_call(
        functools.partial(_fft1_kernel, passes=passes),
        grid=(p, n2 // jb),
        in_specs=[pl.BlockSpec((None, 1, mem, r, jb, w), lambda pp, j: (sel, pp, 0, 0, j, 0)),
                  pl.BlockSpec(f1.shape, lambda pp, j: (0, 0))],
        out_specs=pl.BlockSpec((1, 2, n1, jb, w), lambda pp, j: (pp, 0, 0, j, 0)),
        out_shape=jax.ShapeDtypeStruct((p, 2, n1, n2, w), F32),
        compiler_params=_cp(("parallel", "parallel")),
        name="hyena_fft_stage1",
    )(x, f1)


def _fft2f_kernel(a_ref, f_ref, o_ref, *, kb, n2, passes):
    w = a_ref.shape[-1]
    for t in range(kb):
        xf = _split_dot(f_ref[t], a_ref[0, :, t].reshape(2 * n2, w), passes)
        xb = _split_dot(f_ref[t], a_ref[1, :, t].reshape(2 * n2, w), passes)
        o_ref[0, 0, t] = xf[:n2] + xb[:n2]
        o_ref[0, 1, t] = xf[n2:] - xb[n2:]


def _fft2_filters(a, f2, kb, passes):
    q, _, n1, n2, w = a.shape
    f2 = _dft_operand(f2, passes)
    return pl.pallas_call(
        functools.partial(_fft2f_kernel, kb=kb, n2=n2, passes=passes),
        grid=(q // 2, n1 // kb),
        in_specs=[pl.BlockSpec((2, 2, kb, n2, w), lambda o, k: (o, 0, k, 0, 0)),
                  pl.BlockSpec((kb,) + f2.shape[1:], lambda o, k: (k, 0, 0))],
        out_specs=pl.BlockSpec((1, 2, kb, n2, w), lambda o, k: (o, 0, k, 0, 0)),
        out_shape=jax.ShapeDtypeStruct((q // 2, 2, n1, n2, w), F32),
        compiler_params=_cp(("parallel", "parallel")),
        name="hyena_filter_spectrum",
    )(a, f2)


def _fft2_kernel(a_ref, h_ref, f_ref, g_ref, o_ref, *, kb, n2, passes):
    w = a_ref.shape[-1]
    for t in range(kb):
        a = a_ref[0, :, t].reshape(2 * n2, w)
        x = _split_dot(f_ref[t], a, passes)
        xr, xi = x[:n2], x[n2:]
        hr, hi = h_ref[0, t], h_ref[1, t]
        y = jnp.concatenate([xr * hr - xi * hi, xr * hi + xi * hr], axis=0)
        o_ref[0, :, t] = _split_dot(g_ref[t], y, passes).reshape(2, n2, w)


def _fft2(a, hspec, order, f2, g2, kb, passes):
    p, _, n1, n2, w = a.shape
    f2, g2 = _dft_operand(f2, passes), _dft_operand(g2, passes)
    kern = functools.partial(_fft2_kernel, kb=kb, n2=n2, passes=passes)
    return pl.pallas_call(
        kern,
        grid=(p, n1 // kb),
        in_specs=[pl.BlockSpec((1, 2, kb, n2, w), lambda pp, k: (pp, 0, k, 0, 0)),
                  pl.BlockSpec((None, 2, kb, n2, w), lambda pp, k: (order, 0, k, 0, 0)),
                  pl.BlockSpec((kb,) + f2.shape[1:], lambda pp, k: (k, 0, 0)),
                  pl.BlockSpec((kb,) + g2.shape[1:], lambda pp, k: (k, 0, 0))],
        out_specs=pl.BlockSpec((1, 2, kb, n2, w), lambda pp, k: (pp, 0, k, 0, 0)),
        out_shape=jax.ShapeDtypeStruct(a.shape, F32),
        compiler_params=_cp(("parallel", "parallel")),
        name="hyena_fft_stage2",
    )(a, hspec, f2, g2)


def _fft3_kernel(b_ref, f_ref, gate_ref, prev_ref, skip_ref, o_ref, *, passes):
    f = f_ref[...]
    r = gate_ref.shape[2]
    for j in range(b_ref.shape[3]):
        bj = jnp.concatenate([b_ref[0, 0, :, j, :], b_ref[0, 1, :, j, :]], axis=0)
        conv = _split_dot(f, bj, passes)
        for mem in range(2):
            cm = conv[mem * r:(mem + 1) * r]
            o_ref[0, mem, :, j, :] = gate_ref[0, mem, :, j, :] * (cm + skip_ref[...] * prev_ref[0, mem, :, j, :])


def _fft3(bm, f3, gate, gate_sel, prev, prev_sel, skip, out_rows, jb, passes):
    p, _, n1, n2, w = bm.shape
    r = n1 // 2
    f3 = _dft_operand(f3, passes)
    seq = lambda s: pl.BlockSpec((None, 1, 2, r, jb, w), lambda pp, j: (s, pp, 0, 0, j, 0))
    return pl.pallas_call(
        functools.partial(_fft3_kernel, passes=passes),
        grid=(p, n2 // jb),
        in_specs=[pl.BlockSpec((1, 2, n1, jb, w), lambda pp, j: (pp, 0, 0, j, 0)),
                  pl.BlockSpec(f3.shape, lambda pp, j: (0, 0)),
                  seq(gate_sel), seq(prev_sel),
                  pl.BlockSpec((1, w), lambda pp, j: (0, 0))],
        out_specs=pl.BlockSpec((1, 2, r, jb, w), lambda pp, j: (pp, 0, 0, j, 0)),
        out_shape=jax.ShapeDtypeStruct((p, 2, out_rows, n2, w), F32),
        compiler_params=_cp(("parallel", "parallel")),
        name="hyena_fft_stage3",
    )(bm, f3, gate, prev, skip)


def _hyctx_kernel(v_ref, x1_ref, x2_ref, filt_ref, ff_ref, ffr_ref, fi_ref, skip_ref, yh_hbm_ref, o_ref,
                  *, passes, filter_passes):
    del yh_hbm_ref
    lc, w = v_ref.shape[2], v_ref.shape[3]
    n = 2 * lc
    y = v_ref[0].reshape(2 * lc, w)
    for o, gate_ref in enumerate((x1_ref, x2_ref)):
        sf = _split_dot(ffr_ref[...], filt_ref[o, 0], filter_passes)
        sb = _split_dot(ffr_ref[...], filt_ref[o, 1], filter_passes)
        hr, hi = sf[:n] + sb[:n], sf[n:] - sb[n:]
        x = _split_dot(ff_ref[...], y, passes)
        xr, xi = x[:n], x[n:]
        spec = jnp.concatenate([xr * hr - xi * hi, xr * hi + xi * hr], axis=0)
        conv = _split_dot(fi_ref[...], spec, passes)
        y = gate_ref[0].reshape(2 * lc, w) * (conv + skip_ref[o:o + 1, :] * y)
    o_ref[0] = y.reshape(2, lc, w)


def _hyena_ctx(hc, filt_c, ff, ffr, fi, skip, yh, passes, filter_passes):
    _, nb, lc, w = hc.shape
    p = nb // 2
    t = yh.shape[2]
    ff, fi, ffr = _dft_operand(ff, passes), _dft_operand(fi, passes), _dft_operand(ffr, filter_passes)
    part = pl.BlockSpec((1, 2, lc, w), lambda pp: (pp, 0, 0, 0))
    hc4 = hc.reshape(3, p, 2, lc, w)
    return pl.pallas_call(
        functools.partial(_hyctx_kernel, passes=passes, filter_passes=filter_passes),
        grid=(p,),
        in_specs=[part, part, part,
                  pl.BlockSpec(filt_c.shape, lambda pp: (0, 0, 0, 0)),
                  pl.BlockSpec(ff.shape, lambda pp: (0, 0)),
                  pl.BlockSpec(ffr.shape, lambda pp: (0, 0)),
                  pl.BlockSpec(fi.shape, lambda pp: (0, 0)),
                  pl.BlockSpec(skip.shape, lambda pp: (0, 0)),
                  pl.BlockSpec(memory_space=pl.ANY)],
        out_specs=pl.BlockSpec((1, 2, lc, w), lambda pp: (pp, 0, t // lc - 1, 0)),
        out_shape=jax.ShapeDtypeStruct(yh.shape, F32),
        input_output_aliases={8: 0},
        compiler_params=_cp(("parallel",)),
        name="hyena_ctx",
    )(hc4[0], hc4[1], hc4[2], filt_c, ff, ffr, fi, skip, yh)


def _hyena_filters(l, w1, b1, w2, b2, w3, freq):
    t = jnp.linspace(0.0, 1.0, l, dtype=F32)[:, None]
    bands = (HY_EMB - 1) // 2
    w = 2.0 * math.pi * jnp.arange(l, dtype=F32) / l
    f = jnp.linspace(1e-4, bands - 1, bands, dtype=F32)
    ang = w[:, None] * f[None, :]
    feats = jnp.concatenate([t, jnp.cos(ang), -jnp.sin(ang)], axis=-1)
    hdn = jnp.sin(freq[0] * (feats @ w1 + b1))
    hdn = jnp.sin(freq[1] * (hdn @ w2 + b2))
    n_out = w3.shape[-1]
    deltas = jnp.linspace(math.log(HY_TARGET) / HY_SLOW_PCT, math.log(HY_TARGET) / HY_FAST_PCT, n_out, dtype=F32)
    decay = jnp.exp(-t * jnp.abs(deltas))
    pos = jnp.arange(l)[:, None]
    parts = []
    for q in range(n_out // BRANCH_W):
        sl = slice(q * BRANCH_W, (q + 1) * BRANCH_W)
        f = (hdn @ w3[:, sl]).astype(F32) * decay[:, sl]
        parts.append(jnp.where(pos == 0, 0.0, f) if q % 2 else f)
    return jnp.stack(parts).reshape(n_out // (2 * BRANCH_W), 2, l, BRANCH_W)


def _hyena(z, conv_w, conv_b, filt_w, skip, nb, lat_len):
    m = z.shape[0]
    t = m // nb
    lc = t - lat_len
    w = BRANCH_W
    p = nb // 2
    n1, n2 = FFT_N1, 2 * lat_len // FFT_N1
    r = n1 // 2
    hl, hc = _hyconv(z, conv_w, conv_b, nb, lat_len)
    f1, f1r, f2, g2, f3 = _dft_tables(lat_len)
    ff, ffr, fi = _dft_tables_small(lc)
    jb, kb = 8, 8
    dp, fp = DFT_PASSES, DFT_FILTER_PASSES
    filt_l = _hyena_filters(lat_len, *filt_w)
    fa = _fft1(filt_l.reshape(1, 4, 1, r, n2, w), 0, f1r, jb, fp)
    hs_l = _fft2_filters(fa, f2, kb, fp)
    hl6 = hl.reshape(3, p, 2, r, n2, w)
    y, y_sel = hl6, 0
    for o in range(2):
        a = _fft1(y, y_sel, f1, jb, dp)
        bm = _fft2(a, hs_l, o, f2, g2, kb, dp)
        rows = r if o == 0 else t // n2
        y = _fft3(bm, f3, hl6, 1 + o, y, y_sel, skip[o:o + 1], rows, jb, dp)[None]
        y_sel = 0
    yh = y.reshape(p, 2, t, w)
    yh = _hyena_ctx(hc, _hyena_filters(lc, *filt_w), ff, ffr, fi, skip, yh, dp, fp)
    return yh.reshape(m, w)


def _merge_kernel(x_ref, mod_ref, ya_ref, yr_ref, yh_ref, yd_ref, wg_ref, bg_ref, wb_ref, wo_ref, g_ref, b_ref,
                  o_ref, *, alpha):
    x = x_ref[...]
    u = _modulate(x, mod_ref, 0, 1).astype(BF16)
    acc = None
    for i, y_ref in enumerate((ya_ref, yr_ref, yh_ref, yd_ref)):
        gate = jax.nn.sigmoid(_dot(u, wg_ref[i]) + bg_ref[i:i + 1, :])
        term = gate * _dot(y_ref[...].astype(BF16), wb_ref[i])
        acc = term if acc is None else acc + term
    mix = _dot(acc.astype(BF16), wo_ref[...])
    o_ref[...] = _gated_residual_ln(x, mix, mod_ref, 2, alpha, g_ref[...], b_ref[...])


def _merge(x, modx, ys, wg, bg, wb, wo, layer, g, b, alpha, tm):
    m, d = x.shape
    w = BRANCH_W
    row = lambda i: (i, 0)
    const2 = lambda i: (0, 0)
    one = pl.Buffered(1)
    stacked = lambda a: pl.BlockSpec((None,) + a.shape[1:], lambda i: (layer,) + (0,) * (a.ndim - 1),
                                     pipeline_mode=one)
    return pl.pallas_call(
        functools.partial(_merge_kernel, alpha=alpha),
        grid=(m // tm,),
        in_specs=[pl.BlockSpec((tm, d), row),
                  pl.BlockSpec((tm // SUB, 8, d), lambda i: (i, 0, 0)),
                  pl.BlockSpec((tm, w), row), pl.BlockSpec((tm, w), row),
                  pl.BlockSpec((tm, w), row), pl.BlockSpec((tm, w), row),
                  stacked(wg), pl.BlockSpec(bg.shape, const2), stacked(wb), stacked(wo),
                  pl.BlockSpec((1, d), const2), pl.BlockSpec((1, d), const2)],
        out_specs=pl.BlockSpec((tm, d), row),
        out_shape=jax.ShapeDtypeStruct((m, d), F32),
        compiler_params=_cp(("parallel",)),
        name="branch_merge",
    )(x, modx, *ys, wg, bg, wb, wo, g.reshape(1, d), b.reshape(1, d))


def _ffn_kernel(x_ref, mod_ref, w1_ref, w3_ref, w2_ref, g_ref, b_ref, o_ref, *, alpha, tf):
    x = x_ref[...]
    u = _modulate(x, mod_ref, 3, 4).astype(BF16)
    acc = None
    for j in range(w1_ref.shape[1] // tf):
        sl = slice(j * tf, (j + 1) * tf)
        h1 = _dot(u, w1_ref[:, sl])
        h3 = _dot(u, w3_ref[:, sl])
        part = _dot((h1 * jax.nn.sigmoid(h1) * h3).astype(BF16), w2_ref[sl, :])
        acc = part if acc is None else acc + part
    o_ref[...] = _gated_residual_ln(x, acc, mod_ref, 5, alpha, g_ref[...], b_ref[...])


def _ffn(x, modx, w1, w3, w2, layer, g, b, alpha, tm, tf):
    m, d = x.shape
    resident = lambda a: pl.BlockSpec((None,) + a.shape[1:], lambda i: (layer, 0, 0), pipeline_mode=pl.Buffered(1))
    return pl.pallas_call(
        functools.partial(_ffn_kernel, alpha=alpha, tf=tf),
        grid=(m // tm,),
        in_specs=[pl.BlockSpec((tm, d), lambda i: (i, 0)),
                  pl.BlockSpec((tm // SUB, 8, d), lambda i: (i, 0, 0)),
                  resident(w1), resident(w3), resident(w2),
                  pl.BlockSpec((1, d), lambda i: (0, 0)),
                  pl.BlockSpec((1, d), lambda i: (0, 0))],
        out_specs=pl.BlockSpec((tm, d), lambda i: (i, 0)),
        out_shape=jax.ShapeDtypeStruct((m, d), F32),
        compiler_params=_cp(("parallel",)),
        name="dense_swiglu",
    )(x, modx, w1, w3, w2, g.reshape(1, d), b.reshape(1, d))


def _route_kernel(x_ref, mod_ref, wr_ref, u_ref, lg_ref):
    u = _modulate(x_ref[...], mod_ref, 3, 4)
    u_ref[...] = u
    lg_ref[...] = jnp.dot(u, wr_ref[...], preferred_element_type=F32, precision=lax.Precision.HIGHEST)


def _route(x, modx, wr_pad, tm):
    m, d = x.shape
    return pl.pallas_call(
        _route_kernel,
        grid=(m // tm,),
        in_specs=[pl.BlockSpec((tm, d), lambda i: (i, 0)),
                  pl.BlockSpec((tm // SUB, 8, d), lambda i: (i, 0, 0)),
                  pl.BlockSpec((d, LANE), lambda i: (0, 0))],
        out_specs=[pl.BlockSpec((tm, d), lambda i: (i, 0)), pl.BlockSpec((tm, LANE), lambda i: (i, 0))],
        out_shape=[jax.ShapeDtypeStruct((m, d), F32), jax.ShapeDtypeStruct((m, LANE), F32)],
        compiler_params=_cp(("parallel",)),
        name="moe_router",
    )(x, modx, wr_pad)


def _moe_kernel(be_ref, nu_ref, xb_ref, w1_ref, w3_ref, w2_ref, o_ref, acc_ref, u_ref):
    i, j = pl.program_id(0), pl.program_id(1)
    live = i < nu_ref[0]

    @pl.when(j == 0)
    def _():
        acc_ref[...] = jnp.zeros_like(acc_ref)
        u_ref[...] = xb_ref[...].astype(BF16)

    @pl.when(live)
    def _():
        u = u_ref[...]
        h1 = _dot(u, w1_ref[0])
        h3 = _dot(u, w3_ref[0])
        acc_ref[...] += _dot((h1 * jax.nn.sigmoid(h1) * h3).astype(BF16), w2_ref[0])

    @pl.when(j == pl.num_programs(1) - 1)
    def _():
        o_ref[...] = acc_ref[...]


def _moe_experts(xb, blk_e, n_used, w1, w3, w2, layer, tm, tf):
    r, d = xb.shape
    ff = w1.shape[3]
    nj = ff // tf

    def wmap(i, j, be, nu):
        return (layer, be[i], 0, jnp.where(i < nu[0], j, nj - 1))

    def w2map(i, j, be, nu):
        return (layer, be[i], jnp.where(i < nu[0], j, nj - 1), 0)

    grid_spec = pltpu.PrefetchScalarGridSpec(
        num_scalar_prefetch=2,
        grid=(r // tm, nj),
        in_specs=[pl.BlockSpec((tm, d), lambda i, j, be, nu: (i, 0)),
                  pl.BlockSpec((None, 1, d, tf), wmap),
                  pl.BlockSpec((None, 1, d, tf), wmap),
                  pl.BlockSpec((None, 1, tf, d), w2map)],
        out_specs=pl.BlockSpec((tm, d), lambda i, j, be, nu: (i, 0)),
        scratch_shapes=[pltpu.VMEM((tm, d), F32), pltpu.VMEM((tm, d), BF16)],
    )
    return pl.pallas_call(
        _moe_kernel,
        grid_spec=grid_spec,
        out_shape=jax.ShapeDtypeStruct((r, d), F32),
        compiler_params=_cp(("arbitrary", "arbitrary")),
        name="moe_experts",
    )(blk_e, n_used, xb, w1, w3, w2)


def _ln_res_kernel(x_ref, f0_ref, f1_ref, wt_ref, mod_ref, g_ref, b_ref, o_ref, *, alpha):
    f = f0_ref[...] * wt_ref[:, 0:1] + f1_ref[...] * wt_ref[:, 1:2]
    o_ref[...] = _gated_residual_ln(x_ref[...], f, mod_ref, 5, alpha, g_ref[...], b_ref[...])


def _ln_res(x, f01, wts, modx, g, b, alpha, tm):
    m, d = x.shape
    nt = m // tm
    return pl.pallas_call(
        functools.partial(_ln_res_kernel, alpha=alpha),
        grid=(nt,),
        in_specs=[pl.BlockSpec((tm, d), lambda i: (i, 0)),
                  pl.BlockSpec((tm, d), lambda i: (i, 0)),
                  pl.BlockSpec((tm, d), lambda i: (i + nt, 0)),
                  pl.BlockSpec((tm, TOP_K), lambda i: (i, 0)),
                  pl.BlockSpec((tm // SUB, 8, d), lambda i: (i, 0, 0)),
                  pl.BlockSpec((1, d), lambda i: (0, 0)), pl.BlockSpec((1, d), lambda i: (0, 0))],
        out_specs=pl.BlockSpec((tm, d), lambda i: (i, 0)),
        out_shape=jax.ShapeDtypeStruct((m, d), F32),
        compiler_params=_cp(("parallel",)),
        name="moe_residual_ln",
    )(x, f01, f01, wts, modx, g.reshape(1, d), b.reshape(1, d))


def _moe(x, modx, router, w1, w3, w2, layer, g, b, alpha, tm_rows, tm, tf):
    m, d = x.shape
    wr_pad = jnp.zeros((d, LANE), F32).at[:, :N_EXPERTS].set(router)
    u, logits = _route(x, modx, wr_pad, tm_rows)
    top_v, top_i = lax.top_k(logits[:, :N_EXPERTS], TOP_K)
    wts = jax.nn.softmax(top_v, axis=-1)
    e_flat = top_i.reshape(-1).astype(jnp.int32)
    n_pair = m * TOP_K
    onehot = (e_flat[:, None] == jnp.arange(N_EXPERTS, dtype=jnp.int32)[None, :]).astype(jnp.int32)
    csum = jnp.cumsum(onehot, axis=0)
    counts = csum[-1]
    rank = jnp.take_along_axis(csum, e_flat[:, None], axis=1)[:, 0] - 1
    padded = (counts + tm - 1) // tm * tm
    pad_end = jnp.cumsum(padded)
    pad_start = pad_end - padded
    dest = pad_start[e_flat] + rank
    n_blk = -(-n_pair // tm) + N_EXPERTS
    r = n_blk * tm
    tok = jnp.arange(n_pair, dtype=jnp.int32) // TOP_K
    n_used = (pad_end[-1] // tm).astype(jnp.int32)
    blk = jnp.minimum(jnp.arange(n_blk, dtype=jnp.int32), n_used - 1) * tm
    blk_e = jnp.minimum(jnp.searchsorted(pad_end, blk, side='right'), N_EXPERTS - 1).astype(jnp.int32)
    _, tok_sorted = lax.sort_key_val(dest, tok)
    shift = pad_start - (jnp.cumsum(counts) - counts)
    src = jnp.arange(r, dtype=jnp.int32) - jnp.repeat(shift[blk_e], tm).astype(jnp.int32)
    row_tok = jnp.take(tok_sorted, jnp.clip(src, 0, n_pair - 1), mode="clip")
    xb = jnp.take(u, row_tok, axis=0, mode="clip")
    yb = _moe_experts(xb, blk_e, n_used.reshape(1), w1, w3, w2, layer, tm, tf)
    f01 = jnp.take(yb, dest.reshape(m, TOP_K).T.reshape(-1), axis=0, mode="clip")
    return _ln_res(x, f01, wts, modx, g, b, alpha, tm_rows)


def _rope_tables(nb, lat_len, ctx_len):
    rows = lat_len // GRID_W
    row = jnp.repeat(jnp.arange(rows, dtype=F32), GRID_W)
    col = jnp.tile(jnp.arange(GRID_W, dtype=F32), rows)
    n_freq = HEAD_DIM // 4
    inv = ROPE_THETA ** (-jnp.arange(n_freq, dtype=F32) / n_freq)
    ar, ac = row[:, None] * inv, col[:, None] * inv
    cos = jnp.concatenate([jnp.cos(ar), jnp.cos(ar), jnp.cos(ac), jnp.cos(ac)], axis=1)
    sin = jnp.concatenate([-jnp.sin(ar), jnp.sin(ar), -jnp.sin(ac), jnp.sin(ac)], axis=1)
    cos = jnp.concatenate([cos, jnp.ones((ctx_len, HEAD_DIM), F32)], axis=0)
    sin = jnp.concatenate([sin, jnp.zeros((ctx_len, HEAD_DIM), F32)], axis=0)
    return jnp.tile(cos, (nb, 1)), jnp.tile(sin, (nb, 1))


def _row_tile(m, pref):
    tm = pref
    while m % tm:
        tm //= 2
    return tm


def kernel(x, c, ctx, c_ctx, w_mod, b_mod, w_in, conv_a_w, conv_a_b, ret_decay, hy_conv_w, hy_conv_b, hy_w1, hy_b1, hy_w2, hy_b2, hy_w3, hy_freq, hy_skip, q_norm, k_norm, w_gate, b_gate, w_branch, w_o, ln_g, ln_b, ffn_w1, ffn_w3, ffn_w2, router, moe_w1, moe_w3, moe_w2):
    nb, lat_len, d = x.shape
    ctx_len = ctx.shape[1]
    depth = w_mod.shape[0]
    t = lat_len + ctx_len
    m = nb * t
    alpha = float((2 * depth) ** 0.25)
    assert lat_len % SUB == 0 and ctx_len % SUB == 0 and nb % 2 == 0

    act = jnp.concatenate([jax.nn.silu(c), jax.nn.silu(c_ctx)[None, :],
                           jnp.zeros((8 - nb - 1, d), F32)], axis=0)
    mods = _mod_all(act, w_mod, b_mod).reshape(depth, 8, N_MOD, d)
    mods = jnp.pad(mods, ((0, 0), (0, 0), (0, 8 - N_MOD), (0, 0)))
    tiles_per_batch = t // SUB
    tile_pos = np.arange(m // SUB) % tiles_per_batch
    tile_group = np.where(tile_pos < lat_len // SUB, np.arange(m // SUB) // tiles_per_batch, nb)
    cos_t, sin_t = _rope_tables(nb, lat_len, ctx_len)

    cast = lambda a: a.astype(BF16)
    w_in_b, w_gate_b, w_branch_b, w_o_b = cast(w_in), cast(w_gate), cast(w_branch), cast(w_o)
    ffn_b = (cast(ffn_w1), cast(ffn_w3), cast(ffn_w2))
    moe_w = (cast(moe_w1), cast(moe_w3), cast(moe_w2))

    rows = jnp.concatenate([x, ctx], axis=1).reshape(m, d)
    tm = _row_tile(m, 512)
    for l in range(depth):
        modx = mods[l][tile_group]
        za, zvg, zh, rq, rk, aq, ak, av = _proj(rows, modx, w_in_b, l, cos_t, sin_t, q_norm[l], k_norm[l], tm)
        y_d = _attention(aq, ak, av, nb, lat_len)
        log_gamma = -jnp.exp(ret_decay[l].astype(F32))
        y_r = _retention(rq, rk, zvg, _retention_tables(log_gamma), nb, lat_len)
        y_a = _shortconv(za, conv_a_w[l], conv_a_b[l], nb, lat_len)
        y_h = _hyena(zh, hy_conv_w[l], hy_conv_b[l],
                     (hy_w1[l], hy_b1[l], hy_w2[l], hy_b2[l], hy_w3[l], hy_freq[l]), hy_skip[l], nb, lat_len)
        rows = _merge(rows, modx, (y_a, y_r, y_h, y_d), w_gate_b, b_gate[l], w_branch_b, w_o_b, l,
                      ln_g[l, 0], ln_b[l, 0], alpha, tm)
        i = l // 2
        if l % 2 == 0:
            rows = _ffn(rows, modx, *ffn_b, i, ln_g[l, 1], ln_b[l, 1], alpha, tm, 256)
        else:
            rows = _moe(rows, modx, router[i], *moe_w, i, ln_g[l, 1], ln_b[l, 1], alpha, tm, 512, 1792)
    return rows.reshape(nb, t, d)[:, :lat_len]
```

```python
import functools
import math

import numpy as np
import jax
import jax.numpy as jnp
from jax import lax
from jax.experimental import pallas as pl
from jax.experimental.pallas import tpu as pltpu

F32 = jnp.float32
BF16 = jnp.bfloat16

GRID_W = 64
BRANCH_W = 512
RET_HEADS = 4
RET_D = 128
RET_CHUNK = 128
HY_EMB = 33
HY_FAST_PCT = 0.3
HY_SLOW_PCT = 1.5
HY_TARGET = 1e-2
ATT_HEADS = 4
ATT_KV_HEADS = 2
HEAD_DIM = 128
ROPE_THETA = 10000.0
N_EXPERTS = 8
TOP_K = 2
N_MOD = 6
A_W = 3 * BRANCH_W
R_W = 4 * RET_HEADS * RET_D
H_W = 3 * BRANCH_W
D_W = (ATT_HEADS + 2 * ATT_KV_HEADS) * HEAD_DIM
IN_W = A_W + R_W + H_W + D_W
LN_EPS = 1e-6

SUB = 256
LANE = 128
FFT_N1 = 128
RET_UNROLL = 8
ATT_TQ = 512
ATT_KC = 1024
DFT_PASSES = 1
DFT_FILTER_PASSES = 2
VMEM_LIMIT = 56 * 1024 * 1024


def _cp(sem, vmem=VMEM_LIMIT):
    return pltpu.CompilerParams(dimension_semantics=sem, vmem_limit_bytes=vmem)


def _dot(a, b):
    return jnp.dot(a, b, preferred_element_type=F32)


def _dot_nt(a, b):
    return lax.dot_general(a, b, (((1,), (1,)), ((), ())), preferred_element_type=F32)


def _dot_tn(a, b):
    return lax.dot_general(a, b, (((0,), (0,)), ((), ())), preferred_element_type=F32)


def _split_dot(f_cat, x, passes):
    xh = x.astype(BF16)
    if passes == 1:
        return _dot(f_cat, xh)
    m = f_cat.shape[0] // 2
    r = _dot(f_cat, xh)
    r = r[:m] + r[m:]
    if passes == 3:
        r = r + _dot(f_cat[:m], (x - xh.astype(F32)).astype(BF16))
    return r


def _dft_operand(f_cat, passes):
    return f_cat[..., :f_cat.shape[-2] // 2, :] if passes == 1 else f_cat


def _modulate(x, mod_ref, shift_row, scale_row):
    parts = []
    for s in range(x.shape[0] // SUB):
        sh = mod_ref[s, shift_row:shift_row + 1, :]
        sc = mod_ref[s, scale_row:scale_row + 1, :]
        parts.append(x[s * SUB:(s + 1) * SUB] * (1.0 + sc) + sh)
    return parts[0] if len(parts) == 1 else jnp.concatenate(parts, axis=0)


def _gated_residual_ln(x, f, mod_ref, gate_row, alpha, g, b):
    parts = []
    for s in range(x.shape[0] // SUB):
        gt = mod_ref[s, gate_row:gate_row + 1, :]
        parts.append(alpha * x[s * SUB:(s + 1) * SUB] + gt * f[s * SUB:(s + 1) * SUB])
    r = parts[0] if len(parts) == 1 else jnp.concatenate(parts, axis=0)
    mu = jnp.mean(r, axis=-1, keepdims=True)
    rc = r - mu
    var = jnp.mean(rc * rc, axis=-1, keepdims=True)
    return rc * lax.rsqrt(var + LN_EPS) * g + b


def _mod_kernel(a_ref, w_ref, b_ref, o_ref):
    o_ref[0] = _dot(a_ref[...].astype(BF16), w_ref[0].astype(BF16)) + b_ref[0]


def _mod_all(act, w_mod, b_mod):
    depth, d, n = w_mod.shape
    tn = 512
    return pl.pallas_call(
        _mod_kernel,
        grid=(depth, n // tn),
        in_specs=[pl.BlockSpec((8, d), lambda l, j: (0, 0)),
                  pl.BlockSpec((1, d, tn), lambda l, j: (l, 0, j)),
                  pl.BlockSpec((1, 1, tn), lambda l, j: (l, 0, j))],
        out_specs=pl.BlockSpec((1, 8, tn), lambda l, j: (l, 0, j)),
        out_shape=jax.ShapeDtypeStruct((depth, 8, n), F32),
        compiler_params=_cp(("parallel", "parallel")),
        name="mod_vectors",
    )(act, w_mod, b_mod.reshape(depth, 1, n))


PROJ_TN = 512
assert (A_W, R_W, H_W, D_W) == (3 * PROJ_TN, 4 * PROJ_TN, 3 * PROJ_TN, 2 * PROJ_TN)


def _proj_kernel(x_ref, mod_ref, w_ref, cos_ref, sin_ref, gq_ref, gk_ref,
                 za_ref, zvg_ref, zh_ref, rq_ref, rk_ref, aq_ref, ak_ref, av_ref):
    u = _modulate(x_ref[...], mod_ref, 0, 1).astype(BF16)
    tn = PROJ_TN
    cos, sin = cos_ref[...], sin_ref[...]
    lane = lax.broadcasted_iota(jnp.int32, cos.shape, 1)
    first_half = (lane % 64) < 32

    def tile(j):
        return _dot(u, w_ref[:, j * tn:(j + 1) * tn])

    def rope(x):
        partner = jnp.where(first_half, pltpu.roll(x, 96, 1), pltpu.roll(x, 32, 1))
        return x * cos + partner * sin

    def rms(x, g):
        return x * lax.rsqrt(jnp.mean(x * x, axis=-1, keepdims=True) + LN_EPS) * g

    head = lambda h: slice(h * LANE, (h + 1) * LANE)

    for j in range(3):
        za_ref[:, j * tn:(j + 1) * tn] = tile(j)
    r = tile(3)
    for h in range(RET_HEADS):
        rq_ref[:, head(h)] = rope(r[:, head(h)])
    r = tile(4)
    for h in range(RET_HEADS):
        rk_ref[:, head(h)] = rope(r[:, head(h)] * (RET_D ** -0.5))
    for j in range(2):
        zvg_ref[:, j * tn:(j + 1) * tn] = tile(5 + j)
    for j in range(3):
        zh_ref[:, j * tn:(j + 1) * tn] = tile(7 + j)
    q_scale = (HEAD_DIM ** -0.5) * math.log2(math.e)
    r = tile(10)
    for h in range(ATT_HEADS):
        aq_ref[:, head(h)] = (rope(rms(r[:, head(h)], gq_ref[...])) * q_scale).astype(BF16)
    r = tile(11)
    ones_col = jnp.where(lane == 0, 1.0, 0.0).astype(BF16)
    for h in range(ATT_KV_HEADS):
        ak_ref[:, head(h)] = rope(rms(r[:, head(h)], gk_ref[...])).astype(BF16)
        av_ref[:, head(2 * h)] = r[:, head(ATT_KV_HEADS + h)].astype(BF16)
        av_ref[:, head(2 * h + 1)] = ones_col


def _proj(x, modx, w, layer, cos_t, sin_t, gq, gk, tm):
    m, d = x.shape
    n = w.shape[2]
    tn = PROJ_TN
    qw, kw = ATT_HEADS * HEAD_DIM, ATT_KV_HEADS * HEAD_DIM
    row = lambda i: (i, 0)
    widths = (A_W, 2 * tn, H_W, tn, tn, qw, kw, 2 * kw)
    dtypes = (F32,) * 5 + (BF16,) * 3
    return pl.pallas_call(
        _proj_kernel,
        grid=(m // tm,),
        in_specs=[pl.BlockSpec((tm, d), row),
                  pl.BlockSpec((tm // SUB, 8, d), lambda i: (i, 0, 0)),
                  pl.BlockSpec((None, d, n), lambda i: (layer, 0, 0), pipeline_mode=pl.Buffered(1)),
                  pl.BlockSpec((tm, LANE), row), pl.BlockSpec((tm, LANE), row),
                  pl.BlockSpec((1, LANE), lambda i: (0, 0)), pl.BlockSpec((1, LANE), lambda i: (0, 0))],
        out_specs=[pl.BlockSpec((tm, wd), row) for wd in widths],
        out_shape=[jax.ShapeDtypeStruct((m, wd), dt) for wd, dt in zip(widths, dtypes)],
        compiler_params=_cp(("parallel",)),
        name="in_proj",
    )(x, modx, w, cos_t, sin_t, gq.reshape(1, LANE), gk.reshape(1, LANE))


def _attn_kernel(q_ref, k_ref, v_ref, *rest, key_lo, key_hi, kc):
    o_ref = rest[-1]
    tq = q_ref.shape[1]
    grp = ATT_HEADS // ATT_KV_HEADS
    q2 = jnp.concatenate([q_ref[0, :, g * HEAD_DIM:(g + 1) * HEAD_DIM] for g in range(grp)], axis=0)
    m_run, acc = None, None
    for lo in range(key_lo, key_hi, kc):
        hi = min(lo + kc, key_hi)
        s = _dot_nt(q2, k_ref[0, lo:hi, :])
        m_chunk = jnp.max(s, axis=-1, keepdims=True)
        m_new = m_chunk if m_run is None else jnp.maximum(m_run, m_chunk)
        pv = _dot(jnp.exp2(s - m_new).astype(BF16), v_ref[0, lo:hi, :])
        acc = pv if acc is None else jnp.exp2(m_run - m_new) * acc + pv
        m_run = m_new
    for g in range(grp):
        a = acc[g * tq:(g + 1) * tq]
        o_ref[0, :, g * HEAD_DIM:(g + 1) * HEAD_DIM] = a[:, :HEAD_DIM] / a[:, HEAD_DIM:HEAD_DIM + 1]


def _attention(aq, ak, av, nb, lat_len):
    m = aq.shape[0]
    t = m // nb
    lc = t - lat_len
    grp_w = (ATT_HEADS // ATT_KV_HEADS) * HEAD_DIM
    args = (aq.reshape(nb, t, -1), ak.reshape(nb, t, -1), av.reshape(nb, t, -1))

    def call(tq, n_tiles, first_tile, key_lo, key_hi, prev):
        qmap = lambda b, kv, i: (b, first_tile + i, kv)
        specs = [pl.BlockSpec((1, tq, grp_w), qmap),
                 pl.BlockSpec((1, t, HEAD_DIM), lambda b, kv, i: (b, 0, kv)),
                 pl.BlockSpec((1, t, 2 * HEAD_DIM), lambda b, kv, i: (b, 0, kv))]
        extra, alias = (), {}
        if prev is not None:
            specs.append(pl.BlockSpec(memory_space=pl.ANY))
            extra, alias = (prev,), {3: 0}
        return pl.pallas_call(
            functools.partial(_attn_kernel, key_lo=key_lo, key_hi=key_hi, kc=ATT_KC),
            grid=(nb, ATT_KV_HEADS, n_tiles),
            in_specs=specs,
            out_specs=pl.BlockSpec((1, tq, grp_w), qmap),
            out_shape=jax.ShapeDtypeStruct((nb, t, ATT_HEADS * HEAD_DIM), F32),
            input_output_aliases=alias,
            compiler_params=_cp(("parallel", "parallel", "arbitrary")),
            name="gqa_attention",
        )(*args, *extra)

    tq = math.gcd(ATT_TQ, lat_len)
    out = call(tq, lat_len // tq, 0, 0, t, None)
    out = call(lc, 1, t // lc - 1, lat_len, t, out)
    return out.reshape(m, -1)


def _ret_kernel(q_ref, k_ref, v_ref, g_ref, intra_ref, qd_ref, kd_ref, cd_ref, o_ref, st_ref,
                *, n_lat, n_ctx):
    c_len = RET_CHUNK
    n = n_lat + n_ctx

    def rows(c):
        return pl.ds(pl.multiple_of(c * c_len, c_len), c_len)

    def kv(d, c):
        return _dot_tn((k_ref[0, rows(c), :] * kd_ref[d, 0]).astype(BF16), v_ref[0, rows(c), :].astype(BF16))

    def scan_step(t, carry):
        sf, sb = carry
        cf = jnp.where(t < n_ctx, n_lat + t, t - n_ctx)
        cb = jnp.where(t < n_ctx, n - 1 - t, n_lat - 1 - (t - n_ctx))
        st_ref[cf, :RET_D, :] = sf.astype(BF16)
        st_ref[cb, RET_D:, :] = sb.astype(BF16)
        return (cd_ref[0, 0, 0:1, :] * sf + kv(0, cf), cd_ref[1, 0, 0:1, :] * sb + kv(1, cb))

    zero = jnp.zeros((RET_D, RET_D), F32)
    lax.fori_loop(0, n, scan_step, (zero, zero), unroll=RET_UNROLL)

    mask = intra_ref[0, 0] + intra_ref[1, 0]

    def out_step(c, carry):
        qc = q_ref[0, rows(c), :]
        vc = v_ref[0, rows(c), :].astype(BF16)
        s = _dot_nt(qc.astype(BF16), k_ref[0, rows(c), :].astype(BF16)) * mask
        qdec = jnp.concatenate([qc * qd_ref[0, 0], qc * qd_ref[1, 0]], axis=1).astype(BF16)
        o = _dot(s.astype(BF16), vc) + _dot(qdec, st_ref[c])
        mu = jnp.mean(o, axis=-1, keepdims=True)
        oc = o - mu
        var = jnp.mean(oc * oc, axis=-1, keepdims=True)
        g = g_ref[0, rows(c), :]
        o_ref[0, rows(c), :] = (g * jax.nn.sigmoid(g)) * (oc * lax.rsqrt(var + LN_EPS))
        return carry

    lax.fori_loop(0, n, out_step, 0, unroll=RET_UNROLL)


def _retention(rq, rk, z, tabs, nb, lat_len):
    m = rq.shape[0]
    t = m // nb
    intra, qd, kd, cd = tabs
    v_blk, g_blk = 0, RET_HEADS
    kern = functools.partial(_ret_kernel, n_lat=lat_len // RET_CHUNK, n_ctx=(t - lat_len) // RET_CHUNK)
    seq = lambda off: pl.BlockSpec((1, t, LANE), lambda b, h: (b, 0, off + h))
    tab = lambda r: pl.BlockSpec((2, 1, r, LANE), lambda b, h: (0, h, 0, 0))
    out = pl.pallas_call(
        kern,
        grid=(nb, RET_HEADS),
        in_specs=[seq(0), seq(0), seq(v_blk), seq(g_blk),
                  tab(RET_CHUNK), tab(RET_CHUNK), tab(RET_CHUNK), tab(8)],
        out_specs=seq(0),
        out_shape=jax.ShapeDtypeStruct((nb, t, RET_HEADS * RET_D), F32),
        scratch_shapes=[pltpu.VMEM((t // RET_CHUNK, 2 * RET_D, RET_D), BF16)],
        compiler_params=_cp(("parallel", "parallel")),
        name="retention",
    )(rq.reshape(nb, t, -1), rk.reshape(nb, t, -1), z.reshape(nb, t, -1), z.reshape(nb, t, -1), intra, qd, kd, cd)
    return out.reshape(m, -1)


def _retention_tables(log_gamma):
    c = RET_CHUNK
    i = jnp.arange(c, dtype=F32)
    lg = log_gamma.astype(F32)[:, :, None]
    rel = i[:, None] - i[None, :]
    intra_f = jnp.where(rel >= 0, jnp.exp(lg[0][..., None] * jnp.maximum(rel, 0.0)), 0.0)
    intra_b = jnp.where(rel <= 0, jnp.exp(lg[1][..., None] * jnp.maximum(-rel, 0.0)), 0.0)
    qd = jnp.stack([jnp.exp(lg[0] * (i + 1.0)), jnp.exp(lg[1] * (c - i))])
    kd = jnp.stack([jnp.exp(lg[0] * (c - 1.0 - i)), jnp.exp(lg[1] * i)])
    cd = jnp.exp(lg * c)
    bc = lambda a: jnp.broadcast_to(a[..., None], a.shape + (LANE,))
    return (jnp.stack([intra_f, intra_b]), bc(qd), bc(kd),
            jnp.broadcast_to(cd[..., None], (2, RET_HEADS, 8, LANE)))


def _shift_rows(x, lat_len):
    t = x.shape[0]
    row = lax.broadcasted_iota(jnp.int32, x.shape, 0)
    xm = jnp.where((row == 0) | (row == lat_len), 0.0, pltpu.roll(x, 1, 0))
    xp = jnp.where((row == lat_len - 1) | (row == t - 1), 0.0, pltpu.roll(x, t - 1, 0))
    return xm, xp


def _shortconv_kernel(bg_ref, cg_ref, xv_ref, w_ref, b_ref, o_ref, *, lat_len):
    p = cg_ref[0] * xv_ref[0]
    pm, pp = _shift_rows(p, lat_len)
    o_ref[0] = bg_ref[0] * (pm * w_ref[0:1, :] + p * w_ref[1:2, :] + pp * w_ref[2:3, :] + b_ref[...])


def _shortconv(z, w, b, nb, lat_len):
    m = z.shape[0]
    t = m // nb
    nct = BRANCH_W // LANE
    kern = functools.partial(_shortconv_kernel, lat_len=lat_len)
    seq = lambda off: pl.BlockSpec((1, t, LANE), lambda bb, j: (bb, 0, off + j))
    out = pl.pallas_call(
        kern,
        grid=(nb, nct),
        in_specs=[seq(0), seq(nct), seq(2 * nct),
                  pl.BlockSpec((3, LANE), lambda bb, j: (0, j)),
                  pl.BlockSpec((1, LANE), lambda bb, j: (0, j))],
        out_specs=seq(0),
        out_shape=jax.ShapeDtypeStruct((nb, t, BRANCH_W), F32),
        compiler_params=_cp(("parallel", "parallel")),
        name="short_conv",
    )(z.reshape(nb, t, -1), z.reshape(nb, t, -1), z.reshape(nb, t, -1), w, b.reshape(1, -1))
    return out.reshape(m, -1)


def _hyconv_kernel(z_ref, w_ref, b_ref, ol_ref, oc_ref, *, lat_len):
    x = z_ref[0]
    xm, xp = _shift_rows(x, lat_len)
    y = xm * w_ref[0:1, :] + x * w_ref[1:2, :] + xp * w_ref[2:3, :] + b_ref[...]
    ol_ref[0, 0] = y[:lat_len]
    oc_ref[0, 0] = y[lat_len:]


def _hyconv(z, w, b, nb, lat_len):
    m = z.shape[0]
    t = m // nb
    nct = BRANCH_W // LANE
    off = 0
    kern = functools.partial(_hyconv_kernel, lat_len=lat_len)
    return pl.pallas_call(
        kern,
        grid=(nb, 3 * nct),
        in_specs=[pl.BlockSpec((1, t, LANE), lambda bb, j: (bb, 0, off + j)),
                  pl.BlockSpec((3, LANE), lambda bb, j: (0, j)),
                  pl.BlockSpec((1, LANE), lambda bb, j: (0, j))],
        out_specs=[pl.BlockSpec((1, 1, lat_len, LANE), lambda bb, j: (j // nct, bb, 0, j % nct)),
                   pl.BlockSpec((1, 1, t - lat_len, LANE), lambda bb, j: (j // nct, bb, 0, j % nct))],
        out_shape=[jax.ShapeDtypeStruct((3, nb, lat_len, BRANCH_W), F32),
                   jax.ShapeDtypeStruct((3, nb, t - lat_len, BRANCH_W), F32)],
        compiler_params=_cp(("parallel", "parallel")),
        name="hyena_dwconv",
    )(z.reshape(nb, t, -1), w, b.reshape(1, -1))


def _stack_c(m):
    return np.block([[m.real, -m.imag], [m.imag, m.real]])


def _split_np(m):
    m32 = np.asarray(m, np.float32)
    hi = m32.astype(BF16)
    lo = (m32 - hi.astype(np.float32)).astype(BF16)
    return np.concatenate([hi, lo], axis=-2)


@functools.lru_cache(maxsize=None)
def _dft_tables(lat_len):
    n = 2 * lat_len
    n1, n2 = FFT_N1, 2 * lat_len // FFT_N1
    a1 = np.arange(n1)
    a2 = np.arange(n2)
    w1 = np.exp(-2j * np.pi * np.outer(a1, a1[:n1 // 2]) / n1)
    f1 = _stack_c(w1)
    w2 = np.exp(-2j * np.pi * np.outer(a2, a2) / n2)
    tw = np.exp(-2j * np.pi * np.outer(a1, a2) / n)
    cf = w2[None, :, :] * tw[:, None, :]
    f2 = np.stack([_stack_c(cf[k]) for k in range(n1)])
    g2 = np.stack([_stack_c(cf[k].conj().T) for k in range(n1)])
    w1i = np.exp(2j * np.pi * np.outer(a1[:n1 // 2], a1) / n1) / n
    f3 = _stack_c(w1i)
    f1_real = f1[:, :n1 // 2]
    return _split_np(f1), _split_np(f1_real), _split_np(f2), _split_np(g2), _split_np(f3)


@functools.lru_cache(maxsize=None)
def _dft_tables_small(ctx_len):
    n = 2 * ctx_len
    a = np.arange(n)
    ff = _stack_c(np.exp(-2j * np.pi * np.outer(a, a[:ctx_len]) / n))
    fi = _stack_c(np.exp(2j * np.pi * np.outer(a[:ctx_len], a) / n) / n)
    return _split_np(ff), _split_np(ff[:, :ctx_len]), _split_np(fi)


def _fft1_kernel(x_ref, f_ref, o_ref, *, passes):
    f = f_ref[...]
    for j in range(x_ref.shape[3]):
        parts = [x_ref[0, b, :, j, :] for b in range(x_ref.shape[1])]
        r = _split_dot(f, parts[0] if len(parts) == 1 else jnp.concatenate(parts, axis=0), passes)
        h = r.shape[0] // 2
        o_ref[0, 0, :, j, :] = r[:h]
        o_ref[0, 1, :, j, :] = r[h:]


def _fft1(x, sel, f1, jb, passes):
    _, p, mem, r, n2, w = x.shape
    f1 = _dft_operand(f1, passes)
    n1 = f1.shape[0] // (2 if passes == 1 else 4)
    return pl.pallas_call(
        functools.partial(_fft1_kernel, passes=passes),
        grid=(p, n2 // jb),
        in_specs=[pl.BlockSpec((None, 1, mem, r, jb, w), lambda pp, j: (sel, pp, 0, 0, j, 0)),
                  pl.BlockSpec(f1.shape, lambda pp, j: (0, 0))],
        out_specs=pl.BlockSpec((1, 2, n1, jb, w), lambda pp, j: (pp, 0, 0, j, 0)),
        out_shape=jax.ShapeDtypeStruct((p, 2, n1, n2, w), F32),
        compiler_params=_cp(("parallel", "parallel")),
        name="hyena_fft_stage1",
    )(x, f1)


def _fft2f_kernel(a_ref, f_ref, o_ref, *, kb, n2, passes):
    w = a_ref.shape[-1]
    for t in range(kb):
        xf = _split_dot(f_ref[t], a_ref[0, :, t].reshape(2 * n2, w), passes)
        xb = _split_dot(f_ref[t], a_ref[1, :, t].reshape(2 * n2, w), passes)
        o_ref[0, 0, t] = xf[:n2] + xb[:n2]
        o_ref[0, 1, t] = xf[n2:] - xb[n2:]


def _fft2_filters(a, f2, kb, passes):
    q, _, n1, n2, w = a.shape
    f2 = _dft_operand(f2, passes)
    return pl.pallas_call(
        functools.partial(_fft2f_kernel, kb=kb, n2=n2, passes=passes),
        grid=(q // 2, n1 // kb),
        in_specs=[pl.BlockSpec((2, 2, kb, n2, w), lambda o, k: (o, 0, k, 0, 0)),
                  pl.BlockSpec((kb,) + f2.shape[1:], lambda o, k: (k, 0, 0))],
        out_specs=pl.BlockSpec((1, 2, kb, n2, w), lambda o, k: (o, 0, k, 0, 0)),
        out_shape=jax.ShapeDtypeStruct((q // 2, 2, n1, n2, w), F32),
        compiler_params=_cp(("parallel", "parallel")),
        name="hyena_filter_spectrum",
    )(a, f2)


def _fft2_kernel(a_ref, h_ref, f_ref, g_ref, o_ref, *, kb, n2, passes):
    w = a_ref.shape[-1]
    for t in range(kb):
        a = a_ref[0, :, t].reshape(2 * n2, w)
        x = _split_dot(f_ref[t], a, passes)
        xr, xi = x[:n2], x[n2:]
        hr, hi = h_ref[0, t], h_ref[1, t]
        y = jnp.concatenate([xr * hr - xi * hi, xr * hi + xi * hr], axis=0)
        o_ref[0, :, t] = _split_dot(g_ref[t], y, passes).reshape(2, n2, w)


def _fft2(a, hspec, order, f2, g2, kb, passes):
    p, _, n1, n2, w = a.shape
    f2, g2 = _dft_operand(f2, passes), _dft_operand(g2, passes)
    kern = functools.partial(_fft2_kernel, kb=kb, n2=n2, passes=passes)
    return pl.pallas_call(
        kern,
        grid=(p, n1 // kb),
        in_specs=[pl.BlockSpec((1, 2, kb, n2, w), lambda pp, k: (pp, 0, k, 0, 0)),
                  pl.BlockSpec((None, 2, kb, n2, w), lambda pp, k: (order, 0, k, 0, 0)),
                  pl.BlockSpec((kb,) + f2.shape[1:], lambda pp, k: (k, 0, 0)),
                  pl.BlockSpec((kb,) + g2.shape[1:], lambda pp, k: (k, 0, 0))],
        out_specs=pl.BlockSpec((1, 2, kb, n2, w), lambda pp, k: (pp, 0, k, 0, 0)),
        out_shape=jax.ShapeDtypeStruct(a.shape, F32),
        compiler_params=_cp(("parallel", "parallel")),
        name="hyena_fft_stage2",
    )(a, hspec, f2, g2)


def _fft3_kernel(b_ref, f_ref, gate_ref, prev_ref, skip_ref, o_ref, *, passes):
    f = f_ref[...]
    r = gate_ref.shape[2]
    for j in range(b_ref.shape[3]):
        bj = jnp.concatenate([b_ref[0, 0, :, j, :], b_ref[0, 1, :, j, :]], axis=0)
        conv = _split_dot(f, bj, passes)
        for mem in range(2):
            cm = conv[mem * r:(mem + 1) * r]
            o_ref[0, mem, :, j, :] = gate_ref[0, mem, :, j, :] * (cm + skip_ref[...] * prev_ref[0, mem, :, j, :])


def _fft3(bm, f3, gate, gate_sel, prev, prev_sel, skip, out_rows, jb, passes):
    p, _, n1, n2, w = bm.shape
    r = n1 // 2
    f3 = _dft_operand(f3, passes)
    seq = lambda s: pl.BlockSpec((None, 1, 2, r, jb, w), lambda pp, j: (s, pp, 0, 0, j, 0))
    return pl.pallas_call(
        functools.partial(_fft3_kernel, passes=passes),
        grid=(p, n2 // jb),
        in_specs=[pl.BlockSpec((1, 2, n1, jb, w), lambda pp, j: (pp, 0, 0, j, 0)),
                  pl.BlockSpec(f3.shape, lambda pp, j: (0, 0)),
                  seq(gate_sel), seq(prev_sel),
                  pl.BlockSpec((1, w), lambda pp, j: (0, 0))],
        out_specs=pl.BlockSpec((1, 2, r, jb, w), lambda pp, j: (pp, 0, 0, j, 0)),
        out_shape=jax.ShapeDtypeStruct((p, 2, out_rows, n2, w), F32),
        compiler_params=_cp(("parallel", "parallel")),
        name="hyena_fft_stage3",
    )(bm, f3, gate, prev, skip)


def _hyctx_kernel(v_ref, x1_ref, x2_ref, filt_ref, ff_ref, ffr_ref, fi_ref, skip_ref, yh_hbm_ref, o_ref,
                  *, passes, filter_passes):
    del yh_hbm_ref
    lc, w = v_ref.shape[2], v_ref.shape[3]
    n = 2 * lc
    y = v_ref[0].reshape(2 * lc, w)
    for o, gate_ref in enumerate((x1_ref, x2_ref)):
        sf = _split_dot(ffr_ref[...], filt_ref[o, 0], filter_passes)
        sb = _split_dot(ffr_ref[...], filt_ref[o, 1], filter_passes)
        hr, hi = sf[:n] + sb[:n], sf[n:] - sb[n:]
        x = _split_dot(ff_ref[...], y, passes)
        xr, xi = x[:n], x[n:]
        spec = jnp.concatenate([xr * hr - xi * hi, xr * hi + xi * hr], axis=0)
        conv = _split_dot(fi_ref[...], spec, passes)
        y = gate_ref[0].reshape(2 * lc, w) * (conv + skip_ref[o:o + 1, :] * y)
    o_ref[0] = y.reshape(2, lc, w)


def _hyena_ctx(hc, filt_c, ff, ffr, fi, skip, yh, passes, filter_passes):
    _, nb, lc, w = hc.shape
    p = nb // 2
    t = yh.shape[2]
    ff, fi, ffr = _dft_operand(ff, passes), _dft_operand(fi, passes), _dft_operand(ffr, filter_passes)
    part = pl.BlockSpec((1, 2, lc, w), lambda pp: (pp, 0, 0, 0))
    hc4 = hc.reshape(3, p, 2, lc, w)
    return pl.pallas_call(
        functools.partial(_hyctx_kernel, passes=passes, filter_passes=filter_passes),
        grid=(p,),
        in_specs=[part, part, part,
                  pl.BlockSpec(filt_c.shape, lambda pp: (0, 0, 0, 0)),
                  pl.BlockSpec(ff.shape, lambda pp: (0, 0)),
                  pl.BlockSpec(ffr.shape, lambda pp: (0, 0)),
                  pl.BlockSpec(fi.shape, lambda pp: (0, 0)),
                  pl.BlockSpec(skip.shape, lambda pp: (0, 0)),
                  pl.BlockSpec(memory_space=pl.ANY)],
        out_specs=pl.BlockSpec((1, 2, lc, w), lambda pp: (pp, 0, t // lc - 1, 0)),
        out_shape=jax.ShapeDtypeStruct(yh.shape, F32),
        input_output_aliases={8: 0},
        compiler_params=_cp(("parallel",)),
        name="hyena_ctx",
    )(hc4[0], hc4[1], hc4[2], filt_c, ff, ffr, fi, skip, yh)


def _hyena_filters(l, w1, b1, w2, b2, w3, freq):
    t = jnp.linspace(0.0, 1.0, l, dtype=F32)[:, None]
    bands = (HY_EMB - 1) // 2
    w = 2.0 * math.pi * jnp.arange(l, dtype=F32) / l
    f = jnp.linspace(1e-4, bands - 1, bands, dtype=F32)
    ang = w[:, None] * f[None, :]
    feats = jnp.concatenate([t, jnp.cos(ang), -jnp.sin(ang)], axis=-1)
    hdn = jnp.sin(freq[0] * (feats @ w1 + b1))
    hdn = jnp.sin(freq[1] * (hdn @ w2 + b2))
    n_out = w3.shape[-1]
    deltas = jnp.linspace(math.log(HY_TARGET) / HY_SLOW_PCT, math.log(HY_TARGET) / HY_FAST_PCT, n_out, dtype=F32)
    decay = jnp.exp(-t * jnp.abs(deltas))
    pos = jnp.arange(l)[:, None]
    parts = []
    for q in range(n_out // BRANCH_W):
        sl = slice(q * BRANCH_W, (q + 1) * BRANCH_W)
        f = (hdn @ w3[:, sl]).astype(F32) * decay[:, sl]
        parts.append(jnp.where(pos == 0, 0.0, f) if q % 2 else f)
    return jnp.stack(parts).reshape(n_out // (2 * BRANCH_W), 2, l, BRANCH_W)


def _hyena(z, conv_w, conv_b, filt_w, skip, nb, lat_len):
    m = z.shape[0]
    t = m // nb
    lc = t - lat_len
    w = BRANCH_W
    p = nb // 2
    n1, n2 = FFT_N1, 2 * lat_len // FFT_N1
    r = n1 // 2
    hl, hc = _hyconv(z, conv_w, conv_b, nb, lat_len)
    f1, f1r, f2, g2, f3 = _dft_tables(lat_len)
    ff, ffr, fi = _dft_tables_small(lc)
    jb, kb = 8, 8
    dp, fp = DFT_PASSES, DFT_FILTER_PASSES
    filt_l = _hyena_filters(lat_len, *filt_w)
    fa = _fft1(filt_l.reshape(1, 4, 1, r, n2, w), 0, f1r, jb, fp)
    hs_l = _fft2_filters(fa, f2, kb, fp)
    hl6 = hl.reshape(3, p, 2, r, n2, w)
    y, y_sel = hl6, 0
    for o in range(2):
        a = _fft1(y, y_sel, f1, jb, dp)
        bm = _fft2(a, hs_l, o, f2, g2, kb, dp)
        rows = r if o == 0 else t // n2
        y = _fft3(bm, f3, hl6, 1 + o, y, y_sel, skip[o:o + 1], rows, jb, dp)[None]
        y_sel = 0
    yh = y.reshape(p, 2, t, w)
    yh = _hyena_ctx(hc, _hyena_filters(lc, *filt_w), ff, ffr, fi, skip, yh, dp, fp)
    return yh.reshape(m, w)


def _merge_kernel(x_ref, mod_ref, ya_ref, yr_ref, yh_ref, yd_ref, wg_ref, bg_ref, wb_ref, wo_ref, g_ref, b_ref,
                  o_ref, *, alpha):
    x = x_ref[...]
    u = _modulate(x, mod_ref, 0, 1).astype(BF16)
    acc = None
    for i, y_ref in enumerate((ya_ref, yr_ref, yh_ref, yd_ref)):
        gate = jax.nn.sigmoid(_dot(u, wg_ref[i]) + bg_ref[i:i + 1, :])
        term = gate * _dot(y_ref[...].astype(BF16), wb_ref[i])
        acc = term if acc is None else acc + term
    mix = _dot(acc.astype(BF16), wo_ref[...])
    o_ref[...] = _gated_residual_ln(x, mix, mod_ref, 2, alpha, g_ref[...], b_ref[...])


def _merge(x, modx, ys, wg, bg, wb, wo, layer, g, b, alpha, tm):
    m, d = x.shape
    w = BRANCH_W
    row = lambda i: (i, 0)
    const2 = lambda i: (0, 0)
    one = pl.Buffered(1)
    stacked = lambda a: pl.BlockSpec((None,) + a.shape[1:], lambda i: (layer,) + (0,) * (a.ndim - 1),
                                     pipeline_mode=one)
    return pl.pallas_call(
        functools.partial(_merge_kernel, alpha=alpha),
        grid=(m // tm,),
        in_specs=[pl.BlockSpec((tm, d), row),
                  pl.BlockSpec((tm // SUB, 8, d), lambda i: (i, 0, 0)),
                  pl.BlockSpec((tm, w), row), pl.BlockSpec((tm, w), row),
                  pl.BlockSpec((tm, w), row), pl.BlockSpec((tm, w), row),
                  stacked(wg), pl.BlockSpec(bg.shape, const2), stacked(wb), stacked(wo),
                  pl.BlockSpec((1, d), const2), pl.BlockSpec((1, d), const2)],
        out_specs=pl.BlockSpec((tm, d), row),
        out_shape=jax.ShapeDtypeStruct((m, d), F32),
        compiler_params=_cp(("parallel",)),
        name="branch_merge",
    )(x, modx, *ys, wg, bg, wb, wo, g.reshape(1, d), b.reshape(1, d))


def _ffn_kernel(x_ref, mod_ref, w1_ref, w3_ref, w2_ref, g_ref, b_ref, o_ref, *, alpha, tf):
    x = x_ref[...]
    u = _modulate(x, mod_ref, 3, 4).astype(BF16)
    acc = None
    for j in range(w1_ref.shape[1] // tf):
        sl = slice(j * tf, (j + 1) * tf)
        h1 = _dot(u, w1_ref[:, sl])
        h3 = _dot(u, w3_ref[:, sl])
        part = _dot((h1 * jax.nn.sigmoid(h1) * h3).astype(BF16), w2_ref[sl, :])
        acc = part if acc is None else acc + part
    o_ref[...] = _gated_residual_ln(x, acc, mod_ref, 5, alpha, g_ref[...], b_ref[...])


def _ffn(x, modx, w1, w3, w2, layer, g, b, alpha, tm, tf):
    m, d = x.shape
    resident = lambda a: pl.BlockSpec((None,) + a.shape[1:], lambda i: (layer, 0, 0), pipeline_mode=pl.Buffered(1))
    return pl.pallas_call(
        functools.partial(_ffn_kernel, alpha=alpha, tf=tf),
        grid=(m // tm,),
        in_specs=[pl.BlockSpec((tm, d), lambda i: (i, 0)),
                  pl.BlockSpec((tm // SUB, 8, d), lambda i: (i, 0, 0)),
                  resident(w1), resident(w3), resident(w2),
                  pl.BlockSpec((1, d), lambda i: (0, 0)),
                  pl.BlockSpec((1, d), lambda i: (0, 0))],
        out_specs=pl.BlockSpec((tm, d), lambda i: (i, 0)),
        out_shape=jax.ShapeDtypeStruct((m, d), F32),
        compiler_params=_cp(("parallel",)),
        name="dense_swiglu",
    )(x, modx, w1, w3, w2, g.reshape(1, d), b.reshape(1, d))


def _route_kernel(x_ref, mod_ref, wr_ref, u_ref, lg_ref):
    u = _modulate(x_ref[...], mod_ref, 3, 4)
    u_ref[...] = u
    lg_ref[...] = jnp.dot(u, wr_ref[...], preferred_element_type=F32, precision=lax.Precision.HIGHEST)


def _route(x, modx, wr_pad, tm):
    m, d = x.shape
    return pl.pallas_call(
        _route_kernel,
        grid=(m // tm,),
        in_specs=[pl.BlockSpec((tm, d), lambda i: (i, 0)),
                  pl.BlockSpec((tm // SUB, 8, d), lambda i: (i, 0, 0)),
                  pl.BlockSpec((d, LANE), lambda i: (0, 0))],
        out_specs=[pl.BlockSpec((tm, d), lambda i: (i, 0)), pl.BlockSpec((tm, LANE), lambda i: (i, 0))],
        out_shape=[jax.ShapeDtypeStruct((m, d), F32), jax.ShapeDtypeStruct((m, LANE), F32)],
        compiler_params=_cp(("parallel",)),
        name="moe_router",
    )(x, modx, wr_pad)


def _moe_kernel(be_ref, nu_ref, xb_ref, w1_ref, w3_ref, w2_ref, o_ref, acc_ref, u_ref):
    i, j = pl.program_id(0), pl.program_id(1)
    live = i < nu_ref[0]

    @pl.when(j == 0)
    def _():
        acc_ref[...] = jnp.zeros_like(acc_ref)
        u_ref[...] = xb_ref[...].astype(BF16)

    @pl.when(live)
    def _():
        u = u_ref[...]
        h1 = _dot(u, w1_ref[0])
        h3 = _dot(u, w3_ref[0])
        acc_ref[...] += _dot((h1 * jax.nn.sigmoid(h1) * h3).astype(BF16), w2_ref[0])

    @pl.when(j == pl.num_programs(1) - 1)
    def _():
        o_ref[...] = acc_ref[...]


def _moe_experts(xb, blk_e, n_used, w1, w3, w2, layer, tm, tf):
    r, d = xb.shape
    ff = w1.shape[3]
    nj = ff // tf

    def wmap(i, j, be, nu):
        return (layer, be[i], 0, jnp.where(i < nu[0], j, nj - 1))

    def w2map(i, j, be, nu):
        return (layer, be[i], jnp.where(i < nu[0], j, nj - 1), 0)

    grid_spec = pltpu.PrefetchScalarGridSpec(
        num_scalar_prefetch=2,
        grid=(r // tm, nj),
        in_specs=[pl.BlockSpec((tm, d), lambda i, j, be, nu: (i, 0)),
                  pl.BlockSpec((None, 1, d, tf), wmap),
                  pl.BlockSpec((None, 1, d, tf), wmap),
                  pl.BlockSpec((None, 1, tf, d), w2map)],
        out_specs=pl.BlockSpec((tm, d), lambda i, j, be, nu: (i, 0)),
        scratch_shapes=[pltpu.VMEM((tm, d), F32), pltpu.VMEM((tm, d), BF16)],
    )
    return pl.pallas_call(
        _moe_kernel,
        grid_spec=grid_spec,
        out_shape=jax.ShapeDtypeStruct((r, d), F32),
        compiler_params=_cp(("arbitrary", "arbitrary")),
        name="moe_experts",
    )(blk_e, n_used, xb, w1, w3, w2)


def _ln_res_kernel(x_ref, f0_ref, f1_ref, wt_ref, mod_ref, g_ref, b_ref, o_ref, *, alpha):
    f = f0_ref[...] * wt_ref[:, 0:1] + f1_ref[...] * wt_ref[:, 1:2]
    o_ref[...] = _gated_residual_ln(x_ref[...], f, mod_ref, 5, alpha, g_ref[...], b_ref[...])


def _ln_res(x, f01, wts, modx, g, b, alpha, tm):
    m, d = x.shape
    nt = m // tm
    return pl.pallas_call(
        functools.partial(_ln_res_kernel, alpha=alpha),
        grid=(nt,),
        in_specs=[pl.BlockSpec((tm, d), lambda i: (i, 0)),
                  pl.BlockSpec((tm, d), lambda i: (i, 0)),
                  pl.BlockSpec((tm, d), lambda i: (i + nt, 0)),
                  pl.BlockSpec((tm, TOP_K), lambda i: (i, 0)),
                  pl.BlockSpec((tm // SUB, 8, d), lambda i: (i, 0, 0)),
                  pl.BlockSpec((1, d), lambda i: (0, 0)), pl.BlockSpec((1, d), lambda i: (0, 0))],
        out_specs=pl.BlockSpec((tm, d), lambda i: (i, 0)),
        out_shape=jax.ShapeDtypeStruct((m, d), F32),
        compiler_params=_cp(("parallel",)),
        name="moe_residual_ln",
    )(x, f01, f01, wts, modx, g.reshape(1, d), b.reshape(1, d))


def _moe(x, modx, router, w1, w3, w2, layer, g, b, alpha, tm_rows, tm, tf):
    m, d = x.shape
    wr_pad = jnp.zeros((d, LANE), F32).at[:, :N_EXPERTS].set(router)
    u, logits = _route(x, modx, wr_pad, tm_rows)
    top_v, top_i = lax.top_k(logits[:, :N_EXPERTS], TOP_K)
    wts = jax.nn.softmax(top_v, axis=-1)
    e_flat = top_i.reshape(-1).astype(jnp.int32)
    n_pair = m * TOP_K
    onehot = (e_flat[:, None] == jnp.arange(N_EXPERTS, dtype=jnp.int32)[None, :]).astype(jnp.int32)
    csum = jnp.cumsum(onehot, axis=0)
    counts = csum[-1]
    rank = jnp.take_along_axis(csum, e_flat[:, None], axis=1)[:, 0] - 1
    padded = (counts + tm - 1) // tm * tm
    pad_end = jnp.cumsum(padded)
    pad_start = pad_end - padded
    dest = pad_start[e_flat] + rank
    n_blk = -(-n_pair // tm) + N_EXPERTS
    r = n_blk * tm
    tok = jnp.arange(n_pair, dtype=jnp.int32) // TOP_K
    n_used = (pad_end[-1] // tm).astype(jnp.int32)
    blk = jnp.minimum(jnp.arange(n_blk, dtype=jnp.int32), n_used - 1) * tm
    blk_e = jnp.minimum(jnp.searchsorted(pad_end, blk, side='right'), N_EXPERTS - 1).astype(jnp.int32)
    _, tok_sorted = lax.sort_key_val(dest, tok)
    shift = pad_start - (jnp.cumsum(counts) - counts)
    src = jnp.arange(r, dtype=jnp.int32) - jnp.repeat(shift[blk_e], tm).astype(jnp.int32)
    row_tok = jnp.take(tok_sorted, jnp.clip(src, 0, n_pair - 1), mode="clip")
    xb = jnp.take(u, row_tok, axis=0, mode="clip")
    yb = _moe_experts(xb, blk_e, n_used.reshape(1), w1, w3, w2, layer, tm, tf)
    f01 = jnp.take(yb, dest.reshape(m, TOP_K).T.reshape(-1), axis=0, mode="clip")
    return _ln_res(x, f01, wts, modx, g, b, alpha, tm_rows)


def _rope_tables(nb, lat_len, ctx_len):
    rows = lat_len // GRID_W
    row = jnp.repeat(jnp.arange(rows, dtype=F32), GRID_W)
    col = jnp.tile(jnp.arange(GRID_W, dtype=F32), rows)
    n_freq = HEAD_DIM // 4
    inv = ROPE_THETA ** (-jnp.arange(n_freq, dtype=F32) / n_freq)
    ar, ac = row[:, None] * inv, col[:, None] * inv
    cos = jnp.concatenate([jnp.cos(ar), jnp.cos(ar), jnp.cos(ac), jnp.cos(ac)], axis=1)
    sin = jnp.concatenate([-jnp.sin(ar), jnp.sin(ar), -jnp.sin(ac), jnp.sin(ac)], axis=1)
    cos = jnp.concatenate([cos, jnp.ones((ctx_len, HEAD_DIM), F32)], axis=0)
    sin = jnp.concatenate([sin, jnp.zeros((ctx_len, HEAD_DIM), F32)], axis=0)
    return jnp.tile(cos, (nb, 1)), jnp.tile(sin, (nb, 1))


def _row_tile(m, pref):
    tm = pref
    while m % tm:
        tm //= 2
    return tm


def kernel(x, c, ctx, c_ctx, w_mod, b_mod, w_in, conv_a_w, conv_a_b, ret_decay, hy_conv_w, hy_conv_b, hy_w1, hy_b1, hy_w2, hy_b2, hy_w3, hy_freq, hy_skip, q_norm, k_norm, w_gate, b_gate, w_branch, w_o, ln_g, ln_b, ffn_w1, ffn_w3, ffn_w2, router, moe_w1, moe_w3, moe_w2):
    nb, lat_len, d = x.shape
    ctx_len = ctx.shape[1]
    depth = w_mod.shape[0]
    t = lat_len + ctx_len
    m = nb * t
    alpha = float((2 * depth) ** 0.25)
    assert lat_len % SUB == 0 and ctx_len % SUB == 0 and nb % 2 == 0

    act = jnp.concatenate([jax.nn.silu(c), jax.nn.silu(c_ctx)[None, :],
                           jnp.zeros((8 - nb - 1, d), F32)], axis=0)
    mods = _mod_all(act, w_mod, b_mod).reshape(depth, 8, N_MOD, d)
    mods = jnp.pad(mods, ((0, 0), (0, 0), (0, 8 - N_MOD), (0, 0)))
    tiles_per_batch = t // SUB
    tile_pos = np.arange(m // SUB) % tiles_per_batch
    tile_group = np.where(tile_pos < lat_len // SUB, np.arange(m // SUB) // tiles_per_batch, nb)
    cos_t, sin_t = _rope_tables(nb, lat_len, ctx_len)

    cast = lambda a: a.astype(BF16)
    w_in_b, w_gate_b, w_branch_b, w_o_b = cast(w_in), cast(w_gate), cast(w_branch), cast(w_o)
    ffn_b = (cast(ffn_w1), cast(ffn_w3), cast(ffn_w2))
    moe_w = (cast(moe_w1), cast(moe_w3), cast(moe_w2))

    rows = jnp.concatenate([x, ctx], axis=1).reshape(m, d)
    tm = _row_tile(m, 512)
    for l in range(depth):
        modx = mods[l][tile_group]
        za, zvg, zh, rq, rk, aq, ak, av = _proj(rows, modx, w_in_b, l, cos_t, sin_t, q_norm[l], k_norm[l], tm)
        y_d = _attention(aq, ak, av, nb, lat_len)
        log_gamma = -jnp.exp(ret_decay[l].astype(F32))
        y_r = _retention(rq, rk, zvg, _retention_tables(log_gamma), nb, lat_len)
        y_a = _shortconv(za, conv_a_w[l], conv_a_b[l], nb, lat_len)
        y_h = _hyena(zh, hy_conv_w[l], hy_conv_b[l],
                     (hy_w1[l], hy_b1[l], hy_w2[l], hy_b2[l], hy_w3[l], hy_freq[l]), hy_skip[l], nb, lat_len)
        rows = _merge(rows, modx, (y_a, y_r, y_h, y_d), w_gate_b, b_gate[l], w_branch_b, w_o_b, l,
                      ln_g[l, 0], ln_b[l, 0], alpha, tm)
        i = l // 2
        if l % 2 == 0:
            rows = _ffn(rows, modx, *ffn_b, i, ln_g[l, 1], ln_b[l, 1], alpha, tm, 256)
        else:
            rows = _moe(rows, modx, router[i], *moe_w, i, ln_g[l, 1], ln_b[l, 1], alpha, tm, 512, 1792)
    return rows.reshape(nb, t, d)[:, :lat_len]
```

```python
import functools
import math

import numpy as np
import jax
import jax.numpy as jnp
from jax import lax
from jax.experimental import pallas as pl
from jax.experimental.pallas import tpu as pltpu

F32 = jnp.float32
BF16 = jnp.bfloat16

GRID_W = 64
BRANCH_W = 512
RET_HEADS = 4
RET_D = 128
RET_CHUNK = 128
HY_EMB = 33
HY_FAST_PCT = 0.3
HY_SLOW_PCT = 1.5
HY_TARGET = 1e-2
ATT_HEADS = 4
ATT_KV_HEADS = 2
HEAD_DIM = 128
ROPE_THETA = 10000.0
N_EXPERTS = 8
TOP_K = 2
N_MOD = 6
A_W = 3 * BRANCH_W
R_W = 4 * RET_HEADS * RET_D
H_W = 3 * BRANCH_W
D_W = (ATT_HEADS + 2 * ATT_KV_HEADS) * HEAD_DIM
IN_W = A_W + R_W + H_W + D_W
LN_EPS = 1e-6

SUB = 256
LANE = 128
FFT_N1 = 128
RET_UNROLL = 8
ATT_TQ = 1024
ATT_KC = 1024
DFT_PASSES = 1
DFT_FILTER_PASSES = 1
VMEM_LIMIT = 56 * 1024 * 1024


def _cp(sem, vmem=VMEM_LIMIT):
    return pltpu.CompilerParams(dimension_semantics=sem, vmem_limit_bytes=vmem)


def _dot(a, b):
    return jnp.dot(a, b, preferred_element_type=F32)


def _dot_nt(a, b):
    return lax.dot_general(a, b, (((1,), (1,)), ((), ())), preferred_element_type=F32)


def _dot_tn(a, b):
    return lax.dot_general(a, b, (((0,), (0,)), ((), ())), preferred_element_type=F32)


def _split_dot(f_cat, x, passes):
    xh = x.astype(BF16)
    if passes == 1:
        return _dot(f_cat, xh)
    m = f_cat.shape[0] // 2
    r = _dot(f_cat, xh)
    r = r[:m] + r[m:]
    if passes == 3:
        r = r + _dot(f_cat[:m], (x - xh.astype(F32)).astype(BF16))
    return r


def _dft_operand(f_cat, passes):
    return f_cat[..., :f_cat.shape[-2] // 2, :] if passes == 1 else f_cat


def _modulate(x, mod_ref, shift_row, scale_row):
    parts = []
    for s in range(x.shape[0] // SUB):
        sh = mod_ref[s, shift_row:shift_row + 1, :]
        sc = mod_ref[s, scale_row:scale_row + 1, :]
        parts.append(x[s * SUB:(s + 1) * SUB] * (1.0 + sc) + sh)
    return parts[0] if len(parts) == 1 else jnp.concatenate(parts, axis=0)


def _gated_residual_ln(x, f, mod_ref, gate_row, alpha, g, b):
    parts = []
    for s in range(x.shape[0] // SUB):
        gt = mod_ref[s, gate_row:gate_row + 1, :]
        parts.append(alpha * x[s * SUB:(s + 1) * SUB] + gt * f[s * SUB:(s + 1) * SUB])
    r = parts[0] if len(parts) == 1 else jnp.concatenate(parts, axis=0)
    mu = jnp.mean(r, axis=-1, keepdims=True)
    rc = r - mu
    var = jnp.mean(rc * rc, axis=-1, keepdims=True)
    return rc * lax.rsqrt(var + LN_EPS) * g + b


def _mod_kernel(a_ref, w_ref, b_ref, o_ref):
    o_ref[0] = _dot(a_ref[...].astype(BF16), w_ref[0].astype(BF16)) + b_ref[0]


def _mod_all(act, w_mod, b_mod):
    depth, d, n = w_mod.shape
    tn = 512
    return pl.pallas_call(
        _mod_kernel,
        grid=(depth, n // tn),
        in_specs=[pl.BlockSpec((8, d), lambda l, j: (0, 0)),
                  pl.BlockSpec((1, d, tn), lambda l, j: (l, 0, j)),
                  pl.BlockSpec((1, 1, tn), lambda l, j: (l, 0, j))],
        out_specs=pl.BlockSpec((1, 8, tn), lambda l, j: (l, 0, j)),
        out_shape=jax.ShapeDtypeStruct((depth, 8, n), F32),
        compiler_params=_cp(("parallel", "parallel")),
        name="mod_vectors",
    )(act, w_mod, b_mod.reshape(depth, 1, n))


PROJ_TN = 512
assert (A_W, R_W, H_W, D_W) == (3 * PROJ_TN, 4 * PROJ_TN, 3 * PROJ_TN, 2 * PROJ_TN)


def _proj_kernel(x_ref, mod_ref, w_ref, cos_ref, sin_ref, gq_ref, gk_ref,
                 za_ref, zvg_ref, zh_ref, rq_ref, rk_ref, aq_ref, ak_ref, av_ref):
    u = _modulate(x_ref[...], mod_ref, 0, 1).astype(BF16)
    tn = PROJ_TN
    cos, sin = cos_ref[...], sin_ref[...]
    lane = lax.broadcasted_iota(jnp.int32, cos.shape, 1)
    first_half = (lane % 64) < 32

    def tile(j):
        return _dot(u, w_ref[:, j * tn:(j + 1) * tn])

    def rope(x):
        partner = jnp.where(first_half, pltpu.roll(x, 96, 1), pltpu.roll(x, 32, 1))
        return x * cos + partner * sin

    def rms(x, g):
        return x * lax.rsqrt(jnp.mean(x * x, axis=-1, keepdims=True) + LN_EPS) * g

    head = lambda h: slice(h * LANE, (h + 1) * LANE)

    for j in range(3):
        za_ref[:, j * tn:(j + 1) * tn] = tile(j)
    r = tile(3)
    for h in range(RET_HEADS):
        rq_ref[:, head(h)] = rope(r[:, head(h)])
    r = tile(4)
    for h in range(RET_HEADS):
        rk_ref[:, head(h)] = rope(r[:, head(h)] * (RET_D ** -0.5))
    for j in range(2):
        zvg_ref[:, j * tn:(j + 1) * tn] = tile(5 + j)
    for j in range(3):
        zh_ref[:, j * tn:(j + 1) * tn] = tile(7 + j)
    q_scale = (HEAD_DIM ** -0.5) * math.log2(math.e)
    r = tile(10)
    for h in range(ATT_HEADS):
        aq_ref[:, head(h)] = (rope(rms(r[:, head(h)], gq_ref[...])) * q_scale).astype(BF16)
    r = tile(11)
    ones_col = jnp.where(lane == 0, 1.0, 0.0).astype(BF16)
    for h in range(ATT_KV_HEADS):
        ak_ref[:, head(h)] = rope(rms(r[:, head(h)], gk_ref[...])).astype(BF16)
        av_ref[:, head(2 * h)] = r[:, head(ATT_KV_HEADS + h)].astype(BF16)
        av_ref[:, head(2 * h + 1)] = ones_col


def _proj(x, modx, w, layer, cos_t, sin_t, gq, gk, tm):
    m, d = x.shape
    n = w.shape[2]
    tn = PROJ_TN
    qw, kw = ATT_HEADS * HEAD_DIM, ATT_KV_HEADS * HEAD_DIM
    row = lambda i: (i, 0)
    widths = (A_W, 2 * tn, H_W, tn, tn, qw, kw, 2 * kw)
    dtypes = (F32,) * 5 + (BF16,) * 3
    return pl.pallas_call(
        _proj_kernel,
        grid=(m // tm,),
        in_specs=[pl.BlockSpec((tm, d), row),
                  pl.BlockSpec((tm // SUB, 8, d), lambda i: (i, 0, 0)),
                  pl.BlockSpec((None, d, n), lambda i: (layer, 0, 0), pipeline_mode=pl.Buffered(1)),
                  pl.BlockSpec((tm, LANE), row), pl.BlockSpec((tm, LANE), row),
                  pl.BlockSpec((1, LANE), lambda i: (0, 0)), pl.BlockSpec((1, LANE), lambda i: (0, 0))],
        out_specs=[pl.BlockSpec((tm, wd), row) for wd in widths],
        out_shape=[jax.ShapeDtypeStruct((m, wd), dt) for wd, dt in zip(widths, dtypes)],
        compiler_params=_cp(("parallel",)),
        name="in_proj",
    )(x, modx, w, cos_t, sin_t, gq.reshape(1, LANE), gk.reshape(1, LANE))


def _attn_kernel(q_ref, k_ref, v_ref, *rest, key_lo, key_hi, kc):
    o_ref = rest[-1]
    tq = q_ref.shape[1]
    grp = ATT_HEADS // ATT_KV_HEADS
    q2 = jnp.concatenate([q_ref[0, :, g * HEAD_DIM:(g + 1) * HEAD_DIM] for g in range(grp)], axis=0)
    m_run, acc = None, None
    for lo in range(key_lo, key_hi, kc):
        hi = min(lo + kc, key_hi)
        s = _dot_nt(q2, k_ref[0, lo:hi, :])
        m_chunk = jnp.max(s, axis=-1, keepdims=True)
        m_new = m_chunk if m_run is None else jnp.maximum(m_run, m_chunk)
        pv = _dot(jnp.exp2(s - m_new).astype(BF16), v_ref[0, lo:hi, :])
        acc = pv if acc is None else jnp.exp2(m_run - m_new) * acc + pv
        m_run = m_new
    for g in range(grp):
        a = acc[g * tq:(g + 1) * tq]
        o_ref[0, :, g * HEAD_DIM:(g + 1) * HEAD_DIM] = a[:, :HEAD_DIM] / a[:, HEAD_DIM:HEAD_DIM + 1]


def _attention(aq, ak, av, nb, lat_len):
    m = aq.shape[0]
    t = m // nb
    lc = t - lat_len
    grp_w = (ATT_HEADS // ATT_KV_HEADS) * HEAD_DIM
    args = (aq.reshape(nb, t, -1), ak.reshape(nb, t, -1), av.reshape(nb, t, -1))

    def call(tq, n_tiles, first_tile, key_lo, key_hi, prev):
        qmap = lambda b, kv, i: (b, first_tile + i, kv)
        specs = [pl.BlockSpec((1, tq, grp_w), qmap),
                 pl.BlockSpec((1, t, HEAD_DIM), lambda b, kv, i: (b, 0, kv)),
                 pl.BlockSpec((1, t, 2 * HEAD_DIM), lambda b, kv, i: (b, 0, kv))]
        extra, alias = (), {}
        if prev is not None:
            specs.append(pl.BlockSpec(memory_space=pl.ANY))
            extra, alias = (prev,), {3: 0}
        return pl.pallas_call(
            functools.partial(_attn_kernel, key_lo=key_lo, key_hi=key_hi, kc=ATT_KC),
            grid=(nb, ATT_KV_HEADS, n_tiles),
            in_specs=specs,
            out_specs=pl.BlockSpec((1, tq, grp_w), qmap),
            out_shape=jax.ShapeDtypeStruct((nb, t, ATT_HEADS * HEAD_DIM), F32),
            input_output_aliases=alias,
            compiler_params=_cp(("parallel", "parallel", "arbitrary")),
            name="gqa_attention",
        )(*args, *extra)

    tq = math.gcd(ATT_TQ, lat_len)
    out = call(tq, lat_len // tq, 0, 0, t, None)
    out = call(lc, 1, t // lc - 1, lat_len, t, out)
    return out.reshape(m, -1)


def _ret_kernel(q_ref, k_ref, v_ref, g_ref, intra_ref, qd_ref, kd_ref, cd_ref, o_ref, st_ref,
                *, n_lat, n_ctx):
    c_len = RET_CHUNK
    n = n_lat + n_ctx

    def rows(c):
        return pl.ds(pl.multiple_of(c * c_len, c_len), c_len)

    def kv(d, c):
        return _dot_tn((k_ref[0, rows(c), :] * kd_ref[d, 0]).astype(BF16), v_ref[0, rows(c), :].astype(BF16))

    def scan_step(t, carry):
        sf, sb = carry
        cf = jnp.where(t < n_ctx, n_lat + t, t - n_ctx)
        cb = jnp.where(t < n_ctx, n - 1 - t, n_lat - 1 - (t - n_ctx))
        st_ref[cf, :RET_D, :] = sf.astype(BF16)
        st_ref[cb, RET_D:, :] = sb.astype(BF16)
        return (cd_ref[0, 0, 0:1, :] * sf + kv(0, cf), cd_ref[1, 0, 0:1, :] * sb + kv(1, cb))

    zero = jnp.zeros((RET_D, RET_D), F32)
    lax.fori_loop(0, n, scan_step, (zero, zero), unroll=RET_UNROLL)

    mask = intra_ref[0, 0] + intra_ref[1, 0]

    def out_step(c, carry):
        qc = q_ref[0, rows(c), :]
        vc = v_ref[0, rows(c), :].astype(BF16)
        s = _dot_nt(qc.astype(BF16), k_ref[0, rows(c), :].astype(BF16)) * mask
        qdec = jnp.concatenate([qc * qd_ref[0, 0], qc * qd_ref[1, 0]], axis=1).astype(BF16)
        o = _dot(s.astype(BF16), vc) + _dot(qdec, st_ref[c])
        mu = jnp.mean(o, axis=-1, keepdims=True)
        oc = o - mu
        var = jnp.mean(oc * oc, axis=-1, keepdims=True)
        g = g_ref[0, rows(c), :]
        o_ref[0, rows(c), :] = (g * jax.nn.sigmoid(g)) * (oc * lax.rsqrt(var + LN_EPS))
        return carry

    lax.fori_loop(0, n, out_step, 0, unroll=RET_UNROLL)


def _retention(rq, rk, z, tabs, nb, lat_len):
    m = rq.shape[0]
    t = m // nb
    intra, qd, kd, cd = tabs
    v_blk, g_blk = 0, RET_HEADS
    kern = functools.partial(_ret_kernel, n_lat=lat_len // RET_CHUNK, n_ctx=(t - lat_len) // RET_CHUNK)
    seq = lambda off: pl.BlockSpec((1, t, LANE), lambda b, h: (b, 0, off + h))
    tab = lambda r: pl.BlockSpec((2, 1, r, LANE), lambda b, h: (0, h, 0, 0))
    out = pl.pallas_call(
        kern,
        grid=(nb, RET_HEADS),
        in_specs=[seq(0), seq(0), seq(v_blk), seq(g_blk),
                  tab(RET_CHUNK), tab(RET_CHUNK), tab(RET_CHUNK), tab(8)],
        out_specs=seq(0),
        out_shape=jax.ShapeDtypeStruct((nb, t, RET_HEADS * RET_D), F32),
        scratch_shapes=[pltpu.VMEM((t // RET_CHUNK, 2 * RET_D, RET_D), BF16)],
        compiler_params=_cp(("parallel", "parallel")),
        name="retention",
    )(rq.reshape(nb, t, -1), rk.reshape(nb, t, -1), z.reshape(nb, t, -1), z.reshape(nb, t, -1), intra, qd, kd, cd)
    return out.reshape(m, -1)


def _retention_tables(log_gamma):
    c = RET_CHUNK
    i = jnp.arange(c, dtype=F32)
    lg = log_gamma.astype(F32)[:, :, None]
    rel = i[:, None] - i[None, :]
    intra_f = jnp.where(rel >= 0, jnp.exp(lg[0][..., None] * jnp.maximum(rel, 0.0)), 0.0)
    intra_b = jnp.where(rel <= 0, jnp.exp(lg[1][..., None] * jnp.maximum(-rel, 0.0)), 0.0)
    qd = jnp.stack([jnp.exp(lg[0] * (i + 1.0)), jnp.exp(lg[1] * (c - i))])
    kd = jnp.stack([jnp.exp(lg[0] * (c - 1.0 - i)), jnp.exp(lg[1] * i)])
    cd = jnp.exp(lg * c)
    bc = lambda a: jnp.broadcast_to(a[..., None], a.shape + (LANE,))
    return (jnp.stack([intra_f, intra_b]), bc(qd), bc(kd),
            jnp.broadcast_to(cd[..., None], (2, RET_HEADS, 8, LANE)))


def _shift_rows(x, lat_len):
    t = x.shape[0]
    row = lax.broadcasted_iota(jnp.int32, x.shape, 0)
    xm = jnp.where((row == 0) | (row == lat_len), 0.0, pltpu.roll(x, 1, 0))
    xp = jnp.where((row == lat_len - 1) | (row == t - 1), 0.0, pltpu.roll(x, t - 1, 0))
    return xm, xp


def _shortconv_kernel(bg_ref, cg_ref, xv_ref, w_ref, b_ref, o_ref, *, lat_len):
    p = cg_ref[0] * xv_ref[0]
    pm, pp = _shift_rows(p, lat_len)
    o_ref[0] = bg_ref[0] * (pm * w_ref[0:1, :] + p * w_ref[1:2, :] + pp * w_ref[2:3, :] + b_ref[...])


def _shortconv(z, w, b, nb, lat_len):
    m = z.shape[0]
    t = m // nb
    nct = BRANCH_W // LANE
    kern = functools.partial(_shortconv_kernel, lat_len=lat_len)
    seq = lambda off: pl.BlockSpec((1, t, LANE), lambda bb, j: (bb, 0, off + j))
    out = pl.pallas_call(
        kern,
        grid=(nb, nct),
        in_specs=[seq(0), seq(nct), seq(2 * nct),
                  pl.BlockSpec((3, LANE), lambda bb, j: (0, j)),
                  pl.BlockSpec((1, LANE), lambda bb, j: (0, j))],
        out_specs=seq(0),
        out_shape=jax.ShapeDtypeStruct((nb, t, BRANCH_W), F32),
        compiler_params=_cp(("parallel", "parallel")),
        name="short_conv",
    )(z.reshape(nb, t, -1), z.reshape(nb, t, -1), z.reshape(nb, t, -1), w, b.reshape(1, -1))
    return out.reshape(m, -1)


def _hyconv_kernel(z_ref, w_ref, b_ref, ol_ref, oc_ref, *, lat_len):
    x = z_ref[0]
    xm, xp = _shift_rows(x, lat_len)
    y = xm * w_ref[0:1, :] + x * w_ref[1:2, :] + xp * w_ref[2:3, :] + b_ref[...]
    ol_ref[0, 0] = y[:lat_len]
    oc_ref[0, 0] = y[lat_len:]


def _hyconv(z, w, b, nb, lat_len):
    m = z.shape[0]
    t = m // nb
    nct = BRANCH_W // LANE
    off = 0
    kern = functools.partial(_hyconv_kernel, lat_len=lat_len)
    return pl.pallas_call(
        kern,
        grid=(nb, 3 * nct),
        in_specs=[pl.BlockSpec((1, t, LANE), lambda bb, j: (bb, 0, off + j)),
                  pl.BlockSpec((3, LANE), lambda bb, j: (0, j)),
                  pl.BlockSpec((1, LANE), lambda bb, j: (0, j))],
        out_specs=[pl.BlockSpec((1, 1, lat_len, LANE), lambda bb, j: (j // nct, bb, 0, j % nct)),
                   pl.BlockSpec((1, 1, t - lat_len, LANE), lambda bb, j: (j // nct, bb, 0, j % nct))],
        out_shape=[jax.ShapeDtypeStruct((3, nb, lat_len, BRANCH_W), F32),
                   jax.ShapeDtypeStruct((3, nb, t - lat_len, BRANCH_W), F32)],
        compiler_params=_cp(("parallel", "parallel")),
        name="hyena_dwconv",
    )(z.reshape(nb, t, -1), w, b.reshape(1, -1))


def _stack_c(m):
    return np.block([[m.real, -m.imag], [m.imag, m.real]])


def _split_np(m):
    m32 = np.asarray(m, np.float32)
    hi = m32.astype(BF16)
    lo = (m32 - hi.astype(np.float32)).astype(BF16)
    return np.concatenate([hi, lo], axis=-2)


@functools.lru_cache(maxsize=None)
def _dft_tables(lat_len):
    n = 2 * lat_len
    n1, n2 = FFT_N1, 2 * lat_len // FFT_N1
    a1 = np.arange(n1)
    a2 = np.arange(n2)
    w1 = np.exp(-2j * np.pi * np.outer(a1, a1[:n1 // 2]) / n1)
    f1 = _stack_c(w1)
    w2 = np.exp(-2j * np.pi * np.outer(a2, a2) / n2)
    tw = np.exp(-2j * np.pi * np.outer(a1, a2) / n)
    cf = w2[None, :, :] * tw[:, None, :]
    f2 = np.stack([_stack_c(cf[k]) for k in range(n1)])
    g2 = np.stack([_stack_c(cf[k].conj().T) for k in range(n1)])
    w1i = np.exp(2j * np.pi * np.outer(a1[:n1 // 2], a1) / n1) / n
    f3 = _stack_c(w1i)
    f1_real = f1[:, :n1 // 2]
    return _split_np(f1), _split_np(f1_real), _split_np(f2), _split_np(g2), _split_np(f3)


@functools.lru_cache(maxsize=None)
def _dft_tables_small(ctx_len):
    n = 2 * ctx_len
    a = np.arange(n)
    ff = _stack_c(np.exp(-2j * np.pi * np.outer(a, a[:ctx_len]) / n))
    fi = _stack_c(np.exp(2j * np.pi * np.outer(a[:ctx_len], a) / n) / n)
    return _split_np(ff), _split_np(ff[:, :ctx_len]), _split_np(fi)


def _fft1_kernel(x_ref, f_ref, o_ref, *, passes):
    f = f_ref[...]
    for j in range(x_ref.shape[3]):
        parts = [x_ref[0, b, :, j, :] for b in range(x_ref.shape[1])]
        r = _split_dot(f, parts[0] if len(parts) == 1 else jnp.concatenate(parts, axis=0), passes)
        h = r.shape[0] // 2
        o_ref[0, 0, :, j, :] = r[:h]
        o_ref[0, 1, :, j, :] = r[h:]


def _fft1(x, sel, f1, jb, passes):
    _, p, mem, r, n2, w = x.shape
    f1 = _dft_operand(f1, passes)
    n1 = f1.shape[0] // (2 if passes == 1 else 4)
    return pl.pallas_call(
        functools.partial(_fft1_kernel, passes=passes),
        grid=(p, n2 // jb),
        in_specs=[pl.BlockSpec((None, 1, mem, r, jb, w), lambda pp, j: (sel, pp, 0, 0, j, 0)),
                  pl.BlockSpec(f1.shape, lambda pp, j: (0, 0))],
        out_specs=pl.BlockSpec((1, 2, n1, jb, w), lambda pp, j: (pp, 0, 0, j, 0)),
        out_shape=jax.ShapeDtypeStruct((p, 2, n1, n2, w), F32),
        compiler_params=_cp(("parallel", "parallel")),
        name="hyena_fft_stage1",
    )(x, f1)


def _fft2f_kernel(a_ref, f_ref, o_ref, *, kb, n2, passes):
    w = a_ref.shape[-1]
    for t in range(kb):
        xf = _split_dot(f_ref[t], a_ref[0, :, t].reshape(2 * n2, w), passes)
        xb = _split_dot(f_ref[t], a_ref[1, :, t].reshape(2 * n2, w), passes)
        o_ref[0, 0, t] = xf[:n2] + xb[:n2]
        o_ref[0, 1, t] = xf[n2:] - xb[n2:]


def _fft2_filters(a, f2, kb, passes):
    q, _, n1, n2, w = a.shape
    f2 = _dft_operand(f2, passes)
    return pl.pallas_call(
        functools.partial(_fft2f_kernel, kb=kb, n2=n2, passes=passes),
        grid=(q // 2, n1 // kb),
        in_specs=[pl.BlockSpec((2, 2, kb, n2, w), lambda o, k: (o, 0, k, 0, 0)),
                  pl.BlockSpec((kb,) + f2.shape[1:], lambda o, k: (k, 0, 0))],
        out_specs=pl.BlockSpec((1, 2, kb, n2, w), lambda o, k: (o, 0, k, 0, 0)),
        out_shape=jax.ShapeDtypeStruct((q // 2, 2, n1, n2, w), F32),
        compiler_params=_cp(("parallel", "parallel")),
        name="hyena_filter_spectrum",
    )(a, f2)


def _fft2_kernel(a_ref, h_ref, f_ref, g_ref, o_ref, *, kb, n2, passes):
    w = a_ref.shape[-1]
    for t in range(kb):
        a = a_ref[0, :, t].reshape(2 * n2, w)
        x = _split_dot(f_ref[t], a, passes)
        xr, xi = x[:n2], x[n2:]
        hr, hi = h_ref[0, t], h_ref[1, t]
        y = jnp.concatenate([xr * hr - xi * hi, xr * hi + xi * hr], axis=0)
        o_ref[0, :, t] = _split_dot(g_ref[t], y, passes).reshape(2, n2, w)


def _fft2(a, hspec, order, f2, g2, kb, passes):
    p, _, n1, n2, w = a.shape
    f2, g2 = _dft_operand(f2, passes), _dft_operand(g2, passes)
    kern = functools.partial(_fft2_kernel, kb=kb, n2=n2, passes=passes)
    return pl.pallas_call(
        kern,
        grid=(p, n1 // kb),
        in_specs=[pl.BlockSpec((1, 2, kb, n2, w), lambda pp, k: (pp, 0, k, 0, 0)),
                  pl.BlockSpec((None, 2, kb, n2, w), lambda pp, k: (order, 0, k, 0, 0)),
                  pl.BlockSpec((kb,) + f2.shape[1:], lambda pp, k: (k, 0, 0)),
                  pl.BlockSpec((kb,) + g2.shape[1:], lambda pp, k: (k, 0, 0))],
        out_specs=pl.BlockSpec((1, 2, kb, n2, w), lambda pp, k: (pp, 0, k, 0, 0)),
        out_shape=jax.ShapeDtypeStruct(a.shape, F32),
        compiler_params=_cp(("parallel", "parallel")),
        name="hyena_fft_stage2",
    )(a, hspec, f2, g2)


def _fft3_kernel(b_ref, f_ref, gate_ref, prev_ref, skip_ref, o_ref, *, passes):
    f = f_ref[...]
    r = gate_ref.shape[2]
    for j in range(b_ref.shape[3]):
        bj = jnp.concatenate([b_ref[0, 0, :, j, :], b_ref[0, 1, :, j, :]], axis=0)
        conv = _split_dot(f, bj, passes)
        for mem in range(2):
            cm = conv[mem * r:(mem + 1) * r]
            o_ref[0, mem, :, j, :] = gate_ref[0, mem, :, j, :] * (cm + skip_ref[...] * prev_ref[0, mem, :, j, :])


def _fft3(bm, f3, gate, gate_sel, prev, prev_sel, skip, out_rows, jb, passes):
    p, _, n1, n2, w = bm.shape
    r = n1 // 2
    f3 = _dft_operand(f3, passes)
    seq = lambda s: pl.BlockSpec((None, 1, 2, r, jb, w), lambda pp, j: (s, pp, 0, 0, j, 0))
    return pl.pallas_call(
        functools.partial(_fft3_kernel, passes=passes),
        grid=(p, n2 // jb),
        in_specs=[pl.BlockSpec((1, 2, n1, jb, w), lambda pp, j: (pp, 0, 0, j, 0)),
                  pl.BlockSpec(f3.shape, lambda pp, j: (0, 0)),
                  seq(gate_sel), seq(prev_sel),
                  pl.BlockSpec((1, w), lambda pp, j: (0, 0))],
        out_specs=pl.BlockSpec((1, 2, r, jb, w), lambda pp, j: (pp, 0, 0, j, 0)),
        out_shape=jax.ShapeDtypeStruct((p, 2, out_rows, n2, w), F32),
        compiler_params=_cp(("parallel", "parallel")),
        name="hyena_fft_stage3",
    )(bm, f3, gate, prev, skip)


def _hyctx_kernel(v_ref, x1_ref, x2_ref, filt_ref, ff_ref, ffr_ref, fi_ref, skip_ref, yh_hbm_ref, o_ref,
                  *, passes, filter_passes):
    del yh_hbm_ref
    lc, w = v_ref.shape[2], v_ref.shape[3]
    n = 2 * lc
    y = v_ref[0].reshape(2 * lc, w)
    for o, gate_ref in enumerate((x1_ref, x2_ref)):
        sf = _split_dot(ffr_ref[...], filt_ref[o, 0], filter_passes)
        sb = _split_dot(ffr_ref[...], filt_ref[o, 1], filter_passes)
        hr, hi = sf[:n] + sb[:n], sf[n:] - sb[n:]
        x = _split_dot(ff_ref[...], y, passes)
        xr, xi = x[:n], x[n:]
        spec = jnp.concatenate([xr * hr - xi * hi, xr * hi + xi * hr], axis=0)
        conv = _split_dot(fi_ref[...], spec, passes)
        y = gate_ref[0].reshape(2 * lc, w) * (conv + skip_ref[o:o + 1, :] * y)
    o_ref[0] = y.reshape(2, lc, w)


def _hyena_ctx(hc, filt_c, ff, ffr, fi, skip, yh, passes, filter_passes):
    _, nb, lc, w = hc.shape
    p = nb // 2
    t = yh.shape[2]
    ff, fi, ffr = _dft_operand(ff, passes), _dft_operand(fi, passes), _dft_operand(ffr, filter_passes)
    part = pl.BlockSpec((1, 2, lc, w), lambda pp: (pp, 0, 0, 0))
    hc4 = hc.reshape(3, p, 2, lc, w)
    return pl.pallas_call(
        functools.partial(_hyctx_kernel, passes=passes, filter_passes=filter_passes),
        grid=(p,),
        in_specs=[part, part, part,
                  pl.BlockSpec(filt_c.shape, lambda pp: (0, 0, 0, 0)),
                  pl.BlockSpec(ff.shape, lambda pp: (0, 0)),
                  pl.BlockSpec(ffr.shape, lambda pp: (0, 0)),
                  pl.BlockSpec(fi.shape, lambda pp: (0, 0)),
                  pl.BlockSpec(skip.shape, lambda pp: (0, 0)),
                  pl.BlockSpec(memory_space=pl.ANY)],
        out_specs=pl.BlockSpec((1, 2, lc, w), lambda pp: (pp, 0, t // lc - 1, 0)),
        out_shape=jax.ShapeDtypeStruct(yh.shape, F32),
        input_output_aliases={8: 0},
        compiler_params=_cp(("parallel",)),
        name="hyena_ctx",
    )(hc4[0], hc4[1], hc4[2], filt_c, ff, ffr, fi, skip, yh)


def _hyena_filters(l, w1, b1, w2, b2, w3, freq):
    t = jnp.linspace(0.0, 1.0, l, dtype=F32)[:, None]
    bands = (HY_EMB - 1) // 2
    w = 2.0 * math.pi * jnp.arange(l, dtype=F32) / l
    f = jnp.linspace(1e-4, bands - 1, bands, dtype=F32)
    ang = w[:, None] * f[None, :]
    feats = jnp.concatenate([t, jnp.cos(ang), -jnp.sin(ang)], axis=-1)
    hdn = jnp.sin(freq[0] * (feats @ w1 + b1))
    hdn = jnp.sin(freq[1] * (hdn @ w2 + b2))
    n_out = w3.shape[-1]
    deltas = jnp.linspace(math.log(HY_TARGET) / HY_SLOW_PCT, math.log(HY_TARGET) / HY_FAST_PCT, n_out, dtype=F32)
    decay = jnp.exp(-t * jnp.abs(deltas))
    pos = jnp.arange(l)[:, None]
    parts = []
    for q in range(n_out // BRANCH_W):
        sl = slice(q * BRANCH_W, (q + 1) * BRANCH_W)
        f = (hdn @ w3[:, sl]).astype(F32) * decay[:, sl]
        parts.append(jnp.where(pos == 0, 0.0, f) if q % 2 else f)
    return jnp.stack(parts).reshape(n_out // (2 * BRANCH_W), 2, l, BRANCH_W)


def _hyena(z, conv_w, conv_b, filt_w, skip, nb, lat_len):
    m = z.shape[0]
    t = m // nb
    lc = t - lat_len
    w = BRANCH_W
    p = nb // 2
    n1, n2 = FFT_N1, 2 * lat_len // FFT_N1
    r = n1 // 2
    hl, hc = _hyconv(z, conv_w, conv_b, nb, lat_len)
    f1, f1r, f2, g2, f3 = _dft_tables(lat_len)
    ff, ffr, fi = _dft_tables_small(lc)
    jb, kb = min(16, n2), 16
    dp, fp = DFT_PASSES, DFT_FILTER_PASSES
    filt_l = _hyena_filters(lat_len, *filt_w)
    fa = _fft1(filt_l.reshape(1, 4, 1, r, n2, w), 0, f1r, jb, fp)
    hs_l = _fft2_filters(fa, f2, kb, fp)
    hl6 = hl.reshape(3, p, 2, r, n2, w)
    y, y_sel = hl6, 0
    for o in range(2):
        a = _fft1(y, y_sel, f1, jb, dp)
        bm = _fft2(a, hs_l, o, f2, g2, kb, dp)
        rows = r if o == 0 else t // n2
        y = _fft3(bm, f3, hl6, 1 + o, y, y_sel, skip[o:o + 1], rows, jb, dp)[None]
        y_sel = 0
    yh = y.reshape(p, 2, t, w)
    yh = _hyena_ctx(hc, _hyena_filters(lc, *filt_w), ff, ffr, fi, skip, yh, dp, fp)
    return yh.reshape(m, w)


def _merge_kernel(x_ref, mod_ref, ya_ref, yr_ref, yh_ref, yd_ref, wg_ref, bg_ref, wb_ref, wo_ref, g_ref, b_ref,
                  o_ref, *, alpha):
    x = x_ref[...]
    u = _modulate(x, mod_ref, 0, 1).astype(BF16)
    acc = None
    for i, y_ref in enumerate((ya_ref, yr_ref, yh_ref, yd_ref)):
        gate = jax.nn.sigmoid(_dot(u, wg_ref[i]) + bg_ref[i:i + 1, :])
        term = gate * _dot(y_ref[...].astype(BF16), wb_ref[i])
        acc = term if acc is None else acc + term
    mix = _dot(acc.astype(BF16), wo_ref[...])
    o_ref[...] = _gated_residual_ln(x, mix, mod_ref, 2, alpha, g_ref[...], b_ref[...])


def _merge(x, modx, ys, wg, bg, wb, wo, layer, g, b, alpha, tm):
    m, d = x.shape
    w = BRANCH_W
    row = lambda i: (i, 0)
    const2 = lambda i: (0, 0)
    one = pl.Buffered(1)
    stacked = lambda a: pl.BlockSpec((None,) + a.shape[1:], lambda i: (layer,) + (0,) * (a.ndim - 1),
                                     pipeline_mode=one)
    return pl.pallas_call(
        functools.partial(_merge_kernel, alpha=alpha),
        grid=(m // tm,),
        in_specs=[pl.BlockSpec((tm, d), row),
                  pl.BlockSpec((tm // SUB, 8, d), lambda i: (i, 0, 0)),
                  pl.BlockSpec((tm, w), row), pl.BlockSpec((tm, w), row),
                  pl.BlockSpec((tm, w), row), pl.BlockSpec((tm, w), row),
                  stacked(wg), pl.BlockSpec(bg.shape, const2), stacked(wb), stacked(wo),
                  pl.BlockSpec((1, d), const2), pl.BlockSpec((1, d), const2)],
        out_specs=pl.BlockSpec((tm, d), row),
        out_shape=jax.ShapeDtypeStruct((m, d), F32),
        compiler_params=_cp(("parallel",)),
        name="branch_merge",
    )(x, modx, *ys, wg, bg, wb, wo, g.reshape(1, d), b.reshape(1, d))


def _ffn_kernel(x_ref, mod_ref, w1_ref, w3_ref, w2_ref, g_ref, b_ref, o_ref, *, alpha, tf):
    x = x_ref[...]
    u = _modulate(x, mod_ref, 3, 4).astype(BF16)
    acc = None
    for j in range(w1_ref.shape[1] // tf):
        sl = slice(j * tf, (j + 1) * tf)
        h1 = _dot(u, w1_ref[:, sl])
        h3 = _dot(u, w3_ref[:, sl])
        part = _dot((h1 * jax.nn.sigmoid(h1) * h3).astype(BF16), w2_ref[sl, :])
        acc = part if acc is None else acc + part
    o_ref[...] = _gated_residual_ln(x, acc, mod_ref, 5, alpha, g_ref[...], b_ref[...])


def _ffn(x, modx, w1, w3, w2, layer, g, b, alpha, tm, tf):
    m, d = x.shape
    resident = lambda a: pl.BlockSpec((None,) + a.shape[1:], lambda i: (layer, 0, 0), pipeline_mode=pl.Buffered(1))
    return pl.pallas_call(
        functools.partial(_ffn_kernel, alpha=alpha, tf=tf),
        grid=(m // tm,),
        in_specs=[pl.BlockSpec((tm, d), lambda i: (i, 0)),
                  pl.BlockSpec((tm // SUB, 8, d), lambda i: (i, 0, 0)),
                  resident(w1), resident(w3), resident(w2),
                  pl.BlockSpec((1, d), lambda i: (0, 0)),
                  pl.BlockSpec((1, d), lambda i: (0, 0))],
        out_specs=pl.BlockSpec((tm, d), lambda i: (i, 0)),
        out_shape=jax.ShapeDtypeStruct((m, d), F32),
        compiler_params=_cp(("parallel",)),
        name="dense_swiglu",
    )(x, modx, w1, w3, w2, g.reshape(1, d), b.reshape(1, d))


def _route_kernel(x_ref, mod_ref, wr_ref, u_ref, lg_ref):
    u = _modulate(x_ref[...], mod_ref, 3, 4)
    u_ref[...] = u
    lg_ref[...] = jnp.dot(u, wr_ref[...], preferred_element_type=F32, precision=lax.Precision.HIGHEST)


def _route(x, modx, wr_pad, tm):
    m, d = x.shape
    return pl.pallas_call(
        _route_kernel,
        grid=(m // tm,),
        in_specs=[pl.BlockSpec((tm, d), lambda i: (i, 0)),
                  pl.BlockSpec((tm // SUB, 8, d), lambda i: (i, 0, 0)),
                  pl.BlockSpec((d, LANE), lambda i: (0, 0))],
        out_specs=[pl.BlockSpec((tm, d), lambda i: (i, 0)), pl.BlockSpec((tm, LANE), lambda i: (i, 0))],
        out_shape=[jax.ShapeDtypeStruct((m, d), F32), jax.ShapeDtypeStruct((m, LANE), F32)],
        compiler_params=_cp(("parallel",)),
        name="moe_router",
    )(x, modx, wr_pad)


def _moe_kernel(be_ref, nu_ref, xb_ref, w1_ref, w3_ref, w2_ref, o_ref, acc_ref, u_ref):
    i, j = pl.program_id(0), pl.program_id(1)
    live = i < nu_ref[0]

    @pl.when(j == 0)
    def _():
        acc_ref[...] = jnp.zeros_like(acc_ref)
        u_ref[...] = xb_ref[...].astype(BF16)

    @pl.when(live)
    def _():
        u = u_ref[...]
        h1 = _dot(u, w1_ref[0])
        h3 = _dot(u, w3_ref[0])
        acc_ref[...] += _dot((h1 * jax.nn.sigmoid(h1) * h3).astype(BF16), w2_ref[0])

    @pl.when(j == pl.num_programs(1) - 1)
    def _():
        o_ref[...] = acc_ref[...]


def _moe_experts(xb, blk_e, n_used, w1, w3, w2, layer, tm, tf):
    r, d = xb.shape
    ff = w1.shape[3]
    nj = ff // tf

    def wmap(i, j, be, nu):
        return (layer, be[i], 0, jnp.where(i < nu[0], j, nj - 1))

    def w2map(i, j, be, nu):
        return (layer, be[i], jnp.where(i < nu[0], j, nj - 1), 0)

    grid_spec = pltpu.PrefetchScalarGridSpec(
        num_scalar_prefetch=2,
        grid=(r // tm, nj),
        in_specs=[pl.BlockSpec((tm, d), lambda i, j, be, nu: (i, 0)),
                  pl.BlockSpec((None, 1, d, tf), wmap),
                  pl.BlockSpec((None, 1, d, tf), wmap),
                  pl.BlockSpec((None, 1, tf, d), w2map)],
        out_specs=pl.BlockSpec((tm, d), lambda i, j, be, nu: (i, 0)),
        scratch_shapes=[pltpu.VMEM((tm, d), F32), pltpu.VMEM((tm, d), BF16)],
    )
    return pl.pallas_call(
        _moe_kernel,
        grid_spec=grid_spec,
        out_shape=jax.ShapeDtypeStruct((r, d), F32),
        compiler_params=_cp(("arbitrary", "arbitrary")),
        name="moe_experts",
    )(blk_e, n_used, xb, w1, w3, w2)


def _ln_res_kernel(x_ref, f0_ref, f1_ref, wt_ref, mod_ref, g_ref, b_ref, o_ref, *, alpha):
    f = f0_ref[...] * wt_ref[:, 0:1] + f1_ref[...] * wt_ref[:, 1:2]
    o_ref[...] = _gated_residual_ln(x_ref[...], f, mod_ref, 5, alpha, g_ref[...], b_ref[...])


def _ln_res(x, f01, wts, modx, g, b, alpha, tm):
    m, d = x.shape
    nt = m // tm
    return pl.pallas_call(
        functools.partial(_ln_res_kernel, alpha=alpha),
        grid=(nt,),
        in_specs=[pl.BlockSpec((tm, d), lambda i: (i, 0)),
                  pl.BlockSpec((tm, d), lambda i: (i, 0)),
                  pl.BlockSpec((tm, d), lambda i: (i + nt, 0)),
                  pl.BlockSpec((tm, TOP_K), lambda i: (i, 0)),
                  pl.BlockSpec((tm // SUB, 8, d), lambda i: (i, 0, 0)),
                  pl.BlockSpec((1, d), lambda i: (0, 0)), pl.BlockSpec((1, d), lambda i: (0, 0))],
        out_specs=pl.BlockSpec((tm, d), lambda i: (i, 0)),
        out_shape=jax.ShapeDtypeStruct((m, d), F32),
        compiler_params=_cp(("parallel",)),
        name="moe_residual_ln",
    )(x, f01, f01, wts, modx, g.reshape(1, d), b.reshape(1, d))


def _moe(x, modx, router, w1, w3, w2, layer, g, b, alpha, tm_rows, tm, tf):
    m, d = x.shape
    wr_pad = jnp.zeros((d, LANE), F32).at[:, :N_EXPERTS].set(router)
    u, logits = _route(x, modx, wr_pad, tm_rows)
    top_v, top_i = lax.top_k(logits[:, :N_EXPERTS], TOP_K)
    wts = jax.nn.softmax(top_v, axis=-1)
    e_flat = top_i.reshape(-1).astype(jnp.int32)
    n_pair = m * TOP_K
    onehot = (e_flat[:, None] == jnp.arange(N_EXPERTS, dtype=jnp.int32)[None, :]).astype(jnp.int32)
    csum = jnp.cumsum(onehot, axis=0)
    counts = csum[-1]
    rank = jnp.take_along_axis(csum, e_flat[:, None], axis=1)[:, 0] - 1
    padded = (counts + tm - 1) // tm * tm
    pad_end = jnp.cumsum(padded)
    pad_start = pad_end - padded
    dest = pad_start[e_flat] + rank
    n_blk = -(-n_pair // tm) + N_EXPERTS
    r = n_blk * tm
    tok = jnp.arange(n_pair, dtype=jnp.int32) // TOP_K
    n_used = (pad_end[-1] // tm).astype(jnp.int32)
    blk = jnp.minimum(jnp.arange(n_blk, dtype=jnp.int32), n_used - 1) * tm
    blk_e = jnp.minimum(jnp.searchsorted(pad_end, blk, side='right'), N_EXPERTS - 1).astype(jnp.int32)
    _, tok_sorted = lax.sort_key_val(dest, tok)
    shift = pad_start - (jnp.cumsum(counts) - counts)
    src = jnp.arange(r, dtype=jnp.int32) - jnp.repeat(shift[blk_e], tm).astype(jnp.int32)
    row_tok = jnp.take(tok_sorted, jnp.clip(src, 0, n_pair - 1), mode="clip")
    xb = jnp.take(u, row_tok, axis=0, mode="clip")
    yb = _moe_experts(xb, blk_e, n_used.reshape(1), w1, w3, w2, layer, tm, tf)
    f01 = jnp.take(yb, dest.reshape(m, TOP_K).T.reshape(-1), axis=0, mode="clip")
    return _ln_res(x, f01, wts, modx, g, b, alpha, tm_rows)


def _rope_tables(nb, lat_len, ctx_len):
    rows = lat_len // GRID_W
    row = jnp.repeat(jnp.arange(rows, dtype=F32), GRID_W)
    col = jnp.tile(jnp.arange(GRID_W, dtype=F32), rows)
    n_freq = HEAD_DIM // 4
    inv = ROPE_THETA ** (-jnp.arange(n_freq, dtype=F32) / n_freq)
    ar, ac = row[:, None] * inv, col[:, None] * inv
    cos = jnp.concatenate([jnp.cos(ar), jnp.cos(ar), jnp.cos(ac), jnp.cos(ac)], axis=1)
    sin = jnp.concatenate([-jnp.sin(ar), jnp.sin(ar), -jnp.sin(ac), jnp.sin(ac)], axis=1)
    cos = jnp.concatenate([cos, jnp.ones((ctx_len, HEAD_DIM), F32)], axis=0)
    sin = jnp.concatenate([sin, jnp.zeros((ctx_len, HEAD_DIM), F32)], axis=0)
    return jnp.tile(cos, (nb, 1)), jnp.tile(sin, (nb, 1))


def _row_tile(m, pref):
    tm = pref
    while m % tm:
        tm //= 2
    return tm


def kernel(x, c, ctx, c_ctx, w_mod, b_mod, w_in, conv_a_w, conv_a_b, ret_decay, hy_conv_w, hy_conv_b, hy_w1, hy_b1, hy_w2, hy_b2, hy_w3, hy_freq, hy_skip, q_norm, k_norm, w_gate, b_gate, w_branch, w_o, ln_g, ln_b, ffn_w1, ffn_w3, ffn_w2, router, moe_w1, moe_w3, moe_w2):
    nb, lat_len, d = x.shape
    ctx_len = ctx.shape[1]
    depth = w_mod.shape[0]
    t = lat_len + ctx_len
    m = nb * t
    alpha = float((2 * depth) ** 0.25)
    assert lat_len % SUB == 0 and ctx_len % SUB == 0 and nb % 2 == 0

    act = jnp.concatenate([jax.nn.silu(c), jax.nn.silu(c_ctx)[None, :],
                           jnp.zeros((8 - nb - 1, d), F32)], axis=0)
    mods = _mod_all(act, w_mod, b_mod).reshape(depth, 8, N_MOD, d)
    mods = jnp.pad(mods, ((0, 0), (0, 0), (0, 8 - N_MOD), (0, 0)))
    tiles_per_batch = t // SUB
    tile_pos = np.arange(m // SUB) % tiles_per_batch
    tile_group = np.where(tile_pos < lat_len // SUB, np.arange(m // SUB) // tiles_per_batch, nb)
    cos_t, sin_t = _rope_tables(nb, lat_len, ctx_len)

    cast = lambda a: a.astype(BF16)
    w_in_b, w_gate_b, w_branch_b, w_o_b = cast(w_in), cast(w_gate), cast(w_branch), cast(w_o)
    ffn_b = (cast(ffn_w1), cast(ffn_w3), cast(ffn_w2))
    moe_w = (cast(moe_w1), cast(moe_w3), cast(moe_w2))

    rows = jnp.concatenate([x, ctx], axis=1).reshape(m, d)
    tm = _row_tile(m, 512)
    for l in range(depth):
        modx = mods[l][tile_group]
        za, zvg, zh, rq, rk, aq, ak, av = _proj(rows, modx, w_in_b, l, cos_t, sin_t, q_norm[l], k_norm[l], tm)
        y_d = _attention(aq, ak, av, nb, lat_len)
        log_gamma = -jnp.exp(ret_decay[l].astype(F32))
        y_r = _retention(rq, rk, zvg, _retention_tables(log_gamma), nb, lat_len)
        y_a = _shortconv(za, conv_a_w[l], conv_a_b[l], nb, lat_len)
        y_h = _hyena(zh, hy_conv_w[l], hy_conv_b[l],
                     (hy_w1[l], hy_b1[l], hy_w2[l], hy_b2[l], hy_w3[l], hy_freq[l]), hy_skip[l], nb, lat_len)
        rows = _merge(rows, modx, (y_a, y_r, y_h, y_d), w_gate_b, b_gate[l], w_branch_b, w_o_b, l,
                      ln_g[l, 0], ln_b[l, 0], alpha, tm)
        i = l // 2
        if l % 2 == 0:
            rows = _ffn(rows, modx, *ffn_b, i, ln_g[l, 1], ln_b[l, 1], alpha, tm, 256)
        else:
            rows = _moe(rows, modx, router[i], *moe_w, i, ln_g[l, 1], ln_b[l, 1], alpha, tm, 512, 1792)
    return rows.reshape(nb, t, d)[:, :lat_len]
```

```python
import functools
import math

import numpy as np
import jax
import jax.numpy as jnp
from jax import lax
from jax.experimental import pallas as pl
from jax.experimental.pallas import tpu as pltpu

F32 = jnp.float32
BF16 = jnp.bfloat16

GRID_W = 64
BRANCH_W = 512
RET_HEADS = 4
RET_D = 128
RET_CHUNK = 128
HY_EMB = 33
HY_FAST_PCT = 0.3
HY_SLOW_PCT = 1.5
HY_TARGET = 1e-2
ATT_HEADS = 4
ATT_KV_HEADS = 2
HEAD_DIM = 128
ROPE_THETA = 10000.0
N_EXPERTS = 8
TOP_K = 2
N_MOD = 6
A_W = 3 * BRANCH_W
R_W = 4 * RET_HEADS * RET_D
H_W = 3 * BRANCH_W
D_W = (ATT_HEADS + 2 * ATT_KV_HEADS) * HEAD_DIM
IN_W = A_W + R_W + H_W + D_W
LN_EPS = 1e-6

SUB = 256
LANE = 128
FFT_N1 = 128
RET_UNROLL = 8
ATT_TQ = 1024
ATT_KC = 1024
DFT_PASSES = 1
DFT_FILTER_PASSES = 1
VMEM_LIMIT = 56 * 1024 * 1024
CAST_BLOCK_BYTES = 8 * 1024 * 1024


def _cp(sem, vmem=VMEM_LIMIT):
    return pltpu.CompilerParams(dimension_semantics=sem, vmem_limit_bytes=vmem)


def _dot(a, b):
    return jnp.dot(a, b, preferred_element_type=F32)


def _dot_nt(a, b):
    return lax.dot_general(a, b, (((1,), (1,)), ((), ())), preferred_element_type=F32)


def _dot_tn(a, b):
    return lax.dot_general(a, b, (((0,), (0,)), ((), ())), preferred_element_type=F32)


def _split_dot(f_cat, x, passes):
    xh = x.astype(BF16)
    if passes == 1:
        return _dot(f_cat, xh)
    m = f_cat.shape[0] // 2
    r = _dot(f_cat, xh)
    r = r[:m] + r[m:]
    if passes == 3:
        r = r + _dot(f_cat[:m], (x - xh.astype(F32)).astype(BF16))
    return r


def _dft_operand(f_cat, passes):
    return f_cat[..., :f_cat.shape[-2] // 2, :] if passes == 1 else f_cat


def _modulate(x, mod_ref, shift_row, scale_row):
    parts = []
    for s in range(x.shape[0] // SUB):
        sh = mod_ref[s, shift_row:shift_row + 1, :]
        sc = mod_ref[s, scale_row:scale_row + 1, :]
        parts.append(x[s * SUB:(s + 1) * SUB] * (1.0 + sc) + sh)
    return parts[0] if len(parts) == 1 else jnp.concatenate(parts, axis=0)


def _gated_residual_ln(x, f, mod_ref, gate_row, alpha, g, b):
    parts = []
    for s in range(x.shape[0] // SUB):
        gt = mod_ref[s, gate_row:gate_row + 1, :]
        parts.append(alpha * x[s * SUB:(s + 1) * SUB] + gt * f[s * SUB:(s + 1) * SUB])
    r = parts[0] if len(parts) == 1 else jnp.concatenate(parts, axis=0)
    mu = jnp.mean(r, axis=-1, keepdims=True)
    rc = r - mu
    var = jnp.mean(rc * rc, axis=-1, keepdims=True)
    return rc * lax.rsqrt(var + LN_EPS) * g + b


def _cast_kernel(a_ref, o_ref):
    o_ref[...] = a_ref[...].astype(BF16)


def _cast_bf16(a):
    cols = a.shape[-1]
    rows = a.size // cols
    br = rows
    while br * cols * 4 > CAST_BLOCK_BYTES and br % 32 == 0:
        br //= 2
    out = pl.pallas_call(
        _cast_kernel,
        grid=(rows // br,),
        in_specs=[pl.BlockSpec((br, cols), lambda i: (i, 0))],
        out_specs=pl.BlockSpec((br, cols), lambda i: (i, 0)),
        out_shape=jax.ShapeDtypeStruct((rows, cols), BF16),
        compiler_params=_cp(("parallel",)),
        name="cast_bf16",
    )(a.reshape(rows, cols))
    return out.reshape(a.shape)


def _mod_kernel(a_ref, w_ref, b_ref, o_ref):
    o_ref[0] = _dot(a_ref[...].astype(BF16), w_ref[0].astype(BF16)) + b_ref[0]


def _mod_all(act, w_mod, b_mod):
    depth, d, n = w_mod.shape
    tn = 512
    return pl.pallas_call(
        _mod_kernel,
        grid=(depth, n // tn),
        in_specs=[pl.BlockSpec((8, d), lambda l, j: (0, 0)),
                  pl.BlockSpec((1, d, tn), lambda l, j: (l, 0, j)),
                  pl.BlockSpec((1, 1, tn), lambda l, j: (l, 0, j))],
        out_specs=pl.BlockSpec((1, 8, tn), lambda l, j: (l, 0, j)),
        out_shape=jax.ShapeDtypeStruct((depth, 8, n), F32),
        compiler_params=_cp(("parallel", "parallel")),
        name="mod_vectors",
    )(act, w_mod, b_mod.reshape(depth, 1, n))


PROJ_TN = 512
assert (A_W, R_W, H_W, D_W) == (3 * PROJ_TN, 4 * PROJ_TN, 3 * PROJ_TN, 2 * PROJ_TN)


def _proj_kernel(x_ref, mod_ref, w_ref, cos_ref, sin_ref, gq_ref, gk_ref,
                 za_ref, zvg_ref, zh_ref, rq_ref, rk_ref, aq_ref, ak_ref, av_ref):
    u = _modulate(x_ref[...], mod_ref, 0, 1).astype(BF16)
    tn = PROJ_TN
    cos, sin = cos_ref[...], sin_ref[...]
    lane = lax.broadcasted_iota(jnp.int32, cos.shape, 1)
    first_half = (lane % 64) < 32

    def tile(j):
        return _dot(u, w_ref[:, j * tn:(j + 1) * tn])

    def rope(x):
        partner = jnp.where(first_half, pltpu.roll(x, 96, 1), pltpu.roll(x, 32, 1))
        return x * cos + partner * sin

    def rms(x, g):
        return x * lax.rsqrt(jnp.mean(x * x, axis=-1, keepdims=True) + LN_EPS) * g

    head = lambda h: slice(h * LANE, (h + 1) * LANE)

    for j in range(3):
        za_ref[:, j * tn:(j + 1) * tn] = tile(j)
    r = tile(3)
    for h in range(RET_HEADS):
        rq_ref[:, head(h)] = rope(r[:, head(h)])
    r = tile(4)
    for h in range(RET_HEADS):
        rk_ref[:, head(h)] = rope(r[:, head(h)] * (RET_D ** -0.5))
    for j in range(2):
        zvg_ref[:, j * tn:(j + 1) * tn] = tile(5 + j)
    for j in range(3):
        zh_ref[:, j * tn:(j + 1) * tn] = tile(7 + j)
    q_scale = (HEAD_DIM ** -0.5) * math.log2(math.e)
    r = tile(10)
    for h in range(ATT_HEADS):
        aq_ref[:, head(h)] = (rope(rms(r[:, head(h)], gq_ref[...])) * q_scale).astype(BF16)
    r = tile(11)
    ones_col = jnp.where(lane == 0, 1.0, 0.0).astype(BF16)
    for h in range(ATT_KV_HEADS):
        ak_ref[:, head(h)] = rope(rms(r[:, head(h)], gk_ref[...])).astype(BF16)
        av_ref[:, head(2 * h)] = r[:, head(ATT_KV_HEADS + h)].astype(BF16)
        av_ref[:, head(2 * h + 1)] = ones_col


def _proj(x, modx, w, layer, cos_t, sin_t, gq, gk, tm):
    m, d = x.shape
    n = w.shape[2]
    tn = PROJ_TN
    qw, kw = ATT_HEADS * HEAD_DIM, ATT_KV_HEADS * HEAD_DIM
    row = lambda i: (i, 0)
    widths = (A_W, 2 * tn, H_W, tn, tn, qw, kw, 2 * kw)
    dtypes = (F32,) * 5 + (BF16,) * 3
    return pl.pallas_call(
        _proj_kernel,
        grid=(m // tm,),
        in_specs=[pl.BlockSpec((tm, d), row),
                  pl.BlockSpec((tm // SUB, 8, d), lambda i: (i, 0, 0)),
                  pl.BlockSpec((None, d, n), lambda i: (layer, 0, 0), pipeline_mode=pl.Buffered(1)),
                  pl.BlockSpec((tm, LANE), row), pl.BlockSpec((tm, LANE), row),
                  pl.BlockSpec((1, LANE), lambda i: (0, 0)), pl.BlockSpec((1, LANE), lambda i: (0, 0))],
        out_specs=[pl.BlockSpec((tm, wd), row) for wd in widths],
        out_shape=[jax.ShapeDtypeStruct((m, wd), dt) for wd, dt in zip(widths, dtypes)],
        compiler_params=_cp(("parallel",)),
        name="in_proj",
    )(x, modx, w, cos_t, sin_t, gq.reshape(1, LANE), gk.reshape(1, LANE))


def _attn_kernel(q_ref, k_ref, v_ref, *rest, key_lo, key_hi, kc):
    o_ref = rest[-1]
    tq = q_ref.shape[1]
    grp = ATT_HEADS // ATT_KV_HEADS
    q2 = jnp.concatenate([q_ref[0, :, g * HEAD_DIM:(g + 1) * HEAD_DIM] for g in range(grp)], axis=0)
    m_run, acc = None, None
    for lo in range(key_lo, key_hi, kc):
        hi = min(lo + kc, key_hi)
        s = _dot_nt(q2, k_ref[0, lo:hi, :])
        m_chunk = jnp.max(s, axis=-1, keepdims=True)
        m_new = m_chunk if m_run is None else jnp.maximum(m_run, m_chunk)
        pv = _dot(jnp.exp2(s - m_new).astype(BF16), v_ref[0, lo:hi, :])
        acc = pv if acc is None else jnp.exp2(m_run - m_new) * acc + pv
        m_run = m_new
    for g in range(grp):
        a = acc[g * tq:(g + 1) * tq]
        o_ref[0, :, g * HEAD_DIM:(g + 1) * HEAD_DIM] = a[:, :HEAD_DIM] / a[:, HEAD_DIM:HEAD_DIM + 1]


def _attention(aq, ak, av, nb, lat_len):
    m = aq.shape[0]
    t = m // nb
    lc = t - lat_len
    grp_w = (ATT_HEADS // ATT_KV_HEADS) * HEAD_DIM
    args = (aq.reshape(nb, t, -1), ak.reshape(nb, t, -1), av.reshape(nb, t, -1))

    def call(tq, n_tiles, first_tile, key_lo, key_hi, prev):
        qmap = lambda b, kv, i: (b, first_tile + i, kv)
        specs = [pl.BlockSpec((1, tq, grp_w), qmap),
                 pl.BlockSpec((1, t, HEAD_DIM), lambda b, kv, i: (b, 0, kv)),
                 pl.BlockSpec((1, t, 2 * HEAD_DIM), lambda b, kv, i: (b, 0, kv))]
        extra, alias = (), {}
        if prev is not None:
            specs.append(pl.BlockSpec(memory_space=pl.ANY))
            extra, alias = (prev,), {3: 0}
        return pl.pallas_call(
            functools.partial(_attn_kernel, key_lo=key_lo, key_hi=key_hi, kc=ATT_KC),
            grid=(nb, ATT_KV_HEADS, n_tiles),
            in_specs=specs,
            out_specs=pl.BlockSpec((1, tq, grp_w), qmap),
            out_shape=jax.ShapeDtypeStruct((nb, t, ATT_HEADS * HEAD_DIM), F32),
            input_output_aliases=alias,
            compiler_params=_cp(("parallel", "parallel", "arbitrary")),
            name="gqa_attention",
        )(*args, *extra)

    tq = math.gcd(ATT_TQ, lat_len)
    out = call(tq, lat_len // tq, 0, 0, t, None)
    out = call(lc, 1, t // lc - 1, lat_len, t, out)
    return out.reshape(m, -1)


def _ret_kernel(q_ref, k_ref, v_ref, g_ref, intra_ref, qd_ref, kd_ref, cd_ref, o_ref, st_ref,
                *, n_lat, n_ctx):
    c_len = RET_CHUNK
    n = n_lat + n_ctx

    def rows(c):
        return pl.ds(pl.multiple_of(c * c_len, c_len), c_len)

    def kv(d, c):
        return _dot_tn((k_ref[0, rows(c), :] * kd_ref[d, 0]).astype(BF16), v_ref[0, rows(c), :].astype(BF16))

    def scan_step(t, carry):
        sf, sb = carry
        cf = jnp.where(t < n_ctx, n_lat + t, t - n_ctx)
        cb = jnp.where(t < n_ctx, n - 1 - t, n_lat - 1 - (t - n_ctx))
        st_ref[cf, :RET_D, :] = sf.astype(BF16)
        st_ref[cb, RET_D:, :] = sb.astype(BF16)
        return (cd_ref[0, 0, 0:1, :] * sf + kv(0, cf), cd_ref[1, 0, 0:1, :] * sb + kv(1, cb))

    zero = jnp.zeros((RET_D, RET_D), F32)
    lax.fori_loop(0, n, scan_step, (zero, zero), unroll=RET_UNROLL)

    mask = intra_ref[0, 0] + intra_ref[1, 0]

    def out_step(c, carry):
        qc = q_ref[0, rows(c), :]
        vc = v_ref[0, rows(c), :].astype(BF16)
        s = _dot_nt(qc.astype(BF16), k_ref[0, rows(c), :].astype(BF16)) * mask
        qdec = jnp.concatenate([qc * qd_ref[0, 0], qc * qd_ref[1, 0]], axis=1).astype(BF16)
        o = _dot(s.astype(BF16), vc) + _dot(qdec, st_ref[c])
        mu = jnp.mean(o, axis=-1, keepdims=True)
        oc = o - mu
        var = jnp.mean(oc * oc, axis=-1, keepdims=True)
        g = g_ref[0, rows(c), :]
        o_ref[0, rows(c), :] = (g * jax.nn.sigmoid(g)) * (oc * lax.rsqrt(var + LN_EPS))
        return carry

    lax.fori_loop(0, n, out_step, 0, unroll=RET_UNROLL)


def _retention(rq, rk, z, tabs, nb, lat_len):
    m = rq.shape[0]
    t = m // nb
    intra, qd, kd, cd = tabs
    v_blk, g_blk = 0, RET_HEADS
    kern = functools.partial(_ret_kernel, n_lat=lat_len // RET_CHUNK, n_ctx=(t - lat_len) // RET_CHUNK)
    seq = lambda off: pl.BlockSpec((1, t, LANE), lambda b, h: (b, 0, off + h))
    tab = lambda r: pl.BlockSpec((2, 1, r, LANE), lambda b, h: (0, h, 0, 0))
    out = pl.pallas_call(
        kern,
        grid=(nb, RET_HEADS),
        in_specs=[seq(0), seq(0), seq(v_blk), seq(g_blk),
                  tab(RET_CHUNK), tab(RET_CHUNK), tab(RET_CHUNK), tab(8)],
        out_specs=seq(0),
        out_shape=jax.ShapeDtypeStruct((nb, t, RET_HEADS * RET_D), F32),
        scratch_shapes=[pltpu.VMEM((t // RET_CHUNK, 2 * RET_D, RET_D), BF16)],
        compiler_params=_cp(("parallel", "parallel")),
        name="retention",
    )(rq.reshape(nb, t, -1), rk.reshape(nb, t, -1), z.reshape(nb, t, -1), z.reshape(nb, t, -1), intra, qd, kd, cd)
    return out.reshape(m, -1)


def _retention_tables(log_gamma):
    c = RET_CHUNK
    i = jnp.arange(c, dtype=F32)
    lg = log_gamma.astype(F32)[:, :, None]
    rel = i[:, None] - i[None, :]
    intra_f = jnp.where(rel >= 0, jnp.exp(lg[0][..., None] * jnp.maximum(rel, 0.0)), 0.0)
    intra_b = jnp.where(rel <= 0, jnp.exp(lg[1][..., None] * jnp.maximum(-rel, 0.0)), 0.0)
    qd = jnp.stack([jnp.exp(lg[0] * (i + 1.0)), jnp.exp(lg[1] * (c - i))])
    kd = jnp.stack([jnp.exp(lg[0] * (c - 1.0 - i)), jnp.exp(lg[1] * i)])
    cd = jnp.exp(lg * c)
    bc = lambda a: jnp.broadcast_to(a[..., None], a.shape + (LANE,))
    return (jnp.stack([intra_f, intra_b]), bc(qd), bc(kd),
            jnp.broadcast_to(cd[..., None], (2, RET_HEADS, 8, LANE)))


def _shift_rows(x, lat_len):
    t = x.shape[0]
    row = lax.broadcasted_iota(jnp.int32, x.shape, 0)
    xm = jnp.where((row == 0) | (row == lat_len), 0.0, pltpu.roll(x, 1, 0))
    xp = jnp.where((row == lat_len - 1) | (row == t - 1), 0.0, pltpu.roll(x, t - 1, 0))
    return xm, xp


def _shortconv_kernel(bg_ref, cg_ref, xv_ref, w_ref, b_ref, o_ref, *, lat_len):
    p = cg_ref[0] * xv_ref[0]
    pm, pp = _shift_rows(p, lat_len)
    o_ref[0] = bg_ref[0] * (pm * w_ref[0:1, :] + p * w_ref[1:2, :] + pp * w_ref[2:3, :] + b_ref[...])


def _shortconv(z, w, b, nb, lat_len):
    m = z.shape[0]
    t = m // nb
    nct = BRANCH_W // LANE
    kern = functools.partial(_shortconv_kernel, lat_len=lat_len)
    seq = lambda off: pl.BlockSpec((1, t, LANE), lambda bb, j: (bb, 0, off + j))
    out = pl.pallas_call(
        kern,
        grid=(nb, nct),
        in_specs=[seq(0), seq(nct), seq(2 * nct),
                  pl.BlockSpec((3, LANE), lambda bb, j: (0, j)),
                  pl.BlockSpec((1, LANE), lambda bb, j: (0, j))],
        out_specs=seq(0),
        out_shape=jax.ShapeDtypeStruct((nb, t, BRANCH_W), F32),
        compiler_params=_cp(("parallel", "parallel")),
        name="short_conv",
    )(z.reshape(nb, t, -1), z.reshape(nb, t, -1), z.reshape(nb, t, -1), w, b.reshape(1, -1))
    return out.reshape(m, -1)


def _hyconv_kernel(z_ref, w_ref, b_ref, ol_ref, oc_ref, *, lat_len):
    x = z_ref[0]
    xm, xp = _shift_rows(x, lat_len)
    y = xm * w_ref[0:1, :] + x * w_ref[1:2, :] + xp * w_ref[2:3, :] + b_ref[...]
    ol_ref[0, 0] = y[:lat_len]
    oc_ref[0, 0] = y[lat_len:]


def _hyconv(z, w, b, nb, lat_len):
    m = z.shape[0]
    t = m // nb
    nct = BRANCH_W // LANE
    off = 0
    kern = functools.partial(_hyconv_kernel, lat_len=lat_len)
    return pl.pallas_call(
        kern,
        grid=(nb, 3 * nct),
        in_specs=[pl.BlockSpec((1, t, LANE), lambda bb, j: (bb, 0, off + j)),
                  pl.BlockSpec((3, LANE), lambda bb, j: (0, j)),
                  pl.BlockSpec((1, LANE), lambda bb, j: (0, j))],
        out_specs=[pl.BlockSpec((1, 1, lat_len, LANE), lambda bb, j: (j // nct, bb, 0, j % nct)),
                   pl.BlockSpec((1, 1, t - lat_len, LANE), lambda bb, j: (j // nct, bb, 0, j % nct))],
        out_shape=[jax.ShapeDtypeStruct((3, nb, lat_len, BRANCH_W), F32),
                   jax.ShapeDtypeStruct((3, nb, t - lat_len, BRANCH_W), F32)],
        compiler_params=_cp(("parallel", "parallel")),
        name="hyena_dwconv",
    )(z.reshape(nb, t, -1), w, b.reshape(1, -1))


def _stack_c(m):
    return np.block([[m.real, -m.imag], [m.imag, m.real]])


def _split_np(m):
    m32 = np.asarray(m, np.float32)
    hi = m32.astype(BF16)
    lo = (m32 - hi.astype(np.float32)).astype(BF16)
    return np.concatenate([hi, lo], axis=-2)


@functools.lru_cache(maxsize=None)
def _dft_tables(lat_len):
    n = 2 * lat_len
    n1, n2 = FFT_N1, 2 * lat_len // FFT_N1
    a1 = np.arange(n1)
    a2 = np.arange(n2)
    w1 = np.exp(-2j * np.pi * np.outer(a1, a1[:n1 // 2]) / n1)
    f1 = _stack_c(w1)
    w2 = np.exp(-2j * np.pi * np.outer(a2, a2) / n2)
    tw = np.exp(-2j * np.pi * np.outer(a1, a2) / n)
    cf = w2[None, :, :] * tw[:, None, :]
    f2 = np.stack([_stack_c(cf[k]) for k in range(n1)])
    g2 = np.stack([_stack_c(cf[k].conj().T) for k in range(n1)])
    w1i = np.exp(2j * np.pi * np.outer(a1[:n1 // 2], a1) / n1) / n
    f3 = _stack_c(w1i)
    f1_real = f1[:, :n1 // 2]
    return _split_np(f1), _split_np(f1_real), _split_np(f2), _split_np(g2), _split_np(f3)


@functools.lru_cache(maxsize=None)
def _dft_tables_small(ctx_len):
    n = 2 * ctx_len
    a = np.arange(n)
    ff = _stack_c(np.exp(-2j * np.pi * np.outer(a, a[:ctx_len]) / n))
    fi = _stack_c(np.exp(2j * np.pi * np.outer(a[:ctx_len], a) / n) / n)
    return _split_np(ff), _split_np(ff[:, :ctx_len]), _split_np(fi)


def _fft1_kernel(x_ref, f_ref, o_ref, *, passes):
    f = f_ref[...]
    for j in range(x_ref.shape[3]):
        parts = [x_ref[0, b, :, j, :] for b in range(x_ref.shape[1])]
        r = _split_dot(f, parts[0] if len(parts) == 1 else jnp.concatenate(parts, axis=0), passes)
        h = r.shape[0] // 2
        o_ref[0, 0, :, j, :] = r[:h]
        o_ref[0, 1, :, j, :] = r[h:]


def _fft1(x, sel, f1, jb, passes):
    _, p, mem, r, n2, w = x.shape
    f1 = _dft_operand(f1, passes)
    n1 = f1.shape[0] // (2 if passes == 1 else 4)
    return pl.pallas_call(
        functools.partial(_fft1_kernel, passes=passes),
        grid=(p, n2 // jb),
        in_specs=[pl.BlockSpec((None, 1, mem, r, jb, w), lambda pp, j: (sel, pp, 0, 0, j, 0)),
                  pl.BlockSpec(f1.shape, lambda pp, j: (0, 0))],
        out_specs=pl.BlockSpec((1, 2, n1, jb, w), lambda pp, j: (pp, 0, 0, j, 0)),
        out_shape=jax.ShapeDtypeStruct((p, 2, n1, n2, w), F32),
        compiler_params=_cp(("parallel", "parallel")),
        name="hyena_fft_stage1",
    )(x, f1)


def _fft2f_kernel(a_ref, f_ref, o_ref, *, kb, n2, passes):
    w = a_ref.shape[-1]
    for t in range(kb):
        xf = _split_dot(f_ref[t], a_ref[0, :, t].reshape(2 * n2, w), passes)
        xb = _split_dot(f_ref[t], a_ref[1, :, t].reshape(2 * n2, w), passes)
        o_ref[0, 0, t] = xf[:n2] + xb[:n2]
        o_ref[0, 1, t] = xf[n2:] - xb[n2:]


def _fft2_filters(a, f2, kb, passes):
    q, _, n1, n2, w = a.shape
    f2 = _dft_operand(f2, passes)
    return pl.pallas_call(
        functools.partial(_fft2f_kernel, kb=kb, n2=n2, passes=passes),
        grid=(q // 2, n1 // kb),
        in_specs=[pl.BlockSpec((2, 2, kb, n2, w), lambda o, k: (o, 0, k, 0, 0)),
                  pl.BlockSpec((kb,) + f2.shape[1:], lambda o, k: (k, 0, 0))],
        out_specs=pl.BlockSpec((1, 2, kb, n2, w), lambda o, k: (o, 0, k, 0, 0)),
        out_shape=jax.ShapeDtypeStruct((q // 2, 2, n1, n2, w), F32),
        compiler_params=_cp(("parallel", "parallel")),
        name="hyena_filter_spectrum",
    )(a, f2)


def _fft2_kernel(a_ref, h_ref, f_ref, g_ref, o_ref, *, kb, n2, passes):
    w = a_ref.shape[-1]
    for t in range(kb):
        a = a_ref[0, :, t].reshape(2 * n2, w)
        x = _split_dot(f_ref[t], a, passes)
        xr, xi = x[:n2], x[n2:]
        hr, hi = h_ref[0, t], h_ref[1, t]
        y = jnp.concatenate([xr * hr - xi * hi, xr * hi + xi * hr], axis=0)
        o_ref[0, :, t] = _split_dot(g_ref[t], y, passes).reshape(2, n2, w)


def _fft2(a, hspec, order, f2, g2, kb, passes):
    p, _, n1, n2, w = a.shape
    f2, g2 = _dft_operand(f2, passes), _dft_operand(g2, passes)
    kern = functools.partial(_fft2_kernel, kb=kb, n2=n2, passes=passes)
    return pl.pallas_call(
        kern,
        grid=(p, n1 // kb),
        in_specs=[pl.BlockSpec((1, 2, kb, n2, w), lambda pp, k: (pp, 0, k, 0, 0)),
                  pl.BlockSpec((None, 2, kb, n2, w), lambda pp, k: (order, 0, k, 0, 0)),
                  pl.BlockSpec((kb,) + f2.shape[1:], lambda pp, k: (k, 0, 0)),
                  pl.BlockSpec((kb,) + g2.shape[1:], lambda pp, k: (k, 0, 0))],
        out_specs=pl.BlockSpec((1, 2, kb, n2, w), lambda pp, k: (pp, 0, k, 0, 0)),
        out_shape=jax.ShapeDtypeStruct(a.shape, F32),
        compiler_params=_cp(("parallel", "parallel")),
        name="hyena_fft_stage2",
    )(a, hspec, f2, g2)


def _fft3_kernel(b_ref, f_ref, gate_ref, prev_ref, skip_ref, o_ref, *, passes):
    f = f_ref[...]
    r = gate_ref.shape[2]
    for j in range(b_ref.shape[3]):
        bj = jnp.concatenate([b_ref[0, 0, :, j, :], b_ref[0, 1, :, j, :]], axis=0)
        conv = _split_dot(f, bj, passes)
        for mem in range(2):
            cm = conv[mem * r:(mem + 1) * r]
            o_ref[0, mem, :, j, :] = gate_ref[0, mem, :, j, :] * (cm + skip_ref[...] * prev_ref[0, mem, :, j, :])


def _fft3(bm, f3, gate, gate_sel, prev, prev_sel, skip, out_rows, jb, passes):
    p, _, n1, n2, w = bm.shape
    r = n1 // 2
    f3 = _dft_operand(f3, passes)
    seq = lambda s: pl.BlockSpec((None, 1, 2, r, jb, w), lambda pp, j: (s, pp, 0, 0, j, 0))
    return pl.pallas_call(
        functools.partial(_fft3_kernel, passes=passes),
        grid=(p, n2 // jb),
        in_specs=[pl.BlockSpec((1, 2, n1, jb, w), lambda pp, j: (pp, 0, 0, j, 0)),
                  pl.BlockSpec(f3.shape, lambda pp, j: (0, 0)),
                  seq(gate_sel), seq(prev_sel),
                  pl.BlockSpec((1, w), lambda pp, j: (0, 0))],
        out_specs=pl.BlockSpec((1, 2, r, jb, w), lambda pp, j: (pp, 0, 0, j, 0)),
        out_shape=jax.ShapeDtypeStruct((p, 2, out_rows, n2, w), F32),
        compiler_params=_cp(("parallel", "parallel")),
        name="hyena_fft_stage3",
    )(bm, f3, gate, prev, skip)


def _hyctx_kernel(v_ref, x1_ref, x2_ref, filt_ref, ff_ref, ffr_ref, fi_ref, skip_ref, yh_hbm_ref, o_ref,
                  *, passes, filter_passes):
    del yh_hbm_ref
    lc, w = v_ref.shape[2], v_ref.shape[3]
    n = 2 * lc
    y = v_ref[0].reshape(2 * lc, w)
    for o, gate_ref in enumerate((x1_ref, x2_ref)):
        sf = _split_dot(ffr_ref[...], filt_ref[o, 0], filter_passes)
        sb = _split_dot(ffr_ref[...], filt_ref[o, 1], filter_passes)
        hr, hi = sf[:n] + sb[:n], sf[n:] - sb[n:]
        x = _split_dot(ff_ref[...], y, passes)
        xr, xi = x[:n], x[n:]
        spec = jnp.concatenate([xr * hr - xi * hi, xr * hi + xi * hr], axis=0)
        conv = _split_dot(fi_ref[...], spec, passes)
        y = gate_ref[0].reshape(2 * lc, w) * (conv + skip_ref[o:o + 1, :] * y)
    o_ref[0] = y.reshape(2, lc, w)


def _hyena_ctx(hc, filt_c, ff, ffr, fi, skip, yh, passes, filter_passes):
    _, nb, lc, w = hc.shape
    p = nb // 2
    t = yh.shape[2]
    ff, fi, ffr = _dft_operand(ff, passes), _dft_operand(fi, passes), _dft_operand(ffr, filter_passes)
    part = pl.BlockSpec((1, 2, lc, w), lambda pp: (pp, 0, 0, 0))
    hc4 = hc.reshape(3, p, 2, lc, w)
    return pl.pallas_call(
        functools.partial(_hyctx_kernel, passes=passes, filter_passes=filter_passes),
        grid=(p,),
        in_specs=[part, part, part,
                  pl.BlockSpec(filt_c.shape, lambda pp: (0, 0, 0, 0)),
                  pl.BlockSpec(ff.shape, lambda pp: (0, 0)),
                  pl.BlockSpec(ffr.shape, lambda pp: (0, 0)),
                  pl.BlockSpec(fi.shape, lambda pp: (0, 0)),
                  pl.BlockSpec(skip.shape, lambda pp: (0, 0)),
                  pl.BlockSpec(memory_space=pl.ANY)],
        out_specs=pl.BlockSpec((1, 2, lc, w), lambda pp: (pp, 0, t // lc - 1, 0)),
        out_shape=jax.ShapeDtypeStruct(yh.shape, F32),
        input_output_aliases={8: 0},
        compiler_params=_cp(("parallel",)),
        name="hyena_ctx",
    )(hc4[0], hc4[1], hc4[2], filt_c, ff, ffr, fi, skip, yh)


def _hyena_filters(l, w1, b1, w2, b2, w3, freq):
    t = jnp.linspace(0.0, 1.0, l, dtype=F32)[:, None]
    bands = (HY_EMB - 1) // 2
    w = 2.0 * math.pi * jnp.arange(l, dtype=F32) / l
    f = jnp.linspace(1e-4, bands - 1, bands, dtype=F32)
    ang = w[:, None] * f[None, :]
    feats = jnp.concatenate([t, jnp.cos(ang), -jnp.sin(ang)], axis=-1)
    hdn = jnp.sin(freq[0] * (feats @ w1 + b1))
    hdn = jnp.sin(freq[1] * (hdn @ w2 + b2))
    n_out = w3.shape[-1]
    deltas = jnp.linspace(math.log(HY_TARGET) / HY_SLOW_PCT, math.log(HY_TARGET) / HY_FAST_PCT, n_out, dtype=F32)
    decay = jnp.exp(-t * jnp.abs(deltas))
    pos = jnp.arange(l)[:, None]
    parts = []
    for q in range(n_out // BRANCH_W):
        sl = slice(q * BRANCH_W, (q + 1) * BRANCH_W)
        f = (hdn @ w3[:, sl]).astype(F32) * decay[:, sl]
        parts.append(jnp.where(pos == 0, 0.0, f) if q % 2 else f)
    return jnp.stack(parts).reshape(n_out // (2 * BRANCH_W), 2, l, BRANCH_W)


def _hyena(z, conv_w, conv_b, filt_w, skip, nb, lat_len):
    m = z.shape[0]
    t = m // nb
    lc = t - lat_len
    w = BRANCH_W
    p = nb // 2
    n1, n2 = FFT_N1, 2 * lat_len // FFT_N1
    r = n1 // 2
    hl, hc = _hyconv(z, conv_w, conv_b, nb, lat_len)
    f1, f1r, f2, g2, f3 = _dft_tables(lat_len)
    ff, ffr, fi = _dft_tables_small(lc)
    jb, kb = min(16, n2), 16
    dp, fp = DFT_PASSES, DFT_FILTER_PASSES
    filt_l = _hyena_filters(lat_len, *filt_w)
    fa = _fft1(filt_l.reshape(1, 4, 1, r, n2, w), 0, f1r, jb, fp)
    hs_l = _fft2_filters(fa, f2, kb, fp)
    hl6 = hl.reshape(3, p, 2, r, n2, w)
    y, y_sel = hl6, 0
    for o in range(2):
        a = _fft1(y, y_sel, f1, jb, dp)
        bm = _fft2(a, hs_l, o, f2, g2, kb, dp)
        rows = r if o == 0 else t // n2
        y = _fft3(bm, f3, hl6, 1 + o, y, y_sel, skip[o:o + 1], rows, jb, dp)[None]
        y_sel = 0
    yh = y.reshape(p, 2, t, w)
    yh = _hyena_ctx(hc, _hyena_filters(lc, *filt_w), ff, ffr, fi, skip, yh, dp, fp)
    return yh.reshape(m, w)


def _merge_kernel(x_ref, mod_ref, ya_ref, yr_ref, yh_ref, yd_ref, wg_ref, bg_ref, wb_ref, wo_ref, g_ref, b_ref,
                  o_ref, *, alpha):
    x = x_ref[...]
    u = _modulate(x, mod_ref, 0, 1).astype(BF16)
    acc = None
    for i, y_ref in enumerate((ya_ref, yr_ref, yh_ref, yd_ref)):
        gate = jax.nn.sigmoid(_dot(u, wg_ref[i]) + bg_ref[i:i + 1, :])
        term = gate * _dot(y_ref[...].astype(BF16), wb_ref[i])
        acc = term if acc is None else acc + term
    mix = _dot(acc.astype(BF16), wo_ref[...])
    o_ref[...] = _gated_residual_ln(x, mix, mod_ref, 2, alpha, g_ref[...], b_ref[...])


def _merge(x, modx, ys, wg, bg, wb, wo, layer, g, b, alpha, tm):
    m, d = x.shape
    w = BRANCH_W
    row = lambda i: (i, 0)
    const2 = lambda i: (0, 0)
    one = pl.Buffered(1)
    stacked = lambda a: pl.BlockSpec((None,) + a.shape[1:], lambda i: (layer,) + (0,) * (a.ndim - 1),
                                     pipeline_mode=one)
    return pl.pallas_call(
        functools.partial(_merge_kernel, alpha=alpha),
        grid=(m // tm,),
        in_specs=[pl.BlockSpec((tm, d), row),
                  pl.BlockSpec((tm // SUB, 8, d), lambda i: (i, 0, 0)),
                  pl.BlockSpec((tm, w), row), pl.BlockSpec((tm, w), row),
                  pl.BlockSpec((tm, w), row), pl.BlockSpec((tm, w), row),
                  stacked(wg), pl.BlockSpec(bg.shape, const2), stacked(wb), stacked(wo),
                  pl.BlockSpec((1, d), const2), pl.BlockSpec((1, d), const2)],
        out_specs=pl.BlockSpec((tm, d), row),
        out_shape=jax.ShapeDtypeStruct((m, d), F32),
        compiler_params=_cp(("parallel",)),
        name="branch_merge",
    )(x, modx, *ys, wg, bg, wb, wo, g.reshape(1, d), b.reshape(1, d))


def _ffn_kernel(x_ref, mod_ref, w1_ref, w3_ref, w2_ref, g_ref, b_ref, o_ref, *, alpha, tf):
    x = x_ref[...]
    u = _modulate(x, mod_ref, 3, 4).astype(BF16)
    acc = None
    for j in range(w1_ref.shape[1] // tf):
        sl = slice(j * tf, (j + 1) * tf)
        h1 = _dot(u, w1_ref[:, sl])
        h3 = _dot(u, w3_ref[:, sl])
        part = _dot((h1 * jax.nn.sigmoid(h1) * h3).astype(BF16), w2_ref[sl, :])
        acc = part if acc is None else acc + part
    o_ref[...] = _gated_residual_ln(x, acc, mod_ref, 5, alpha, g_ref[...], b_ref[...])


def _ffn(x, modx, w1, w3, w2, layer, g, b, alpha, tm, tf):
    m, d = x.shape
    resident = lambda a: pl.BlockSpec((None,) + a.shape[1:], lambda i: (layer, 0, 0), pipeline_mode=pl.Buffered(1))
    return pl.pallas_call(
        functools.partial(_ffn_kernel, alpha=alpha, tf=tf),
        grid=(m // tm,),
        in_specs=[pl.BlockSpec((tm, d), lambda i: (i, 0)),
                  pl.BlockSpec((tm // SUB, 8, d), lambda i: (i, 0, 0)),
                  resident(w1), resident(w3), resident(w2),
                  pl.BlockSpec((1, d), lambda i: (0, 0)),
                  pl.BlockSpec((1, d), lambda i: (0, 0))],
        out_specs=pl.BlockSpec((tm, d), lambda i: (i, 0)),
        out_shape=jax.ShapeDtypeStruct((m, d), F32),
        compiler_params=_cp(("parallel",)),
        name="dense_swiglu",
    )(x, modx, w1, w3, w2, g.reshape(1, d), b.reshape(1, d))


def _route_kernel(x_ref, mod_ref, wr_ref, u_ref, lg_ref):
    u = _modulate(x_ref[...], mod_ref, 3, 4)
    u_ref[...] = u
    lg_ref[...] = jnp.dot(u, wr_ref[...], preferred_element_type=F32, precision=lax.Precision.HIGHEST)


def _route(x, modx, wr_pad, tm):
    m, d = x.shape
    return pl.pallas_call(
        _route_kernel,
        grid=(m // tm,),
        in_specs=[pl.BlockSpec((tm, d), lambda i: (i, 0)),
                  pl.BlockSpec((tm // SUB, 8, d), lambda i: (i, 0, 0)),
                  pl.BlockSpec((d, LANE), lambda i: (0, 0))],
        out_specs=[pl.BlockSpec((tm, d), lambda i: (i, 0)), pl.BlockSpec((tm, LANE), lambda i: (i, 0))],
        out_shape=[jax.ShapeDtypeStruct((m, d), F32), jax.ShapeDtypeStruct((m, LANE), F32)],
        compiler_params=_cp(("parallel",)),
        name="moe_router",
    )(x, modx, wr_pad)


def _moe_kernel(be_ref, nu_ref, xb_ref, w1_ref, w3_ref, w2_ref, o_ref, acc_ref, u_ref):
    i, j = pl.program_id(0), pl.program_id(1)
    live = i < nu_ref[0]

    @pl.when(j == 0)
    def _():
        acc_ref[...] = jnp.zeros_like(acc_ref)
        u_ref[...] = xb_ref[...].astype(BF16)

    @pl.when(live)
    def _():
        u = u_ref[...]
        h1 = _dot(u, w1_ref[0])
        h3 = _dot(u, w3_ref[0])
        acc_ref[...] += _dot((h1 * jax.nn.sigmoid(h1) * h3).astype(BF16), w2_ref[0])

    @pl.when(j == pl.num_programs(1) - 1)
    def _():
        o_ref[...] = acc_ref[...]


def _moe_experts(xb, blk_e, n_used, w1, w3, w2, layer, tm, tf):
    r, d = xb.shape
    ff = w1.shape[3]
    nj = ff // tf

    def wmap(i, j, be, nu):
        return (layer, be[i], 0, jnp.where(i < nu[0], j, nj - 1))

    def w2map(i, j, be, nu):
        return (layer, be[i], jnp.where(i < nu[0], j, nj - 1), 0)

    grid_spec = pltpu.PrefetchScalarGridSpec(
        num_scalar_prefetch=2,
        grid=(r // tm, nj),
        in_specs=[pl.BlockSpec((tm, d), lambda i, j, be, nu: (i, 0)),
                  pl.BlockSpec((None, 1, d, tf), wmap),
                  pl.BlockSpec((None, 1, d, tf), wmap),
                  pl.BlockSpec((None, 1, tf, d), w2map)],
        out_specs=pl.BlockSpec((tm, d), lambda i, j, be, nu: (i, 0)),
        scratch_shapes=[pltpu.VMEM((tm, d), F32), pltpu.VMEM((tm, d), BF16)],
    )
    return pl.pallas_call(
        _moe_kernel,
        grid_spec=grid_spec,
        out_shape=jax.ShapeDtypeStruct((r, d), F32),
        compiler_params=_cp(("arbitrary", "arbitrary")),
        name="moe_experts",
    )(blk_e, n_used, xb, w1, w3, w2)


def _ln_res_kernel(x_ref, f0_ref, f1_ref, wt_ref, mod_ref, g_ref, b_ref, o_ref, *, alpha):
    f = f0_ref[...] * wt_ref[:, 0:1] + f1_ref[...] * wt_ref[:, 1:2]
    o_ref[...] = _gated_residual_ln(x_ref[...], f, mod_ref, 5, alpha, g_ref[...], b_ref[...])


def _ln_res(x, f01, wts, modx, g, b, alpha, tm):
    m, d = x.shape
    nt = m // tm
    return pl.pallas_call(
        functools.partial(_ln_res_kernel, alpha=alpha),
        grid=(nt,),
        in_specs=[pl.BlockSpec((tm, d), lambda i: (i, 0)),
                  pl.BlockSpec((tm, d), lambda i: (i, 0)),
                  pl.BlockSpec((tm, d), lambda i: (i + nt, 0)),
                  pl.BlockSpec((tm, TOP_K), lambda i: (i, 0)),
                  pl.BlockSpec((tm // SUB, 8, d), lambda i: (i, 0, 0)),
                  pl.BlockSpec((1, d), lambda i: (0, 0)), pl.BlockSpec((1, d), lambda i: (0, 0))],
        out_specs=pl.BlockSpec((tm, d), lambda i: (i, 0)),
        out_shape=jax.ShapeDtypeStruct((m, d), F32),
        compiler_params=_cp(("parallel",)),
        name="moe_residual_ln",
    )(x, f01, f01, wts, modx, g.reshape(1, d), b.reshape(1, d))


def _moe(x, modx, router, w1, w3, w2, layer, g, b, alpha, tm_rows, tm, tf):
    m, d = x.shape
    wr_pad = jnp.zeros((d, LANE), F32).at[:, :N_EXPERTS].set(router)
    u, logits = _route(x, modx, wr_pad, tm_rows)
    top_v, top_i = lax.top_k(logits[:, :N_EXPERTS], TOP_K)
    wts = jax.nn.softmax(top_v, axis=-1)
    e_flat = top_i.reshape(-1).astype(jnp.int32)
    n_pair = m * TOP_K
    onehot = (e_flat[:, None] == jnp.arange(N_EXPERTS, dtype=jnp.int32)[None, :]).astype(jnp.int32)
    csum = jnp.cumsum(onehot, axis=0)
    counts = csum[-1]
    rank = jnp.take_along_axis(csum, e_flat[:, None], axis=1)[:, 0] - 1
    padded = (counts + tm - 1) // tm * tm
    pad_end = jnp.cumsum(padded)
    pad_start = pad_end - padded
    dest = pad_start[e_flat] + rank
    n_blk = -(-n_pair // tm) + N_EXPERTS
    r = n_blk * tm
    tok = jnp.arange(n_pair, dtype=jnp.int32) // TOP_K
    n_used = (pad_end[-1] // tm).astype(jnp.int32)
    blk = jnp.minimum(jnp.arange(n_blk, dtype=jnp.int32), n_used - 1) * tm
    blk_e = jnp.minimum(jnp.searchsorted(pad_end, blk, side='right'), N_EXPERTS - 1).astype(jnp.int32)
    _, tok_sorted = lax.sort_key_val(dest, tok)
    shift = pad_start - (jnp.cumsum(counts) - counts)
    src = jnp.arange(r, dtype=jnp.int32) - jnp.repeat(shift[blk_e], tm).astype(jnp.int32)
    row_tok = jnp.take(tok_sorted, jnp.clip(src, 0, n_pair - 1), mode="clip")
    xb = jnp.take(u, row_tok, axis=0, mode="clip")
    yb = _moe_experts(xb, blk_e, n_used.reshape(1), w1, w3, w2, layer, tm, tf)
    f01 = jnp.take(yb, dest.reshape(m, TOP_K).T.reshape(-1), axis=0, mode="clip")
    return _ln_res(x, f01, wts, modx, g, b, alpha, tm_rows)


def _rope_tables(nb, lat_len, ctx_len):
    rows = lat_len // GRID_W
    row = jnp.repeat(jnp.arange(rows, dtype=F32), GRID_W)
    col = jnp.tile(jnp.arange(GRID_W, dtype=F32), rows)
    n_freq = HEAD_DIM // 4
    inv = ROPE_THETA ** (-jnp.arange(n_freq, dtype=F32) / n_freq)
    ar, ac = row[:, None] * inv, col[:, None] * inv
    cos = jnp.concatenate([jnp.cos(ar), jnp.cos(ar), jnp.cos(ac), jnp.cos(ac)], axis=1)
    sin = jnp.concatenate([-jnp.sin(ar), jnp.sin(ar), -jnp.sin(ac), jnp.sin(ac)], axis=1)
    cos = jnp.concatenate([cos, jnp.ones((ctx_len, HEAD_DIM), F32)], axis=0)
    sin = jnp.concatenate([sin, jnp.zeros((ctx_len, HEAD_DIM), F32)], axis=0)
    return jnp.tile(cos, (nb, 1)), jnp.tile(sin, (nb, 1))


def _row_tile(m, pref):
    tm = pref
    while m % tm:
        tm //= 2
    return tm


def kernel(x, c, ctx, c_ctx, w_mod, b_mod, w_in, conv_a_w, conv_a_b, ret_decay, hy_conv_w, hy_conv_b, hy_w1, hy_b1, hy_w2, hy_b2, hy_w3, hy_freq, hy_skip, q_norm, k_norm, w_gate, b_gate, w_branch, w_o, ln_g, ln_b, ffn_w1, ffn_w3, ffn_w2, router, moe_w1, moe_w3, moe_w2):
    nb, lat_len, d = x.shape
    ctx_len = ctx.shape[1]
    depth = w_mod.shape[0]
    t = lat_len + ctx_len
    m = nb * t
    alpha = float((2 * depth) ** 0.25)
    assert lat_len % SUB == 0 and ctx_len % SUB == 0 and nb % 2 == 0

    act = jnp.concatenate([jax.nn.silu(c), jax.nn.silu(c_ctx)[None, :],
                           jnp.zeros((8 - nb - 1, d), F32)], axis=0)
    mods = _mod_all(act, w_mod, b_mod).reshape(depth, 8, N_MOD, d)
    mods = jnp.pad(mods, ((0, 0), (0, 0), (0, 8 - N_MOD), (0, 0)))
    tiles_per_batch = t // SUB
    tile_pos = np.arange(m // SUB) % tiles_per_batch
    tile_group = np.where(tile_pos < lat_len // SUB, np.arange(m // SUB) // tiles_per_batch, nb)
    cos_t, sin_t = _rope_tables(nb, lat_len, ctx_len)

    cast = _cast_bf16
    w_in_b, w_gate_b, w_branch_b, w_o_b = cast(w_in), cast(w_gate), cast(w_branch), cast(w_o)
    ffn_b = (cast(ffn_w1), cast(ffn_w3), cast(ffn_w2))
    moe_w = (cast(moe_w1), cast(moe_w3), cast(moe_w2))

    rows = jnp.concatenate([x, ctx], axis=1).reshape(m, d)
    tm = _row_tile(m, 512)
    for l in range(depth):
        modx = mods[l][tile_group]
        za, zvg, zh, rq, rk, aq, ak, av = _proj(rows, modx, w_in_b, l, cos_t, sin_t, q_norm[l], k_norm[l], tm)
        y_d = _attention(aq, ak, av, nb, lat_len)
        log_gamma = -jnp.exp(ret_decay[l].astype(F32))
        y_r = _retention(rq, rk, zvg, _retention_tables(log_gamma), nb, lat_len)
        y_a = _shortconv(za, conv_a_w[l], conv_a_b[l], nb, lat_len)
        y_h = _hyena(zh, hy_conv_w[l], hy_conv_b[l],
                     (hy_w1[l], hy_b1[l], hy_w2[l], hy_b2[l], hy_w3[l], hy_freq[l]), hy_skip[l], nb, lat_len)
        rows = _merge(rows, modx, (y_a, y_r, y_h, y_d), w_gate_b, b_gate[l], w_branch_b, w_o_b, l,
                      ln_g[l, 0], ln_b[l, 0], alpha, tm)
        i = l // 2
        if l % 2 == 0:
            rows = _ffn(rows, modx, *ffn_b, i, ln_g[l, 1], ln_b[l, 1], alpha, tm, 256)
        else:
            rows = _moe(rows, modx, router[i], *moe_w, i, ln_g[l, 1], ln_b[l, 1], alpha, tm, 512, 1792)
    return rows.reshape(nb, t, d)[:, :lat_len]
```

```python
import functools
import math

import numpy as np
import jax
import jax.numpy as jnp
from jax import lax
from jax.experimental import pallas as pl
from jax.experimental.pallas import tpu as pltpu

F32 = jnp.float32
BF16 = jnp.bfloat16

GRID_W = 64
BRANCH_W = 512
RET_HEADS = 4
RET_D = 128
RET_CHUNK = 128
HY_EMB = 33
HY_FAST_PCT = 0.3
HY_SLOW_PCT = 1.5
HY_TARGET = 1e-2
ATT_HEADS = 4
ATT_KV_HEADS = 2
HEAD_DIM = 128
ROPE_THETA = 10000.0
N_EXPERTS = 8
TOP_K = 2
N_MOD = 6
A_W = 3 * BRANCH_W
R_W = 4 * RET_HEADS * RET_D
H_W = 3 * BRANCH_W
D_W = (ATT_HEADS + 2 * ATT_KV_HEADS) * HEAD_DIM
IN_W = A_W + R_W + H_W + D_W
LN_EPS = 1e-6

SUB = 256
LANE = 128
FFT_N1 = 128
RET_UNROLL = 8
ATT_TQ = 1024
ATT_KC = 1024
DFT_PASSES = 1
DFT_FILTER_PASSES = 1
VMEM_LIMIT = 56 * 1024 * 1024


def _cp(sem, vmem=VMEM_LIMIT):
    return pltpu.CompilerParams(dimension_semantics=sem, vmem_limit_bytes=vmem)


def _dot(a, b):
    return jnp.dot(a, b, preferred_element_type=F32)


def _dot_nt(a, b):
    return lax.dot_general(a, b, (((1,), (1,)), ((), ())), preferred_element_type=F32)


def _dot_tn(a, b):
    return lax.dot_general(a, b, (((0,), (0,)), ((), ())), preferred_element_type=F32)


def _split_dot(f_cat, x, passes):
    xh = x.astype(BF16)
    if passes == 1:
        return _dot(f_cat, xh)
    m = f_cat.shape[0] // 2
    r = _dot(f_cat, xh)
    r = r[:m] + r[m:]
    if passes == 3:
        r = r + _dot(f_cat[:m], (x - xh.astype(F32)).astype(BF16))
    return r


def _dft_operand(f_cat, passes):
    return f_cat[..., :f_cat.shape[-2] // 2, :] if passes == 1 else f_cat


def _modulate(x, mod_ref, shift_row, scale_row):
    parts = []
    for s in range(x.shape[0] // SUB):
        sh = mod_ref[s, shift_row:shift_row + 1, :]
        sc = mod_ref[s, scale_row:scale_row + 1, :]
        parts.append(x[s * SUB:(s + 1) * SUB] * (1.0 + sc) + sh)
    return parts[0] if len(parts) == 1 else jnp.concatenate(parts, axis=0)


def _gated_residual_ln(x, f, mod_ref, gate_row, alpha, g, b):
    parts = []
    for s in range(x.shape[0] // SUB):
        gt = mod_ref[s, gate_row:gate_row + 1, :]
        parts.append(alpha * x[s * SUB:(s + 1) * SUB] + gt * f[s * SUB:(s + 1) * SUB])
    r = parts[0] if len(parts) == 1 else jnp.concatenate(parts, axis=0)
    mu = jnp.mean(r, axis=-1, keepdims=True)
    rc = r - mu
    var = jnp.mean(rc * rc, axis=-1, keepdims=True)
    return rc * lax.rsqrt(var + LN_EPS) * g + b


def _mod_kernel(a_ref, w_ref, b_ref, o_ref):
    o_ref[0] = _dot(a_ref[...].astype(BF16), w_ref[0].astype(BF16)) + b_ref[0]


def _mod_all(act, w_mod, b_mod):
    depth, d, n = w_mod.shape
    tn = 512
    return pl.pallas_call(
        _mod_kernel,
        grid=(depth, n // tn),
        in_specs=[pl.BlockSpec((8, d), lambda l, j: (0, 0)),
                  pl.BlockSpec((1, d, tn), lambda l, j: (l, 0, j)),
                  pl.BlockSpec((1, 1, tn), lambda l, j: (l, 0, j))],
        out_specs=pl.BlockSpec((1, 8, tn), lambda l, j: (l, 0, j)),
        out_shape=jax.ShapeDtypeStruct((depth, 8, n), F32),
        compiler_params=_cp(("parallel", "parallel")),
        name="mod_vectors",
    )(act, w_mod, b_mod.reshape(depth, 1, n))


PROJ_TN = 512
assert (A_W, R_W, H_W, D_W) == (3 * PROJ_TN, 4 * PROJ_TN, 3 * PROJ_TN, 2 * PROJ_TN)


def _proj_kernel(x_ref, mod_ref, w_ref, cos_ref, sin_ref, gq_ref, gk_ref,
                 za_ref, zvg_ref, zh_ref, rq_ref, rk_ref, aq_ref, ak_ref, av_ref):
    u = _modulate(x_ref[...], mod_ref, 0, 1).astype(BF16)
    tn = PROJ_TN
    cos, sin = cos_ref[...], sin_ref[...]
    lane = lax.broadcasted_iota(jnp.int32, cos.shape, 1)
    first_half = (lane % 64) < 32

    def tile(j):
        return _dot(u, w_ref[:, j * tn:(j + 1) * tn])

    def rope(x):
        partner = jnp.where(first_half, pltpu.roll(x, 96, 1), pltpu.roll(x, 32, 1))
        return x * cos + partner * sin

    def rms(x, g):
        return x * lax.rsqrt(jnp.mean(x * x, axis=-1, keepdims=True) + LN_EPS) * g

    head = lambda h: slice(h * LANE, (h + 1) * LANE)

    for j in range(3):
        za_ref[:, j * tn:(j + 1) * tn] = tile(j)
    r = tile(3)
    for h in range(RET_HEADS):
        rq_ref[:, head(h)] = rope(r[:, head(h)])
    r = tile(4)
    for h in range(RET_HEADS):
        rk_ref[:, head(h)] = rope(r[:, head(h)] * (RET_D ** -0.5))
    for j in range(2):
        zvg_ref[:, j * tn:(j + 1) * tn] = tile(5 + j)
    for j in range(3):
        zh_ref[:, j * tn:(j + 1) * tn] = tile(7 + j)
    q_scale = (HEAD_DIM ** -0.5) * math.log2(math.e)
    r = tile(10)
    for h in range(ATT_HEADS):
        aq_ref[:, head(h)] = (rope(rms(r[:, head(h)], gq_ref[...])) * q_scale).astype(BF16)
    r = tile(11)
    ones_col = jnp.where(lane == 0, 1.0, 0.0).astype(BF16)
    for h in range(ATT_KV_HEADS):
        ak_ref[:, head(h)] = rope(rms(r[:, head(h)], gk_ref[...])).astype(BF16)
        av_ref[:, head(2 * h)] = r[:, head(ATT_KV_HEADS + h)].astype(BF16)
        av_ref[:, head(2 * h + 1)] = ones_col


def _proj(x, modx, w, layer, cos_t, sin_t, gq, gk, tm):
    m, d = x.shape
    n = w.shape[2]
    tn = PROJ_TN
    qw, kw = ATT_HEADS * HEAD_DIM, ATT_KV_HEADS * HEAD_DIM
    row = lambda i: (i, 0)
    widths = (A_W, 2 * tn, H_W, tn, tn, qw, kw, 2 * kw)
    dtypes = (F32,) * 5 + (BF16,) * 3
    return pl.pallas_call(
        _proj_kernel,
        grid=(m // tm,),
        in_specs=[pl.BlockSpec((tm, d), row),
                  pl.BlockSpec((tm // SUB, 8, d), lambda i: (i, 0, 0)),
                  pl.BlockSpec((None, d, n), lambda i: (layer, 0, 0), pipeline_mode=pl.Buffered(1)),
                  pl.BlockSpec((tm, LANE), row), pl.BlockSpec((tm, LANE), row),
                  pl.BlockSpec((1, LANE), lambda i: (0, 0)), pl.BlockSpec((1, LANE), lambda i: (0, 0))],
        out_specs=[pl.BlockSpec((tm, wd), row) for wd in widths],
        out_shape=[jax.ShapeDtypeStruct((m, wd), dt) for wd, dt in zip(widths, dtypes)],
        compiler_params=_cp(("parallel",)),
        name="in_proj",
    )(x, modx, w, cos_t, sin_t, gq.reshape(1, LANE), gk.reshape(1, LANE))


def _attn_kernel(q_ref, k_ref, v_ref, *rest, key_lo, key_hi, kc):
    o_ref = rest[-1]
    tq = q_ref.shape[1]
    grp = ATT_HEADS // ATT_KV_HEADS
    q2 = jnp.concatenate([q_ref[0, :, g * HEAD_DIM:(g + 1) * HEAD_DIM] for g in range(grp)], axis=0)
    m_run, acc = None, None
    for lo in range(key_lo, key_hi, kc):
        hi = min(lo + kc, key_hi)
        s = _dot_nt(q2, k_ref[0, lo:hi, :])
        m_chunk = jnp.max(s, axis=-1, keepdims=True)
        m_new = m_chunk if m_run is None else jnp.maximum(m_run, m_chunk)
        pv = _dot(jnp.exp2(s - m_new).astype(BF16), v_ref[0, lo:hi, :])
        acc = pv if acc is None else jnp.exp2(m_run - m_new) * acc + pv
        m_run = m_new
    for g in range(grp):
        a = acc[g * tq:(g + 1) * tq]
        o_ref[0, :, g * HEAD_DIM:(g + 1) * HEAD_DIM] = a[:, :HEAD_DIM] / a[:, HEAD_DIM:HEAD_DIM + 1]


def _attention(aq, ak, av, nb, lat_len):
    m = aq.shape[0]
    t = m // nb
    lc = t - lat_len
    grp_w = (ATT_HEADS // ATT_KV_HEADS) * HEAD_DIM
    args = (aq.reshape(nb, t, -1), ak.reshape(nb, t, -1), av.reshape(nb, t, -1))

    def call(tq, n_tiles, first_tile, key_lo, key_hi, prev):
        qmap = lambda b, kv, i: (b, first_tile + i, kv)
        specs = [pl.BlockSpec((1, tq, grp_w), qmap),
                 pl.BlockSpec((1, t, HEAD_DIM), lambda b, kv, i: (b, 0, kv)),
                 pl.BlockSpec((1, t, 2 * HEAD_DIM), lambda b, kv, i: (b, 0, kv))]
        extra, alias = (), {}
        if prev is not None:
            specs.append(pl.BlockSpec(memory_space=pl.ANY))
            extra, alias = (prev,), {3: 0}
        return pl.pallas_call(
            functools.partial(_attn_kernel, key_lo=key_lo, key_hi=key_hi, kc=ATT_KC),
            grid=(nb, ATT_KV_HEADS, n_tiles),
            in_specs=specs,
            out_specs=pl.BlockSpec((1, tq, grp_w), qmap),
            out_shape=jax.ShapeDtypeStruct((nb, t, ATT_HEADS * HEAD_DIM), F32),
            input_output_aliases=alias,
            compiler_params=_cp(("parallel", "parallel", "arbitrary")),
            name="gqa_attention",
        )(*args, *extra)

    tq = math.gcd(ATT_TQ, lat_len)
    out = call(tq, lat_len // tq, 0, 0, t, None)
    out = call(lc, 1, t // lc - 1, lat_len, t, out)
    return out.reshape(m, -1)


def _ret_kernel(q_ref, k_ref, v_ref, g_ref, intra_ref, qd_ref, kd_ref, cd_ref, o_ref, st_ref,
                *, n_lat, n_ctx):
    c_len = RET_CHUNK
    n = n_lat + n_ctx

    def rows(c):
        return pl.ds(pl.multiple_of(c * c_len, c_len), c_len)

    def kv(d, c):
        return _dot_tn((k_ref[0, rows(c), :] * kd_ref[d, 0]).astype(BF16), v_ref[0, rows(c), :].astype(BF16))

    def scan_step(t, carry):
        sf, sb = carry
        cf = jnp.where(t < n_ctx, n_lat + t, t - n_ctx)
        cb = jnp.where(t < n_ctx, n - 1 - t, n_lat - 1 - (t - n_ctx))
        st_ref[cf, :RET_D, :] = sf.astype(BF16)
        st_ref[cb, RET_D:, :] = sb.astype(BF16)
        return (cd_ref[0, 0, 0:1, :] * sf + kv(0, cf), cd_ref[1, 0, 0:1, :] * sb + kv(1, cb))

    zero = jnp.zeros((RET_D, RET_D), F32)
    lax.fori_loop(0, n, scan_step, (zero, zero), unroll=RET_UNROLL)

    mask = intra_ref[0, 0] + intra_ref[1, 0]

    def out_step(c, carry):
        qc = q_ref[0, rows(c), :]
        vc = v_ref[0, rows(c), :].astype(BF16)
        s = _dot_nt(qc.astype(BF16), k_ref[0, rows(c), :].astype(BF16)) * mask
        qdec = jnp.concatenate([qc * qd_ref[0, 0], qc * qd_ref[1, 0]], axis=1).astype(BF16)
        o = _dot(s.astype(BF16), vc) + _dot(qdec, st_ref[c])
        mu = jnp.mean(o, axis=-1, keepdims=True)
        oc = o - mu
        var = jnp.mean(oc * oc, axis=-1, keepdims=True)
        g = g_ref[0, rows(c), :]
        o_ref[0, rows(c), :] = (g * jax.nn.sigmoid(g)) * (oc * lax.rsqrt(var + LN_EPS))
        return carry

    lax.fori_loop(0, n, out_step, 0, unroll=RET_UNROLL)


def _retention(rq, rk, z, tabs, nb, lat_len):
    m = rq.shape[0]
    t = m // nb
    intra, qd, kd, cd = tabs
    v_blk, g_blk = 0, RET_HEADS
    kern = functools.partial(_ret_kernel, n_lat=lat_len // RET_CHUNK, n_ctx=(t - lat_len) // RET_CHUNK)
    seq = lambda off: pl.BlockSpec((1, t, LANE), lambda b, h: (b, 0, off + h))
    tab = lambda r: pl.BlockSpec((2, 1, r, LANE), lambda b, h: (0, h, 0, 0))
    out = pl.pallas_call(
        kern,
        grid=(nb, RET_HEADS),
        in_specs=[seq(0), seq(0), seq(v_blk), seq(g_blk),
                  tab(RET_CHUNK), tab(RET_CHUNK), tab(RET_CHUNK), tab(8)],
        out_specs=seq(0),
        out_shape=jax.ShapeDtypeStruct((nb, t, RET_HEADS * RET_D), F32),
        scratch_shapes=[pltpu.VMEM((t // RET_CHUNK, 2 * RET_D, RET_D), BF16)],
        compiler_params=_cp(("parallel", "parallel")),
        name="retention",
    )(rq.reshape(nb, t, -1), rk.reshape(nb, t, -1), z.reshape(nb, t, -1), z.reshape(nb, t, -1), intra, qd, kd, cd)
    return out.reshape(m, -1)


def _retention_tables(log_gamma):
    c = RET_CHUNK
    i = jnp.arange(c, dtype=F32)
    lg = log_gamma.astype(F32)[:, :, None]
    rel = i[:, None] - i[None, :]
    intra_f = jnp.where(rel >= 0, jnp.exp(lg[0][..., None] * jnp.maximum(rel, 0.0)), 0.0)
    intra_b = jnp.where(rel <= 0, jnp.exp(lg[1][..., None] * jnp.maximum(-rel, 0.0)), 0.0)
    qd = jnp.stack([jnp.exp(lg[0] * (i + 1.0)), jnp.exp(lg[1] * (c - i))])
    kd = jnp.stack([jnp.exp(lg[0] * (c - 1.0 - i)), jnp.exp(lg[1] * i)])
    cd = jnp.exp(lg * c)
    bc = lambda a: jnp.broadcast_to(a[..., None], a.shape + (LANE,))
    return (jnp.stack([intra_f, intra_b]), bc(qd), bc(kd),
            jnp.broadcast_to(cd[..., None], (2, RET_HEADS, 8, LANE)))


def _shift_rows(x, lat_len):
    t = x.shape[0]
    row = lax.broadcasted_iota(jnp.int32, x.shape, 0)
    xm = jnp.where((row == 0) | (row == lat_len), 0.0, pltpu.roll(x, 1, 0))
    xp = jnp.where((row == lat_len - 1) | (row == t - 1), 0.0, pltpu.roll(x, t - 1, 0))
    return xm, xp


def _shortconv_kernel(bg_ref, cg_ref, xv_ref, w_ref, b_ref, o_ref, *, lat_len):
    p = cg_ref[0] * xv_ref[0]
    pm, pp = _shift_rows(p, lat_len)
    o_ref[0] = bg_ref[0] * (pm * w_ref[0:1, :] + p * w_ref[1:2, :] + pp * w_ref[2:3, :] + b_ref[...])


def _shortconv(z, w, b, nb, lat_len):
    m = z.shape[0]
    t = m // nb
    nct = BRANCH_W // LANE
    kern = functools.partial(_shortconv_kernel, lat_len=lat_len)
    seq = lambda off: pl.BlockSpec((1, t, LANE), lambda bb, j: (bb, 0, off + j))
    out = pl.pallas_call(
        kern,
        grid=(nb, nct),
        in_specs=[seq(0), seq(nct), seq(2 * nct),
                  pl.BlockSpec((3, LANE), lambda bb, j: (0, j)),
                  pl.BlockSpec((1, LANE), lambda bb, j: (0, j))],
        out_specs=seq(0),
        out_shape=jax.ShapeDtypeStruct((nb, t, BRANCH_W), F32),
        compiler_params=_cp(("parallel", "parallel")),
        name="short_conv",
    )(z.reshape(nb, t, -1), z.reshape(nb, t, -1), z.reshape(nb, t, -1), w, b.reshape(1, -1))
    return out.reshape(m, -1)


def _hyconv_kernel(z_ref, w_ref, b_ref, ol_ref, oc_ref, *, lat_len):
    x = z_ref[0]
    xm, xp = _shift_rows(x, lat_len)
    y = xm * w_ref[0:1, :] + x * w_ref[1:2, :] + xp * w_ref[2:3, :] + b_ref[...]
    ol_ref[0, 0] = y[:lat_len]
    oc_ref[0, 0] = y[lat_len:]


def _hyconv(z, w, b, nb, lat_len):
    m = z.shape[0]
    t = m // nb
    nct = BRANCH_W // LANE
    off = 0
    kern = functools.partial(_hyconv_kernel, lat_len=lat_len)
    return pl.pallas_call(
        kern,
        grid=(nb, 3 * nct),
        in_specs=[pl.BlockSpec((1, t, LANE), lambda bb, j: (bb, 0, off + j)),
                  pl.BlockSpec((3, LANE), lambda bb, j: (0, j)),
                  pl.BlockSpec((1, LANE), lambda bb, j: (0, j))],
        out_specs=[pl.BlockSpec((1, 1, lat_len, LANE), lambda bb, j: (j // nct, bb, 0, j % nct)),
                   pl.BlockSpec((1, 1, t - lat_len, LANE), lambda bb, j: (j // nct, bb, 0, j % nct))],
        out_shape=[jax.ShapeDtypeStruct((3, nb, lat_len, BRANCH_W), F32),
                   jax.ShapeDtypeStruct((3, nb, t - lat_len, BRANCH_W), F32)],
        compiler_params=_cp(("parallel", "parallel")),
        name="hyena_dwconv",
    )(z.reshape(nb, t, -1), w, b.reshape(1, -1))


def _stack_c(m):
    return np.block([[m.real, -m.imag], [m.imag, m.real]])


def _split_np(m):
    m32 = np.asarray(m, np.float32)
    hi = m32.astype(BF16)
    lo = (m32 - hi.astype(np.float32)).astype(BF16)
    return np.concatenate([hi, lo], axis=-2)


@functools.lru_cache(maxsize=None)
def _dft_tables(lat_len):
    n = 2 * lat_len
    n1, n2 = FFT_N1, 2 * lat_len // FFT_N1
    a1 = np.arange(n1)
    a2 = np.arange(n2)
    w1 = np.exp(-2j * np.pi * np.outer(a1, a1[:n1 // 2]) / n1)
    f1 = _stack_c(w1)
    w2 = np.exp(-2j * np.pi * np.outer(a2, a2) / n2)
    tw = np.exp(-2j * np.pi * np.outer(a1, a2) / n)
    cf = w2[None, :, :] * tw[:, None, :]
    f2 = np.stack([_stack_c(cf[k]) for k in range(n1)])
    g2 = np.stack([_stack_c(cf[k].conj().T) for k in range(n1)])
    w1i = np.exp(2j * np.pi * np.outer(a1[:n1 // 2], a1) / n1) / n
    f3 = _stack_c(w1i)
    f1_real = f1[:, :n1 // 2]
    return _split_np(f1), _split_np(f1_real), _split_np(f2), _split_np(g2), _split_np(f3)


@functools.lru_cache(maxsize=None)
def _dft_tables_small(ctx_len):
    n = 2 * ctx_len
    a = np.arange(n)
    ff = _stack_c(np.exp(-2j * np.pi * np.outer(a, a[:ctx_len]) / n))
    fi = _stack_c(np.exp(2j * np.pi * np.outer(a[:ctx_len], a) / n) / n)
    return _split_np(ff), _split_np(ff[:, :ctx_len]), _split_np(fi)


def _fft1_kernel(x_ref, f_ref, o_ref, *, passes):
    f = f_ref[...]
    for j in range(x_ref.shape[3]):
        parts = [x_ref[0, b, :, j, :] for b in range(x_ref.shape[1])]
        r = _split_dot(f, parts[0] if len(parts) == 1 else jnp.concatenate(parts, axis=0), passes)
        h = r.shape[0] // 2
        o_ref[0, 0, :, j, :] = r[:h]
        o_ref[0, 1, :, j, :] = r[h:]


def _fft1(x, sel, f1, jb, passes):
    _, p, mem, r, n2, w = x.shape
    f1 = _dft_operand(f1, passes)
    n1 = f1.shape[0] // (2 if passes == 1 else 4)
    return pl.pallas_call(
        functools.partial(_fft1_kernel, passes=passes),
        grid=(p, n2 // jb),
        in_specs=[pl.BlockSpec((None, 1, mem, r, jb, w), lambda pp, j: (sel, pp, 0, 0, j, 0)),
                  pl.BlockSpec(f1.shape, lambda pp, j: (0, 0))],
        out_specs=pl.BlockSpec((1, 2, n1, jb, w), lambda pp, j: (pp, 0, 0, j, 0)),
        out_shape=jax.ShapeDtypeStruct((p, 2, n1, n2, w), F32),
        compiler_params=_cp(("parallel", "parallel")),
        name="hyena_fft_stage1",
    )(x, f1)


def _fft2f_kernel(a_ref, f_ref, o_ref, *, kb, n2, passes):
    w = a_ref.shape[-1]
    for t in range(kb):
        xf = _split_dot(f_ref[t], a_ref[0, :, t].reshape(2 * n2, w), passes)
        xb = _split_dot(f_ref[t], a_ref[1, :, t].reshape(2 * n2, w), passes)
        o_ref[0, 0, t] = xf[:n2] + xb[:n2]
        o_ref[0, 1, t] = xf[n2:] - xb[n2:]


def _fft2_filters(a, f2, kb, passes):
    q, _, n1, n2, w = a.shape
    f2 = _dft_operand(f2, passes)
    return pl.pallas_call(
        functools.partial(_fft2f_kernel, kb=kb, n2=n2, passes=passes),
        grid=(q // 2, n1 // kb),
        in_specs=[pl.BlockSpec((2, 2, kb, n2, w), lambda o, k: (o, 0, k, 0, 0)),
                  pl.BlockSpec((kb,) + f2.shape[1:], lambda o, k: (k, 0, 0))],
        out_specs=pl.BlockSpec((1, 2, kb, n2, w), lambda o, k: (o, 0, k, 0, 0)),
        out_shape=jax.ShapeDtypeStruct((q // 2, 2, n1, n2, w), F32),
        compiler_params=_cp(("parallel", "parallel")),
        name="hyena_filter_spectrum",
    )(a, f2)


def _fft2_kernel(a_ref, h_ref, f_ref, g_ref, o_ref, *, kb, n2, passes):
    w = a_ref.shape[-1]
    for t in range(kb):
        a = a_ref[0, :, t].reshape(2 * n2, w)
        x = _split_dot(f_ref[t], a, passes)
        xr, xi = x[:n2], x[n2:]
        hr, hi = h_ref[0, t], h_ref[1, t]
        y = jnp.concatenate([xr * hr - xi * hi, xr * hi + xi * hr], axis=0)
        o_ref[0, :, t] = _split_dot(g_ref[t], y, passes).reshape(2, n2, w)


def _fft2(a, hspec, order, f2, g2, kb, passes):
    p, _, n1, n2, w = a.shape
    f2, g2 = _dft_operand(f2, passes), _dft_operand(g2, passes)
    kern = functools.partial(_fft2_kernel, kb=kb, n2=n2, passes=passes)
    return pl.pallas_call(
        kern,
        grid=(n1 // kb, p),
        in_specs=[pl.BlockSpec((1, 2, kb, n2, w), lambda k, pp: (pp, 0, k, 0, 0)),
                  pl.BlockSpec((None, 2, kb, n2, w), lambda k, pp: (order, 0, k, 0, 0)),
                  pl.BlockSpec((kb,) + f2.shape[1:], lambda k, pp: (k, 0, 0)),
                  pl.BlockSpec((kb,) + g2.shape[1:], lambda k, pp: (k, 0, 0))],
        out_specs=pl.BlockSpec((1, 2, kb, n2, w), lambda k, pp: (pp, 0, k, 0, 0)),
        out_shape=jax.ShapeDtypeStruct(a.shape, F32),
        compiler_params=_cp(("parallel", "arbitrary")),
        name="hyena_fft_stage2",
    )(a, hspec, f2, g2)


def _fft3_kernel(b_ref, f_ref, gate_ref, prev_ref, skip_ref, o_ref, *, passes):
    f = f_ref[...]
    r = gate_ref.shape[2]
    for j in range(b_ref.shape[3]):
        bj = jnp.concatenate([b_ref[0, 0, :, j, :], b_ref[0, 1, :, j, :]], axis=0)
        conv = _split_dot(f, bj, passes)
        for mem in range(2):
            cm = conv[mem * r:(mem + 1) * r]
            o_ref[0, mem, :, j, :] = gate_ref[0, mem, :, j, :] * (cm + skip_ref[...] * prev_ref[0, mem, :, j, :])


def _fft3(bm, f3, gate, gate_sel, prev, prev_sel, skip, out_rows, jb, passes):
    p, _, n1, n2, w = bm.shape
    r = n1 // 2
    f3 = _dft_operand(f3, passes)
    seq = lambda s: pl.BlockSpec((None, 1, 2, r, jb, w), lambda pp, j: (s, pp, 0, 0, j, 0))
    return pl.pallas_call(
        functools.partial(_fft3_kernel, passes=passes),
        grid=(p, n2 // jb),
        in_specs=[pl.BlockSpec((1, 2, n1, jb, w), lambda pp, j: (pp, 0, 0, j, 0)),
                  pl.BlockSpec(f3.shape, lambda pp, j: (0, 0)),
                  seq(gate_sel), seq(prev_sel),
                  pl.BlockSpec((1, w), lambda pp, j: (0, 0))],
        out_specs=pl.BlockSpec((1, 2, r, jb, w), lambda pp, j: (pp, 0, 0, j, 0)),
        out_shape=jax.ShapeDtypeStruct((p, 2, out_rows, n2, w), F32),
        compiler_params=_cp(("parallel", "parallel")),
        name="hyena_fft_stage3",
    )(bm, f3, gate, prev, skip)


def _hyctx_kernel(v_ref, x1_ref, x2_ref, filt_ref, ff_ref, ffr_ref, fi_ref, skip_ref, yh_hbm_ref, o_ref,
                  *, passes, filter_passes):
    del yh_hbm_ref
    lc, w = v_ref.shape[2], v_ref.shape[3]
    n = 2 * lc
    y = v_ref[0].reshape(2 * lc, w)
    for o, gate_ref in enumerate((x1_ref, x2_ref)):
        sf = _split_dot(ffr_ref[...], filt_ref[o, 0], filter_passes)
        sb = _split_dot(ffr_ref[...], filt_ref[o, 1], filter_passes)
        hr, hi = sf[:n] + sb[:n], sf[n:] - sb[n:]
        x = _split_dot(ff_ref[...], y, passes)
        xr, xi = x[:n], x[n:]
        spec = jnp.concatenate([xr * hr - xi * hi, xr * hi + xi * hr], axis=0)
        conv = _split_dot(fi_ref[...], spec, passes)
        y = gate_ref[0].reshape(2 * lc, w) * (conv + skip_ref[o:o + 1, :] * y)
    o_ref[0] = y.reshape(2, lc, w)


def _hyena_ctx(hc, filt_c, ff, ffr, fi, skip, yh, passes, filter_passes):
    _, nb, lc, w = hc.shape
    p = nb // 2
    t = yh.shape[2]
    ff, fi, ffr = _dft_operand(ff, passes), _dft_operand(fi, passes), _dft_operand(ffr, filter_passes)
    part = pl.BlockSpec((1, 2, lc, w), lambda pp: (pp, 0, 0, 0))
    hc4 = hc.reshape(3, p, 2, lc, w)
    return pl.pallas_call(
        functools.partial(_hyctx_kernel, passes=passes, filter_passes=filter_passes),
        grid=(p,),
        in_specs=[part, part, part,
                  pl.BlockSpec(filt_c.shape, lambda pp: (0, 0, 0, 0)),
                  pl.BlockSpec(ff.shape, lambda pp: (0, 0)),
                  pl.BlockSpec(ffr.shape, lambda pp: (0, 0)),
                  pl.BlockSpec(fi.shape, lambda pp: (0, 0)),
                  pl.BlockSpec(skip.shape, lambda pp: (0, 0)),
                  pl.BlockSpec(memory_space=pl.ANY)],
        out_specs=pl.BlockSpec((1, 2, lc, w), lambda pp: (pp, 0, t // lc - 1, 0)),
        out_shape=jax.ShapeDtypeStruct(yh.shape, F32),
        input_output_aliases={8: 0},
        compiler_params=_cp(("parallel",)),
        name="hyena_ctx",
    )(hc4[0], hc4[1], hc4[2], filt_c, ff, ffr, fi, skip, yh)


def _hyena_filters(l, w1, b1, w2, b2, w3, freq):
    t = jnp.linspace(0.0, 1.0, l, dtype=F32)[:, None]
    bands = (HY_EMB - 1) // 2
    w = 2.0 * math.pi * jnp.arange(l, dtype=F32) / l
    f = jnp.linspace(1e-4, bands - 1, bands, dtype=F32)
    ang = w[:, None] * f[None, :]
    feats = jnp.concatenate([t, jnp.cos(ang), -jnp.sin(ang)], axis=-1)
    hdn = jnp.sin(freq[0] * (feats @ w1 + b1))
    hdn = jnp.sin(freq[1] * (hdn @ w2 + b2))
    n_out = w3.shape[-1]
    deltas = jnp.linspace(math.log(HY_TARGET) / HY_SLOW_PCT, math.log(HY_TARGET) / HY_FAST_PCT, n_out, dtype=F32)
    decay = jnp.exp(-t * jnp.abs(deltas))
    pos = jnp.arange(l)[:, None]
    parts = []
    for q in range(n_out // BRANCH_W):
        sl = slice(q * BRANCH_W, (q + 1) * BRANCH_W)
        f = (hdn @ w3[:, sl]).astype(F32) * decay[:, sl]
        parts.append(jnp.where(pos == 0, 0.0, f) if q % 2 else f)
    return jnp.stack(parts).reshape(n_out // (2 * BRANCH_W), 2, l, BRANCH_W)


def _hyena(z, conv_w, conv_b, filt_w, skip, nb, lat_len):
    m = z.shape[0]
    t = m // nb
    lc = t - lat_len
    w = BRANCH_W
    p = nb // 2
    n1, n2 = FFT_N1, 2 * lat_len // FFT_N1
    r = n1 // 2
    hl, hc = _hyconv(z, conv_w, conv_b, nb, lat_len)
    f1, f1r, f2, g2, f3 = _dft_tables(lat_len)
    ff, ffr, fi = _dft_tables_small(lc)
    jb, kb = min(16, n2), 16
    dp, fp = DFT_PASSES, DFT_FILTER_PASSES
    filt_l = _hyena_filters(lat_len, *filt_w)
    fa = _fft1(filt_l.reshape(1, 4, 1, r, n2, w), 0, f1r, jb, fp)
    hs_l = _fft2_filters(fa, f2, kb, fp)
    hl6 = hl.reshape(3, p, 2, r, n2, w)
    y, y_sel = hl6, 0
    for o in range(2):
        a = _fft1(y, y_sel, f1, jb, dp)
        bm = _fft2(a, hs_l, o, f2, g2, kb, dp)
        rows = r if o == 0 else t // n2
        y = _fft3(bm, f3, hl6, 1 + o, y, y_sel, skip[o:o + 1], rows, jb, dp)[None]
        y_sel = 0
    yh = y.reshape(p, 2, t, w)
    yh = _hyena_ctx(hc, _hyena_filters(lc, *filt_w), ff, ffr, fi, skip, yh, dp, fp)
    return yh.reshape(m, w)


def _merge_kernel(x_ref, mod_ref, ya_ref, yr_ref, yh_ref, yd_ref, wg_ref, bg_ref, wb_ref, wo_ref, g_ref, b_ref,
                  o_ref, *, alpha):
    x = x_ref[...]
    u = _modulate(x, mod_ref, 0, 1).astype(BF16)
    acc = None
    for i, y_ref in enumerate((ya_ref, yr_ref, yh_ref, yd_ref)):
        gate = jax.nn.sigmoid(_dot(u, wg_ref[i]) + bg_ref[i:i + 1, :])
        term = gate * _dot(y_ref[...].astype(BF16), wb_ref[i])
        acc = term if acc is None else acc + term
    mix = _dot(acc.astype(BF16), wo_ref[...])
    o_ref[...] = _gated_residual_ln(x, mix, mod_ref, 2, alpha, g_ref[...], b_ref[...])


def _merge(x, modx, ys, wg, bg, wb, wo, layer, g, b, alpha, tm):
    m, d = x.shape
    w = BRANCH_W
    row = lambda i: (i, 0)
    const2 = lambda i: (0, 0)
    one = pl.Buffered(1)
    stacked = lambda a: pl.BlockSpec((None,) + a.shape[1:], lambda i: (layer,) + (0,) * (a.ndim - 1),
                                     pipeline_mode=one)
    return pl.pallas_call(
        functools.partial(_merge_kernel, alpha=alpha),
        grid=(m // tm,),
        in_specs=[pl.BlockSpec((tm, d), row),
                  pl.BlockSpec((tm // SUB, 8, d), lambda i: (i, 0, 0)),
                  pl.BlockSpec((tm, w), row), pl.BlockSpec((tm, w), row),
                  pl.BlockSpec((tm, w), row), pl.BlockSpec((tm, w), row),
                  stacked(wg), pl.BlockSpec(bg.shape, const2), stacked(wb), stacked(wo),
                  pl.BlockSpec((1, d), const2), pl.BlockSpec((1, d), const2)],
        out_specs=pl.BlockSpec((tm, d), row),
        out_shape=jax.ShapeDtypeStruct((m, d), F32),
        compiler_params=_cp(("parallel",)),
        name="branch_merge",
    )(x, modx, *ys, wg, bg, wb, wo, g.reshape(1, d), b.reshape(1, d))


def _ffn_kernel(x_ref, mod_ref, w1_ref, w3_ref, w2_ref, g_ref, b_ref, o_ref, *, alpha, tf):
    x = x_ref[...]
    u = _modulate(x, mod_ref, 3, 4).astype(BF16)
    acc = None
    for j in range(w1_ref.shape[1] // tf):
        sl = slice(j * tf, (j + 1) * tf)
        h1 = _dot(u, w1_ref[:, sl])
        h3 = _dot(u, w3_ref[:, sl])
        part = _dot((h1 * jax.nn.sigmoid(h1) * h3).astype(BF16), w2_ref[sl, :])
        acc = part if acc is None else acc + part
    o_ref[...] = _gated_residual_ln(x, acc, mod_ref, 5, alpha, g_ref[...], b_ref[...])


def _ffn(x, modx, w1, w3, w2, layer, g, b, alpha, tm, tf):
    m, d = x.shape
    resident = lambda a: pl.BlockSpec((None,) + a.shape[1:], lambda i: (layer, 0, 0), pipeline_mode=pl.Buffered(1))
    return pl.pallas_call(
        functools.partial(_ffn_kernel, alpha=alpha, tf=tf),
        grid=(m // tm,),
        in_specs=[pl.BlockSpec((tm, d), lambda i: (i, 0)),
                  pl.BlockSpec((tm // SUB, 8, d), lambda i: (i, 0, 0)),
                  resident(w1), resident(w3), resident(w2),
                  pl.BlockSpec((1, d), lambda i: (0, 0)),
                  pl.BlockSpec((1, d), lambda i: (0, 0))],
        out_specs=pl.BlockSpec((tm, d), lambda i: (i, 0)),
        out_shape=jax.ShapeDtypeStruct((m, d), F32),
        compiler_params=_cp(("parallel",)),
        name="dense_swiglu",
    )(x, modx, w1, w3, w2, g.reshape(1, d), b.reshape(1, d))


def _route_kernel(x_ref, mod_ref, wr_ref, u_ref, lg_ref):
    u = _modulate(x_ref[...], mod_ref, 3, 4)
    lg_ref[...] = jnp.dot(u, wr_ref[...], preferred_element_type=F32, precision=lax.Precision.HIGHEST)
    half = u.shape[1] // 2
    bits = lax.bitcast_convert_type(u.astype(BF16).astype(F32), jnp.uint32)
    u_ref[...] = bits[:, :half] | (bits[:, half:] >> 16)


def _route(x, modx, wr_pad, tm):
    m, d = x.shape
    return pl.pallas_call(
        _route_kernel,
        grid=(m // tm,),
        in_specs=[pl.BlockSpec((tm, d), lambda i: (i, 0)),
                  pl.BlockSpec((tm // SUB, 8, d), lambda i: (i, 0, 0)),
                  pl.BlockSpec((d, LANE), lambda i: (0, 0))],
        out_specs=[pl.BlockSpec((tm, d // 2), lambda i: (i, 0)), pl.BlockSpec((tm, LANE), lambda i: (i, 0))],
        out_shape=[jax.ShapeDtypeStruct((m, d // 2), jnp.uint32), jax.ShapeDtypeStruct((m, LANE), F32)],
        compiler_params=_cp(("parallel",)),
        name="moe_router",
    )(x, modx, wr_pad)


def _moe_kernel(be_ref, nu_ref, xb_ref, w1_ref, w3_ref, w2_ref, o_ref, acc_ref, u_ref):
    i, j = pl.program_id(0), pl.program_id(1)
    live = i < nu_ref[0]

    @pl.when(j == 0)
    def _():
        acc_ref[...] = jnp.zeros_like(acc_ref)
        words = xb_ref[...]
        half = words.shape[1]
        hi = lax.bitcast_convert_type(words & jnp.uint32(0xFFFF0000), F32)
        lo = lax.bitcast_convert_type(words << 16, F32)
        u_ref[:, :half] = hi.astype(BF16)
        u_ref[:, half:] = lo.astype(BF16)

    @pl.when(live)
    def _():
        u = u_ref[...]
        h1 = _dot(u, w1_ref[0])
        h3 = _dot(u, w3_ref[0])
        acc_ref[...] += _dot((h1 * jax.nn.sigmoid(h1) * h3).astype(BF16), w2_ref[0])

    @pl.when(j == pl.num_programs(1) - 1)
    def _():
        o_ref[...] = acc_ref[...]


def _moe_experts(xb, blk_e, n_used, w1, w3, w2, layer, tm, tf):
    r = xb.shape[0]
    d = w1.shape[2]
    ff = w1.shape[3]
    nj = ff // tf

    def wmap(i, j, be, nu):
        return (layer, be[i], 0, jnp.where(i < nu[0], j, nj - 1))

    def w2map(i, j, be, nu):
        return (layer, be[i], jnp.where(i < nu[0], j, nj - 1), 0)

    grid_spec = pltpu.PrefetchScalarGridSpec(
        num_scalar_prefetch=2,
        grid=(r // tm, nj),
        in_specs=[pl.BlockSpec((tm, d // 2), lambda i, j, be, nu: (i, 0)),
                  pl.BlockSpec((None, 1, d, tf), wmap),
                  pl.BlockSpec((None, 1, d, tf), wmap),
                  pl.BlockSpec((None, 1, tf, d), w2map)],
        out_specs=pl.BlockSpec((tm, d), lambda i, j, be, nu: (i, 0)),
        scratch_shapes=[pltpu.VMEM((tm, d), F32), pltpu.VMEM((tm, d), BF16)],
    )
    return pl.pallas_call(
        _moe_kernel,
        grid_spec=grid_spec,
        out_shape=jax.ShapeDtypeStruct((r, d), F32),
        compiler_params=_cp(("arbitrary", "arbitrary")),
        name="moe_experts",
    )(blk_e, n_used, xb, w1, w3, w2)


def _ln_res_kernel(x_ref, f0_ref, f1_ref, wt_ref, mod_ref, g_ref, b_ref, o_ref, *, alpha):
    f = f0_ref[...] * wt_ref[:, 0:1] + f1_ref[...] * wt_ref[:, 1:2]
    o_ref[...] = _gated_residual_ln(x_ref[...], f, mod_ref, 5, alpha, g_ref[...], b_ref[...])


def _ln_res(x, f01, wts, modx, g, b, alpha, tm):
    m, d = x.shape
    nt = m // tm
    return pl.pallas_call(
        functools.partial(_ln_res_kernel, alpha=alpha),
        grid=(nt,),
        in_specs=[pl.BlockSpec((tm, d), lambda i: (i, 0)),
                  pl.BlockSpec((tm, d), lambda i: (i, 0)),
                  pl.BlockSpec((tm, d), lambda i: (i + nt, 0)),
                  pl.BlockSpec((tm, TOP_K), lambda i: (i, 0)),
                  pl.BlockSpec((tm // SUB, 8, d), lambda i: (i, 0, 0)),
                  pl.BlockSpec((1, d), lambda i: (0, 0)), pl.BlockSpec((1, d), lambda i: (0, 0))],
        out_specs=pl.BlockSpec((tm, d), lambda i: (i, 0)),
        out_shape=jax.ShapeDtypeStruct((m, d), F32),
        compiler_params=_cp(("parallel",)),
        name="moe_residual_ln",
    )(x, f01, f01, wts, modx, g.reshape(1, d), b.reshape(1, d))


def _moe(x, modx, router, w1, w3, w2, layer, g, b, alpha, tm_rows, tm, tf):
    m, d = x.shape
    wr_pad = jnp.zeros((d, LANE), F32).at[:, :N_EXPERTS].set(router)
    u, logits = _route(x, modx, wr_pad, tm_rows)
    top_v, top_i = lax.top_k(logits[:, :N_EXPERTS], TOP_K)
    wts = jax.nn.softmax(top_v, axis=-1)
    e_flat = top_i.reshape(-1).astype(jnp.int32)
    n_pair = m * TOP_K
    onehot = (e_flat[:, None] == jnp.arange(N_EXPERTS, dtype=jnp.int32)[None, :]).astype(jnp.int32)
    csum = jnp.cumsum(onehot, axis=0)
    counts = csum[-1]
    rank = jnp.take_along_axis(csum, e_flat[:, None], axis=1)[:, 0] - 1
    padded = (counts + tm - 1) // tm * tm
    pad_end = jnp.cumsum(padded)
    pad_start = pad_end - padded
    dest = pad_start[e_flat] + rank
    n_blk = -(-n_pair // tm) + N_EXPERTS
    r = n_blk * tm
    tok = jnp.arange(n_pair, dtype=jnp.int32) // TOP_K
    n_used = (pad_end[-1] // tm).astype(jnp.int32)
    blk = jnp.minimum(jnp.arange(n_blk, dtype=jnp.int32), n_used - 1) * tm
    blk_e = jnp.minimum(jnp.searchsorted(pad_end, blk, side='right'), N_EXPERTS - 1).astype(jnp.int32)
    _, tok_sorted = lax.sort_key_val(dest, tok)
    shift = pad_start - (jnp.cumsum(counts) - counts)
    src = jnp.arange(r, dtype=jnp.int32) - jnp.repeat(shift[blk_e], tm).astype(jnp.int32)
    row_tok = jnp.take(tok_sorted, jnp.clip(src, 0, n_pair - 1), mode="clip")
    xb = jnp.take(u, row_tok, axis=0, mode="clip")
    yb = _moe_experts(xb, blk_e, n_used.reshape(1), w1, w3, w2, layer, tm, tf)
    f01 = jnp.take(yb, dest.reshape(m, TOP_K).T.reshape(-1), axis=0, mode="clip")
    return _ln_res(x, f01, wts, modx, g, b, alpha, tm_rows)


def _rope_tables(nb, lat_len, ctx_len):
    rows = lat_len // GRID_W
    row = jnp.repeat(jnp.arange(rows, dtype=F32), GRID_W)
    col = jnp.tile(jnp.arange(GRID_W, dtype=F32), rows)
    n_freq = HEAD_DIM // 4
    inv = ROPE_THETA ** (-jnp.arange(n_freq, dtype=F32) / n_freq)
    ar, ac = row[:, None] * inv, col[:, None] * inv
    cos = jnp.concatenate([jnp.cos(ar), jnp.cos(ar), jnp.cos(ac), jnp.cos(ac)], axis=1)
    sin = jnp.concatenate([-jnp.sin(ar), jnp.sin(ar), -jnp.sin(ac), jnp.sin(ac)], axis=1)
    cos = jnp.concatenate([cos, jnp.ones((ctx_len, HEAD_DIM), F32)], axis=0)
    sin = jnp.concatenate([sin, jnp.zeros((ctx_len, HEAD_DIM), F32)], axis=0)
    return jnp.tile(cos, (nb, 1)), jnp.tile(sin, (nb, 1))


def _row_tile(m, pref):
    tm = pref
    while m % tm:
        tm //= 2
    return tm


def kernel(x, c, ctx, c_ctx, w_mod, b_mod, w_in, conv_a_w, conv_a_b, ret_decay, hy_conv_w, hy_conv_b, hy_w1, hy_b1, hy_w2, hy_b2, hy_w3, hy_freq, hy_skip, q_norm, k_norm, w_gate, b_gate, w_branch, w_o, ln_g, ln_b, ffn_w1, ffn_w3, ffn_w2, router, moe_w1, moe_w3, moe_w2):
    nb, lat_len, d = x.shape
    ctx_len = ctx.shape[1]
    depth = w_mod.shape[0]
    t = lat_len + ctx_len
    m = nb * t
    alpha = float((2 * depth) ** 0.25)
    assert lat_len % SUB == 0 and ctx_len % SUB == 0 and nb % 2 == 0

    act = jnp.concatenate([jax.nn.silu(c), jax.nn.silu(c_ctx)[None, :],
                           jnp.zeros((8 - nb - 1, d), F32)], axis=0)
    mods = _mod_all(act, w_mod, b_mod).reshape(depth, 8, N_MOD, d)
    mods = jnp.pad(mods, ((0, 0), (0, 0), (0, 8 - N_MOD), (0, 0)))
    tiles_per_batch = t // SUB
    tile_pos = np.arange(m // SUB) % tiles_per_batch
    tile_group = np.where(tile_pos < lat_len // SUB, np.arange(m // SUB) // tiles_per_batch, nb)
    cos_t, sin_t = _rope_tables(nb, lat_len, ctx_len)

    cast = lambda a: a.astype(BF16)
    w_in_b, w_gate_b, w_branch_b, w_o_b = cast(w_in), cast(w_gate), cast(w_branch), cast(w_o)
    ffn_b = (cast(ffn_w1), cast(ffn_w3), cast(ffn_w2))
    moe_w = (cast(moe_w1), cast(moe_w3), cast(moe_w2))

    rows = jnp.concatenate([x, ctx], axis=1).reshape(m, d)
    tm = _row_tile(m, 512)
    for l in range(depth):
        modx = mods[l][tile_group]
        za, zvg, zh, rq, rk, aq, ak, av = _proj(rows, modx, w_in_b, l, cos_t, sin_t, q_norm[l], k_norm[l], tm)
        y_d = _attention(aq, ak, av, nb, lat_len)
        log_gamma = -jnp.exp(ret_decay[l].astype(F32))
        y_r = _retention(rq, rk, zvg, _retention_tables(log_gamma), nb, lat_len)
        y_a = _shortconv(za, conv_a_w[l], conv_a_b[l], nb, lat_len)
        y_h = _hyena(zh, hy_conv_w[l], hy_conv_b[l],
                     (hy_w1[l], hy_b1[l], hy_w2[l], hy_b2[l], hy_w3[l], hy_freq[l]), hy_skip[l], nb, lat_len)
        rows = _merge(rows, modx, (y_a, y_r, y_h, y_d), w_gate_b, b_gate[l], w_branch_b, w_o_b, l,
                      ln_g[l, 0], ln_b[l, 0], alpha, tm)
        i = l // 2
        if l % 2 == 0:
            rows = _ffn(rows, modx, *ffn_b, i, ln_g[l, 1], ln_b[l, 1], alpha, tm, 256)
        else:
            rows = _moe(rows, modx, router[i], *moe_w, i, ln_g[l, 1], ln_b[l, 1], alpha, tm, 512, 1792)
    return rows.reshape(nb, t, d)[:, :lat_len]
```

```python
import functools
import math

import numpy as np
import jax
import jax.numpy as jnp
from jax import lax
from jax.experimental import pallas as pl
from jax.experimental.pallas import tpu as pltpu

F32 = jnp.float32
BF16 = jnp.bfloat16

GRID_W = 64
BRANCH_W = 512
RET_HEADS = 4
RET_D = 128
RET_CHUNK = 128
HY_EMB = 33
HY_FAST_PCT = 0.3
HY_SLOW_PCT = 1.5
HY_TARGET = 1e-2
ATT_HEADS = 4
ATT_KV_HEADS = 2
HEAD_DIM = 128
ROPE_THETA = 10000.0
N_EXPERTS = 8
TOP_K = 2
N_MOD = 6
A_W = 3 * BRANCH_W
R_W = 4 * RET_HEADS * RET_D
H_W = 3 * BRANCH_W
D_W = (ATT_HEADS + 2 * ATT_KV_HEADS) * HEAD_DIM
IN_W = A_W + R_W + H_W + D_W
LN_EPS = 1e-6

SUB = 256
LANE = 128
FFT_N1 = 128
RET_UNROLL = 8
ATT_TQ = 1024
ATT_KC = 1024
DFT_PASSES = 1
DFT_FILTER_PASSES = 1
VMEM_LIMIT = 56 * 1024 * 1024


def _cp(sem, vmem=VMEM_LIMIT):
    return pltpu.CompilerParams(dimension_semantics=sem, vmem_limit_bytes=vmem)


def _dot(a, b):
    return jnp.dot(a, b, preferred_element_type=F32)


def _dot_nt(a, b):
    return lax.dot_general(a, b, (((1,), (1,)), ((), ())), preferred_element_type=F32)


def _dot_tn(a, b):
    return lax.dot_general(a, b, (((0,), (0,)), ((), ())), preferred_element_type=F32)


def _split_dot(f_cat, x, passes):
    xh = x.astype(BF16)
    if passes == 1:
        return _dot(f_cat, xh)
    m = f_cat.shape[0] // 2
    r = _dot(f_cat, xh)
    r = r[:m] + r[m:]
    if passes == 3:
        r = r + _dot(f_cat[:m], (x - xh.astype(F32)).astype(BF16))
    return r


def _dft_operand(f_cat, passes):
    return f_cat[..., :f_cat.shape[-2] // 2, :] if passes == 1 else f_cat


def _modulate(x, mod_ref, shift_row, scale_row):
    parts = []
    for s in range(x.shape[0] // SUB):
        sh = mod_ref[s, shift_row:shift_row + 1, :]
        sc = mod_ref[s, scale_row:scale_row + 1, :]
        parts.append(x[s * SUB:(s + 1) * SUB] * (1.0 + sc) + sh)
    return parts[0] if len(parts) == 1 else jnp.concatenate(parts, axis=0)


def _gated_residual_ln(x, f, mod_ref, gate_row, alpha, g, b):
    parts = []
    for s in range(x.shape[0] // SUB):
        gt = mod_ref[s, gate_row:gate_row + 1, :]
        parts.append(alpha * x[s * SUB:(s + 1) * SUB] + gt * f[s * SUB:(s + 1) * SUB])
    r = parts[0] if len(parts) == 1 else jnp.concatenate(parts, axis=0)
    mu = jnp.mean(r, axis=-1, keepdims=True)
    rc = r - mu
    var = jnp.mean(rc * rc, axis=-1, keepdims=True)
    return rc * lax.rsqrt(var + LN_EPS) * g + b


def _mod_kernel(a_ref, w_ref, b_ref, o_ref):
    o_ref[0] = _dot(a_ref[...].astype(BF16), w_ref[0].astype(BF16)) + b_ref[0]


def _mod_all(act, w_mod, b_mod):
    depth, d, n = w_mod.shape
    tn = 512
    return pl.pallas_call(
        _mod_kernel,
        grid=(depth, n // tn),
        in_specs=[pl.BlockSpec((8, d), lambda l, j: (0, 0)),
                  pl.BlockSpec((1, d, tn), lambda l, j: (l, 0, j)),
                  pl.BlockSpec((1, 1, tn), lambda l, j: (l, 0, j))],
        out_specs=pl.BlockSpec((1, 8, tn), lambda l, j: (l, 0, j)),
        out_shape=jax.ShapeDtypeStruct((depth, 8, n), F32),
        compiler_params=_cp(("parallel", "parallel")),
        name="mod_vectors",
    )(act, w_mod, b_mod.reshape(depth, 1, n))


PROJ_TN = 512
assert (A_W, R_W, H_W, D_W) == (3 * PROJ_TN, 4 * PROJ_TN, 3 * PROJ_TN, 2 * PROJ_TN)


def _proj_kernel(x_ref, mod_ref, w_ref, cos_ref, sin_ref, gq_ref, gk_ref,
                 za_ref, zvg_ref, zh_ref, rq_ref, rk_ref, aq_ref, ak_ref, av_ref):
    u = _modulate(x_ref[...], mod_ref, 0, 1).astype(BF16)
    tn = PROJ_TN
    cos, sin = cos_ref[...], sin_ref[...]
    lane = lax.broadcasted_iota(jnp.int32, cos.shape, 1)
    first_half = (lane % 64) < 32

    def tile(j):
        return _dot(u, w_ref[:, j * tn:(j + 1) * tn])

    def rope(x):
        partner = jnp.where(first_half, pltpu.roll(x, 96, 1), pltpu.roll(x, 32, 1))
        return x * cos + partner * sin

    def rms(x, g):
        return x * lax.rsqrt(jnp.mean(x * x, axis=-1, keepdims=True) + LN_EPS) * g

    head = lambda h: slice(h * LANE, (h + 1) * LANE)

    for j in range(3):
        za_ref[:, j * tn:(j + 1) * tn] = tile(j)
    r = tile(3)
    for h in range(RET_HEADS):
        rq_ref[:, head(h)] = rope(r[:, head(h)])
    r = tile(4)
    for h in range(RET_HEADS):
        rk_ref[:, head(h)] = rope(r[:, head(h)] * (RET_D ** -0.5))
    for j in range(2):
        zvg_ref[:, j * tn:(j + 1) * tn] = tile(5 + j)
    for j in range(3):
        zh_ref[:, j * tn:(j + 1) * tn] = tile(7 + j)
    q_scale = (HEAD_DIM ** -0.5) * math.log2(math.e)
    r = tile(10)
    for h in range(ATT_HEADS):
        aq_ref[:, head(h)] = (rope(rms(r[:, head(h)], gq_ref[...])) * q_scale).astype(BF16)
    r = tile(11)
    ones_col = jnp.where(lane == 0, 1.0, 0.0).astype(BF16)
    for h in range(ATT_KV_HEADS):
        ak_ref[:, head(h)] = rope(rms(r[:, head(h)], gk_ref[...])).astype(BF16)
        av_ref[:, head(2 * h)] = r[:, head(ATT_KV_HEADS + h)].astype(BF16)
        av_ref[:, head(2 * h + 1)] = ones_col


def _proj(x, modx, w, layer, cos_t, sin_t, gq, gk, tm):
    m, d = x.shape
    n = w.shape[2]
    tn = PROJ_TN
    qw, kw = ATT_HEADS * HEAD_DIM, ATT_KV_HEADS * HEAD_DIM
    row = lambda i: (i, 0)
    widths = (A_W, 2 * tn, H_W, tn, tn, qw, kw, 2 * kw)
    dtypes = (F32,) * 5 + (BF16,) * 3
    return pl.pallas_call(
        _proj_kernel,
        grid=(m // tm,),
        in_specs=[pl.BlockSpec((tm, d), row),
                  pl.BlockSpec((tm // SUB, 8, d), lambda i: (i, 0, 0)),
                  pl.BlockSpec((None, d, n), lambda i: (layer, 0, 0), pipeline_mode=pl.Buffered(1)),
                  pl.BlockSpec((tm, LANE), row), pl.BlockSpec((tm, LANE), row),
                  pl.BlockSpec((1, LANE), lambda i: (0, 0)), pl.BlockSpec((1, LANE), lambda i: (0, 0))],
        out_specs=[pl.BlockSpec((tm, wd), row) for wd in widths],
        out_shape=[jax.ShapeDtypeStruct((m, wd), dt) for wd, dt in zip(widths, dtypes)],
        compiler_params=_cp(("parallel",)),
        name="in_proj",
    )(x, modx, w, cos_t, sin_t, gq.reshape(1, LANE), gk.reshape(1, LANE))


def _attn_kernel(q_ref, k_ref, v_ref, *rest, key_lo, key_hi, kc):
    o_ref = rest[-1]
    tq = q_ref.shape[1]
    grp = ATT_HEADS // ATT_KV_HEADS
    q2 = jnp.concatenate([q_ref[0, :, g * HEAD_DIM:(g + 1) * HEAD_DIM] for g in range(grp)], axis=0)
    m_run, acc = None, None
    for lo in range(key_lo, key_hi, kc):
        hi = min(lo + kc, key_hi)
        s = _dot_nt(q2, k_ref[0, lo:hi, :])
        m_chunk = jnp.max(s, axis=-1, keepdims=True)
        m_new = m_chunk if m_run is None else jnp.maximum(m_run, m_chunk)
        pv = _dot(jnp.exp2(s - m_new).astype(BF16), v_ref[0, lo:hi, :])
        acc = pv if acc is None else jnp.exp2(m_run - m_new) * acc + pv
        m_run = m_new
    for g in range(grp):
        a = acc[g * tq:(g + 1) * tq]
        o_ref[0, :, g * HEAD_DIM:(g + 1) * HEAD_DIM] = a[:, :HEAD_DIM] / a[:, HEAD_DIM:HEAD_DIM + 1]


def _attention(aq, ak, av, nb, lat_len):
    m = aq.shape[0]
    t = m // nb
    lc = t - lat_len
    grp_w = (ATT_HEADS // ATT_KV_HEADS) * HEAD_DIM
    args = (aq.reshape(nb, t, -1), ak.reshape(nb, t, -1), av.reshape(nb, t, -1))

    def call(tq, n_tiles, first_tile, key_lo, key_hi, prev):
        qmap = lambda b, kv, i: (b, first_tile + i, kv)
        specs = [pl.BlockSpec((1, tq, grp_w), qmap),
                 pl.BlockSpec((1, t, HEAD_DIM), lambda b, kv, i: (b, 0, kv)),
                 pl.BlockSpec((1, t, 2 * HEAD_DIM), lambda b, kv, i: (b, 0, kv))]
        extra, alias = (), {}
        if prev is not None:
            specs.append(pl.BlockSpec(memory_space=pl.ANY))
            extra, alias = (prev,), {3: 0}
        return pl.pallas_call(
            functools.partial(_attn_kernel, key_lo=key_lo, key_hi=key_hi, kc=ATT_KC),
            grid=(nb, ATT_KV_HEADS, n_tiles),
            in_specs=specs,
            out_specs=pl.BlockSpec((1, tq, grp_w), qmap),
            out_shape=jax.ShapeDtypeStruct((nb, t, ATT_HEADS * HEAD_DIM), F32),
            input_output_aliases=alias,
            compiler_params=_cp(("parallel", "parallel", "arbitrary")),
            name="gqa_attention",
        )(*args, *extra)

    tq = math.gcd(ATT_TQ, lat_len)
    out = call(tq, lat_len // tq, 0, 0, t, None)
    out = call(lc, 1, t // lc - 1, lat_len, t, out)
    return out.reshape(m, -1)


def _ret_kernel(q_ref, k_ref, v_ref, g_ref, intra_ref, qd_ref, kd_ref, cd_ref, o_ref, st_ref,
                *, n_lat, n_ctx):
    c_len = RET_CHUNK
    n = n_lat + n_ctx

    def rows(c):
        return pl.ds(pl.multiple_of(c * c_len, c_len), c_len)

    def kv(d, c):
        return _dot_tn((k_ref[0, rows(c), :] * kd_ref[d, 0]).astype(BF16), v_ref[0, rows(c), :].astype(BF16))

    def scan_step(t, carry):
        sf, sb = carry
        cf = jnp.where(t < n_ctx, n_lat + t, t - n_ctx)
        cb = jnp.where(t < n_ctx, n - 1 - t, n_lat - 1 - (t - n_ctx))
        st_ref[cf, :RET_D, :] = sf.astype(BF16)
        st_ref[cb, RET_D:, :] = sb.astype(BF16)
        return (cd_ref[0, 0, 0:1, :] * sf + kv(0, cf), cd_ref[1, 0, 0:1, :] * sb + kv(1, cb))

    zero = jnp.zeros((RET_D, RET_D), F32)
    lax.fori_loop(0, n, scan_step, (zero, zero), unroll=RET_UNROLL)

    mask = intra_ref[0, 0] + intra_ref[1, 0]

    def out_step(c, carry):
        qc = q_ref[0, rows(c), :]
        vc = v_ref[0, rows(c), :].astype(BF16)
        s = _dot_nt(qc.astype(BF16), k_ref[0, rows(c), :].astype(BF16)) * mask
        qdec = jnp.concatenate([qc * qd_ref[0, 0], qc * qd_ref[1, 0]], axis=1).astype(BF16)
        o = _dot(s.astype(BF16), vc) + _dot(qdec, st_ref[c])
        mu = jnp.mean(o, axis=-1, keepdims=True)
        oc = o - mu
        var = jnp.mean(oc * oc, axis=-1, keepdims=True)
        g = g_ref[0, rows(c), :]
        o_ref[0, rows(c), :] = (g * jax.nn.sigmoid(g)) * (oc * lax.rsqrt(var + LN_EPS))
        return carry

    lax.fori_loop(0, n, out_step, 0, unroll=RET_UNROLL)


def _retention(rq, rk, z, tabs, nb, lat_len):
    m = rq.shape[0]
    t = m // nb
    intra, qd, kd, cd = tabs
    v_blk, g_blk = 0, RET_HEADS
    kern = functools.partial(_ret_kernel, n_lat=lat_len // RET_CHUNK, n_ctx=(t - lat_len) // RET_CHUNK)
    seq = lambda off: pl.BlockSpec((1, t, LANE), lambda b, h: (b, 0, off + h))
    tab = lambda r: pl.BlockSpec((2, 1, r, LANE), lambda b, h: (0, h, 0, 0))
    out = pl.pallas_call(
        kern,
        grid=(nb, RET_HEADS),
        in_specs=[seq(0), seq(0), seq(v_blk), seq(g_blk),
                  tab(RET_CHUNK), tab(RET_CHUNK), tab(RET_CHUNK), tab(8)],
        out_specs=seq(0),
        out_shape=jax.ShapeDtypeStruct((nb, t, RET_HEADS * RET_D), F32),
        scratch_shapes=[pltpu.VMEM((t // RET_CHUNK, 2 * RET_D, RET_D), BF16)],
        compiler_params=_cp(("parallel", "parallel")),
        name="retention",
    )(rq.reshape(nb, t, -1), rk.reshape(nb, t, -1), z.reshape(nb, t, -1), z.reshape(nb, t, -1), intra, qd, kd, cd)
    return out.reshape(m, -1)


def _retention_tables(log_gamma):
    c = RET_CHUNK
    i = jnp.arange(c, dtype=F32)
    lg = log_gamma.astype(F32)[:, :, None]
    rel = i[:, None] - i[None, :]
    intra_f = jnp.where(rel >= 0, jnp.exp(lg[0][..., None] * jnp.maximum(rel, 0.0)), 0.0)
    intra_b = jnp.where(rel <= 0, jnp.exp(lg[1][..., None] * jnp.maximum(-rel, 0.0)), 0.0)
    qd = jnp.stack([jnp.exp(lg[0] * (i + 1.0)), jnp.exp(lg[1] * (c - i))])
    kd = jnp.stack([jnp.exp(lg[0] * (c - 1.0 - i)), jnp.exp(lg[1] * i)])
    cd = jnp.exp(lg * c)
    bc = lambda a: jnp.broadcast_to(a[..., None], a.shape + (LANE,))
    return (jnp.stack([intra_f, intra_b]), bc(qd), bc(kd),
            jnp.broadcast_to(cd[..., None], (2, RET_HEADS, 8, LANE)))


def _shift_rows(x, lat_len):
    t = x.shape[0]
    row = lax.broadcasted_iota(jnp.int32, x.shape, 0)
    xm = jnp.where((row == 0) | (row == lat_len), 0.0, pltpu.roll(x, 1, 0))
    xp = jnp.where((row == lat_len - 1) | (row == t - 1), 0.0, pltpu.roll(x, t - 1, 0))
    return xm, xp


def _shortconv_kernel(bg_ref, cg_ref, xv_ref, w_ref, b_ref, o_ref, *, lat_len):
    p = cg_ref[0] * xv_ref[0]
    pm, pp = _shift_rows(p, lat_len)
    o_ref[0] = bg_ref[0] * (pm * w_ref[0:1, :] + p * w_ref[1:2, :] + pp * w_ref[2:3, :] + b_ref[...])


def _shortconv(z, w, b, nb, lat_len):
    m = z.shape[0]
    t = m // nb
    nct = BRANCH_W // LANE
    kern = functools.partial(_shortconv_kernel, lat_len=lat_len)
    seq = lambda off: pl.BlockSpec((1, t, LANE), lambda bb, j: (bb, 0, off + j))
    out = pl.pallas_call(
        kern,
        grid=(nb, nct),
        in_specs=[seq(0), seq(nct), seq(2 * nct),
                  pl.BlockSpec((3, LANE), lambda bb, j: (0, j)),
                  pl.BlockSpec((1, LANE), lambda bb, j: (0, j))],
        out_specs=seq(0),
        out_shape=jax.ShapeDtypeStruct((nb, t, BRANCH_W), F32),
        compiler_params=_cp(("parallel", "parallel")),
        name="short_conv",
    )(z.reshape(nb, t, -1), z.reshape(nb, t, -1), z.reshape(nb, t, -1), w, b.reshape(1, -1))
    return out.reshape(m, -1)


def _hyconv_kernel(z_ref, w_ref, b_ref, ol_ref, oc_ref, *, lat_len):
    x = z_ref[0]
    xm, xp = _shift_rows(x, lat_len)
    y = xm * w_ref[0:1, :] + x * w_ref[1:2, :] + xp * w_ref[2:3, :] + b_ref[...]
    ol_ref[0, 0] = y[:lat_len]
    oc_ref[0, 0] = y[lat_len:]


def _hyconv(z, w, b, nb, lat_len):
    m = z.shape[0]
    t = m // nb
    nct = BRANCH_W // LANE
    off = 0
    kern = functools.partial(_hyconv_kernel, lat_len=lat_len)
    return pl.pallas_call(
        kern,
        grid=(nb, 3 * nct),
        in_specs=[pl.BlockSpec((1, t, LANE), lambda bb, j: (bb, 0, off + j)),
                  pl.BlockSpec((3, LANE), lambda bb, j: (0, j)),
                  pl.BlockSpec((1, LANE), lambda bb, j: (0, j))],
        out_specs=[pl.BlockSpec((1, 1, lat_len, LANE), lambda bb, j: (j // nct, bb, 0, j % nct)),
                   pl.BlockSpec((1, 1, t - lat_len, LANE), lambda bb, j: (j // nct, bb, 0, j % nct))],
        out_shape=[jax.ShapeDtypeStruct((3, nb, lat_len, BRANCH_W), F32),
                   jax.ShapeDtypeStruct((3, nb, t - lat_len, BRANCH_W), F32)],
        compiler_params=_cp(("parallel", "parallel")),
        name="hyena_dwconv",
    )(z.reshape(nb, t, -1), w, b.reshape(1, -1))


def _stack_c(m):
    return np.block([[m.real, -m.imag], [m.imag, m.real]])


def _split_np(m):
    m32 = np.asarray(m, np.float32)
    hi = m32.astype(BF16)
    lo = (m32 - hi.astype(np.float32)).astype(BF16)
    return np.concatenate([hi, lo], axis=-2)


@functools.lru_cache(maxsize=None)
def _dft_tables(lat_len):
    n = 2 * lat_len
    n1, n2 = FFT_N1, 2 * lat_len // FFT_N1
    a1 = np.arange(n1)
    a2 = np.arange(n2)
    w1 = np.exp(-2j * np.pi * np.outer(a1, a1[:n1 // 2]) / n1)
    f1 = _stack_c(w1)
    w2 = np.exp(-2j * np.pi * np.outer(a2, a2) / n2)
    tw = np.exp(-2j * np.pi * np.outer(a1, a2) / n)
    cf = w2[None, :, :] * tw[:, None, :]
    f2 = np.stack([_stack_c(cf[k]) for k in range(n1)])
    g2 = np.stack([_stack_c(cf[k].conj().T) for k in range(n1)])
    w1i = np.exp(2j * np.pi * np.outer(a1[:n1 // 2], a1) / n1) / n
    f3 = _stack_c(w1i)
    f1_real = f1[:, :n1 // 2]
    return _split_np(f1), _split_np(f1_real), _split_np(f2), _split_np(g2), _split_np(f3)


@functools.lru_cache(maxsize=None)
def _dft_tables_small(ctx_len):
    n = 2 * ctx_len
    a = np.arange(n)
    ff = _stack_c(np.exp(-2j * np.pi * np.outer(a, a[:ctx_len]) / n))
    fi = _stack_c(np.exp(2j * np.pi * np.outer(a[:ctx_len], a) / n) / n)
    return _split_np(ff), _split_np(ff[:, :ctx_len]), _split_np(fi)


def _fft1_kernel(x_ref, f_ref, o_ref, *, passes):
    f = f_ref[...]
    for j in range(x_ref.shape[3]):
        parts = [x_ref[0, b, :, j, :] for b in range(x_ref.shape[1])]
        r = _split_dot(f, parts[0] if len(parts) == 1 else jnp.concatenate(parts, axis=0), passes)
        h = r.shape[0] // 2
        o_ref[0, 0, :, j, :] = r[:h]
        o_ref[0, 1, :, j, :] = r[h:]


def _fft1(x, sel, f1, jb, passes):
    _, p, mem, r, n2, w = x.shape
    f1 = _dft_operand(f1, passes)
    n1 = f1.shape[0] // (2 if passes == 1 else 4)
    return pl.pallas_call(
        functools.partial(_fft1_kernel, passes=passes),
        grid=(p, n2 // jb),
        in_specs=[pl.BlockSpec((None, 1, mem, r, jb, w), lambda pp, j: (sel, pp, 0, 0, j, 0)),
                  pl.BlockSpec(f1.shape, lambda pp, j: (0, 0))],
        out_specs=pl.BlockSpec((1, 2, n1, jb, w), lambda pp, j: (pp, 0, 0, j, 0)),
        out_shape=jax.ShapeDtypeStruct((p, 2, n1, n2, w), F32),
        compiler_params=_cp(("parallel", "parallel")),
        name="hyena_fft_stage1",
    )(x, f1)


def _fft2f_kernel(a_ref, f_ref, o_ref, *, kb, n2, passes):
    w = a_ref.shape[-1]
    for t in range(kb):
        xf = _split_dot(f_ref[t], a_ref[0, :, t].reshape(2 * n2, w), passes)
        xb = _split_dot(f_ref[t], a_ref[1, :, t].reshape(2 * n2, w), passes)
        o_ref[0, 0, t] = xf[:n2] + xb[:n2]
        o_ref[0, 1, t] = xf[n2:] - xb[n2:]


def _fft2_filters(a, f2, kb, passes):
    q, _, n1, n2, w = a.shape
    f2 = _dft_operand(f2, passes)
    return pl.pallas_call(
        functools.partial(_fft2f_kernel, kb=kb, n2=n2, passes=passes),
        grid=(q // 2, n1 // kb),
        in_specs=[pl.BlockSpec((2, 2, kb, n2, w), lambda o, k: (o, 0, k, 0, 0)),
                  pl.BlockSpec((kb,) + f2.shape[1:], lambda o, k: (k, 0, 0))],
        out_specs=pl.BlockSpec((1, 2, kb, n2, w), lambda o, k: (o, 0, k, 0, 0)),
        out_shape=jax.ShapeDtypeStruct((q // 2, 2, n1, n2, w), F32),
        compiler_params=_cp(("parallel", "parallel")),
        name="hyena_filter_spectrum",
    )(a, f2)


def _fft2_kernel(a_ref, h_ref, f_ref, g_ref, o_ref, *, kb, n2, passes):
    w = a_ref.shape[-1]
    for t in range(kb):
        a = a_ref[0, :, t].reshape(2 * n2, w)
        x = _split_dot(f_ref[t], a, passes)
        xr, xi = x[:n2], x[n2:]
        hr, hi = h_ref[0, t], h_ref[1, t]
        y = jnp.concatenate([xr * hr - xi * hi, xr * hi + xi * hr], axis=0)
        o_ref[0, :, t] = _split_dot(g_ref[t], y, passes).reshape(2, n2, w)


def _fft2(a, hspec, order, f2, g2, kb, passes):
    p, _, n1, n2, w = a.shape
    f2, g2 = _dft_operand(f2, passes), _dft_operand(g2, passes)
    kern = functools.partial(_fft2_kernel, kb=kb, n2=n2, passes=passes)
    return pl.pallas_call(
        kern,
        grid=(n1 // kb, p),
        in_specs=[pl.BlockSpec((1, 2, kb, n2, w), lambda k, pp: (pp, 0, k, 0, 0)),
                  pl.BlockSpec((None, 2, kb, n2, w), lambda k, pp: (order, 0, k, 0, 0)),
                  pl.BlockSpec((kb,) + f2.shape[1:], lambda k, pp: (k, 0, 0)),
                  pl.BlockSpec((kb,) + g2.shape[1:], lambda k, pp: (k, 0, 0))],
        out_specs=pl.BlockSpec((1, 2, kb, n2, w), lambda k, pp: (pp, 0, k, 0, 0)),
        out_shape=jax.ShapeDtypeStruct(a.shape, F32),
        compiler_params=_cp(("parallel", "arbitrary")),
        name="hyena_fft_stage2",
    )(a, hspec, f2, g2)


def _fft3_kernel(b_ref, f_ref, gate_ref, prev_ref, skip_ref, o_ref, *, passes):
    f = f_ref[...]
    r = gate_ref.shape[2]
    for j in range(b_ref.shape[3]):
        bj = jnp.concatenate([b_ref[0, 0, :, j, :], b_ref[0, 1, :, j, :]], axis=0)
        conv = _split_dot(f, bj, passes)
        for mem in range(2):
            cm = conv[mem * r:(mem + 1) * r]
            o_ref[0, mem, :, j, :] = gate_ref[0, mem, :, j, :] * (cm + skip_ref[...] * prev_ref[0, mem, :, j, :])


def _fft3(bm, f3, gate, gate_sel, prev, prev_sel, skip, out_rows, jb, passes):
    p, _, n1, n2, w = bm.shape
    r = n1 // 2
    f3 = _dft_operand(f3, passes)
    seq = lambda s: pl.BlockSpec((None, 1, 2, r, jb, w), lambda pp, j: (s, pp, 0, 0, j, 0))
    return pl.pallas_call(
        functools.partial(_fft3_kernel, passes=passes),
        grid=(p, n2 // jb),
        in_specs=[pl.BlockSpec((1, 2, n1, jb, w), lambda pp, j: (pp, 0, 0, j, 0)),
                  pl.BlockSpec(f3.shape, lambda pp, j: (0, 0)),
                  seq(gate_sel), seq(prev_sel),
                  pl.BlockSpec((1, w), lambda pp, j: (0, 0))],
        out_specs=pl.BlockSpec((1, 2, r, jb, w), lambda pp, j: (pp, 0, 0, j, 0)),
        out_shape=jax.ShapeDtypeStruct((p, 2, out_rows, n2, w), F32),
        compiler_params=_cp(("parallel", "parallel")),
        name="hyena_fft_stage3",
    )(bm, f3, gate, prev, skip)


def _hyctx_kernel(v_ref, x1_ref, x2_ref, filt_ref, ff_ref, ffr_ref, fi_ref, skip_ref, yh_hbm_ref, o_ref,
                  *, passes, filter_passes):
    del yh_hbm_ref
    lc, w = v_ref.shape[2], v_ref.shape[3]
    n = 2 * lc
    y = v_ref[0].reshape(2 * lc, w)
    for o, gate_ref in enumerate((x1_ref, x2_ref)):
        sf = _split_dot(ffr_ref[...], filt_ref[o, 0], filter_passes)
        sb = _split_dot(ffr_ref[...], filt_ref[o, 1], filter_passes)
        hr, hi = sf[:n] + sb[:n], sf[n:] - sb[n:]
        x = _split_dot(ff_ref[...], y, passes)
        xr, xi = x[:n], x[n:]
        spec = jnp.concatenate([xr * hr - xi * hi, xr * hi + xi * hr], axis=0)
        conv = _split_dot(fi_ref[...], spec, passes)
        y = gate_ref[0].reshape(2 * lc, w) * (conv + skip_ref[o:o + 1, :] * y)
    o_ref[0] = y.reshape(2, lc, w)


def _hyena_ctx(hc, filt_c, ff, ffr, fi, skip, yh, passes, filter_passes):
    _, nb, lc, w = hc.shape
    p = nb // 2
    t = yh.shape[2]
    ff, fi, ffr = _dft_operand(ff, passes), _dft_operand(fi, passes), _dft_operand(ffr, filter_passes)
    part = pl.BlockSpec((1, 2, lc, w), lambda pp: (pp, 0, 0, 0))
    hc4 = hc.reshape(3, p, 2, lc, w)
    return pl.pallas_call(
        functools.partial(_hyctx_kernel, passes=passes, filter_passes=filter_passes),
        grid=(p,),
        in_specs=[part, part, part,
                  pl.BlockSpec(filt_c.shape, lambda pp: (0, 0, 0, 0)),
                  pl.BlockSpec(ff.shape, lambda pp: (0, 0)),
                  pl.BlockSpec(ffr.shape, lambda pp: (0, 0)),
                  pl.BlockSpec(fi.shape, lambda pp: (0, 0)),
                  pl.BlockSpec(skip.shape, lambda pp: (0, 0)),
                  pl.BlockSpec(memory_space=pl.ANY)],
        out_specs=pl.BlockSpec((1, 2, lc, w), lambda pp: (pp, 0, t // lc - 1, 0)),
        out_shape=jax.ShapeDtypeStruct(yh.shape, F32),
        input_output_aliases={8: 0},
        compiler_params=_cp(("parallel",)),
        name="hyena_ctx",
    )(hc4[0], hc4[1], hc4[2], filt_c, ff, ffr, fi, skip, yh)


def _hyena_filters(l, w1, b1, w2, b2, w3, freq):
    t = jnp.linspace(0.0, 1.0, l, dtype=F32)[:, None]
    bands = (HY_EMB - 1) // 2
    w = 2.0 * math.pi * jnp.arange(l, dtype=F32) / l
    f = jnp.linspace(1e-4, bands - 1, bands, dtype=F32)
    ang = w[:, None] * f[None, :]
    feats = jnp.concatenate([t, jnp.cos(ang), -jnp.sin(ang)], axis=-1)
    hdn = jnp.sin(freq[0] * (feats @ w1 + b1))
    hdn = jnp.sin(freq[1] * (hdn @ w2 + b2))
    n_out = w3.shape[-1]
    deltas = jnp.linspace(math.log(HY_TARGET) / HY_SLOW_PCT, math.log(HY_TARGET) / HY_FAST_PCT, n_out, dtype=F32)
    decay = jnp.exp(-t * jnp.abs(deltas))
    pos = jnp.arange(l)[:, None]
    parts = []
    for q in range(n_out // BRANCH_W):
        sl = slice(q * BRANCH_W, (q + 1) * BRANCH_W)
        f = (hdn @ w3[:, sl]).astype(F32) * decay[:, sl]
        parts.append(jnp.where(pos == 0, 0.0, f) if q % 2 else f)
    return jnp.stack(parts).reshape(n_out // (2 * BRANCH_W), 2, l, BRANCH_W)


def _hyena(z, conv_w, conv_b, filt_w, skip, nb, lat_len):
    m = z.shape[0]
    t = m // nb
    lc = t - lat_len
    w = BRANCH_W
    p = nb // 2
    n1, n2 = FFT_N1, 2 * lat_len // FFT_N1
    r = n1 // 2
    hl, hc = _hyconv(z, conv_w, conv_b, nb, lat_len)
    f1, f1r, f2, g2, f3 = _dft_tables(lat_len)
    ff, ffr, fi = _dft_tables_small(lc)
    jb, kb = min(16, n2), 16
    dp, fp = DFT_PASSES, DFT_FILTER_PASSES
    filt_l = _hyena_filters(lat_len, *filt_w)
    fa = _fft1(filt_l.reshape(1, 4, 1, r, n2, w), 0, f1r, jb, fp)
    hs_l = _fft2_filters(fa, f2, kb, fp)
    hl6 = hl.reshape(3, p, 2, r, n2, w)
    y, y_sel = hl6, 0
    for o in range(2):
        a = _fft1(y, y_sel, f1, jb, dp)
        bm = _fft2(a, hs_l, o, f2, g2, kb, dp)
        rows = r if o == 0 else t // n2
        y = _fft3(bm, f3, hl6, 1 + o, y, y_sel, skip[o:o + 1], rows, jb, dp)[None]
        y_sel = 0
    yh = y.reshape(p, 2, t, w)
    yh = _hyena_ctx(hc, _hyena_filters(lc, *filt_w), ff, ffr, fi, skip, yh, dp, fp)
    return yh.reshape(m, w)


def _merge_kernel(x_ref, mod_ref, ya_ref, yr_ref, yh_ref, yd_ref, wg_ref, bg_ref, wb_ref, wo_ref, g_ref, b_ref,
                  o_ref, *, alpha):
    x = x_ref[...]
    u = _modulate(x, mod_ref, 0, 1).astype(BF16)
    acc = None
    for i, y_ref in enumerate((ya_ref, yr_ref, yh_ref, yd_ref)):
        gate = jax.nn.sigmoid(_dot(u, wg_ref[i]) + bg_ref[i:i + 1, :])
        term = gate * _dot(y_ref[...].astype(BF16), wb_ref[i])
        acc = term if acc is None else acc + term
    mix = _dot(acc.astype(BF16), wo_ref[...])
    o_ref[...] = _gated_residual_ln(x, mix, mod_ref, 2, alpha, g_ref[...], b_ref[...])


def _merge(x, modx, ys, wg, bg, wb, wo, layer, g, b, alpha, tm):
    m, d = x.shape
    w = BRANCH_W
    row = lambda i: (i, 0)
    const2 = lambda i: (0, 0)
    one = pl.Buffered(1)
    stacked = lambda a: pl.BlockSpec((None,) + a.shape[1:], lambda i: (layer,) + (0,) * (a.ndim - 1),
                                     pipeline_mode=one)
    return pl.pallas_call(
        functools.partial(_merge_kernel, alpha=alpha),
        grid=(m // tm,),
        in_specs=[pl.BlockSpec((tm, d), row),
                  pl.BlockSpec((tm // SUB, 8, d), lambda i: (i, 0, 0)),
                  pl.BlockSpec((tm, w), row), pl.BlockSpec((tm, w), row),
                  pl.BlockSpec((tm, w), row), pl.BlockSpec((tm, w), row),
                  stacked(wg), pl.BlockSpec(bg.shape, const2), stacked(wb), stacked(wo),
                  pl.BlockSpec((1, d), const2), pl.BlockSpec((1, d), const2)],
        out_specs=pl.BlockSpec((tm, d), row),
        out_shape=jax.ShapeDtypeStruct((m, d), F32),
        compiler_params=_cp(("parallel",)),
        name="branch_merge",
    )(x, modx, *ys, wg, bg, wb, wo, g.reshape(1, d), b.reshape(1, d))


def _ffn_kernel(x_ref, mod_ref, w1_ref, w3_ref, w2_ref, g_ref, b_ref, o_ref, *, alpha, tf):
    x = x_ref[...]
    u = _modulate(x, mod_ref, 3, 4).astype(BF16)
    acc = None
    for j in range(w1_ref.shape[1] // tf):
        sl = slice(j * tf, (j + 1) * tf)
        h1 = _dot(u, w1_ref[:, sl])
        h3 = _dot(u, w3_ref[:, sl])
        part = _dot((h1 * jax.nn.sigmoid(h1) * h3).astype(BF16), w2_ref[sl, :])
        acc = part if acc is None else acc + part
    o_ref[...] = _gated_residual_ln(x, acc, mod_ref, 5, alpha, g_ref[...], b_ref[...])


def _ffn(x, modx, w1, w3, w2, layer, g, b, alpha, tm, tf):
    m, d = x.shape
    resident = lambda a: pl.BlockSpec((None,) + a.shape[1:], lambda i: (layer, 0, 0), pipeline_mode=pl.Buffered(1))
    return pl.pallas_call(
        functools.partial(_ffn_kernel, alpha=alpha, tf=tf),
        grid=(m // tm,),
        in_specs=[pl.BlockSpec((tm, d), lambda i: (i, 0)),
                  pl.BlockSpec((tm // SUB, 8, d), lambda i: (i, 0, 0)),
                  resident(w1), resident(w3), resident(w2),
                  pl.BlockSpec((1, d), lambda i: (0, 0)),
                  pl.BlockSpec((1, d), lambda i: (0, 0))],
        out_specs=pl.BlockSpec((tm, d), lambda i: (i, 0)),
        out_shape=jax.ShapeDtypeStruct((m, d), F32),
        compiler_params=_cp(("parallel",)),
        name="dense_swiglu",
    )(x, modx, w1, w3, w2, g.reshape(1, d), b.reshape(1, d))


def _route_kernel(x_ref, mod_ref, wr_ref, u_ref, lg_ref):
    u = _modulate(x_ref[...], mod_ref, 3, 4)
    lg_ref[...] = jnp.dot(u, wr_ref[...], preferred_element_type=F32, precision=lax.Precision.HIGHEST)
    half = u.shape[1] // 2
    bits = lax.bitcast_convert_type(u.astype(BF16).astype(F32), jnp.uint32)
    u_ref[...] = lax.bitcast_convert_type(bits[:, :half] | (bits[:, half:] >> 16), F32)


def _route(x, modx, wr_pad, tm):
    m, d = x.shape
    return pl.pallas_call(
        _route_kernel,
        grid=(m // tm,),
        in_specs=[pl.BlockSpec((tm, d), lambda i: (i, 0)),
                  pl.BlockSpec((tm // SUB, 8, d), lambda i: (i, 0, 0)),
                  pl.BlockSpec((d, LANE), lambda i: (0, 0))],
        out_specs=[pl.BlockSpec((tm, d // 2), lambda i: (i, 0)), pl.BlockSpec((tm, LANE), lambda i: (i, 0))],
        out_shape=[jax.ShapeDtypeStruct((m, d // 2), F32), jax.ShapeDtypeStruct((m, LANE), F32)],
        compiler_params=_cp(("parallel",)),
        name="moe_router",
    )(x, modx, wr_pad)


def _moe_kernel(be_ref, nu_ref, xb_ref, w1_ref, w3_ref, w2_ref, o_ref, acc_ref, u_ref):
    i, j = pl.program_id(0), pl.program_id(1)
    live = i < nu_ref[0]

    @pl.when(j == 0)
    def _():
        acc_ref[...] = jnp.zeros_like(acc_ref)
        words = lax.bitcast_convert_type(xb_ref[...], jnp.uint32)
        half = words.shape[1]
        hi = lax.bitcast_convert_type(words & jnp.uint32(0xFFFF0000), F32)
        lo = lax.bitcast_convert_type(words << 16, F32)
        u_ref[:, :half] = hi.astype(BF16)
        u_ref[:, half:] = lo.astype(BF16)

    @pl.when(live)
    def _():
        u = u_ref[...]
        h1 = _dot(u, w1_ref[0])
        h3 = _dot(u, w3_ref[0])
        acc_ref[...] += _dot((h1 * jax.nn.sigmoid(h1) * h3).astype(BF16), w2_ref[0])

    @pl.when(j == pl.num_programs(1) - 1)
    def _():
        o_ref[...] = acc_ref[...]


def _moe_experts(xb, blk_e, n_used, w1, w3, w2, layer, tm, tf):
    r = xb.shape[0]
    d = w1.shape[2]
    ff = w1.shape[3]
    nj = ff // tf

    def wmap(i, j, be, nu):
        return (layer, be[i], 0, jnp.where(i < nu[0], j, nj - 1))

    def w2map(i, j, be, nu):
        return (layer, be[i], jnp.where(i < nu[0], j, nj - 1), 0)

    grid_spec = pltpu.PrefetchScalarGridSpec(
        num_scalar_prefetch=2,
        grid=(r // tm, nj),
        in_specs=[pl.BlockSpec((tm, d // 2), lambda i, j, be, nu: (i, 0)),
                  pl.BlockSpec((None, 1, d, tf), wmap),
                  pl.BlockSpec((None, 1, d, tf), wmap),
                  pl.BlockSpec((None, 1, tf, d), w2map)],
        out_specs=pl.BlockSpec((tm, d), lambda i, j, be, nu: (i, 0)),
        scratch_shapes=[pltpu.VMEM((tm, d), F32), pltpu.VMEM((tm, d), BF16)],
    )
    return pl.pallas_call(
        _moe_kernel,
        grid_spec=grid_spec,
        out_shape=jax.ShapeDtypeStruct((r, d), F32),
        compiler_params=_cp(("arbitrary", "arbitrary")),
        name="moe_experts",
    )(blk_e, n_used, xb, w1, w3, w2)


def _ln_res_kernel(x_ref, f0_ref, f1_ref, wt_ref, mod_ref, g_ref, b_ref, o_ref, *, alpha):
    f = f0_ref[...] * wt_ref[:, 0:1] + f1_ref[...] * wt_ref[:, 1:2]
    o_ref[...] = _gated_residual_ln(x_ref[...], f, mod_ref, 5, alpha, g_ref[...], b_ref[...])


def _ln_res(x, f01, wts, modx, g, b, alpha, tm):
    m, d = x.shape
    nt = m // tm
    return pl.pallas_call(
        functools.partial(_ln_res_kernel, alpha=alpha),
        grid=(nt,),
        in_specs=[pl.BlockSpec((tm, d), lambda i: (i, 0)),
                  pl.BlockSpec((tm, d), lambda i: (i, 0)),
                  pl.BlockSpec((tm, d), lambda i: (i + nt, 0)),
                  pl.BlockSpec((tm, TOP_K), lambda i: (i, 0)),
                  pl.BlockSpec((tm // SUB, 8, d), lambda i: (i, 0, 0)),
                  pl.BlockSpec((1, d), lambda i: (0, 0)), pl.BlockSpec((1, d), lambda i: (0, 0))],
        out_specs=pl.BlockSpec((tm, d), lambda i: (i, 0)),
        out_shape=jax.ShapeDtypeStruct((m, d), F32),
        compiler_params=_cp(("parallel",)),
        name="moe_residual_ln",
    )(x, f01, f01, wts, modx, g.reshape(1, d), b.reshape(1, d))


def _moe(x, modx, router, w1, w3, w2, layer, g, b, alpha, tm_rows, tm, tf):
    m, d = x.shape
    wr_pad = jnp.zeros((d, LANE), F32).at[:, :N_EXPERTS].set(router)
    u, logits = _route(x, modx, wr_pad, tm_rows)
    top_v, top_i = lax.top_k(logits[:, :N_EXPERTS], TOP_K)
    wts = jax.nn.softmax(top_v, axis=-1)
    e_flat = top_i.reshape(-1).astype(jnp.int32)
    n_pair = m * TOP_K
    onehot = (e_flat[:, None] == jnp.arange(N_EXPERTS, dtype=jnp.int32)[None, :]).astype(jnp.int32)
    csum = jnp.cumsum(onehot, axis=0)
    counts = csum[-1]
    rank = jnp.take_along_axis(csum, e_flat[:, None], axis=1)[:, 0] - 1
    padded = (counts + tm - 1) // tm * tm
    pad_end = jnp.cumsum(padded)
    pad_start = pad_end - padded
    dest = pad_start[e_flat] + rank
    n_blk = -(-n_pair // tm) + N_EXPERTS
    r = n_blk * tm
    tok = jnp.arange(n_pair, dtype=jnp.int32) // TOP_K
    n_used = (pad_end[-1] // tm).astype(jnp.int32)
    blk = jnp.minimum(jnp.arange(n_blk, dtype=jnp.int32), n_used - 1) * tm
    blk_e = jnp.minimum(jnp.searchsorted(pad_end, blk, side='right'), N_EXPERTS - 1).astype(jnp.int32)
    _, tok_sorted = lax.sort_key_val(dest, tok)
    shift = pad_start - (jnp.cumsum(counts) - counts)
    src = jnp.arange(r, dtype=jnp.int32) - jnp.repeat(shift[blk_e], tm).astype(jnp.int32)
    row_tok = jnp.take(tok_sorted, jnp.clip(src, 0, n_pair - 1), mode="clip")
    xb = jnp.take(u, row_tok, axis=0, mode="clip")
    yb = _moe_experts(xb, blk_e, n_used.reshape(1), w1, w3, w2, layer, tm, tf)
    f01 = jnp.take(yb, dest.reshape(m, TOP_K).T.reshape(-1), axis=0, mode="clip")
    return _ln_res(x, f01, wts, modx, g, b, alpha, tm_rows)


def _rope_tables(nb, lat_len, ctx_len):
    rows = lat_len // GRID_W
    row = jnp.repeat(jnp.arange(rows, dtype=F32), GRID_W)
    col = jnp.tile(jnp.arange(GRID_W, dtype=F32), rows)
    n_freq = HEAD_DIM // 4
    inv = ROPE_THETA ** (-jnp.arange(n_freq, dtype=F32) / n_freq)
    ar, ac = row[:, None] * inv, col[:, None] * inv
    cos = jnp.concatenate([jnp.cos(ar), jnp.cos(ar), jnp.cos(ac), jnp.cos(ac)], axis=1)
    sin = jnp.concatenate([-jnp.sin(ar), jnp.sin(ar), -jnp.sin(ac), jnp.sin(ac)], axis=1)
    cos = jnp.concatenate([cos, jnp.ones((ctx_len, HEAD_DIM), F32)], axis=0)
    sin = jnp.concatenate([sin, jnp.zeros((ctx_len, HEAD_DIM), F32)], axis=0)
    return jnp.tile(cos, (nb, 1)), jnp.tile(sin, (nb, 1))


def _row_tile(m, pref):
    tm = pref
    while m % tm:
        tm //= 2
    return tm


def kernel(x, c, ctx, c_ctx, w_mod, b_mod, w_in, conv_a_w, conv_a_b, ret_decay, hy_conv_w, hy_conv_b, hy_w1, hy_b1, hy_w2, hy_b2, hy_w3, hy_freq, hy_skip, q_norm, k_norm, w_gate, b_gate, w_branch, w_o, ln_g, ln_b, ffn_w1, ffn_w3, ffn_w2, router, moe_w1, moe_w3, moe_w2):
    nb, lat_len, d = x.shape
    ctx_len = ctx.shape[1]
    depth = w_mod.shape[0]
    t = lat_len + ctx_len
    m = nb * t
    alpha = float((2 * depth) ** 0.25)
    assert lat_len % SUB == 0 and ctx_len % SUB == 0 and nb % 2 == 0

    act = jnp.concatenate([jax.nn.silu(c), jax.nn.silu(c_ctx)[None, :],
                           jnp.zeros((8 - nb - 1, d), F32)], axis=0)
    mods = _mod_all(act, w_mod, b_mod).reshape(depth, 8, N_MOD, d)
    mods = jnp.pad(mods, ((0, 0), (0, 0), (0, 8 - N_MOD), (0, 0)))
    tiles_per_batch = t // SUB
    tile_pos = np.arange(m // SUB) % tiles_per_batch
    tile_group = np.where(tile_pos < lat_len // SUB, np.arange(m // SUB) // tiles_per_batch, nb)
    cos_t, sin_t = _rope_tables(nb, lat_len, ctx_len)

    cast = lambda a: a.astype(BF16)
    w_in_b, w_gate_b, w_branch_b, w_o_b = cast(w_in), cast(w_gate), cast(w_branch), cast(w_o)
    ffn_b = (cast(ffn_w1), cast(ffn_w3), cast(ffn_w2))
    moe_w = (cast(moe_w1), cast(moe_w3), cast(moe_w2))

    rows = jnp.concatenate([x, ctx], axis=1).reshape(m, d)
    tm = _row_tile(m, 512)
    for l in range(depth):
        modx = mods[l][tile_group]
        za, zvg, zh, rq, rk, aq, ak, av = _proj(rows, modx, w_in_b, l, cos_t, sin_t, q_norm[l], k_norm[l], tm)
        y_d = _attention(aq, ak, av, nb, lat_len)
        log_gamma = -jnp.exp(ret_decay[l].astype(F32))
        y_r = _retention(rq, rk, zvg, _retention_tables(log_gamma), nb, lat_len)
        y_a = _shortconv(za, conv_a_w[l], conv_a_b[l], nb, lat_len)
        y_h = _hyena(zh, hy_conv_w[l], hy_conv_b[l],
                     (hy_w1[l], hy_b1[l], hy_w2[l], hy_b2[l], hy_w3[l], hy_freq[l]), hy_skip[l], nb, lat_len)
        rows = _merge(rows, modx, (y_a, y_r, y_h, y_d), w_gate_b, b_gate[l], w_branch_b, w_o_b, l,
                      ln_g[l, 0], ln_b[l, 0], alpha, tm)
        i = l // 2
        if l % 2 == 0:
            rows = _ffn(rows, modx, *ffn_b, i, ln_g[l, 1], ln_b[l, 1], alpha, tm, 256)
        else:
            rows = _moe(rows, modx, router[i], *moe_w, i, ln_g[l, 1], ln_b[l, 1], alpha, tm, 512, 1792)
    return rows.reshape(nb, t, d)[:, :lat_len]
```

```python
import functools
import math

import numpy as np
import jax
import jax.numpy as jnp
from jax import lax
from jax.experimental import pallas as pl
from jax.experimental.pallas import tpu as pltpu

F32 = jnp.float32
BF16 = jnp.bfloat16

GRID_W = 64
BRANCH_W = 512
RET_HEADS = 4
RET_D = 128
RET_CHUNK = 128
HY_EMB = 33
HY_FAST_PCT = 0.3
HY_SLOW_PCT = 1.5
HY_TARGET = 1e-2
ATT_HEADS = 4
ATT_KV_HEADS = 2
HEAD_DIM = 128
ROPE_THETA = 10000.0
N_EXPERTS = 8
TOP_K = 2
N_MOD = 6
A_W = 3 * BRANCH_W
R_W = 4 * RET_HEADS * RET_D
H_W = 3 * BRANCH_W
D_W = (ATT_HEADS + 2 * ATT_KV_HEADS) * HEAD_DIM
IN_W = A_W + R_W + H_W + D_W
LN_EPS = 1e-6

SUB = 256
LANE = 128
FFT_N1 = 128
RET_UNROLL = 8
ATT_TQ = 1024
ATT_KC = 1024
DFT_PASSES = 1
DFT_FILTER_PASSES = 1
VMEM_LIMIT = 56 * 1024 * 1024


def _cp(sem, vmem=VMEM_LIMIT):
    return pltpu.CompilerParams(dimension_semantics=sem, vmem_limit_bytes=vmem)


def _dot(a, b):
    return jnp.dot(a, b, preferred_element_type=F32)


def _dot_nt(a, b):
    return lax.dot_general(a, b, (((1,), (1,)), ((), ())), preferred_element_type=F32)


def _dot_tn(a, b):
    return lax.dot_general(a, b, (((0,), (0,)), ((), ())), preferred_element_type=F32)


def _split_dot(f_cat, x, passes):
    xh = x.astype(BF16)
    if passes == 1:
        return _dot(f_cat, xh)
    m = f_cat.shape[0] // 2
    r = _dot(f_cat, xh)
    r = r[:m] + r[m:]
    if passes == 3:
        r = r + _dot(f_cat[:m], (x - xh.astype(F32)).astype(BF16))
    return r


def _dft_operand(f_cat, passes):
    return f_cat[..., :f_cat.shape[-2] // 2, :] if passes == 1 else f_cat


def _modulate(x, mod_ref, shift_row, scale_row):
    parts = []
    for s in range(x.shape[0] // SUB):
        sh = mod_ref[s, shift_row:shift_row + 1, :]
        sc = mod_ref[s, scale_row:scale_row + 1, :]
        parts.append(x[s * SUB:(s + 1) * SUB] * (1.0 + sc) + sh)
    return parts[0] if len(parts) == 1 else jnp.concatenate(parts, axis=0)


def _gated_residual_ln(x, f, mod_ref, gate_row, alpha, g, b):
    parts = []
    for s in range(x.shape[0] // SUB):
        gt = mod_ref[s, gate_row:gate_row + 1, :]
        parts.append(alpha * x[s * SUB:(s + 1) * SUB] + gt * f[s * SUB:(s + 1) * SUB])
    r = parts[0] if len(parts) == 1 else jnp.concatenate(parts, axis=0)
    mu = jnp.mean(r, axis=-1, keepdims=True)
    rc = r - mu
    var = jnp.mean(rc * rc, axis=-1, keepdims=True)
    return rc * lax.rsqrt(var + LN_EPS) * g + b


def _mod_kernel(a_ref, w_ref, b_ref, o_ref):
    o_ref[0] = _dot(a_ref[...].astype(BF16), w_ref[0].astype(BF16)) + b_ref[0]


def _mod_all(act, w_mod, b_mod):
    depth, d, n = w_mod.shape
    tn = 512
    return pl.pallas_call(
        _mod_kernel,
        grid=(depth, n // tn),
        in_specs=[pl.BlockSpec((8, d), lambda l, j: (0, 0)),
                  pl.BlockSpec((1, d, tn), lambda l, j: (l, 0, j)),
                  pl.BlockSpec((1, 1, tn), lambda l, j: (l, 0, j))],
        out_specs=pl.BlockSpec((1, 8, tn), lambda l, j: (l, 0, j)),
        out_shape=jax.ShapeDtypeStruct((depth, 8, n), F32),
        compiler_params=_cp(("parallel", "parallel")),
        name="mod_vectors",
    )(act, w_mod, b_mod.reshape(depth, 1, n))


PROJ_TN = 512
assert (A_W, R_W, H_W, D_W) == (3 * PROJ_TN, 4 * PROJ_TN, 3 * PROJ_TN, 2 * PROJ_TN)


def _proj_kernel(x_ref, mod_ref, w_ref, cos_ref, sin_ref, gq_ref, gk_ref,
                 za_ref, zvg_ref, zh_ref, rq_ref, rk_ref, aq_ref, ak_ref, av_ref):
    u = _modulate(x_ref[...], mod_ref, 0, 1).astype(BF16)
    tn = PROJ_TN
    cos, sin = cos_ref[...], sin_ref[...]
    lane = lax.broadcasted_iota(jnp.int32, cos.shape, 1)
    first_half = (lane % 64) < 32

    def tile(j):
        return _dot(u, w_ref[:, j * tn:(j + 1) * tn])

    def rope(x):
        partner = jnp.where(first_half, pltpu.roll(x, 96, 1), pltpu.roll(x, 32, 1))
        return x * cos + partner * sin

    def rms(x, g):
        return x * lax.rsqrt(jnp.mean(x * x, axis=-1, keepdims=True) + LN_EPS) * g

    head = lambda h: slice(h * LANE, (h + 1) * LANE)

    for j in range(3):
        za_ref[:, j * tn:(j + 1) * tn] = tile(j)
    r = tile(3)
    for h in range(RET_HEADS):
        rq_ref[:, head(h)] = rope(r[:, head(h)])
    r = tile(4)
    for h in range(RET_HEADS):
        rk_ref[:, head(h)] = rope(r[:, head(h)] * (RET_D ** -0.5))
    for j in range(2):
        zvg_ref[:, j * tn:(j + 1) * tn] = tile(5 + j)
    for j in range(3):
        zh_ref[:, j * tn:(j + 1) * tn] = tile(7 + j)
    q_scale = (HEAD_DIM ** -0.5) * math.log2(math.e)
    r = tile(10)
    for h in range(ATT_HEADS):
        aq_ref[:, head(h)] = (rope(rms(r[:, head(h)], gq_ref[...])) * q_scale).astype(BF16)
    r = tile(11)
    ones_col = jnp.where(lane == 0, 1.0, 0.0).astype(BF16)
    for h in range(ATT_KV_HEADS):
        ak_ref[:, head(h)] = rope(rms(r[:, head(h)], gk_ref[...])).astype(BF16)
        av_ref[:, head(2 * h)] = r[:, head(ATT_KV_HEADS + h)].astype(BF16)
        av_ref[:, head(2 * h + 1)] = ones_col


def _proj(x, modx, w, layer, cos_t, sin_t, gq, gk, tm):
    m, d = x.shape
    n = w.shape[2]
    tn = PROJ_TN
    qw, kw = ATT_HEADS * HEAD_DIM, ATT_KV_HEADS * HEAD_DIM
    row = lambda i: (i, 0)
    widths = (A_W, 2 * tn, H_W, tn, tn, qw, kw, 2 * kw)
    dtypes = (F32,) * 5 + (BF16,) * 3
    return pl.pallas_call(
        _proj_kernel,
        grid=(m // tm,),
        in_specs=[pl.BlockSpec((tm, d), row),
                  pl.BlockSpec((tm // SUB, 8, d), lambda i: (i, 0, 0)),
                  pl.BlockSpec((None, d, n), lambda i: (layer, 0, 0), pipeline_mode=pl.Buffered(1)),
                  pl.BlockSpec((tm, LANE), row), pl.BlockSpec((tm, LANE), row),
                  pl.BlockSpec((1, LANE), lambda i: (0, 0)), pl.BlockSpec((1, LANE), lambda i: (0, 0))],
        out_specs=[pl.BlockSpec((tm, wd), row) for wd in widths],
        out_shape=[jax.ShapeDtypeStruct((m, wd), dt) for wd, dt in zip(widths, dtypes)],
        compiler_params=_cp(("parallel",)),
        name="in_proj",
    )(x, modx, w, cos_t, sin_t, gq.reshape(1, LANE), gk.reshape(1, LANE))


def _attn_kernel(q_ref, k_ref, v_ref, *rest, key_lo, key_hi, kc):
    o_ref = rest[-1]
    tq = q_ref.shape[1]
    grp = ATT_HEADS // ATT_KV_HEADS
    q2 = jnp.concatenate([q_ref[0, :, g * HEAD_DIM:(g + 1) * HEAD_DIM] for g in range(grp)], axis=0)
    m_run, acc = None, None
    for lo in range(key_lo, key_hi, kc):
        hi = min(lo + kc, key_hi)
        s = _dot_nt(q2, k_ref[0, lo:hi, :])
        m_chunk = jnp.max(s, axis=-1, keepdims=True)
        m_new = m_chunk if m_run is None else jnp.maximum(m_run, m_chunk)
        pv = _dot(jnp.exp2(s - m_new).astype(BF16), v_ref[0, lo:hi, :])
        acc = pv if acc is None else jnp.exp2(m_run - m_new) * acc + pv
        m_run = m_new
    for g in range(grp):
        a = acc[g * tq:(g + 1) * tq]
        o_ref[0, :, g * HEAD_DIM:(g + 1) * HEAD_DIM] = a[:, :HEAD_DIM] / a[:, HEAD_DIM:HEAD_DIM + 1]


def _attention(aq, ak, av, nb, lat_len):
    m = aq.shape[0]
    t = m // nb
    lc = t - lat_len
    grp_w = (ATT_HEADS // ATT_KV_HEADS) * HEAD_DIM
    args = (aq.reshape(nb, t, -1), ak.reshape(nb, t, -1), av.reshape(nb, t, -1))

    def call(tq, n_tiles, first_tile, key_lo, key_hi, prev):
        qmap = lambda b, kv, i: (b, first_tile + i, kv)
        specs = [pl.BlockSpec((1, tq, grp_w), qmap),
                 pl.BlockSpec((1, t, HEAD_DIM), lambda b, kv, i: (b, 0, kv)),
                 pl.BlockSpec((1, t, 2 * HEAD_DIM), lambda b, kv, i: (b, 0, kv))]
        extra, alias = (), {}
        if prev is not None:
            specs.append(pl.BlockSpec(memory_space=pl.ANY))
            extra, alias = (prev,), {3: 0}
        return pl.pallas_call(
            functools.partial(_attn_kernel, key_lo=key_lo, key_hi=key_hi, kc=ATT_KC),
            grid=(nb, ATT_KV_HEADS, n_tiles),
            in_specs=specs,
            out_specs=pl.BlockSpec((1, tq, grp_w), qmap),
            out_shape=jax.ShapeDtypeStruct((nb, t, ATT_HEADS * HEAD_DIM), F32),
            input_output_aliases=alias,
            compiler_params=_cp(("parallel", "parallel", "arbitrary")),
            name="gqa_attention",
        )(*args, *extra)

    tq = math.gcd(ATT_TQ, lat_len)
    out = call(tq, lat_len // tq, 0, 0, t, None)
    out = call(lc, 1, t // lc - 1, lat_len, t, out)
    return out.reshape(m, -1)


def _ret_kernel(q_ref, k_ref, v_ref, g_ref, intra_ref, qd_ref, kd_ref, cd_ref, o_ref, st_ref,
                *, n_lat, n_ctx):
    c_len = RET_CHUNK
    n = n_lat + n_ctx

    def rows(c):
        return pl.ds(pl.multiple_of(c * c_len, c_len), c_len)

    def kv(d, c):
        return _dot_tn((k_ref[0, rows(c), :] * kd_ref[d, 0]).astype(BF16), v_ref[0, rows(c), :].astype(BF16))

    def scan_step(t, carry):
        sf, sb = carry
        cf = jnp.where(t < n_ctx, n_lat + t, t - n_ctx)
        cb = jnp.where(t < n_ctx, n - 1 - t, n_lat - 1 - (t - n_ctx))
        st_ref[cf, :RET_D, :] = sf.astype(BF16)
        st_ref[cb, RET_D:, :] = sb.astype(BF16)
        return (cd_ref[0, 0, 0:1, :] * sf + kv(0, cf), cd_ref[1, 0, 0:1, :] * sb + kv(1, cb))

    zero = jnp.zeros((RET_D, RET_D), F32)
    lax.fori_loop(0, n, scan_step, (zero, zero), unroll=RET_UNROLL)

    mask = intra_ref[0, 0] + intra_ref[1, 0]

    def out_step(c, carry):
        qc = q_ref[0, rows(c), :]
        vc = v_ref[0, rows(c), :].astype(BF16)
        s = _dot_nt(qc.astype(BF16), k_ref[0, rows(c), :].astype(BF16)) * mask
        qdec = jnp.concatenate([qc * qd_ref[0, 0], qc * qd_ref[1, 0]], axis=1).astype(BF16)
        o = _dot(s.astype(BF16), vc) + _dot(qdec, st_ref[c])
        mu = jnp.mean(o, axis=-1, keepdims=True)
        oc = o - mu
        var = jnp.mean(oc * oc, axis=-1, keepdims=True)
        g = g_ref[0, rows(c), :]
        o_ref[0, rows(c), :] = (g * jax.nn.sigmoid(g)) * (oc * lax.rsqrt(var + LN_EPS))
        return carry

    lax.fori_loop(0, n, out_step, 0, unroll=RET_UNROLL)


def _retention(rq, rk, z, tabs, nb, lat_len):
    m = rq.shape[0]
    t = m // nb
    intra, qd, kd, cd = tabs
    v_blk, g_blk = 0, RET_HEADS
    kern = functools.partial(_ret_kernel, n_lat=lat_len // RET_CHUNK, n_ctx=(t - lat_len) // RET_CHUNK)
    seq = lambda off: pl.BlockSpec((1, t, LANE), lambda b, h: (b, 0, off + h))
    tab = lambda r: pl.BlockSpec((2, 1, r, LANE), lambda b, h: (0, h, 0, 0))
    out = pl.pallas_call(
        kern,
        grid=(nb, RET_HEADS),
        in_specs=[seq(0), seq(0), seq(v_blk), seq(g_blk),
                  tab(RET_CHUNK), tab(RET_CHUNK), tab(RET_CHUNK), tab(8)],
        out_specs=seq(0),
        out_shape=jax.ShapeDtypeStruct((nb, t, RET_HEADS * RET_D), F32),
        scratch_shapes=[pltpu.VMEM((t // RET_CHUNK, 2 * RET_D, RET_D), BF16)],
        compiler_params=_cp(("parallel", "parallel")),
        name="retention",
    )(rq.reshape(nb, t, -1), rk.reshape(nb, t, -1), z.reshape(nb, t, -1), z.reshape(nb, t, -1), intra, qd, kd, cd)
    return out.reshape(m, -1)


def _retention_tables(log_gamma):
    c = RET_CHUNK
    i = jnp.arange(c, dtype=F32)
    lg = log_gamma.astype(F32)[:, :, None]
    rel = i[:, None] - i[None, :]
    intra_f = jnp.where(rel >= 0, jnp.exp(lg[0][..., None] * jnp.maximum(rel, 0.0)), 0.0)
    intra_b = jnp.where(rel <= 0, jnp.exp(lg[1][..., None] * jnp.maximum(-rel, 0.0)), 0.0)
    qd = jnp.stack([jnp.exp(lg[0] * (i + 1.0)), jnp.exp(lg[1] * (c - i))])
    kd = jnp.stack([jnp.exp(lg[0] * (c - 1.0 - i)), jnp.exp(lg[1] * i)])
    cd = jnp.exp(lg * c)
    bc = lambda a: jnp.broadcast_to(a[..., None], a.shape + (LANE,))
    return (jnp.stack([intra_f, intra_b]), bc(qd), bc(kd),
            jnp.broadcast_to(cd[..., None], (2, RET_HEADS, 8, LANE)))


def _shift_rows(x, lat_len):
    t = x.shape[0]
    row = lax.broadcasted_iota(jnp.int32, x.shape, 0)
    xm = jnp.where((row == 0) | (row == lat_len), 0.0, pltpu.roll(x, 1, 0))
    xp = jnp.where((row == lat_len - 1) | (row == t - 1), 0.0, pltpu.roll(x, t - 1, 0))
    return xm, xp


def _shortconv_kernel(bg_ref, cg_ref, xv_ref, w_ref, b_ref, o_ref, *, lat_len):
    p = cg_ref[0] * xv_ref[0]
    pm, pp = _shift_rows(p, lat_len)
    o_ref[0] = bg_ref[0] * (pm * w_ref[0:1, :] + p * w_ref[1:2, :] + pp * w_ref[2:3, :] + b_ref[...])


def _shortconv(z, w, b, nb, lat_len):
    m = z.shape[0]
    t = m // nb
    nct = BRANCH_W // LANE
    kern = functools.partial(_shortconv_kernel, lat_len=lat_len)
    seq = lambda off: pl.BlockSpec((1, t, LANE), lambda bb, j: (bb, 0, off + j))
    out = pl.pallas_call(
        kern,
        grid=(nb, nct),
        in_specs=[seq(0), seq(nct), seq(2 * nct),
                  pl.BlockSpec((3, LANE), lambda bb, j: (0, j)),
                  pl.BlockSpec((1, LANE), lambda bb, j: (0, j))],
        out_specs=seq(0),
        out_shape=jax.ShapeDtypeStruct((nb, t, BRANCH_W), F32),
        compiler_params=_cp(("parallel", "parallel")),
        name="short_conv",
    )(z.reshape(nb, t, -1), z.reshape(nb, t, -1), z.reshape(nb, t, -1), w, b.reshape(1, -1))
    return out.reshape(m, -1)


def _hyconv_kernel(z_ref, w_ref, b_ref, ol_ref, oc_ref, *, lat_len):
    x = z_ref[0]
    xm, xp = _shift_rows(x, lat_len)
    y = xm * w_ref[0:1, :] + x * w_ref[1:2, :] + xp * w_ref[2:3, :] + b_ref[...]
    ol_ref[0, 0] = y[:lat_len]
    oc_ref[0, 0] = y[lat_len:]


def _hyconv(z, w, b, nb, lat_len):
    m = z.shape[0]
    t = m // nb
    nct = BRANCH_W // LANE
    off = 0
    kern = functools.partial(_hyconv_kernel, lat_len=lat_len)
    return pl.pallas_call(
        kern,
        grid=(nb, 3 * nct),
        in_specs=[pl.BlockSpec((1, t, LANE), lambda bb, j: (bb, 0, off + j)),
                  pl.BlockSpec((3, LANE), lambda bb, j: (0, j)),
                  pl.BlockSpec((1, LANE), lambda bb, j: (0, j))],
        out_specs=[pl.BlockSpec((1, 1, lat_len, LANE), lambda bb, j: (j // nct, bb, 0, j % nct)),
                   pl.BlockSpec((1, 1, t - lat_len, LANE), lambda bb, j: (j // nct, bb, 0, j % nct))],
        out_shape=[jax.ShapeDtypeStruct((3, nb, lat_len, BRANCH_W), F32),
                   jax.ShapeDtypeStruct((3, nb, t - lat_len, BRANCH_W), F32)],
        compiler_params=_cp(("parallel", "parallel")),
        name="hyena_dwconv",
    )(z.reshape(nb, t, -1), w, b.reshape(1, -1))


def _stack_c(m):
    return np.block([[m.real, -m.imag], [m.imag, m.real]])


def _split_np(m):
    m32 = np.asarray(m, np.float32)
    hi = m32.astype(BF16)
    lo = (m32 - hi.astype(np.float32)).astype(BF16)
    return np.concatenate([hi, lo], axis=-2)


@functools.lru_cache(maxsize=None)
def _dft_tables(lat_len):
    n = 2 * lat_len
    n1, n2 = FFT_N1, 2 * lat_len // FFT_N1
    a1 = np.arange(n1)
    a2 = np.arange(n2)
    w1 = np.exp(-2j * np.pi * np.outer(a1, a1[:n1 // 2]) / n1)
    f1 = _stack_c(w1)
    w2 = np.exp(-2j * np.pi * np.outer(a2, a2) / n2)
    tw = np.exp(-2j * np.pi * np.outer(a1, a2) / n)
    cf = w2[None, :, :] * tw[:, None, :]
    f2 = np.stack([_stack_c(cf[k]) for k in range(n1)])
    g2 = np.stack([_stack_c(cf[k].conj().T) for k in range(n1)])
    w1i = np.exp(2j * np.pi * np.outer(a1[:n1 // 2], a1) / n1) / n
    f3 = _stack_c(w1i)
    f1_real = f1[:, :n1 // 2]
    return _split_np(f1), _split_np(f1_real), _split_np(f2), _split_np(g2), _split_np(f3)


@functools.lru_cache(maxsize=None)
def _dft_tables_small(ctx_len):
    n = 2 * ctx_len
    a = np.arange(n)
    ff = _stack_c(np.exp(-2j * np.pi * np.outer(a, a[:ctx_len]) / n))
    fi = _stack_c(np.exp(2j * np.pi * np.outer(a[:ctx_len], a) / n) / n)
    return _split_np(ff), _split_np(ff[:, :ctx_len]), _split_np(fi)


def _fft1_kernel(x_ref, f_ref, o_ref, *, passes):
    f = f_ref[...]
    for j in range(x_ref.shape[3]):
        parts = [x_ref[0, b, :, j, :] for b in range(x_ref.shape[1])]
        r = _split_dot(f, parts[0] if len(parts) == 1 else jnp.concatenate(parts, axis=0), passes)
        h = r.shape[0] // 2
        o_ref[0, 0, :, j, :] = r[:h]
        o_ref[0, 1, :, j, :] = r[h:]


def _fft1(x, sel, f1, jb, passes):
    _, p, mem, r, n2, w = x.shape
    f1 = _dft_operand(f1, passes)
    n1 = f1.shape[0] // (2 if passes == 1 else 4)
    return pl.pallas_call(
        functools.partial(_fft1_kernel, passes=passes),
        grid=(p, n2 // jb),
        in_specs=[pl.BlockSpec((None, 1, mem, r, jb, w), lambda pp, j: (sel, pp, 0, 0, j, 0)),
                  pl.BlockSpec(f1.shape, lambda pp, j: (0, 0))],
        out_specs=pl.BlockSpec((1, 2, n1, jb, w), lambda pp, j: (pp, 0, 0, j, 0)),
        out_shape=jax.ShapeDtypeStruct((p, 2, n1, n2, w), F32),
        compiler_params=_cp(("parallel", "parallel")),
        name="hyena_fft_stage1",
    )(x, f1)


def _fft2f_kernel(a_ref, f_ref, o_ref, *, kb, n2, passes):
    w = a_ref.shape[-1]
    for t in range(kb):
        xf = _split_dot(f_ref[t], a_ref[0, :, t].reshape(2 * n2, w), passes)
        xb = _split_dot(f_ref[t], a_ref[1, :, t].reshape(2 * n2, w), passes)
        o_ref[0, 0, t] = xf[:n2] + xb[:n2]
        o_ref[0, 1, t] = xf[n2:] - xb[n2:]


def _fft2_filters(a, f2, kb, passes):
    q, _, n1, n2, w = a.shape
    f2 = _dft_operand(f2, passes)
    return pl.pallas_call(
        functools.partial(_fft2f_kernel, kb=kb, n2=n2, passes=passes),
        grid=(q // 2, n1 // kb),
        in_specs=[pl.BlockSpec((2, 2, kb, n2, w), lambda o, k: (o, 0, k, 0, 0)),
                  pl.BlockSpec((kb,) + f2.shape[1:], lambda o, k: (k, 0, 0))],
        out_specs=pl.BlockSpec((1, 2, kb, n2, w), lambda o, k: (o, 0, k, 0, 0)),
        out_shape=jax.ShapeDtypeStruct((q // 2, 2, n1, n2, w), F32),
        compiler_params=_cp(("parallel", "parallel")),
        name="hyena_filter_spectrum",
    )(a, f2)


def _fft2_kernel(a_ref, h_ref, f_ref, g_ref, o_ref, *, kb, n2, passes):
    w = a_ref.shape[-1]
    for t in range(kb):
        a = a_ref[0, :, t].reshape(2 * n2, w)
        x = _split_dot(f_ref[t], a, passes)
        xr, xi = x[:n2], x[n2:]
        hr, hi = h_ref[0, t], h_ref[1, t]
        y = jnp.concatenate([xr * hr - xi * hi, xr * hi + xi * hr], axis=0)
        o_ref[0, :, t] = _split_dot(g_ref[t], y, passes).reshape(2, n2, w)


def _fft2(a, hspec, order, f2, g2, kb, passes):
    p, _, n1, n2, w = a.shape
    f2, g2 = _dft_operand(f2, passes), _dft_operand(g2, passes)
    kern = functools.partial(_fft2_kernel, kb=kb, n2=n2, passes=passes)
    return pl.pallas_call(
        kern,
        grid=(n1 // kb, p),
        in_specs=[pl.BlockSpec((1, 2, kb, n2, w), lambda k, pp: (pp, 0, k, 0, 0)),
                  pl.BlockSpec((None, 2, kb, n2, w), lambda k, pp: (order, 0, k, 0, 0)),
                  pl.BlockSpec((kb,) + f2.shape[1:], lambda k, pp: (k, 0, 0)),
                  pl.BlockSpec((kb,) + g2.shape[1:], lambda k, pp: (k, 0, 0))],
        out_specs=pl.BlockSpec((1, 2, kb, n2, w), lambda k, pp: (pp, 0, k, 0, 0)),
        out_shape=jax.ShapeDtypeStruct(a.shape, F32),
        compiler_params=_cp(("parallel", "arbitrary")),
        name="hyena_fft_stage2",
    )(a, hspec, f2, g2)


def _fft3_kernel(b_ref, f_ref, gate_ref, prev_ref, skip_ref, o_ref, *, passes):
    f = f_ref[...]
    r = gate_ref.shape[2]
    for j in range(b_ref.shape[3]):
        bj = jnp.concatenate([b_ref[0, 0, :, j, :], b_ref[0, 1, :, j, :]], axis=0)
        conv = _split_dot(f, bj, passes)
        for mem in range(2):
            cm = conv[mem * r:(mem + 1) * r]
            o_ref[0, mem, :, j, :] = gate_ref[0, mem, :, j, :] * (cm + skip_ref[...] * prev_ref[0, mem, :, j, :])


def _fft3(bm, f3, gate, gate_sel, prev, prev_sel, skip, out_rows, jb, passes):
    p, _, n1, n2, w = bm.shape
    r = n1 // 2
    f3 = _dft_operand(f3, passes)
    seq = lambda s: pl.BlockSpec((None, 1, 2, r, jb, w), lambda pp, j: (s, pp, 0, 0, j, 0))
    return pl.pallas_call(
        functools.partial(_fft3_kernel, passes=passes),
        grid=(p, n2 // jb),
        in_specs=[pl.BlockSpec((1, 2, n1, jb, w), lambda pp, j: (pp, 0, 0, j, 0)),
                  pl.BlockSpec(f3.shape, lambda pp, j: (0, 0)),
                  seq(gate_sel), seq(prev_sel),
                  pl.BlockSpec((1, w), lambda pp, j: (0, 0))],
        out_specs=pl.BlockSpec((1, 2, r, jb, w), lambda pp, j: (pp, 0, 0, j, 0)),
        out_shape=jax.ShapeDtypeStruct((p, 2, out_rows, n2, w), F32),
        compiler_params=_cp(("parallel", "parallel")),
        name="hyena_fft_stage3",
    )(bm, f3, gate, prev, skip)


def _hyctx_kernel(v_ref, x1_ref, x2_ref, filt_ref, ff_ref, ffr_ref, fi_ref, skip_ref, yh_hbm_ref, o_ref,
                  *, passes, filter_passes):
    del yh_hbm_ref
    lc, w = v_ref.shape[2], v_ref.shape[3]
    n = 2 * lc
    y = v_ref[0].reshape(2 * lc, w)
    for o, gate_ref in enumerate((x1_ref, x2_ref)):
        sf = _split_dot(ffr_ref[...], filt_ref[o, 0], filter_passes)
        sb = _split_dot(ffr_ref[...], filt_ref[o, 1], filter_passes)
        hr, hi = sf[:n] + sb[:n], sf[n:] - sb[n:]
        x = _split_dot(ff_ref[...], y, passes)
        xr, xi = x[:n], x[n:]
        spec = jnp.concatenate([xr * hr - xi * hi, xr * hi + xi * hr], axis=0)
        conv = _split_dot(fi_ref[...], spec, passes)
        y = gate_ref[0].reshape(2 * lc, w) * (conv + skip_ref[o:o + 1, :] * y)
    o_ref[0] = y.reshape(2, lc, w)


def _hyena_ctx(hc, filt_c, ff, ffr, fi, skip, yh, passes, filter_passes):
    _, nb, lc, w = hc.shape
    p = nb // 2
    t = yh.shape[2]
    ff, fi, ffr = _dft_operand(ff, passes), _dft_operand(fi, passes), _dft_operand(ffr, filter_passes)
    part = pl.BlockSpec((1, 2, lc, w), lambda pp: (pp, 0, 0, 0))
    hc4 = hc.reshape(3, p, 2, lc, w)
    return pl.pallas_call(
        functools.partial(_hyctx_kernel, passes=passes, filter_passes=filter_passes),
        grid=(p,),
        in_specs=[part, part, part,
                  pl.BlockSpec(filt_c.shape, lambda pp: (0, 0, 0, 0)),
                  pl.BlockSpec(ff.shape, lambda pp: (0, 0)),
                  pl.BlockSpec(ffr.shape, lambda pp: (0, 0)),
                  pl.BlockSpec(fi.shape, lambda pp: (0, 0)),
                  pl.BlockSpec(skip.shape, lambda pp: (0, 0)),
                  pl.BlockSpec(memory_space=pl.ANY)],
        out_specs=pl.BlockSpec((1, 2, lc, w), lambda pp: (pp, 0, t // lc - 1, 0)),
        out_shape=jax.ShapeDtypeStruct(yh.shape, F32),
        input_output_aliases={8: 0},
        compiler_params=_cp(("parallel",)),
        name="hyena_ctx",
    )(hc4[0], hc4[1], hc4[2], filt_c, ff, ffr, fi, skip, yh)


def _hyena_filters(l, w1, b1, w2, b2, w3, freq):
    t = jnp.linspace(0.0, 1.0, l, dtype=F32)[:, None]
    bands = (HY_EMB - 1) // 2
    w = 2.0 * math.pi * jnp.arange(l, dtype=F32) / l
    f = jnp.linspace(1e-4, bands - 1, bands, dtype=F32)
    ang = w[:, None] * f[None, :]
    feats = jnp.concatenate([t, jnp.cos(ang), -jnp.sin(ang)], axis=-1)
    hdn = jnp.sin(freq[0] * (feats @ w1 + b1))
    hdn = jnp.sin(freq[1] * (hdn @ w2 + b2))
    n_out = w3.shape[-1]
    deltas = jnp.linspace(math.log(HY_TARGET) / HY_SLOW_PCT, math.log(HY_TARGET) / HY_FAST_PCT, n_out, dtype=F32)
    decay = jnp.exp(-t * jnp.abs(deltas))
    pos = jnp.arange(l)[:, None]
    parts = []
    for q in range(n_out // BRANCH_W):
        sl = slice(q * BRANCH_W, (q + 1) * BRANCH_W)
        f = (hdn @ w3[:, sl]).astype(F32) * decay[:, sl]
        parts.append(jnp.where(pos == 0, 0.0, f) if q % 2 else f)
    return jnp.stack(parts).reshape(n_out // (2 * BRANCH_W), 2, l, BRANCH_W)


def _hyena(z, conv_w, conv_b, filt_w, skip, nb, lat_len):
    m = z.shape[0]
    t = m // nb
    lc = t - lat_len
    w = BRANCH_W
    p = nb // 2
    n1, n2 = FFT_N1, 2 * lat_len // FFT_N1
    r = n1 // 2
    hl, hc = _hyconv(z, conv_w, conv_b, nb, lat_len)
    f1, f1r, f2, g2, f3 = _dft_tables(lat_len)
    ff, ffr, fi = _dft_tables_small(lc)
    jb, kb = min(16, n2), 16
    dp, fp = DFT_PASSES, DFT_FILTER_PASSES
    filt_l = _hyena_filters(lat_len, *filt_w)
    fa = _fft1(filt_l.reshape(1, 4, 1, r, n2, w), 0, f1r, jb, fp)
    hs_l = _fft2_filters(fa, f2, kb, fp)
    hl6 = hl.reshape(3, p, 2, r, n2, w)
    y, y_sel = hl6, 0
    for o in range(2):
        a = _fft1(y, y_sel, f1, jb, dp)
        bm = _fft2(a, hs_l, o, f2, g2, kb, dp)
        rows = r if o == 0 else t // n2
        y = _fft3(bm, f3, hl6, 1 + o, y, y_sel, skip[o:o + 1], rows, jb, dp)[None]
        y_sel = 0
    yh = y.reshape(p, 2, t, w)
    yh = _hyena_ctx(hc, _hyena_filters(lc, *filt_w), ff, ffr, fi, skip, yh, dp, fp)
    return yh.reshape(m, w)


def _merge_kernel(x_ref, mod_ref, ya_ref, yr_ref, yh_ref, yd_ref, wg_ref, bg_ref, wb_ref, wo_ref, g_ref, b_ref,
                  o_ref, *, alpha):
    x = x_ref[...]
    u = _modulate(x, mod_ref, 0, 1).astype(BF16)
    acc = None
    for i, y_ref in enumerate((ya_ref, yr_ref, yh_ref, yd_ref)):
        gate = jax.nn.sigmoid(_dot(u, wg_ref[i]) + bg_ref[i:i + 1, :])
        term = gate * _dot(y_ref[...].astype(BF16), wb_ref[i])
        acc = term if acc is None else acc + term
    mix = _dot(acc.astype(BF16), wo_ref[...])
    o_ref[...] = _gated_residual_ln(x, mix, mod_ref, 2, alpha, g_ref[...], b_ref[...])


def _merge(x, modx, ys, wg, bg, wb, wo, layer, g, b, alpha, tm):
    m, d = x.shape
    w = BRANCH_W
    row = lambda i: (i, 0)
    const2 = lambda i: (0, 0)
    one = pl.Buffered(1)
    stacked = lambda a: pl.BlockSpec((None,) + a.shape[1:], lambda i: (layer,) + (0,) * (a.ndim - 1),
                                     pipeline_mode=one)
    return pl.pallas_call(
        functools.partial(_merge_kernel, alpha=alpha),
        grid=(m // tm,),
        in_specs=[pl.BlockSpec((tm, d), row),
                  pl.BlockSpec((tm // SUB, 8, d), lambda i: (i, 0, 0)),
                  pl.BlockSpec((tm, w), row), pl.BlockSpec((tm, w), row),
                  pl.BlockSpec((tm, w), row), pl.BlockSpec((tm, w), row),
                  stacked(wg), pl.BlockSpec(bg.shape, const2), stacked(wb), stacked(wo),
                  pl.BlockSpec((1, d), const2), pl.BlockSpec((1, d), const2)],
        out_specs=pl.BlockSpec((tm, d), row),
        out_shape=jax.ShapeDtypeStruct((m, d), F32),
        compiler_params=_cp(("parallel",)),
        name="branch_merge",
    )(x, modx, *ys, wg, bg, wb, wo, g.reshape(1, d), b.reshape(1, d))


def _ffn_kernel(x_ref, mod_ref, w1_ref, w3_ref, w2_ref, g_ref, b_ref, o_ref, *, alpha, tf):
    x = x_ref[...]
    u = _modulate(x, mod_ref, 3, 4).astype(BF16)
    acc = None
    for j in range(w1_ref.shape[1] // tf):
        sl = slice(j * tf, (j + 1) * tf)
        h1 = _dot(u, w1_ref[:, sl])
        h3 = _dot(u, w3_ref[:, sl])
        part = _dot((h1 * jax.nn.sigmoid(h1) * h3).astype(BF16), w2_ref[sl, :])
        acc = part if acc is None else acc + part
    o_ref[...] = _gated_residual_ln(x, acc, mod_ref, 5, alpha, g_ref[...], b_ref[...])


def _ffn(x, modx, w1, w3, w2, layer, g, b, alpha, tm, tf):
    m, d = x.shape
    resident = lambda a: pl.BlockSpec((None,) + a.shape[1:], lambda i: (layer, 0, 0), pipeline_mode=pl.Buffered(1))
    return pl.pallas_call(
        functools.partial(_ffn_kernel, alpha=alpha, tf=tf),
        grid=(m // tm,),
        in_specs=[pl.BlockSpec((tm, d), lambda i: (i, 0)),
                  pl.BlockSpec((tm // SUB, 8, d), lambda i: (i, 0, 0)),
                  resident(w1), resident(w3), resident(w2),
                  pl.BlockSpec((1, d), lambda i: (0, 0)),
                  pl.BlockSpec((1, d), lambda i: (0, 0))],
        out_specs=pl.BlockSpec((tm, d), lambda i: (i, 0)),
        out_shape=jax.ShapeDtypeStruct((m, d), F32),
        compiler_params=_cp(("parallel",)),
        name="dense_swiglu",
    )(x, modx, w1, w3, w2, g.reshape(1, d), b.reshape(1, d))


def _route_kernel(x_ref, mod_ref, wr_ref, u_ref, lg_ref):
    u = _modulate(x_ref[...], mod_ref, 3, 4)
    lg_ref[...] = jnp.dot(u, wr_ref[...], preferred_element_type=F32, precision=lax.Precision.HIGHEST)
    half = u.shape[1] // 2
    bits = lax.bitcast_convert_type(u.astype(BF16).astype(F32), jnp.uint32)
    u_ref[...] = lax.bitcast_convert_type(bits[:, :half] | (bits[:, half:] >> 16), F32)


def _route(x, modx, wr_pad, tm):
    m, d = x.shape
    return pl.pallas_call(
        _route_kernel,
        grid=(m // tm,),
        in_specs=[pl.BlockSpec((tm, d), lambda i: (i, 0)),
                  pl.BlockSpec((tm // SUB, 8, d), lambda i: (i, 0, 0)),
                  pl.BlockSpec((d, LANE), lambda i: (0, 0))],
        out_specs=[pl.BlockSpec((tm, d // 2), lambda i: (i, 0)), pl.BlockSpec((tm, LANE), lambda i: (i, 0))],
        out_shape=[jax.ShapeDtypeStruct((m, d // 2), F32), jax.ShapeDtypeStruct((m, LANE), F32)],
        compiler_params=_cp(("parallel",)),
        name="moe_router",
    )(x, modx, wr_pad)


def _moe_kernel(be_ref, nu_ref, xb_ref, w1_ref, w3_ref, w2_ref, o_ref, acc_ref, u_ref):
    i, j = pl.program_id(0), pl.program_id(1)
    live = i < nu_ref[0]

    @pl.when(j == 0)
    def _():
        acc_ref[...] = jnp.zeros_like(acc_ref)
        words = lax.bitcast_convert_type(xb_ref[...], jnp.uint32)
        half = words.shape[1]
        hi = lax.bitcast_convert_type(words & jnp.uint32(0xFFFF0000), F32)
        lo = lax.bitcast_convert_type(words << 16, F32)
        u_ref[:, :half] = hi.astype(BF16)
        u_ref[:, half:] = lo.astype(BF16)

    @pl.when(live)
    def _():
        u = u_ref[...]
        h1 = _dot(u, w1_ref[0])
        h3 = _dot(u, w3_ref[0])
        acc_ref[...] += _dot((h1 * jax.nn.sigmoid(h1) * h3).astype(BF16), w2_ref[0])

    @pl.when(j == pl.num_programs(1) - 1)
    def _():
        o_ref[...] = acc_ref[...]


def _moe_experts(xb, blk_e, n_used, w1, w3, w2, layer, tm, tf):
    r = xb.shape[0]
    d = w1.shape[2]
    ff = w1.shape[3]
    nj = ff // tf

    def wmap(i, j, be, nu):
        return (layer, be[i], 0, jnp.where(i < nu[0], j, nj - 1))

    def w2map(i, j, be, nu):
        return (layer, be[i], jnp.where(i < nu[0], j, nj - 1), 0)

    grid_spec = pltpu.PrefetchScalarGridSpec(
        num_scalar_prefetch=2,
        grid=(r // tm, nj),
        in_specs=[pl.BlockSpec((tm, d // 2), lambda i, j, be, nu: (i, 0)),
                  pl.BlockSpec((None, 1, d, tf), wmap),
                  pl.BlockSpec((None, 1, d, tf), wmap),
                  pl.BlockSpec((None, 1, tf, d), w2map)],
        out_specs=pl.BlockSpec((tm, d), lambda i, j, be, nu: (i, 0)),
        scratch_shapes=[pltpu.VMEM((tm, d), F32), pltpu.VMEM((tm, d), BF16)],
    )
    return pl.pallas_call(
        _moe_kernel,
        grid_spec=grid_spec,
        out_shape=jax.ShapeDtypeStruct((r, d), F32),
        compiler_params=_cp(("arbitrary", "arbitrary")),
        name="moe_experts",
    )(blk_e, n_used, xb, w1, w3, w2)


def _ln_res_kernel(x_ref, f0_ref, f1_ref, wt_ref, mod_ref, g_ref, b_ref, o_ref, *, alpha):
    f = f0_ref[...] * wt_ref[:, 0:1] + f1_ref[...] * wt_ref[:, 1:2]
    o_ref[...] = _gated_residual_ln(x_ref[...], f, mod_ref, 5, alpha, g_ref[...], b_ref[...])


def _ln_res(x, f01, wts, modx, g, b, alpha, tm):
    m, d = x.shape
    nt = m // tm
    return pl.pallas_call(
        functools.partial(_ln_res_kernel, alpha=alpha),
        grid=(nt,),
        in_specs=[pl.BlockSpec((tm, d), lambda i: (i, 0)),
                  pl.BlockSpec((tm, d), lambda i: (i, 0)),
                  pl.BlockSpec((tm, d), lambda i: (i + nt, 0)),
                  pl.BlockSpec((tm, TOP_K), lambda i: (i, 0)),
                  pl.BlockSpec((tm // SUB, 8, d), lambda i: (i, 0, 0)),
                  pl.BlockSpec((1, d), lambda i: (0, 0)), pl.BlockSpec((1, d), lambda i: (0, 0))],
        out_specs=pl.BlockSpec((tm, d), lambda i: (i, 0)),
        out_shape=jax.ShapeDtypeStruct((m, d), F32),
        compiler_params=_cp(("parallel",)),
        name="moe_residual_ln",
    )(x, f01, f01, wts, modx, g.reshape(1, d), b.reshape(1, d))


def _moe(x, modx, router, w1, w3, w2, layer, g, b, alpha, tm_rows, tm, tf):
    m, d = x.shape
    wr_pad = jnp.zeros((d, LANE), F32).at[:, :N_EXPERTS].set(router)
    u, logits = _route(x, modx, wr_pad, tm_rows)
    top_v, top_i = lax.top_k(logits[:, :N_EXPERTS], TOP_K)
    wts = jax.nn.softmax(top_v, axis=-1)
    e_flat = top_i.reshape(-1).astype(jnp.int32)
    n_pair = m * TOP_K
    onehot = (e_flat[:, None] == jnp.arange(N_EXPERTS, dtype=jnp.int32)[None, :]).astype(jnp.int32)
    csum = jnp.cumsum(onehot, axis=0)
    counts = csum[-1]
    rank = jnp.take_along_axis(csum, e_flat[:, None], axis=1)[:, 0] - 1
    padded = (counts + tm - 1) // tm * tm
    pad_end = jnp.cumsum(padded)
    pad_start = pad_end - padded
    dest = pad_start[e_flat] + rank
    n_blk = -(-n_pair // tm) + N_EXPERTS
    r = n_blk * tm
    tok = jnp.arange(n_pair, dtype=jnp.int32) // TOP_K
    n_used = (pad_end[-1] // tm).astype(jnp.int32)
    blk = jnp.minimum(jnp.arange(n_blk, dtype=jnp.int32), n_used - 1) * tm
    blk_e = jnp.minimum(jnp.searchsorted(pad_end, blk, side='right'), N_EXPERTS - 1).astype(jnp.int32)
    _, tok_sorted = lax.sort_key_val(dest, tok)
    shift = pad_start - (jnp.cumsum(counts) - counts)
    src = jnp.arange(r, dtype=jnp.int32) - jnp.repeat(shift[blk_e], tm).astype(jnp.int32)
    row_tok = jnp.take(tok_sorted, jnp.clip(src, 0, n_pair - 1), mode="clip")
    xb = jnp.take(u, row_tok, axis=0, mode="clip")
    yb = _moe_experts(xb, blk_e, n_used.reshape(1), w1, w3, w2, layer, tm, tf)
    f01 = jnp.take(yb, dest.reshape(m, TOP_K).T.reshape(-1), axis=0, mode="clip")
    return _ln_res(x, f01, wts, modx, g, b, alpha, tm_rows)


def _rope_tables(nb, lat_len, ctx_len):
    rows = lat_len // GRID_W
    row = jnp.repeat(jnp.arange(rows, dtype=F32), GRID_W)
    col = jnp.tile(jnp.arange(GRID_W, dtype=F32), rows)
    n_freq = HEAD_DIM // 4
    inv = ROPE_THETA ** (-jnp.arange(n_freq, dtype=F32) / n_freq)
    ar, ac = row[:, None] * inv, col[:, None] * inv
    cos = jnp.concatenate([jnp.cos(ar), jnp.cos(ar), jnp.cos(ac), jnp.cos(ac)], axis=1)
    sin = jnp.concatenate([-jnp.sin(ar), jnp.sin(ar), -jnp.sin(ac), jnp.sin(ac)], axis=1)
    cos = jnp.concatenate([cos, jnp.ones((ctx_len, HEAD_DIM), F32)], axis=0)
    sin = jnp.concatenate([sin, jnp.zeros((ctx_len, HEAD_DIM), F32)], axis=0)
    return jnp.tile(cos, (nb, 1)), jnp.tile(sin, (nb, 1))


def _row_tile(m, pref):
    tm = pref
    while m % tm:
        tm //= 2
    return tm


def kernel(x, c, ctx, c_ctx, w_mod, b_mod, w_in, conv_a_w, conv_a_b, ret_decay, hy_conv_w, hy_conv_b, hy_w1, hy_b1, hy_w2, hy_b2, hy_w3, hy_freq, hy_skip, q_norm, k_norm, w_gate, b_gate, w_branch, w_o, ln_g, ln_b, ffn_w1, ffn_w3, ffn_w2, router, moe_w1, moe_w3, moe_w2):
    nb, lat_len, d = x.shape
    ctx_len = ctx.shape[1]
    depth = w_mod.shape[0]
    t = lat_len + ctx_len
    m = nb * t
    alpha = float((2 * depth) ** 0.25)
    assert lat_len % SUB == 0 and ctx_len % SUB == 0 and nb % 2 == 0

    act = jnp.concatenate([jax.nn.silu(c), jax.nn.silu(c_ctx)[None, :],
                           jnp.zeros((8 - nb - 1, d), F32)], axis=0)
    mods = _mod_all(act, w_mod, b_mod).reshape(depth, 8, N_MOD, d)
    mods = jnp.pad(mods, ((0, 0), (0, 0), (0, 8 - N_MOD), (0, 0)))
    tiles_per_batch = t // SUB
    tile_pos = np.arange(m // SUB) % tiles_per_batch
    tile_group = np.where(tile_pos < lat_len // SUB, np.arange(m // SUB) // tiles_per_batch, nb)
    cos_t, sin_t = _rope_tables(nb, lat_len, ctx_len)

    cast = lambda a: a.astype(BF16)
    w_in_b, w_gate_b, w_branch_b, w_o_b = cast(w_in), cast(w_gate), cast(w_branch), cast(w_o)
    ffn_b = (cast(ffn_w1), cast(ffn_w3), cast(ffn_w2))
    moe_w = (cast(moe_w1), cast(moe_w3), cast(moe_w2))

    rows = jnp.concatenate([x, ctx], axis=1).reshape(m, d)
    tm = _row_tile(m, 512)
    for l in range(depth):
        modx = mods[l][tile_group]
        za, zvg, zh, rq, rk, aq, ak, av = _proj(rows, modx, w_in_b, l, cos_t, sin_t, q_norm[l], k_norm[l], tm)
        y_d = _attention(aq, ak, av, nb, lat_len)
        log_gamma = -jnp.exp(ret_decay[l].astype(F32))
        y_r = _retention(rq, rk, zvg, _retention_tables(log_gamma), nb, lat_len)
        y_a = _shortconv(za, conv_a_w[l], conv_a_b[l], nb, lat_len)
        y_h = _hyena(zh, hy_conv_w[l], hy_conv_b[l],
                     (hy_w1[l], hy_b1[l], hy_w2[l], hy_b2[l], hy_w3[l], hy_freq[l]), hy_skip[l], nb, lat_len)
        rows = _merge(rows, modx, (y_a, y_r, y_h, y_d), w_gate_b, b_gate[l], w_branch_b, w_o_b, l,
                      ln_g[l, 0], ln_b[l, 0], alpha, tm)
        i = l // 2
        if l % 2 == 0:
            rows = _ffn(rows, modx, *ffn_b, i, ln_g[l, 1], ln_b[l, 1], alpha, tm, 256)
        else:
            rows = _moe(rows, modx, router[i], *moe_w, i, ln_g[l, 1], ln_b[l, 1], alpha, tm, 1024, 896)
    return rows.reshape(nb, t, d)[:, :lat_len]
```

```python
import functools
import math

import numpy as np
import jax
import jax.numpy as jnp
from jax import lax
from jax.experimental import pallas as pl
from jax.experimental.pallas import tpu as pltpu

F32 = jnp.float32
BF16 = jnp.bfloat16

GRID_W = 64
BRANCH_W = 512
RET_HEADS = 4
RET_D = 128
RET_CHUNK = 128
HY_EMB = 33
HY_FAST_PCT = 0.3
HY_SLOW_PCT = 1.5
HY_TARGET = 1e-2
ATT_HEADS = 4
ATT_KV_HEADS = 2
HEAD_DIM = 128
ROPE_THETA = 10000.0
N_EXPERTS = 8
TOP_K = 2
N_MOD = 6
A_W = 3 * BRANCH_W
R_W = 4 * RET_HEADS * RET_D
H_W = 3 * BRANCH_W
D_W = (ATT_HEADS + 2 * ATT_KV_HEADS) * HEAD_DIM
IN_W = A_W + R_W + H_W + D_W
LN_EPS = 1e-6

SUB = 256
LANE = 128
FFT_N1 = 128
RET_UNROLL = 8
ATT_TQ = 1024
ATT_KC = 1024
DFT_PASSES = 1
DFT_FILTER_PASSES = 1
VMEM_LIMIT = 56 * 1024 * 1024


def _cp(sem, vmem=VMEM_LIMIT):
    return pltpu.CompilerParams(dimension_semantics=sem, vmem_limit_bytes=vmem)


def _dot(a, b):
    return jnp.dot(a, b, preferred_element_type=F32)


def _dot_nt(a, b):
    return lax.dot_general(a, b, (((1,), (1,)), ((), ())), preferred_element_type=F32)


def _dot_tn(a, b):
    return lax.dot_general(a, b, (((0,), (0,)), ((), ())), preferred_element_type=F32)


def _split_dot(f_cat, x, passes):
    xh = x.astype(BF16)
    if passes == 1:
        return _dot(f_cat, xh)
    m = f_cat.shape[0] // 2
    r = _dot(f_cat, xh)
    r = r[:m] + r[m:]
    if passes == 3:
        r = r + _dot(f_cat[:m], (x - xh.astype(F32)).astype(BF16))
    return r


def _dft_operand(f_cat, passes):
    return f_cat[..., :f_cat.shape[-2] // 2, :] if passes == 1 else f_cat


def _modulate(x, mod_ref, shift_row, scale_row):
    parts = []
    for s in range(x.shape[0] // SUB):
        sh = mod_ref[s, shift_row:shift_row + 1, :]
        sc = mod_ref[s, scale_row:scale_row + 1, :]
        parts.append(x[s * SUB:(s + 1) * SUB] * (1.0 + sc) + sh)
    return parts[0] if len(parts) == 1 else jnp.concatenate(parts, axis=0)


def _gated_residual_ln(x, f, mod_ref, gate_row, alpha, g, b):
    parts = []
    for s in range(x.shape[0] // SUB):
        gt = mod_ref[s, gate_row:gate_row + 1, :]
        parts.append(alpha * x[s * SUB:(s + 1) * SUB] + gt * f[s * SUB:(s + 1) * SUB])
    r = parts[0] if len(parts) == 1 else jnp.concatenate(parts, axis=0)
    mu = jnp.mean(r, axis=-1, keepdims=True)
    rc = r - mu
    var = jnp.mean(rc * rc, axis=-1, keepdims=True)
    return rc * lax.rsqrt(var + LN_EPS) * g + b


def _mod_kernel(a_ref, w_ref, b_ref, o_ref):
    o_ref[0] = _dot(a_ref[...].astype(BF16), w_ref[0].astype(BF16)) + b_ref[0]


def _mod_all(act, w_mod, b_mod):
    depth, d, n = w_mod.shape
    tn = 512
    return pl.pallas_call(
        _mod_kernel,
        grid=(depth, n // tn),
        in_specs=[pl.BlockSpec((8, d), lambda l, j: (0, 0)),
                  pl.BlockSpec((1, d, tn), lambda l, j: (l, 0, j)),
                  pl.BlockSpec((1, 1, tn), lambda l, j: (l, 0, j))],
        out_specs=pl.BlockSpec((1, 8, tn), lambda l, j: (l, 0, j)),
        out_shape=jax.ShapeDtypeStruct((depth, 8, n), F32),
        compiler_params=_cp(("parallel", "parallel")),
        name="mod_vectors",
    )(act, w_mod, b_mod.reshape(depth, 1, n))


PROJ_TN = 512
assert (A_W, R_W, H_W, D_W) == (3 * PROJ_TN, 4 * PROJ_TN, 3 * PROJ_TN, 2 * PROJ_TN)


def _proj_kernel(x_ref, mod_ref, w_ref, cos_ref, sin_ref, gq_ref, gk_ref,
                 za_ref, zvg_ref, zh_ref, rq_ref, rk_ref, aq_ref, ak_ref, av_ref):
    u = _modulate(x_ref[...], mod_ref, 0, 1).astype(BF16)
    tn = PROJ_TN
    cos, sin = cos_ref[...], sin_ref[...]
    lane = lax.broadcasted_iota(jnp.int32, cos.shape, 1)
    first_half = (lane % 64) < 32

    def tile(j):
        return _dot(u, w_ref[:, j * tn:(j + 1) * tn])

    def rope(x):
        partner = jnp.where(first_half, pltpu.roll(x, 96, 1), pltpu.roll(x, 32, 1))
        return x * cos + partner * sin

    def rms(x, g):
        return x * lax.rsqrt(jnp.mean(x * x, axis=-1, keepdims=True) + LN_EPS) * g

    head = lambda h: slice(h * LANE, (h + 1) * LANE)

    for j in range(3):
        za_ref[:, j * tn:(j + 1) * tn] = tile(j)
    r = tile(3)
    for h in range(RET_HEADS):
        rq_ref[:, head(h)] = rope(r[:, head(h)])
    r = tile(4)
    for h in range(RET_HEADS):
        rk_ref[:, head(h)] = rope(r[:, head(h)] * (RET_D ** -0.5))
    for j in range(2):
        zvg_ref[:, j * tn:(j + 1) * tn] = tile(5 + j)
    for j in range(3):
        zh_ref[:, j * tn:(j + 1) * tn] = tile(7 + j)
    q_scale = (HEAD_DIM ** -0.5) * math.log2(math.e)
    r = tile(10)
    for h in range(ATT_HEADS):
        aq_ref[:, head(h)] = (rope(rms(r[:, head(h)], gq_ref[...])) * q_scale).astype(BF16)
    r = tile(11)
    ones_col = jnp.where(lane == 0, 1.0, 0.0).astype(BF16)
    for h in range(ATT_KV_HEADS):
        ak_ref[:, head(h)] = rope(rms(r[:, head(h)], gk_ref[...])).astype(BF16)
        av_ref[:, head(2 * h)] = r[:, head(ATT_KV_HEADS + h)].astype(BF16)
        av_ref[:, head(2 * h + 1)] = ones_col


def _proj(x, modx, w, layer, cos_t, sin_t, gq, gk, tm):
    m, d = x.shape
    n = w.shape[2]
    tn = PROJ_TN
    qw, kw = ATT_HEADS * HEAD_DIM, ATT_KV_HEADS * HEAD_DIM
    row = lambda i: (i, 0)
    widths = (A_W, 2 * tn, H_W, tn, tn, qw, kw, 2 * kw)
    dtypes = (F32,) * 5 + (BF16,) * 3
    return pl.pallas_call(
        _proj_kernel,
        grid=(m // tm,),
        in_specs=[pl.BlockSpec((tm, d), row),
                  pl.BlockSpec((tm // SUB, 8, d), lambda i: (i, 0, 0)),
                  pl.BlockSpec((None, d, n), lambda i: (layer, 0, 0), pipeline_mode=pl.Buffered(1)),
                  pl.BlockSpec((tm, LANE), row), pl.BlockSpec((tm, LANE), row),
                  pl.BlockSpec((1, LANE), lambda i: (0, 0)), pl.BlockSpec((1, LANE), lambda i: (0, 0))],
        out_specs=[pl.BlockSpec((tm, wd), row) for wd in widths],
        out_shape=[jax.ShapeDtypeStruct((m, wd), dt) for wd, dt in zip(widths, dtypes)],
        compiler_params=_cp(("parallel",)),
        name="in_proj",
    )(x, modx, w, cos_t, sin_t, gq.reshape(1, LANE), gk.reshape(1, LANE))


def _attn_kernel(q_ref, k_ref, v_ref, *rest, key_lo, key_hi, kc):
    o_ref = rest[-1]
    tq = q_ref.shape[1]
    grp = ATT_HEADS // ATT_KV_HEADS
    q2 = jnp.concatenate([q_ref[0, :, g * HEAD_DIM:(g + 1) * HEAD_DIM] for g in range(grp)], axis=0)
    m_run, acc = None, None
    for lo in range(key_lo, key_hi, kc):
        hi = min(lo + kc, key_hi)
        s = _dot_nt(q2, k_ref[0, lo:hi, :])
        m_chunk = jnp.max(s, axis=-1, keepdims=True)
        m_new = m_chunk if m_run is None else jnp.maximum(m_run, m_chunk)
        pv = _dot(jnp.exp2(s - m_new).astype(BF16), v_ref[0, lo:hi, :])
        acc = pv if acc is None else jnp.exp2(m_run - m_new) * acc + pv
        m_run = m_new
    for g in range(grp):
        a = acc[g * tq:(g + 1) * tq]
        o_ref[0, :, g * HEAD_DIM:(g + 1) * HEAD_DIM] = a[:, :HEAD_DIM] / a[:, HEAD_DIM:HEAD_DIM + 1]


def _attention(aq, ak, av, nb, lat_len):
    m = aq.shape[0]
    t = m // nb
    lc = t - lat_len
    grp_w = (ATT_HEADS // ATT_KV_HEADS) * HEAD_DIM
    args = (aq.reshape(nb, t, -1), ak.reshape(nb, t, -1), av.reshape(nb, t, -1))

    def call(tq, n_tiles, first_tile, key_lo, key_hi, prev):
        qmap = lambda b, kv, i: (b, first_tile + i, kv)
        specs = [pl.BlockSpec((1, tq, grp_w), qmap),
                 pl.BlockSpec((1, t, HEAD_DIM), lambda b, kv, i: (b, 0, kv)),
                 pl.BlockSpec((1, t, 2 * HEAD_DIM), lambda b, kv, i: (b, 0, kv))]
        extra, alias = (), {}
        if prev is not None:
            specs.append(pl.BlockSpec(memory_space=pl.ANY))
            extra, alias = (prev,), {3: 0}
        return pl.pallas_call(
            functools.partial(_attn_kernel, key_lo=key_lo, key_hi=key_hi, kc=ATT_KC),
            grid=(nb, ATT_KV_HEADS, n_tiles),
            in_specs=specs,
            out_specs=pl.BlockSpec((1, tq, grp_w), qmap),
            out_shape=jax.ShapeDtypeStruct((nb, t, ATT_HEADS * HEAD_DIM), F32),
            input_output_aliases=alias,
            compiler_params=_cp(("parallel", "parallel", "arbitrary")),
            name="gqa_attention",
        )(*args, *extra)

    tq = math.gcd(ATT_TQ, lat_len)
    out = call(tq, lat_len // tq, 0, 0, t, None)
    out = call(lc, 1, t // lc - 1, lat_len, t, out)
    return out.reshape(m, -1)


def _ret_kernel(q_ref, k_ref, v_ref, g_ref, intra_ref, qd_ref, kd_ref, cd_ref, o_ref, st_ref,
                *, n_lat, n_ctx):
    c_len = RET_CHUNK
    n = n_lat + n_ctx

    def rows(c):
        return pl.ds(pl.multiple_of(c * c_len, c_len), c_len)

    def kv(d, c):
        return _dot_tn((k_ref[0, rows(c), :] * kd_ref[d, 0]).astype(BF16), v_ref[0, rows(c), :].astype(BF16))

    def scan_step(t, carry):
        sf, sb = carry
        cf = jnp.where(t < n_ctx, n_lat + t, t - n_ctx)
        cb = jnp.where(t < n_ctx, n - 1 - t, n_lat - 1 - (t - n_ctx))
        st_ref[cf, :RET_D, :] = sf.astype(BF16)
        st_ref[cb, RET_D:, :] = sb.astype(BF16)
        return (cd_ref[0, 0, 0:1, :] * sf + kv(0, cf), cd_ref[1, 0, 0:1, :] * sb + kv(1, cb))

    zero = jnp.zeros((RET_D, RET_D), F32)
    lax.fori_loop(0, n, scan_step, (zero, zero), unroll=RET_UNROLL)

    mask = intra_ref[0, 0] + intra_ref[1, 0]

    def out_step(c, carry):
        qc = q_ref[0, rows(c), :]
        vc = v_ref[0, rows(c), :].astype(BF16)
        s = _dot_nt(qc.astype(BF16), k_ref[0, rows(c), :].astype(BF16)) * mask
        qdec = jnp.concatenate([qc * qd_ref[0, 0], qc * qd_ref[1, 0]], axis=1).astype(BF16)
        o = _dot(s.astype(BF16), vc) + _dot(qdec, st_ref[c])
        mu = jnp.mean(o, axis=-1, keepdims=True)
        oc = o - mu
        var = jnp.mean(oc * oc, axis=-1, keepdims=True)
        g = g_ref[0, rows(c), :]
        o_ref[0, rows(c), :] = (g * jax.nn.sigmoid(g)) * (oc * lax.rsqrt(var + LN_EPS))
        return carry

    lax.fori_loop(0, n, out_step, 0, unroll=RET_UNROLL)


def _retention(rq, rk, z, tabs, nb, lat_len):
    m = rq.shape[0]
    t = m // nb
    intra, qd, kd, cd = tabs
    v_blk, g_blk = 0, RET_HEADS
    kern = functools.partial(_ret_kernel, n_lat=lat_len // RET_CHUNK, n_ctx=(t - lat_len) // RET_CHUNK)
    seq = lambda off: pl.BlockSpec((1, t, LANE), lambda b, h: (b, 0, off + h))
    tab = lambda r: pl.BlockSpec((2, 1, r, LANE), lambda b, h: (0, h, 0, 0))
    out = pl.pallas_call(
        kern,
        grid=(nb, RET_HEADS),
        in_specs=[seq(0), seq(0), seq(v_blk), seq(g_blk),
                  tab(RET_CHUNK), tab(RET_CHUNK), tab(RET_CHUNK), tab(8)],
        out_specs=seq(0),
        out_shape=jax.ShapeDtypeStruct((nb, t, RET_HEADS * RET_D), F32),
        scratch_shapes=[pltpu.VMEM((t // RET_CHUNK, 2 * RET_D, RET_D), BF16)],
        compiler_params=_cp(("parallel", "parallel")),
        name="retention",
    )(rq.reshape(nb, t, -1), rk.reshape(nb, t, -1), z.reshape(nb, t, -1), z.reshape(nb, t, -1), intra, qd, kd, cd)
    return out.reshape(m, -1)


def _retention_tables(log_gamma):
    c = RET_CHUNK
    i = jnp.arange(c, dtype=F32)
    lg = log_gamma.astype(F32)[:, :, None]
    rel = i[:, None] - i[None, :]
    intra_f = jnp.where(rel >= 0, jnp.exp(lg[0][..., None] * jnp.maximum(rel, 0.0)), 0.0)
    intra_b = jnp.where(rel <= 0, jnp.exp(lg[1][..., None] * jnp.maximum(-rel, 0.0)), 0.0)
    qd = jnp.stack([jnp.exp(lg[0] * (i + 1.0)), jnp.exp(lg[1] * (c - i))])
    kd = jnp.stack([jnp.exp(lg[0] * (c - 1.0 - i)), jnp.exp(lg[1] * i)])
    cd = jnp.exp(lg * c)
    bc = lambda a: jnp.broadcast_to(a[..., None], a.shape + (LANE,))
    return (jnp.stack([intra_f, intra_b]), bc(qd), bc(kd),
            jnp.broadcast_to(cd[..., None], (2, RET_HEADS, 8, LANE)))


def _shift_rows(x, lat_len):
    t = x.shape[0]
    row = lax.broadcasted_iota(jnp.int32, x.shape, 0)
    xm = jnp.where((row == 0) | (row == lat_len), 0.0, pltpu.roll(x, 1, 0))
    xp = jnp.where((row == lat_len - 1) | (row == t - 1), 0.0, pltpu.roll(x, t - 1, 0))
    return xm, xp


def _shortconv_kernel(bg_ref, cg_ref, xv_ref, w_ref, b_ref, o_ref, *, lat_len):
    p = cg_ref[0] * xv_ref[0]
    pm, pp = _shift_rows(p, lat_len)
    o_ref[0] = bg_ref[0] * (pm * w_ref[0:1, :] + p * w_ref[1:2, :] + pp * w_ref[2:3, :] + b_ref[...])


def _shortconv(z, w, b, nb, lat_len):
    m = z.shape[0]
    t = m // nb
    nct = BRANCH_W // LANE
    kern = functools.partial(_shortconv_kernel, lat_len=lat_len)
    seq = lambda off: pl.BlockSpec((1, t, LANE), lambda bb, j: (bb, 0, off + j))
    out = pl.pallas_call(
        kern,
        grid=(nb, nct),
        in_specs=[seq(0), seq(nct), seq(2 * nct),
                  pl.BlockSpec((3, LANE), lambda bb, j: (0, j)),
                  pl.BlockSpec((1, LANE), lambda bb, j: (0, j))],
        out_specs=seq(0),
        out_shape=jax.ShapeDtypeStruct((nb, t, BRANCH_W), F32),
        compiler_params=_cp(("parallel", "parallel")),
        name="short_conv",
    )(z.reshape(nb, t, -1), z.reshape(nb, t, -1), z.reshape(nb, t, -1), w, b.reshape(1, -1))
    return out.reshape(m, -1)


def _hyconv_kernel(z_ref, w_ref, b_ref, ol_ref, oc_ref, *, lat_len):
    x = z_ref[0]
    xm, xp = _shift_rows(x, lat_len)
    y = xm * w_ref[0:1, :] + x * w_ref[1:2, :] + xp * w_ref[2:3, :] + b_ref[...]
    ol_ref[0, 0] = y[:lat_len]
    oc_ref[0, 0] = y[lat_len:]


def _hyconv(z, w, b, nb, lat_len):
    m = z.shape[0]
    t = m // nb
    nct = BRANCH_W // LANE
    off = 0
    kern = functools.partial(_hyconv_kernel, lat_len=lat_len)
    return pl.pallas_call(
        kern,
        grid=(nb, 3 * nct),
        in_specs=[pl.BlockSpec((1, t, LANE), lambda bb, j: (bb, 0, off + j)),
                  pl.BlockSpec((3, LANE), lambda bb, j: (0, j)),
                  pl.BlockSpec((1, LANE), lambda bb, j: (0, j))],
        out_specs=[pl.BlockSpec((1, 1, lat_len, LANE), lambda bb, j: (j // nct, bb, 0, j % nct)),
                   pl.BlockSpec((1, 1, t - lat_len, LANE), lambda bb, j: (j // nct, bb, 0, j % nct))],
        out_shape=[jax.ShapeDtypeStruct((3, nb, lat_len, BRANCH_W), F32),
                   jax.ShapeDtypeStruct((3, nb, t - lat_len, BRANCH_W), F32)],
        compiler_params=_cp(("parallel", "parallel")),
        name="hyena_dwconv",
    )(z.reshape(nb, t, -1), w, b.reshape(1, -1))


def _stack_c(m):
    return np.block([[m.real, -m.imag], [m.imag, m.real]])


def _split_np(m):
    m32 = np.asarray(m, np.float32)
    hi = m32.astype(BF16)
    lo = (m32 - hi.astype(np.float32)).astype(BF16)
    return np.concatenate([hi, lo], axis=-2)


@functools.lru_cache(maxsize=None)
def _dft_tables(lat_len):
    n = 2 * lat_len
    n1, n2 = FFT_N1, 2 * lat_len // FFT_N1
    a1 = np.arange(n1)
    a2 = np.arange(n2)
    w1 = np.exp(-2j * np.pi * np.outer(a1, a1[:n1 // 2]) / n1)
    f1 = _stack_c(w1)
    w2 = np.exp(-2j * np.pi * np.outer(a2, a2) / n2)
    tw = np.exp(-2j * np.pi * np.outer(a1, a2) / n)
    cf = w2[None, :, :] * tw[:, None, :]
    f2 = np.stack([_stack_c(cf[k]) for k in range(n1)])
    g2 = np.stack([_stack_c(cf[k].conj().T) for k in range(n1)])
    w1i = np.exp(2j * np.pi * np.outer(a1[:n1 // 2], a1) / n1) / n
    f3 = _stack_c(w1i)
    f1_real = f1[:, :n1 // 2]
    return _split_np(f1), _split_np(f1_real), _split_np(f2), _split_np(g2), _split_np(f3)


@functools.lru_cache(maxsize=None)
def _dft_tables_small(ctx_len):
    n = 2 * ctx_len
    a = np.arange(n)
    ff = _stack_c(np.exp(-2j * np.pi * np.outer(a, a[:ctx_len]) / n))
    fi = _stack_c(np.exp(2j * np.pi * np.outer(a[:ctx_len], a) / n) / n)
    return _split_np(ff), _split_np(ff[:, :ctx_len]), _split_np(fi)


def _fft1_kernel(x_ref, f_ref, o_ref, *, passes):
    f = f_ref[...]
    for j in range(x_ref.shape[3]):
        parts = [x_ref[0, b, :, j, :] for b in range(x_ref.shape[1])]
        r = _split_dot(f, parts[0] if len(parts) == 1 else jnp.concatenate(parts, axis=0), passes)
        h = r.shape[0] // 2
        o_ref[0, 0, :, j, :] = r[:h]
        o_ref[0, 1, :, j, :] = r[h:]


def _fft1(x, sel, f1, jb, passes):
    _, p, mem, r, n2, w = x.shape
    f1 = _dft_operand(f1, passes)
    n1 = f1.shape[0] // (2 if passes == 1 else 4)
    return pl.pallas_call(
        functools.partial(_fft1_kernel, passes=passes),
        grid=(p, n2 // jb),
        in_specs=[pl.BlockSpec((None, 1, mem, r, jb, w), lambda pp, j: (sel, pp, 0, 0, j, 0)),
                  pl.BlockSpec(f1.shape, lambda pp, j: (0, 0))],
        out_specs=pl.BlockSpec((1, 2, n1, jb, w), lambda pp, j: (pp, 0, 0, j, 0)),
        out_shape=jax.ShapeDtypeStruct((p, 2, n1, n2, w), F32),
        compiler_params=_cp(("parallel", "parallel")),
        name="hyena_fft_stage1",
    )(x, f1)


def _fft2f_kernel(a_ref, f_ref, o_ref, *, kb, n2, passes):
    w = a_ref.shape[-1]
    for t in range(kb):
        xf = _split_dot(f_ref[t], a_ref[0, :, t].reshape(2 * n2, w), passes)
        xb = _split_dot(f_ref[t], a_ref[1, :, t].reshape(2 * n2, w), passes)
        o_ref[0, 0, t] = xf[:n2] + xb[:n2]
        o_ref[0, 1, t] = xf[n2:] - xb[n2:]


def _fft2_filters(a, f2, kb, passes):
    q, _, n1, n2, w = a.shape
    f2 = _dft_operand(f2, passes)
    return pl.pallas_call(
        functools.partial(_fft2f_kernel, kb=kb, n2=n2, passes=passes),
        grid=(q // 2, n1 // kb),
        in_specs=[pl.BlockSpec((2, 2, kb, n2, w), lambda o, k: (o, 0, k, 0, 0)),
                  pl.BlockSpec((kb,) + f2.shape[1:], lambda o, k: (k, 0, 0))],
        out_specs=pl.BlockSpec((1, 2, kb, n2, w), lambda o, k: (o, 0, k, 0, 0)),
        out_shape=jax.ShapeDtypeStruct((q // 2, 2, n1, n2, w), F32),
        compiler_params=_cp(("parallel", "parallel")),
        name="hyena_filter_spectrum",
    )(a, f2)


def _fft2_kernel(a_ref, h_ref, f_ref, g_ref, o_ref, *, kb, n2, passes):
    w = a_ref.shape[-1]
    for t in range(kb):
        a = a_ref[0, :, t].reshape(2 * n2, w)
        x = _split_dot(f_ref[t], a, passes)
        xr, xi = x[:n2], x[n2:]
        hr, hi = h_ref[0, t], h_ref[1, t]
        y = jnp.concatenate([xr * hr - xi * hi, xr * hi + xi * hr], axis=0)
        o_ref[0, :, t] = _split_dot(g_ref[t], y, passes).reshape(2, n2, w)


def _fft2(a, hspec, order, f2, g2, kb, passes):
    p, _, n1, n2, w = a.shape
    f2, g2 = _dft_operand(f2, passes), _dft_operand(g2, passes)
    kern = functools.partial(_fft2_kernel, kb=kb, n2=n2, passes=passes)
    return pl.pallas_call(
        kern,
        grid=(n1 // kb, p),
        in_specs=[pl.BlockSpec((1, 2, kb, n2, w), lambda k, pp: (pp, 0, k, 0, 0)),
                  pl.BlockSpec((None, 2, kb, n2, w), lambda k, pp: (order, 0, k, 0, 0)),
                  pl.BlockSpec((kb,) + f2.shape[1:], lambda k, pp: (k, 0, 0)),
                  pl.BlockSpec((kb,) + g2.shape[1:], lambda k, pp: (k, 0, 0))],
        out_specs=pl.BlockSpec((1, 2, kb, n2, w), lambda k, pp: (pp, 0, k, 0, 0)),
        out_shape=jax.ShapeDtypeStruct(a.shape, F32),
        compiler_params=_cp(("parallel", "arbitrary")),
        name="hyena_fft_stage2",
    )(a, hspec, f2, g2)


def _fft3_kernel(b_ref, f_ref, gate_ref, prev_ref, skip_ref, o_ref, *, passes):
    f = f_ref[...]
    r = gate_ref.shape[2]
    for j in range(b_ref.shape[3]):
        bj = jnp.concatenate([b_ref[0, 0, :, j, :], b_ref[0, 1, :, j, :]], axis=0)
        conv = _split_dot(f, bj, passes)
        for mem in range(2):
            cm = conv[mem * r:(mem + 1) * r]
            o_ref[0, mem, :, j, :] = gate_ref[0, mem, :, j, :] * (cm + skip_ref[...] * prev_ref[0, mem, :, j, :])


def _fft3(bm, f3, gate, gate_sel, prev, prev_sel, skip, out_rows, jb, passes):
    p, _, n1, n2, w = bm.shape
    r = n1 // 2
    f3 = _dft_operand(f3, passes)
    seq = lambda s: pl.BlockSpec((None, 1, 2, r, jb, w), lambda pp, j: (s, pp, 0, 0, j, 0))
    return pl.pallas_call(
        functools.partial(_fft3_kernel, passes=passes),
        grid=(p, n2 // jb),
        in_specs=[pl.BlockSpec((1, 2, n1, jb, w), lambda pp, j: (pp, 0, 0, j, 0)),
                  pl.BlockSpec(f3.shape, lambda pp, j: (0, 0)),
                  seq(gate_sel), seq(prev_sel),
                  pl.BlockSpec((1, w), lambda pp, j: (0, 0))],
        out_specs=pl.BlockSpec((1, 2, r, jb, w), lambda pp, j: (pp, 0, 0, j, 0)),
        out_shape=jax.ShapeDtypeStruct((p, 2, out_rows, n2, w), F32),
        compiler_params=_cp(("parallel", "parallel")),
        name="hyena_fft_stage3",
    )(bm, f3, gate, prev, skip)


def _hyctx_kernel(v_ref, x1_ref, x2_ref, filt_ref, ff_ref, ffr_ref, fi_ref, skip_ref, yh_hbm_ref, o_ref,
                  *, passes, filter_passes):
    del yh_hbm_ref
    lc, w = v_ref.shape[2], v_ref.shape[3]
    n = 2 * lc
    y = v_ref[0].reshape(2 * lc, w)
    for o, gate_ref in enumerate((x1_ref, x2_ref)):
        sf = _split_dot(ffr_ref[...], filt_ref[o, 0], filter_passes)
        sb = _split_dot(ffr_ref[...], filt_ref[o, 1], filter_passes)
        hr, hi = sf[:n] + sb[:n], sf[n:] - sb[n:]
        x = _split_dot(ff_ref[...], y, passes)
        xr, xi = x[:n], x[n:]
        spec = jnp.concatenate([xr * hr - xi * hi, xr * hi + xi * hr], axis=0)
        conv = _split_dot(fi_ref[...], spec, passes)
        y = gate_ref[0].reshape(2 * lc, w) * (conv + skip_ref[o:o + 1, :] * y)
    o_ref[0] = y.reshape(2, lc, w)


def _hyena_ctx(hc, filt_c, ff, ffr, fi, skip, yh, passes, filter_passes):
    _, nb, lc, w = hc.shape
    p = nb // 2
    t = yh.shape[2]
    ff, fi, ffr = _dft_operand(ff, passes), _dft_operand(fi, passes), _dft_operand(ffr, filter_passes)
    part = pl.BlockSpec((1, 2, lc, w), lambda pp: (pp, 0, 0, 0))
    hc4 = hc.reshape(3, p, 2, lc, w)
    return pl.pallas_call(
        functools.partial(_hyctx_kernel, passes=passes, filter_passes=filter_passes),
        grid=(p,),
        in_specs=[part, part, part,
                  pl.BlockSpec(filt_c.shape, lambda pp: (0, 0, 0, 0)),
                  pl.BlockSpec(ff.shape, lambda pp: (0, 0)),
                  pl.BlockSpec(ffr.shape, lambda pp: (0, 0)),
                  pl.BlockSpec(fi.shape, lambda pp: (0, 0)),
                  pl.BlockSpec(skip.shape, lambda pp: (0, 0)),
                  pl.BlockSpec(memory_space=pl.ANY)],
        out_specs=pl.BlockSpec((1, 2, lc, w), lambda pp: (pp, 0, t // lc - 1, 0)),
        out_shape=jax.ShapeDtypeStruct(yh.shape, F32),
        input_output_aliases={8: 0},
        compiler_params=_cp(("parallel",)),
        name="hyena_ctx",
    )(hc4[0], hc4[1], hc4[2], filt_c, ff, ffr, fi, skip, yh)


def _hyena_filters(l, w1, b1, w2, b2, w3, freq):
    t = jnp.linspace(0.0, 1.0, l, dtype=F32)[:, None]
    bands = (HY_EMB - 1) // 2
    w = 2.0 * math.pi * jnp.arange(l, dtype=F32) / l
    f = jnp.linspace(1e-4, bands - 1, bands, dtype=F32)
    ang = w[:, None] * f[None, :]
    feats = jnp.concatenate([t, jnp.cos(ang), -jnp.sin(ang)], axis=-1)
    hdn = jnp.sin(freq[0] * (feats @ w1 + b1))
    hdn = jnp.sin(freq[1] * (hdn @ w2 + b2))
    n_out = w3.shape[-1]
    deltas = jnp.linspace(math.log(HY_TARGET) / HY_SLOW_PCT, math.log(HY_TARGET) / HY_FAST_PCT, n_out, dtype=F32)
    decay = jnp.exp(-t * jnp.abs(deltas))
    pos = jnp.arange(l)[:, None]
    parts = []
    for q in range(n_out // BRANCH_W):
        sl = slice(q * BRANCH_W, (q + 1) * BRANCH_W)
        f = (hdn @ w3[:, sl]).astype(F32) * decay[:, sl]
        parts.append(jnp.where(pos == 0, 0.0, f) if q % 2 else f)
    return jnp.stack(parts).reshape(n_out // (2 * BRANCH_W), 2, l, BRANCH_W)


def _hyena(z, conv_w, conv_b, filt_w, skip, nb, lat_len):
    m = z.shape[0]
    t = m // nb
    lc = t - lat_len
    w = BRANCH_W
    p = nb // 2
    n1, n2 = FFT_N1, 2 * lat_len // FFT_N1
    r = n1 // 2
    hl, hc = _hyconv(z, conv_w, conv_b, nb, lat_len)
    f1, f1r, f2, g2, f3 = _dft_tables(lat_len)
    ff, ffr, fi = _dft_tables_small(lc)
    jb, kb = min(16, n2), 16
    dp, fp = DFT_PASSES, DFT_FILTER_PASSES
    filt_l = _hyena_filters(lat_len, *filt_w)
    fa = _fft1(filt_l.reshape(1, 4, 1, r, n2, w), 0, f1r, jb, fp)
    hs_l = _fft2_filters(fa, f2, kb, fp)
    hl6 = hl.reshape(3, p, 2, r, n2, w)
    y, y_sel = hl6, 0
    for o in range(2):
        a = _fft1(y, y_sel, f1, jb, dp)
        bm = _fft2(a, hs_l, o, f2, g2, kb, dp)
        rows = r if o == 0 else t // n2
        y = _fft3(bm, f3, hl6, 1 + o, y, y_sel, skip[o:o + 1], rows, jb, dp)[None]
        y_sel = 0
    yh = y.reshape(p, 2, t, w)
    yh = _hyena_ctx(hc, _hyena_filters(lc, *filt_w), ff, ffr, fi, skip, yh, dp, fp)
    return yh.reshape(m, w)


def _merge_kernel(x_ref, mod_ref, ya_ref, yr_ref, yh_ref, yd_ref, wg_ref, bg_ref, wb_ref, wo_ref, g_ref, b_ref,
                  o_ref, *, alpha):
    x = x_ref[...]
    u = _modulate(x, mod_ref, 0, 1).astype(BF16)
    acc = None
    for i, y_ref in enumerate((ya_ref, yr_ref, yh_ref, yd_ref)):
        gate = jax.nn.sigmoid(_dot(u, wg_ref[i]) + bg_ref[i:i + 1, :])
        term = gate * _dot(y_ref[...].astype(BF16), wb_ref[i])
        acc = term if acc is None else acc + term
    mix = _dot(acc.astype(BF16), wo_ref[...])
    o_ref[...] = _gated_residual_ln(x, mix, mod_ref, 2, alpha, g_ref[...], b_ref[...])


def _merge(x, modx, ys, wg, bg, wb, wo, layer, g, b, alpha, tm):
    m, d = x.shape
    w = BRANCH_W
    row = lambda i: (i, 0)
    const2 = lambda i: (0, 0)
    one = pl.Buffered(1)
    stacked = lambda a: pl.BlockSpec((None,) + a.shape[1:], lambda i: (layer,) + (0,) * (a.ndim - 1),
                                     pipeline_mode=one)
    return pl.pallas_call(
        functools.partial(_merge_kernel, alpha=alpha),
        grid=(m // tm,),
        in_specs=[pl.BlockSpec((tm, d), row),
                  pl.BlockSpec((tm // SUB, 8, d), lambda i: (i, 0, 0)),
                  pl.BlockSpec((tm, w), row), pl.BlockSpec((tm, w), row),
                  pl.BlockSpec((tm, w), row), pl.BlockSpec((tm, w), row),
                  stacked(wg), pl.BlockSpec(bg.shape, const2), stacked(wb), stacked(wo),
                  pl.BlockSpec((1, d), const2), pl.BlockSpec((1, d), const2)],
        out_specs=pl.BlockSpec((tm, d), row),
        out_shape=jax.ShapeDtypeStruct((m, d), F32),
        compiler_params=_cp(("parallel",)),
        name="branch_merge",
    )(x, modx, *ys, wg, bg, wb, wo, g.reshape(1, d), b.reshape(1, d))


def _ffn_kernel(x_ref, mod_ref, w1_ref, w3_ref, w2_ref, g_ref, b_ref, o_ref, *, alpha, tf):
    x = x_ref[...]
    u = _modulate(x, mod_ref, 3, 4).astype(BF16)
    acc = None
    for j in range(w1_ref.shape[1] // tf):
        sl = slice(j * tf, (j + 1) * tf)
        h1 = _dot(u, w1_ref[:, sl])
        h3 = _dot(u, w3_ref[:, sl])
        part = _dot((h1 * jax.nn.sigmoid(h1) * h3).astype(BF16), w2_ref[sl, :])
        acc = part if acc is None else acc + part
    o_ref[...] = _gated_residual_ln(x, acc, mod_ref, 5, alpha, g_ref[...], b_ref[...])


def _ffn(x, modx, w1, w3, w2, layer, g, b, alpha, tm, tf):
    m, d = x.shape
    resident = lambda a: pl.BlockSpec((None,) + a.shape[1:], lambda i: (layer, 0, 0), pipeline_mode=pl.Buffered(1))
    return pl.pallas_call(
        functools.partial(_ffn_kernel, alpha=alpha, tf=tf),
        grid=(m // tm,),
        in_specs=[pl.BlockSpec((tm, d), lambda i: (i, 0)),
                  pl.BlockSpec((tm // SUB, 8, d), lambda i: (i, 0, 0)),
                  resident(w1), resident(w3), resident(w2),
                  pl.BlockSpec((1, d), lambda i: (0, 0)),
                  pl.BlockSpec((1, d), lambda i: (0, 0))],
        out_specs=pl.BlockSpec((tm, d), lambda i: (i, 0)),
        out_shape=jax.ShapeDtypeStruct((m, d), F32),
        compiler_params=_cp(("parallel",)),
        name="dense_swiglu",
    )(x, modx, w1, w3, w2, g.reshape(1, d), b.reshape(1, d))


def _route_kernel(x_ref, mod_ref, wr_ref, u_ref, lg_ref):
    u = _modulate(x_ref[...], mod_ref, 3, 4)
    lg_ref[...] = jnp.dot(u, wr_ref[...], preferred_element_type=F32, precision=lax.Precision.HIGHEST)
    half = u.shape[1] // 2
    bits = lax.bitcast_convert_type(u.astype(BF16).astype(F32), jnp.uint32)
    u_ref[...] = lax.bitcast_convert_type(bits[:, :half] | (bits[:, half:] >> 16), F32)


def _route(x, modx, wr_pad, tm):
    m, d = x.shape
    return pl.pallas_call(
        _route_kernel,
        grid=(m // tm,),
        in_specs=[pl.BlockSpec((tm, d), lambda i: (i, 0)),
                  pl.BlockSpec((tm // SUB, 8, d), lambda i: (i, 0, 0)),
                  pl.BlockSpec((d, LANE), lambda i: (0, 0))],
        out_specs=[pl.BlockSpec((tm, d // 2), lambda i: (i, 0)), pl.BlockSpec((tm, LANE), lambda i: (i, 0))],
        out_shape=[jax.ShapeDtypeStruct((m, d // 2), F32), jax.ShapeDtypeStruct((m, LANE), F32)],
        compiler_params=_cp(("parallel",)),
        name="moe_router",
    )(x, modx, wr_pad)


def _moe_kernel(be_ref, nu_ref, xb_ref, w1_ref, w3_ref, w2_ref, o_ref, acc_ref, u_ref):
    i, j = pl.program_id(0), pl.program_id(1)
    live = i < nu_ref[0]

    @pl.when(j == 0)
    def _():
        acc_ref[...] = jnp.zeros_like(acc_ref)
        words = lax.bitcast_convert_type(xb_ref[...], jnp.uint32)
        half = words.shape[1]
        hi = lax.bitcast_convert_type(words & jnp.uint32(0xFFFF0000), F32)
        lo = lax.bitcast_convert_type(words << 16, F32)
        u_ref[:, :half] = hi.astype(BF16)
        u_ref[:, half:] = lo.astype(BF16)

    @pl.when(live)
    def _():
        u = u_ref[...]
        h1 = _dot(u, w1_ref[0])
        h3 = _dot(u, w3_ref[0])
        acc_ref[...] += _dot((h1 * jax.nn.sigmoid(h1) * h3).astype(BF16), w2_ref[0])

    @pl.when(j == pl.num_programs(1) - 1)
    def _():
        o_ref[...] = acc_ref[...]


def _moe_experts(xb, blk_e, n_used, w1, w3, w2, layer, tm, tf):
    r = xb.shape[0]
    d = w1.shape[2]
    ff = w1.shape[3]
    nj = ff // tf

    def wmap(i, j, be, nu):
        return (layer, be[i], 0, jnp.where(i < nu[0], j, nj - 1))

    def w2map(i, j, be, nu):
        return (layer, be[i], jnp.where(i < nu[0], j, nj - 1), 0)

    grid_spec = pltpu.PrefetchScalarGridSpec(
        num_scalar_prefetch=2,
        grid=(r // tm, nj),
        in_specs=[pl.BlockSpec((tm, d // 2), lambda i, j, be, nu: (i, 0)),
                  pl.BlockSpec((None, 1, d, tf), wmap),
                  pl.BlockSpec((None, 1, d, tf), wmap),
                  pl.BlockSpec((None, 1, tf, d), w2map)],
        out_specs=pl.BlockSpec((tm, d), lambda i, j, be, nu: (i, 0)),
        scratch_shapes=[pltpu.VMEM((tm, d), F32), pltpu.VMEM((tm, d), BF16)],
    )
    return pl.pallas_call(
        _moe_kernel,
        grid_spec=grid_spec,
        out_shape=jax.ShapeDtypeStruct((r, d), F32),
        compiler_params=_cp(("arbitrary", "arbitrary")),
        name="moe_experts",
    )(blk_e, n_used, xb, w1, w3, w2)


def _ln_res_kernel(x_ref, f0_ref, f1_ref, wt_ref, mod_ref, g_ref, b_ref, o_ref, *, alpha):
    f = f0_ref[...] * wt_ref[:, 0:1] + f1_ref[...] * wt_ref[:, 1:2]
    o_ref[...] = _gated_residual_ln(x_ref[...], f, mod_ref, 5, alpha, g_ref[...], b_ref[...])


def _ln_res(x, f01, wts, modx, g, b, alpha, tm):
    m, d = x.shape
    nt = m // tm
    return pl.pallas_call(
        functools.partial(_ln_res_kernel, alpha=alpha),
        grid=(nt,),
        in_specs=[pl.BlockSpec((tm, d), lambda i: (i, 0)),
                  pl.BlockSpec((tm, d), lambda i: (i, 0)),
                  pl.BlockSpec((tm, d), lambda i: (i + nt, 0)),
                  pl.BlockSpec((tm, TOP_K), lambda i: (i, 0)),
                  pl.BlockSpec((tm // SUB, 8, d), lambda i: (i, 0, 0)),
                  pl.BlockSpec((1, d), lambda i: (0, 0)), pl.BlockSpec((1, d), lambda i: (0, 0))],
        out_specs=pl.BlockSpec((tm, d), lambda i: (i, 0)),
        out_shape=jax.ShapeDtypeStruct((m, d), F32),
        compiler_params=_cp(("parallel",)),
        name="moe_residual_ln",
    )(x, f01, f01, wts, modx, g.reshape(1, d), b.reshape(1, d))


def _moe(x, modx, router, w1, w3, w2, layer, g, b, alpha, tm_rows, tm, tf):
    m, d = x.shape
    wr_pad = jnp.zeros((d, LANE), F32).at[:, :N_EXPERTS].set(router)
    u, logits = _route(x, modx, wr_pad, tm_rows)
    top_v, top_i = lax.top_k(logits[:, :N_EXPERTS], TOP_K)
    wts = jax.nn.softmax(top_v, axis=-1)
    e_flat = top_i.reshape(-1).astype(jnp.int32)
    n_pair = m * TOP_K
    onehot = (e_flat[:, None] == jnp.arange(N_EXPERTS, dtype=jnp.int32)[None, :]).astype(jnp.int32)
    csum = jnp.cumsum(onehot, axis=0)
    counts = csum[-1]
    rank = jnp.take_along_axis(csum, e_flat[:, None], axis=1)[:, 0] - 1
    padded = (counts + tm - 1) // tm * tm
    pad_end = jnp.cumsum(padded)
    pad_start = pad_end - padded
    dest = pad_start[e_flat] + rank
    n_blk = -(-n_pair // tm) + N_EXPERTS
    r = n_blk * tm
    tok = jnp.arange(n_pair, dtype=jnp.int32) // TOP_K
    n_used = (pad_end[-1] // tm).astype(jnp.int32)
    blk = jnp.minimum(jnp.arange(n_blk, dtype=jnp.int32), n_used - 1) * tm
    blk_e = jnp.minimum(jnp.searchsorted(pad_end, blk, side='right'), N_EXPERTS - 1).astype(jnp.int32)
    _, tok_sorted = lax.sort_key_val(dest, tok)
    shift = pad_start - (jnp.cumsum(counts) - counts)
    src = jnp.arange(r, dtype=jnp.int32) - jnp.repeat(shift[blk_e], tm).astype(jnp.int32)
    row_tok = jnp.take(tok_sorted, jnp.clip(src, 0, n_pair - 1), mode="clip")
    xb = jnp.take(u, row_tok, axis=0, mode="clip")
    yb = _moe_experts(xb, blk_e, n_used.reshape(1), w1, w3, w2, layer, tm, tf)
    f01 = jnp.take(yb, dest.reshape(m, TOP_K).T.reshape(-1), axis=0, mode="clip")
    return _ln_res(x, f01, wts, modx, g, b, alpha, tm_rows)


def _rope_tables(nb, lat_len, ctx_len):
    rows = lat_len // GRID_W
    row = jnp.repeat(jnp.arange(rows, dtype=F32), GRID_W)
    col = jnp.tile(jnp.arange(GRID_W, dtype=F32), rows)
    n_freq = HEAD_DIM // 4
    inv = ROPE_THETA ** (-jnp.arange(n_freq, dtype=F32) / n_freq)
    ar, ac = row[:, None] * inv, col[:, None] * inv
    cos = jnp.concatenate([jnp.cos(ar), jnp.cos(ar), jnp.cos(ac), jnp.cos(ac)], axis=1)
    sin = jnp.concatenate([-jnp.sin(ar), jnp.sin(ar), -jnp.sin(ac), jnp.sin(ac)], axis=1)
    cos = jnp.concatenate([cos, jnp.ones((ctx_len, HEAD_DIM), F32)], axis=0)
    sin = jnp.concatenate([sin, jnp.zeros((ctx_len, HEAD_DIM), F32)], axis=0)
    return jnp.tile(cos, (nb, 1)), jnp.tile(sin, (nb, 1))


def _row_tile(m, pref):
    tm = pref
    while m % tm:
        tm //= 2
    return tm


def kernel(x, c, ctx, c_ctx, w_mod, b_mod, w_in, conv_a_w, conv_a_b, ret_decay, hy_conv_w, hy_conv_b, hy_w1, hy_b1, hy_w2, hy_b2, hy_w3, hy_freq, hy_skip, q_norm, k_norm, w_gate, b_gate, w_branch, w_o, ln_g, ln_b, ffn_w1, ffn_w3, ffn_w2, router, moe_w1, moe_w3, moe_w2):
    nb, lat_len, d = x.shape
    ctx_len = ctx.shape[1]
    depth = w_mod.shape[0]
    t = lat_len + ctx_len
    m = nb * t
    alpha = float((2 * depth) ** 0.25)
    assert lat_len % SUB == 0 and ctx_len % SUB == 0 and nb % 2 == 0

    act = jnp.concatenate([jax.nn.silu(c), jax.nn.silu(c_ctx)[None, :],
                           jnp.zeros((8 - nb - 1, d), F32)], axis=0)
    mods = _mod_all(act, w_mod, b_mod).reshape(depth, 8, N_MOD, d)
    mods = jnp.pad(mods, ((0, 0), (0, 0), (0, 8 - N_MOD), (0, 0)))
    tiles_per_batch = t // SUB
    tile_pos = np.arange(m // SUB) % tiles_per_batch
    tile_group = np.where(tile_pos < lat_len // SUB, np.arange(m // SUB) // tiles_per_batch, nb)
    cos_t, sin_t = _rope_tables(nb, lat_len, ctx_len)

    cast = lambda a: a.astype(BF16)
    w_in_b, w_gate_b, w_branch_b, w_o_b = cast(w_in), cast(w_gate), cast(w_branch), cast(w_o)
    ffn_b = (cast(ffn_w1), cast(ffn_w3), cast(ffn_w2))
    moe_w = (cast(moe_w1), cast(moe_w3), cast(moe_w2))

    rows = jnp.concatenate([x, ctx], axis=1).reshape(m, d)
    tm = _row_tile(m, 512)
    for l in range(depth):
        modx = mods[l][tile_group]
        za, zvg, zh, rq, rk, aq, ak, av = _proj(rows, modx, w_in_b, l, cos_t, sin_t, q_norm[l], k_norm[l], tm)
        y_d = _attention(aq, ak, av, nb, lat_len)
        log_gamma = -jnp.exp(ret_decay[l].astype(F32))
        y_r = _retention(rq, rk, zvg, _retention_tables(log_gamma), nb, lat_len)
        y_a = _shortconv(za, conv_a_w[l], conv_a_b[l], nb, lat_len)
        y_h = _hyena(zh, hy_conv_w[l], hy_conv_b[l],
                     (hy_w1[l], hy_b1[l], hy_w2[l], hy_b2[l], hy_w3[l], hy_freq[l]), hy_skip[l], nb, lat_len)
        rows = _merge(rows, modx, (y_a, y_r, y_h, y_d), w_gate_b, b_gate[l], w_branch_b, w_o_b, l,
                      ln_g[l, 0], ln_b[l, 0], alpha, tm)
        i = l // 2
        if l % 2 == 0:
            rows = _ffn(rows, modx, *ffn_b, i, ln_g[l, 1], ln_b[l, 1], alpha, tm, 256)
        else:
            rows = _moe(rows, modx, router[i], *moe_w, i, ln_g[l, 1], ln_b[l, 1], alpha, tm, 768, 1792)
    return rows.reshape(nb, t, d)[:, :lat_len]
```

```python
import functools
import math

import numpy as np
import jax
import jax.numpy as jnp
from jax import lax
from jax.experimental import pallas as pl
from jax.experimental.pallas import tpu as pltpu

F32 = jnp.float32
BF16 = jnp.bfloat16

GRID_W = 64
BRANCH_W = 512
RET_HEADS = 4
RET_D = 128
RET_CHUNK = 128
HY_EMB = 33
HY_FAST_PCT = 0.3
HY_SLOW_PCT = 1.5
HY_TARGET = 1e-2
ATT_HEADS = 4
ATT_KV_HEADS = 2
HEAD_DIM = 128
ROPE_THETA = 10000.0
N_EXPERTS = 8
TOP_K = 2
N_MOD = 6
A_W = 3 * BRANCH_W
R_W = 4 * RET_HEADS * RET_D
H_W = 3 * BRANCH_W
D_W = (ATT_HEADS + 2 * ATT_KV_HEADS) * HEAD_DIM
IN_W = A_W + R_W + H_W + D_W
LN_EPS = 1e-6

SUB = 256
LANE = 128
FFT_N1 = 128
RET_UNROLL = 8
ATT_TQ = 1024
ATT_KC = 1024
DFT_PASSES = 1
DFT_FILTER_PASSES = 1
VMEM_LIMIT = 56 * 1024 * 1024


def _cp(sem, vmem=VMEM_LIMIT):
    return pltpu.CompilerParams(dimension_semantics=sem, vmem_limit_bytes=vmem)


def _dot(a, b):
    return jnp.dot(a, b, preferred_element_type=F32)


def _dot_nt(a, b):
    return lax.dot_general(a, b, (((1,), (1,)), ((), ())), preferred_element_type=F32)


def _dot_tn(a, b):
    return lax.dot_general(a, b, (((0,), (0,)), ((), ())), preferred_element_type=F32)


def _split_dot(f_cat, x, passes):
    xh = x.astype(BF16)
    if passes == 1:
        return _dot(f_cat, xh)
    m = f_cat.shape[0] // 2
    r = _dot(f_cat, xh)
    r = r[:m] + r[m:]
    if passes == 3:
        r = r + _dot(f_cat[:m], (x - xh.astype(F32)).astype(BF16))
    return r


def _dft_operand(f_cat, passes):
    return f_cat[..., :f_cat.shape[-2] // 2, :] if passes == 1 else f_cat


def _modulate(x, mod_ref, shift_row, scale_row):
    parts = []
    for s in range(x.shape[0] // SUB):
        sh = mod_ref[s, shift_row:shift_row + 1, :]
        sc = mod_ref[s, scale_row:scale_row + 1, :]
        parts.append(x[s * SUB:(s + 1) * SUB] * (1.0 + sc) + sh)
    return parts[0] if len(parts) == 1 else jnp.concatenate(parts, axis=0)


def _gated_residual_ln(x, f, mod_ref, gate_row, alpha, g, b):
    parts = []
    for s in range(x.shape[0] // SUB):
        gt = mod_ref[s, gate_row:gate_row + 1, :]
        parts.append(alpha * x[s * SUB:(s + 1) * SUB] + gt * f[s * SUB:(s + 1) * SUB])
    r = parts[0] if len(parts) == 1 else jnp.concatenate(parts, axis=0)
    mu = jnp.mean(r, axis=-1, keepdims=True)
    rc = r - mu
    var = jnp.mean(rc * rc, axis=-1, keepdims=True)
    return rc * lax.rsqrt(var + LN_EPS) * g + b


def _mod_kernel(a_ref, w_ref, b_ref, o_ref):
    o_ref[0] = _dot(a_ref[...].astype(BF16), w_ref[0].astype(BF16)) + b_ref[0]


def _mod_all(act, w_mod, b_mod):
    depth, d, n = w_mod.shape
    tn = 512
    return pl.pallas_call(
        _mod_kernel,
        grid=(depth, n // tn),
        in_specs=[pl.BlockSpec((8, d), lambda l, j: (0, 0)),
                  pl.BlockSpec((1, d, tn), lambda l, j: (l, 0, j)),
                  pl.BlockSpec((1, 1, tn), lambda l, j: (l, 0, j))],
        out_specs=pl.BlockSpec((1, 8, tn), lambda l, j: (l, 0, j)),
        out_shape=jax.ShapeDtypeStruct((depth, 8, n), F32),
        compiler_params=_cp(("parallel", "parallel")),
        name="mod_vectors",
    )(act, w_mod, b_mod.reshape(depth, 1, n))


PROJ_TN = 512
assert (A_W, R_W, H_W, D_W) == (3 * PROJ_TN, 4 * PROJ_TN, 3 * PROJ_TN, 2 * PROJ_TN)


def _proj_kernel(x_ref, mod_ref, w_ref, cos_ref, sin_ref, gq_ref, gk_ref,
                 za_ref, zvg_ref, zh_ref, rq_ref, rk_ref, aq_ref, ak_ref, av_ref):
    u = _modulate(x_ref[...], mod_ref, 0, 1).astype(BF16)
    tn = PROJ_TN
    cos, sin = cos_ref[...], sin_ref[...]
    lane = lax.broadcasted_iota(jnp.int32, cos.shape, 1)
    first_half = (lane % 64) < 32

    def tile(j):
        return _dot(u, w_ref[:, j * tn:(j + 1) * tn])

    def rope(x):
        partner = jnp.where(first_half, pltpu.roll(x, 96, 1), pltpu.roll(x, 32, 1))
        return x * cos + partner * sin

    def rms(x, g):
        return x * lax.rsqrt(jnp.mean(x * x, axis=-1, keepdims=True) + LN_EPS) * g

    head = lambda h: slice(h * LANE, (h + 1) * LANE)

    for j in range(3):
        za_ref[:, j * tn:(j + 1) * tn] = tile(j)
    r = tile(3)
    for h in range(RET_HEADS):
        rq_ref[:, head(h)] = rope(r[:, head(h)])
    r = tile(4)
    for h in range(RET_HEADS):
        rk_ref[:, head(h)] = rope(r[:, head(h)] * (RET_D ** -0.5))
    for j in range(2):
        zvg_ref[:, j * tn:(j + 1) * tn] = tile(5 + j)
    for j in range(3):
        zh_ref[:, j * tn:(j + 1) * tn] = tile(7 + j)
    q_scale = (HEAD_DIM ** -0.5) * math.log2(math.e)
    r = tile(10)
    for h in range(ATT_HEADS):
        aq_ref[:, head(h)] = (rope(rms(r[:, head(h)], gq_ref[...])) * q_scale).astype(BF16)
    r = tile(11)
    ones_col = jnp.where(lane == 0, 1.0, 0.0).astype(BF16)
    for h in range(ATT_KV_HEADS):
        ak_ref[:, head(h)] = rope(rms(r[:, head(h)], gk_ref[...])).astype(BF16)
        av_ref[:, head(2 * h)] = r[:, head(ATT_KV_HEADS + h)].astype(BF16)
        av_ref[:, head(2 * h + 1)] = ones_col


def _proj(x, modx, w, layer, cos_t, sin_t, gq, gk, tm):
    m, d = x.shape
    n = w.shape[2]
    tn = PROJ_TN
    qw, kw = ATT_HEADS * HEAD_DIM, ATT_KV_HEADS * HEAD_DIM
    row = lambda i: (i, 0)
    widths = (A_W, 2 * tn, H_W, tn, tn, qw, kw, 2 * kw)
    dtypes = (F32,) * 5 + (BF16,) * 3
    return pl.pallas_call(
        _proj_kernel,
        grid=(m // tm,),
        in_specs=[pl.BlockSpec((tm, d), row),
                  pl.BlockSpec((tm // SUB, 8, d), lambda i: (i, 0, 0)),
                  pl.BlockSpec((None, d, n), lambda i: (layer, 0, 0), pipeline_mode=pl.Buffered(1)),
                  pl.BlockSpec((tm, LANE), row), pl.BlockSpec((tm, LANE), row),
                  pl.BlockSpec((1, LANE), lambda i: (0, 0)), pl.BlockSpec((1, LANE), lambda i: (0, 0))],
        out_specs=[pl.BlockSpec((tm, wd), row) for wd in widths],
        out_shape=[jax.ShapeDtypeStruct((m, wd), dt) for wd, dt in zip(widths, dtypes)],
        compiler_params=_cp(("parallel",)),
        name="in_proj",
    )(x, modx, w, cos_t, sin_t, gq.reshape(1, LANE), gk.reshape(1, LANE))


def _attn_kernel(q_ref, k_ref, v_ref, *rest, key_lo, key_hi, kc):
    o_ref = rest[-1]
    tq = q_ref.shape[1]
    grp = ATT_HEADS // ATT_KV_HEADS
    q2 = jnp.concatenate([q_ref[0, :, g * HEAD_DIM:(g + 1) * HEAD_DIM] for g in range(grp)], axis=0)
    m_run, acc = None, None
    for lo in range(key_lo, key_hi, kc):
        hi = min(lo + kc, key_hi)
        s = _dot_nt(q2, k_ref[0, lo:hi, :])
        m_chunk = jnp.max(s, axis=-1, keepdims=True)
        m_new = m_chunk if m_run is None else jnp.maximum(m_run, m_chunk)
        pv = _dot(jnp.exp2(s - m_new).astype(BF16), v_ref[0, lo:hi, :])
        acc = pv if acc is None else jnp.exp2(m_run - m_new) * acc + pv
        m_run = m_new
    for g in range(grp):
        a = acc[g * tq:(g + 1) * tq]
        o_ref[0, :, g * HEAD_DIM:(g + 1) * HEAD_DIM] = a[:, :HEAD_DIM] / a[:, HEAD_DIM:HEAD_DIM + 1]


def _attention(aq, ak, av, nb, lat_len):
    m = aq.shape[0]
    t = m // nb
    lc = t - lat_len
    grp_w = (ATT_HEADS // ATT_KV_HEADS) * HEAD_DIM
    args = (aq.reshape(nb, t, -1), ak.reshape(nb, t, -1), av.reshape(nb, t, -1))

    def call(tq, n_tiles, first_tile, key_lo, key_hi, prev):
        qmap = lambda b, kv, i: (b, first_tile + i, kv)
        specs = [pl.BlockSpec((1, tq, grp_w), qmap),
                 pl.BlockSpec((1, t, HEAD_DIM), lambda b, kv, i: (b, 0, kv)),
                 pl.BlockSpec((1, t, 2 * HEAD_DIM), lambda b, kv, i: (b, 0, kv))]
        extra, alias = (), {}
        if prev is not None:
            specs.append(pl.BlockSpec(memory_space=pl.ANY))
            extra, alias = (prev,), {3: 0}
        return pl.pallas_call(
            functools.partial(_attn_kernel, key_lo=key_lo, key_hi=key_hi, kc=ATT_KC),
            grid=(nb, ATT_KV_HEADS, n_tiles),
            in_specs=specs,
            out_specs=pl.BlockSpec((1, tq, grp_w), qmap),
            out_shape=jax.ShapeDtypeStruct((nb, t, ATT_HEADS * HEAD_DIM), F32),
            input_output_aliases=alias,
            compiler_params=_cp(("parallel", "parallel", "arbitrary")),
            name="gqa_attention",
        )(*args, *extra)

    tq = math.gcd(ATT_TQ, lat_len)
    out = call(tq, lat_len // tq, 0, 0, t, None)
    out = call(lc, 1, t // lc - 1, lat_len, t, out)
    return out.reshape(m, -1)


def _ret_kernel(q_ref, k_ref, v_ref, g_ref, intra_ref, qd_ref, kd_ref, cd_ref, o_ref, st_ref,
                *, n_lat, n_ctx):
    c_len = RET_CHUNK
    n = n_lat + n_ctx

    def rows(c):
        return pl.ds(pl.multiple_of(c * c_len, c_len), c_len)

    def kv(d, c):
        return _dot_tn((k_ref[0, rows(c), :] * kd_ref[d, 0]).astype(BF16), v_ref[0, rows(c), :].astype(BF16))

    def scan_step(t, carry):
        sf, sb = carry
        cf = jnp.where(t < n_ctx, n_lat + t, t - n_ctx)
        cb = jnp.where(t < n_ctx, n - 1 - t, n_lat - 1 - (t - n_ctx))
        st_ref[cf, :RET_D, :] = sf.astype(BF16)
        st_ref[cb, RET_D:, :] = sb.astype(BF16)
        return (cd_ref[0, 0, 0:1, :] * sf + kv(0, cf), cd_ref[1, 0, 0:1, :] * sb + kv(1, cb))

    zero = jnp.zeros((RET_D, RET_D), F32)
    lax.fori_loop(0, n, scan_step, (zero, zero), unroll=RET_UNROLL)

    mask = intra_ref[0, 0] + intra_ref[1, 0]

    def out_step(c, carry):
        qc = q_ref[0, rows(c), :]
        vc = v_ref[0, rows(c), :].astype(BF16)
        s = _dot_nt(qc.astype(BF16), k_ref[0, rows(c), :].astype(BF16)) * mask
        qdec = jnp.concatenate([qc * qd_ref[0, 0], qc * qd_ref[1, 0]], axis=1).astype(BF16)
        o = _dot(s.astype(BF16), vc) + _dot(qdec, st_ref[c])
        mu = jnp.mean(o, axis=-1, keepdims=True)
        oc = o - mu
        var = jnp.mean(oc * oc, axis=-1, keepdims=True)
        g = g_ref[0, rows(c), :]
        o_ref[0, rows(c), :] = (g * jax.nn.sigmoid(g)) * (oc * lax.rsqrt(var + LN_EPS))
        return carry

    lax.fori_loop(0, n, out_step, 0, unroll=RET_UNROLL)


def _retention(rq, rk, z, tabs, nb, lat_len):
    m = rq.shape[0]
    t = m // nb
    intra, qd, kd, cd = tabs
    v_blk, g_blk = 0, RET_HEADS
    kern = functools.partial(_ret_kernel, n_lat=lat_len // RET_CHUNK, n_ctx=(t - lat_len) // RET_CHUNK)
    seq = lambda off: pl.BlockSpec((1, t, LANE), lambda b, h: (b, 0, off + h))
    tab = lambda r: pl.BlockSpec((2, 1, r, LANE), lambda b, h: (0, h, 0, 0))
    out = pl.pallas_call(
        kern,
        grid=(nb, RET_HEADS),
        in_specs=[seq(0), seq(0), seq(v_blk), seq(g_blk),
                  tab(RET_CHUNK), tab(RET_CHUNK), tab(RET_CHUNK), tab(8)],
        out_specs=seq(0),
        out_shape=jax.ShapeDtypeStruct((nb, t, RET_HEADS * RET_D), F32),
        scratch_shapes=[pltpu.VMEM((t // RET_CHUNK, 2 * RET_D, RET_D), BF16)],
        compiler_params=_cp(("parallel", "parallel")),
        name="retention",
    )(rq.reshape(nb, t, -1), rk.reshape(nb, t, -1), z.reshape(nb, t, -1), z.reshape(nb, t, -1), intra, qd, kd, cd)
    return out.reshape(m, -1)


def _retention_tables(log_gamma):
    c = RET_CHUNK
    i = jnp.arange(c, dtype=F32)
    lg = log_gamma.astype(F32)[:, :, None]
    rel = i[:, None] - i[None, :]
    intra_f = jnp.where(rel >= 0, jnp.exp(lg[0][..., None] * jnp.maximum(rel, 0.0)), 0.0)
    intra_b = jnp.where(rel <= 0, jnp.exp(lg[1][..., None] * jnp.maximum(-rel, 0.0)), 0.0)
    qd = jnp.stack([jnp.exp(lg[0] * (i + 1.0)), jnp.exp(lg[1] * (c - i))])
    kd = jnp.stack([jnp.exp(lg[0] * (c - 1.0 - i)), jnp.exp(lg[1] * i)])
    cd = jnp.exp(lg * c)
    bc = lambda a: jnp.broadcast_to(a[..., None], a.shape + (LANE,))
    return (jnp.stack([intra_f, intra_b]), bc(qd), bc(kd),
            jnp.broadcast_to(cd[..., None], (2, RET_HEADS, 8, LANE)))


def _shift_rows(x, lat_len):
    t = x.shape[0]
    row = lax.broadcasted_iota(jnp.int32, x.shape, 0)
    xm = jnp.where((row == 0) | (row == lat_len), 0.0, pltpu.roll(x, 1, 0))
    xp = jnp.where((row == lat_len - 1) | (row == t - 1), 0.0, pltpu.roll(x, t - 1, 0))
    return xm, xp


def _shortconv_kernel(bg_ref, cg_ref, xv_ref, w_ref, b_ref, o_ref, *, lat_len):
    p = cg_ref[0] * xv_ref[0]
    pm, pp = _shift_rows(p, lat_len)
    o_ref[0] = bg_ref[0] * (pm * w_ref[0:1, :] + p * w_ref[1:2, :] + pp * w_ref[2:3, :] + b_ref[...])


def _shortconv(z, w, b, nb, lat_len):
    m = z.shape[0]
    t = m // nb
    nct = BRANCH_W // LANE
    kern = functools.partial(_shortconv_kernel, lat_len=lat_len)
    seq = lambda off: pl.BlockSpec((1, t, LANE), lambda bb, j: (bb, 0, off + j))
    out = pl.pallas_call(
        kern,
        grid=(nb, nct),
        in_specs=[seq(0), seq(nct), seq(2 * nct),
                  pl.BlockSpec((3, LANE), lambda bb, j: (0, j)),
                  pl.BlockSpec((1, LANE), lambda bb, j: (0, j))],
        out_specs=seq(0),
        out_shape=jax.ShapeDtypeStruct((nb, t, BRANCH_W), F32),
        compiler_params=_cp(("parallel", "parallel")),
        name="short_conv",
    )(z.reshape(nb, t, -1), z.reshape(nb, t, -1), z.reshape(nb, t, -1), w, b.reshape(1, -1))
    return out.reshape(m, -1)


def _hyconv_kernel(z_ref, w_ref, b_ref, ol_ref, oc_ref, *, lat_len):
    x = z_ref[0]
    xm, xp = _shift_rows(x, lat_len)
    y = xm * w_ref[0:1, :] + x * w_ref[1:2, :] + xp * w_ref[2:3, :] + b_ref[...]
    ol_ref[0, 0] = y[:lat_len]
    oc_ref[0, 0] = y[lat_len:]


def _hyconv(z, w, b, nb, lat_len):
    m = z.shape[0]
    t = m // nb
    nct = BRANCH_W // LANE
    off = 0
    kern = functools.partial(_hyconv_kernel, lat_len=lat_len)
    return pl.pallas_call(
        kern,
        grid=(nb, 3 * nct),
        in_specs=[pl.BlockSpec((1, t, LANE), lambda bb, j: (bb, 0, off + j)),
                  pl.BlockSpec((3, LANE), lambda bb, j: (0, j)),
                  pl.BlockSpec((1, LANE), lambda bb, j: (0, j))],
        out_specs=[pl.BlockSpec((1, 1, lat_len, LANE), lambda bb, j: (j // nct, bb, 0, j % nct)),
                   pl.BlockSpec((1, 1, t - lat_len, LANE), lambda bb, j: (j // nct, bb, 0, j % nct))],
        out_shape=[jax.ShapeDtypeStruct((3, nb, lat_len, BRANCH_W), F32),
                   jax.ShapeDtypeStruct((3, nb, t - lat_len, BRANCH_W), F32)],
        compiler_params=_cp(("parallel", "parallel")),
        name="hyena_dwconv",
    )(z.reshape(nb, t, -1), w, b.reshape(1, -1))


def _stack_c(m):
    return np.block([[m.real, -m.imag], [m.imag, m.real]])


def _split_np(m):
    m32 = np.asarray(m, np.float32)
    hi = m32.astype(BF16)
    lo = (m32 - hi.astype(np.float32)).astype(BF16)
    return np.concatenate([hi, lo], axis=-2)


@functools.lru_cache(maxsize=None)
def _dft_tables(lat_len):
    n = 2 * lat_len
    n1, n2 = FFT_N1, 2 * lat_len // FFT_N1
    a1 = np.arange(n1)
    a2 = np.arange(n2)
    w1 = np.exp(-2j * np.pi * np.outer(a1, a1[:n1 // 2]) / n1)
    f1 = _stack_c(w1)
    w2 = np.exp(-2j * np.pi * np.outer(a2, a2) / n2)
    tw = np.exp(-2j * np.pi * np.outer(a1, a2) / n)
    cf = w2[None, :, :] * tw[:, None, :]
    f2 = np.stack([_stack_c(cf[k]) for k in range(n1)])
    g2 = np.stack([_stack_c(cf[k].conj().T) for k in range(n1)])
    w1i = np.exp(2j * np.pi * np.outer(a1[:n1 // 2], a1) / n1) / n
    f3 = _stack_c(w1i)
    f1_real = f1[:, :n1 // 2]
    return _split_np(f1), _split_np(f1_real), _split_np(f2), _split_np(g2), _split_np(f3)


@functools.lru_cache(maxsize=None)
def _dft_tables_small(ctx_len):
    n = 2 * ctx_len
    a = np.arange(n)
    ff = _stack_c(np.exp(-2j * np.pi * np.outer(a, a[:ctx_len]) / n))
    fi = _stack_c(np.exp(2j * np.pi * np.outer(a[:ctx_len], a) / n) / n)
    return _split_np(ff), _split_np(ff[:, :ctx_len]), _split_np(fi)


def _fft1_kernel(x_ref, f_ref, o_ref, *, passes):
    f = f_ref[...]
    for j in range(x_ref.shape[3]):
        parts = [x_ref[0, b, :, j, :] for b in range(x_ref.shape[1])]
        r = _split_dot(f, parts[0] if len(parts) == 1 else jnp.concatenate(parts, axis=0), passes)
        h = r.shape[0] // 2
        o_ref[0, 0, :, j, :] = r[:h]
        o_ref[0, 1, :, j, :] = r[h:]


def _fft1(x, sel, f1, jb, passes):
    _, p, mem, r, n2, w = x.shape
    f1 = _dft_operand(f1, passes)
    n1 = f1.shape[0] // (2 if passes == 1 else 4)
    return pl.pallas_call(
        functools.partial(_fft1_kernel, passes=passes),
        grid=(p, n2 // jb),
        in_specs=[pl.BlockSpec((None, 1, mem, r, jb, w), lambda pp, j: (sel, pp, 0, 0, j, 0)),
                  pl.BlockSpec(f1.shape, lambda pp, j: (0, 0))],
        out_specs=pl.BlockSpec((1, 2, n1, jb, w), lambda pp, j: (pp, 0, 0, j, 0)),
        out_shape=jax.ShapeDtypeStruct((p, 2, n1, n2, w), F32),
        compiler_params=_cp(("parallel", "parallel")),
        name="hyena_fft_stage1",
    )(x, f1)


def _fft2f_kernel(a_ref, f_ref, o_ref, *, kb, n2, passes):
    w = a_ref.shape[-1]
    for t in range(kb):
        xf = _split_dot(f_ref[t], a_ref[0, :, t].reshape(2 * n2, w), passes)
        xb = _split_dot(f_ref[t], a_ref[1, :, t].reshape(2 * n2, w), passes)
        o_ref[0, 0, t] = xf[:n2] + xb[:n2]
        o_ref[0, 1, t] = xf[n2:] - xb[n2:]


def _fft2_filters(a, f2, kb, passes):
    q, _, n1, n2, w = a.shape
    f2 = _dft_operand(f2, passes)
    return pl.pallas_call(
        functools.partial(_fft2f_kernel, kb=kb, n2=n2, passes=passes),
        grid=(q // 2, n1 // kb),
        in_specs=[pl.BlockSpec((2, 2, kb, n2, w), lambda o, k: (o, 0, k, 0, 0)),
                  pl.BlockSpec((kb,) + f2.shape[1:], lambda o, k: (k, 0, 0))],
        out_specs=pl.BlockSpec((1, 2, kb, n2, w), lambda o, k: (o, 0, k, 0, 0)),
        out_shape=jax.ShapeDtypeStruct((q // 2, 2, n1, n2, w), F32),
        compiler_params=_cp(("parallel", "parallel")),
        name="hyena_filter_spectrum",
    )(a, f2)


def _fft2_kernel(a_ref, h_ref, f_ref, g_ref, o_ref, *, kb, n2, passes):
    w = a_ref.shape[-1]
    for t in range(kb):
        a = a_ref[0, :, t].reshape(2 * n2, w)
        x = _split_dot(f_ref[t], a, passes)
        xr, xi = x[:n2], x[n2:]
        hr, hi = h_ref[0, t], h_ref[1, t]
        y = jnp.concatenate([xr * hr - xi * hi, xr * hi + xi * hr], axis=0)
        o_ref[0, :, t] = _split_dot(g_ref[t], y, passes).reshape(2, n2, w)


def _fft2(a, hspec, order, f2, g2, kb, passes):
    p, _, n1, n2, w = a.shape
    f2, g2 = _dft_operand(f2, passes), _dft_operand(g2, passes)
    kern = functools.partial(_fft2_kernel, kb=kb, n2=n2, passes=passes)
    return pl.pallas_call(
        kern,
        grid=(n1 // kb, p),
        in_specs=[pl.BlockSpec((1, 2, kb, n2, w), lambda k, pp: (pp, 0, k, 0, 0)),
                  pl.BlockSpec((None, 2, kb, n2, w), lambda k, pp: (order, 0, k, 0, 0)),
                  pl.BlockSpec((kb,) + f2.shape[1:], lambda k, pp: (k, 0, 0)),
                  pl.BlockSpec((kb,) + g2.shape[1:], lambda k, pp: (k, 0, 0))],
        out_specs=pl.BlockSpec((1, 2, kb, n2, w), lambda k, pp: (pp, 0, k, 0, 0)),
        out_shape=jax.ShapeDtypeStruct(a.shape, F32),
        compiler_params=_cp(("parallel", "arbitrary")),
        name="hyena_fft_stage2",
    )(a, hspec, f2, g2)


def _fft3_kernel(b_ref, f_ref, gate_ref, prev_ref, skip_ref, o_ref, *, passes):
    f = f_ref[...]
    r = gate_ref.shape[2]
    for j in range(b_ref.shape[3]):
        bj = jnp.concatenate([b_ref[0, 0, :, j, :], b_ref[0, 1, :, j, :]], axis=0)
        conv = _split_dot(f, bj, passes)
        for mem in range(2):
            cm = conv[mem * r:(mem + 1) * r]
            o_ref[0, mem, :, j, :] = gate_ref[0, mem, :, j, :] * (cm + skip_ref[...] * prev_ref[0, mem, :, j, :])


def _fft3(bm, f3, gate, gate_sel, prev, prev_sel, skip, out_rows, jb, passes):
    p, _, n1, n2, w = bm.shape
    r = n1 // 2
    f3 = _dft_operand(f3, passes)
    seq = lambda s: pl.BlockSpec((None, 1, 2, r, jb, w), lambda pp, j: (s, pp, 0, 0, j, 0))
    return pl.pallas_call(
        functools.partial(_fft3_kernel, passes=passes),
        grid=(p, n2 // jb),
        in_specs=[pl.BlockSpec((1, 2, n1, jb, w), lambda pp, j: (pp, 0, 0, j, 0)),
                  pl.BlockSpec(f3.shape, lambda pp, j: (0, 0)),
                  seq(gate_sel), seq(prev_sel),
                  pl.BlockSpec((1, w), lambda pp, j: (0, 0))],
        out_specs=pl.BlockSpec((1, 2, r, jb, w), lambda pp, j: (pp, 0, 0, j, 0)),
        out_shape=jax.ShapeDtypeStruct((p, 2, out_rows, n2, w), F32),
        compiler_params=_cp(("parallel", "parallel")),
        name="hyena_fft_stage3",
    )(bm, f3, gate, prev, skip)


def _hyctx_kernel(v_ref, x1_ref, x2_ref, filt_ref, ff_ref, ffr_ref, fi_ref, skip_ref, yh_hbm_ref, o_ref,
                  *, passes, filter_passes):
    del yh_hbm_ref
    lc, w = v_ref.shape[2], v_ref.shape[3]
    n = 2 * lc
    y = v_ref[0].reshape(2 * lc, w)
    for o, gate_ref in enumerate((x1_ref, x2_ref)):
        sf = _split_dot(ffr_ref[...], filt_ref[o, 0], filter_passes)
        sb = _split_dot(ffr_ref[...], filt_ref[o, 1], filter_passes)
        hr, hi = sf[:n] + sb[:n], sf[n:] - sb[n:]
        x = _split_dot(ff_ref[...], y, passes)
        xr, xi = x[:n], x[n:]
        spec = jnp.concatenate([xr * hr - xi * hi, xr * hi + xi * hr], axis=0)
        conv = _split_dot(fi_ref[...], spec, passes)
        y = gate_ref[0].reshape(2 * lc, w) * (conv + skip_ref[o:o + 1, :] * y)
    o_ref[0] = y.reshape(2, lc, w)


def _hyena_ctx(hc, filt_c, ff, ffr, fi, skip, yh, passes, filter_passes):
    _, nb, lc, w = hc.shape
    p = nb // 2
    t = yh.shape[2]
    ff, fi, ffr = _dft_operand(ff, passes), _dft_operand(fi, passes), _dft_operand(ffr, filter_passes)
    part = pl.BlockSpec((1, 2, lc, w), lambda pp: (pp, 0, 0, 0))
    hc4 = hc.reshape(3, p, 2, lc, w)
    return pl.pallas_call(
        functools.partial(_hyctx_kernel, passes=passes, filter_passes=filter_passes),
        grid=(p,),
        in_specs=[part, part, part,
                  pl.BlockSpec(filt_c.shape, lambda pp: (0, 0, 0, 0)),
                  pl.BlockSpec(ff.shape, lambda pp: (0, 0)),
                  pl.BlockSpec(ffr.shape, lambda pp: (0, 0)),
                  pl.BlockSpec(fi.shape, lambda pp: (0, 0)),
                  pl.BlockSpec(skip.shape, lambda pp: (0, 0)),
                  pl.BlockSpec(memory_space=pl.ANY)],
        out_specs=pl.BlockSpec((1, 2, lc, w), lambda pp: (pp, 0, t // lc - 1, 0)),
        out_shape=jax.ShapeDtypeStruct(yh.shape, F32),
        input_output_aliases={8: 0},
        compiler_params=_cp(("parallel",)),
        name="hyena_ctx",
    )(hc4[0], hc4[1], hc4[2], filt_c, ff, ffr, fi, skip, yh)


def _hyena_filters(l, w1, b1, w2, b2, w3, freq):
    t = jnp.linspace(0.0, 1.0, l, dtype=F32)[:, None]
    bands = (HY_EMB - 1) // 2
    w = 2.0 * math.pi * jnp.arange(l, dtype=F32) / l
    f = jnp.linspace(1e-4, bands - 1, bands, dtype=F32)
    ang = w[:, None] * f[None, :]
    feats = jnp.concatenate([t, jnp.cos(ang), -jnp.sin(ang)], axis=-1)
    hdn = jnp.sin(freq[0] * (feats @ w1 + b1))
    hdn = jnp.sin(freq[1] * (hdn @ w2 + b2))
    n_out = w3.shape[-1]
    deltas = jnp.linspace(math.log(HY_TARGET) / HY_SLOW_PCT, math.log(HY_TARGET) / HY_FAST_PCT, n_out, dtype=F32)
    decay = jnp.exp(-t * jnp.abs(deltas))
    pos = jnp.arange(l)[:, None]
    parts = []
    for q in range(n_out // BRANCH_W):
        sl = slice(q * BRANCH_W, (q + 1) * BRANCH_W)
        f = (hdn @ w3[:, sl]).astype(F32) * decay[:, sl]
        parts.append(jnp.where(pos == 0, 0.0, f) if q % 2 else f)
    return jnp.stack(parts).reshape(n_out // (2 * BRANCH_W), 2, l, BRANCH_W)


def _hyena(z, conv_w, conv_b, filt_w, skip, nb, lat_len):
    m = z.shape[0]
    t = m // nb
    lc = t - lat_len
    w = BRANCH_W
    p = nb // 2
    n1, n2 = FFT_N1, 2 * lat_len // FFT_N1
    r = n1 // 2
    hl, hc = _hyconv(z, conv_w, conv_b, nb, lat_len)
    f1, f1r, f2, g2, f3 = _dft_tables(lat_len)
    ff, ffr, fi = _dft_tables_small(lc)
    jb, kb = min(16, n2), 16
    dp, fp = DFT_PASSES, DFT_FILTER_PASSES
    filt_l = _hyena_filters(lat_len, *filt_w)
    fa = _fft1(filt_l.reshape(1, 4, 1, r, n2, w), 0, f1r, jb, fp)
    hs_l = _fft2_filters(fa, f2, kb, fp)
    hl6 = hl.reshape(3, p, 2, r, n2, w)
    y, y_sel = hl6, 0
    for o in range(2):
        a = _fft1(y, y_sel, f1, jb, dp)
        bm = _fft2(a, hs_l, o, f2, g2, kb, dp)
        rows = r if o == 0 else t // n2
        y = _fft3(bm, f3, hl6, 1 + o, y, y_sel, skip[o:o + 1], rows, jb, dp)[None]
        y_sel = 0
    yh = y.reshape(p, 2, t, w)
    yh = _hyena_ctx(hc, _hyena_filters(lc, *filt_w), ff, ffr, fi, skip, yh, dp, fp)
    return yh.reshape(m, w)


def _merge_kernel(x_ref, mod_ref, za_ref, zprev_ref, znext_ref, edge_ref, cw_ref, cb_ref, yr_ref, yh_ref, yd_ref,
                  wg_ref, bg_ref, wb_ref, wo_ref, g_ref, b_ref, *rest, alpha, route):
    x = x_ref[...]
    u = _modulate(x, mod_ref, 0, 1).astype(BF16)

    w = BRANCH_W
    z = za_ref[...]
    tm = z.shape[0]
    p = z[:, w:2 * w] * z[:, 2 * w:]
    p_before = zprev_ref[0, 7:8, w:2 * w] * zprev_ref[0, 7:8, 2 * w:]
    p_after = znext_ref[0, 0:1, w:2 * w] * znext_ref[0, 0:1, 2 * w:]
    r_idx = lax.broadcasted_iota(jnp.int32, p.shape, 0)
    pm = jnp.where(r_idx == 0, p_before, pltpu.roll(p, 1, 0)) * edge_ref[:, 0:1]
    pp = jnp.where(r_idx == tm - 1, p_after, pltpu.roll(p, tm - 1, 0)) * edge_ref[:, 1:2]
    y_a = z[:, :w] * (pm * cw_ref[0:1, :] + p * cw_ref[1:2, :] + pp * cw_ref[2:3, :] + cb_ref[...])

    acc = None
    for i, y in enumerate((y_a, yr_ref[...], yh_ref[...], yd_ref[...])):
        gate = jax.nn.sigmoid(_dot(u, wg_ref[i]) + bg_ref[i:i + 1, :])
        term = gate * _dot(y.astype(BF16), wb_ref[i])
        acc = term if acc is None else acc + term
    mix = _dot(acc.astype(BF16), wo_ref[...])
    x_new = _gated_residual_ln(x, mix, mod_ref, 2, alpha, g_ref[...], b_ref[...])
    if not route:
        (o_ref,) = rest
        o_ref[...] = x_new
        return
    wr_ref, o_ref, uw_ref, lg_ref = rest
    o_ref[...] = x_new
    u2 = _modulate(x_new, mod_ref, 3, 4)
    lg_ref[...] = jnp.dot(u2, wr_ref[...], preferred_element_type=F32, precision=lax.Precision.HIGHEST)
    half = u2.shape[1] // 2
    bits = lax.bitcast_convert_type(u2.astype(BF16).astype(F32), jnp.uint32)
    uw_ref[...] = lax.bitcast_convert_type(bits[:, :half] | (bits[:, half:] >> 16), F32)


def _merge(x, modx, za, edge, conv_w, conv_b, ys, wg, bg, wb, wo, layer, g, b, alpha, tm, router=None):
    m, d = x.shape
    w = BRANCH_W
    row = lambda i: (i, 0)
    const2 = lambda i: (0, 0)
    one = pl.Buffered(1)
    stacked = lambda a: pl.BlockSpec((None,) + a.shape[1:], lambda i: (layer,) + (0,) * (a.ndim - 1),
                                     pipeline_mode=one)
    nblk8 = m // 8
    za8 = za.reshape(nblk8, 8, 3 * w)
    in_specs = [pl.BlockSpec((tm, d), row),
                pl.BlockSpec((tm // SUB, 8, d), lambda i: (i, 0, 0)),
                pl.BlockSpec((tm, 3 * w), row),
                pl.BlockSpec((1, 8, 3 * w), lambda i: (jnp.maximum(i * (tm // 8) - 1, 0), 0, 0)),
                pl.BlockSpec((1, 8, 3 * w), lambda i: (jnp.minimum((i + 1) * (tm // 8), nblk8 - 1), 0, 0)),
                pl.BlockSpec((tm, 2), row),
                pl.BlockSpec((3, w), const2), pl.BlockSpec((1, w), const2),
                pl.BlockSpec((tm, w), row), pl.BlockSpec((tm, w), row), pl.BlockSpec((tm, w), row),
                stacked(wg), pl.BlockSpec(bg.shape, const2), stacked(wb), stacked(wo),
                pl.BlockSpec((1, d), const2), pl.BlockSpec((1, d), const2)]
    args = [x, modx, za, za8, za8, edge, conv_w, conv_b.reshape(1, w), *ys, wg, bg, wb, wo,
            g.reshape(1, d), b.reshape(1, d)]
    out_specs = [pl.BlockSpec((tm, d), row)]
    out_shape = [jax.ShapeDtypeStruct((m, d), F32)]
    if router is not None:
        in_specs.append(pl.BlockSpec((d, LANE), const2))
        args.append(router)
        out_specs += [pl.BlockSpec((tm, d // 2), row), pl.BlockSpec((tm, LANE), row)]
        out_shape += [jax.ShapeDtypeStruct((m, d // 2), F32), jax.ShapeDtypeStruct((m, LANE), F32)]
    out = pl.pallas_call(
        functools.partial(_merge_kernel, alpha=alpha, route=router is not None),
        grid=(m // tm,),
        in_specs=in_specs,
        out_specs=out_specs,
        out_shape=out_shape,
        compiler_params=_cp(("parallel",)),
        name="branch_merge",
    )(*args)
    return out[0] if router is None else out


def _ffn_kernel(x_ref, mod_ref, w1_ref, w3_ref, w2_ref, g_ref, b_ref, o_ref, *, alpha, tf):
    x = x_ref[...]
    u = _modulate(x, mod_ref, 3, 4).astype(BF16)
    acc = None
    for j in range(w1_ref.shape[1] // tf):
        sl = slice(j * tf, (j + 1) * tf)
        h1 = _dot(u, w1_ref[:, sl])
        h3 = _dot(u, w3_ref[:, sl])
        part = _dot((h1 * jax.nn.sigmoid(h1) * h3).astype(BF16), w2_ref[sl, :])
        acc = part if acc is None else acc + part
    o_ref[...] = _gated_residual_ln(x, acc, mod_ref, 5, alpha, g_ref[...], b_ref[...])


def _ffn(x, modx, w1, w3, w2, layer, g, b, alpha, tm, tf):
    m, d = x.shape
    resident = lambda a: pl.BlockSpec((None,) + a.shape[1:], lambda i: (layer, 0, 0), pipeline_mode=pl.Buffered(1))
    return pl.pallas_call(
        functools.partial(_ffn_kernel, alpha=alpha, tf=tf),
        grid=(m // tm,),
        in_specs=[pl.BlockSpec((tm, d), lambda i: (i, 0)),
                  pl.BlockSpec((tm // SUB, 8, d), lambda i: (i, 0, 0)),
                  resident(w1), resident(w3), resident(w2),
                  pl.BlockSpec((1, d), lambda i: (0, 0)),
                  pl.BlockSpec((1, d), lambda i: (0, 0))],
        out_specs=pl.BlockSpec((tm, d), lambda i: (i, 0)),
        out_shape=jax.ShapeDtypeStruct((m, d), F32),
        compiler_params=_cp(("parallel",)),
        name="dense_swiglu",
    )(x, modx, w1, w3, w2, g.reshape(1, d), b.reshape(1, d))


def _moe_kernel(be_ref, nu_ref, xb_ref, w1_ref, w3_ref, w2_ref, o_ref, acc_ref, u_ref):
    i, j = pl.program_id(0), pl.program_id(1)
    live = i < nu_ref[0]

    @pl.when(j == 0)
    def _():
        acc_ref[...] = jnp.zeros_like(acc_ref)
        words = lax.bitcast_convert_type(xb_ref[...], jnp.uint32)
        half = words.shape[1]
        hi = lax.bitcast_convert_type(words & jnp.uint32(0xFFFF0000), F32)
        lo = lax.bitcast_convert_type(words << 16, F32)
        u_ref[:, :half] = hi.astype(BF16)
        u_ref[:, half:] = lo.astype(BF16)

    @pl.when(live)
    def _():
        u = u_ref[...]
        h1 = _dot(u, w1_ref[0])
        h3 = _dot(u, w3_ref[0])
        acc_ref[...] += _dot((h1 * jax.nn.sigmoid(h1) * h3).astype(BF16), w2_ref[0])

    @pl.when(j == pl.num_programs(1) - 1)
    def _():
        o_ref[...] = acc_ref[...]


def _moe_experts(xb, blk_e, n_used, w1, w3, w2, layer, tm, tf):
    r = xb.shape[0]
    d = w1.shape[2]
    ff = w1.shape[3]
    nj = ff // tf

    def wmap(i, j, be, nu):
        return (layer, be[i], 0, jnp.where(i < nu[0], j, nj - 1))

    def w2map(i, j, be, nu):
        return (layer, be[i], jnp.where(i < nu[0], j, nj - 1), 0)

    grid_spec = pltpu.PrefetchScalarGridSpec(
        num_scalar_prefetch=2,
        grid=(r // tm, nj),
        in_specs=[pl.BlockSpec((tm, d // 2), lambda i, j, be, nu: (i, 0)),
                  pl.BlockSpec((None, 1, d, tf), wmap),
                  pl.BlockSpec((None, 1, d, tf), wmap),
                  pl.BlockSpec((None, 1, tf, d), w2map)],
        out_specs=pl.BlockSpec((tm, d), lambda i, j, be, nu: (i, 0)),
        scratch_shapes=[pltpu.VMEM((tm, d), F32), pltpu.VMEM((tm, d), BF16)],
    )
    return pl.pallas_call(
        _moe_kernel,
        grid_spec=grid_spec,
        out_shape=jax.ShapeDtypeStruct((r, d), F32),
        compiler_params=_cp(("arbitrary", "arbitrary")),
        name="moe_experts",
    )(blk_e, n_used, xb, w1, w3, w2)


def _ln_res_kernel(x_ref, f0_ref, f1_ref, wt_ref, mod_ref, g_ref, b_ref, o_ref, *, alpha):
    f = f0_ref[...] * wt_ref[:, 0:1] + f1_ref[...] * wt_ref[:, 1:2]
    o_ref[...] = _gated_residual_ln(x_ref[...], f, mod_ref, 5, alpha, g_ref[...], b_ref[...])


def _ln_res(x, f01, wts, modx, g, b, alpha, tm):
    m, d = x.shape
    nt = m // tm
    return pl.pallas_call(
        functools.partial(_ln_res_kernel, alpha=alpha),
        grid=(nt,),
        in_specs=[pl.BlockSpec((tm, d), lambda i: (i, 0)),
                  pl.BlockSpec((tm, d), lambda i: (i, 0)),
                  pl.BlockSpec((tm, d), lambda i: (i + nt, 0)),
                  pl.BlockSpec((tm, TOP_K), lambda i: (i, 0)),
                  pl.BlockSpec((tm // SUB, 8, d), lambda i: (i, 0, 0)),
                  pl.BlockSpec((1, d), lambda i: (0, 0)), pl.BlockSpec((1, d), lambda i: (0, 0))],
        out_specs=pl.BlockSpec((tm, d), lambda i: (i, 0)),
        out_shape=jax.ShapeDtypeStruct((m, d), F32),
        compiler_params=_cp(("parallel",)),
        name="moe_residual_ln",
    )(x, f01, f01, wts, modx, g.reshape(1, d), b.reshape(1, d))


def _moe(x, u, logits, modx, w1, w3, w2, layer, g, b, alpha, tm_rows, tm, tf):
    m, d = x.shape
    top_v, top_i = lax.top_k(logits[:, :N_EXPERTS], TOP_K)
    wts = jax.nn.softmax(top_v, axis=-1)
    e_flat = top_i.reshape(-1).astype(jnp.int32)
    n_pair = m * TOP_K
    onehot = (e_flat[:, None] == jnp.arange(N_EXPERTS, dtype=jnp.int32)[None, :]).astype(jnp.int32)
    csum = jnp.cumsum(onehot, axis=0)
    counts = csum[-1]
    rank = jnp.take_along_axis(csum, e_flat[:, None], axis=1)[:, 0] - 1
    padded = (counts + tm - 1) // tm * tm
    pad_end = jnp.cumsum(padded)
    pad_start = pad_end - padded
    dest = pad_start[e_flat] + rank
    n_blk = -(-n_pair // tm) + N_EXPERTS
    r = n_blk * tm
    tok = jnp.arange(n_pair, dtype=jnp.int32) // TOP_K
    n_used = (pad_end[-1] // tm).astype(jnp.int32)
    blk = jnp.minimum(jnp.arange(n_blk, dtype=jnp.int32), n_used - 1) * tm
    blk_e = jnp.minimum(jnp.searchsorted(pad_end, blk, side='right'), N_EXPERTS - 1).astype(jnp.int32)
    _, tok_sorted = lax.sort_key_val(dest, tok)
    shift = pad_start - (jnp.cumsum(counts) - counts)
    src = jnp.arange(r, dtype=jnp.int32) - jnp.repeat(shift[blk_e], tm).astype(jnp.int32)
    row_tok = jnp.take(tok_sorted, jnp.clip(src, 0, n_pair - 1), mode="clip")
    xb = jnp.take(u, row_tok, axis=0, mode="clip")
    yb = _moe_experts(xb, blk_e, n_used.reshape(1), w1, w3, w2, layer, tm, tf)
    f01 = jnp.take(yb, dest.reshape(m, TOP_K).T.reshape(-1), axis=0, mode="clip")
    return _ln_res(x, f01, wts, modx, g, b, alpha, tm_rows)


def _rope_tables(nb, lat_len, ctx_len):
    rows = lat_len // GRID_W
    row = jnp.repeat(jnp.arange(rows, dtype=F32), GRID_W)
    col = jnp.tile(jnp.arange(GRID_W, dtype=F32), rows)
    n_freq = HEAD_DIM // 4
    inv = ROPE_THETA ** (-jnp.arange(n_freq, dtype=F32) / n_freq)
    ar, ac = row[:, None] * inv, col[:, None] * inv
    cos = jnp.concatenate([jnp.cos(ar), jnp.cos(ar), jnp.cos(ac), jnp.cos(ac)], axis=1)
    sin = jnp.concatenate([-jnp.sin(ar), jnp.sin(ar), -jnp.sin(ac), jnp.sin(ac)], axis=1)
    cos = jnp.concatenate([cos, jnp.ones((ctx_len, HEAD_DIM), F32)], axis=0)
    sin = jnp.concatenate([sin, jnp.zeros((ctx_len, HEAD_DIM), F32)], axis=0)
    return jnp.tile(cos, (nb, 1)), jnp.tile(sin, (nb, 1))


def _row_tile(m, pref):
    tm = pref
    while m % tm:
        tm //= 2
    return tm


def kernel(x, c, ctx, c_ctx, w_mod, b_mod, w_in, conv_a_w, conv_a_b, ret_decay, hy_conv_w, hy_conv_b, hy_w1, hy_b1, hy_w2, hy_b2, hy_w3, hy_freq, hy_skip, q_norm, k_norm, w_gate, b_gate, w_branch, w_o, ln_g, ln_b, ffn_w1, ffn_w3, ffn_w2, router, moe_w1, moe_w3, moe_w2):
    nb, lat_len, d = x.shape
    ctx_len = ctx.shape[1]
    depth = w_mod.shape[0]
    t = lat_len + ctx_len
    m = nb * t
    alpha = float((2 * depth) ** 0.25)
    assert lat_len % SUB == 0 and ctx_len % SUB == 0 and nb % 2 == 0

    act = jnp.concatenate([jax.nn.silu(c), jax.nn.silu(c_ctx)[None, :],
                           jnp.zeros((8 - nb - 1, d), F32)], axis=0)
    mods = _mod_all(act, w_mod, b_mod).reshape(depth, 8, N_MOD, d)
    mods = jnp.pad(mods, ((0, 0), (0, 0), (0, 8 - N_MOD), (0, 0)))
    tiles_per_batch = t // SUB
    tile_pos = np.arange(m // SUB) % tiles_per_batch
    tile_group = np.where(tile_pos < lat_len // SUB, np.arange(m // SUB) // tiles_per_batch, nb)
    cos_t, sin_t = _rope_tables(nb, lat_len, ctx_len)
    pos = np.arange(m) % t
    edge = jnp.asarray(np.stack([(pos != 0) & (pos != lat_len), (pos != lat_len - 1) & (pos != t - 1)], axis=1),
                       dtype=F32)

    cast = lambda a: a.astype(BF16)
    w_in_b, w_gate_b, w_branch_b, w_o_b = cast(w_in), cast(w_gate), cast(w_branch), cast(w_o)
    ffn_b = (cast(ffn_w1), cast(ffn_w3), cast(ffn_w2))
    moe_w = (cast(moe_w1), cast(moe_w3), cast(moe_w2))

    rows = jnp.concatenate([x, ctx], axis=1).reshape(m, d)
    tm = _row_tile(m, 512)
    for l in range(depth):
        modx = mods[l][tile_group]
        za, zvg, zh, rq, rk, aq, ak, av = _proj(rows, modx, w_in_b, l, cos_t, sin_t, q_norm[l], k_norm[l], tm)
        y_d = _attention(aq, ak, av, nb, lat_len)
        log_gamma = -jnp.exp(ret_decay[l].astype(F32))
        y_r = _retention(rq, rk, zvg, _retention_tables(log_gamma), nb, lat_len)
        y_h = _hyena(zh, hy_conv_w[l], hy_conv_b[l],
                     (hy_w1[l], hy_b1[l], hy_w2[l], hy_b2[l], hy_w3[l], hy_freq[l]), hy_skip[l], nb, lat_len)
        i = l // 2
        wr_pad = None if l % 2 == 0 else jnp.zeros((d, LANE), F32).at[:, :N_EXPERTS].set(router[i])
        merged = _merge(rows, modx, za, edge, conv_a_w[l], conv_a_b[l], (y_r, y_h, y_d), w_gate_b, b_gate[l],
                        w_branch_b, w_o_b, l, ln_g[l, 0], ln_b[l, 0], alpha, tm, wr_pad)
        if l % 2 == 0:
            rows = _ffn(merged, modx, *ffn_b, i, ln_g[l, 1], ln_b[l, 1], alpha, tm, 256)
        else:
            rows, u_words, logits = merged
            rows = _moe(rows, u_words, logits, modx, *moe_w, i, ln_g[l, 1], ln_b[l, 1], alpha, tm, 768, 1792)
    return rows.reshape(nb, t, d)[:, :lat_len]
```

```python
import functools
import math

import numpy as np
import jax
import jax.numpy as jnp
from jax import lax
from jax.experimental import pallas as pl
from jax.experimental.pallas import tpu as pltpu

F32 = jnp.float32
BF16 = jnp.bfloat16

GRID_W = 64
BRANCH_W = 512
RET_HEADS = 4
RET_D = 128
RET_CHUNK = 128
HY_EMB = 33
HY_FAST_PCT = 0.3
HY_SLOW_PCT = 1.5
HY_TARGET = 1e-2
ATT_HEADS = 4
ATT_KV_HEADS = 2
HEAD_DIM = 128
ROPE_THETA = 10000.0
N_EXPERTS = 8
TOP_K = 2
N_MOD = 6
A_W = 3 * BRANCH_W
R_W = 4 * RET_HEADS * RET_D
H_W = 3 * BRANCH_W
D_W = (ATT_HEADS + 2 * ATT_KV_HEADS) * HEAD_DIM
IN_W = A_W + R_W + H_W + D_W
LN_EPS = 1e-6

SUB = 256
LANE = 128
FFT_N1 = 128
RET_UNROLL = 17
ATT_TQ = 512
ATT_KC = 2176
DFT_PASSES = 1
DFT_FILTER_PASSES = 1
VMEM_LIMIT = 56 * 1024 * 1024


def _cp(sem, vmem=VMEM_LIMIT):
    return pltpu.CompilerParams(dimension_semantics=sem, vmem_limit_bytes=vmem)


def _dot(a, b):
    return jnp.dot(a, b, preferred_element_type=F32)


def _dot_nt(a, b):
    return lax.dot_general(a, b, (((1,), (1,)), ((), ())), preferred_element_type=F32)


def _dot_tn(a, b):
    return lax.dot_general(a, b, (((0,), (0,)), ((), ())), preferred_element_type=F32)


def _split_dot(f_cat, x, passes):
    xh = x.astype(BF16)
    if passes == 1:
        return _dot(f_cat, xh)
    m = f_cat.shape[0] // 2
    r = _dot(f_cat, xh)
    r = r[:m] + r[m:]
    if passes == 3:
        r = r + _dot(f_cat[:m], (x - xh.astype(F32)).astype(BF16))
    return r


def _dft_operand(f_cat, passes):
    return f_cat[..., :f_cat.shape[-2] // 2, :] if passes == 1 else f_cat


def _modulate(x, mod_ref, shift_row, scale_row):
    parts = []
    for s in range(x.shape[0] // SUB):
        sh = mod_ref[s, shift_row:shift_row + 1, :]
        sc = mod_ref[s, scale_row:scale_row + 1, :]
        parts.append(x[s * SUB:(s + 1) * SUB] * (1.0 + sc) + sh)
    return parts[0] if len(parts) == 1 else jnp.concatenate(parts, axis=0)


def _gated_residual_ln(x, f, mod_ref, gate_row, alpha, g, b):
    parts = []
    for s in range(x.shape[0] // SUB):
        gt = mod_ref[s, gate_row:gate_row + 1, :]
        parts.append(alpha * x[s * SUB:(s + 1) * SUB] + gt * f[s * SUB:(s + 1) * SUB])
    r = parts[0] if len(parts) == 1 else jnp.concatenate(parts, axis=0)
    mu = jnp.mean(r, axis=-1, keepdims=True)
    rc = r - mu
    var = jnp.mean(rc * rc, axis=-1, keepdims=True)
    return rc * lax.rsqrt(var + LN_EPS) * g + b


def _mod_kernel(a_ref, w_ref, b_ref, o_ref):
    o_ref[0] = _dot(a_ref[...].astype(BF16), w_ref[0].astype(BF16)) + b_ref[0]


def _mod_all(act, w_mod, b_mod):
    depth, d, n = w_mod.shape
    tn = 512
    return pl.pallas_call(
        _mod_kernel,
        grid=(depth, n // tn),
        in_specs=[pl.BlockSpec((8, d), lambda l, j: (0, 0)),
                  pl.BlockSpec((1, d, tn), lambda l, j: (l, 0, j)),
                  pl.BlockSpec((1, 1, tn), lambda l, j: (l, 0, j))],
        out_specs=pl.BlockSpec((1, 8, tn), lambda l, j: (l, 0, j)),
        out_shape=jax.ShapeDtypeStruct((depth, 8, n), F32),
        compiler_params=_cp(("parallel", "parallel")),
        name="mod_vectors",
    )(act, w_mod, b_mod.reshape(depth, 1, n))


PROJ_TN = 512
assert (A_W, R_W, H_W, D_W) == (3 * PROJ_TN, 4 * PROJ_TN, 3 * PROJ_TN, 2 * PROJ_TN)


def _proj_kernel(x_ref, mod_ref, w_ref, cos_ref, sin_ref, gq_ref, gk_ref,
                 za_ref, zvg_ref, zh_ref, rq_ref, rk_ref, aq_ref, ak_ref, av_ref):
    u = _modulate(x_ref[...], mod_ref, 0, 1).astype(BF16)
    tn = PROJ_TN
    cos, sin = cos_ref[...], sin_ref[...]
    lane = lax.broadcasted_iota(jnp.int32, cos.shape, 1)
    first_half = (lane % 64) < 32

    def tile(j):
        return _dot(u, w_ref[:, j * tn:(j + 1) * tn])

    def rope(x):
        partner = jnp.where(first_half, pltpu.roll(x, 96, 1), pltpu.roll(x, 32, 1))
        return x * cos + partner * sin

    def rms(x, g):
        return x * lax.rsqrt(jnp.mean(x * x, axis=-1, keepdims=True) + LN_EPS) * g

    head = lambda h: slice(h * LANE, (h + 1) * LANE)

    for j in range(3):
        za_ref[:, j * tn:(j + 1) * tn] = tile(j)
    r = tile(3)
    for h in range(RET_HEADS):
        rq_ref[:, head(h)] = rope(r[:, head(h)])
    r = tile(4)
    for h in range(RET_HEADS):
        rk_ref[:, head(h)] = rope(r[:, head(h)] * (RET_D ** -0.5))
    for j in range(2):
        zvg_ref[:, j * tn:(j + 1) * tn] = tile(5 + j)
    for j in range(3):
        zh_ref[:, j * tn:(j + 1) * tn] = tile(7 + j)
    q_scale = (HEAD_DIM ** -0.5) * math.log2(math.e)
    r = tile(10)
    for h in range(ATT_HEADS):
        aq_ref[:, head(h)] = (rope(rms(r[:, head(h)], gq_ref[...])) * q_scale).astype(BF16)
    r = tile(11)
    ones_col = jnp.where(lane == 0, 1.0, 0.0).astype(BF16)
    for h in range(ATT_KV_HEADS):
        ak_ref[:, head(h)] = rope(rms(r[:, head(h)], gk_ref[...])).astype(BF16)
        av_ref[:, head(2 * h)] = r[:, head(ATT_KV_HEADS + h)].astype(BF16)
        av_ref[:, head(2 * h + 1)] = ones_col


def _proj(x, modx, w, layer, cos_t, sin_t, gq, gk, tm):
    m, d = x.shape
    n = w.shape[2]
    tn = PROJ_TN
    qw, kw = ATT_HEADS * HEAD_DIM, ATT_KV_HEADS * HEAD_DIM
    row = lambda i: (i, 0)
    widths = (A_W, 2 * tn, H_W, tn, tn, qw, kw, 2 * kw)
    dtypes = (F32,) * 5 + (BF16,) * 3
    return pl.pallas_call(
        _proj_kernel,
        grid=(m // tm,),
        in_specs=[pl.BlockSpec((tm, d), row),
                  pl.BlockSpec((tm // SUB, 8, d), lambda i: (i, 0, 0)),
                  pl.BlockSpec((None, d, n), lambda i: (layer, 0, 0), pipeline_mode=pl.Buffered(1)),
                  pl.BlockSpec((tm, LANE), row), pl.BlockSpec((tm, LANE), row),
                  pl.BlockSpec((1, LANE), lambda i: (0, 0)), pl.BlockSpec((1, LANE), lambda i: (0, 0))],
        out_specs=[pl.BlockSpec((tm, wd), row) for wd in widths],
        out_shape=[jax.ShapeDtypeStruct((m, wd), dt) for wd, dt in zip(widths, dtypes)],
        compiler_params=_cp(("parallel",)),
        name="in_proj",
    )(x, modx, w, cos_t, sin_t, gq.reshape(1, LANE), gk.reshape(1, LANE))


def _attn_kernel(q_ref, k_ref, v_ref, *rest, key_lo, key_hi, kc):
    o_ref = rest[-1]
    tq = q_ref.shape[1]
    grp = ATT_HEADS // ATT_KV_HEADS
    q2 = jnp.concatenate([q_ref[0, :, g * HEAD_DIM:(g + 1) * HEAD_DIM] for g in range(grp)], axis=0)
    m_run, acc = None, None
    for lo in range(key_lo, key_hi, kc):
        hi = min(lo + kc, key_hi)
        s = _dot_nt(q2, k_ref[0, lo:hi, :])
        m_chunk = jnp.max(s, axis=-1, keepdims=True)
        m_new = m_chunk if m_run is None else jnp.maximum(m_run, m_chunk)
        pv = _dot(jnp.exp2(s - m_new).astype(BF16), v_ref[0, lo:hi, :])
        acc = pv if acc is None else jnp.exp2(m_run - m_new) * acc + pv
        m_run = m_new
    for g in range(grp):
        a = acc[g * tq:(g + 1) * tq]
        o_ref[0, :, g * HEAD_DIM:(g + 1) * HEAD_DIM] = a[:, :HEAD_DIM] / a[:, HEAD_DIM:HEAD_DIM + 1]


def _attention(aq, ak, av, nb, lat_len):
    m = aq.shape[0]
    t = m // nb
    lc = t - lat_len
    grp_w = (ATT_HEADS // ATT_KV_HEADS) * HEAD_DIM
    args = (aq.reshape(nb, t, -1), ak.reshape(nb, t, -1), av.reshape(nb, t, -1))

    def call(tq, n_tiles, first_tile, key_lo, key_hi, prev):
        qmap = lambda b, kv, i: (b, first_tile + i, kv)
        specs = [pl.BlockSpec((1, tq, grp_w), qmap),
                 pl.BlockSpec((1, t, HEAD_DIM), lambda b, kv, i: (b, 0, kv)),
                 pl.BlockSpec((1, t, 2 * HEAD_DIM), lambda b, kv, i: (b, 0, kv))]
        extra, alias = (), {}
        if prev is not None:
            specs.append(pl.BlockSpec(memory_space=pl.ANY))
            extra, alias = (prev,), {3: 0}
        return pl.pallas_call(
            functools.partial(_attn_kernel, key_lo=key_lo, key_hi=key_hi, kc=ATT_KC),
            grid=(nb, ATT_KV_HEADS, n_tiles),
            in_specs=specs,
            out_specs=pl.BlockSpec((1, tq, grp_w), qmap),
            out_shape=jax.ShapeDtypeStruct((nb, t, ATT_HEADS * HEAD_DIM), F32),
            input_output_aliases=alias,
            compiler_params=_cp(("parallel", "parallel", "arbitrary")),
            name="gqa_attention",
        )(*args, *extra)

    tq = math.gcd(ATT_TQ, lat_len)
    out = call(tq, lat_len // tq, 0, 0, t, None)
    out = call(lc, 1, t // lc - 1, lat_len, t, out)
    return out.reshape(m, -1)


def _ret_kernel(q_ref, k_ref, v_ref, g_ref, intra_ref, qd_ref, kd_ref, cd_ref, o_ref, st_ref,
                *, n_lat, n_ctx):
    c_len = RET_CHUNK
    n = n_lat + n_ctx

    def rows(c):
        return pl.ds(pl.multiple_of(c * c_len, c_len), c_len)

    def kv(d, c):
        return _dot_tn((k_ref[0, rows(c), :] * kd_ref[d, 0]).astype(BF16), v_ref[0, rows(c), :].astype(BF16))

    def scan_step(t, carry):
        sf, sb = carry
        cf = jnp.where(t < n_ctx, n_lat + t, t - n_ctx)
        cb = jnp.where(t < n_ctx, n - 1 - t, n_lat - 1 - (t - n_ctx))
        st_ref[cf, :RET_D, :] = sf.astype(BF16)
        st_ref[cb, RET_D:, :] = sb.astype(BF16)
        return (cd_ref[0, 0, 0:1, :] * sf + kv(0, cf), cd_ref[1, 0, 0:1, :] * sb + kv(1, cb))

    zero = jnp.zeros((RET_D, RET_D), F32)
    lax.fori_loop(0, n, scan_step, (zero, zero), unroll=RET_UNROLL)

    mask = intra_ref[0, 0] + intra_ref[1, 0]

    def out_step(c, carry):
        qc = q_ref[0, rows(c), :]
        vc = v_ref[0, rows(c), :].astype(BF16)
        s = _dot_nt(qc.astype(BF16), k_ref[0, rows(c), :].astype(BF16)) * mask
        qdec = jnp.concatenate([qc * qd_ref[0, 0], qc * qd_ref[1, 0]], axis=1).astype(BF16)
        o = _dot(s.astype(BF16), vc) + _dot(qdec, st_ref[c])
        mu = jnp.mean(o, axis=-1, keepdims=True)
        oc = o - mu
        var = jnp.mean(oc * oc, axis=-1, keepdims=True)
        g = g_ref[0, rows(c), :]
        o_ref[0, rows(c), :] = (g * jax.nn.sigmoid(g)) * (oc * lax.rsqrt(var + LN_EPS))
        return carry

    lax.fori_loop(0, n, out_step, 0, unroll=RET_UNROLL)


def _retention(rq, rk, z, tabs, nb, lat_len):
    m = rq.shape[0]
    t = m // nb
    intra, qd, kd, cd = tabs
    v_blk, g_blk = 0, RET_HEADS
    kern = functools.partial(_ret_kernel, n_lat=lat_len // RET_CHUNK, n_ctx=(t - lat_len) // RET_CHUNK)
    seq = lambda off: pl.BlockSpec((1, t, LANE), lambda b, h: (b, 0, off + h))
    tab = lambda r: pl.BlockSpec((2, 1, r, LANE), lambda b, h: (0, h, 0, 0))
    out = pl.pallas_call(
        kern,
        grid=(nb, RET_HEADS),
        in_specs=[seq(0), seq(0), seq(v_blk), seq(g_blk),
                  tab(RET_CHUNK), tab(RET_CHUNK), tab(RET_CHUNK), tab(8)],
        out_specs=seq(0),
        out_shape=jax.ShapeDtypeStruct((nb, t, RET_HEADS * RET_D), F32),
        scratch_shapes=[pltpu.VMEM((t // RET_CHUNK, 2 * RET_D, RET_D), BF16)],
        compiler_params=_cp(("parallel", "parallel")),
        name="retention",
    )(rq.reshape(nb, t, -1), rk.reshape(nb, t, -1), z.reshape(nb, t, -1), z.reshape(nb, t, -1), intra, qd, kd, cd)
    return out.reshape(m, -1)


def _retention_tables(log_gamma):
    c = RET_CHUNK
    i = jnp.arange(c, dtype=F32)
    lg = log_gamma.astype(F32)[:, :, None]
    rel = i[:, None] - i[None, :]
    intra_f = jnp.where(rel >= 0, jnp.exp(lg[0][..., None] * jnp.maximum(rel, 0.0)), 0.0)
    intra_b = jnp.where(rel <= 0, jnp.exp(lg[1][..., None] * jnp.maximum(-rel, 0.0)), 0.0)
    qd = jnp.stack([jnp.exp(lg[0] * (i + 1.0)), jnp.exp(lg[1] * (c - i))])
    kd = jnp.stack([jnp.exp(lg[0] * (c - 1.0 - i)), jnp.exp(lg[1] * i)])
    cd = jnp.exp(lg * c)
    bc = lambda a: jnp.broadcast_to(a[..., None], a.shape + (LANE,))
    return (jnp.stack([intra_f, intra_b]), bc(qd), bc(kd),
            jnp.broadcast_to(cd[..., None], (2, RET_HEADS, 8, LANE)))


def _shift_rows(x, lat_len):
    t = x.shape[0]
    row = lax.broadcasted_iota(jnp.int32, x.shape, 0)
    xm = jnp.where((row == 0) | (row == lat_len), 0.0, pltpu.roll(x, 1, 0))
    xp = jnp.where((row == lat_len - 1) | (row == t - 1), 0.0, pltpu.roll(x, t - 1, 0))
    return xm, xp


def _shortconv_kernel(bg_ref, cg_ref, xv_ref, w_ref, b_ref, o_ref, *, lat_len):
    p = cg_ref[0] * xv_ref[0]
    pm, pp = _shift_rows(p, lat_len)
    o_ref[0] = bg_ref[0] * (pm * w_ref[0:1, :] + p * w_ref[1:2, :] + pp * w_ref[2:3, :] + b_ref[...])


def _shortconv(z, w, b, nb, lat_len):
    m = z.shape[0]
    t = m // nb
    nct = BRANCH_W // LANE
    kern = functools.partial(_shortconv_kernel, lat_len=lat_len)
    seq = lambda off: pl.BlockSpec((1, t, LANE), lambda bb, j: (bb, 0, off + j))
    out = pl.pallas_call(
        kern,
        grid=(nb, nct),
        in_specs=[seq(0), seq(nct), seq(2 * nct),
                  pl.BlockSpec((3, LANE), lambda bb, j: (0, j)),
                  pl.BlockSpec((1, LANE), lambda bb, j: (0, j))],
        out_specs=seq(0),
        out_shape=jax.ShapeDtypeStruct((nb, t, BRANCH_W), F32),
        compiler_params=_cp(("parallel", "parallel")),
        name="short_conv",
    )(z.reshape(nb, t, -1), z.reshape(nb, t, -1), z.reshape(nb, t, -1), w, b.reshape(1, -1))
    return out.reshape(m, -1)


def _hyconv_kernel(z_ref, w_ref, b_ref, ol_ref, oc_ref, *, lat_len):
    x = z_ref[0]
    xm, xp = _shift_rows(x, lat_len)
    y = xm * w_ref[0:1, :] + x * w_ref[1:2, :] + xp * w_ref[2:3, :] + b_ref[...]
    ol_ref[0, 0] = y[:lat_len]
    oc_ref[0, 0] = y[lat_len:]


def _hyconv(z, w, b, nb, lat_len):
    m = z.shape[0]
    t = m // nb
    nct = BRANCH_W // LANE
    off = 0
    kern = functools.partial(_hyconv_kernel, lat_len=lat_len)
    return pl.pallas_call(
        kern,
        grid=(nb, 3 * nct),
        in_specs=[pl.BlockSpec((1, t, LANE), lambda bb, j: (bb, 0, off + j)),
                  pl.BlockSpec((3, LANE), lambda bb, j: (0, j)),
                  pl.BlockSpec((1, LANE), lambda bb, j: (0, j))],
        out_specs=[pl.BlockSpec((1, 1, lat_len, LANE), lambda bb, j: (j // nct, bb, 0, j % nct)),
                   pl.BlockSpec((1, 1, t - lat_len, LANE), lambda bb, j: (j // nct, bb, 0, j % nct))],
        out_shape=[jax.ShapeDtypeStruct((3, nb, lat_len, BRANCH_W), F32),
                   jax.ShapeDtypeStruct((3, nb, t - lat_len, BRANCH_W), F32)],
        compiler_params=_cp(("parallel", "parallel")),
        name="hyena_dwconv",
    )(z.reshape(nb, t, -1), w, b.reshape(1, -1))


def _stack_c(m):
    return np.block([[m.real, -m.imag], [m.imag, m.real]])


def _split_np(m):
    m32 = np.asarray(m, np.float32)
    hi = m32.astype(BF16)
    lo = (m32 - hi.astype(np.float32)).astype(BF16)
    return np.concatenate([hi, lo], axis=-2)


@functools.lru_cache(maxsize=None)
def _dft_tables(lat_len):
    n = 2 * lat_len
    n1, n2 = FFT_N1, 2 * lat_len // FFT_N1
    a1 = np.arange(n1)
    a2 = np.arange(n2)
    w1 = np.exp(-2j * np.pi * np.outer(a1, a1[:n1 // 2]) / n1)
    f1 = _stack_c(w1)
    w2 = np.exp(-2j * np.pi * np.outer(a2, a2) / n2)
    tw = np.exp(-2j * np.pi * np.outer(a1, a2) / n)
    cf = w2[None, :, :] * tw[:, None, :]
    f2 = np.stack([_stack_c(cf[k]) for k in range(n1)])
    g2 = np.stack([_stack_c(cf[k].conj().T) for k in range(n1)])
    w1i = np.exp(2j * np.pi * np.outer(a1[:n1 // 2], a1) / n1) / n
    f3 = _stack_c(w1i)
    f1_real = f1[:, :n1 // 2]
    return _split_np(f1), _split_np(f1_real), _split_np(f2), _split_np(g2), _split_np(f3)


@functools.lru_cache(maxsize=None)
def _dft_tables_small(ctx_len):
    n = 2 * ctx_len
    a = np.arange(n)
    ff = _stack_c(np.exp(-2j * np.pi * np.outer(a, a[:ctx_len]) / n))
    fi = _stack_c(np.exp(2j * np.pi * np.outer(a[:ctx_len], a) / n) / n)
    return _split_np(ff), _split_np(ff[:, :ctx_len]), _split_np(fi)


def _fft1_kernel(x_ref, f_ref, o_ref, *, passes):
    f = f_ref[...]
    for j in range(x_ref.shape[3]):
        parts = [x_ref[0, b, :, j, :] for b in range(x_ref.shape[1])]
        r = _split_dot(f, parts[0] if len(parts) == 1 else jnp.concatenate(parts, axis=0), passes)
        h = r.shape[0] // 2
        o_ref[0, 0, :, j, :] = r[:h]
        o_ref[0, 1, :, j, :] = r[h:]


def _fft1(x, sel, f1, jb, passes):
    _, p, mem, r, n2, w = x.shape
    f1 = _dft_operand(f1, passes)
    n1 = f1.shape[0] // (2 if passes == 1 else 4)
    return pl.pallas_call(
        functools.partial(_fft1_kernel, passes=passes),
        grid=(p, n2 // jb),
        in_specs=[pl.BlockSpec((None, 1, mem, r, jb, w), lambda pp, j: (sel, pp, 0, 0, j, 0)),
                  pl.BlockSpec(f1.shape, lambda pp, j: (0, 0))],
        out_specs=pl.BlockSpec((1, 2, n1, jb, w), lambda pp, j: (pp, 0, 0, j, 0)),
        out_shape=jax.ShapeDtypeStruct((p, 2, n1, n2, w), F32),
        compiler_params=_cp(("parallel", "parallel")),
        name="hyena_fft_stage1",
    )(x, f1)


def _fft2f_kernel(a_ref, f_ref, o_ref, *, kb, n2, passes):
    w = a_ref.shape[-1]
    for t in range(kb):
        xf = _split_dot(f_ref[t], a_ref[0, :, t].reshape(2 * n2, w), passes)
        xb = _split_dot(f_ref[t], a_ref[1, :, t].reshape(2 * n2, w), passes)
        o_ref[0, 0, t] = xf[:n2] + xb[:n2]
        o_ref[0, 1, t] = xf[n2:] - xb[n2:]


def _fft2_filters(a, f2, kb, passes):
    q, _, n1, n2, w = a.shape
    f2 = _dft_operand(f2, passes)
    return pl.pallas_call(
        functools.partial(_fft2f_kernel, kb=kb, n2=n2, passes=passes),
        grid=(q // 2, n1 // kb),
        in_specs=[pl.BlockSpec((2, 2, kb, n2, w), lambda o, k: (o, 0, k, 0, 0)),
                  pl.BlockSpec((kb,) + f2.shape[1:], lambda o, k: (k, 0, 0))],
        out_specs=pl.BlockSpec((1, 2, kb, n2, w), lambda o, k: (o, 0, k, 0, 0)),
        out_shape=jax.ShapeDtypeStruct((q // 2, 2, n1, n2, w), F32),
        compiler_params=_cp(("parallel", "parallel")),
        name="hyena_filter_spectrum",
    )(a, f2)


def _fft2_kernel(a_ref, h_ref, f_ref, g_ref, o_ref, *, kb, n2, passes):
    w = a_ref.shape[-1]
    for t in range(kb):
        a = a_ref[0, :, t].reshape(2 * n2, w)
        x = _split_dot(f_ref[t], a, passes)
        xr, xi = x[:n2], x[n2:]
        hr, hi = h_ref[0, t], h_ref[1, t]
        y = jnp.concatenate([xr * hr - xi * hi, xr * hi + xi * hr], axis=0)
        o_ref[0, :, t] = _split_dot(g_ref[t], y, passes).reshape(2, n2, w)


def _fft2(a, hspec, order, f2, g2, kb, passes):
    p, _, n1, n2, w = a.shape
    f2, g2 = _dft_operand(f2, passes), _dft_operand(g2, passes)
    kern = functools.partial(_fft2_kernel, kb=kb, n2=n2, passes=passes)
    return pl.pallas_call(
        kern,
        grid=(n1 // kb, p),
        in_specs=[pl.BlockSpec((1, 2, kb, n2, w), lambda k, pp: (pp, 0, k, 0, 0)),
                  pl.BlockSpec((None, 2, kb, n2, w), lambda k, pp: (order, 0, k, 0, 0)),
                  pl.BlockSpec((kb,) + f2.shape[1:], lambda k, pp: (k, 0, 0)),
                  pl.BlockSpec((kb,) + g2.shape[1:], lambda k, pp: (k, 0, 0))],
        out_specs=pl.BlockSpec((1, 2, kb, n2, w), lambda k, pp: (pp, 0, k, 0, 0)),
        out_shape=jax.ShapeDtypeStruct(a.shape, F32),
        compiler_params=_cp(("parallel", "arbitrary")),
        name="hyena_fft_stage2",
    )(a, hspec, f2, g2)


def _fft3_kernel(b_ref, f_ref, gate_ref, prev_ref, skip_ref, o_ref, *, passes):
    f = f_ref[...]
    r = gate_ref.shape[2]
    for j in range(b_ref.shape[3]):
        bj = jnp.concatenate([b_ref[0, 0, :, j, :], b_ref[0, 1, :, j, :]], axis=0)
        conv = _split_dot(f, bj, passes)
        for mem in range(2):
            cm = conv[mem * r:(mem + 1) * r]
            o_ref[0, mem, :, j, :] = gate_ref[0, mem, :, j, :] * (cm + skip_ref[...] * prev_ref[0, mem, :, j, :])


def _fft3(bm, f3, gate, gate_sel, prev, prev_sel, skip, out_rows, jb, passes):
    p, _, n1, n2, w = bm.shape
    r = n1 // 2
    f3 = _dft_operand(f3, passes)
    seq = lambda s: pl.BlockSpec((None, 1, 2, r, jb, w), lambda pp, j: (s, pp, 0, 0, j, 0))
    return pl.pallas_call(
        functools.partial(_fft3_kernel, passes=passes),
        grid=(p, n2 // jb),
        in_specs=[pl.BlockSpec((1, 2, n1, jb, w), lambda pp, j: (pp, 0, 0, j, 0)),
                  pl.BlockSpec(f3.shape, lambda pp, j: (0, 0)),
                  seq(gate_sel), seq(prev_sel),
                  pl.BlockSpec((1, w), lambda pp, j: (0, 0))],
        out_specs=pl.BlockSpec((1, 2, r, jb, w), lambda pp, j: (pp, 0, 0, j, 0)),
        out_shape=jax.ShapeDtypeStruct((p, 2, out_rows, n2, w), F32),
        compiler_params=_cp(("parallel", "parallel")),
        name="hyena_fft_stage3",
    )(bm, f3, gate, prev, skip)


def _hyctx_kernel(v_ref, x1_ref, x2_ref, filt_ref, ff_ref, ffr_ref, fi_ref, skip_ref, yh_hbm_ref, o_ref,
                  *, passes, filter_passes):
    del yh_hbm_ref
    lc, w = v_ref.shape[2], v_ref.shape[3]
    n = 2 * lc
    y = v_ref[0].reshape(2 * lc, w)
    for o, gate_ref in enumerate((x1_ref, x2_ref)):
        sf = _split_dot(ffr_ref[...], filt_ref[o, 0], filter_passes)
        sb = _split_dot(ffr_ref[...], filt_ref[o, 1], filter_passes)
        hr, hi = sf[:n] + sb[:n], sf[n:] - sb[n:]
        x = _split_dot(ff_ref[...], y, passes)
        xr, xi = x[:n], x[n:]
        spec = jnp.concatenate([xr * hr - xi * hi, xr * hi + xi * hr], axis=0)
        conv = _split_dot(fi_ref[...], spec, passes)
        y = gate_ref[0].reshape(2 * lc, w) * (conv + skip_ref[o:o + 1, :] * y)
    o_ref[0] = y.reshape(2, lc, w)


def _hyena_ctx(hc, filt_c, ff, ffr, fi, skip, yh, passes, filter_passes):
    _, nb, lc, w = hc.shape
    p = nb // 2
    t = yh.shape[2]
    ff, fi, ffr = _dft_operand(ff, passes), _dft_operand(fi, passes), _dft_operand(ffr, filter_passes)
    part = pl.BlockSpec((1, 2, lc, w), lambda pp: (pp, 0, 0, 0))
    hc4 = hc.reshape(3, p, 2, lc, w)
    return pl.pallas_call(
        functools.partial(_hyctx_kernel, passes=passes, filter_passes=filter_passes),
        grid=(p,),
        in_specs=[part, part, part,
                  pl.BlockSpec(filt_c.shape, lambda pp: (0, 0, 0, 0)),
                  pl.BlockSpec(ff.shape, lambda pp: (0, 0)),
                  pl.BlockSpec(ffr.shape, lambda pp: (0, 0)),
                  pl.BlockSpec(fi.shape, lambda pp: (0, 0)),
                  pl.BlockSpec(skip.shape, lambda pp: (0, 0)),
                  pl.BlockSpec(memory_space=pl.ANY)],
        out_specs=pl.BlockSpec((1, 2, lc, w), lambda pp: (pp, 0, t // lc - 1, 0)),
        out_shape=jax.ShapeDtypeStruct(yh.shape, F32),
        input_output_aliases={8: 0},
        compiler_params=_cp(("parallel",)),
        name="hyena_ctx",
    )(hc4[0], hc4[1], hc4[2], filt_c, ff, ffr, fi, skip, yh)


def _hyena_filters(l, w1, b1, w2, b2, w3, freq):
    t = jnp.linspace(0.0, 1.0, l, dtype=F32)[:, None]
    bands = (HY_EMB - 1) // 2
    w = 2.0 * math.pi * jnp.arange(l, dtype=F32) / l
    f = jnp.linspace(1e-4, bands - 1, bands, dtype=F32)
    ang = w[:, None] * f[None, :]
    feats = jnp.concatenate([t, jnp.cos(ang), -jnp.sin(ang)], axis=-1)
    hdn = jnp.sin(freq[0] * (feats @ w1 + b1))
    hdn = jnp.sin(freq[1] * (hdn @ w2 + b2))
    n_out = w3.shape[-1]
    deltas = jnp.linspace(math.log(HY_TARGET) / HY_SLOW_PCT, math.log(HY_TARGET) / HY_FAST_PCT, n_out, dtype=F32)
    decay = jnp.exp(-t * jnp.abs(deltas))
    pos = jnp.arange(l)[:, None]
    parts = []
    for q in range(n_out // BRANCH_W):
        sl = slice(q * BRANCH_W, (q + 1) * BRANCH_W)
        f = (hdn @ w3[:, sl]).astype(F32) * decay[:, sl]
        parts.append(jnp.where(pos == 0, 0.0, f) if q % 2 else f)
    return jnp.stack(parts).reshape(n_out // (2 * BRANCH_W), 2, l, BRANCH_W)


def _hyena(z, conv_w, conv_b, filt_w, skip, nb, lat_len):
    m = z.shape[0]
    t = m // nb
    lc = t - lat_len
    w = BRANCH_W
    p = nb // 2
    n1, n2 = FFT_N1, 2 * lat_len // FFT_N1
    r = n1 // 2
    hl, hc = _hyconv(z, conv_w, conv_b, nb, lat_len)
    f1, f1r, f2, g2, f3 = _dft_tables(lat_len)
    ff, ffr, fi = _dft_tables_small(lc)
    jb, kb = min(16, n2), 16
    dp, fp = DFT_PASSES, DFT_FILTER_PASSES
    filt_l = _hyena_filters(lat_len, *filt_w)
    fa = _fft1(filt_l.reshape(1, 4, 1, r, n2, w), 0, f1r, jb, fp)
    hs_l = _fft2_filters(fa, f2, kb, fp)
    hl6 = hl.reshape(3, p, 2, r, n2, w)
    y, y_sel = hl6, 0
    for o in range(2):
        a = _fft1(y, y_sel, f1, jb, dp)
        bm = _fft2(a, hs_l, o, f2, g2, kb, dp)
        rows = r if o == 0 else t // n2
        y = _fft3(bm, f3, hl6, 1 + o, y, y_sel, skip[o:o + 1], rows, jb, dp)[None]
        y_sel = 0
    yh = y.reshape(p, 2, t, w)
    yh = _hyena_ctx(hc, _hyena_filters(lc, *filt_w), ff, ffr, fi, skip, yh, dp, fp)
    return yh.reshape(m, w)


def _merge_kernel(x_ref, mod_ref, ya_ref, yr_ref, yh_ref, yd_ref, wg_ref, bg_ref, wb_ref, wo_ref, g_ref, b_ref,
                  o_ref, *, alpha):
    x = x_ref[...]
    u = _modulate(x, mod_ref, 0, 1).astype(BF16)
    acc = None
    for i, y_ref in enumerate((ya_ref, yr_ref, yh_ref, yd_ref)):
        gate = jax.nn.sigmoid(_dot(u, wg_ref[i]) + bg_ref[i:i + 1, :])
        term = gate * _dot(y_ref[...].astype(BF16), wb_ref[i])
        acc = term if acc is None else acc + term
    mix = _dot(acc.astype(BF16), wo_ref[...])
    o_ref[...] = _gated_residual_ln(x, mix, mod_ref, 2, alpha, g_ref[...], b_ref[...])


def _merge(x, modx, ys, wg, bg, wb, wo, layer, g, b, alpha, tm):
    m, d = x.shape
    w = BRANCH_W
    row = lambda i: (i, 0)
    const2 = lambda i: (0, 0)
    one = pl.Buffered(1)
    stacked = lambda a: pl.BlockSpec((None,) + a.shape[1:], lambda i: (layer,) + (0,) * (a.ndim - 1),
                                     pipeline_mode=one)
    return pl.pallas_call(
        functools.partial(_merge_kernel, alpha=alpha),
        grid=(m // tm,),
        in_specs=[pl.BlockSpec((tm, d), row),
                  pl.BlockSpec((tm // SUB, 8, d), lambda i: (i, 0, 0)),
                  pl.BlockSpec((tm, w), row), pl.BlockSpec((tm, w), row),
                  pl.BlockSpec((tm, w), row), pl.BlockSpec((tm, w), row),
                  stacked(wg), pl.BlockSpec(bg.shape, const2), stacked(wb), stacked(wo),
                  pl.BlockSpec((1, d), const2), pl.BlockSpec((1, d), const2)],
        out_specs=pl.BlockSpec((tm, d), row),
        out_shape=jax.ShapeDtypeStruct((m, d), F32),
        compiler_params=_cp(("parallel",)),
        name="branch_merge",
    )(x, modx, *ys, wg, bg, wb, wo, g.reshape(1, d), b.reshape(1, d))


def _ffn_kernel(x_ref, mod_ref, w1_ref, w3_ref, w2_ref, g_ref, b_ref, o_ref, *, alpha, tf):
    x = x_ref[...]
    u = _modulate(x, mod_ref, 3, 4).astype(BF16)
    acc = None
    for j in range(w1_ref.shape[1] // tf):
        sl = slice(j * tf, (j + 1) * tf)
        h1 = _dot(u, w1_ref[:, sl])
        h3 = _dot(u, w3_ref[:, sl])
        part = _dot((h1 * jax.nn.sigmoid(h1) * h3).astype(BF16), w2_ref[sl, :])
        acc = part if acc is None else acc + part
    o_ref[...] = _gated_residual_ln(x, acc, mod_ref, 5, alpha, g_ref[...], b_ref[...])


def _ffn(x, modx, w1, w3, w2, layer, g, b, alpha, tm, tf):
    m, d = x.shape
    resident = lambda a: pl.BlockSpec((None,) + a.shape[1:], lambda i: (layer, 0, 0), pipeline_mode=pl.Buffered(1))
    return pl.pallas_call(
        functools.partial(_ffn_kernel, alpha=alpha, tf=tf),
        grid=(m // tm,),
        in_specs=[pl.BlockSpec((tm, d), lambda i: (i, 0)),
                  pl.BlockSpec((tm // SUB, 8, d), lambda i: (i, 0, 0)),
                  resident(w1), resident(w3), resident(w2),
                  pl.BlockSpec((1, d), lambda i: (0, 0)),
                  pl.BlockSpec((1, d), lambda i: (0, 0))],
        out_specs=pl.BlockSpec((tm, d), lambda i: (i, 0)),
        out_shape=jax.ShapeDtypeStruct((m, d), F32),
        compiler_params=_cp(("parallel",)),
        name="dense_swiglu",
    )(x, modx, w1, w3, w2, g.reshape(1, d), b.reshape(1, d))


def _route_kernel(x_ref, mod_ref, wr_ref, u_ref, lg_ref):
    u = _modulate(x_ref[...], mod_ref, 3, 4)
    lg_ref[...] = jnp.dot(u, wr_ref[...], preferred_element_type=F32, precision=lax.Precision.HIGHEST)
    half = u.shape[1] // 2
    bits = lax.bitcast_convert_type(u.astype(BF16).astype(F32), jnp.uint32)
    u_ref[...] = lax.bitcast_convert_type(bits[:, :half] | (bits[:, half:] >> 16), F32)


def _route(x, modx, wr_pad, tm):
    m, d = x.shape
    return pl.pallas_call(
        _route_kernel,
        grid=(m // tm,),
        in_specs=[pl.BlockSpec((tm, d), lambda i: (i, 0)),
                  pl.BlockSpec((tm // SUB, 8, d), lambda i: (i, 0, 0)),
                  pl.BlockSpec((d, LANE), lambda i: (0, 0))],
        out_specs=[pl.BlockSpec((tm, d // 2), lambda i: (i, 0)), pl.BlockSpec((tm, LANE), lambda i: (i, 0))],
        out_shape=[jax.ShapeDtypeStruct((m, d // 2), F32), jax.ShapeDtypeStruct((m, LANE), F32)],
        compiler_params=_cp(("parallel",)),
        name="moe_router",
    )(x, modx, wr_pad)


def _moe_kernel(be_ref, nu_ref, xb_ref, w1_ref, w3_ref, w2_ref, o_ref, acc_ref, u_ref):
    i, j = pl.program_id(0), pl.program_id(1)
    live = i < nu_ref[0]

    @pl.when(j == 0)
    def _():
        acc_ref[...] = jnp.zeros_like(acc_ref)
        words = lax.bitcast_convert_type(xb_ref[...], jnp.uint32)
        half = words.shape[1]
        hi = lax.bitcast_convert_type(words & jnp.uint32(0xFFFF0000), F32)
        lo = lax.bitcast_convert_type(words << 16, F32)
        u_ref[:, :half] = hi.astype(BF16)
        u_ref[:, half:] = lo.astype(BF16)

    @pl.when(live)
    def _():
        u = u_ref[...]
        h1 = _dot(u, w1_ref[0])
        h3 = _dot(u, w3_ref[0])
        acc_ref[...] += _dot((h1 * jax.nn.sigmoid(h1) * h3).astype(BF16), w2_ref[0])

    @pl.when(j == pl.num_programs(1) - 1)
    def _():
        o_ref[...] = acc_ref[...]


def _moe_experts(xb, blk_e, n_used, w1, w3, w2, layer, tm, tf):
    r = xb.shape[0]
    d = w1.shape[2]
    ff = w1.shape[3]
    nj = ff // tf

    def wmap(i, j, be, nu):
        return (layer, be[i], 0, jnp.where(i < nu[0], j, nj - 1))

    def w2map(i, j, be, nu):
        return (layer, be[i], jnp.where(i < nu[0], j, nj - 1), 0)

    grid_spec = pltpu.PrefetchScalarGridSpec(
        num_scalar_prefetch=2,
        grid=(r // tm, nj),
        in_specs=[pl.BlockSpec((tm, d // 2), lambda i, j, be, nu: (i, 0)),
                  pl.BlockSpec((None, 1, d, tf), wmap),
                  pl.BlockSpec((None, 1, d, tf), wmap),
                  pl.BlockSpec((None, 1, tf, d), w2map)],
        out_specs=pl.BlockSpec((tm, d), lambda i, j, be, nu: (i, 0)),
        scratch_shapes=[pltpu.VMEM((tm, d), F32), pltpu.VMEM((tm, d), BF16)],
    )
    return pl.pallas_call(
        _moe_kernel,
        grid_spec=grid_spec,
        out_shape=jax.ShapeDtypeStruct((r, d), F32),
        compiler_params=_cp(("arbitrary", "arbitrary")),
        name="moe_experts",
    )(blk_e, n_used, xb, w1, w3, w2)


def _ln_res_kernel(x_ref, f0_ref, f1_ref, wt_ref, mod_ref, g_ref, b_ref, o_ref, *, alpha):
    f = f0_ref[...] * wt_ref[:, 0:1] + f1_ref[...] * wt_ref[:, 1:2]
    o_ref[...] = _gated_residual_ln(x_ref[...], f, mod_ref, 5, alpha, g_ref[...], b_ref[...])


def _ln_res(x, f01, wts, modx, g, b, alpha, tm):
    m, d = x.shape
    nt = m // tm
    return pl.pallas_call(
        functools.partial(_ln_res_kernel, alpha=alpha),
        grid=(nt,),
        in_specs=[pl.BlockSpec((tm, d), lambda i: (i, 0)),
                  pl.BlockSpec((tm, d), lambda i: (i, 0)),
                  pl.BlockSpec((tm, d), lambda i: (i + nt, 0)),
                  pl.BlockSpec((tm, TOP_K), lambda i: (i, 0)),
                  pl.BlockSpec((tm // SUB, 8, d), lambda i: (i, 0, 0)),
                  pl.BlockSpec((1, d), lambda i: (0, 0)), pl.BlockSpec((1, d), lambda i: (0, 0))],
        out_specs=pl.BlockSpec((tm, d), lambda i: (i, 0)),
        out_shape=jax.ShapeDtypeStruct((m, d), F32),
        compiler_params=_cp(("parallel",)),
        name="moe_residual_ln",
    )(x, f01, f01, wts, modx, g.reshape(1, d), b.reshape(1, d))


def _moe(x, modx, router, w1, w3, w2, layer, g, b, alpha, tm_rows, tm, tf):
    m, d = x.shape
    wr_pad = jnp.zeros((d, LANE), F32).at[:, :N_EXPERTS].set(router)
    u, logits = _route(x, modx, wr_pad, tm_rows)
    top_v, top_i = lax.top_k(logits[:, :N_EXPERTS], TOP_K)
    wts = jax.nn.softmax(top_v, axis=-1)
    e_flat = top_i.reshape(-1).astype(jnp.int32)
    n_pair = m * TOP_K
    onehot = (e_flat[:, None] == jnp.arange(N_EXPERTS, dtype=jnp.int32)[None, :]).astype(jnp.int32)
    csum = jnp.cumsum(onehot, axis=0)
    counts = csum[-1]
    rank = jnp.take_along_axis(csum, e_flat[:, None], axis=1)[:, 0] - 1
    padded = (counts + tm - 1) // tm * tm
    pad_end = jnp.cumsum(padded)
    pad_start = pad_end - padded
    dest = pad_start[e_flat] + rank
    n_blk = -(-n_pair // tm) + N_EXPERTS
    r = n_blk * tm
    tok = jnp.arange(n_pair, dtype=jnp.int32) // TOP_K
    n_used = (pad_end[-1] // tm).astype(jnp.int32)
    blk = jnp.minimum(jnp.arange(n_blk, dtype=jnp.int32), n_used - 1) * tm
    blk_e = jnp.minimum(jnp.searchsorted(pad_end, blk, side='right'), N_EXPERTS - 1).astype(jnp.int32)
    _, tok_sorted = lax.sort_key_val(dest, tok)
    shift = pad_start - (jnp.cumsum(counts) - counts)
    src = jnp.arange(r, dtype=jnp.int32) - jnp.repeat(shift[blk_e], tm).astype(jnp.int32)
    row_tok = jnp.take(tok_sorted, jnp.clip(src, 0, n_pair - 1), mode="clip")
    xb = jnp.take(u, row_tok, axis=0, mode="clip")
    yb = _moe_experts(xb, blk_e, n_used.reshape(1), w1, w3, w2, layer, tm, tf)
    f01 = jnp.take(yb, dest.reshape(m, TOP_K).T.reshape(-1), axis=0, mode="clip")
    return _ln_res(x, f01, wts, modx, g, b, alpha, tm_rows)


def _rope_tables(nb, lat_len, ctx_len):
    rows = lat_len // GRID_W
    row = jnp.repeat(jnp.arange(rows, dtype=F32), GRID_W)
    col = jnp.tile(jnp.arange(GRID_W, dtype=F32), rows)
    n_freq = HEAD_DIM // 4
    inv = ROPE_THETA ** (-jnp.arange(n_freq, dtype=F32) / n_freq)
    ar, ac = row[:, None] * inv, col[:, None] * inv
    cos = jnp.concatenate([jnp.cos(ar), jnp.cos(ar), jnp.cos(ac), jnp.cos(ac)], axis=1)
    sin = jnp.concatenate([-jnp.sin(ar), jnp.sin(ar), -jnp.sin(ac), jnp.sin(ac)], axis=1)
    cos = jnp.concatenate([cos, jnp.ones((ctx_len, HEAD_DIM), F32)], axis=0)
    sin = jnp.concatenate([sin, jnp.zeros((ctx_len, HEAD_DIM), F32)], axis=0)
    return jnp.tile(cos, (nb, 1)), jnp.tile(sin, (nb, 1))


def _row_tile(m, pref):
    tm = pref
    while m % tm:
        tm //= 2
    return tm


def kernel(x, c, ctx, c_ctx, w_mod, b_mod, w_in, conv_a_w, conv_a_b, ret_decay, hy_conv_w, hy_conv_b, hy_w1, hy_b1, hy_w2, hy_b2, hy_w3, hy_freq, hy_skip, q_norm, k_norm, w_gate, b_gate, w_branch, w_o, ln_g, ln_b, ffn_w1, ffn_w3, ffn_w2, router, moe_w1, moe_w3, moe_w2):
    nb, lat_len, d = x.shape
    ctx_len = ctx.shape[1]
    depth = w_mod.shape[0]
    t = lat_len + ctx_len
    m = nb * t
    alpha = float((2 * depth) ** 0.25)
    assert lat_len % SUB == 0 and ctx_len % SUB == 0 and nb % 2 == 0

    act = jnp.concatenate([jax.nn.silu(c), jax.nn.silu(c_ctx)[None, :],
                           jnp.zeros((8 - nb - 1, d), F32)], axis=0)
    mods = _mod_all(act, w_mod, b_mod).reshape(depth, 8, N_MOD, d)
    mods = jnp.pad(mods, ((0, 0), (0, 0), (0, 8 - N_MOD), (0, 0)))
    tiles_per_batch = t // SUB
    tile_pos = np.arange(m // SUB) % tiles_per_batch
    tile_group = np.where(tile_pos < lat_len // SUB, np.arange(m // SUB) // tiles_per_batch, nb)
    cos_t, sin_t = _rope_tables(nb, lat_len, ctx_len)

    cast = lambda a: a.astype(BF16)
    w_in_b, w_gate_b, w_branch_b, w_o_b = cast(w_in), cast(w_gate), cast(w_branch), cast(w_o)
    ffn_b = (cast(ffn_w1), cast(ffn_w3), cast(ffn_w2))
    moe_w = (cast(moe_w1), cast(moe_w3), cast(moe_w2))

    rows = jnp.concatenate([x, ctx], axis=1).reshape(m, d)
    tm = _row_tile(m, 512)
    for l in range(depth):
        modx = mods[l][tile_group]
        za, zvg, zh, rq, rk, aq, ak, av = _proj(rows, modx, w_in_b, l, cos_t, sin_t, q_norm[l], k_norm[l], tm)
        y_d = _attention(aq, ak, av, nb, lat_len)
        log_gamma = -jnp.exp(ret_decay[l].astype(F32))
        y_r = _retention(rq, rk, zvg, _retention_tables(log_gamma), nb, lat_len)
        y_a = _shortconv(za, conv_a_w[l], conv_a_b[l], nb, lat_len)
        y_h = _hyena(zh, hy_conv_w[l], hy_conv_b[l],
                     (hy_w1[l], hy_b1[l], hy_w2[l], hy_b2[l], hy_w3[l], hy_freq[l]), hy_skip[l], nb, lat_len)
        rows = _merge(rows, modx, (y_a, y_r, y_h, y_d), w_gate_b, b_gate[l], w_branch_b, w_o_b, l,
                      ln_g[l, 0], ln_b[l, 0], alpha, tm)
        i = l // 2
        if l % 2 == 0:
            rows = _ffn(rows, modx, *ffn_b, i, ln_g[l, 1], ln_b[l, 1], alpha, tm, 256)
        else:
            rows = _moe(rows, modx, router[i], *moe_w, i, ln_g[l, 1], ln_b[l, 1], alpha, tm, 768, 1792)
    return rows.reshape(nb, t, d)[:, :lat_len]
```

```python
import functools
import math

import numpy as np
import jax
import jax.numpy as jnp
from jax import lax
from jax.experimental import pallas as pl
from jax.experimental.pallas import tpu as pltpu

F32 = jnp.float32
BF16 = jnp.bfloat16

GRID_W = 64
BRANCH_W = 512
RET_HEADS = 4
RET_D = 128
RET_CHUNK = 128
HY_EMB = 33
HY_FAST_PCT = 0.3
HY_SLOW_PCT = 1.5
HY_TARGET = 1e-2
ATT_HEADS = 4
ATT_KV_HEADS = 2
HEAD_DIM = 128
ROPE_THETA = 10000.0
N_EXPERTS = 8
TOP_K = 2
N_MOD = 6
A_W = 3 * BRANCH_W
R_W = 4 * RET_HEADS * RET_D
H_W = 3 * BRANCH_W
D_W = (ATT_HEADS + 2 * ATT_KV_HEADS) * HEAD_DIM
IN_W = A_W + R_W + H_W + D_W
LN_EPS = 1e-6

SUB = 256
LANE = 128
FFT_N1 = 128
RET_UNROLL = 34
ATT_TQ = 512
ATT_KC = 4352
DFT_PASSES = 1
DFT_FILTER_PASSES = 1
VMEM_LIMIT = 56 * 1024 * 1024


def _cp(sem, vmem=VMEM_LIMIT):
    return pltpu.CompilerParams(dimension_semantics=sem, vmem_limit_bytes=vmem)


def _dot(a, b):
    return jnp.dot(a, b, preferred_element_type=F32)


def _dot_nt(a, b):
    return lax.dot_general(a, b, (((1,), (1,)), ((), ())), preferred_element_type=F32)


def _dot_tn(a, b):
    return lax.dot_general(a, b, (((0,), (0,)), ((), ())), preferred_element_type=F32)


def _split_dot(f_cat, x, passes):
    xh = x.astype(BF16)
    if passes == 1:
        return _dot(f_cat, xh)
    m = f_cat.shape[0] // 2
    r = _dot(f_cat, xh)
    r = r[:m] + r[m:]
    if passes == 3:
        r = r + _dot(f_cat[:m], (x - xh.astype(F32)).astype(BF16))
    return r


def _dft_operand(f_cat, passes):
    return f_cat[..., :f_cat.shape[-2] // 2, :] if passes == 1 else f_cat


def _modulate(x, mod_ref, shift_row, scale_row):
    parts = []
    for s in range(x.shape[0] // SUB):
        sh = mod_ref[s, shift_row:shift_row + 1, :]
        sc = mod_ref[s, scale_row:scale_row + 1, :]
        parts.append(x[s * SUB:(s + 1) * SUB] * (1.0 + sc) + sh)
    return parts[0] if len(parts) == 1 else jnp.concatenate(parts, axis=0)


def _gated_residual_ln(x, f, mod_ref, gate_row, alpha, g, b):
    parts = []
    for s in range(x.shape[0] // SUB):
        gt = mod_ref[s, gate_row:gate_row + 1, :]
        parts.append(alpha * x[s * SUB:(s + 1) * SUB] + gt * f[s * SUB:(s + 1) * SUB])
    r = parts[0] if len(parts) == 1 else jnp.concatenate(parts, axis=0)
    mu = jnp.mean(r, axis=-1, keepdims=True)
    rc = r - mu
    var = jnp.mean(rc * rc, axis=-1, keepdims=True)
    return rc * lax.rsqrt(var + LN_EPS) * g + b


def _mod_kernel(a_ref, w_ref, b_ref, o_ref):
    o_ref[0] = _dot(a_ref[...].astype(BF16), w_ref[0].astype(BF16)) + b_ref[0]


def _mod_all(act, w_mod, b_mod):
    depth, d, n = w_mod.shape
    tn = 512
    return pl.pallas_call(
        _mod_kernel,
        grid=(depth, n // tn),
        in_specs=[pl.BlockSpec((8, d), lambda l, j: (0, 0)),
                  pl.BlockSpec((1, d, tn), lambda l, j: (l, 0, j)),
                  pl.BlockSpec((1, 1, tn), lambda l, j: (l, 0, j))],
        out_specs=pl.BlockSpec((1, 8, tn), lambda l, j: (l, 0, j)),
        out_shape=jax.ShapeDtypeStruct((depth, 8, n), F32),
        compiler_params=_cp(("parallel", "parallel")),
        name="mod_vectors",
    )(act, w_mod, b_mod.reshape(depth, 1, n))


PROJ_TN = 512
assert (A_W, R_W, H_W, D_W) == (3 * PROJ_TN, 4 * PROJ_TN, 3 * PROJ_TN, 2 * PROJ_TN)


def _proj_kernel(x_ref, mod_ref, w_ref, cos_ref, sin_ref, gq_ref, gk_ref,
                 za_ref, zvg_ref, zh_ref, rq_ref, rk_ref, aq_ref, ak_ref, av_ref):
    u = _modulate(x_ref[...], mod_ref, 0, 1).astype(BF16)
    tn = PROJ_TN
    cos, sin = cos_ref[...], sin_ref[...]
    lane = lax.broadcasted_iota(jnp.int32, cos.shape, 1)
    first_half = (lane % 64) < 32

    def tile(j):
        return _dot(u, w_ref[:, j * tn:(j + 1) * tn])

    def rope(x):
        partner = jnp.where(first_half, pltpu.roll(x, 96, 1), pltpu.roll(x, 32, 1))
        return x * cos + partner * sin

    def rms(x, g):
        return x * lax.rsqrt(jnp.mean(x * x, axis=-1, keepdims=True) + LN_EPS) * g

    head = lambda h: slice(h * LANE, (h + 1) * LANE)

    for j in range(3):
        za_ref[:, j * tn:(j + 1) * tn] = tile(j)
    r = tile(3)
    for h in range(RET_HEADS):
        rq_ref[:, head(h)] = rope(r[:, head(h)])
    r = tile(4)
    for h in range(RET_HEADS):
        rk_ref[:, head(h)] = rope(r[:, head(h)] * (RET_D ** -0.5))
    for j in range(2):
        zvg_ref[:, j * tn:(j + 1) * tn] = tile(5 + j)
    for j in range(3):
        zh_ref[:, j * tn:(j + 1) * tn] = tile(7 + j)
    q_scale = (HEAD_DIM ** -0.5) * math.log2(math.e)
    r = tile(10)
    for h in range(ATT_HEADS):
        aq_ref[:, head(h)] = (rope(rms(r[:, head(h)], gq_ref[...])) * q_scale).astype(BF16)
    r = tile(11)
    ones_col = jnp.where(lane == 0, 1.0, 0.0).astype(BF16)
    for h in range(ATT_KV_HEADS):
        ak_ref[:, head(h)] = rope(rms(r[:, head(h)], gk_ref[...])).astype(BF16)
        av_ref[:, head(2 * h)] = r[:, head(ATT_KV_HEADS + h)].astype(BF16)
        av_ref[:, head(2 * h + 1)] = ones_col


def _proj(x, modx, w, layer, cos_t, sin_t, gq, gk, tm):
    m, d = x.shape
    n = w.shape[2]
    tn = PROJ_TN
    qw, kw = ATT_HEADS * HEAD_DIM, ATT_KV_HEADS * HEAD_DIM
    row = lambda i: (i, 0)
    widths = (A_W, 2 * tn, H_W, tn, tn, qw, kw, 2 * kw)
    dtypes = (F32,) * 5 + (BF16,) * 3
    return pl.pallas_call(
        _proj_kernel,
        grid=(m // tm,),
        in_specs=[pl.BlockSpec((tm, d), row),
                  pl.BlockSpec((tm // SUB, 8, d), lambda i: (i, 0, 0)),
                  pl.BlockSpec((None, d, n), lambda i: (layer, 0, 0), pipeline_mode=pl.Buffered(1)),
                  pl.BlockSpec((tm, LANE), row), pl.BlockSpec((tm, LANE), row),
                  pl.BlockSpec((1, LANE), lambda i: (0, 0)), pl.BlockSpec((1, LANE), lambda i: (0, 0))],
        out_specs=[pl.BlockSpec((tm, wd), row) for wd in widths],
        out_shape=[jax.ShapeDtypeStruct((m, wd), dt) for wd, dt in zip(widths, dtypes)],
        compiler_params=_cp(("parallel",)),
        name="in_proj",
    )(x, modx, w, cos_t, sin_t, gq.reshape(1, LANE), gk.reshape(1, LANE))


def _attn_kernel(q_ref, k_ref, v_ref, *rest, key_lo, key_hi, kc):
    o_ref = rest[-1]
    tq = q_ref.shape[1]
    grp = ATT_HEADS // ATT_KV_HEADS
    q2 = jnp.concatenate([q_ref[0, :, g * HEAD_DIM:(g + 1) * HEAD_DIM] for g in range(grp)], axis=0)
    m_run, acc = None, None
    for lo in range(key_lo, key_hi, kc):
        hi = min(lo + kc, key_hi)
        s = _dot_nt(q2, k_ref[0, lo:hi, :])
        m_chunk = jnp.max(s, axis=-1, keepdims=True)
        m_new = m_chunk if m_run is None else jnp.maximum(m_run, m_chunk)
        pv = _dot(jnp.exp2(s - m_new).astype(BF16), v_ref[0, lo:hi, :])
        acc = pv if acc is None else jnp.exp2(m_run - m_new) * acc + pv
        m_run = m_new
    for g in range(grp):
        a = acc[g * tq:(g + 1) * tq]
        o_ref[0, :, g * HEAD_DIM:(g + 1) * HEAD_DIM] = a[:, :HEAD_DIM] / a[:, HEAD_DIM:HEAD_DIM + 1]


def _attention(aq, ak, av, nb, lat_len):
    m = aq.shape[0]
    t = m // nb
    lc = t - lat_len
    grp_w = (ATT_HEADS // ATT_KV_HEADS) * HEAD_DIM
    args = (aq.reshape(nb, t, -1), ak.reshape(nb, t, -1), av.reshape(nb, t, -1))

    def call(tq, n_tiles, first_tile, key_lo, key_hi, prev):
        qmap = lambda b, kv, i: (b, first_tile + i, kv)
        specs = [pl.BlockSpec((1, tq, grp_w), qmap),
                 pl.BlockSpec((1, t, HEAD_DIM), lambda b, kv, i: (b, 0, kv)),
                 pl.BlockSpec((1, t, 2 * HEAD_DIM), lambda b, kv, i: (b, 0, kv))]
        extra, alias = (), {}
        if prev is not None:
            specs.append(pl.BlockSpec(memory_space=pl.ANY))
            extra, alias = (prev,), {3: 0}
        return pl.pallas_call(
            functools.partial(_attn_kernel, key_lo=key_lo, key_hi=key_hi, kc=ATT_KC),
            grid=(nb, ATT_KV_HEADS, n_tiles),
            in_specs=specs,
            out_specs=pl.BlockSpec((1, tq, grp_w), qmap),
            out_shape=jax.ShapeDtypeStruct((nb, t, ATT_HEADS * HEAD_DIM), F32),
            input_output_aliases=alias,
            compiler_params=_cp(("parallel", "parallel", "arbitrary")),
            name="gqa_attention",
        )(*args, *extra)

    tq = math.gcd(ATT_TQ, lat_len)
    out = call(tq, lat_len // tq, 0, 0, t, None)
    out = call(lc, 1, t // lc - 1, lat_len, t, out)
    return out.reshape(m, -1)


def _ret_kernel(q_ref, k_ref, v_ref, g_ref, intra_ref, qd_ref, kd_ref, cd_ref, o_ref, st_ref,
                *, n_lat, n_ctx):
    c_len = RET_CHUNK
    n = n_lat + n_ctx

    def rows(c):
        return pl.ds(pl.multiple_of(c * c_len, c_len), c_len)

    def kv(d, c):
        return _dot_tn((k_ref[0, rows(c), :] * kd_ref[d, 0]).astype(BF16), v_ref[0, rows(c), :].astype(BF16))

    def scan_step(t, carry):
        sf, sb = carry
        cf = jnp.where(t < n_ctx, n_lat + t, t - n_ctx)
        cb = jnp.where(t < n_ctx, n - 1 - t, n_lat - 1 - (t - n_ctx))
        st_ref[cf, :RET_D, :] = sf.astype(BF16)
        st_ref[cb, RET_D:, :] = sb.astype(BF16)
        return (cd_ref[0, 0, 0:1, :] * sf + kv(0, cf), cd_ref[1, 0, 0:1, :] * sb + kv(1, cb))

    zero = jnp.zeros((RET_D, RET_D), F32)
    lax.fori_loop(0, n, scan_step, (zero, zero), unroll=RET_UNROLL)

    mask = intra_ref[0, 0] + intra_ref[1, 0]

    def out_step(c, carry):
        qc = q_ref[0, rows(c), :]
        vc = v_ref[0, rows(c), :].astype(BF16)
        s = _dot_nt(qc.astype(BF16), k_ref[0, rows(c), :].astype(BF16)) * mask
        qdec = jnp.concatenate([qc * qd_ref[0, 0], qc * qd_ref[1, 0]], axis=1).astype(BF16)
        o = _dot(s.astype(BF16), vc) + _dot(qdec, st_ref[c])
        mu = jnp.mean(o, axis=-1, keepdims=True)
        oc = o - mu
        var = jnp.mean(oc * oc, axis=-1, keepdims=True)
        g = g_ref[0, rows(c), :]
        o_ref[0, rows(c), :] = (g * jax.nn.sigmoid(g)) * (oc * lax.rsqrt(var + LN_EPS))
        return carry

    lax.fori_loop(0, n, out_step, 0, unroll=RET_UNROLL)


def _retention(rq, rk, z, tabs, nb, lat_len):
    m = rq.shape[0]
    t = m // nb
    intra, qd, kd, cd = tabs
    v_blk, g_blk = 0, RET_HEADS
    kern = functools.partial(_ret_kernel, n_lat=lat_len // RET_CHUNK, n_ctx=(t - lat_len) // RET_CHUNK)
    seq = lambda off: pl.BlockSpec((1, t, LANE), lambda b, h: (b, 0, off + h))
    tab = lambda r: pl.BlockSpec((2, 1, r, LANE), lambda b, h: (0, h, 0, 0))
    out = pl.pallas_call(
        kern,
        grid=(nb, RET_HEADS),
        in_specs=[seq(0), seq(0), seq(v_blk), seq(g_blk),
                  tab(RET_CHUNK), tab(RET_CHUNK), tab(RET_CHUNK), tab(8)],
        out_specs=seq(0),
        out_shape=jax.ShapeDtypeStruct((nb, t, RET_HEADS * RET_D), F32),
        scratch_shapes=[pltpu.VMEM((t // RET_CHUNK, 2 * RET_D, RET_D), BF16)],
        compiler_params=_cp(("parallel", "parallel")),
        name="retention",
    )(rq.reshape(nb, t, -1), rk.reshape(nb, t, -1), z.reshape(nb, t, -1), z.reshape(nb, t, -1), intra, qd, kd, cd)
    return out.reshape(m, -1)


def _retention_tables(log_gamma):
    c = RET_CHUNK
    i = jnp.arange(c, dtype=F32)
    lg = log_gamma.astype(F32)[:, :, None]
    rel = i[:, None] - i[None, :]
    intra_f = jnp.where(rel >= 0, jnp.exp(lg[0][..., None] * jnp.maximum(rel, 0.0)), 0.0)
    intra_b = jnp.where(rel <= 0, jnp.exp(lg[1][..., None] * jnp.maximum(-rel, 0.0)), 0.0)
    qd = jnp.stack([jnp.exp(lg[0] * (i + 1.0)), jnp.exp(lg[1] * (c - i))])
    kd = jnp.stack([jnp.exp(lg[0] * (c - 1.0 - i)), jnp.exp(lg[1] * i)])
    cd = jnp.exp(lg * c)
    bc = lambda a: jnp.broadcast_to(a[..., None], a.shape + (LANE,))
    return (jnp.stack([intra_f, intra_b]), bc(qd), bc(kd),
            jnp.broadcast_to(cd[..., None], (2, RET_HEADS, 8, LANE)))


def _shift_rows(x, lat_len):
    t = x.shape[0]
    row = lax.broadcasted_iota(jnp.int32, x.shape, 0)
    xm = jnp.where((row == 0) | (row == lat_len), 0.0, pltpu.roll(x, 1, 0))
    xp = jnp.where((row == lat_len - 1) | (row == t - 1), 0.0, pltpu.roll(x, t - 1, 0))
    return xm, xp


def _shortconv_kernel(bg_ref, cg_ref, xv_ref, w_ref, b_ref, o_ref, *, lat_len):
    p = cg_ref[0] * xv_ref[0]
    pm, pp = _shift_rows(p, lat_len)
    o_ref[0] = bg_ref[0] * (pm * w_ref[0:1, :] + p * w_ref[1:2, :] + pp * w_ref[2:3, :] + b_ref[...])


def _shortconv(z, w, b, nb, lat_len):
    m = z.shape[0]
    t = m // nb
    nct = BRANCH_W // LANE
    kern = functools.partial(_shortconv_kernel, lat_len=lat_len)
    seq = lambda off: pl.BlockSpec((1, t, LANE), lambda bb, j: (bb, 0, off + j))
    out = pl.pallas_call(
        kern,
        grid=(nb, nct),
        in_specs=[seq(0), seq(nct), seq(2 * nct),
                  pl.BlockSpec((3, LANE), lambda bb, j: (0, j)),
                  pl.BlockSpec((1, LANE), lambda bb, j: (0, j))],
        out_specs=seq(0),
        out_shape=jax.ShapeDtypeStruct((nb, t, BRANCH_W), F32),
        compiler_params=_cp(("parallel", "parallel")),
        name="short_conv",
    )(z.reshape(nb, t, -1), z.reshape(nb, t, -1), z.reshape(nb, t, -1), w, b.reshape(1, -1))
    return out.reshape(m, -1)


def _hyconv_kernel(z_ref, w_ref, b_ref, ol_ref, oc_ref, *, lat_len):
    x = z_ref[0]
    xm, xp = _shift_rows(x, lat_len)
    y = xm * w_ref[0:1, :] + x * w_ref[1:2, :] + xp * w_ref[2:3, :] + b_ref[...]
    ol_ref[0, 0] = y[:lat_len]
    oc_ref[0, 0] = y[lat_len:]


def _hyconv(z, w, b, nb, lat_len):
    m = z.shape[0]
    t = m // nb
    nct = BRANCH_W // LANE
    off = 0
    kern = functools.partial(_hyconv_kernel, lat_len=lat_len)
    return pl.pallas_call(
        kern,
        grid=(nb, 3 * nct),
        in_specs=[pl.BlockSpec((1, t, LANE), lambda bb, j: (bb, 0, off + j)),
                  pl.BlockSpec((3, LANE), lambda bb, j: (0, j)),
                  pl.BlockSpec((1, LANE), lambda bb, j: (0, j))],
        out_specs=[pl.BlockSpec((1, 1, lat_len, LANE), lambda bb, j: (j // nct, bb, 0, j % nct)),
                   pl.BlockSpec((1, 1, t - lat_len, LANE), lambda bb, j: (j // nct, bb, 0, j % nct))],
        out_shape=[jax.ShapeDtypeStruct((3, nb, lat_len, BRANCH_W), F32),
                   jax.ShapeDtypeStruct((3, nb, t - lat_len, BRANCH_W), F32)],
        compiler_params=_cp(("parallel", "parallel")),
        name="hyena_dwconv",
    )(z.reshape(nb, t, -1), w, b.reshape(1, -1))


def _stack_c(m):
    return np.block([[m.real, -m.imag], [m.imag, m.real]])


def _split_np(m):
    m32 = np.asarray(m, np.float32)
    hi = m32.astype(BF16)
    lo = (m32 - hi.astype(np.float32)).astype(BF16)
    return np.concatenate([hi, lo], axis=-2)


@functools.lru_cache(maxsize=None)
def _dft_tables(lat_len):
    n = 2 * lat_len
    n1, n2 = FFT_N1, 2 * lat_len // FFT_N1
    a1 = np.arange(n1)
    a2 = np.arange(n2)
    w1 = np.exp(-2j * np.pi * np.outer(a1, a1[:n1 // 2]) / n1)
    f1 = _stack_c(w1)
    w2 = np.exp(-2j * np.pi * np.outer(a2, a2) / n2)
    tw = np.exp(-2j * np.pi * np.outer(a1, a2) / n)
    cf = w2[None, :, :] * tw[:, None, :]
    f2 = np.stack([_stack_c(cf[k]) for k in range(n1)])
    g2 = np.stack([_stack_c(cf[k].conj().T) for k in range(n1)])
    w1i = np.exp(2j * np.pi * np.outer(a1[:n1 // 2], a1) / n1) / n
    f3 = _stack_c(w1i)
    f1_real = f1[:, :n1 // 2]
    return _split_np(f1), _split_np(f1_real), _split_np(f2), _split_np(g2), _split_np(f3)


@functools.lru_cache(maxsize=None)
def _dft_tables_small(ctx_len):
    n = 2 * ctx_len
    a = np.arange(n)
    ff = _stack_c(np.exp(-2j * np.pi * np.outer(a, a[:ctx_len]) / n))
    fi = _stack_c(np.exp(2j * np.pi * np.outer(a[:ctx_len], a) / n) / n)
    return _split_np(ff), _split_np(ff[:, :ctx_len]), _split_np(fi)


def _fft1_kernel(x_ref, f_ref, o_ref, *, passes):
    f = f_ref[...]
    for j in range(x_ref.shape[3]):
        parts = [x_ref[0, b, :, j, :] for b in range(x_ref.shape[1])]
        r = _split_dot(f, parts[0] if len(parts) == 1 else jnp.concatenate(parts, axis=0), passes)
        h = r.shape[0] // 2
        o_ref[0, 0, :, j, :] = r[:h]
        o_ref[0, 1, :, j, :] = r[h:]


def _fft1(x, sel, f1, jb, passes):
    _, p, mem, r, n2, w = x.shape
    f1 = _dft_operand(f1, passes)
    n1 = f1.shape[0] // (2 if passes == 1 else 4)
    return pl.pallas_call(
        functools.partial(_fft1_kernel, passes=passes),
        grid=(p, n2 // jb),
        in_specs=[pl.BlockSpec((None, 1, mem, r, jb, w), lambda pp, j: (sel, pp, 0, 0, j, 0)),
                  pl.BlockSpec(f1.shape, lambda pp, j: (0, 0))],
        out_specs=pl.BlockSpec((1, 2, n1, jb, w), lambda pp, j: (pp, 0, 0, j, 0)),
        out_shape=jax.ShapeDtypeStruct((p, 2, n1, n2, w), F32),
        compiler_params=_cp(("parallel", "parallel")),
        name="hyena_fft_stage1",
    )(x, f1)


def _fft2f_kernel(a_ref, f_ref, o_ref, *, kb, n2, passes):
    w = a_ref.shape[-1]
    for t in range(kb):
        xf = _split_dot(f_ref[t], a_ref[0, :, t].reshape(2 * n2, w), passes)
        xb = _split_dot(f_ref[t], a_ref[1, :, t].reshape(2 * n2, w), passes)
        o_ref[0, 0, t] = xf[:n2] + xb[:n2]
        o_ref[0, 1, t] = xf[n2:] - xb[n2:]


def _fft2_filters(a, f2, kb, passes):
    q, _, n1, n2, w = a.shape
    f2 = _dft_operand(f2, passes)
    return pl.pallas_call(
        functools.partial(_fft2f_kernel, kb=kb, n2=n2, passes=passes),
        grid=(q // 2, n1 // kb),
        in_specs=[pl.BlockSpec((2, 2, kb, n2, w), lambda o, k: (o, 0, k, 0, 0)),
                  pl.BlockSpec((kb,) + f2.shape[1:], lambda o, k: (k, 0, 0))],
        out_specs=pl.BlockSpec((1, 2, kb, n2, w), lambda o, k: (o, 0, k, 0, 0)),
        out_shape=jax.ShapeDtypeStruct((q // 2, 2, n1, n2, w), F32),
        compiler_params=_cp(("parallel", "parallel")),
        name="hyena_filter_spectrum",
    )(a, f2)


def _fft2_kernel(a_ref, h_ref, f_ref, g_ref, o_ref, *, kb, n2, passes):
    w = a_ref.shape[-1]
    for t in range(kb):
        a = a_ref[0, :, t].reshape(2 * n2, w)
        x = _split_dot(f_ref[t], a, passes)
        xr, xi = x[:n2], x[n2:]
        hr, hi = h_ref[0, t], h_ref[1, t]
        y = jnp.concatenate([xr * hr - xi * hi, xr * hi + xi * hr], axis=0)
        o_ref[0, :, t] = _split_dot(g_ref[t], y, passes).reshape(2, n2, w)


def _fft2(a, hspec, order, f2, g2, kb, passes):
    p, _, n1, n2, w = a.shape
    f2, g2 = _dft_operand(f2, passes), _dft_operand(g2, passes)
    kern = functools.partial(_fft2_kernel, kb=kb, n2=n2, passes=passes)
    return pl.pallas_call(
        kern,
        grid=(n1 // kb, p),
        in_specs=[pl.BlockSpec((1, 2, kb, n2, w), lambda k, pp: (pp, 0, k, 0, 0)),
                  pl.BlockSpec((None, 2, kb, n2, w), lambda k, pp: (order, 0, k, 0, 0)),
                  pl.BlockSpec((kb,) + f2.shape[1:], lambda k, pp: (k, 0, 0)),
                  pl.BlockSpec((kb,) + g2.shape[1:], lambda k, pp: (k, 0, 0))],
        out_specs=pl.BlockSpec((1, 2, kb, n2, w), lambda k, pp: (pp, 0, k, 0, 0)),
        out_shape=jax.ShapeDtypeStruct(a.shape, F32),
        compiler_params=_cp(("parallel", "arbitrary")),
        name="hyena_fft_stage2",
    )(a, hspec, f2, g2)


def _fft3_kernel(b_ref, f_ref, gate_ref, prev_ref, skip_ref, o_ref, *, passes):
    f = f_ref[...]
    r = gate_ref.shape[2]
    for j in range(b_ref.shape[3]):
        bj = jnp.concatenate([b_ref[0, 0, :, j, :], b_ref[0, 1, :, j, :]], axis=0)
        conv = _split_dot(f, bj, passes)
        for mem in range(2):
            cm = conv[mem * r:(mem + 1) * r]
            o_ref[0, mem, :, j, :] = gate_ref[0, mem, :, j, :] * (cm + skip_ref[...] * prev_ref[0, mem, :, j, :])


def _fft3(bm, f3, gate, gate_sel, prev, prev_sel, skip, out_rows, jb, passes):
    p, _, n1, n2, w = bm.shape
    r = n1 // 2
    f3 = _dft_operand(f3, passes)
    seq = lambda s: pl.BlockSpec((None, 1, 2, r, jb, w), lambda pp, j: (s, pp, 0, 0, j, 0))
    return pl.pallas_call(
        functools.partial(_fft3_kernel, passes=passes),
        grid=(p, n2 // jb),
        in_specs=[pl.BlockSpec((1, 2, n1, jb, w), lambda pp, j: (pp, 0, 0, j, 0)),
                  pl.BlockSpec(f3.shape, lambda pp, j: (0, 0)),
                  seq(gate_sel), seq(prev_sel),
                  pl.BlockSpec((1, w), lambda pp, j: (0, 0))],
        out_specs=pl.BlockSpec((1, 2, r, jb, w), lambda pp, j: (pp, 0, 0, j, 0)),
        out_shape=jax.ShapeDtypeStruct((p, 2, out_rows, n2, w), F32),
        compiler_params=_cp(("parallel", "parallel")),
        name="hyena_fft_stage3",
    )(bm, f3, gate, prev, skip)


def _hyctx_kernel(v_ref, x1_ref, x2_ref, filt_ref, ff_ref, ffr_ref, fi_ref, skip_ref, yh_hbm_ref, o_ref,
                  *, passes, filter_passes):
    del yh_hbm_ref
    lc, w = v_ref.shape[2], v_ref.shape[3]
    n = 2 * lc
    y = v_ref[0].reshape(2 * lc, w)
    for o, gate_ref in enumerate((x1_ref, x2_ref)):
        sf = _split_dot(ffr_ref[...], filt_ref[o, 0], filter_passes)
        sb = _split_dot(ffr_ref[...], filt_ref[o, 1], filter_passes)
        hr, hi = sf[:n] + sb[:n], sf[n:] - sb[n:]
        x = _split_dot(ff_ref[...], y, passes)
        xr, xi = x[:n], x[n:]
        spec = jnp.concatenate([xr * hr - xi * hi, xr * hi + xi * hr], axis=0)
        conv = _split_dot(fi_ref[...], spec, passes)
        y = gate_ref[0].reshape(2 * lc, w) * (conv + skip_ref[o:o + 1, :] * y)
    o_ref[0] = y.reshape(2, lc, w)


def _hyena_ctx(hc, filt_c, ff, ffr, fi, skip, yh, passes, filter_passes):
    _, nb, lc, w = hc.shape
    p = nb // 2
    t = yh.shape[2]
    ff, fi, ffr = _dft_operand(ff, passes), _dft_operand(fi, passes), _dft_operand(ffr, filter_passes)
    part = pl.BlockSpec((1, 2, lc, w), lambda pp: (pp, 0, 0, 0))
    hc4 = hc.reshape(3, p, 2, lc, w)
    return pl.pallas_call(
        functools.partial(_hyctx_kernel, passes=passes, filter_passes=filter_passes),
        grid=(p,),
        in_specs=[part, part, part,
                  pl.BlockSpec(filt_c.shape, lambda pp: (0, 0, 0, 0)),
                  pl.BlockSpec(ff.shape, lambda pp: (0, 0)),
                  pl.BlockSpec(ffr.shape, lambda pp: (0, 0)),
                  pl.BlockSpec(fi.shape, lambda pp: (0, 0)),
                  pl.BlockSpec(skip.shape, lambda pp: (0, 0)),
                  pl.BlockSpec(memory_space=pl.ANY)],
        out_specs=pl.BlockSpec((1, 2, lc, w), lambda pp: (pp, 0, t // lc - 1, 0)),
        out_shape=jax.ShapeDtypeStruct(yh.shape, F32),
        input_output_aliases={8: 0},
        compiler_params=_cp(("parallel",)),
        name="hyena_ctx",
    )(hc4[0], hc4[1], hc4[2], filt_c, ff, ffr, fi, skip, yh)


def _hyena_filters(l, w1, b1, w2, b2, w3, freq):
    t = jnp.linspace(0.0, 1.0, l, dtype=F32)[:, None]
    bands = (HY_EMB - 1) // 2
    w = 2.0 * math.pi * jnp.arange(l, dtype=F32) / l
    f = jnp.linspace(1e-4, bands - 1, bands, dtype=F32)
    ang = w[:, None] * f[None, :]
    feats = jnp.concatenate([t, jnp.cos(ang), -jnp.sin(ang)], axis=-1)
    hdn = jnp.sin(freq[0] * (feats @ w1 + b1))
    hdn = jnp.sin(freq[1] * (hdn @ w2 + b2))
    n_out = w3.shape[-1]
    deltas = jnp.linspace(math.log(HY_TARGET) / HY_SLOW_PCT, math.log(HY_TARGET) / HY_FAST_PCT, n_out, dtype=F32)
    decay = jnp.exp(-t * jnp.abs(deltas))
    pos = jnp.arange(l)[:, None]
    parts = []
    for q in range(n_out // BRANCH_W):
        sl = slice(q * BRANCH_W, (q + 1) * BRANCH_W)
        f = (hdn @ w3[:, sl]).astype(F32) * decay[:, sl]
        parts.append(jnp.where(pos == 0, 0.0, f) if q % 2 else f)
    return jnp.stack(parts).reshape(n_out // (2 * BRANCH_W), 2, l, BRANCH_W)


def _hyena(z, conv_w, conv_b, filt_w, skip, nb, lat_len):
    m = z.shape[0]
    t = m // nb
    lc = t - lat_len
    w = BRANCH_W
    p = nb // 2
    n1, n2 = FFT_N1, 2 * lat_len // FFT_N1
    r = n1 // 2
    hl, hc = _hyconv(z, conv_w, conv_b, nb, lat_len)
    f1, f1r, f2, g2, f3 = _dft_tables(lat_len)
    ff, ffr, fi = _dft_tables_small(lc)
    jb, kb = min(16, n2), 16
    dp, fp = DFT_PASSES, DFT_FILTER_PASSES
    filt_l = _hyena_filters(lat_len, *filt_w)
    fa = _fft1(filt_l.reshape(1, 4, 1, r, n2, w), 0, f1r, jb, fp)
    hs_l = _fft2_filters(fa, f2, kb, fp)
    hl6 = hl.reshape(3, p, 2, r, n2, w)
    y, y_sel = hl6, 0
    for o in range(2):
        a = _fft1(y, y_sel, f1, jb, dp)
        bm = _fft2(a, hs_l, o, f2, g2, kb, dp)
        rows = r if o == 0 else t // n2
        y = _fft3(bm, f3, hl6, 1 + o, y, y_sel, skip[o:o + 1], rows, jb, dp)[None]
        y_sel = 0
    yh = y.reshape(p, 2, t, w)
    yh = _hyena_ctx(hc, _hyena_filters(lc, *filt_w), ff, ffr, fi, skip, yh, dp, fp)
    return yh.reshape(m, w)


def _merge_kernel(x_ref, mod_ref, ya_ref, yr_ref, yh_ref, yd_ref, wg_ref, bg_ref, wb_ref, wo_ref, g_ref, b_ref,
                  o_ref, *, alpha):
    x = x_ref[...]
    u = _modulate(x, mod_ref, 0, 1).astype(BF16)
    acc = None
    for i, y_ref in enumerate((ya_ref, yr_ref, yh_ref, yd_ref)):
        gate = jax.nn.sigmoid(_dot(u, wg_ref[i]) + bg_ref[i:i + 1, :])
        term = gate * _dot(y_ref[...].astype(BF16), wb_ref[i])
        acc = term if acc is None else acc + term
    mix = _dot(acc.astype(BF16), wo_ref[...])
    o_ref[...] = _gated_residual_ln(x, mix, mod_ref, 2, alpha, g_ref[...], b_ref[...])


def _merge(x, modx, ys, wg, bg, wb, wo, layer, g, b, alpha, tm):
    m, d = x.shape
    w = BRANCH_W
    row = lambda i: (i, 0)
    const2 = lambda i: (0, 0)
    one = pl.Buffered(1)
    stacked = lambda a: pl.BlockSpec((None,) + a.shape[1:], lambda i: (layer,) + (0,) * (a.ndim - 1),
                                     pipeline_mode=one)
    return pl.pallas_call(
        functools.partial(_merge_kernel, alpha=alpha),
        grid=(m // tm,),
        in_specs=[pl.BlockSpec((tm, d), row),
                  pl.BlockSpec((tm // SUB, 8, d), lambda i: (i, 0, 0)),
                  pl.BlockSpec((tm, w), row), pl.BlockSpec((tm, w), row),
                  pl.BlockSpec((tm, w), row), pl.BlockSpec((tm, w), row),
                  stacked(wg), pl.BlockSpec(bg.shape, const2), stacked(wb), stacked(wo),
                  pl.BlockSpec((1, d), const2), pl.BlockSpec((1, d), const2)],
        out_specs=pl.BlockSpec((tm, d), row),
        out_shape=jax.ShapeDtypeStruct((m, d), F32),
        compiler_params=_cp(("parallel",)),
        name="branch_merge",
    )(x, modx, *ys, wg, bg, wb, wo, g.reshape(1, d), b.reshape(1, d))


def _ffn_kernel(x_ref, mod_ref, w1_ref, w3_ref, w2_ref, g_ref, b_ref, o_ref, *, alpha, tf):
    x = x_ref[...]
    u = _modulate(x, mod_ref, 3, 4).astype(BF16)
    acc = None
    for j in range(w1_ref.shape[1] // tf):
        sl = slice(j * tf, (j + 1) * tf)
        h1 = _dot(u, w1_ref[:, sl])
        h3 = _dot(u, w3_ref[:, sl])
        part = _dot((h1 * jax.nn.sigmoid(h1) * h3).astype(BF16), w2_ref[sl, :])
        acc = part if acc is None else acc + part
    o_ref[...] = _gated_residual_ln(x, acc, mod_ref, 5, alpha, g_ref[...], b_ref[...])


def _ffn(x, modx, w1, w3, w2, layer, g, b, alpha, tm, tf):
    m, d = x.shape
    resident = lambda a: pl.BlockSpec((None,) + a.shape[1:], lambda i: (layer, 0, 0), pipeline_mode=pl.Buffered(1))
    return pl.pallas_call(
        functools.partial(_ffn_kernel, alpha=alpha, tf=tf),
        grid=(m // tm,),
        in_specs=[pl.BlockSpec((tm, d), lambda i: (i, 0)),
                  pl.BlockSpec((tm // SUB, 8, d), lambda i: (i, 0, 0)),
                  resident(w1), resident(w3), resident(w2),
                  pl.BlockSpec((1, d), lambda i: (0, 0)),
                  pl.BlockSpec((1, d), lambda i: (0, 0))],
        out_specs=pl.BlockSpec((tm, d), lambda i: (i, 0)),
        out_shape=jax.ShapeDtypeStruct((m, d), F32),
        compiler_params=_cp(("parallel",)),
        name="dense_swiglu",
    )(x, modx, w1, w3, w2, g.reshape(1, d), b.reshape(1, d))


def _route_kernel(x_ref, mod_ref, wr_ref, u_ref, lg_ref):
    u = _modulate(x_ref[...], mod_ref, 3, 4)
    lg_ref[...] = jnp.dot(u, wr_ref[...], preferred_element_type=F32, precision=lax.Precision.HIGHEST)
    half = u.shape[1] // 2
    bits = lax.bitcast_convert_type(u.astype(BF16).astype(F32), jnp.uint32)
    u_ref[...] = lax.bitcast_convert_type(bits[:, :half] | (bits[:, half:] >> 16), F32)


def _route(x, modx, wr_pad, tm):
    m, d = x.shape
    return pl.pallas_call(
        _route_kernel,
        grid=(m // tm,),
        in_specs=[pl.BlockSpec((tm, d), lambda i: (i, 0)),
                  pl.BlockSpec((tm // SUB, 8, d), lambda i: (i, 0, 0)),
                  pl.BlockSpec((d, LANE), lambda i: (0, 0))],
        out_specs=[pl.BlockSpec((tm, d // 2), lambda i: (i, 0)), pl.BlockSpec((tm, LANE), lambda i: (i, 0))],
        out_shape=[jax.ShapeDtypeStruct((m, d // 2), F32), jax.ShapeDtypeStruct((m, LANE), F32)],
        compiler_params=_cp(("parallel",)),
        name="moe_router",
    )(x, modx, wr_pad)


def _moe_kernel(be_ref, nu_ref, xb_ref, w1_ref, w3_ref, w2_ref, o_ref, acc_ref, u_ref):
    i, j = pl.program_id(0), pl.program_id(1)
    live = i < nu_ref[0]

    @pl.when(j == 0)
    def _():
        acc_ref[...] = jnp.zeros_like(acc_ref)
        words = lax.bitcast_convert_type(xb_ref[...], jnp.uint32)
        half = words.shape[1]
        hi = lax.bitcast_convert_type(words & jnp.uint32(0xFFFF0000), F32)
        lo = lax.bitcast_convert_type(words << 16, F32)
        u_ref[:, :half] = hi.astype(BF16)
        u_ref[:, half:] = lo.astype(BF16)

    @pl.when(live)
    def _():
        u = u_ref[...]
        h1 = _dot(u, w1_ref[0])
        h3 = _dot(u, w3_ref[0])
        acc_ref[...] += _dot((h1 * jax.nn.sigmoid(h1) * h3).astype(BF16), w2_ref[0])

    @pl.when(j == pl.num_programs(1) - 1)
    def _():
        o_ref[...] = acc_ref[...]


def _moe_experts(xb, blk_e, n_used, w1, w3, w2, layer, tm, tf):
    r = xb.shape[0]
    d = w1.shape[2]
    ff = w1.shape[3]
    nj = ff // tf

    def wmap(i, j, be, nu):
        return (layer, be[i], 0, jnp.where(i < nu[0], j, nj - 1))

    def w2map(i, j, be, nu):
        return (layer, be[i], jnp.where(i < nu[0], j, nj - 1), 0)

    grid_spec = pltpu.PrefetchScalarGridSpec(
        num_scalar_prefetch=2,
        grid=(r // tm, nj),
        in_specs=[pl.BlockSpec((tm, d // 2), lambda i, j, be, nu: (i, 0)),
                  pl.BlockSpec((None, 1, d, tf), wmap),
                  pl.BlockSpec((None, 1, d, tf), wmap),
                  pl.BlockSpec((None, 1, tf, d), w2map)],
        out_specs=pl.BlockSpec((tm, d), lambda i, j, be, nu: (i, 0)),
        scratch_shapes=[pltpu.VMEM((tm, d), F32), pltpu.VMEM((tm, d), BF16)],
    )
    return pl.pallas_call(
        _moe_kernel,
        grid_spec=grid_spec,
        out_shape=jax.ShapeDtypeStruct((r, d), F32),
        compiler_params=_cp(("arbitrary", "arbitrary")),
        name="moe_experts",
    )(blk_e, n_used, xb, w1, w3, w2)


def _ln_res_kernel(x_ref, f0_ref, f1_ref, wt_ref, mod_ref, g_ref, b_ref, o_ref, *, alpha):
    f = f0_ref[...] * wt_ref[:, 0:1] + f1_ref[...] * wt_ref[:, 1:2]
    o_ref[...] = _gated_residual_ln(x_ref[...], f, mod_ref, 5, alpha, g_ref[...], b_ref[...])


def _ln_res(x, f01, wts, modx, g, b, alpha, tm):
    m, d = x.shape
    nt = m // tm
    return pl.pallas_call(
        functools.partial(_ln_res_kernel, alpha=alpha),
        grid=(nt,),
        in_specs=[pl.BlockSpec((tm, d), lambda i: (i, 0)),
                  pl.BlockSpec((tm, d), lambda i: (i, 0)),
                  pl.BlockSpec((tm, d), lambda i: (i + nt, 0)),
                  pl.BlockSpec((tm, TOP_K), lambda i: (i, 0)),
                  pl.BlockSpec((tm // SUB, 8, d), lambda i: (i, 0, 0)),
                  pl.BlockSpec((1, d), lambda i: (0, 0)), pl.BlockSpec((1, d), lambda i: (0, 0))],
        out_specs=pl.BlockSpec((tm, d), lambda i: (i, 0)),
        out_shape=jax.ShapeDtypeStruct((m, d), F32),
        compiler_params=_cp(("parallel",)),
        name="moe_residual_ln",
    )(x, f01, f01, wts, modx, g.reshape(1, d), b.reshape(1, d))


def _moe(x, modx, router, w1, w3, w2, layer, g, b, alpha, tm_rows, tm, tf):
    m, d = x.shape
    wr_pad = jnp.zeros((d, LANE), F32).at[:, :N_EXPERTS].set(router)
    u, logits = _route(x, modx, wr_pad, tm_rows)
    top_v, top_i = lax.top_k(logits[:, :N_EXPERTS], TOP_K)
    wts = jax.nn.softmax(top_v, axis=-1)
    e_flat = top_i.reshape(-1).astype(jnp.int32)
    n_pair = m * TOP_K
    onehot = (e_flat[:, None] == jnp.arange(N_EXPERTS, dtype=jnp.int32)[None, :]).astype(jnp.int32)
    csum = jnp.cumsum(onehot, axis=0)
    counts = csum[-1]
    rank = jnp.take_along_axis(csum, e_flat[:, None], axis=1)[:, 0] - 1
    padded = (counts + tm - 1) // tm * tm
    pad_end = jnp.cumsum(padded)
    pad_start = pad_end - padded
    dest = pad_start[e_flat] + rank
    n_blk = -(-n_pair // tm) + N_EXPERTS
    r = n_blk * tm
    tok = jnp.arange(n_pair, dtype=jnp.int32) // TOP_K
    n_used = (pad_end[-1] // tm).astype(jnp.int32)
    blk = jnp.minimum(jnp.arange(n_blk, dtype=jnp.int32), n_used - 1) * tm
    blk_e = jnp.minimum(jnp.searchsorted(pad_end, blk, side='right'), N_EXPERTS - 1).astype(jnp.int32)
    _, tok_sorted = lax.sort_key_val(dest, tok)
    shift = pad_start - (jnp.cumsum(counts) - counts)
    src = jnp.arange(r, dtype=jnp.int32) - jnp.repeat(shift[blk_e], tm).astype(jnp.int32)
    row_tok = jnp.take(tok_sorted, jnp.clip(src, 0, n_pair - 1), mode="clip")
    xb = jnp.take(u, row_tok, axis=0, mode="clip")
    yb = _moe_experts(xb, blk_e, n_used.reshape(1), w1, w3, w2, layer, tm, tf)
    f01 = jnp.take(yb, dest.reshape(m, TOP_K).T.reshape(-1), axis=0, mode="clip")
    return _ln_res(x, f01, wts, modx, g, b, alpha, tm_rows)


def _rope_tables(nb, lat_len, ctx_len):
    rows = lat_len // GRID_W
    row = jnp.repeat(jnp.arange(rows, dtype=F32), GRID_W)
    col = jnp.tile(jnp.arange(GRID_W, dtype=F32), rows)
    n_freq = HEAD_DIM // 4
    inv = ROPE_THETA ** (-jnp.arange(n_freq, dtype=F32) / n_freq)
    ar, ac = row[:, None] * inv, col[:, None] * inv
    cos = jnp.concatenate([jnp.cos(ar), jnp.cos(ar), jnp.cos(ac), jnp.cos(ac)], axis=1)
    sin = jnp.concatenate([-jnp.sin(ar), jnp.sin(ar), -jnp.sin(ac), jnp.sin(ac)], axis=1)
    cos = jnp.concatenate([cos, jnp.ones((ctx_len, HEAD_DIM), F32)], axis=0)
    sin = jnp.concatenate([sin, jnp.zeros((ctx_len, HEAD_DIM), F32)], axis=0)
    return jnp.tile(cos, (nb, 1)), jnp.tile(sin, (nb, 1))


def _row_tile(m, pref):
    tm = pref
    while m % tm:
        tm //= 2
    return tm


def kernel(x, c, ctx, c_ctx, w_mod, b_mod, w_in, conv_a_w, conv_a_b, ret_decay, hy_conv_w, hy_conv_b, hy_w1, hy_b1, hy_w2, hy_b2, hy_w3, hy_freq, hy_skip, q_norm, k_norm, w_gate, b_gate, w_branch, w_o, ln_g, ln_b, ffn_w1, ffn_w3, ffn_w2, router, moe_w1, moe_w3, moe_w2):
    nb, lat_len, d = x.shape
    ctx_len = ctx.shape[1]
    depth = w_mod.shape[0]
    t = lat_len + ctx_len
    m = nb * t
    alpha = float((2 * depth) ** 0.25)
    assert lat_len % SUB == 0 and ctx_len % SUB == 0 and nb % 2 == 0

    act = jnp.concatenate([jax.nn.silu(c), jax.nn.silu(c_ctx)[None, :],
                           jnp.zeros((8 - nb - 1, d), F32)], axis=0)
    mods = _mod_all(act, w_mod, b_mod).reshape(depth, 8, N_MOD, d)
    mods = jnp.pad(mods, ((0, 0), (0, 0), (0, 8 - N_MOD), (0, 0)))
    tiles_per_batch = t // SUB
    tile_pos = np.arange(m // SUB) % tiles_per_batch
    tile_group = np.where(tile_pos < lat_len // SUB, np.arange(m // SUB) // tiles_per_batch, nb)
    cos_t, sin_t = _rope_tables(nb, lat_len, ctx_len)

    cast = lambda a: a.astype(BF16)
    w_in_b, w_gate_b, w_branch_b, w_o_b = cast(w_in), cast(w_gate), cast(w_branch), cast(w_o)
    ffn_b = (cast(ffn_w1), cast(ffn_w3), cast(ffn_w2))
    moe_w = (cast(moe_w1), cast(moe_w3), cast(moe_w2))

    rows = jnp.concatenate([x, ctx], axis=1).reshape(m, d)
    tm = _row_tile(m, 512)
    for l in range(depth):
        modx = mods[l][tile_group]
        za, zvg, zh, rq, rk, aq, ak, av = _proj(rows, modx, w_in_b, l, cos_t, sin_t, q_norm[l], k_norm[l], tm)
        y_d = _attention(aq, ak, av, nb, lat_len)
        log_gamma = -jnp.exp(ret_decay[l].astype(F32))
        y_r = _retention(rq, rk, zvg, _retention_tables(log_gamma), nb, lat_len)
        y_a = _shortconv(za, conv_a_w[l], conv_a_b[l], nb, lat_len)
        y_h = _hyena(zh, hy_conv_w[l], hy_conv_b[l],
                     (hy_w1[l], hy_b1[l], hy_w2[l], hy_b2[l], hy_w3[l], hy_freq[l]), hy_skip[l], nb, lat_len)
        rows = _merge(rows, modx, (y_a, y_r, y_h, y_d), w_gate_b, b_gate[l], w_branch_b, w_o_b, l,
                      ln_g[l, 0], ln_b[l, 0], alpha, tm)
        i = l // 2
        if l % 2 == 0:
            rows = _ffn(rows, modx, *ffn_b, i, ln_g[l, 1], ln_b[l, 1], alpha, tm, 256)
        else:
            rows = _moe(rows, modx, router[i], *moe_w, i, ln_g[l, 1], ln_b[l, 1], alpha, tm, 768, 1792)
    return rows.reshape(nb, t, d)[:, :lat_len]
```

```python
import functools
import math

import numpy as np
import jax
import jax.numpy as jnp
from jax import lax
from jax.experimental import pallas as pl
from jax.experimental.pallas import tpu as pltpu

F32 = jnp.float32
BF16 = jnp.bfloat16

GRID_W = 64
BRANCH_W = 512
RET_HEADS = 4
RET_D = 128
RET_CHUNK = 128
HY_EMB = 33
HY_FAST_PCT = 0.3
HY_SLOW_PCT = 1.5
HY_TARGET = 1e-2
ATT_HEADS = 4
ATT_KV_HEADS = 2
HEAD_DIM = 128
ROPE_THETA = 10000.0
N_EXPERTS = 8
TOP_K = 2
N_MOD = 6
A_W = 3 * BRANCH_W
R_W = 4 * RET_HEADS * RET_D
H_W = 3 * BRANCH_W
D_W = (ATT_HEADS + 2 * ATT_KV_HEADS) * HEAD_DIM
IN_W = A_W + R_W + H_W + D_W
LN_EPS = 1e-6

SUB = 256
LANE = 128
FFT_N1 = 128
RET_UNROLL = 34
CONV_TC = 256
ATT_TQ = 256
ATT_KC = 4352
DFT_PASSES = 1
DFT_FILTER_PASSES = 1
VMEM_LIMIT = 56 * 1024 * 1024


def _cp(sem, vmem=VMEM_LIMIT):
    return pltpu.CompilerParams(dimension_semantics=sem, vmem_limit_bytes=vmem)


def _dot(a, b):
    return jnp.dot(a, b, preferred_element_type=F32)


def _dot_nt(a, b):
    return lax.dot_general(a, b, (((1,), (1,)), ((), ())), preferred_element_type=F32)


def _dot_tn(a, b):
    return lax.dot_general(a, b, (((0,), (0,)), ((), ())), preferred_element_type=F32)


def _split_dot(f_cat, x, passes):
    xh = x.astype(BF16)
    if passes == 1:
        return _dot(f_cat, xh)
    m = f_cat.shape[0] // 2
    r = _dot(f_cat, xh)
    r = r[:m] + r[m:]
    if passes == 3:
        r = r + _dot(f_cat[:m], (x - xh.astype(F32)).astype(BF16))
    return r


def _dft_operand(f_cat, passes):
    return f_cat[..., :f_cat.shape[-2] // 2, :] if passes == 1 else f_cat


def _modulate(x, mod_ref, shift_row, scale_row):
    parts = []
    for s in range(x.shape[0] // SUB):
        sh = mod_ref[s, shift_row:shift_row + 1, :]
        sc = mod_ref[s, scale_row:scale_row + 1, :]
        parts.append(x[s * SUB:(s + 1) * SUB] * (1.0 + sc) + sh)
    return parts[0] if len(parts) == 1 else jnp.concatenate(parts, axis=0)


def _gated_residual_ln(x, f, mod_ref, gate_row, alpha, g, b):
    parts = []
    for s in range(x.shape[0] // SUB):
        gt = mod_ref[s, gate_row:gate_row + 1, :]
        parts.append(alpha * x[s * SUB:(s + 1) * SUB] + gt * f[s * SUB:(s + 1) * SUB])
    r = parts[0] if len(parts) == 1 else jnp.concatenate(parts, axis=0)
    mu = jnp.mean(r, axis=-1, keepdims=True)
    rc = r - mu
    var = jnp.mean(rc * rc, axis=-1, keepdims=True)
    return rc * lax.rsqrt(var + LN_EPS) * g + b


def _mod_kernel(a_ref, w_ref, b_ref, o_ref):
    o_ref[0] = _dot(a_ref[...].astype(BF16), w_ref[0].astype(BF16)) + b_ref[0]


def _mod_all(act, w_mod, b_mod):
    depth, d, n = w_mod.shape
    tn = 512
    return pl.pallas_call(
        _mod_kernel,
        grid=(depth, n // tn),
        in_specs=[pl.BlockSpec((8, d), lambda l, j: (0, 0)),
                  pl.BlockSpec((1, d, tn), lambda l, j: (l, 0, j)),
                  pl.BlockSpec((1, 1, tn), lambda l, j: (l, 0, j))],
        out_specs=pl.BlockSpec((1, 8, tn), lambda l, j: (l, 0, j)),
        out_shape=jax.ShapeDtypeStruct((depth, 8, n), F32),
        compiler_params=_cp(("parallel", "parallel")),
        name="mod_vectors",
    )(act, w_mod, b_mod.reshape(depth, 1, n))


PROJ_TN = 512
assert (A_W, R_W, H_W, D_W) == (3 * PROJ_TN, 4 * PROJ_TN, 3 * PROJ_TN, 2 * PROJ_TN)


def _proj_kernel(x_ref, mod_ref, w_ref, cos_ref, sin_ref, gq_ref, gk_ref,
                 za_ref, zvg_ref, zh_ref, rq_ref, rk_ref, aq_ref, ak_ref, av_ref):
    u = _modulate(x_ref[...], mod_ref, 0, 1).astype(BF16)
    tn = PROJ_TN
    cos, sin = cos_ref[...], sin_ref[...]
    lane = lax.broadcasted_iota(jnp.int32, cos.shape, 1)
    first_half = (lane % 64) < 32

    def tile(j):
        return _dot(u, w_ref[:, j * tn:(j + 1) * tn])

    def rope(x):
        partner = jnp.where(first_half, pltpu.roll(x, 96, 1), pltpu.roll(x, 32, 1))
        return x * cos + partner * sin

    def rms(x, g):
        return x * lax.rsqrt(jnp.mean(x * x, axis=-1, keepdims=True) + LN_EPS) * g

    head = lambda h: slice(h * LANE, (h + 1) * LANE)

    for j in range(3):
        za_ref[:, j * tn:(j + 1) * tn] = tile(j)
    r = tile(3)
    for h in range(RET_HEADS):
        rq_ref[:, head(h)] = rope(r[:, head(h)])
    r = tile(4)
    for h in range(RET_HEADS):
        rk_ref[:, head(h)] = rope(r[:, head(h)] * (RET_D ** -0.5))
    for j in range(2):
        zvg_ref[:, j * tn:(j + 1) * tn] = tile(5 + j)
    for j in range(3):
        zh_ref[:, j * tn:(j + 1) * tn] = tile(7 + j)
    q_scale = (HEAD_DIM ** -0.5) * math.log2(math.e)
    r = tile(10)
    for h in range(ATT_HEADS):
        aq_ref[:, head(h)] = (rope(rms(r[:, head(h)], gq_ref[...])) * q_scale).astype(BF16)
    r = tile(11)
    ones_col = jnp.where(lane == 0, 1.0, 0.0).astype(BF16)
    for h in range(ATT_KV_HEADS):
        ak_ref[:, head(h)] = rope(rms(r[:, head(h)], gk_ref[...])).astype(BF16)
        av_ref[:, head(2 * h)] = r[:, head(ATT_KV_HEADS + h)].astype(BF16)
        av_ref[:, head(2 * h + 1)] = ones_col


def _proj(x, modx, w, layer, cos_t, sin_t, gq, gk, tm):
    m, d = x.shape
    n = w.shape[2]
    tn = PROJ_TN
    qw, kw = ATT_HEADS * HEAD_DIM, ATT_KV_HEADS * HEAD_DIM
    row = lambda i: (i, 0)
    widths = (A_W, 2 * tn, H_W, tn, tn, qw, kw, 2 * kw)
    dtypes = (F32,) * 5 + (BF16,) * 3
    return pl.pallas_call(
        _proj_kernel,
        grid=(m // tm,),
        in_specs=[pl.BlockSpec((tm, d), row),
                  pl.BlockSpec((tm // SUB, 8, d), lambda i: (i, 0, 0)),
                  pl.BlockSpec((None, d, n), lambda i: (layer, 0, 0), pipeline_mode=pl.Buffered(1)),
                  pl.BlockSpec((tm, LANE), row), pl.BlockSpec((tm, LANE), row),
                  pl.BlockSpec((1, LANE), lambda i: (0, 0)), pl.BlockSpec((1, LANE), lambda i: (0, 0))],
        out_specs=[pl.BlockSpec((tm, wd), row) for wd in widths],
        out_shape=[jax.ShapeDtypeStruct((m, wd), dt) for wd, dt in zip(widths, dtypes)],
        compiler_params=_cp(("parallel",)),
        name="in_proj",
    )(x, modx, w, cos_t, sin_t, gq.reshape(1, LANE), gk.reshape(1, LANE))


def _attn_kernel(q_ref, k_ref, v_ref, *rest, key_lo, key_hi, kc):
    o_ref = rest[-1]
    tq = q_ref.shape[1]
    grp = ATT_HEADS // ATT_KV_HEADS
    q2 = jnp.concatenate([q_ref[0, :, g * HEAD_DIM:(g + 1) * HEAD_DIM] for g in range(grp)], axis=0)
    m_run, acc = None, None
    for lo in range(key_lo, key_hi, kc):
        hi = min(lo + kc, key_hi)
        s = _dot_nt(q2, k_ref[0, lo:hi, :])
        m_chunk = jnp.max(s, axis=-1, keepdims=True)
        m_new = m_chunk if m_run is None else jnp.maximum(m_run, m_chunk)
        pv = _dot(jnp.exp2(s - m_new).astype(BF16), v_ref[0, lo:hi, :])
        acc = pv if acc is None else jnp.exp2(m_run - m_new) * acc + pv
        m_run = m_new
    for g in range(grp):
        a = acc[g * tq:(g + 1) * tq]
        o_ref[0, :, g * HEAD_DIM:(g + 1) * HEAD_DIM] = a[:, :HEAD_DIM] / a[:, HEAD_DIM:HEAD_DIM + 1]


def _attention(aq, ak, av, nb, lat_len):
    m = aq.shape[0]
    t = m // nb
    lc = t - lat_len
    grp_w = (ATT_HEADS // ATT_KV_HEADS) * HEAD_DIM
    args = (aq.reshape(nb, t, -1), ak.reshape(nb, t, -1), av.reshape(nb, t, -1))

    def call(tq, n_tiles, first_tile, key_lo, key_hi, prev):
        qmap = lambda b, kv, i: (b, first_tile + i, kv)
        specs = [pl.BlockSpec((1, tq, grp_w), qmap),
                 pl.BlockSpec((1, t, HEAD_DIM), lambda b, kv, i: (b, 0, kv)),
                 pl.BlockSpec((1, t, 2 * HEAD_DIM), lambda b, kv, i: (b, 0, kv))]
        extra, alias = (), {}
        if prev is not None:
            specs.append(pl.BlockSpec(memory_space=pl.ANY))
            extra, alias = (prev,), {3: 0}
        return pl.pallas_call(
            functools.partial(_attn_kernel, key_lo=key_lo, key_hi=key_hi, kc=ATT_KC),
            grid=(nb, ATT_KV_HEADS, n_tiles),
            in_specs=specs,
            out_specs=pl.BlockSpec((1, tq, grp_w), qmap),
            out_shape=jax.ShapeDtypeStruct((nb, t, ATT_HEADS * HEAD_DIM), F32),
            input_output_aliases=alias,
            compiler_params=_cp(("parallel", "parallel", "arbitrary")),
            name="gqa_attention",
        )(*args, *extra)

    tq = math.gcd(ATT_TQ, lat_len)
    out = call(tq, lat_len // tq, 0, 0, t, None)
    out = call(lc, 1, t // lc - 1, lat_len, t, out)
    return out.reshape(m, -1)


def _ret_kernel(q_ref, k_ref, v_ref, g_ref, intra_ref, qd_ref, kd_ref, cd_ref, o_ref, st_ref,
                *, n_lat, n_ctx):
    c_len = RET_CHUNK
    n = n_lat + n_ctx

    def rows(c):
        return pl.ds(pl.multiple_of(c * c_len, c_len), c_len)

    def kv(d, c):
        return _dot_tn((k_ref[0, rows(c), :] * kd_ref[d, 0]).astype(BF16), v_ref[0, rows(c), :].astype(BF16))

    def scan_step(t, carry):
        sf, sb = carry
        cf = jnp.where(t < n_ctx, n_lat + t, t - n_ctx)
        cb = jnp.where(t < n_ctx, n - 1 - t, n_lat - 1 - (t - n_ctx))
        st_ref[cf, :RET_D, :] = sf.astype(BF16)
        st_ref[cb, RET_D:, :] = sb.astype(BF16)
        return (cd_ref[0, 0, 0:1, :] * sf + kv(0, cf), cd_ref[1, 0, 0:1, :] * sb + kv(1, cb))

    zero = jnp.zeros((RET_D, RET_D), F32)
    lax.fori_loop(0, n, scan_step, (zero, zero), unroll=RET_UNROLL)

    mask = intra_ref[0, 0] + intra_ref[1, 0]

    def out_step(c, carry):
        qc = q_ref[0, rows(c), :]
        vc = v_ref[0, rows(c), :].astype(BF16)
        s = _dot_nt(qc.astype(BF16), k_ref[0, rows(c), :].astype(BF16)) * mask
        qdec = jnp.concatenate([qc * qd_ref[0, 0], qc * qd_ref[1, 0]], axis=1).astype(BF16)
        o = _dot(s.astype(BF16), vc) + _dot(qdec, st_ref[c])
        mu = jnp.mean(o, axis=-1, keepdims=True)
        oc = o - mu
        var = jnp.mean(oc * oc, axis=-1, keepdims=True)
        g = g_ref[0, rows(c), :]
        o_ref[0, rows(c), :] = (g * jax.nn.sigmoid(g)) * (oc * lax.rsqrt(var + LN_EPS))
        return carry

    lax.fori_loop(0, n, out_step, 0, unroll=RET_UNROLL)


def _retention(rq, rk, z, tabs, nb, lat_len):
    m = rq.shape[0]
    t = m // nb
    intra, qd, kd, cd = tabs
    v_blk, g_blk = 0, RET_HEADS
    kern = functools.partial(_ret_kernel, n_lat=lat_len // RET_CHUNK, n_ctx=(t - lat_len) // RET_CHUNK)
    seq = lambda off: pl.BlockSpec((1, t, LANE), lambda b, h: (b, 0, off + h))
    tab = lambda r: pl.BlockSpec((2, 1, r, LANE), lambda b, h: (0, h, 0, 0))
    out = pl.pallas_call(
        kern,
        grid=(nb, RET_HEADS),
        in_specs=[seq(0), seq(0), seq(v_blk), seq(g_blk),
                  tab(RET_CHUNK), tab(RET_CHUNK), tab(RET_CHUNK), tab(8)],
        out_specs=seq(0),
        out_shape=jax.ShapeDtypeStruct((nb, t, RET_HEADS * RET_D), F32),
        scratch_shapes=[pltpu.VMEM((t // RET_CHUNK, 2 * RET_D, RET_D), BF16)],
        compiler_params=_cp(("parallel", "parallel")),
        name="retention",
    )(rq.reshape(nb, t, -1), rk.reshape(nb, t, -1), z.reshape(nb, t, -1), z.reshape(nb, t, -1), intra, qd, kd, cd)
    return out.reshape(m, -1)


def _retention_tables(log_gamma):
    c = RET_CHUNK
    i = jnp.arange(c, dtype=F32)
    lg = log_gamma.astype(F32)[:, :, None]
    rel = i[:, None] - i[None, :]
    intra_f = jnp.where(rel >= 0, jnp.exp(lg[0][..., None] * jnp.maximum(rel, 0.0)), 0.0)
    intra_b = jnp.where(rel <= 0, jnp.exp(lg[1][..., None] * jnp.maximum(-rel, 0.0)), 0.0)
    qd = jnp.stack([jnp.exp(lg[0] * (i + 1.0)), jnp.exp(lg[1] * (c - i))])
    kd = jnp.stack([jnp.exp(lg[0] * (c - 1.0 - i)), jnp.exp(lg[1] * i)])
    cd = jnp.exp(lg * c)
    bc = lambda a: jnp.broadcast_to(a[..., None], a.shape + (LANE,))
    return (jnp.stack([intra_f, intra_b]), bc(qd), bc(kd),
            jnp.broadcast_to(cd[..., None], (2, RET_HEADS, 8, LANE)))


def _shift_rows(x, lat_len):
    t = x.shape[0]
    row = lax.broadcasted_iota(jnp.int32, x.shape, 0)
    xm = jnp.where((row == 0) | (row == lat_len), 0.0, pltpu.roll(x, 1, 0))
    xp = jnp.where((row == lat_len - 1) | (row == t - 1), 0.0, pltpu.roll(x, t - 1, 0))
    return xm, xp


def _shortconv_kernel(bg_ref, cg_ref, xv_ref, w_ref, b_ref, o_ref, *, lat_len):
    p = cg_ref[0] * xv_ref[0]
    pm, pp = _shift_rows(p, lat_len)
    o_ref[0] = bg_ref[0] * (pm * w_ref[0:1, :] + p * w_ref[1:2, :] + pp * w_ref[2:3, :] + b_ref[...])


def _shortconv(z, w, b, nb, lat_len):
    m = z.shape[0]
    t = m // nb
    cw = CONV_TC
    nct = BRANCH_W // cw
    kern = functools.partial(_shortconv_kernel, lat_len=lat_len)
    seq = lambda off: pl.BlockSpec((1, t, cw), lambda bb, j: (bb, 0, off + j))
    out = pl.pallas_call(
        kern,
        grid=(nb, nct),
        in_specs=[seq(0), seq(nct), seq(2 * nct),
                  pl.BlockSpec((3, cw), lambda bb, j: (0, j)),
                  pl.BlockSpec((1, cw), lambda bb, j: (0, j))],
        out_specs=seq(0),
        out_shape=jax.ShapeDtypeStruct((nb, t, BRANCH_W), F32),
        compiler_params=_cp(("parallel", "parallel")),
        name="short_conv",
    )(z.reshape(nb, t, -1), z.reshape(nb, t, -1), z.reshape(nb, t, -1), w, b.reshape(1, -1))
    return out.reshape(m, -1)


def _hyconv_kernel(z_ref, w_ref, b_ref, ol_ref, oc_ref, *, lat_len):
    x = z_ref[0]
    xm, xp = _shift_rows(x, lat_len)
    y = xm * w_ref[0:1, :] + x * w_ref[1:2, :] + xp * w_ref[2:3, :] + b_ref[...]
    ol_ref[0, 0] = y[:lat_len]
    oc_ref[0, 0] = y[lat_len:]


def _hyconv(z, w, b, nb, lat_len):
    m = z.shape[0]
    t = m // nb
    cw = CONV_TC
    nct = BRANCH_W // cw
    kern = functools.partial(_hyconv_kernel, lat_len=lat_len)
    return pl.pallas_call(
        kern,
        grid=(nb, 3 * nct),
        in_specs=[pl.BlockSpec((1, t, cw), lambda bb, j: (bb, 0, j)),
                  pl.BlockSpec((3, cw), lambda bb, j: (0, j)),
                  pl.BlockSpec((1, cw), lambda bb, j: (0, j))],
        out_specs=[pl.BlockSpec((1, 1, lat_len, cw), lambda bb, j: (j // nct, bb, 0, j % nct)),
                   pl.BlockSpec((1, 1, t - lat_len, cw), lambda bb, j: (j // nct, bb, 0, j % nct))],
        out_shape=[jax.ShapeDtypeStruct((3, nb, lat_len, BRANCH_W), F32),
                   jax.ShapeDtypeStruct((3, nb, t - lat_len, BRANCH_W), F32)],
        compiler_params=_cp(("parallel", "parallel")),
        name="hyena_dwconv",
    )(z.reshape(nb, t, -1), w, b.reshape(1, -1))


def _stack_c(m):
    return np.block([[m.real, -m.imag], [m.imag, m.real]])


def _split_np(m):
    m32 = np.asarray(m, np.float32)
    hi = m32.astype(BF16)
    lo = (m32 - hi.astype(np.float32)).astype(BF16)
    return np.concatenate([hi, lo], axis=-2)


@functools.lru_cache(maxsize=None)
def _dft_tables(lat_len):
    n = 2 * lat_len
    n1, n2 = FFT_N1, 2 * lat_len // FFT_N1
    a1 = np.arange(n1)
    a2 = np.arange(n2)
    w1 = np.exp(-2j * np.pi * np.outer(a1, a1[:n1 // 2]) / n1)
    f1 = _stack_c(w1)
    w2 = np.exp(-2j * np.pi * np.outer(a2, a2) / n2)
    tw = np.exp(-2j * np.pi * np.outer(a1, a2) / n)
    cf = w2[None, :, :] * tw[:, None, :]
    f2 = np.stack([_stack_c(cf[k]) for k in range(n1)])
    g2 = np.stack([_stack_c(cf[k].conj().T) for k in range(n1)])
    w1i = np.exp(2j * np.pi * np.outer(a1[:n1 // 2], a1) / n1) / n
    f3 = _stack_c(w1i)
    f1_real = f1[:, :n1 // 2]
    return _split_np(f1), _split_np(f1_real), _split_np(f2), _split_np(g2), _split_np(f3)


@functools.lru_cache(maxsize=None)
def _dft_tables_small(ctx_len):
    n = 2 * ctx_len
    a = np.arange(n)
    ff = _stack_c(np.exp(-2j * np.pi * np.outer(a, a[:ctx_len]) / n))
    fi = _stack_c(np.exp(2j * np.pi * np.outer(a[:ctx_len], a) / n) / n)
    return _split_np(ff), _split_np(ff[:, :ctx_len]), _split_np(fi)


def _fft1_kernel(x_ref, f_ref, o_ref, *, passes):
    f = f_ref[...]
    for j in range(x_ref.shape[3]):
        parts = [x_ref[0, b, :, j, :] for b in range(x_ref.shape[1])]
        r = _split_dot(f, parts[0] if len(parts) == 1 else jnp.concatenate(parts, axis=0), passes)
        h = r.shape[0] // 2
        o_ref[0, 0, :, j, :] = r[:h]
        o_ref[0, 1, :, j, :] = r[h:]


def _fft1(x, sel, f1, jb, passes):
    _, p, mem, r, n2, w = x.shape
    f1 = _dft_operand(f1, passes)
    n1 = f1.shape[0] // (2 if passes == 1 else 4)
    return pl.pallas_call(
        functools.partial(_fft1_kernel, passes=passes),
        grid=(p, n2 // jb),
        in_specs=[pl.BlockSpec((None, 1, mem, r, jb, w), lambda pp, j: (sel, pp, 0, 0, j, 0)),
                  pl.BlockSpec(f1.shape, lambda pp, j: (0, 0))],
        out_specs=pl.BlockSpec((1, 2, n1, jb, w), lambda pp, j: (pp, 0, 0, j, 0)),
        out_shape=jax.ShapeDtypeStruct((p, 2, n1, n2, w), F32),
        compiler_params=_cp(("parallel", "parallel")),
        name="hyena_fft_stage1",
    )(x, f1)


def _fft2f_kernel(a_ref, f_ref, o_ref, *, kb, n2, passes):
    w = a_ref.shape[-1]
    for t in range(kb):
        xf = _split_dot(f_ref[t], a_ref[0, :, t].reshape(2 * n2, w), passes)
        xb = _split_dot(f_ref[t], a_ref[1, :, t].reshape(2 * n2, w), passes)
        o_ref[0, 0, t] = xf[:n2] + xb[:n2]
        o_ref[0, 1, t] = xf[n2:] - xb[n2:]


def _fft2_filters(a, f2, kb, passes):
    q, _, n1, n2, w = a.shape
    f2 = _dft_operand(f2, passes)
    return pl.pallas_call(
        functools.partial(_fft2f_kernel, kb=kb, n2=n2, passes=passes),
        grid=(q // 2, n1 // kb),
        in_specs=[pl.BlockSpec((2, 2, kb, n2, w), lambda o, k: (o, 0, k, 0, 0)),
                  pl.BlockSpec((kb,) + f2.shape[1:], lambda o, k: (k, 0, 0))],
        out_specs=pl.BlockSpec((1, 2, kb, n2, w), lambda o, k: (o, 0, k, 0, 0)),
        out_shape=jax.ShapeDtypeStruct((q // 2, 2, n1, n2, w), F32),
        compiler_params=_cp(("parallel", "parallel")),
        name="hyena_filter_spectrum",
    )(a, f2)


def _fft2_kernel(a_ref, h_ref, f_ref, g_ref, o_ref, *, kb, n2, passes):
    w = a_ref.shape[-1]
    for t in range(kb):
        a = a_ref[0, :, t].reshape(2 * n2, w)
        x = _split_dot(f_ref[t], a, passes)
        xr, xi = x[:n2], x[n2:]
        hr, hi = h_ref[0, t], h_ref[1, t]
        y = jnp.concatenate([xr * hr - xi * hi, xr * hi + xi * hr], axis=0)
        o_ref[0, :, t] = _split_dot(g_ref[t], y, passes).reshape(2, n2, w)


def _fft2(a, hspec, order, f2, g2, kb, passes):
    p, _, n1, n2, w = a.shape
    f2, g2 = _dft_operand(f2, passes), _dft_operand(g2, passes)
    kern = functools.partial(_fft2_kernel, kb=kb, n2=n2, passes=passes)
    return pl.pallas_call(
        kern,
        grid=(n1 // kb, p),
        in_specs=[pl.BlockSpec((1, 2, kb, n2, w), lambda k, pp: (pp, 0, k, 0, 0)),
                  pl.BlockSpec((None, 2, kb, n2, w), lambda k, pp: (order, 0, k, 0, 0)),
                  pl.BlockSpec((kb,) + f2.shape[1:], lambda k, pp: (k, 0, 0)),
                  pl.BlockSpec((kb,) + g2.shape[1:], lambda k, pp: (k, 0, 0))],
        out_specs=pl.BlockSpec((1, 2, kb, n2, w), lambda k, pp: (pp, 0, k, 0, 0)),
        out_shape=jax.ShapeDtypeStruct(a.shape, F32),
        compiler_params=_cp(("parallel", "arbitrary")),
        name="hyena_fft_stage2",
    )(a, hspec, f2, g2)


def _fft3_kernel(b_ref, f_ref, gate_ref, prev_ref, skip_ref, o_ref, *, passes):
    f = f_ref[...]
    r = gate_ref.shape[2]
    for j in range(b_ref.shape[3]):
        bj = jnp.concatenate([b_ref[0, 0, :, j, :], b_ref[0, 1, :, j, :]], axis=0)
        conv = _split_dot(f, bj, passes)
        for mem in range(2):
            cm = conv[mem * r:(mem + 1) * r]
            o_ref[0, mem, :, j, :] = gate_ref[0, mem, :, j, :] * (cm + skip_ref[...] * prev_ref[0, mem, :, j, :])


def _fft3(bm, f3, gate, gate_sel, prev, prev_sel, skip, out_rows, jb, passes):
    p, _, n1, n2, w = bm.shape
    r = n1 // 2
    f3 = _dft_operand(f3, passes)
    seq = lambda s: pl.BlockSpec((None, 1, 2, r, jb, w), lambda pp, j: (s, pp, 0, 0, j, 0))
    return pl.pallas_call(
        functools.partial(_fft3_kernel, passes=passes),
        grid=(p, n2 // jb),
        in_specs=[pl.BlockSpec((1, 2, n1, jb, w), lambda pp, j: (pp, 0, 0, j, 0)),
                  pl.BlockSpec(f3.shape, lambda pp, j: (0, 0)),
                  seq(gate_sel), seq(prev_sel),
                  pl.BlockSpec((1, w), lambda pp, j: (0, 0))],
        out_specs=pl.BlockSpec((1, 2, r, jb, w), lambda pp, j: (pp, 0, 0, j, 0)),
        out_shape=jax.ShapeDtypeStruct((p, 2, out_rows, n2, w), F32),
        compiler_params=_cp(("parallel", "parallel")),
        name="hyena_fft_stage3",
    )(bm, f3, gate, prev, skip)


def _hyctx_kernel(v_ref, x1_ref, x2_ref, filt_ref, ff_ref, ffr_ref, fi_ref, skip_ref, yh_hbm_ref, o_ref,
                  *, passes, filter_passes):
    del yh_hbm_ref
    lc, w = v_ref.shape[2], v_ref.shape[3]
    n = 2 * lc
    y = v_ref[0].reshape(2 * lc, w)
    for o, gate_ref in enumerate((x1_ref, x2_ref)):
        sf = _split_dot(ffr_ref[...], filt_ref[o, 0], filter_passes)
        sb = _split_dot(ffr_ref[...], filt_ref[o, 1], filter_passes)
        hr, hi = sf[:n] + sb[:n], sf[n:] - sb[n:]
        x = _split_dot(ff_ref[...], y, passes)
        xr, xi = x[:n], x[n:]
        spec = jnp.concatenate([xr * hr - xi * hi, xr * hi + xi * hr], axis=0)
        conv = _split_dot(fi_ref[...], spec, passes)
        y = gate_ref[0].reshape(2 * lc, w) * (conv + skip_ref[o:o + 1, :] * y)
    o_ref[0] = y.reshape(2, lc, w)


def _hyena_ctx(hc, filt_c, ff, ffr, fi, skip, yh, passes, filter_passes):
    _, nb, lc, w = hc.shape
    p = nb // 2
    t = yh.shape[2]
    ff, fi, ffr = _dft_operand(ff, passes), _dft_operand(fi, passes), _dft_operand(ffr, filter_passes)
    part = pl.BlockSpec((1, 2, lc, w), lambda pp: (pp, 0, 0, 0))
    hc4 = hc.reshape(3, p, 2, lc, w)
    return pl.pallas_call(
        functools.partial(_hyctx_kernel, passes=passes, filter_passes=filter_passes),
        grid=(p,),
        in_specs=[part, part, part,
                  pl.BlockSpec(filt_c.shape, lambda pp: (0, 0, 0, 0)),
                  pl.BlockSpec(ff.shape, lambda pp: (0, 0)),
                  pl.BlockSpec(ffr.shape, lambda pp: (0, 0)),
                  pl.BlockSpec(fi.shape, lambda pp: (0, 0)),
                  pl.BlockSpec(skip.shape, lambda pp: (0, 0)),
                  pl.BlockSpec(memory_space=pl.ANY)],
        out_specs=pl.BlockSpec((1, 2, lc, w), lambda pp: (pp, 0, t // lc - 1, 0)),
        out_shape=jax.ShapeDtypeStruct(yh.shape, F32),
        input_output_aliases={8: 0},
        compiler_params=_cp(("parallel",)),
        name="hyena_ctx",
    )(hc4[0], hc4[1], hc4[2], filt_c, ff, ffr, fi, skip, yh)


def _hyena_filters(l, w1, b1, w2, b2, w3, freq):
    t = jnp.linspace(0.0, 1.0, l, dtype=F32)[:, None]
    bands = (HY_EMB - 1) // 2
    w = 2.0 * math.pi * jnp.arange(l, dtype=F32) / l
    f = jnp.linspace(1e-4, bands - 1, bands, dtype=F32)
    ang = w[:, None] * f[None, :]
    feats = jnp.concatenate([t, jnp.cos(ang), -jnp.sin(ang)], axis=-1)
    hdn = jnp.sin(freq[0] * (feats @ w1 + b1))
    hdn = jnp.sin(freq[1] * (hdn @ w2 + b2))
    n_out = w3.shape[-1]
    deltas = jnp.linspace(math.log(HY_TARGET) / HY_SLOW_PCT, math.log(HY_TARGET) / HY_FAST_PCT, n_out, dtype=F32)
    decay = jnp.exp(-t * jnp.abs(deltas))
    pos = jnp.arange(l)[:, None]
    parts = []
    for q in range(n_out // BRANCH_W):
        sl = slice(q * BRANCH_W, (q + 1) * BRANCH_W)
        f = (hdn @ w3[:, sl]).astype(F32) * decay[:, sl]
        parts.append(jnp.where(pos == 0, 0.0, f) if q % 2 else f)
    return jnp.stack(parts).reshape(n_out // (2 * BRANCH_W), 2, l, BRANCH_W)


def _hyena(z, conv_w, conv_b, filt_w, skip, nb, lat_len):
    m = z.shape[0]
    t = m // nb
    lc = t - lat_len
    w = BRANCH_W
    p = nb // 2
    n1, n2 = FFT_N1, 2 * lat_len // FFT_N1
    r = n1 // 2
    hl, hc = _hyconv(z, conv_w, conv_b, nb, lat_len)
    f1, f1r, f2, g2, f3 = _dft_tables(lat_len)
    ff, ffr, fi = _dft_tables_small(lc)
    jb, kb = min(16, n2), 16
    dp, fp = DFT_PASSES, DFT_FILTER_PASSES
    filt_l = _hyena_filters(lat_len, *filt_w)
    fa = _fft1(filt_l.reshape(1, 4, 1, r, n2, w), 0, f1r, jb, fp)
    hs_l = _fft2_filters(fa, f2, kb, fp)
    hl6 = hl.reshape(3, p, 2, r, n2, w)
    y, y_sel = hl6, 0
    for o in range(2):
        a = _fft1(y, y_sel, f1, jb, dp)
        bm = _fft2(a, hs_l, o, f2, g2, kb, dp)
        rows = r if o == 0 else t // n2
        y = _fft3(bm, f3, hl6, 1 + o, y, y_sel, skip[o:o + 1], rows, jb, dp)[None]
        y_sel = 0
    yh = y.reshape(p, 2, t, w)
    yh = _hyena_ctx(hc, _hyena_filters(lc, *filt_w), ff, ffr, fi, skip, yh, dp, fp)
    return yh.reshape(m, w)


def _merge_kernel(x_ref, mod_ref, ya_ref, yr_ref, yh_ref, yd_ref, wg_ref, bg_ref, wb_ref, wo_ref, g_ref, b_ref,
                  o_ref, *, alpha):
    x = x_ref[...]
    u = _modulate(x, mod_ref, 0, 1).astype(BF16)
    acc = None
    for i, y_ref in enumerate((ya_ref, yr_ref, yh_ref, yd_ref)):
        gate = jax.nn.sigmoid(_dot(u, wg_ref[i]) + bg_ref[i:i + 1, :])
        term = gate * _dot(y_ref[...].astype(BF16), wb_ref[i])
        acc = term if acc is None else acc + term
    mix = _dot(acc.astype(BF16), wo_ref[...])
    o_ref[...] = _gated_residual_ln(x, mix, mod_ref, 2, alpha, g_ref[...], b_ref[...])


def _merge(x, modx, ys, wg, bg, wb, wo, layer, g, b, alpha, tm):
    m, d = x.shape
    w = BRANCH_W
    row = lambda i: (i, 0)
    const2 = lambda i: (0, 0)
    one = pl.Buffered(1)
    stacked = lambda a: pl.BlockSpec((None,) + a.shape[1:], lambda i: (layer,) + (0,) * (a.ndim - 1),
                                     pipeline_mode=one)
    return pl.pallas_call(
        functools.partial(_merge_kernel, alpha=alpha),
        grid=(m // tm,),
        in_specs=[pl.BlockSpec((tm, d), row),
                  pl.BlockSpec((tm // SUB, 8, d), lambda i: (i, 0, 0)),
                  pl.BlockSpec((tm, w), row), pl.BlockSpec((tm, w), row),
                  pl.BlockSpec((tm, w), row), pl.BlockSpec((tm, w), row),
                  stacked(wg), pl.BlockSpec(bg.shape, const2), stacked(wb), stacked(wo),
                  pl.BlockSpec((1, d), const2), pl.BlockSpec((1, d), const2)],
        out_specs=pl.BlockSpec((tm, d), row),
        out_shape=jax.ShapeDtypeStruct((m, d), F32),
        compiler_params=_cp(("parallel",)),
        name="branch_merge",
    )(x, modx, *ys, wg, bg, wb, wo, g.reshape(1, d), b.reshape(1, d))


def _ffn_kernel(x_ref, mod_ref, w1_ref, w3_ref, w2_ref, g_ref, b_ref, o_ref, *, alpha, tf):
    x = x_ref[...]
    u = _modulate(x, mod_ref, 3, 4).astype(BF16)
    acc = None
    for j in range(w1_ref.shape[1] // tf):
        sl = slice(j * tf, (j + 1) * tf)
        h1 = _dot(u, w1_ref[:, sl])
        h3 = _dot(u, w3_ref[:, sl])
        part = _dot((h1 * jax.nn.sigmoid(h1) * h3).astype(BF16), w2_ref[sl, :])
        acc = part if acc is None else acc + part
    o_ref[...] = _gated_residual_ln(x, acc, mod_ref, 5, alpha, g_ref[...], b_ref[...])


def _ffn(x, modx, w1, w3, w2, layer, g, b, alpha, tm, tf):
    m, d = x.shape
    resident = lambda a: pl.BlockSpec((None,) + a.shape[1:], lambda i: (layer, 0, 0), pipeline_mode=pl.Buffered(1))
    return pl.pallas_call(
        functools.partial(_ffn_kernel, alpha=alpha, tf=tf),
        grid=(m // tm,),
        in_specs=[pl.BlockSpec((tm, d), lambda i: (i, 0)),
                  pl.BlockSpec((tm // SUB, 8, d), lambda i: (i, 0, 0)),
                  resident(w1), resident(w3), resident(w2),
                  pl.BlockSpec((1, d), lambda i: (0, 0)),
                  pl.BlockSpec((1, d), lambda i: (0, 0))],
        out_specs=pl.BlockSpec((tm, d), lambda i: (i, 0)),
        out_shape=jax.ShapeDtypeStruct((m, d), F32),
        compiler_params=_cp(("parallel",)),
        name="dense_swiglu",
    )(x, modx, w1, w3, w2, g.reshape(1, d), b.reshape(1, d))


def _route_kernel(x_ref, mod_ref, wr_ref, u_ref, lg_ref):
    u = _modulate(x_ref[...], mod_ref, 3, 4)
    lg_ref[...] = jnp.dot(u, wr_ref[...], preferred_element_type=F32, precision=lax.Precision.HIGHEST)
    half = u.shape[1] // 2
    bits = lax.bitcast_convert_type(u.astype(BF16).astype(F32), jnp.uint32)
    u_ref[...] = lax.bitcast_convert_type(bits[:, :half] | (bits[:, half:] >> 16), F32)


def _route(x, modx, wr_pad, tm):
    m, d = x.shape
    return pl.pallas_call(
        _route_kernel,
        grid=(m // tm,),
        in_specs=[pl.BlockSpec((tm, d), lambda i: (i, 0)),
                  pl.BlockSpec((tm // SUB, 8, d), lambda i: (i, 0, 0)),
                  pl.BlockSpec((d, LANE), lambda i: (0, 0))],
        out_specs=[pl.BlockSpec((tm, d // 2), lambda i: (i, 0)), pl.BlockSpec((tm, LANE), lambda i: (i, 0))],
        out_shape=[jax.ShapeDtypeStruct((m, d // 2), F32), jax.ShapeDtypeStruct((m, LANE), F32)],
        compiler_params=_cp(("parallel",)),
        name="moe_router",
    )(x, modx, wr_pad)


def _moe_kernel(be_ref, nu_ref, xb_ref, w1_ref, w3_ref, w2_ref, o_ref, acc_ref, u_ref):
    i, j = pl.program_id(0), pl.program_id(1)
    live = i < nu_ref[0]

    @pl.when(j == 0)
    def _():
        acc_ref[...] = jnp.zeros_like(acc_ref)
        words = lax.bitcast_convert_type(xb_ref[...], jnp.uint32)
        half = words.shape[1]
        hi = lax.bitcast_convert_type(words & jnp.uint32(0xFFFF0000), F32)
        lo = lax.bitcast_convert_type(words << 16, F32)
        u_ref[:, :half] = hi.astype(BF16)
        u_ref[:, half:] = lo.astype(BF16)

    @pl.when(live)
    def _():
        u = u_ref[...]
        h1 = _dot(u, w1_ref[0])
        h3 = _dot(u, w3_ref[0])
        acc_ref[...] += _dot((h1 * jax.nn.sigmoid(h1) * h3).astype(BF16), w2_ref[0])

    @pl.when(j == pl.num_programs(1) - 1)
    def _():
        o_ref[...] = acc_ref[...]


def _moe_experts(xb, blk_e, n_used, w1, w3, w2, layer, tm, tf):
    r = xb.shape[0]
    d = w1.shape[2]
    ff = w1.shape[3]
    nj = ff // tf

    def wmap(i, j, be, nu):
        return (layer, be[i], 0, jnp.where(i < nu[0], j, nj - 1))

    def w2map(i, j, be, nu):
        return (layer, be[i], jnp.where(i < nu[0], j, nj - 1), 0)

    grid_spec = pltpu.PrefetchScalarGridSpec(
        num_scalar_prefetch=2,
        grid=(r // tm, nj),
        in_specs=[pl.BlockSpec((tm, d // 2), lambda i, j, be, nu: (i, 0)),
                  pl.BlockSpec((None, 1, d, tf), wmap),
                  pl.BlockSpec((None, 1, d, tf), wmap),
                  pl.BlockSpec((None, 1, tf, d), w2map)],
        out_specs=pl.BlockSpec((tm, d), lambda i, j, be, nu: (i, 0)),
        scratch_shapes=[pltpu.VMEM((tm, d), F32), pltpu.VMEM((tm, d), BF16)],
    )
    return pl.pallas_call(
        _moe_kernel,
        grid_spec=grid_spec,
        out_shape=jax.ShapeDtypeStruct((r, d), F32),
        compiler_params=_cp(("arbitrary", "arbitrary")),
        name="moe_experts",
    )(blk_e, n_used, xb, w1, w3, w2)


def _ln_res_kernel(x_ref, f0_ref, f1_ref, wt_ref, mod_ref, g_ref, b_ref, o_ref, *, alpha):
    f = f0_ref[...] * wt_ref[:, 0:1] + f1_ref[...] * wt_ref[:, 1:2]
    o_ref[...] = _gated_residual_ln(x_ref[...], f, mod_ref, 5, alpha, g_ref[...], b_ref[...])


def _ln_res(x, f01, wts, modx, g, b, alpha, tm):
    m, d = x.shape
    nt = m // tm
    return pl.pallas_call(
        functools.partial(_ln_res_kernel, alpha=alpha),
        grid=(nt,),
        in_specs=[pl.BlockSpec((tm, d), lambda i: (i, 0)),
                  pl.BlockSpec((tm, d), lambda i: (i, 0)),
                  pl.BlockSpec((tm, d), lambda i: (i + nt, 0)),
                  pl.BlockSpec((tm, TOP_K), lambda i: (i, 0)),
                  pl.BlockSpec((tm // SUB, 8, d), lambda i: (i, 0, 0)),
                  pl.BlockSpec((1, d), lambda i: (0, 0)), pl.BlockSpec((1, d), lambda i: (0, 0))],
        out_specs=pl.BlockSpec((tm, d), lambda i: (i, 0)),
        out_shape=jax.ShapeDtypeStruct((m, d), F32),
        compiler_params=_cp(("parallel",)),
        name="moe_residual_ln",
    )(x, f01, f01, wts, modx, g.reshape(1, d), b.reshape(1, d))


def _moe(x, modx, router, w1, w3, w2, layer, g, b, alpha, tm_rows, tm, tf):
    m, d = x.shape
    wr_pad = jnp.zeros((d, LANE), F32).at[:, :N_EXPERTS].set(router)
    u, logits = _route(x, modx, wr_pad, tm_rows)
    top_v, top_i = lax.top_k(logits[:, :N_EXPERTS], TOP_K)
    wts = jax.nn.softmax(top_v, axis=-1)
    e_flat = top_i.reshape(-1).astype(jnp.int32)
    n_pair = m * TOP_K
    onehot = (e_flat[:, None] == jnp.arange(N_EXPERTS, dtype=jnp.int32)[None, :]).astype(jnp.int32)
    csum = jnp.cumsum(onehot, axis=0)
    counts = csum[-1]
    rank = jnp.take_along_axis(csum, e_flat[:, None], axis=1)[:, 0] - 1
    padded = (counts + tm - 1) // tm * tm
    pad_end = jnp.cumsum(padded)
    pad_start = pad_end - padded
    dest = pad_start[e_flat] + rank
    n_blk = -(-n_pair // tm) + N_EXPERTS
    r = n_blk * tm
    tok = jnp.arange(n_pair, dtype=jnp.int32) // TOP_K
    n_used = (pad_end[-1] // tm).astype(jnp.int32)
    blk = jnp.minimum(jnp.arange(n_blk, dtype=jnp.int32), n_used - 1) * tm
    blk_e = jnp.minimum(jnp.searchsorted(pad_end, blk, side='right'), N_EXPERTS - 1).astype(jnp.int32)
    _, tok_sorted = lax.sort_key_val(dest, tok)
    shift = pad_start - (jnp.cumsum(counts) - counts)
    src = jnp.arange(r, dtype=jnp.int32) - jnp.repeat(shift[blk_e], tm).astype(jnp.int32)
    row_tok = jnp.take(tok_sorted, jnp.clip(src, 0, n_pair - 1), mode="clip")
    xb = jnp.take(u, row_tok, axis=0, mode="clip")
    yb = _moe_experts(xb, blk_e, n_used.reshape(1), w1, w3, w2, layer, tm, tf)
    f01 = jnp.take(yb, dest.reshape(m, TOP_K).T.reshape(-1), axis=0, mode="clip")
    return _ln_res(x, f01, wts, modx, g, b, alpha, tm_rows)


def _rope_tables(nb, lat_len, ctx_len):
    rows = lat_len // GRID_W
    row = jnp.repeat(jnp.arange(rows, dtype=F32), GRID_W)
    col = jnp.tile(jnp.arange(GRID_W, dtype=F32), rows)
    n_freq = HEAD_DIM // 4
    inv = ROPE_THETA ** (-jnp.arange(n_freq, dtype=F32) / n_freq)
    ar, ac = row[:, None] * inv, col[:, None] * inv
    cos = jnp.concatenate([jnp.cos(ar), jnp.cos(ar), jnp.cos(ac), jnp.cos(ac)], axis=1)
    sin = jnp.concatenate([-jnp.sin(ar), jnp.sin(ar), -jnp.sin(ac), jnp.sin(ac)], axis=1)
    cos = jnp.concatenate([cos, jnp.ones((ctx_len, HEAD_DIM), F32)], axis=0)
    sin = jnp.concatenate([sin, jnp.zeros((ctx_len, HEAD_DIM), F32)], axis=0)
    return jnp.tile(cos, (nb, 1)), jnp.tile(sin, (nb, 1))


def _row_tile(m, pref):
    tm = pref
    while m % tm:
        tm //= 2
    return tm


def kernel(x, c, ctx, c_ctx, w_mod, b_mod, w_in, conv_a_w, conv_a_b, ret_decay, hy_conv_w, hy_conv_b, hy_w1, hy_b1, hy_w2, hy_b2, hy_w3, hy_freq, hy_skip, q_norm, k_norm, w_gate, b_gate, w_branch, w_o, ln_g, ln_b, ffn_w1, ffn_w3, ffn_w2, router, moe_w1, moe_w3, moe_w2):
    nb, lat_len, d = x.shape
    ctx_len = ctx.shape[1]
    depth = w_mod.shape[0]
    t = lat_len + ctx_len
    m = nb * t
    alpha = float((2 * depth) ** 0.25)
    assert lat_len % SUB == 0 and ctx_len % SUB == 0 and nb % 2 == 0

    act = jnp.concatenate([jax.nn.silu(c), jax.nn.silu(c_ctx)[None, :],
                           jnp.zeros((8 - nb - 1, d), F32)], axis=0)
    mods = _mod_all(act, w_mod, b_mod).reshape(depth, 8, N_MOD, d)
    mods = jnp.pad(mods, ((0, 0), (0, 0), (0, 8 - N_MOD), (0, 0)))
    tiles_per_batch = t // SUB
    tile_pos = np.arange(m // SUB) % tiles_per_batch
    tile_group = np.where(tile_pos < lat_len // SUB, np.arange(m // SUB) // tiles_per_batch, nb)
    cos_t, sin_t = _rope_tables(nb, lat_len, ctx_len)

    cast = lambda a: a.astype(BF16)
    w_in_b, w_gate_b, w_branch_b, w_o_b = cast(w_in), cast(w_gate), cast(w_branch), cast(w_o)
    ffn_b = (cast(ffn_w1), cast(ffn_w3), cast(ffn_w2))
    moe_w = (cast(moe_w1), cast(moe_w3), cast(moe_w2))

    rows = jnp.concatenate([x, ctx], axis=1).reshape(m, d)
    tm = _row_tile(m, 512)
    for l in range(depth):
        modx = mods[l][tile_group]
        za, zvg, zh, rq, rk, aq, ak, av = _proj(rows, modx, w_in_b, l, cos_t, sin_t, q_norm[l], k_norm[l], tm)
        y_d = _attention(aq, ak, av, nb, lat_len)
        log_gamma = -jnp.exp(ret_decay[l].astype(F32))
        y_r = _retention(rq, rk, zvg, _retention_tables(log_gamma), nb, lat_len)
        y_a = _shortconv(za, conv_a_w[l], conv_a_b[l], nb, lat_len)
        y_h = _hyena(zh, hy_conv_w[l], hy_conv_b[l],
                     (hy_w1[l], hy_b1[l], hy_w2[l], hy_b2[l], hy_w3[l], hy_freq[l]), hy_skip[l], nb, lat_len)
        rows = _merge(rows, modx, (y_a, y_r, y_h, y_d), w_gate_b, b_gate[l], w_branch_b, w_o_b, l,
                      ln_g[l, 0], ln_b[l, 0], alpha, tm)
        i = l // 2
        if l % 2 == 0:
            rows = _ffn(rows, modx, *ffn_b, i, ln_g[l, 1], ln_b[l, 1], alpha, tm, 256)
        else:
            rows = _moe(rows, modx, router[i], *moe_w, i, ln_g[l, 1], ln_b[l, 1], alpha, tm, 768, 1792)
    return rows.reshape(nb, t, d)[:, :lat_len]
```

```python
import functools
import math

import numpy as np
import jax
import jax.numpy as jnp
from jax import lax
from jax.experimental import pallas as pl
from jax.experimental.pallas import tpu as pltpu

F32 = jnp.float32
BF16 = jnp.bfloat16

GRID_W = 64
BRANCH_W = 512
RET_HEADS = 4
RET_D = 128
RET_CHUNK = 128
HY_EMB = 33
HY_FAST_PCT = 0.3
HY_SLOW_PCT = 1.5
HY_TARGET = 1e-2
ATT_HEADS = 4
ATT_KV_HEADS = 2
HEAD_DIM = 128
ROPE_THETA = 10000.0
N_EXPERTS = 8
TOP_K = 2
N_MOD = 6
A_W = 3 * BRANCH_W
R_W = 4 * RET_HEADS * RET_D
H_W = 3 * BRANCH_W
D_W = (ATT_HEADS + 2 * ATT_KV_HEADS) * HEAD_DIM
IN_W = A_W + R_W + H_W + D_W
LN_EPS = 1e-6

SUB = 256
LANE = 128
FFT_N1 = 128
RET_UNROLL = 34
CONV_TC = 256
ATT_TQ = 512
ATT_KC = 1536
DFT_PASSES = 1
DFT_FILTER_PASSES = 1
VMEM_LIMIT = 56 * 1024 * 1024


def _cp(sem, vmem=VMEM_LIMIT):
    return pltpu.CompilerParams(dimension_semantics=sem, vmem_limit_bytes=vmem)


def _dot(a, b):
    return jnp.dot(a, b, preferred_element_type=F32)


def _dot_nt(a, b):
    return lax.dot_general(a, b, (((1,), (1,)), ((), ())), preferred_element_type=F32)


def _dot_tn(a, b):
    return lax.dot_general(a, b, (((0,), (0,)), ((), ())), preferred_element_type=F32)


def _split_dot(f_cat, x, passes):
    xh = x.astype(BF16)
    if passes == 1:
        return _dot(f_cat, xh)
    m = f_cat.shape[0] // 2
    r = _dot(f_cat, xh)
    r = r[:m] + r[m:]
    if passes == 3:
        r = r + _dot(f_cat[:m], (x - xh.astype(F32)).astype(BF16))
    return r


def _dft_operand(f_cat, passes):
    return f_cat[..., :f_cat.shape[-2] // 2, :] if passes == 1 else f_cat


def _modulate(x, mod_ref, shift_row, scale_row):
    parts = []
    for s in range(x.shape[0] // SUB):
        sh = mod_ref[s, shift_row:shift_row + 1, :]
        sc = mod_ref[s, scale_row:scale_row + 1, :]
        parts.append(x[s * SUB:(s + 1) * SUB] * (1.0 + sc) + sh)
    return parts[0] if len(parts) == 1 else jnp.concatenate(parts, axis=0)


def _gated_residual_ln(x, f, mod_ref, gate_row, alpha, g, b):
    parts = []
    for s in range(x.shape[0] // SUB):
        gt = mod_ref[s, gate_row:gate_row + 1, :]
        parts.append(alpha * x[s * SUB:(s + 1) * SUB] + gt * f[s * SUB:(s + 1) * SUB])
    r = parts[0] if len(parts) == 1 else jnp.concatenate(parts, axis=0)
    mu = jnp.mean(r, axis=-1, keepdims=True)
    rc = r - mu
    var = jnp.mean(rc * rc, axis=-1, keepdims=True)
    return rc * lax.rsqrt(var + LN_EPS) * g + b


def _mod_kernel(a_ref, w_ref, b_ref, o_ref):
    o_ref[0] = _dot(a_ref[...].astype(BF16), w_ref[0].astype(BF16)) + b_ref[0]


def _mod_all(act, w_mod, b_mod):
    depth, d, n = w_mod.shape
    tn = 512
    return pl.pallas_call(
        _mod_kernel,
        grid=(depth, n // tn),
        in_specs=[pl.BlockSpec((8, d), lambda l, j: (0, 0)),
                  pl.BlockSpec((1, d, tn), lambda l, j: (l, 0, j)),
                  pl.BlockSpec((1, 1, tn), lambda l, j: (l, 0, j))],
        out_specs=pl.BlockSpec((1, 8, tn), lambda l, j: (l, 0, j)),
        out_shape=jax.ShapeDtypeStruct((depth, 8, n), F32),
        compiler_params=_cp(("parallel", "parallel")),
        name="mod_vectors",
    )(act, w_mod, b_mod.reshape(depth, 1, n))


PROJ_TN = 512
assert (A_W, R_W, H_W, D_W) == (3 * PROJ_TN, 4 * PROJ_TN, 3 * PROJ_TN, 2 * PROJ_TN)


def _proj_kernel(x_ref, mod_ref, w_ref, cos_ref, sin_ref, gq_ref, gk_ref,
                 za_ref, zvg_ref, zh_ref, rq_ref, rk_ref, aq_ref, ak_ref, av_ref):
    u = _modulate(x_ref[...], mod_ref, 0, 1).astype(BF16)
    tn = PROJ_TN
    cos, sin = cos_ref[...], sin_ref[...]
    lane = lax.broadcasted_iota(jnp.int32, cos.shape, 1)
    first_half = (lane % 64) < 32

    def tile(j):
        return _dot(u, w_ref[:, j * tn:(j + 1) * tn])

    def rope(x):
        partner = jnp.where(first_half, pltpu.roll(x, 96, 1), pltpu.roll(x, 32, 1))
        return x * cos + partner * sin

    def rms(x, g):
        return x * lax.rsqrt(jnp.mean(x * x, axis=-1, keepdims=True) + LN_EPS) * g

    head = lambda h: slice(h * LANE, (h + 1) * LANE)

    for j in range(3):
        za_ref[:, j * tn:(j + 1) * tn] = tile(j)
    r = tile(3)
    for h in range(RET_HEADS):
        rq_ref[:, head(h)] = rope(r[:, head(h)])
    r = tile(4)
    for h in range(RET_HEADS):
        rk_ref[:, head(h)] = rope(r[:, head(h)] * (RET_D ** -0.5))
    for j in range(2):
        zvg_ref[:, j * tn:(j + 1) * tn] = tile(5 + j)
    for j in range(3):
        zh_ref[:, j * tn:(j + 1) * tn] = tile(7 + j)
    q_scale = (HEAD_DIM ** -0.5) * math.log2(math.e)
    r = tile(10)
    for h in range(ATT_HEADS):
        aq_ref[:, head(h)] = (rope(rms(r[:, head(h)], gq_ref[...])) * q_scale).astype(BF16)
    r = tile(11)
    ones_col = jnp.where(lane == 0, 1.0, 0.0).astype(BF16)
    for h in range(ATT_KV_HEADS):
        ak_ref[:, head(h)] = rope(rms(r[:, head(h)], gk_ref[...])).astype(BF16)
        av_ref[:, head(2 * h)] = r[:, head(ATT_KV_HEADS + h)].astype(BF16)
        av_ref[:, head(2 * h + 1)] = ones_col


def _proj(x, modx, w, layer, cos_t, sin_t, gq, gk, tm):
    m, d = x.shape
    n = w.shape[2]
    tn = PROJ_TN
    qw, kw = ATT_HEADS * HEAD_DIM, ATT_KV_HEADS * HEAD_DIM
    row = lambda i: (i, 0)
    widths = (A_W, 2 * tn, H_W, tn, tn, qw, kw, 2 * kw)
    dtypes = (F32,) * 5 + (BF16,) * 3
    return pl.pallas_call(
        _proj_kernel,
        grid=(m // tm,),
        in_specs=[pl.BlockSpec((tm, d), row),
                  pl.BlockSpec((tm // SUB, 8, d), lambda i: (i, 0, 0)),
                  pl.BlockSpec((None, d, n), lambda i: (layer, 0, 0), pipeline_mode=pl.Buffered(1)),
                  pl.BlockSpec((tm, LANE), row), pl.BlockSpec((tm, LANE), row),
                  pl.BlockSpec((1, LANE), lambda i: (0, 0)), pl.BlockSpec((1, LANE), lambda i: (0, 0))],
        out_specs=[pl.BlockSpec((tm, wd), row) for wd in widths],
        out_shape=[jax.ShapeDtypeStruct((m, wd), dt) for wd, dt in zip(widths, dtypes)],
        compiler_params=_cp(("parallel",)),
        name="in_proj",
    )(x, modx, w, cos_t, sin_t, gq.reshape(1, LANE), gk.reshape(1, LANE))


def _attn_kernel(q_ref, k_ref, v_ref, *rest, key_lo, key_hi, kc):
    o_ref = rest[-1]
    tq = q_ref.shape[1]
    grp = ATT_HEADS // ATT_KV_HEADS
    q2 = jnp.concatenate([q_ref[0, :, g * HEAD_DIM:(g + 1) * HEAD_DIM] for g in range(grp)], axis=0)
    m_run, acc = None, None
    for lo in range(key_lo, key_hi, kc):
        hi = min(lo + kc, key_hi)
        s = _dot_nt(q2, k_ref[0, lo:hi, :])
        m_chunk = jnp.max(s, axis=-1, keepdims=True)
        m_new = m_chunk if m_run is None else jnp.maximum(m_run, m_chunk)
        pv = _dot(jnp.exp2(s - m_new).astype(BF16), v_ref[0, lo:hi, :])
        acc = pv if acc is None else jnp.exp2(m_run - m_new) * acc + pv
        m_run = m_new
    for g in range(grp):
        a = acc[g * tq:(g + 1) * tq]
        o_ref[0, :, g * HEAD_DIM:(g + 1) * HEAD_DIM] = a[:, :HEAD_DIM] / a[:, HEAD_DIM:HEAD_DIM + 1]


def _attention(aq, ak, av, nb, lat_len):
    m = aq.shape[0]
    t = m // nb
    lc = t - lat_len
    grp_w = (ATT_HEADS // ATT_KV_HEADS) * HEAD_DIM
    args = (aq.reshape(nb, t, -1), ak.reshape(nb, t, -1), av.reshape(nb, t, -1))

    def call(tq, n_tiles, first_tile, key_lo, key_hi, prev):
        qmap = lambda b, kv, i: (b, first_tile + i, kv)
        specs = [pl.BlockSpec((1, tq, grp_w), qmap),
                 pl.BlockSpec((1, t, HEAD_DIM), lambda b, kv, i: (b, 0, kv)),
                 pl.BlockSpec((1, t, 2 * HEAD_DIM), lambda b, kv, i: (b, 0, kv))]
        extra, alias = (), {}
        if prev is not None:
            specs.append(pl.BlockSpec(memory_space=pl.ANY))
            extra, alias = (prev,), {3: 0}
        return pl.pallas_call(
            functools.partial(_attn_kernel, key_lo=key_lo, key_hi=key_hi, kc=ATT_KC),
            grid=(nb, ATT_KV_HEADS, n_tiles),
            in_specs=specs,
            out_specs=pl.BlockSpec((1, tq, grp_w), qmap),
            out_shape=jax.ShapeDtypeStruct((nb, t, ATT_HEADS * HEAD_DIM), F32),
            input_output_aliases=alias,
            compiler_params=_cp(("parallel", "parallel", "arbitrary")),
            name="gqa_attention",
        )(*args, *extra)

    tq = math.gcd(ATT_TQ, lat_len)
    out = call(tq, lat_len // tq, 0, 0, t, None)
    out = call(lc, 1, t // lc - 1, lat_len, t, out)
    return out.reshape(m, -1)


def _ret_kernel(q_ref, k_ref, v_ref, g_ref, intra_ref, qd_ref, kd_ref, cd_ref, o_ref, st_ref,
                *, n_lat, n_ctx):
    c_len = RET_CHUNK
    n = n_lat + n_ctx

    def rows(c):
        return pl.ds(pl.multiple_of(c * c_len, c_len), c_len)

    def kv(d, c):
        return _dot_tn((k_ref[0, rows(c), :] * kd_ref[d, 0]).astype(BF16), v_ref[0, rows(c), :].astype(BF16))

    def scan_step(t, carry):
        sf, sb = carry
        cf = jnp.where(t < n_ctx, n_lat + t, t - n_ctx)
        cb = jnp.where(t < n_ctx, n - 1 - t, n_lat - 1 - (t - n_ctx))
        st_ref[cf, :RET_D, :] = sf.astype(BF16)
        st_ref[cb, RET_D:, :] = sb.astype(BF16)
        return (cd_ref[0, 0, 0:1, :] * sf + kv(0, cf), cd_ref[1, 0, 0:1, :] * sb + kv(1, cb))

    zero = jnp.zeros((RET_D, RET_D), F32)
    lax.fori_loop(0, n, scan_step, (zero, zero), unroll=RET_UNROLL)

    mask = intra_ref[0, 0] + intra_ref[1, 0]

    def out_step(c, carry):
        qc = q_ref[0, rows(c), :]
        vc = v_ref[0, rows(c), :].astype(BF16)
        s = _dot_nt(qc.astype(BF16), k_ref[0, rows(c), :].astype(BF16)) * mask
        qdec = jnp.concatenate([qc * qd_ref[0, 0], qc * qd_ref[1, 0]], axis=1).astype(BF16)
        o = _dot(s.astype(BF16), vc) + _dot(qdec, st_ref[c])
        mu = jnp.mean(o, axis=-1, keepdims=True)
        oc = o - mu
        var = jnp.mean(oc * oc, axis=-1, keepdims=True)
        g = g_ref[0, rows(c), :]
        o_ref[0, rows(c), :] = (g * jax.nn.sigmoid(g)) * (oc * lax.rsqrt(var + LN_EPS))
        return carry

    lax.fori_loop(0, n, out_step, 0, unroll=RET_UNROLL)


def _retention(rq, rk, z, tabs, nb, lat_len):
    m = rq.shape[0]
    t = m // nb
    intra, qd, kd, cd = tabs
    v_blk, g_blk = 0, RET_HEADS
    kern = functools.partial(_ret_kernel, n_lat=lat_len // RET_CHUNK, n_ctx=(t - lat_len) // RET_CHUNK)
    seq = lambda off: pl.BlockSpec((1, t, LANE), lambda b, h: (b, 0, off + h))
    tab = lambda r: pl.BlockSpec((2, 1, r, LANE), lambda b, h: (0, h, 0, 0))
    out = pl.pallas_call(
        kern,
        grid=(nb, RET_HEADS),
        in_specs=[seq(0), seq(0), seq(v_blk), seq(g_blk),
                  tab(RET_CHUNK), tab(RET_CHUNK), tab(RET_CHUNK), tab(8)],
        out_specs=seq(0),
        out_shape=jax.ShapeDtypeStruct((nb, t, RET_HEADS * RET_D), F32),
        scratch_shapes=[pltpu.VMEM((t // RET_CHUNK, 2 * RET_D, RET_D), BF16)],
        compiler_params=_cp(("parallel", "parallel")),
        name="retention",
    )(rq.reshape(nb, t, -1), rk.reshape(nb, t, -1), z.reshape(nb, t, -1), z.reshape(nb, t, -1), intra, qd, kd, cd)
    return out.reshape(m, -1)


def _retention_tables(log_gamma):
    c = RET_CHUNK
    i = jnp.arange(c, dtype=F32)
    lg = log_gamma.astype(F32)[:, :, None]
    rel = i[:, None] - i[None, :]
    intra_f = jnp.where(rel >= 0, jnp.exp(lg[0][..., None] * jnp.maximum(rel, 0.0)), 0.0)
    intra_b = jnp.where(rel <= 0, jnp.exp(lg[1][..., None] * jnp.maximum(-rel, 0.0)), 0.0)
    qd = jnp.stack([jnp.exp(lg[0] * (i + 1.0)), jnp.exp(lg[1] * (c - i))])
    kd = jnp.stack([jnp.exp(lg[0] * (c - 1.0 - i)), jnp.exp(lg[1] * i)])
    cd = jnp.exp(lg * c)
    bc = lambda a: jnp.broadcast_to(a[..., None], a.shape + (LANE,))
    return (jnp.stack([intra_f, intra_b]), bc(qd), bc(kd),
            jnp.broadcast_to(cd[..., None], (2, RET_HEADS, 8, LANE)))


def _shift_rows(x, lat_len):
    t = x.shape[0]
    row = lax.broadcasted_iota(jnp.int32, x.shape, 0)
    xm = jnp.where((row == 0) | (row == lat_len), 0.0, pltpu.roll(x, 1, 0))
    xp = jnp.where((row == lat_len - 1) | (row == t - 1), 0.0, pltpu.roll(x, t - 1, 0))
    return xm, xp


def _shortconv_kernel(bg_ref, cg_ref, xv_ref, w_ref, b_ref, o_ref, *, lat_len):
    p = cg_ref[0] * xv_ref[0]
    pm, pp = _shift_rows(p, lat_len)
    o_ref[0] = bg_ref[0] * (pm * w_ref[0:1, :] + p * w_ref[1:2, :] + pp * w_ref[2:3, :] + b_ref[...])


def _shortconv(z, w, b, nb, lat_len):
    m = z.shape[0]
    t = m // nb
    cw = CONV_TC
    nct = BRANCH_W // cw
    kern = functools.partial(_shortconv_kernel, lat_len=lat_len)
    seq = lambda off: pl.BlockSpec((1, t, cw), lambda bb, j: (bb, 0, off + j))
    out = pl.pallas_call(
        kern,
        grid=(nb, nct),
        in_specs=[seq(0), seq(nct), seq(2 * nct),
                  pl.BlockSpec((3, cw), lambda bb, j: (0, j)),
                  pl.BlockSpec((1, cw), lambda bb, j: (0, j))],
        out_specs=seq(0),
        out_shape=jax.ShapeDtypeStruct((nb, t, BRANCH_W), F32),
        compiler_params=_cp(("parallel", "parallel")),
        name="short_conv",
    )(z.reshape(nb, t, -1), z.reshape(nb, t, -1), z.reshape(nb, t, -1), w, b.reshape(1, -1))
    return out.reshape(m, -1)


def _hyconv_kernel(z_ref, w_ref, b_ref, ol_ref, oc_ref, *, lat_len):
    x = z_ref[0]
    xm, xp = _shift_rows(x, lat_len)
    y = xm * w_ref[0:1, :] + x * w_ref[1:2, :] + xp * w_ref[2:3, :] + b_ref[...]
    ol_ref[0, 0] = y[:lat_len]
    oc_ref[0, 0] = y[lat_len:]


def _hyconv(z, w, b, nb, lat_len):
    m = z.shape[0]
    t = m // nb
    cw = 2 * CONV_TC
    nct = BRANCH_W // cw
    kern = functools.partial(_hyconv_kernel, lat_len=lat_len)
    return pl.pallas_call(
        kern,
        grid=(nb, 3 * nct),
        in_specs=[pl.BlockSpec((1, t, cw), lambda bb, j: (bb, 0, j)),
                  pl.BlockSpec((3, cw), lambda bb, j: (0, j)),
                  pl.BlockSpec((1, cw), lambda bb, j: (0, j))],
        out_specs=[pl.BlockSpec((1, 1, lat_len, cw), lambda bb, j: (j // nct, bb, 0, j % nct)),
                   pl.BlockSpec((1, 1, t - lat_len, cw), lambda bb, j: (j // nct, bb, 0, j % nct))],
        out_shape=[jax.ShapeDtypeStruct((3, nb, lat_len, BRANCH_W), F32),
                   jax.ShapeDtypeStruct((3, nb, t - lat_len, BRANCH_W), F32)],
        compiler_params=_cp(("parallel", "parallel")),
        name="hyena_dwconv",
    )(z.reshape(nb, t, -1), w, b.reshape(1, -1))


def _stack_c(m):
    return np.block([[m.real, -m.imag], [m.imag, m.real]])


def _split_np(m):
    m32 = np.asarray(m, np.float32)
    hi = m32.astype(BF16)
    lo = (m32 - hi.astype(np.float32)).astype(BF16)
    return np.concatenate([hi, lo], axis=-2)


@functools.lru_cache(maxsize=None)
def _dft_tables(lat_len):
    n = 2 * lat_len
    n1, n2 = FFT_N1, 2 * lat_len // FFT_N1
    a1 = np.arange(n1)
    a2 = np.arange(n2)
    w1 = np.exp(-2j * np.pi * np.outer(a1, a1[:n1 // 2]) / n1)
    f1 = _stack_c(w1)
    w2 = np.exp(-2j * np.pi * np.outer(a2, a2) / n2)
    tw = np.exp(-2j * np.pi * np.outer(a1, a2) / n)
    cf = w2[None, :, :] * tw[:, None, :]
    f2 = np.stack([_stack_c(cf[k]) for k in range(n1)])
    g2 = np.stack([_stack_c(cf[k].conj().T) for k in range(n1)])
    w1i = np.exp(2j * np.pi * np.outer(a1[:n1 // 2], a1) / n1) / n
    f3 = _stack_c(w1i)
    f1_real = f1[:, :n1 // 2]
    return _split_np(f1), _split_np(f1_real), _split_np(f2), _split_np(g2), _split_np(f3)


@functools.lru_cache(maxsize=None)
def _dft_tables_small(ctx_len):
    n = 2 * ctx_len
    a = np.arange(n)
    ff = _stack_c(np.exp(-2j * np.pi * np.outer(a, a[:ctx_len]) / n))
    fi = _stack_c(np.exp(2j * np.pi * np.outer(a[:ctx_len], a) / n) / n)
    return _split_np(ff), _split_np(ff[:, :ctx_len]), _split_np(fi)


def _fft1_kernel(x_ref, f_ref, o_ref, *, passes):
    f = f_ref[...]
    for j in range(x_ref.shape[3]):
        parts = [x_ref[0, b, :, j, :] for b in range(x_ref.shape[1])]
        r = _split_dot(f, parts[0] if len(parts) == 1 else jnp.concatenate(parts, axis=0), passes)
        h = r.shape[0] // 2
        o_ref[0, 0, :, j, :] = r[:h]
        o_ref[0, 1, :, j, :] = r[h:]


def _fft1(x, sel, f1, jb, passes):
    _, p, mem, r, n2, w = x.shape
    f1 = _dft_operand(f1, passes)
    n1 = f1.shape[0] // (2 if passes == 1 else 4)
    return pl.pallas_call(
        functools.partial(_fft1_kernel, passes=passes),
        grid=(p, n2 // jb),
        in_specs=[pl.BlockSpec((None, 1, mem, r, jb, w), lambda pp, j: (sel, pp, 0, 0, j, 0)),
                  pl.BlockSpec(f1.shape, lambda pp, j: (0, 0))],
        out_specs=pl.BlockSpec((1, 2, n1, jb, w), lambda pp, j: (pp, 0, 0, j, 0)),
        out_shape=jax.ShapeDtypeStruct((p, 2, n1, n2, w), F32),
        compiler_params=_cp(("parallel", "parallel")),
        name="hyena_fft_stage1",
    )(x, f1)


def _fft2f_kernel(a_ref, f_ref, o_ref, *, kb, n2, passes):
    w = a_ref.shape[-1]
    for t in range(kb):
        xf = _split_dot(f_ref[t], a_ref[0, :, t].reshape(2 * n2, w), passes)
        xb = _split_dot(f_ref[t], a_ref[1, :, t].reshape(2 * n2, w), passes)
        o_ref[0, 0, t] = xf[:n2] + xb[:n2]
        o_ref[0, 1, t] = xf[n2:] - xb[n2:]


def _fft2_filters(a, f2, kb, passes):
    q, _, n1, n2, w = a.shape
    f2 = _dft_operand(f2, passes)
    return pl.pallas_call(
        functools.partial(_fft2f_kernel, kb=kb, n2=n2, passes=passes),
        grid=(q // 2, n1 // kb),
        in_specs=[pl.BlockSpec((2, 2, kb, n2, w), lambda o, k: (o, 0, k, 0, 0)),
                  pl.BlockSpec((kb,) + f2.shape[1:], lambda o, k: (k, 0, 0))],
        out_specs=pl.BlockSpec((1, 2, kb, n2, w), lambda o, k: (o, 0, k, 0, 0)),
        out_shape=jax.ShapeDtypeStruct((q // 2, 2, n1, n2, w), F32),
        compiler_params=_cp(("parallel", "parallel")),
        name="hyena_filter_spectrum",
    )(a, f2)


def _fft2_kernel(a_ref, h_ref, f_ref, g_ref, o_ref, *, kb, n2, passes):
    w = a_ref.shape[-1]
    for t in range(kb):
        a = a_ref[0, :, t].reshape(2 * n2, w)
        x = _split_dot(f_ref[t], a, passes)
        xr, xi = x[:n2], x[n2:]
        hr, hi = h_ref[0, t], h_ref[1, t]
        y = jnp.concatenate([xr * hr - xi * hi, xr * hi + xi * hr], axis=0)
        o_ref[0, :, t] = _split_dot(g_ref[t], y, passes).reshape(2, n2, w)


def _fft2(a, hspec, order, f2, g2, kb, passes):
    p, _, n1, n2, w = a.shape
    f2, g2 = _dft_operand(f2, passes), _dft_operand(g2, passes)
    kern = functools.partial(_fft2_kernel, kb=kb, n2=n2, passes=passes)
    return pl.pallas_call(
        kern,
        grid=(n1 // kb, p),
        in_specs=[pl.BlockSpec((1, 2, kb, n2, w), lambda k, pp: (pp, 0, k, 0, 0)),
                  pl.BlockSpec((None, 2, kb, n2, w), lambda k, pp: (order, 0, k, 0, 0)),
                  pl.BlockSpec((kb,) + f2.shape[1:], lambda k, pp: (k, 0, 0)),
                  pl.BlockSpec((kb,) + g2.shape[1:], lambda k, pp: (k, 0, 0))],
        out_specs=pl.BlockSpec((1, 2, kb, n2, w), lambda k, pp: (pp, 0, k, 0, 0)),
        out_shape=jax.ShapeDtypeStruct(a.shape, F32),
        compiler_params=_cp(("parallel", "arbitrary")),
        name="hyena_fft_stage2",
    )(a, hspec, f2, g2)


def _fft3_kernel(b_ref, f_ref, gate_ref, prev_ref, skip_ref, o_ref, *, passes):
    f = f_ref[...]
    r = gate_ref.shape[2]
    for j in range(b_ref.shape[3]):
        bj = jnp.concatenate([b_ref[0, 0, :, j, :], b_ref[0, 1, :, j, :]], axis=0)
        conv = _split_dot(f, bj, passes)
        for mem in range(2):
            cm = conv[mem * r:(mem + 1) * r]
            o_ref[0, mem, :, j, :] = gate_ref[0, mem, :, j, :] * (cm + skip_ref[...] * prev_ref[0, mem, :, j, :])


def _fft3(bm, f3, gate, gate_sel, prev, prev_sel, skip, out_rows, jb, passes):
    p, _, n1, n2, w = bm.shape
    r = n1 // 2
    f3 = _dft_operand(f3, passes)
    seq = lambda s: pl.BlockSpec((None, 1, 2, r, jb, w), lambda pp, j: (s, pp, 0, 0, j, 0))
    return pl.pallas_call(
        functools.partial(_fft3_kernel, passes=passes),
        grid=(p, n2 // jb),
        in_specs=[pl.BlockSpec((1, 2, n1, jb, w), lambda pp, j: (pp, 0, 0, j, 0)),
                  pl.BlockSpec(f3.shape, lambda pp, j: (0, 0)),
                  seq(gate_sel), seq(prev_sel),
                  pl.BlockSpec((1, w), lambda pp, j: (0, 0))],
        out_specs=pl.BlockSpec((1, 2, r, jb, w), lambda pp, j: (pp, 0, 0, j, 0)),
        out_shape=jax.ShapeDtypeStruct((p, 2, out_rows, n2, w), F32),
        compiler_params=_cp(("parallel", "parallel")),
        name="hyena_fft_stage3",
    )(bm, f3, gate, prev, skip)


def _hyctx_kernel(v_ref, x1_ref, x2_ref, filt_ref, ff_ref, ffr_ref, fi_ref, skip_ref, yh_hbm_ref, o_ref,
                  *, passes, filter_passes):
    del yh_hbm_ref
    lc, w = v_ref.shape[2], v_ref.shape[3]
    n = 2 * lc
    y = v_ref[0].reshape(2 * lc, w)
    for o, gate_ref in enumerate((x1_ref, x2_ref)):
        sf = _split_dot(ffr_ref[...], filt_ref[o, 0], filter_passes)
        sb = _split_dot(ffr_ref[...], filt_ref[o, 1], filter_passes)
        hr, hi = sf[:n] + sb[:n], sf[n:] - sb[n:]
        x = _split_dot(ff_ref[...], y, passes)
        xr, xi = x[:n], x[n:]
        spec = jnp.concatenate([xr * hr - xi * hi, xr * hi + xi * hr], axis=0)
        conv = _split_dot(fi_ref[...], spec, passes)
        y = gate_ref[0].reshape(2 * lc, w) * (conv + skip_ref[o:o + 1, :] * y)
    o_ref[0] = y.reshape(2, lc, w)


def _hyena_ctx(hc, filt_c, ff, ffr, fi, skip, yh, passes, filter_passes):
    _, nb, lc, w = hc.shape
    p = nb // 2
    t = yh.shape[2]
    ff, fi, ffr = _dft_operand(ff, passes), _dft_operand(fi, passes), _dft_operand(ffr, filter_passes)
    part = pl.BlockSpec((1, 2, lc, w), lambda pp: (pp, 0, 0, 0))
    hc4 = hc.reshape(3, p, 2, lc, w)
    return pl.pallas_call(
        functools.partial(_hyctx_kernel, passes=passes, filter_passes=filter_passes),
        grid=(p,),
        in_specs=[part, part, part,
                  pl.BlockSpec(filt_c.shape, lambda pp: (0, 0, 0, 0)),
                  pl.BlockSpec(ff.shape, lambda pp: (0, 0)),
                  pl.BlockSpec(ffr.shape, lambda pp: (0, 0)),
                  pl.BlockSpec(fi.shape, lambda pp: (0, 0)),
                  pl.BlockSpec(skip.shape, lambda pp: (0, 0)),
                  pl.BlockSpec(memory_space=pl.ANY)],
        out_specs=pl.BlockSpec((1, 2, lc, w), lambda pp: (pp, 0, t // lc - 1, 0)),
        out_shape=jax.ShapeDtypeStruct(yh.shape, F32),
        input_output_aliases={8: 0},
        compiler_params=_cp(("parallel",)),
        name="hyena_ctx",
    )(hc4[0], hc4[1], hc4[2], filt_c, ff, ffr, fi, skip, yh)


def _hyena_filters(l, w1, b1, w2, b2, w3, freq):
    t = jnp.linspace(0.0, 1.0, l, dtype=F32)[:, None]
    bands = (HY_EMB - 1) // 2
    w = 2.0 * math.pi * jnp.arange(l, dtype=F32) / l
    f = jnp.linspace(1e-4, bands - 1, bands, dtype=F32)
    ang = w[:, None] * f[None, :]
    feats = jnp.concatenate([t, jnp.cos(ang), -jnp.sin(ang)], axis=-1)
    hdn = jnp.sin(freq[0] * (feats @ w1 + b1))
    hdn = jnp.sin(freq[1] * (hdn @ w2 + b2))
    n_out = w3.shape[-1]
    deltas = jnp.linspace(math.log(HY_TARGET) / HY_SLOW_PCT, math.log(HY_TARGET) / HY_FAST_PCT, n_out, dtype=F32)
    decay = jnp.exp(-t * jnp.abs(deltas))
    pos = jnp.arange(l)[:, None]
    parts = []
    for q in range(n_out // BRANCH_W):
        sl = slice(q * BRANCH_W, (q + 1) * BRANCH_W)
        f = (hdn @ w3[:, sl]).astype(F32) * decay[:, sl]
        parts.append(jnp.where(pos == 0, 0.0, f) if q % 2 else f)
    return jnp.stack(parts).reshape(n_out // (2 * BRANCH_W), 2, l, BRANCH_W)


def _hyena(z, conv_w, conv_b, filt_w, skip, nb, lat_len):
    m = z.shape[0]
    t = m // nb
    lc = t - lat_len
    w = BRANCH_W
    p = nb // 2
    n1, n2 = FFT_N1, 2 * lat_len // FFT_N1
    r = n1 // 2
    hl, hc = _hyconv(z, conv_w, conv_b, nb, lat_len)
    f1, f1r, f2, g2, f3 = _dft_tables(lat_len)
    ff, ffr, fi = _dft_tables_small(lc)
    jb, kb = min(16, n2), 16
    dp, fp = DFT_PASSES, DFT_FILTER_PASSES
    filt_l = _hyena_filters(lat_len, *filt_w)
    fa = _fft1(filt_l.reshape(1, 4, 1, r, n2, w), 0, f1r, jb, fp)
    hs_l = _fft2_filters(fa, f2, kb, fp)
    hl6 = hl.reshape(3, p, 2, r, n2, w)
    y, y_sel = hl6, 0
    for o in range(2):
        a = _fft1(y, y_sel, f1, jb, dp)
        bm = _fft2(a, hs_l, o, f2, g2, kb, dp)
        rows = r if o == 0 else t // n2
        y = _fft3(bm, f3, hl6, 1 + o, y, y_sel, skip[o:o + 1], rows, jb, dp)[None]
        y_sel = 0
    yh = y.reshape(p, 2, t, w)
    yh = _hyena_ctx(hc, _hyena_filters(lc, *filt_w), ff, ffr, fi, skip, yh, dp, fp)
    return yh.reshape(m, w)


def _merge_kernel(x_ref, mod_ref, ya_ref, yr_ref, yh_ref, yd_ref, wg_ref, bg_ref, wb_ref, wo_ref, g_ref, b_ref,
                  o_ref, *, alpha):
    x = x_ref[...]
    u = _modulate(x, mod_ref, 0, 1).astype(BF16)
    acc = None
    for i, y_ref in enumerate((ya_ref, yr_ref, yh_ref, yd_ref)):
        gate = jax.nn.sigmoid(_dot(u, wg_ref[i]) + bg_ref[i:i + 1, :])
        term = gate * _dot(y_ref[...].astype(BF16), wb_ref[i])
        acc = term if acc is None else acc + term
    mix = _dot(acc.astype(BF16), wo_ref[...])
    o_ref[...] = _gated_residual_ln(x, mix, mod_ref, 2, alpha, g_ref[...], b_ref[...])


def _merge(x, modx, ys, wg, bg, wb, wo, layer, g, b, alpha, tm):
    m, d = x.shape
    w = BRANCH_W
    row = lambda i: (i, 0)
    const2 = lambda i: (0, 0)
    one = pl.Buffered(1)
    stacked = lambda a: pl.BlockSpec((None,) + a.shape[1:], lambda i: (layer,) + (0,) * (a.ndim - 1),
                                     pipeline_mode=one)
    return pl.pallas_call(
        functools.partial(_merge_kernel, alpha=alpha),
        grid=(m // tm,),
        in_specs=[pl.BlockSpec((tm, d), row),
                  pl.BlockSpec((tm // SUB, 8, d), lambda i: (i, 0, 0)),
                  pl.BlockSpec((tm, w), row), pl.BlockSpec((tm, w), row),
                  pl.BlockSpec((tm, w), row), pl.BlockSpec((tm, w), row),
                  stacked(wg), pl.BlockSpec(bg.shape, const2), stacked(wb), stacked(wo),
                  pl.BlockSpec((1, d), const2), pl.BlockSpec((1, d), const2)],
        out_specs=pl.BlockSpec((tm, d), row),
        out_shape=jax.ShapeDtypeStruct((m, d), F32),
        compiler_params=_cp(("parallel",)),
        name="branch_merge",
    )(x, modx, *ys, wg, bg, wb, wo, g.reshape(1, d), b.reshape(1, d))


def _ffn_kernel(x_ref, mod_ref, w1_ref, w3_ref, w2_ref, g_ref, b_ref, o_ref, *, alpha, tf):
    x = x_ref[...]
    u = _modulate(x, mod_ref, 3, 4).astype(BF16)
    acc = None
    for j in range(w1_ref.shape[1] // tf):
        sl = slice(j * tf, (j + 1) * tf)
        h1 = _dot(u, w1_ref[:, sl])
        h3 = _dot(u, w3_ref[:, sl])
        part = _dot((h1 * jax.nn.sigmoid(h1) * h3).astype(BF16), w2_ref[sl, :])
        acc = part if acc is None else acc + part
    o_ref[...] = _gated_residual_ln(x, acc, mod_ref, 5, alpha, g_ref[...], b_ref[...])


def _ffn(x, modx, w1, w3, w2, layer, g, b, alpha, tm, tf):
    m, d = x.shape
    resident = lambda a: pl.BlockSpec((None,) + a.shape[1:], lambda i: (layer, 0, 0), pipeline_mode=pl.Buffered(1))
    return pl.pallas_call(
        functools.partial(_ffn_kernel, alpha=alpha, tf=tf),
        grid=(m // tm,),
        in_specs=[pl.BlockSpec((tm, d), lambda i: (i, 0)),
                  pl.BlockSpec((tm // SUB, 8, d), lambda i: (i, 0, 0)),
                  resident(w1), resident(w3), resident(w2),
                  pl.BlockSpec((1, d), lambda i: (0, 0)),
                  pl.BlockSpec((1, d), lambda i: (0, 0))],
        out_specs=pl.BlockSpec((tm, d), lambda i: (i, 0)),
        out_shape=jax.ShapeDtypeStruct((m, d), F32),
        compiler_params=_cp(("parallel",)),
        name="dense_swiglu",
    )(x, modx, w1, w3, w2, g.reshape(1, d), b.reshape(1, d))


def _route_kernel(x_ref, mod_ref, wr_ref, u_ref, lg_ref):
    u = _modulate(x_ref[...], mod_ref, 3, 4)
    lg_ref[...] = jnp.dot(u, wr_ref[...], preferred_element_type=F32, precision=lax.Precision.HIGHEST)
    half = u.shape[1] // 2
    bits = lax.bitcast_convert_type(u.astype(BF16).astype(F32), jnp.uint32)
    u_ref[...] = lax.bitcast_convert_type(bits[:, :half] | (bits[:, half:] >> 16), F32)


def _route(x, modx, wr_pad, tm):
    m, d = x.shape
    return pl.pallas_call(
        _route_kernel,
        grid=(m // tm,),
        in_specs=[pl.BlockSpec((tm, d), lambda i: (i, 0)),
                  pl.BlockSpec((tm // SUB, 8, d), lambda i: (i, 0, 0)),
                  pl.BlockSpec((d, LANE), lambda i: (0, 0))],
        out_specs=[pl.BlockSpec((tm, d // 2), lambda i: (i, 0)), pl.BlockSpec((tm, LANE), lambda i: (i, 0))],
        out_shape=[jax.ShapeDtypeStruct((m, d // 2), F32), jax.ShapeDtypeStruct((m, LANE), F32)],
        compiler_params=_cp(("parallel",)),
        name="moe_router",
    )(x, modx, wr_pad)


def _moe_kernel(be_ref, nu_ref, xb_ref, w1_ref, w3_ref, w2_ref, o_ref, acc_ref, u_ref):
    i, j = pl.program_id(0), pl.program_id(1)
    live = i < nu_ref[0]

    @pl.when(j == 0)
    def _():
        acc_ref[...] = jnp.zeros_like(acc_ref)
        words = lax.bitcast_convert_type(xb_ref[...], jnp.uint32)
        half = words.shape[1]
        hi = lax.bitcast_convert_type(words & jnp.uint32(0xFFFF0000), F32)
        lo = lax.bitcast_convert_type(words << 16, F32)
        u_ref[:, :half] = hi.astype(BF16)
        u_ref[:, half:] = lo.astype(BF16)

    @pl.when(live)
    def _():
        u = u_ref[...]
        h1 = _dot(u, w1_ref[0])
        h3 = _dot(u, w3_ref[0])
        acc_ref[...] += _dot((h1 * jax.nn.sigmoid(h1) * h3).astype(BF16), w2_ref[0])

    @pl.when(j == pl.num_programs(1) - 1)
    def _():
        o_ref[...] = acc_ref[...]


def _moe_experts(xb, blk_e, n_used, w1, w3, w2, layer, tm, tf):
    r = xb.shape[0]
    d = w1.shape[2]
    ff = w1.shape[3]
    nj = ff // tf

    def wmap(i, j, be, nu):
        return (layer, be[i], 0, jnp.where(i < nu[0], j, nj - 1))

    def w2map(i, j, be, nu):
        return (layer, be[i], jnp.where(i < nu[0], j, nj - 1), 0)

    grid_spec = pltpu.PrefetchScalarGridSpec(
        num_scalar_prefetch=2,
        grid=(r // tm, nj),
        in_specs=[pl.BlockSpec((tm, d // 2), lambda i, j, be, nu: (i, 0)),
                  pl.BlockSpec((None, 1, d, tf), wmap),
                  pl.BlockSpec((None, 1, d, tf), wmap),
                  pl.BlockSpec((None, 1, tf, d), w2map)],
        out_specs=pl.BlockSpec((tm, d), lambda i, j, be, nu: (i, 0)),
        scratch_shapes=[pltpu.VMEM((tm, d), F32), pltpu.VMEM((tm, d), BF16)],
    )
    return pl.pallas_call(
        _moe_kernel,
        grid_spec=grid_spec,
        out_shape=jax.ShapeDtypeStruct((r, d), F32),
        compiler_params=_cp(("arbitrary", "arbitrary")),
        name="moe_experts",
    )(blk_e, n_used, xb, w1, w3, w2)


def _ln_res_kernel(x_ref, f0_ref, f1_ref, wt_ref, mod_ref, g_ref, b_ref, o_ref, *, alpha):
    f = f0_ref[...] * wt_ref[:, 0:1] + f1_ref[...] * wt_ref[:, 1:2]
    o_ref[...] = _gated_residual_ln(x_ref[...], f, mod_ref, 5, alpha, g_ref[...], b_ref[...])


def _ln_res(x, f01, wts, modx, g, b, alpha, tm):
    m, d = x.shape
    nt = m // tm
    return pl.pallas_call(
        functools.partial(_ln_res_kernel, alpha=alpha),
        grid=(nt,),
        in_specs=[pl.BlockSpec((tm, d), lambda i: (i, 0)),
                  pl.BlockSpec((tm, d), lambda i: (i, 0)),
                  pl.BlockSpec((tm, d), lambda i: (i + nt, 0)),
                  pl.BlockSpec((tm, TOP_K), lambda i: (i, 0)),
                  pl.BlockSpec((tm // SUB, 8, d), lambda i: (i, 0, 0)),
                  pl.BlockSpec((1, d), lambda i: (0, 0)), pl.BlockSpec((1, d), lambda i: (0, 0))],
        out_specs=pl.BlockSpec((tm, d), lambda i: (i, 0)),
        out_shape=jax.ShapeDtypeStruct((m, d), F32),
        compiler_params=_cp(("parallel",)),
        name="moe_residual_ln",
    )(x, f01, f01, wts, modx, g.reshape(1, d), b.reshape(1, d))


def _moe(x, modx, router, w1, w3, w2, layer, g, b, alpha, tm_rows, tm, tf):
    m, d = x.shape
    wr_pad = jnp.zeros((d, LANE), F32).at[:, :N_EXPERTS].set(router)
    u, logits = _route(x, modx, wr_pad, tm_rows)
    top_v, top_i = lax.top_k(logits[:, :N_EXPERTS], TOP_K)
    wts = jax.nn.softmax(top_v, axis=-1)
    e_flat = top_i.reshape(-1).astype(jnp.int32)
    n_pair = m * TOP_K
    onehot = (e_flat[:, None] == jnp.arange(N_EXPERTS, dtype=jnp.int32)[None, :]).astype(jnp.int32)
    csum = jnp.cumsum(onehot, axis=0)
    counts = csum[-1]
    rank = jnp.take_along_axis(csum, e_flat[:, None], axis=1)[:, 0] - 1
    padded = (counts + tm - 1) // tm * tm
    pad_end = jnp.cumsum(padded)
    pad_start = pad_end - padded
    dest = pad_start[e_flat] + rank
    n_blk = -(-n_pair // tm) + N_EXPERTS
    r = n_blk * tm
    tok = jnp.arange(n_pair, dtype=jnp.int32) // TOP_K
    n_used = (pad_end[-1] // tm).astype(jnp.int32)
    blk = jnp.minimum(jnp.arange(n_blk, dtype=jnp.int32), n_used - 1) * tm
    blk_e = jnp.minimum(jnp.searchsorted(pad_end, blk, side='right'), N_EXPERTS - 1).astype(jnp.int32)
    _, tok_sorted = lax.sort_key_val(dest, tok)
    shift = pad_start - (jnp.cumsum(counts) - counts)
    src = jnp.arange(r, dtype=jnp.int32) - jnp.repeat(shift[blk_e], tm).astype(jnp.int32)
    row_tok = jnp.take(tok_sorted, jnp.clip(src, 0, n_pair - 1), mode="clip")
    xb = jnp.take(u, row_tok, axis=0, mode="clip")
    yb = _moe_experts(xb, blk_e, n_used.reshape(1), w1, w3, w2, layer, tm, tf)
    f01 = jnp.take(yb, dest.reshape(m, TOP_K).T.reshape(-1), axis=0, mode="clip")
    return _ln_res(x, f01, wts, modx, g, b, alpha, tm_rows)


def _rope_tables(nb, lat_len, ctx_len):
    rows = lat_len // GRID_W
    row = jnp.repeat(jnp.arange(rows, dtype=F32), GRID_W)
    col = jnp.tile(jnp.arange(GRID_W, dtype=F32), rows)
    n_freq = HEAD_DIM // 4
    inv = ROPE_THETA ** (-jnp.arange(n_freq, dtype=F32) / n_freq)
    ar, ac = row[:, None] * inv, col[:, None] * inv
    cos = jnp.concatenate([jnp.cos(ar), jnp.cos(ar), jnp.cos(ac), jnp.cos(ac)], axis=1)
    sin = jnp.concatenate([-jnp.sin(ar), jnp.sin(ar), -jnp.sin(ac), jnp.sin(ac)], axis=1)
    cos = jnp.concatenate([cos, jnp.ones((ctx_len, HEAD_DIM), F32)], axis=0)
    sin = jnp.concatenate([sin, jnp.zeros((ctx_len, HEAD_DIM), F32)], axis=0)
    return jnp.tile(cos, (nb, 1)), jnp.tile(sin, (nb, 1))


def _row_tile(m, pref):
    tm = pref
    while m % tm:
        tm //= 2
    return tm


def kernel(x, c, ctx, c_ctx, w_mod, b_mod, w_in, conv_a_w, conv_a_b, ret_decay, hy_conv_w, hy_conv_b, hy_w1, hy_b1, hy_w2, hy_b2, hy_w3, hy_freq, hy_skip, q_norm, k_norm, w_gate, b_gate, w_branch, w_o, ln_g, ln_b, ffn_w1, ffn_w3, ffn_w2, router, moe_w1, moe_w3, moe_w2):
    nb, lat_len, d = x.shape
    ctx_len = ctx.shape[1]
    depth = w_mod.shape[0]
    t = lat_len + ctx_len
    m = nb * t
    alpha = float((2 * depth) ** 0.25)
    assert lat_len % SUB == 0 and ctx_len % SUB == 0 and nb % 2 == 0

    act = jnp.concatenate([jax.nn.silu(c), jax.nn.silu(c_ctx)[None, :],
                           jnp.zeros((8 - nb - 1, d), F32)], axis=0)
    mods = _mod_all(act, w_mod, b_mod).reshape(depth, 8, N_MOD, d)
    mods = jnp.pad(mods, ((0, 0), (0, 0), (0, 8 - N_MOD), (0, 0)))
    tiles_per_batch = t // SUB
    tile_pos = np.arange(m // SUB) % tiles_per_batch
    tile_group = np.where(tile_pos < lat_len // SUB, np.arange(m // SUB) // tiles_per_batch, nb)
    cos_t, sin_t = _rope_tables(nb, lat_len, ctx_len)

    cast = lambda a: a.astype(BF16)
    w_in_b, w_gate_b, w_branch_b, w_o_b = cast(w_in), cast(w_gate), cast(w_branch), cast(w_o)
    ffn_b = (cast(ffn_w1), cast(ffn_w3), cast(ffn_w2))
    moe_w = (cast(moe_w1), cast(moe_w3), cast(moe_w2))

    rows = jnp.concatenate([x, ctx], axis=1).reshape(m, d)
    tm = _row_tile(m, 512)
    for l in range(depth):
        modx = mods[l][tile_group]
        za, zvg, zh, rq, rk, aq, ak, av = _proj(rows, modx, w_in_b, l, cos_t, sin_t, q_norm[l], k_norm[l], tm)
        y_d = _attention(aq, ak, av, nb, lat_len)
        log_gamma = -jnp.exp(ret_decay[l].astype(F32))
        y_r = _retention(rq, rk, zvg, _retention_tables(log_gamma), nb, lat_len)
        y_a = _shortconv(za, conv_a_w[l], conv_a_b[l], nb, lat_len)
        y_h = _hyena(zh, hy_conv_w[l], hy_conv_b[l],
                     (hy_w1[l], hy_b1[l], hy_w2[l], hy_b2[l], hy_w3[l], hy_freq[l]), hy_skip[l], nb, lat_len)
        rows = _merge(rows, modx, (y_a, y_r, y_h, y_d), w_gate_b, b_gate[l], w_branch_b, w_o_b, l,
                      ln_g[l, 0], ln_b[l, 0], alpha, tm)
        i = l // 2
        if l % 2 == 0:
            rows = _ffn(rows, modx, *ffn_b, i, ln_g[l, 1], ln_b[l, 1], alpha, tm, 256)
        else:
            rows = _moe(rows, modx, router[i], *moe_w, i, ln_g[l, 1], ln_b[l, 1], alpha, tm, 768, 1792)
    return rows.reshape(nb, t, d)[:, :lat_len]
```

```python
import functools
import math

import numpy as np
import jax
import jax.numpy as jnp
from jax import lax
from jax.experimental import pallas as pl
from jax.experimental.pallas import tpu as pltpu

F32 = jnp.float32
BF16 = jnp.bfloat16

GRID_W = 64
BRANCH_W = 512
RET_HEADS = 4
RET_D = 128
RET_CHUNK = 128
HY_EMB = 33
HY_FAST_PCT = 0.3
HY_SLOW_PCT = 1.5
HY_TARGET = 1e-2
ATT_HEADS = 4
ATT_KV_HEADS = 2
HEAD_DIM = 128
ROPE_THETA = 10000.0
N_EXPERTS = 8
TOP_K = 2
N_MOD = 6
A_W = 3 * BRANCH_W
R_W = 4 * RET_HEADS * RET_D
H_W = 3 * BRANCH_W
D_W = (ATT_HEADS + 2 * ATT_KV_HEADS) * HEAD_DIM
IN_W = A_W + R_W + H_W + D_W
LN_EPS = 1e-6

SUB = 256
LANE = 128
FFT_N1 = 128
RET_UNROLL = 34
CONV_TC = 256
ATT_TQ = 512
ATT_KC = 1536
DFT_PASSES = 1
DFT_FILTER_PASSES = 1
VMEM_LIMIT = 56 * 1024 * 1024


def _cp(sem, vmem=VMEM_LIMIT):
    return pltpu.CompilerParams(dimension_semantics=sem, vmem_limit_bytes=vmem)


def _dot(a, b):
    return jnp.dot(a, b, preferred_element_type=F32)


def _dot_nt(a, b):
    return lax.dot_general(a, b, (((1,), (1,)), ((), ())), preferred_element_type=F32)


def _dot_tn(a, b):
    return lax.dot_general(a, b, (((0,), (0,)), ((), ())), preferred_element_type=F32)


def _split_dot(f_cat, x, passes):
    xh = x.astype(BF16)
    if passes == 1:
        return _dot(f_cat, xh)
    m = f_cat.shape[0] // 2
    r = _dot(f_cat, xh)
    r = r[:m] + r[m:]
    if passes == 3:
        r = r + _dot(f_cat[:m], (x - xh.astype(F32)).astype(BF16))
    return r


def _dft_operand(f_cat, passes):
    return f_cat[..., :f_cat.shape[-2] // 2, :] if passes == 1 else f_cat


def _modulate(x, mod_ref, shift_row, scale_row):
    parts = []
    for s in range(x.shape[0] // SUB):
        sh = mod_ref[s, shift_row:shift_row + 1, :]
        sc = mod_ref[s, scale_row:scale_row + 1, :]
        parts.append(x[s * SUB:(s + 1) * SUB] * (1.0 + sc) + sh)
    return parts[0] if len(parts) == 1 else jnp.concatenate(parts, axis=0)


def _gated_residual_ln(x, f, mod_ref, gate_row, alpha, g, b):
    parts = []
    for s in range(x.shape[0] // SUB):
        gt = mod_ref[s, gate_row:gate_row + 1, :]
        parts.append(alpha * x[s * SUB:(s + 1) * SUB] + gt * f[s * SUB:(s + 1) * SUB])
    r = parts[0] if len(parts) == 1 else jnp.concatenate(parts, axis=0)
    mu = jnp.mean(r, axis=-1, keepdims=True)
    rc = r - mu
    var = jnp.mean(rc * rc, axis=-1, keepdims=True)
    return rc * lax.rsqrt(var + LN_EPS) * g + b


def _mod_kernel(a_ref, w_ref, b_ref, o_ref):
    o_ref[0] = _dot(a_ref[...].astype(BF16), w_ref[0].astype(BF16)) + b_ref[0]


def _mod_all(act, w_mod, b_mod):
    depth, d, n = w_mod.shape
    tn = 512
    return pl.pallas_call(
        _mod_kernel,
        grid=(depth, n // tn),
        in_specs=[pl.BlockSpec((8, d), lambda l, j: (0, 0)),
                  pl.BlockSpec((1, d, tn), lambda l, j: (l, 0, j)),
                  pl.BlockSpec((1, 1, tn), lambda l, j: (l, 0, j))],
        out_specs=pl.BlockSpec((1, 8, tn), lambda l, j: (l, 0, j)),
        out_shape=jax.ShapeDtypeStruct((depth, 8, n), F32),
        compiler_params=_cp(("parallel", "parallel")),
        name="mod_vectors",
    )(act, w_mod, b_mod.reshape(depth, 1, n))


PROJ_TN = 512
assert (A_W, R_W, H_W, D_W) == (3 * PROJ_TN, 4 * PROJ_TN, 3 * PROJ_TN, 2 * PROJ_TN)


def _proj_kernel(x_ref, mod_ref, w_ref, cos_ref, sin_ref, gq_ref, gk_ref,
                 za_ref, zvg_ref, zh_ref, rq_ref, rk_ref, aq_ref, ak_ref, av_ref):
    u = _modulate(x_ref[...], mod_ref, 0, 1).astype(BF16)
    tn = PROJ_TN
    cos, sin = cos_ref[...], sin_ref[...]
    lane = lax.broadcasted_iota(jnp.int32, cos.shape, 1)
    first_half = (lane % 64) < 32

    def tile(j):
        return _dot(u, w_ref[:, j * tn:(j + 1) * tn])

    def rope(x):
        partner = jnp.where(first_half, pltpu.roll(x, 96, 1), pltpu.roll(x, 32, 1))
        return x * cos + partner * sin

    def rms(x, g):
        return x * lax.rsqrt(jnp.mean(x * x, axis=-1, keepdims=True) + LN_EPS) * g

    head = lambda h: slice(h * LANE, (h + 1) * LANE)

    for j in range(3):
        za_ref[:, j * tn:(j + 1) * tn] = tile(j)
    r = tile(3)
    for h in range(RET_HEADS):
        rq_ref[:, head(h)] = rope(r[:, head(h)])
    r = tile(4)
    for h in range(RET_HEADS):
        rk_ref[:, head(h)] = rope(r[:, head(h)] * (RET_D ** -0.5))
    for j in range(2):
        zvg_ref[:, j * tn:(j + 1) * tn] = tile(5 + j)
    for j in range(3):
        zh_ref[:, j * tn:(j + 1) * tn] = tile(7 + j)
    q_scale = (HEAD_DIM ** -0.5) * math.log2(math.e)
    r = tile(10)
    for h in range(ATT_HEADS):
        aq_ref[:, head(h)] = (rope(rms(r[:, head(h)], gq_ref[...])) * q_scale).astype(BF16)
    r = tile(11)
    ones_col = jnp.where(lane == 0, 1.0, 0.0).astype(BF16)
    for h in range(ATT_KV_HEADS):
        ak_ref[:, head(h)] = rope(rms(r[:, head(h)], gk_ref[...])).astype(BF16)
        av_ref[:, head(2 * h)] = r[:, head(ATT_KV_HEADS + h)].astype(BF16)
        av_ref[:, head(2 * h + 1)] = ones_col


def _proj(x, modx, w, layer, cos_t, sin_t, gq, gk, tm):
    m, d = x.shape
    n = w.shape[2]
    tn = PROJ_TN
    qw, kw = ATT_HEADS * HEAD_DIM, ATT_KV_HEADS * HEAD_DIM
    row = lambda i: (i, 0)
    widths = (A_W, 2 * tn, H_W, tn, tn, qw, kw, 2 * kw)
    dtypes = (F32,) * 5 + (BF16,) * 3
    return pl.pallas_call(
        _proj_kernel,
        grid=(m // tm,),
        in_specs=[pl.BlockSpec((tm, d), row),
                  pl.BlockSpec((tm // SUB, 8, d), lambda i: (i, 0, 0)),
                  pl.BlockSpec((None, d, n), lambda i: (layer, 0, 0), pipeline_mode=pl.Buffered(1)),
                  pl.BlockSpec((tm, LANE), row), pl.BlockSpec((tm, LANE), row),
                  pl.BlockSpec((1, LANE), lambda i: (0, 0)), pl.BlockSpec((1, LANE), lambda i: (0, 0))],
        out_specs=[pl.BlockSpec((tm, wd), row) for wd in widths],
        out_shape=[jax.ShapeDtypeStruct((m, wd), dt) for wd, dt in zip(widths, dtypes)],
        compiler_params=_cp(("parallel",)),
        name="in_proj",
    )(x, modx, w, cos_t, sin_t, gq.reshape(1, LANE), gk.reshape(1, LANE))


def _attn_kernel(q_ref, k_ref, v_ref, *rest, key_lo, key_hi, kc):
    o_ref = rest[-1]
    tq = q_ref.shape[1]
    grp = ATT_HEADS // ATT_KV_HEADS
    q2 = jnp.concatenate([q_ref[0, :, g * HEAD_DIM:(g + 1) * HEAD_DIM] for g in range(grp)], axis=0)
    m_run, acc = None, None
    for lo in range(key_lo, key_hi, kc):
        hi = min(lo + kc, key_hi)
        s = _dot_nt(q2, k_ref[0, lo:hi, :])
        m_chunk = jnp.max(s, axis=-1, keepdims=True)
        m_new = m_chunk if m_run is None else jnp.maximum(m_run, m_chunk)
        pv = _dot(jnp.exp2(s - m_new).astype(BF16), v_ref[0, lo:hi, :])
        acc = pv if acc is None else jnp.exp2(m_run - m_new) * acc + pv
        m_run = m_new
    for g in range(grp):
        a = acc[g * tq:(g + 1) * tq]
        o_ref[0, :, g * HEAD_DIM:(g + 1) * HEAD_DIM] = a[:, :HEAD_DIM] / a[:, HEAD_DIM:HEAD_DIM + 1]


def _attention(aq, ak, av, nb, lat_len):
    m = aq.shape[0]
    t = m // nb
    lc = t - lat_len
    grp_w = (ATT_HEADS // ATT_KV_HEADS) * HEAD_DIM
    args = (aq.reshape(nb, t, -1), ak.reshape(nb, t, -1), av.reshape(nb, t, -1))

    def call(tq, n_tiles, first_tile, key_lo, key_hi, prev):
        qmap = lambda b, kv, i: (b, first_tile + i, kv)
        specs = [pl.BlockSpec((1, tq, grp_w), qmap),
                 pl.BlockSpec((1, t, HEAD_DIM), lambda b, kv, i: (b, 0, kv)),
                 pl.BlockSpec((1, t, 2 * HEAD_DIM), lambda b, kv, i: (b, 0, kv))]
        extra, alias = (), {}
        if prev is not None:
            specs.append(pl.BlockSpec(memory_space=pl.ANY))
            extra, alias = (prev,), {3: 0}
        return pl.pallas_call(
            functools.partial(_attn_kernel, key_lo=key_lo, key_hi=key_hi, kc=ATT_KC),
            grid=(nb, ATT_KV_HEADS, n_tiles),
            in_specs=specs,
            out_specs=pl.BlockSpec((1, tq, grp_w), qmap),
            out_shape=jax.ShapeDtypeStruct((nb, t, ATT_HEADS * HEAD_DIM), F32),
            input_output_aliases=alias,
            compiler_params=_cp(("parallel", "parallel", "arbitrary")),
            name="gqa_attention",
        )(*args, *extra)

    tq = math.gcd(ATT_TQ, lat_len)
    out = call(tq, lat_len // tq, 0, 0, t, None)
    out = call(lc, 1, t // lc - 1, lat_len, t, out)
    return out.reshape(m, -1)


def _ret_kernel(q_ref, k_ref, v_ref, g_ref, intra_ref, qd_ref, kd_ref, cd_ref, o_ref, st_ref,
                *, n_lat, n_ctx):
    c_len = RET_CHUNK
    n = n_lat + n_ctx

    def rows(c):
        return pl.ds(pl.multiple_of(c * c_len, c_len), c_len)

    def kv(d, c):
        return _dot_tn((k_ref[0, rows(c), :] * kd_ref[d, 0]).astype(BF16), v_ref[0, rows(c), :].astype(BF16))

    def scan_step(t, carry):
        sf, sb = carry
        cf = jnp.where(t < n_ctx, n_lat + t, t - n_ctx)
        cb = jnp.where(t < n_ctx, n - 1 - t, n_lat - 1 - (t - n_ctx))
        st_ref[cf, :RET_D, :] = sf.astype(BF16)
        st_ref[cb, RET_D:, :] = sb.astype(BF16)
        return (cd_ref[0, 0, 0:1, :] * sf + kv(0, cf), cd_ref[1, 0, 0:1, :] * sb + kv(1, cb))

    zero = jnp.zeros((RET_D, RET_D), F32)
    lax.fori_loop(0, n, scan_step, (zero, zero), unroll=RET_UNROLL)

    mask = intra_ref[0, 0] + intra_ref[1, 0]

    def out_step(c, carry):
        qc = q_ref[0, rows(c), :]
        vc = v_ref[0, rows(c), :].astype(BF16)
        s = _dot_nt(qc.astype(BF16), k_ref[0, rows(c), :].astype(BF16)) * mask
        qdec = jnp.concatenate([qc * qd_ref[0, 0], qc * qd_ref[1, 0]], axis=1).astype(BF16)
        o = _dot(s.astype(BF16), vc) + _dot(qdec, st_ref[c])
        mu = jnp.mean(o, axis=-1, keepdims=True)
        oc = o - mu
        var = jnp.mean(oc * oc, axis=-1, keepdims=True)
        g = g_ref[0, rows(c), :]
        o_ref[0, rows(c), :] = (g * jax.nn.sigmoid(g)) * (oc * lax.rsqrt(var + LN_EPS))
        return carry

    lax.fori_loop(0, n, out_step, 0, unroll=RET_UNROLL)


def _retention(rq, rk, z, tabs, nb, lat_len):
    m = rq.shape[0]
    t = m // nb
    intra, qd, kd, cd = tabs
    v_blk, g_blk = 0, RET_HEADS
    kern = functools.partial(_ret_kernel, n_lat=lat_len // RET_CHUNK, n_ctx=(t - lat_len) // RET_CHUNK)
    seq = lambda off: pl.BlockSpec((1, t, LANE), lambda b, h: (b, 0, off + h))
    tab = lambda r: pl.BlockSpec((2, 1, r, LANE), lambda b, h: (0, h, 0, 0))
    out = pl.pallas_call(
        kern,
        grid=(nb, RET_HEADS),
        in_specs=[seq(0), seq(0), seq(v_blk), seq(g_blk),
                  tab(RET_CHUNK), tab(RET_CHUNK), tab(RET_CHUNK), tab(8)],
        out_specs=seq(0),
        out_shape=jax.ShapeDtypeStruct((nb, t, RET_HEADS * RET_D), F32),
        scratch_shapes=[pltpu.VMEM((t // RET_CHUNK, 2 * RET_D, RET_D), BF16)],
        compiler_params=_cp(("parallel", "parallel")),
        name="retention",
    )(rq.reshape(nb, t, -1), rk.reshape(nb, t, -1), z.reshape(nb, t, -1), z.reshape(nb, t, -1), intra, qd, kd, cd)
    return out.reshape(m, -1)


def _retention_tables(log_gamma):
    c = RET_CHUNK
    i = jnp.arange(c, dtype=F32)
    lg = log_gamma.astype(F32)[:, :, None]
    rel = i[:, None] - i[None, :]
    intra_f = jnp.where(rel >= 0, jnp.exp(lg[0][..., None] * jnp.maximum(rel, 0.0)), 0.0)
    intra_b = jnp.where(rel <= 0, jnp.exp(lg[1][..., None] * jnp.maximum(-rel, 0.0)), 0.0)
    qd = jnp.stack([jnp.exp(lg[0] * (i + 1.0)), jnp.exp(lg[1] * (c - i))])
    kd = jnp.stack([jnp.exp(lg[0] * (c - 1.0 - i)), jnp.exp(lg[1] * i)])
    cd = jnp.exp(lg * c)
    bc = lambda a: jnp.broadcast_to(a[..., None], a.shape + (LANE,))
    return (jnp.stack([intra_f, intra_b]), bc(qd), bc(kd),
            jnp.broadcast_to(cd[..., None], (2, RET_HEADS, 8, LANE)))


def _shift_rows(x, lat_len):
    t = x.shape[0]
    row = lax.broadcasted_iota(jnp.int32, x.shape, 0)
    xm = jnp.where((row == 0) | (row == lat_len), 0.0, pltpu.roll(x, 1, 0))
    xp = jnp.where((row == lat_len - 1) | (row == t - 1), 0.0, pltpu.roll(x, t - 1, 0))
    return xm, xp


def _shortconv_kernel(bg_ref, cg_ref, xv_ref, w_ref, b_ref, o_ref, *, lat_len):
    p = cg_ref[0] * xv_ref[0]
    pm, pp = _shift_rows(p, lat_len)
    o_ref[0] = bg_ref[0] * (pm * w_ref[0:1, :] + p * w_ref[1:2, :] + pp * w_ref[2:3, :] + b_ref[...])


def _shortconv(z, w, b, nb, lat_len):
    m = z.shape[0]
    t = m // nb
    cw = CONV_TC
    nct = BRANCH_W // cw
    kern = functools.partial(_shortconv_kernel, lat_len=lat_len)
    seq = lambda off: pl.BlockSpec((1, t, cw), lambda bb, j: (bb, 0, off + j))
    out = pl.pallas_call(
        kern,
        grid=(nb, nct),
        in_specs=[seq(0), seq(nct), seq(2 * nct),
                  pl.BlockSpec((3, cw), lambda bb, j: (0, j)),
                  pl.BlockSpec((1, cw), lambda bb, j: (0, j))],
        out_specs=seq(0),
        out_shape=jax.ShapeDtypeStruct((nb, t, BRANCH_W), F32),
        compiler_params=_cp(("parallel", "parallel")),
        name="short_conv",
    )(z.reshape(nb, t, -1), z.reshape(nb, t, -1), z.reshape(nb, t, -1), w, b.reshape(1, -1))
    return out.reshape(m, -1)


def _hyconv_kernel(z_ref, w_ref, b_ref, ol_ref, oc_ref, *, lat_len):
    x = z_ref[0]
    xm, xp = _shift_rows(x, lat_len)
    y = xm * w_ref[0:1, :] + x * w_ref[1:2, :] + xp * w_ref[2:3, :] + b_ref[...]
    ol_ref[0, 0] = y[:lat_len]
    oc_ref[0, 0] = y[lat_len:]


def _hyconv(z, w, b, nb, lat_len):
    m = z.shape[0]
    t = m // nb
    cw = 2 * CONV_TC
    nct = BRANCH_W // cw
    kern = functools.partial(_hyconv_kernel, lat_len=lat_len)
    return pl.pallas_call(
        kern,
        grid=(nb, 3 * nct),
        in_specs=[pl.BlockSpec((1, t, cw), lambda bb, j: (bb, 0, j)),
                  pl.BlockSpec((3, cw), lambda bb, j: (0, j)),
                  pl.BlockSpec((1, cw), lambda bb, j: (0, j))],
        out_specs=[pl.BlockSpec((1, 1, lat_len, cw), lambda bb, j: (j // nct, bb, 0, j % nct)),
                   pl.BlockSpec((1, 1, t - lat_len, cw), lambda bb, j: (j // nct, bb, 0, j % nct))],
        out_shape=[jax.ShapeDtypeStruct((3, nb, lat_len, BRANCH_W), F32),
                   jax.ShapeDtypeStruct((3, nb, t - lat_len, BRANCH_W), F32)],
        compiler_params=_cp(("parallel", "parallel")),
        name="hyena_dwconv",
    )(z.reshape(nb, t, -1), w, b.reshape(1, -1))


def _stack_c(m):
    return np.block([[m.real, -m.imag], [m.imag, m.real]])


def _split_np(m):
    m32 = np.asarray(m, np.float32)
    hi = m32.astype(BF16)
    lo = (m32 - hi.astype(np.float32)).astype(BF16)
    return np.concatenate([hi, lo], axis=-2)


@functools.lru_cache(maxsize=None)
def _dft_tables(lat_len):
    n = 2 * lat_len
    n1, n2 = FFT_N1, 2 * lat_len // FFT_N1
    a1 = np.arange(n1)
    a2 = np.arange(n2)
    w1 = np.exp(-2j * np.pi * np.outer(a1, a1[:n1 // 2]) / n1)
    f1 = _stack_c(w1)
    w2 = np.exp(-2j * np.pi * np.outer(a2, a2) / n2)
    tw = np.exp(-2j * np.pi * np.outer(a1, a2) / n)
    cf = w2[None, :, :] * tw[:, None, :]
    f2 = np.stack([_stack_c(cf[k]) for k in range(n1)])
    g2 = np.stack([_stack_c(cf[k].conj().T) for k in range(n1)])
    w1i = np.exp(2j * np.pi * np.outer(a1[:n1 // 2], a1) / n1) / n
    f3 = _stack_c(w1i)
    f1_real = f1[:, :n1 // 2]
    return _split_np(f1), _split_np(f1_real), _split_np(f2), _split_np(g2), _split_np(f3)


@functools.lru_cache(maxsize=None)
def _dft_tables_small(ctx_len):
    n = 2 * ctx_len
    a = np.arange(n)
    ff = _stack_c(np.exp(-2j * np.pi * np.outer(a, a[:ctx_len]) / n))
    fi = _stack_c(np.exp(2j * np.pi * np.outer(a[:ctx_len], a) / n) / n)
    return _split_np(ff), _split_np(ff[:, :ctx_len]), _split_np(fi)


def _fft1_kernel(x_ref, f_ref, o_ref, *, passes):
    f = f_ref[...]
    for j in range(x_ref.shape[3]):
        parts = [x_ref[0, b, :, j, :] for b in range(x_ref.shape[1])]
        r = _split_dot(f, parts[0] if len(parts) == 1 else jnp.concatenate(parts, axis=0), passes)
        h = r.shape[0] // 2
        o_ref[0, 0, :, j, :] = r[:h]
        o_ref[0, 1, :, j, :] = r[h:]


def _fft1(x, sel, f1, jb, passes):
    _, p, mem, r, n2, w = x.shape
    f1 = _dft_operand(f1, passes)
    n1 = f1.shape[0] // (2 if passes == 1 else 4)
    return pl.pallas_call(
        functools.partial(_fft1_kernel, passes=passes),
        grid=(p, n2 // jb),
        in_specs=[pl.BlockSpec((None, 1, mem, r, jb, w), lambda pp, j: (sel, pp, 0, 0, j, 0)),
                  pl.BlockSpec(f1.shape, lambda pp, j: (0, 0))],
        out_specs=pl.BlockSpec((1, 2, n1, jb, w), lambda pp, j: (pp, 0, 0, j, 0)),
        out_shape=jax.ShapeDtypeStruct((p, 2, n1, n2, w), F32),
        compiler_params=_cp(("parallel", "parallel")),
        name="hyena_fft_stage1",
    )(x, f1)


def _fft2f_kernel(a_ref, f_ref, o_ref, *, kb, n2, passes):
    w = a_ref.shape[-1]
    for t in range(kb):
        xf = _split_dot(f_ref[t], a_ref[0, :, t].reshape(2 * n2, w), passes)
        xb = _split_dot(f_ref[t], a_ref[1, :, t].reshape(2 * n2, w), passes)
        o_ref[0, 0, t] = xf[:n2] + xb[:n2]
        o_ref[0, 1, t] = xf[n2:] - xb[n2:]


def _fft2_filters(a, f2, kb, passes):
    q, _, n1, n2, w = a.shape
    f2 = _dft_operand(f2, passes)
    return pl.pallas_call(
        functools.partial(_fft2f_kernel, kb=kb, n2=n2, passes=passes),
        grid=(q // 2, n1 // kb),
        in_specs=[pl.BlockSpec((2, 2, kb, n2, w), lambda o, k: (o, 0, k, 0, 0)),
                  pl.BlockSpec((kb,) + f2.shape[1:], lambda o, k: (k, 0, 0))],
        out_specs=pl.BlockSpec((1, 2, kb, n2, w), lambda o, k: (o, 0, k, 0, 0)),
        out_shape=jax.ShapeDtypeStruct((q // 2, 2, n1, n2, w), F32),
        compiler_params=_cp(("parallel", "parallel")),
        name="hyena_filter_spectrum",
    )(a, f2)


def _fft2_kernel(a_ref, h_ref, f_ref, g_ref, o_ref, *, kb, n2, passes):
    w = a_ref.shape[-1]
    for t in range(kb):
        a = a_ref[0, :, t].reshape(2 * n2, w)
        x = _split_dot(f_ref[t], a, passes)
        xr, xi = x[:n2], x[n2:]
        hr, hi = h_ref[0, t], h_ref[1, t]
        y = jnp.concatenate([xr * hr - xi * hi, xr * hi + xi * hr], axis=0)
        o_ref[0, :, t] = _split_dot(g_ref[t], y, passes).reshape(2, n2, w)


def _fft2(a, hspec, order, f2, g2, kb, passes):
    p, _, n1, n2, w = a.shape
    f2, g2 = _dft_operand(f2, passes), _dft_operand(g2, passes)
    kern = functools.partial(_fft2_kernel, kb=kb, n2=n2, passes=passes)
    return pl.pallas_call(
        kern,
        grid=(n1 // kb, p),
        in_specs=[pl.BlockSpec((1, 2, kb, n2, w), lambda k, pp: (pp, 0, k, 0, 0)),
                  pl.BlockSpec((None, 2, kb, n2, w), lambda k, pp: (order, 0, k, 0, 0)),
                  pl.BlockSpec((kb,) + f2.shape[1:], lambda k, pp: (k, 0, 0)),
                  pl.BlockSpec((kb,) + g2.shape[1:], lambda k, pp: (k, 0, 0))],
        out_specs=pl.BlockSpec((1, 2, kb, n2, w), lambda k, pp: (pp, 0, k, 0, 0)),
        out_shape=jax.ShapeDtypeStruct(a.shape, F32),
        compiler_params=_cp(("parallel", "arbitrary")),
        name="hyena_fft_stage2",
    )(a, hspec, f2, g2)


def _fft3_kernel(b_ref, f_ref, gate_ref, prev_ref, skip_ref, o_ref, *, passes):
    f = f_ref[...]
    r = gate_ref.shape[2]
    for j in range(b_ref.shape[3]):
        bj = jnp.concatenate([b_ref[0, 0, :, j, :], b_ref[0, 1, :, j, :]], axis=0)
        conv = _split_dot(f, bj, passes)
        for mem in range(2):
            cm = conv[mem * r:(mem + 1) * r]
            o_ref[0, mem, :, j, :] = gate_ref[0, mem, :, j, :] * (cm + skip_ref[...] * prev_ref[0, mem, :, j, :])


def _fft3(bm, f3, gate, gate_sel, prev, prev_sel, skip, out_rows, jb, passes):
    p, _, n1, n2, w = bm.shape
    r = n1 // 2
    f3 = _dft_operand(f3, passes)
    seq = lambda s: pl.BlockSpec((None, 1, 2, r, jb, w), lambda pp, j: (s, pp, 0, 0, j, 0))
    return pl.pallas_call(
        functools.partial(_fft3_kernel, passes=passes),
        grid=(p, n2 // jb),
        in_specs=[pl.BlockSpec((1, 2, n1, jb, w), lambda pp, j: (pp, 0, 0, j, 0)),
                  pl.BlockSpec(f3.shape, lambda pp, j: (0, 0)),
                  seq(gate_sel), seq(prev_sel),
                  pl.BlockSpec((1, w), lambda pp, j: (0, 0))],
        out_specs=pl.BlockSpec((1, 2, r, jb, w), lambda pp, j: (pp, 0, 0, j, 0)),
        out_shape=jax.ShapeDtypeStruct((p, 2, out_rows, n2, w), F32),
        compiler_params=_cp(("parallel", "parallel")),
        name="hyena_fft_stage3",
    )(bm, f3, gate, prev, skip)


def _hyctx_kernel(v_ref, x1_ref, x2_ref, filt_ref, ff_ref, ffr_ref, fi_ref, skip_ref, yh_hbm_ref, o_ref,
                  *, passes, filter_passes):
    del yh_hbm_ref
    lc, w = v_ref.shape[2], v_ref.shape[3]
    n = 2 * lc
    y = v_ref[0].reshape(2 * lc, w)
    for o, gate_ref in enumerate((x1_ref, x2_ref)):
        sf = _split_dot(ffr_ref[...], filt_ref[o, 0], filter_passes)
        sb = _split_dot(ffr_ref[...], filt_ref[o, 1], filter_passes)
        hr, hi = sf[:n] + sb[:n], sf[n:] - sb[n:]
        x = _split_dot(ff_ref[...], y, passes)
        xr, xi = x[:n], x[n:]
        spec = jnp.concatenate([xr * hr - xi * hi, xr * hi + xi * hr], axis=0)
        conv = _split_dot(fi_ref[...], spec, passes)
        y = gate_ref[0].reshape(2 * lc, w) * (conv + skip_ref[o:o + 1, :] * y)
    o_ref[0] = y.reshape(2, lc, w)


def _hyena_ctx(hc, filt_c, ff, ffr, fi, skip, yh, passes, filter_passes):
    _, nb, lc, w = hc.shape
    p = nb // 2
    t = yh.shape[2]
    ff, fi, ffr = _dft_operand(ff, passes), _dft_operand(fi, passes), _dft_operand(ffr, filter_passes)
    part = pl.BlockSpec((1, 2, lc, w), lambda pp: (pp, 0, 0, 0))
    hc4 = hc.reshape(3, p, 2, lc, w)
    return pl.pallas_call(
        functools.partial(_hyctx_kernel, passes=passes, filter_passes=filter_passes),
        grid=(p,),
        in_specs=[part, part, part,
                  pl.BlockSpec(filt_c.shape, lambda pp: (0, 0, 0, 0)),
                  pl.BlockSpec(ff.shape, lambda pp: (0, 0)),
                  pl.BlockSpec(ffr.shape, lambda pp: (0, 0)),
                  pl.BlockSpec(fi.shape, lambda pp: (0, 0)),
                  pl.BlockSpec(skip.shape, lambda pp: (0, 0)),
                  pl.BlockSpec(memory_space=pl.ANY)],
        out_specs=pl.BlockSpec((1, 2, lc, w), lambda pp: (pp, 0, t // lc - 1, 0)),
        out_shape=jax.ShapeDtypeStruct(yh.shape, F32),
        input_output_aliases={8: 0},
        compiler_params=_cp(("parallel",)),
        name="hyena_ctx",
    )(hc4[0], hc4[1], hc4[2], filt_c, ff, ffr, fi, skip, yh)


def _hyena_filters(l, w1, b1, w2, b2, w3, freq):
    t = jnp.linspace(0.0, 1.0, l, dtype=F32)[:, None]
    bands = (HY_EMB - 1) // 2
    w = 2.0 * math.pi * jnp.arange(l, dtype=F32) / l
    f = jnp.linspace(1e-4, bands - 1, bands, dtype=F32)
    ang = w[:, None] * f[None, :]
    feats = jnp.concatenate([t, jnp.cos(ang), -jnp.sin(ang)], axis=-1)
    hdn = jnp.sin(freq[0] * (feats @ w1 + b1))
    hdn = jnp.sin(freq[1] * (hdn @ w2 + b2))
    n_out = w3.shape[-1]
    deltas = jnp.linspace(math.log(HY_TARGET) / HY_SLOW_PCT, math.log(HY_TARGET) / HY_FAST_PCT, n_out, dtype=F32)
    decay = jnp.exp(-t * jnp.abs(deltas))
    pos = jnp.arange(l)[:, None]
    parts = []
    for q in range(n_out // BRANCH_W):
        sl = slice(q * BRANCH_W, (q + 1) * BRANCH_W)
        f = (hdn @ w3[:, sl]).astype(F32) * decay[:, sl]
        parts.append(jnp.where(pos == 0, 0.0, f) if q % 2 else f)
    return jnp.stack(parts).reshape(n_out // (2 * BRANCH_W), 2, l, BRANCH_W)


def _hyena_filter_spectra_all(filt_ws, lat_len):
    w = BRANCH_W
    n1, n2 = FFT_N1, 2 * lat_len // FFT_N1
    _, f1r, f2, _, _ = _dft_tables(lat_len)
    jb, kb = min(16, n2), 16
    filt = jnp.stack([_hyena_filters(lat_len, *fw) for fw in filt_ws])
    fa = _fft1(filt.reshape(1, 4 * len(filt_ws), 1, n1 // 2, n2, w), 0, f1r, jb, DFT_FILTER_PASSES)
    return _fft2_filters(fa, f2, kb, DFT_FILTER_PASSES)


def _hyena(z, conv_w, conv_b, filt_w, skip, nb, lat_len, hs_l, order_base):
    m = z.shape[0]
    t = m // nb
    lc = t - lat_len
    w = BRANCH_W
    p = nb // 2
    n1, n2 = FFT_N1, 2 * lat_len // FFT_N1
    r = n1 // 2
    hl, hc = _hyconv(z, conv_w, conv_b, nb, lat_len)
    f1, f1r, f2, g2, f3 = _dft_tables(lat_len)
    ff, ffr, fi = _dft_tables_small(lc)
    jb, kb = min(16, n2), 16
    dp, fp = DFT_PASSES, DFT_FILTER_PASSES
    hl6 = hl.reshape(3, p, 2, r, n2, w)
    y, y_sel = hl6, 0
    for o in range(2):
        a = _fft1(y, y_sel, f1, jb, dp)
        bm = _fft2(a, hs_l, order_base + o, f2, g2, kb, dp)
        rows = r if o == 0 else t // n2
        y = _fft3(bm, f3, hl6, 1 + o, y, y_sel, skip[o:o + 1], rows, jb, dp)[None]
        y_sel = 0
    yh = y.reshape(p, 2, t, w)
    yh = _hyena_ctx(hc, _hyena_filters(lc, *filt_w), ff, ffr, fi, skip, yh, dp, fp)
    return yh.reshape(m, w)


def _merge_kernel(x_ref, mod_ref, ya_ref, yr_ref, yh_ref, yd_ref, wg_ref, bg_ref, wb_ref, wo_ref, g_ref, b_ref,
                  o_ref, *, alpha):
    x = x_ref[...]
    u = _modulate(x, mod_ref, 0, 1).astype(BF16)
    acc = None
    for i, y_ref in enumerate((ya_ref, yr_ref, yh_ref, yd_ref)):
        gate = jax.nn.sigmoid(_dot(u, wg_ref[i]) + bg_ref[i:i + 1, :])
        term = gate * _dot(y_ref[...].astype(BF16), wb_ref[i])
        acc = term if acc is None else acc + term
    mix = _dot(acc.astype(BF16), wo_ref[...])
    o_ref[...] = _gated_residual_ln(x, mix, mod_ref, 2, alpha, g_ref[...], b_ref[...])


def _merge(x, modx, ys, wg, bg, wb, wo, layer, g, b, alpha, tm):
    m, d = x.shape
    w = BRANCH_W
    row = lambda i: (i, 0)
    const2 = lambda i: (0, 0)
    one = pl.Buffered(1)
    stacked = lambda a: pl.BlockSpec((None,) + a.shape[1:], lambda i: (layer,) + (0,) * (a.ndim - 1),
                                     pipeline_mode=one)
    return pl.pallas_call(
        functools.partial(_merge_kernel, alpha=alpha),
        grid=(m // tm,),
        in_specs=[pl.BlockSpec((tm, d), row),
                  pl.BlockSpec((tm // SUB, 8, d), lambda i: (i, 0, 0)),
                  pl.BlockSpec((tm, w), row), pl.BlockSpec((tm, w), row),
                  pl.BlockSpec((tm, w), row), pl.BlockSpec((tm, w), row),
                  stacked(wg), pl.BlockSpec(bg.shape, const2), stacked(wb), stacked(wo),
                  pl.BlockSpec((1, d), const2), pl.BlockSpec((1, d), const2)],
        out_specs=pl.BlockSpec((tm, d), row),
        out_shape=jax.ShapeDtypeStruct((m, d), F32),
        compiler_params=_cp(("parallel",)),
        name="branch_merge",
    )(x, modx, *ys, wg, bg, wb, wo, g.reshape(1, d), b.reshape(1, d))


def _ffn_kernel(x_ref, mod_ref, w1_ref, w3_ref, w2_ref, g_ref, b_ref, o_ref, *, alpha, tf):
    x = x_ref[...]
    u = _modulate(x, mod_ref, 3, 4).astype(BF16)
    acc = None
    for j in range(w1_ref.shape[1] // tf):
        sl = slice(j * tf, (j + 1) * tf)
        h1 = _dot(u, w1_ref[:, sl])
        h3 = _dot(u, w3_ref[:, sl])
        part = _dot((h1 * jax.nn.sigmoid(h1) * h3).astype(BF16), w2_ref[sl, :])
        acc = part if acc is None else acc + part
    o_ref[...] = _gated_residual_ln(x, acc, mod_ref, 5, alpha, g_ref[...], b_ref[...])


def _ffn(x, modx, w1, w3, w2, layer, g, b, alpha, tm, tf):
    m, d = x.shape
    resident = lambda a: pl.BlockSpec((None,) + a.shape[1:], lambda i: (layer, 0, 0), pipeline_mode=pl.Buffered(1))
    return pl.pallas_call(
        functools.partial(_ffn_kernel, alpha=alpha, tf=tf),
        grid=(m // tm,),
        in_specs=[pl.BlockSpec((tm, d), lambda i: (i, 0)),
                  pl.BlockSpec((tm // SUB, 8, d), lambda i: (i, 0, 0)),
                  resident(w1), resident(w3), resident(w2),
                  pl.BlockSpec((1, d), lambda i: (0, 0)),
                  pl.BlockSpec((1, d), lambda i: (0, 0))],
        out_specs=pl.BlockSpec((tm, d), lambda i: (i, 0)),
        out_shape=jax.ShapeDtypeStruct((m, d), F32),
        compiler_params=_cp(("parallel",)),
        name="dense_swiglu",
    )(x, modx, w1, w3, w2, g.reshape(1, d), b.reshape(1, d))


def _route_kernel(x_ref, mod_ref, wr_ref, u_ref, lg_ref):
    u = _modulate(x_ref[...], mod_ref, 3, 4)
    lg_ref[...] = jnp.dot(u, wr_ref[...], preferred_element_type=F32, precision=lax.Precision.HIGHEST)
    half = u.shape[1] // 2
    bits = lax.bitcast_convert_type(u.astype(BF16).astype(F32), jnp.uint32)
    u_ref[...] = lax.bitcast_convert_type(bits[:, :half] | (bits[:, half:] >> 16), F32)


def _route(x, modx, wr_pad, tm):
    m, d = x.shape
    return pl.pallas_call(
        _route_kernel,
        grid=(m // tm,),
        in_specs=[pl.BlockSpec((tm, d), lambda i: (i, 0)),
                  pl.BlockSpec((tm // SUB, 8, d), lambda i: (i, 0, 0)),
                  pl.BlockSpec((d, LANE), lambda i: (0, 0))],
        out_specs=[pl.BlockSpec((tm, d // 2), lambda i: (i, 0)), pl.BlockSpec((tm, LANE), lambda i: (i, 0))],
        out_shape=[jax.ShapeDtypeStruct((m, d // 2), F32), jax.ShapeDtypeStruct((m, LANE), F32)],
        compiler_params=_cp(("parallel",)),
        name="moe_router",
    )(x, modx, wr_pad)


def _moe_kernel(be_ref, nu_ref, xb_ref, w1_ref, w3_ref, w2_ref, o_ref, acc_ref, u_ref):
    i, j = pl.program_id(0), pl.program_id(1)
    live = i < nu_ref[0]

    @pl.when(j == 0)
    def _():
        acc_ref[...] = jnp.zeros_like(acc_ref)
        words = lax.bitcast_convert_type(xb_ref[...], jnp.uint32)
        half = words.shape[1]
        hi = lax.bitcast_convert_type(words & jnp.uint32(0xFFFF0000), F32)
        lo = lax.bitcast_convert_type(words << 16, F32)
        u_ref[:, :half] = hi.astype(BF16)
        u_ref[:, half:] = lo.astype(BF16)

    @pl.when(live)
    def _():
        u = u_ref[...]
        h1 = _dot(u, w1_ref[0])
        h3 = _dot(u, w3_ref[0])
        acc_ref[...] += _dot((h1 * jax.nn.sigmoid(h1) * h3).astype(BF16), w2_ref[0])

    @pl.when(j == pl.num_programs(1) - 1)
    def _():
        o_ref[...] = acc_ref[...]


def _moe_experts(xb, blk_e, n_used, w1, w3, w2, layer, tm, tf):
    r = xb.shape[0]
    d = w1.shape[2]
    ff = w1.shape[3]
    nj = ff // tf

    def wmap(i, j, be, nu):
        return (layer, be[i], 0, jnp.where(i < nu[0], j, nj - 1))

    def w2map(i, j, be, nu):
        return (layer, be[i], jnp.where(i < nu[0], j, nj - 1), 0)

    grid_spec = pltpu.PrefetchScalarGridSpec(
        num_scalar_prefetch=2,
        grid=(r // tm, nj),
        in_specs=[pl.BlockSpec((tm, d // 2), lambda i, j, be, nu: (i, 0)),
                  pl.BlockSpec((None, 1, d, tf), wmap),
                  pl.BlockSpec((None, 1, d, tf), wmap),
                  pl.BlockSpec((None, 1, tf, d), w2map)],
        out_specs=pl.BlockSpec((tm, d), lambda i, j, be, nu: (i, 0)),
        scratch_shapes=[pltpu.VMEM((tm, d), F32), pltpu.VMEM((tm, d), BF16)],
    )
    return pl.pallas_call(
        _moe_kernel,
        grid_spec=grid_spec,
        out_shape=jax.ShapeDtypeStruct((r, d), F32),
        compiler_params=_cp(("arbitrary", "arbitrary")),
        name="moe_experts",
    )(blk_e, n_used, xb, w1, w3, w2)


def _ln_res_kernel(x_ref, f0_ref, f1_ref, wt_ref, mod_ref, g_ref, b_ref, o_ref, *, alpha):
    f = f0_ref[...] * wt_ref[:, 0:1] + f1_ref[...] * wt_ref[:, 1:2]
    o_ref[...] = _gated_residual_ln(x_ref[...], f, mod_ref, 5, alpha, g_ref[...], b_ref[...])


def _ln_res(x, f01, wts, modx, g, b, alpha, tm):
    m, d = x.shape
    nt = m // tm
    return pl.pallas_call(
        functools.partial(_ln_res_kernel, alpha=alpha),
        grid=(nt,),
        in_specs=[pl.BlockSpec((tm, d), lambda i: (i, 0)),
                  pl.BlockSpec((tm, d), lambda i: (i, 0)),
                  pl.BlockSpec((tm, d), lambda i: (i + nt, 0)),
                  pl.BlockSpec((tm, TOP_K), lambda i: (i, 0)),
                  pl.BlockSpec((tm // SUB, 8, d), lambda i: (i, 0, 0)),
                  pl.BlockSpec((1, d), lambda i: (0, 0)), pl.BlockSpec((1, d), lambda i: (0, 0))],
        out_specs=pl.BlockSpec((tm, d), lambda i: (i, 0)),
        out_shape=jax.ShapeDtypeStruct((m, d), F32),
        compiler_params=_cp(("parallel",)),
        name="moe_residual_ln",
    )(x, f01, f01, wts, modx, g.reshape(1, d), b.reshape(1, d))


def _moe(x, modx, router, w1, w3, w2, layer, g, b, alpha, tm_rows, tm, tf):
    m, d = x.shape
    wr_pad = jnp.zeros((d, LANE), F32).at[:, :N_EXPERTS].set(router)
    u, logits = _route(x, modx, wr_pad, tm_rows)
    top_v, top_i = lax.top_k(logits[:, :N_EXPERTS], TOP_K)
    wts = jax.nn.softmax(top_v, axis=-1)
    e_flat = top_i.reshape(-1).astype(jnp.int32)
    n_pair = m * TOP_K
    onehot = (e_flat[:, None] == jnp.arange(N_EXPERTS, dtype=jnp.int32)[None, :]).astype(jnp.int32)
    csum = jnp.cumsum(onehot, axis=0)
    counts = csum[-1]
    rank = jnp.take_along_axis(csum, e_flat[:, None], axis=1)[:, 0] - 1
    padded = (counts + tm - 1) // tm * tm
    pad_end = jnp.cumsum(padded)
    pad_start = pad_end - padded
    dest = pad_start[e_flat] + rank
    n_blk = -(-n_pair // tm) + N_EXPERTS
    r = n_blk * tm
    tok = jnp.arange(n_pair, dtype=jnp.int32) // TOP_K
    n_used = (pad_end[-1] // tm).astype(jnp.int32)
    blk = jnp.minimum(jnp.arange(n_blk, dtype=jnp.int32), n_used - 1) * tm
    blk_e = jnp.minimum(jnp.searchsorted(pad_end, blk, side='right'), N_EXPERTS - 1).astype(jnp.int32)
    _, tok_sorted = lax.sort_key_val(dest, tok)
    shift = pad_start - (jnp.cumsum(counts) - counts)
    src = jnp.arange(r, dtype=jnp.int32) - jnp.repeat(shift[blk_e], tm).astype(jnp.int32)
    row_tok = jnp.take(tok_sorted, jnp.clip(src, 0, n_pair - 1), mode="clip")
    xb = jnp.take(u, row_tok, axis=0, mode="clip")
    yb = _moe_experts(xb, blk_e, n_used.reshape(1), w1, w3, w2, layer, tm, tf)
    f01 = jnp.take(yb, dest.reshape(m, TOP_K).T.reshape(-1), axis=0, mode="clip")
    return _ln_res(x, f01, wts, modx, g, b, alpha, tm_rows)


def _rope_tables(nb, lat_len, ctx_len):
    rows = lat_len // GRID_W
    row = jnp.repeat(jnp.arange(rows, dtype=F32), GRID_W)
    col = jnp.tile(jnp.arange(GRID_W, dtype=F32), rows)
    n_freq = HEAD_DIM // 4
    inv = ROPE_THETA ** (-jnp.arange(n_freq, dtype=F32) / n_freq)
    ar, ac = row[:, None] * inv, col[:, None] * inv
    cos = jnp.concatenate([jnp.cos(ar), jnp.cos(ar), jnp.cos(ac), jnp.cos(ac)], axis=1)
    sin = jnp.concatenate([-jnp.sin(ar), jnp.sin(ar), -jnp.sin(ac), jnp.sin(ac)], axis=1)
    cos = jnp.concatenate([cos, jnp.ones((ctx_len, HEAD_DIM), F32)], axis=0)
    sin = jnp.concatenate([sin, jnp.zeros((ctx_len, HEAD_DIM), F32)], axis=0)
    return jnp.tile(cos, (nb, 1)), jnp.tile(sin, (nb, 1))


def _row_tile(m, pref):
    tm = pref
    while m % tm:
        tm //= 2
    return tm


def kernel(x, c, ctx, c_ctx, w_mod, b_mod, w_in, conv_a_w, conv_a_b, ret_decay, hy_conv_w, hy_conv_b, hy_w1, hy_b1, hy_w2, hy_b2, hy_w3, hy_freq, hy_skip, q_norm, k_norm, w_gate, b_gate, w_branch, w_o, ln_g, ln_b, ffn_w1, ffn_w3, ffn_w2, router, moe_w1, moe_w3, moe_w2):
    nb, lat_len, d = x.shape
    ctx_len = ctx.shape[1]
    depth = w_mod.shape[0]
    t = lat_len + ctx_len
    m = nb * t
    alpha = float((2 * depth) ** 0.25)
    assert lat_len % SUB == 0 and ctx_len % SUB == 0 and nb % 2 == 0

    act = jnp.concatenate([jax.nn.silu(c), jax.nn.silu(c_ctx)[None, :],
                           jnp.zeros((8 - nb - 1, d), F32)], axis=0)
    mods = _mod_all(act, w_mod, b_mod).reshape(depth, 8, N_MOD, d)
    mods = jnp.pad(mods, ((0, 0), (0, 0), (0, 8 - N_MOD), (0, 0)))
    tiles_per_batch = t // SUB
    tile_pos = np.arange(m // SUB) % tiles_per_batch
    tile_group = np.where(tile_pos < lat_len // SUB, np.arange(m // SUB) // tiles_per_batch, nb)
    cos_t, sin_t = _rope_tables(nb, lat_len, ctx_len)

    cast = lambda a: a.astype(BF16)
    w_in_b, w_gate_b, w_branch_b, w_o_b = cast(w_in), cast(w_gate), cast(w_branch), cast(w_o)
    ffn_b = (cast(ffn_w1), cast(ffn_w3), cast(ffn_w2))
    moe_w = (cast(moe_w1), cast(moe_w3), cast(moe_w2))

    filt_ws = [(hy_w1[l], hy_b1[l], hy_w2[l], hy_b2[l], hy_w3[l], hy_freq[l]) for l in range(depth)]
    hs_all = _hyena_filter_spectra_all(filt_ws, lat_len)

    rows = jnp.concatenate([x, ctx], axis=1).reshape(m, d)
    tm = _row_tile(m, 512)
    for l in range(depth):
        modx = mods[l][tile_group]
        za, zvg, zh, rq, rk, aq, ak, av = _proj(rows, modx, w_in_b, l, cos_t, sin_t, q_norm[l], k_norm[l], tm)
        y_d = _attention(aq, ak, av, nb, lat_len)
        log_gamma = -jnp.exp(ret_decay[l].astype(F32))
        y_r = _retention(rq, rk, zvg, _retention_tables(log_gamma), nb, lat_len)
        y_a = _shortconv(za, conv_a_w[l], conv_a_b[l], nb, lat_len)
        y_h = _hyena(zh, hy_conv_w[l], hy_conv_b[l], filt_ws[l], hy_skip[l], nb, lat_len, hs_all, 2 * l)
        rows = _merge(rows, modx, (y_a, y_r, y_h, y_d), w_gate_b, b_gate[l], w_branch_b, w_o_b, l,
                      ln_g[l, 0], ln_b[l, 0], alpha, tm)
        i = l // 2
        if l % 2 == 0:
            rows = _ffn(rows, modx, *ffn_b, i, ln_g[l, 1], ln_b[l, 1], alpha, tm, 256)
        else:
            rows = _moe(rows, modx, router[i], *moe_w, i, ln_g[l, 1], ln_b[l, 1], alpha, tm, 768, 1792)
    return rows.reshape(nb, t, d)[:, :lat_len]
```
